```python
import math, functools
import jax, jax.numpy as jnp
from jax import lax
import numpy as np

D_MODEL = 1024
BATCH = 32
SEQ = 2048
DEPTH = 4

CTX_LEN = 256
GRID_W = 64
N_MOD = 6
RMS_EPS = 1e-6
N_EVEN = (DEPTH + 1) // 2
N_ODD = DEPTH // 2

SSD_W = D_MODEL
SSD_HEAD_DIM = 64
SSD_HEADS = SSD_W // SSD_HEAD_DIM
SSD_GROUPS = 2
SSD_HPG = SSD_HEADS // SSD_GROUPS
SSD_STATE = 128
SSD_BC = SSD_GROUPS * SSD_STATE
SSD_XBC = SSD_W + 2 * SSD_BC
SSD_CONV = 4
SSD_CHUNK = 128
LRU_W = D_MODEL
LRU_BLOCKS = 8
LRU_BLOCK_W = LRU_W // LRU_BLOCKS
LRU_CONV = 4
LRU_C = 8.0
EV_SPLITS = (SSD_W, SSD_W + SSD_XBC, SSD_W + SSD_XBC + 2 * SSD_HEADS, SSD_W + SSD_XBC + 2 * SSD_HEADS + LRU_W)
EV_IN = SSD_W + SSD_XBC + 2 * SSD_HEADS + 2 * LRU_W
EV_MIX = SSD_W + LRU_W
HG_W = 3 * D_MODEL // 4
HG_DK = 128
HG_DV = 128
HG_HEADS = HG_W // HG_DK
HG_CHUNK = 32
S5_W = D_MODEL // 4
S5_GROUP_CH = 16
S5_GROUPS = S5_W // S5_GROUP_CH
S5_STATE = 64
OD_SPLITS = (HG_W, 2 * HG_W, 3 * HG_W, 4 * HG_W, 5 * HG_W)
OD_IN = 5 * HG_W + S5_W
OD_MIX = HG_W + S5_W
D_FF = 2816
FFN_CONV = 3

kernel_name = "hybrid_ssd_rglru_hgrn2_s5_diffusion_trunk"


def rms_norm(t, g):
    tf = t.astype(jnp.float32)
    tf = tf * lax.rsqrt(jnp.mean(jnp.square(tf), axis=-1, keepdims=True) + RMS_EPS)
    return (tf * g.astype(jnp.float32)).astype(t.dtype)


def modulate(t, shift, scale):
    return t * (1 + scale) + shift


def depthwise_conv1d(t, w, b):
    k = w.shape[0]
    y = lax.conv_general_dilated(t, w[:, None, :], (1,), [((k - 1) // 2, k // 2)],
                                 dimension_numbers=("NWC", "WIO", "NWC"),
                                 feature_group_count=t.shape[-1])
    return y + b


def depthwise_conv2d(t, w, b):
    y = lax.conv_general_dilated(t, w[:, :, None, :], (1, 1), [(1, 1), (1, 1)],
                                 dimension_numbers=("NHWC", "HWIO", "NHWC"),
                                 feature_group_count=t.shape[-1])
    return y + b


def to_chunks(t, size):
    b, n = t.shape[:2]
    return jnp.swapaxes(t.reshape((b, n // size, size) + t.shape[2:]), 0, 1)


def from_chunks(t):
    t = jnp.swapaxes(t, 0, 1)
    return t.reshape((t.shape[0], t.shape[1] * t.shape[2]) + t.shape[3:])


def run_direction(scan_fn, ctx_seqs, lat_seqs, h0, reverse):
    flip = (lambda t: jnp.flip(t, axis=1)) if reverse else (lambda t: t)
    y_c, h_c = scan_fn(*map(flip, ctx_seqs), h0)
    y_x, _ = scan_fn(*map(flip, lat_seqs), h_c)
    return flip(y_c), flip(y_x)


def ssd_scan(x, dt, bm, cm, h0, a):
    log_a = dt * a
    xdt = x * dt[..., None]
    mask = jnp.tril(jnp.ones((SSD_CHUNK, SSD_CHUNK), dtype=bool))

    def step(h, inp):
        xc, lac, bc, cc = inp
        cum = jnp.cumsum(lac, axis=1)
        seg = cum[:, :, None] - cum[:, None]
        decay = jnp.exp(jnp.where(mask[None, :, :, None, None], seg, -jnp.inf))
        cb = jnp.einsum("blgn,bsgn->blsg", cc, bc)
        y = jnp.einsum("blsgh,bsghp->blghp", cb[..., None] * decay, xc)
        y = y + jnp.einsum("blgn,bghpn->blghp", cc, h) * jnp.exp(cum)[..., None]
        to_end = jnp.exp(cum[:, -1:] - cum)
        h = h * jnp.exp(cum[:, -1])[..., None, None] + jnp.einsum(
            "bsgn,bsghp->bghpn", bc, xc * to_end[..., None])
        return h, y

    h, ys = lax.scan(step, h0, tuple(to_chunks(t, SSD_CHUNK) for t in (xdt, log_a, bm, cm)))
    return from_chunks(ys), h


def linear_recurrence(a, b, h0):
    b = b.at[:, 0].add(a[:, 0] * h0)

    def combine(l, r):
        return l[0] * r[0], r[0] * l[1] + r[1]

    return lax.associative_scan(combine, (a, b), axis=1)[1]


def rglru_scan(u, h0, w_a, b_a, w_i, b_i, lam):
    r = jax.nn.sigmoid(jnp.einsum("btnk,nkj->btnj", u, w_a) + b_a)
    i = jax.nn.sigmoid(jnp.einsum("btnk,nkj->btnj", u, w_i) + b_i)
    log_a = -LRU_C * jax.nn.softplus(-lam) * r
    a = jnp.exp(log_a)
    bx = jnp.sqrt(-jnp.expm1(2 * log_a)) * (i * u)
    h = linear_recurrence(a, bx, h0)
    return h, h[:, -1]


def hgrn_scan(q, k, log_f, v, s0):
    mask = jnp.tril(jnp.ones((HG_CHUNK, HG_CHUNK), dtype=bool))

    def step(s, inp):
        qc, kc, gc, vc = inp
        cum = jnp.cumsum(gc, axis=1)
        seg = cum[:, :, None] - cum[:, None]
        decay = jnp.exp(jnp.where(mask[None, :, :, None, None], seg, -jnp.inf))
        att = jnp.einsum("blhk,blshk->blsh", qc, decay * kc[:, None])
        y = jnp.einsum("blsh,bshv->blhv", att, vc)
        y = y + jnp.einsum("blhk,bhkv->blhv", qc * jnp.exp(cum), s)
        s = s * jnp.exp(cum[:, -1])[..., None] + jnp.einsum(
            "bshk,bshv->bhkv", kc * jnp.exp(cum[:, -1:] - cum), vc)
        return s, y

    s, ys = lax.scan(step, s0, tuple(to_chunks(t, HG_CHUNK) for t in (q, k, log_f, v)))
    return from_chunks(ys), s


def s5_scan(u, h0, lam_re, lam_im, log_step, b_re, b_im, c_re, c_im):
    step = jnp.exp(log_step)[:, None]
    mag = jnp.exp(lam_re * step)
    ar, ai = mag * jnp.cos(lam_im * step), mag * jnp.sin(lam_im * step)
    den = lam_re * lam_re + lam_im * lam_im
    zr = ((ar - 1) * lam_re + ai * lam_im) / den
    zi = (ai * lam_re - (ar - 1) * lam_im) / den
    bbr = zr[..., None] * b_re - zi[..., None] * b_im
    bbi = zr[..., None] * b_im + zi[..., None] * b_re
    xr = jnp.einsum("btgk,gpk->btgp", u, bbr)
    xi = jnp.einsum("btgk,gpk->btgp", u, bbi)
    h0r, h0i = h0
    xr = xr.at[:, 0].add(ar * h0r - ai * h0i)
    xi = xi.at[:, 0].add(ar * h0i + ai * h0r)
    a_r = jnp.broadcast_to(ar, xr.shape)
    a_i = jnp.broadcast_to(ai, xi.shape)

    def combine(l, r):
        lar, lai, lbr, lbi = l
        rar, rai, rbr, rbi = r
        return (lar * rar - lai * rai, lar * rai + lai * rar,
                rar * lbr - rai * lbi + rbr, rar * lbi + rai * lbr + rbi)

    _, _, hr, hi = lax.associative_scan(combine, (a_r, a_i, xr, xi), axis=1)
    y = jnp.einsum("btgp,gkp->btgk", hr, c_re) - jnp.einsum("btgp,gkp->btgk", hi, c_im)
    return y, (hr[:, -1], hi[:, -1])


def even_mixer(hx, hc, w_in, w_out, ssd_conv_w, ssd_conv_b, ssd_dt_bias, ssd_a_log, ssd_d, ssd_norm_g,
               lru_conv_w, lru_conv_b, lru_w_a, lru_b_a, lru_w_i, lru_b_i, lru_lam, need_ctx):
    ssd_a = -jnp.exp(ssd_a_log).reshape(2, SSD_GROUPS, SSD_HPG)

    def prepare(h):
        bsz, t = h.shape[:2]
        z, xbc, dt, gy, u = jnp.split(h @ w_in, EV_SPLITS, axis=-1)
        xbc = jax.nn.silu(depthwise_conv1d(xbc, ssd_conv_w, ssd_conv_b))
        xs, bm, cm = jnp.split(xbc, [SSD_W, SSD_W + SSD_BC], axis=-1)
        dt = jax.nn.softplus(dt.reshape(bsz, t, 2, SSD_HEADS) + ssd_dt_bias)
        u = depthwise_conv1d(u, lru_conv_w, lru_conv_b)
        return {"z": z, "gy": gy,
                "x": xs.reshape(bsz, t, SSD_GROUPS, SSD_HPG, SSD_HEAD_DIM),
                "b": bm.reshape(bsz, t, SSD_GROUPS, SSD_STATE),
                "c": cm.reshape(bsz, t, SSD_GROUPS, SSD_STATE),
                "dt": dt.reshape(bsz, t, 2, SSD_GROUPS, SSD_HPG),
                "u": u.reshape(bsz, t, LRU_BLOCKS, LRU_BLOCK_W)}

    pc, px = prepare(hc), prepare(hx)
    bsz = hx.shape[0]
    ssd_h0 = jnp.zeros((bsz, SSD_GROUPS, SSD_HPG, SSD_HEAD_DIM, SSD_STATE), hx.dtype)
    lru_h0 = jnp.zeros((bsz, LRU_BLOCKS, LRU_BLOCK_W), hx.dtype)
    ssd_dirs, lru_dirs = [], []
    for d, reverse in enumerate((False, True)):
        ssd_dirs.append(run_direction(
            functools.partial(ssd_scan, a=ssd_a[d]),
            (pc["x"], pc["dt"][:, :, d], pc["b"], pc["c"]),
            (px["x"], px["dt"][:, :, d], px["b"], px["c"]), ssd_h0, reverse))
        lru_dirs.append(run_direction(
            functools.partial(rglru_scan, w_a=lru_w_a[d], b_a=lru_b_a[d].reshape(LRU_BLOCKS, LRU_BLOCK_W),
                              w_i=lru_w_i[d], b_i=lru_b_i[d].reshape(LRU_BLOCKS, LRU_BLOCK_W),
                              lam=lru_lam[d].reshape(LRU_BLOCKS, LRU_BLOCK_W)),
            (pc["u"],), (px["u"],), lru_h0, reverse))

    def finish(p, ssd_y, lru_h):
        bsz_, t = p["z"].shape[:2]
        y = (ssd_y + ssd_d.reshape(SSD_GROUPS, SSD_HPG, 1) * p["x"]).reshape(bsz_, t, SSD_W)
        y = rms_norm(y * jax.nn.silu(p["z"]), ssd_norm_g)
        r = lru_h.reshape(bsz_, t, LRU_W) * jax.nn.gelu(p["gy"])
        return jnp.concatenate([y, r], axis=-1) @ w_out

    out_x = finish(px, ssd_dirs[0][1] + ssd_dirs[1][1], lru_dirs[0][1] + lru_dirs[1][1])
    out_c = finish(pc, ssd_dirs[0][0] + ssd_dirs[1][0], lru_dirs[0][0] + lru_dirs[1][0]) if need_ctx else None
    return out_x, out_c


def odd_mixer(hx, hc, lower_bound, w_in, w_out, hg_norm_g, s5_lam_re, s5_lam_im, s5_log_step,
              s5_b_re, s5_b_im, s5_c_re, s5_c_im, s5_d, s5_glu_w, s5_glu_b, need_ctx):
    lb = lower_bound.reshape(HG_HEADS, HG_DK)

    def prepare(h):
        bsz, t = h.shape[:2]
        q, f_fwd, f_bwd, v, g, u = jnp.split(h @ w_in, OD_SPLITS, axis=-1)
        gates = []
        for f in (f_fwd, f_bwd):
            f = f.reshape(bsz, t, HG_HEADS, HG_DK)
            log_f = jnp.log(lb + (1 - lb) * jax.nn.sigmoid(f))
            k = (1 - lb) * jax.nn.sigmoid(-f)
            gates.append((k, log_f))
        return {"q": jax.nn.silu(q.reshape(bsz, t, HG_HEADS, HG_DK)),
                "v": v.reshape(bsz, t, HG_HEADS, HG_DV), "g": g, "gates": gates,
                "u": u.reshape(bsz, t, S5_GROUPS, S5_GROUP_CH)}

    pc, px = prepare(hc), prepare(hx)
    bsz = hx.shape[0]
    hg_h0 = jnp.zeros((bsz, HG_HEADS, HG_DK, HG_DV), hx.dtype)
    s5_zero = jnp.zeros((bsz, S5_GROUPS, S5_STATE), hx.dtype)
    hg_dirs, s5_dirs = [], []
    for d, reverse in enumerate((False, True)):
        hg_dirs.append(run_direction(
            hgrn_scan, (pc["q"], pc["gates"][d][0], pc["gates"][d][1], pc["v"]),
            (px["q"], px["gates"][d][0], px["gates"][d][1], px["v"]), hg_h0, reverse))
        s5_dirs.append(run_direction(
            functools.partial(s5_scan, lam_re=s5_lam_re[d], lam_im=s5_lam_im[d], log_step=s5_log_step[d],
                              b_re=s5_b_re, b_im=s5_b_im, c_re=s5_c_re[d], c_im=s5_c_im[d]),
            (pc["u"],), (px["u"],), (s5_zero, s5_zero), reverse))

    def finish(p, o, y):
        bsz_, t = p["g"].shape[:2]
        o = rms_norm(o, hg_norm_g) * jax.nn.silu(p["g"].reshape(bsz_, t, HG_HEADS, HG_DV))
        y = y + s5_d.reshape(S5_GROUPS, S5_GROUP_CH) * p["u"]
        y = jax.nn.gelu(y.reshape(bsz_, t, S5_W))
        y = y * jax.nn.sigmoid(y @ s5_glu_w + s5_glu_b)
        return jnp.concatenate([o.reshape(bsz_, t, HG_W), y], axis=-1) @ w_out

    out_x = finish(px, hg_dirs[0][1] + hg_dirs[1][1], s5_dirs[0][1] + s5_dirs[1][1])
    out_c = finish(pc, hg_dirs[0][0] + hg_dirs[1][0], s5_dirs[0][0] + s5_dirs[1][0]) if need_ctx else None
    return out_x, out_c


def conv_ffn(h, rows, w_gate, w_up, conv_w, conv_b, w_down):
    a = h @ w_gate
    if rows is None:
        a = depthwise_conv1d(a, conv_w[1], conv_b)
    else:
        bsz, t, f = a.shape
        a = depthwise_conv2d(a.reshape(bsz, rows, GRID_W, f), conv_w, conv_b).reshape(bsz, t, f)
    return (jax.nn.silu(a) * (h @ w_up)) @ w_down


def _fwd_setup_inputs(seed: int = 0) -> dict:
    key = jax.random.key(seed)
    ks = iter(jax.random.split(key, 64))
    D = D_MODEL

    def nrm(shape, scale):
        return scale * jax.random.normal(next(ks), shape, jnp.float32)

    def uni(shape, lo, hi):
        return jax.random.uniform(next(ks), shape, jnp.float32, lo, hi)

    ssd_dt = jnp.exp(uni((N_EVEN, 2, SSD_HEADS), math.log(1e-3), math.log(1e-1)))
    lru_a = uni((N_EVEN, 2, LRU_W), 0.9, 0.999) ** (1.0 / LRU_C)
    n_idx = jnp.arange(S5_STATE, dtype=jnp.float32)
    return {
        "x": nrm((BATCH, SEQ, D), 1.0),
        "c": nrm((BATCH, D), 1.0),
        "ctx": nrm((BATCH, CTX_LEN, D), 1.0),
        "c_ctx": nrm((D,), 1.0),
        "w_mod": nrm((DEPTH, D, N_MOD * D), 0.5 * D ** -0.5),
        "b_mod": nrm((DEPTH, N_MOD * D), 0.02),
        "norm_mix_g": 1.0 + nrm((DEPTH, D), 0.02),
        "norm_ffn_g": 1.0 + nrm((DEPTH, D), 0.02),
        "final_norm_g": 1.0 + nrm((D,), 0.02),
        "ev_w_in": nrm((N_EVEN, D, EV_IN), D ** -0.5),
        "ev_w_out": nrm((N_EVEN, EV_MIX, D), EV_MIX ** -0.5),
        "ssd_conv_w": nrm((N_EVEN, SSD_CONV, SSD_XBC), SSD_CONV ** -0.5),
        "ssd_conv_b": nrm((N_EVEN, SSD_XBC), 0.02),
        "ssd_dt_bias": ssd_dt + jnp.log(-jnp.expm1(-ssd_dt)),
        "ssd_a_log": jnp.log(uni((N_EVEN, 2, SSD_HEADS), 1.0, 16.0)),
        "ssd_d": 1.0 + nrm((N_EVEN, SSD_HEADS), 0.1),
        "ssd_norm_g": 1.0 + nrm((N_EVEN, SSD_W), 0.02),
        "lru_conv_w": nrm((N_EVEN, LRU_CONV, LRU_W), LRU_CONV ** -0.5),
        "lru_conv_b": nrm((N_EVEN, LRU_W), 0.02),
        "lru_w_a": nrm((N_EVEN, 2, LRU_BLOCKS, LRU_BLOCK_W, LRU_BLOCK_W), LRU_BLOCK_W ** -0.5),
        "lru_b_a": nrm((N_EVEN, 2, LRU_W), 0.02),
        "lru_w_i": nrm((N_EVEN, 2, LRU_BLOCKS, LRU_BLOCK_W, LRU_BLOCK_W), LRU_BLOCK_W ** -0.5),
        "lru_b_i": nrm((N_EVEN, 2, LRU_W), 0.02),
        "lru_lam": jnp.log(lru_a) - jnp.log1p(-lru_a),
        "od_w_in": nrm((N_ODD, D, OD_IN), D ** -0.5),
        "od_w_out": nrm((N_ODD, OD_MIX, D), OD_MIX ** -0.5),
        "hg_lb_logits": nrm((DEPTH, HG_W), 0.1),
        "hg_norm_g": 1.0 + nrm((N_ODD, HG_HEADS, HG_DV), 0.02),
        "s5_lam_re": -0.5 + nrm((N_ODD, 2, S5_GROUPS, S5_STATE), 0.01),
        "s5_lam_im": math.pi * n_idx + nrm((N_ODD, 2, S5_GROUPS, S5_STATE), 0.01),
        "s5_log_step": uni((N_ODD, 2, S5_GROUPS), math.log(1e-3), math.log(1e-1)),
        "s5_b_re": nrm((N_ODD, S5_GROUPS, S5_STATE, S5_GROUP_CH), (2 * S5_GROUP_CH) ** -0.5),
        "s5_b_im": nrm((N_ODD, S5_GROUPS, S5_STATE, S5_GROUP_CH), (2 * S5_GROUP_CH) ** -0.5),
        "s5_c_re": nrm((N_ODD, 2, S5_GROUPS, S5_GROUP_CH, S5_STATE), 0.5),
        "s5_c_im": nrm((N_ODD, 2, S5_GROUPS, S5_GROUP_CH, S5_STATE), 0.5),
        "s5_d": nrm((N_ODD, S5_W), 1.0),
        "s5_glu_w": nrm((N_ODD, S5_W, S5_W), S5_W ** -0.5),
        "s5_glu_b": nrm((N_ODD, S5_W), 0.02),
        "ffn_w_gate": nrm((DEPTH, D, D_FF), D ** -0.5),
        "ffn_w_up": nrm((DEPTH, D, D_FF), D ** -0.5),
        "ffn_conv_w": nrm((DEPTH, FFN_CONV, FFN_CONV, D_FF), 1.0 / FFN_CONV),
        "ffn_conv_b": nrm((DEPTH, D_FF), 0.02),
        "ffn_w_down": nrm((DEPTH, D_FF, D), D_FF ** -0.5),
    }


def _fwd_reference(x, c, ctx, c_ctx, w_mod, b_mod, norm_mix_g, norm_ffn_g, final_norm_g,
              ev_w_in, ev_w_out, ssd_conv_w, ssd_conv_b, ssd_dt_bias, ssd_a_log, ssd_d, ssd_norm_g,
              lru_conv_w, lru_conv_b, lru_w_a, lru_b_a, lru_w_i, lru_b_i, lru_lam,
              od_w_in, od_w_out, hg_lb_logits, hg_norm_g,
              s5_lam_re, s5_lam_im, s5_log_step, s5_b_re, s5_b_im, s5_c_re, s5_c_im, s5_d,
              s5_glu_w, s5_glu_b,
              ffn_w_gate, ffn_w_up, ffn_conv_w, ffn_conv_b, ffn_w_down):
    rows = x.shape[1] // GRID_W
    p = jax.nn.softmax(hg_lb_logits.astype(jnp.float32), axis=0)
    lower_bounds = (jnp.cumsum(p, axis=0) - p[0]).astype(hg_lb_logits.dtype)
    s_c = jax.nn.silu(c)
    s_cc = jax.nn.silu(c_ctx)
    for layer in range(DEPTH):
        last = layer == DEPTH - 1
        j = layer // 2
        mod_x = jnp.split((s_c @ w_mod[layer] + b_mod[layer])[:, None, :], N_MOD, axis=-1)
        mod_c = jnp.split(s_cc @ w_mod[layer] + b_mod[layer], N_MOD, axis=-1)
        hx = modulate(rms_norm(x, norm_mix_g[layer]), mod_x[0], mod_x[1])
        hc = modulate(rms_norm(ctx, norm_mix_g[layer]), mod_c[0], mod_c[1])
        if layer % 2 == 0:
            ox, oc = even_mixer(hx, hc, ev_w_in[j], ev_w_out[j], ssd_conv_w[j], ssd_conv_b[j],
                                ssd_dt_bias[j], ssd_a_log[j], ssd_d[j], ssd_norm_g[j],
                                lru_conv_w[j], lru_conv_b[j], lru_w_a[j], lru_b_a[j],
                                lru_w_i[j], lru_b_i[j], lru_lam[j], not last)
        else:
            ox, oc = odd_mixer(hx, hc, lower_bounds[layer], od_w_in[j], od_w_out[j], hg_norm_g[j],
                               s5_lam_re[j], s5_lam_im[j], s5_log_step[j], s5_b_re[j], s5_b_im[j],
                               s5_c_re[j], s5_c_im[j], s5_d[j], s5_glu_w[j], s5_glu_b[j], not last)
        x = x + mod_x[2] * ox
        fx = modulate(rms_norm(x, norm_ffn_g[layer]), mod_x[3], mod_x[4])
        x = x + mod_x[5] * conv_ffn(fx, rows, ffn_w_gate[layer], ffn_w_up[layer],
                                    ffn_conv_w[layer], ffn_conv_b[layer], ffn_w_down[layer])
        if not last:
            ctx = ctx + mod_c[2] * oc
            fc = modulate(rms_norm(ctx, norm_ffn_g[layer]), mod_c[3], mod_c[4])
            ctx = ctx + mod_c[5] * conv_ffn(fc, None, ffn_w_gate[layer], ffn_w_up[layer],
                                            ffn_conv_w[layer], ffn_conv_b[layer], ffn_w_down[layer])
    return rms_norm(x, final_norm_g)


import jax as _jax
import jax.numpy as _jnp

TWIN_FORMAT = 'train_step'
FWD_PARAMS = ['x', 'c', 'ctx', 'c_ctx', 'w_mod', 'b_mod', 'norm_mix_g', 'norm_ffn_g', 'final_norm_g', 'ev_w_in', 'ev_w_out', 'ssd_conv_w', 'ssd_conv_b', 'ssd_dt_bias', 'ssd_a_log', 'ssd_d', 'ssd_norm_g', 'lru_conv_w', 'lru_conv_b', 'lru_w_a', 'lru_b_a', 'lru_w_i', 'lru_b_i', 'lru_lam', 'od_w_in', 'od_w_out', 'hg_lb_logits', 'hg_norm_g', 's5_lam_re', 's5_lam_im', 's5_log_step', 's5_b_re', 's5_b_im', 's5_c_re', 's5_c_im', 's5_d', 's5_glu_w', 's5_glu_b', 'ffn_w_gate', 'ffn_w_up', 'ffn_conv_w', 'ffn_conv_b', 'ffn_w_down']
TWIN_WEIGHTS = ['c_ctx', 'w_mod', 'b_mod', 'norm_mix_g', 'norm_ffn_g', 'final_norm_g', 'ev_w_in', 'ev_w_out', 'ssd_conv_w', 'ssd_conv_b', 'ssd_dt_bias', 'ssd_a_log', 'ssd_d', 'ssd_norm_g', 'lru_conv_w', 'lru_conv_b', 'lru_w_a', 'lru_b_a', 'lru_w_i', 'lru_b_i', 'lru_lam', 'od_w_in', 'od_w_out', 'hg_lb_logits', 'hg_norm_g', 's5_lam_re', 's5_lam_im', 's5_log_step', 's5_b_re', 's5_b_im', 's5_c_re', 's5_c_im', 's5_d', 's5_glu_w', 's5_glu_b', 'ffn_w_gate', 'ffn_w_up', 'ffn_conv_w', 'ffn_conv_b', 'ffn_w_down']
TWIN_DIFF_INPUT = 'x'
TWIN_INPUTS = ['x', 'c', 'ctx', 'c_ctx', 'w_mod', 'b_mod', 'norm_mix_g', 'norm_ffn_g', 'final_norm_g', 'ev_w_in', 'ev_w_out', 'ssd_conv_w', 'ssd_conv_b', 'ssd_dt_bias', 'ssd_a_log', 'ssd_d', 'ssd_norm_g', 'lru_conv_w', 'lru_conv_b', 'lru_w_a', 'lru_b_a', 'lru_w_i', 'lru_b_i', 'lru_lam', 'od_w_in', 'od_w_out', 'hg_lb_logits', 'hg_norm_g', 's5_lam_re', 's5_lam_im', 's5_log_step', 's5_b_re', 's5_b_im', 's5_c_re', 's5_c_im', 's5_d', 's5_glu_w', 's5_glu_b', 'ffn_w_gate', 'ffn_w_up', 'ffn_conv_w', 'ffn_conv_b', 'ffn_w_down', 'loss_target', 'm_c_ctx', 'm_w_mod', 'm_b_mod', 'm_norm_mix_g', 'm_norm_ffn_g', 'm_final_norm_g', 'm_ev_w_in', 'm_ev_w_out', 'm_ssd_conv_w', 'm_ssd_conv_b', 'm_ssd_dt_bias', 'm_ssd_a_log', 'm_ssd_d', 'm_ssd_norm_g', 'm_lru_conv_w', 'm_lru_conv_b', 'm_lru_w_a', 'm_lru_b_a', 'm_lru_w_i', 'm_lru_b_i', 'm_lru_lam', 'm_od_w_in', 'm_od_w_out', 'm_hg_lb_logits', 'm_hg_norm_g', 'm_s5_lam_re', 'm_s5_lam_im', 'm_s5_log_step', 'm_s5_b_re', 'm_s5_b_im', 'm_s5_c_re', 'm_s5_c_im', 'm_s5_d', 'm_s5_glu_w', 'm_s5_glu_b', 'm_ffn_w_gate', 'm_ffn_w_up', 'm_ffn_conv_w', 'm_ffn_conv_b', 'm_ffn_w_down', 'v_c_ctx', 'v_w_mod', 'v_b_mod', 'v_norm_mix_g', 'v_norm_ffn_g', 'v_final_norm_g', 'v_ev_w_in', 'v_ev_w_out', 'v_ssd_conv_w', 'v_ssd_conv_b', 'v_ssd_dt_bias', 'v_ssd_a_log', 'v_ssd_d', 'v_ssd_norm_g', 'v_lru_conv_w', 'v_lru_conv_b', 'v_lru_w_a', 'v_lru_b_a', 'v_lru_w_i', 'v_lru_b_i', 'v_lru_lam', 'v_od_w_in', 'v_od_w_out', 'v_hg_lb_logits', 'v_hg_norm_g', 'v_s5_lam_re', 'v_s5_lam_im', 'v_s5_log_step', 'v_s5_b_re', 'v_s5_b_im', 'v_s5_c_re', 'v_s5_c_im', 'v_s5_d', 'v_s5_glu_w', 'v_s5_glu_b', 'v_ffn_w_gate', 'v_ffn_w_up', 'v_ffn_conv_w', 'v_ffn_conv_b', 'v_ffn_w_down']
TWIN_OUTPUTS = ['loss', 'grad_x', 'grad_c_ctx', 'grad_w_mod', 'grad_b_mod', 'grad_norm_mix_g', 'grad_norm_ffn_g', 'grad_final_norm_g', 'grad_ev_w_in', 'grad_ev_w_out', 'grad_ssd_conv_w', 'grad_ssd_conv_b', 'grad_ssd_dt_bias', 'grad_ssd_a_log', 'grad_ssd_d', 'grad_ssd_norm_g', 'grad_lru_conv_w', 'grad_lru_conv_b', 'grad_lru_w_a', 'grad_lru_b_a', 'grad_lru_w_i', 'grad_lru_b_i', 'grad_lru_lam', 'grad_od_w_in', 'grad_od_w_out', 'grad_hg_lb_logits', 'grad_hg_norm_g', 'grad_s5_lam_re', 'grad_s5_lam_im', 'grad_s5_log_step', 'grad_s5_b_re', 'grad_s5_b_im', 'grad_s5_c_re', 'grad_s5_c_im', 'grad_s5_d', 'grad_s5_glu_w', 'grad_s5_glu_b', 'grad_ffn_w_gate', 'grad_ffn_w_up', 'grad_ffn_conv_w', 'grad_ffn_conv_b', 'grad_ffn_w_down', 'delta_c_ctx', 'delta_w_mod', 'delta_b_mod', 'delta_norm_mix_g', 'delta_norm_ffn_g', 'delta_final_norm_g', 'delta_ev_w_in', 'delta_ev_w_out', 'delta_ssd_conv_w', 'delta_ssd_conv_b', 'delta_ssd_dt_bias', 'delta_ssd_a_log', 'delta_ssd_d', 'delta_ssd_norm_g', 'delta_lru_conv_w', 'delta_lru_conv_b', 'delta_lru_w_a', 'delta_lru_b_a', 'delta_lru_w_i', 'delta_lru_b_i', 'delta_lru_lam', 'delta_od_w_in', 'delta_od_w_out', 'delta_hg_lb_logits', 'delta_hg_norm_g', 'delta_s5_lam_re', 'delta_s5_lam_im', 'delta_s5_log_step', 'delta_s5_b_re', 'delta_s5_b_im', 'delta_s5_c_re', 'delta_s5_c_im', 'delta_s5_d', 'delta_s5_glu_w', 'delta_s5_glu_b', 'delta_ffn_w_gate', 'delta_ffn_w_up', 'delta_ffn_conv_w', 'delta_ffn_conv_b', 'delta_ffn_w_down', 'new_m_c_ctx', 'new_m_w_mod', 'new_m_b_mod', 'new_m_norm_mix_g', 'new_m_norm_ffn_g', 'new_m_final_norm_g', 'new_m_ev_w_in', 'new_m_ev_w_out', 'new_m_ssd_conv_w', 'new_m_ssd_conv_b', 'new_m_ssd_dt_bias', 'new_m_ssd_a_log', 'new_m_ssd_d', 'new_m_ssd_norm_g', 'new_m_lru_conv_w', 'new_m_lru_conv_b', 'new_m_lru_w_a', 'new_m_lru_b_a', 'new_m_lru_w_i', 'new_m_lru_b_i', 'new_m_lru_lam', 'new_m_od_w_in', 'new_m_od_w_out', 'new_m_hg_lb_logits', 'new_m_hg_norm_g', 'new_m_s5_lam_re', 'new_m_s5_lam_im', 'new_m_s5_log_step', 'new_m_s5_b_re', 'new_m_s5_b_im', 'new_m_s5_c_re', 'new_m_s5_c_im', 'new_m_s5_d', 'new_m_s5_glu_w', 'new_m_s5_glu_b', 'new_m_ffn_w_gate', 'new_m_ffn_w_up', 'new_m_ffn_conv_w', 'new_m_ffn_conv_b', 'new_m_ffn_w_down', 'new_v_c_ctx', 'new_v_w_mod', 'new_v_b_mod', 'new_v_norm_mix_g', 'new_v_norm_ffn_g', 'new_v_final_norm_g', 'new_v_ev_w_in', 'new_v_ev_w_out', 'new_v_ssd_conv_w', 'new_v_ssd_conv_b', 'new_v_ssd_dt_bias', 'new_v_ssd_a_log', 'new_v_ssd_d', 'new_v_ssd_norm_g', 'new_v_lru_conv_w', 'new_v_lru_conv_b', 'new_v_lru_w_a', 'new_v_lru_b_a', 'new_v_lru_w_i', 'new_v_lru_b_i', 'new_v_lru_lam', 'new_v_od_w_in', 'new_v_od_w_out', 'new_v_hg_lb_logits', 'new_v_hg_norm_g', 'new_v_s5_lam_re', 'new_v_s5_lam_im', 'new_v_s5_log_step', 'new_v_s5_b_re', 'new_v_s5_b_im', 'new_v_s5_c_re', 'new_v_s5_c_im', 'new_v_s5_d', 'new_v_s5_glu_w', 'new_v_s5_glu_b', 'new_v_ffn_w_gate', 'new_v_ffn_w_up', 'new_v_ffn_conv_w', 'new_v_ffn_conv_b', 'new_v_ffn_w_down']
TWIN_LEAF_KINDS = {'loss': 'loss', 'grad_x': 'grad_x', 'grad_c_ctx': 'grad_w', 'grad_w_mod': 'grad_w', 'grad_b_mod': 'grad_w', 'grad_norm_mix_g': 'grad_w', 'grad_norm_ffn_g': 'grad_w', 'grad_final_norm_g': 'grad_w', 'grad_ev_w_in': 'grad_w', 'grad_ev_w_out': 'grad_w', 'grad_ssd_conv_w': 'grad_w', 'grad_ssd_conv_b': 'grad_w', 'grad_ssd_dt_bias': 'grad_w', 'grad_ssd_a_log': 'grad_w', 'grad_ssd_d': 'grad_w', 'grad_ssd_norm_g': 'grad_w', 'grad_lru_conv_w': 'grad_w', 'grad_lru_conv_b': 'grad_w', 'grad_lru_w_a': 'grad_w', 'grad_lru_b_a': 'grad_w', 'grad_lru_w_i': 'grad_w', 'grad_lru_b_i': 'grad_w', 'grad_lru_lam': 'grad_w', 'grad_od_w_in': 'grad_w', 'grad_od_w_out': 'grad_w', 'grad_hg_lb_logits': 'grad_w', 'grad_hg_norm_g': 'grad_w', 'grad_s5_lam_re': 'grad_w', 'grad_s5_lam_im': 'grad_w', 'grad_s5_log_step': 'grad_w', 'grad_s5_b_re': 'grad_w', 'grad_s5_b_im': 'grad_w', 'grad_s5_c_re': 'grad_w', 'grad_s5_c_im': 'grad_w', 'grad_s5_d': 'grad_w', 'grad_s5_glu_w': 'grad_w', 'grad_s5_glu_b': 'grad_w', 'grad_ffn_w_gate': 'grad_w', 'grad_ffn_w_up': 'grad_w', 'grad_ffn_conv_w': 'grad_w', 'grad_ffn_conv_b': 'grad_w', 'grad_ffn_w_down': 'grad_w', 'delta_c_ctx': 'delta_w', 'delta_w_mod': 'delta_w', 'delta_b_mod': 'delta_w', 'delta_norm_mix_g': 'delta_w', 'delta_norm_ffn_g': 'delta_w', 'delta_final_norm_g': 'delta_w', 'delta_ev_w_in': 'delta_w', 'delta_ev_w_out': 'delta_w', 'delta_ssd_conv_w': 'delta_w', 'delta_ssd_conv_b': 'delta_w', 'delta_ssd_dt_bias': 'delta_w', 'delta_ssd_a_log': 'delta_w', 'delta_ssd_d': 'delta_w', 'delta_ssd_norm_g': 'delta_w', 'delta_lru_conv_w': 'delta_w', 'delta_lru_conv_b': 'delta_w', 'delta_lru_w_a': 'delta_w', 'delta_lru_b_a': 'delta_w', 'delta_lru_w_i': 'delta_w', 'delta_lru_b_i': 'delta_w', 'delta_lru_lam': 'delta_w', 'delta_od_w_in': 'delta_w', 'delta_od_w_out': 'delta_w', 'delta_hg_lb_logits': 'delta_w', 'delta_hg_norm_g': 'delta_w', 'delta_s5_lam_re': 'delta_w', 'delta_s5_lam_im': 'delta_w', 'delta_s5_log_step': 'delta_w', 'delta_s5_b_re': 'delta_w', 'delta_s5_b_im': 'delta_w', 'delta_s5_c_re': 'delta_w', 'delta_s5_c_im': 'delta_w', 'delta_s5_d': 'delta_w', 'delta_s5_glu_w': 'delta_w', 'delta_s5_glu_b': 'delta_w', 'delta_ffn_w_gate': 'delta_w', 'delta_ffn_w_up': 'delta_w', 'delta_ffn_conv_w': 'delta_w', 'delta_ffn_conv_b': 'delta_w', 'delta_ffn_w_down': 'delta_w', 'new_m_c_ctx': 'new_m', 'new_m_w_mod': 'new_m', 'new_m_b_mod': 'new_m', 'new_m_norm_mix_g': 'new_m', 'new_m_norm_ffn_g': 'new_m', 'new_m_final_norm_g': 'new_m', 'new_m_ev_w_in': 'new_m', 'new_m_ev_w_out': 'new_m', 'new_m_ssd_conv_w': 'new_m', 'new_m_ssd_conv_b': 'new_m', 'new_m_ssd_dt_bias': 'new_m', 'new_m_ssd_a_log': 'new_m', 'new_m_ssd_d': 'new_m', 'new_m_ssd_norm_g': 'new_m', 'new_m_lru_conv_w': 'new_m', 'new_m_lru_conv_b': 'new_m', 'new_m_lru_w_a': 'new_m', 'new_m_lru_b_a': 'new_m', 'new_m_lru_w_i': 'new_m', 'new_m_lru_b_i': 'new_m', 'new_m_lru_lam': 'new_m', 'new_m_od_w_in': 'new_m', 'new_m_od_w_out': 'new_m', 'new_m_hg_lb_logits': 'new_m', 'new_m_hg_norm_g': 'new_m', 'new_m_s5_lam_re': 'new_m', 'new_m_s5_lam_im': 'new_m', 'new_m_s5_log_step': 'new_m', 'new_m_s5_b_re': 'new_m', 'new_m_s5_b_im': 'new_m', 'new_m_s5_c_re': 'new_m', 'new_m_s5_c_im': 'new_m', 'new_m_s5_d': 'new_m', 'new_m_s5_glu_w': 'new_m', 'new_m_s5_glu_b': 'new_m', 'new_m_ffn_w_gate': 'new_m', 'new_m_ffn_w_up': 'new_m', 'new_m_ffn_conv_w': 'new_m', 'new_m_ffn_conv_b': 'new_m', 'new_m_ffn_w_down': 'new_m', 'new_v_c_ctx': 'new_v', 'new_v_w_mod': 'new_v', 'new_v_b_mod': 'new_v', 'new_v_norm_mix_g': 'new_v', 'new_v_norm_ffn_g': 'new_v', 'new_v_final_norm_g': 'new_v', 'new_v_ev_w_in': 'new_v', 'new_v_ev_w_out': 'new_v', 'new_v_ssd_conv_w': 'new_v', 'new_v_ssd_conv_b': 'new_v', 'new_v_ssd_dt_bias': 'new_v', 'new_v_ssd_a_log': 'new_v', 'new_v_ssd_d': 'new_v', 'new_v_ssd_norm_g': 'new_v', 'new_v_lru_conv_w': 'new_v', 'new_v_lru_conv_b': 'new_v', 'new_v_lru_w_a': 'new_v', 'new_v_lru_b_a': 'new_v', 'new_v_lru_w_i': 'new_v', 'new_v_lru_b_i': 'new_v', 'new_v_lru_lam': 'new_v', 'new_v_od_w_in': 'new_v', 'new_v_od_w_out': 'new_v', 'new_v_hg_lb_logits': 'new_v', 'new_v_hg_norm_g': 'new_v', 'new_v_s5_lam_re': 'new_v', 'new_v_s5_lam_im': 'new_v', 'new_v_s5_log_step': 'new_v', 'new_v_s5_b_re': 'new_v', 'new_v_s5_b_im': 'new_v', 'new_v_s5_c_re': 'new_v', 'new_v_s5_c_im': 'new_v', 'new_v_s5_d': 'new_v', 'new_v_s5_glu_w': 'new_v', 'new_v_s5_glu_b': 'new_v', 'new_v_ffn_w_gate': 'new_v', 'new_v_ffn_w_up': 'new_v', 'new_v_ffn_conv_w': 'new_v', 'new_v_ffn_conv_b': 'new_v', 'new_v_ffn_w_down': 'new_v'}


def _forward(args):
    return _fwd_reference(*[args[k] for k in FWD_PARAMS])


def _output_shape():
    out = _jax.eval_shape(lambda: _forward(_fwd_setup_inputs(0)))
    return out.shape, out.dtype

N_MICROBATCH = 1
ADAM_LR = 0.001
ADAM_B1 = 0.9
ADAM_B2 = 0.999
ADAM_EPS = 1e-08
ADAM_WD = 0.01
ADAM_STEP = 10
PER_EXAMPLE_BATCH_AXIS = {'x': 0, 'c': 0, 'ctx': 0, 'loss_target': 0}
SHARED_INPUTS = []
_WEIGHT_DTYPES = {'c_ctx': _jnp.float32, 'w_mod': _jnp.float32, 'b_mod': _jnp.float32, 'norm_mix_g': _jnp.float32, 'norm_ffn_g': _jnp.float32, 'final_norm_g': _jnp.float32, 'ev_w_in': _jnp.float32, 'ev_w_out': _jnp.float32, 'ssd_conv_w': _jnp.float32, 'ssd_conv_b': _jnp.float32, 'ssd_dt_bias': _jnp.float32, 'ssd_a_log': _jnp.float32, 'ssd_d': _jnp.float32, 'ssd_norm_g': _jnp.float32, 'lru_conv_w': _jnp.float32, 'lru_conv_b': _jnp.float32, 'lru_w_a': _jnp.float32, 'lru_b_a': _jnp.float32, 'lru_w_i': _jnp.float32, 'lru_b_i': _jnp.float32, 'lru_lam': _jnp.float32, 'od_w_in': _jnp.float32, 'od_w_out': _jnp.float32, 'hg_lb_logits': _jnp.float32, 'hg_norm_g': _jnp.float32, 's5_lam_re': _jnp.float32, 's5_lam_im': _jnp.float32, 's5_log_step': _jnp.float32, 's5_b_re': _jnp.float32, 's5_b_im': _jnp.float32, 's5_c_re': _jnp.float32, 's5_c_im': _jnp.float32, 's5_d': _jnp.float32, 's5_glu_w': _jnp.float32, 's5_glu_b': _jnp.float32, 'ffn_w_gate': _jnp.float32, 'ffn_w_up': _jnp.float32, 'ffn_conv_w': _jnp.float32, 'ffn_conv_b': _jnp.float32, 'ffn_w_down': _jnp.float32}
MOMENT_SCALE = {'c_ctx': 6.142528e-02, 'w_mod': 1.734372e-01, 'b_mod': 2.863915e-01, 'norm_mix_g': 1.276585e-01, 'norm_ffn_g': 6.758139e-02, 'final_norm_g': 6.499683e+01, 'ev_w_in': 1.166191e-01, 'ev_w_out': 1.752776e-01, 'ssd_conv_w': 4.174165e-02, 'ssd_conv_b': 6.422399e-02, 'ssd_dt_bias': 1.104318e-01, 'ssd_a_log': 1.116933e-01, 'ssd_d': 1.707727e-01, 'ssd_norm_g': 5.173888e-02, 'lru_conv_w': 1.850371e-01, 'lru_conv_b': 4.836120e-01, 'lru_w_a': 9.601309e-03, 'lru_b_a': 1.412703e-02, 'lru_w_i': 1.955489e-02, 'lru_b_i': 3.559226e-02, 'lru_lam': 3.481769e-02, 'od_w_in': 2.807241e-02, 'od_w_out': 4.057109e-02, 'hg_lb_logits': 1.839687e-03, 'hg_norm_g': 4.430568e-02, 's5_lam_re': 1.274488e-02, 's5_lam_im': 1.136855e-02, 's5_log_step': 7.676380e+00, 's5_b_re': 1.244928e-02, 's5_b_im': 1.074295e-02, 's5_c_re': 2.886742e-03, 's5_c_im': 2.706299e-03, 's5_d': 2.978424e-02, 's5_glu_w': 1.210138e-02, 's5_glu_b': 1.600057e-02, 'ffn_w_gate': 3.018963e-02, 'ffn_w_up': 2.965698e-02, 'ffn_conv_w': 3.005678e-02, 'ffn_conv_b': 2.646097e-02, 'ffn_w_down': 4.932197e-02}


def _to_microbatches(a, axis):
    t = _jnp.moveaxis(a, axis, 0)
    t = t.reshape((N_MICROBATCH, t.shape[0] // N_MICROBATCH) + t.shape[1:])
    return _jnp.moveaxis(t, 1, axis + 1)


def setup_inputs(seed: int = 0) -> dict:
    inp = _fwd_setup_inputs(seed)
    key = _jax.random.fold_in(_jax.random.key(seed), 7919)
    shape, _ = _output_shape()
    out = dict(inp)
    out["loss_target"] = _jax.random.normal(_jax.random.fold_in(key, 0), shape, _jnp.float32)
    for i, name in enumerate(TWIN_WEIGHTS):
        w = inp[name].astype(_jnp.float32)
        if MOMENT_SCALE is None:
            s = _jnp.sqrt(_jnp.mean(_jnp.square(w)) + 1e-30)
        else:
            s = MOMENT_SCALE[name]
        km, kv = _jax.random.split(_jax.random.fold_in(key, i + 1))
        out[name] = w
        out["m_" + name] = s * _jax.random.normal(km, w.shape, _jnp.float32)
        out["v_" + name] = (s * s) * _jax.random.uniform(kv, w.shape, _jnp.float32, 0.5, 1.5)
    if N_MICROBATCH > 1:
        for name, axis in PER_EXAMPLE_BATCH_AXIS.items():
            out[name] = _to_microbatches(out[name], axis)
    return {'x': out['x'], 'c': out['c'], 'ctx': out['ctx'], 'c_ctx': out['c_ctx'], 'w_mod': out['w_mod'], 'b_mod': out['b_mod'], 'norm_mix_g': out['norm_mix_g'], 'norm_ffn_g': out['norm_ffn_g'], 'final_norm_g': out['final_norm_g'], 'ev_w_in': out['ev_w_in'], 'ev_w_out': out['ev_w_out'], 'ssd_conv_w': out['ssd_conv_w'], 'ssd_conv_b': out['ssd_conv_b'], 'ssd_dt_bias': out['ssd_dt_bias'], 'ssd_a_log': out['ssd_a_log'], 'ssd_d': out['ssd_d'], 'ssd_norm_g': out['ssd_norm_g'], 'lru_conv_w': out['lru_conv_w'], 'lru_conv_b': out['lru_conv_b'], 'lru_w_a': out['lru_w_a'], 'lru_b_a': out['lru_b_a'], 'lru_w_i': out['lru_w_i'], 'lru_b_i': out['lru_b_i'], 'lru_lam': out['lru_lam'], 'od_w_in': out['od_w_in'], 'od_w_out': out['od_w_out'], 'hg_lb_logits': out['hg_lb_logits'], 'hg_norm_g': out['hg_norm_g'], 's5_lam_re': out['s5_lam_re'], 's5_lam_im': out['s5_lam_im'], 's5_log_step': out['s5_log_step'], 's5_b_re': out['s5_b_re'], 's5_b_im': out['s5_b_im'], 's5_c_re': out['s5_c_re'], 's5_c_im': out['s5_c_im'], 's5_d': out['s5_d'], 's5_glu_w': out['s5_glu_w'], 's5_glu_b': out['s5_glu_b'], 'ffn_w_gate': out['ffn_w_gate'], 'ffn_w_up': out['ffn_w_up'], 'ffn_conv_w': out['ffn_conv_w'], 'ffn_conv_b': out['ffn_conv_b'], 'ffn_w_down': out['ffn_w_down'], 'loss_target': out['loss_target'], 'm_c_ctx': out['m_c_ctx'], 'm_w_mod': out['m_w_mod'], 'm_b_mod': out['m_b_mod'], 'm_norm_mix_g': out['m_norm_mix_g'], 'm_norm_ffn_g': out['m_norm_ffn_g'], 'm_final_norm_g': out['m_final_norm_g'], 'm_ev_w_in': out['m_ev_w_in'], 'm_ev_w_out': out['m_ev_w_out'], 'm_ssd_conv_w': out['m_ssd_conv_w'], 'm_ssd_conv_b': out['m_ssd_conv_b'], 'm_ssd_dt_bias': out['m_ssd_dt_bias'], 'm_ssd_a_log': out['m_ssd_a_log'], 'm_ssd_d': out['m_ssd_d'], 'm_ssd_norm_g': out['m_ssd_norm_g'], 'm_lru_conv_w': out['m_lru_conv_w'], 'm_lru_conv_b': out['m_lru_conv_b'], 'm_lru_w_a': out['m_lru_w_a'], 'm_lru_b_a': out['m_lru_b_a'], 'm_lru_w_i': out['m_lru_w_i'], 'm_lru_b_i': out['m_lru_b_i'], 'm_lru_lam': out['m_lru_lam'], 'm_od_w_in': out['m_od_w_in'], 'm_od_w_out': out['m_od_w_out'], 'm_hg_lb_logits': out['m_hg_lb_logits'], 'm_hg_norm_g': out['m_hg_norm_g'], 'm_s5_lam_re': out['m_s5_lam_re'], 'm_s5_lam_im': out['m_s5_lam_im'], 'm_s5_log_step': out['m_s5_log_step'], 'm_s5_b_re': out['m_s5_b_re'], 'm_s5_b_im': out['m_s5_b_im'], 'm_s5_c_re': out['m_s5_c_re'], 'm_s5_c_im': out['m_s5_c_im'], 'm_s5_d': out['m_s5_d'], 'm_s5_glu_w': out['m_s5_glu_w'], 'm_s5_glu_b': out['m_s5_glu_b'], 'm_ffn_w_gate': out['m_ffn_w_gate'], 'm_ffn_w_up': out['m_ffn_w_up'], 'm_ffn_conv_w': out['m_ffn_conv_w'], 'm_ffn_conv_b': out['m_ffn_conv_b'], 'm_ffn_w_down': out['m_ffn_w_down'], 'v_c_ctx': out['v_c_ctx'], 'v_w_mod': out['v_w_mod'], 'v_b_mod': out['v_b_mod'], 'v_norm_mix_g': out['v_norm_mix_g'], 'v_norm_ffn_g': out['v_norm_ffn_g'], 'v_final_norm_g': out['v_final_norm_g'], 'v_ev_w_in': out['v_ev_w_in'], 'v_ev_w_out': out['v_ev_w_out'], 'v_ssd_conv_w': out['v_ssd_conv_w'], 'v_ssd_conv_b': out['v_ssd_conv_b'], 'v_ssd_dt_bias': out['v_ssd_dt_bias'], 'v_ssd_a_log': out['v_ssd_a_log'], 'v_ssd_d': out['v_ssd_d'], 'v_ssd_norm_g': out['v_ssd_norm_g'], 'v_lru_conv_w': out['v_lru_conv_w'], 'v_lru_conv_b': out['v_lru_conv_b'], 'v_lru_w_a': out['v_lru_w_a'], 'v_lru_b_a': out['v_lru_b_a'], 'v_lru_w_i': out['v_lru_w_i'], 'v_lru_b_i': out['v_lru_b_i'], 'v_lru_lam': out['v_lru_lam'], 'v_od_w_in': out['v_od_w_in'], 'v_od_w_out': out['v_od_w_out'], 'v_hg_lb_logits': out['v_hg_lb_logits'], 'v_hg_norm_g': out['v_hg_norm_g'], 'v_s5_lam_re': out['v_s5_lam_re'], 'v_s5_lam_im': out['v_s5_lam_im'], 'v_s5_log_step': out['v_s5_log_step'], 'v_s5_b_re': out['v_s5_b_re'], 'v_s5_b_im': out['v_s5_b_im'], 'v_s5_c_re': out['v_s5_c_re'], 'v_s5_c_im': out['v_s5_c_im'], 'v_s5_d': out['v_s5_d'], 'v_s5_glu_w': out['v_s5_glu_w'], 'v_s5_glu_b': out['v_s5_glu_b'], 'v_ffn_w_gate': out['v_ffn_w_gate'], 'v_ffn_w_up': out['v_ffn_w_up'], 'v_ffn_conv_w': out['v_ffn_conv_w'], 'v_ffn_conv_b': out['v_ffn_conv_b'], 'v_ffn_w_down': out['v_ffn_w_down']}


def _loss(weights, diff, rest, loss_target):
    with _jax.named_scope("forward"):
        args = {**rest, TWIN_DIFF_INPUT: diff, **{k: w.astype(_WEIGHT_DTYPES[k]) for k, w in weights.items()}}
        y = _forward(args)
    with _jax.named_scope("loss_head"):
        err = _jnp.square(y.astype(_jnp.float32) - loss_target)
        return 0.5 * _jnp.sum(_jnp.mean(err, axis=-1)) if err.ndim else 0.5 * err


def _adamw(w, g, m, v):
    m = ADAM_B1 * m + (1.0 - ADAM_B1) * g
    v = ADAM_B2 * v + (1.0 - ADAM_B2) * _jnp.square(g)
    m_hat = m / (1.0 - ADAM_B1 ** ADAM_STEP)
    v_hat = v / (1.0 - ADAM_B2 ** ADAM_STEP)
    delta = -ADAM_LR * (m_hat / (_jnp.sqrt(v_hat) + ADAM_EPS) + ADAM_WD * w)
    return delta, m, v


def reference(x, c, ctx, c_ctx, w_mod, b_mod, norm_mix_g, norm_ffn_g, final_norm_g, ev_w_in, ev_w_out, ssd_conv_w, ssd_conv_b, ssd_dt_bias, ssd_a_log, ssd_d, ssd_norm_g, lru_conv_w, lru_conv_b, lru_w_a, lru_b_a, lru_w_i, lru_b_i, lru_lam, od_w_in, od_w_out, hg_lb_logits, hg_norm_g, s5_lam_re, s5_lam_im, s5_log_step, s5_b_re, s5_b_im, s5_c_re, s5_c_im, s5_d, s5_glu_w, s5_glu_b, ffn_w_gate, ffn_w_up, ffn_conv_w, ffn_conv_b, ffn_w_down, loss_target, m_c_ctx, m_w_mod, m_b_mod, m_norm_mix_g, m_norm_ffn_g, m_final_norm_g, m_ev_w_in, m_ev_w_out, m_ssd_conv_w, m_ssd_conv_b, m_ssd_dt_bias, m_ssd_a_log, m_ssd_d, m_ssd_norm_g, m_lru_conv_w, m_lru_conv_b, m_lru_w_a, m_lru_b_a, m_lru_w_i, m_lru_b_i, m_lru_lam, m_od_w_in, m_od_w_out, m_hg_lb_logits, m_hg_norm_g, m_s5_lam_re, m_s5_lam_im, m_s5_log_step, m_s5_b_re, m_s5_b_im, m_s5_c_re, m_s5_c_im, m_s5_d, m_s5_glu_w, m_s5_glu_b, m_ffn_w_gate, m_ffn_w_up, m_ffn_conv_w, m_ffn_conv_b, m_ffn_w_down, v_c_ctx, v_w_mod, v_b_mod, v_norm_mix_g, v_norm_ffn_g, v_final_norm_g, v_ev_w_in, v_ev_w_out, v_ssd_conv_w, v_ssd_conv_b, v_ssd_dt_bias, v_ssd_a_log, v_ssd_d, v_ssd_norm_g, v_lru_conv_w, v_lru_conv_b, v_lru_w_a, v_lru_b_a, v_lru_w_i, v_lru_b_i, v_lru_lam, v_od_w_in, v_od_w_out, v_hg_lb_logits, v_hg_norm_g, v_s5_lam_re, v_s5_lam_im, v_s5_log_step, v_s5_b_re, v_s5_b_im, v_s5_c_re, v_s5_c_im, v_s5_d, v_s5_glu_w, v_s5_glu_b, v_ffn_w_gate, v_ffn_w_up, v_ffn_conv_w, v_ffn_conv_b, v_ffn_w_down):
    given = dict(x=x, c=c, ctx=ctx, c_ctx=c_ctx, w_mod=w_mod, b_mod=b_mod, norm_mix_g=norm_mix_g, norm_ffn_g=norm_ffn_g, final_norm_g=final_norm_g, ev_w_in=ev_w_in, ev_w_out=ev_w_out, ssd_conv_w=ssd_conv_w, ssd_conv_b=ssd_conv_b, ssd_dt_bias=ssd_dt_bias, ssd_a_log=ssd_a_log, ssd_d=ssd_d, ssd_norm_g=ssd_norm_g, lru_conv_w=lru_conv_w, lru_conv_b=lru_conv_b, lru_w_a=lru_w_a, lru_b_a=lru_b_a, lru_w_i=lru_w_i, lru_b_i=lru_b_i, lru_lam=lru_lam, od_w_in=od_w_in, od_w_out=od_w_out, hg_lb_logits=hg_lb_logits, hg_norm_g=hg_norm_g, s5_lam_re=s5_lam_re, s5_lam_im=s5_lam_im, s5_log_step=s5_log_step, s5_b_re=s5_b_re, s5_b_im=s5_b_im, s5_c_re=s5_c_re, s5_c_im=s5_c_im, s5_d=s5_d, s5_glu_w=s5_glu_w, s5_glu_b=s5_glu_b, ffn_w_gate=ffn_w_gate, ffn_w_up=ffn_w_up, ffn_conv_w=ffn_conv_w, ffn_conv_b=ffn_conv_b, ffn_w_down=ffn_w_down, loss_target=loss_target, m_c_ctx=m_c_ctx, m_w_mod=m_w_mod, m_b_mod=m_b_mod, m_norm_mix_g=m_norm_mix_g, m_norm_ffn_g=m_norm_ffn_g, m_final_norm_g=m_final_norm_g, m_ev_w_in=m_ev_w_in, m_ev_w_out=m_ev_w_out, m_ssd_conv_w=m_ssd_conv_w, m_ssd_conv_b=m_ssd_conv_b, m_ssd_dt_bias=m_ssd_dt_bias, m_ssd_a_log=m_ssd_a_log, m_ssd_d=m_ssd_d, m_ssd_norm_g=m_ssd_norm_g, m_lru_conv_w=m_lru_conv_w, m_lru_conv_b=m_lru_conv_b, m_lru_w_a=m_lru_w_a, m_lru_b_a=m_lru_b_a, m_lru_w_i=m_lru_w_i, m_lru_b_i=m_lru_b_i, m_lru_lam=m_lru_lam, m_od_w_in=m_od_w_in, m_od_w_out=m_od_w_out, m_hg_lb_logits=m_hg_lb_logits, m_hg_norm_g=m_hg_norm_g, m_s5_lam_re=m_s5_lam_re, m_s5_lam_im=m_s5_lam_im, m_s5_log_step=m_s5_log_step, m_s5_b_re=m_s5_b_re, m_s5_b_im=m_s5_b_im, m_s5_c_re=m_s5_c_re, m_s5_c_im=m_s5_c_im, m_s5_d=m_s5_d, m_s5_glu_w=m_s5_glu_w, m_s5_glu_b=m_s5_glu_b, m_ffn_w_gate=m_ffn_w_gate, m_ffn_w_up=m_ffn_w_up, m_ffn_conv_w=m_ffn_conv_w, m_ffn_conv_b=m_ffn_conv_b, m_ffn_w_down=m_ffn_w_down, v_c_ctx=v_c_ctx, v_w_mod=v_w_mod, v_b_mod=v_b_mod, v_norm_mix_g=v_norm_mix_g, v_norm_ffn_g=v_norm_ffn_g, v_final_norm_g=v_final_norm_g, v_ev_w_in=v_ev_w_in, v_ev_w_out=v_ev_w_out, v_ssd_conv_w=v_ssd_conv_w, v_ssd_conv_b=v_ssd_conv_b, v_ssd_dt_bias=v_ssd_dt_bias, v_ssd_a_log=v_ssd_a_log, v_ssd_d=v_ssd_d, v_ssd_norm_g=v_ssd_norm_g, v_lru_conv_w=v_lru_conv_w, v_lru_conv_b=v_lru_conv_b, v_lru_w_a=v_lru_w_a, v_lru_b_a=v_lru_b_a, v_lru_w_i=v_lru_w_i, v_lru_b_i=v_lru_b_i, v_lru_lam=v_lru_lam, v_od_w_in=v_od_w_in, v_od_w_out=v_od_w_out, v_hg_lb_logits=v_hg_lb_logits, v_hg_norm_g=v_hg_norm_g, v_s5_lam_re=v_s5_lam_re, v_s5_lam_im=v_s5_lam_im, v_s5_log_step=v_s5_log_step, v_s5_b_re=v_s5_b_re, v_s5_b_im=v_s5_b_im, v_s5_c_re=v_s5_c_re, v_s5_c_im=v_s5_c_im, v_s5_d=v_s5_d, v_s5_glu_w=v_s5_glu_w, v_s5_glu_b=v_s5_glu_b, v_ffn_w_gate=v_ffn_w_gate, v_ffn_w_up=v_ffn_w_up, v_ffn_conv_w=v_ffn_conv_w, v_ffn_conv_b=v_ffn_conv_b, v_ffn_w_down=v_ffn_w_down)
    weights = {n: given[n] for n in TWIN_WEIGHTS}
    shared = {n: given[n] for n in SHARED_INPUTS}
    per_example = {n: given[n] for n in ['x', 'c', 'ctx']}
    grad_fn = _jax.value_and_grad(_loss, argnums=(0, 1))

    def one_microbatch(ex, loss_target):
        ex = dict(ex)
        diff = ex.pop(TWIN_DIFF_INPUT)
        return grad_fn(weights, diff, {**shared, **ex}, loss_target)

    if N_MICROBATCH == 1:
        loss, (grad_w, grad_x) = one_microbatch(per_example, given["loss_target"])
    else:
        def body(carry, xs):
            loss_sum, grad_sum = carry
            l_k, (gw_k, gx_k) = one_microbatch(xs[0], xs[1])
            with _jax.named_scope("update"):
                return (loss_sum + l_k, _jax.tree.map(_jnp.add, grad_sum, gw_k)), gx_k

        init = (_jnp.zeros((), _jnp.float32), _jax.tree.map(_jnp.zeros_like, weights))
        (loss, grad_w), grad_x = _jax.lax.scan(body, init, (per_example, given["loss_target"]))
    with _jax.named_scope("update"):
        delta_w, new_m, new_v = {}, {}, {}
        for n in TWIN_WEIGHTS:
            delta_w[n], new_m[n], new_v[n] = _adamw(weights[n], grad_w[n], given["m_" + n], given["v_" + n])
    return (loss, grad_x, *[grad_w[n] for n in TWIN_WEIGHTS], *[delta_w[n] for n in TWIN_WEIGHTS],
            *[new_m[n] for n in TWIN_WEIGHTS], *[new_v[n] for n in TWIN_WEIGHTS])
```

```python
import functools
import math

import jax
import jax.numpy as jnp
from jax import lax
from jax.experimental import pallas as pl
from jax.experimental.pallas import tpu as pltpu

F32 = jnp.float32
BF16 = jnp.bfloat16
HI = lax.Precision.HIGHEST
MESH = pl.DeviceIdType.MESH

D = 1024
DEPTH = 4
N_MOD = 6
RMS_EPS = 1e-6
GRID_W = 64
SSD_HEADS = 16
SSD_CHUNK = 128
HG_W = 768
HG_HEADS = 6
HG_CHUNK = 32
S5_W = 256
D_FF = 2816
EV_PAD = 5120
LRU_C = 8.0
V7X_VMEM_LIMIT = 56 * 1024 * 1024
MM_VMEM_BUDGET = 36 * 1024 * 1024

ADAM_LR, ADAM_B1, ADAM_B2, ADAM_EPS, ADAM_WD, ADAM_STEP = 0.001, 0.9, 0.999, 1e-08, 0.01, 10

W_NAMES = ['c_ctx', 'w_mod', 'b_mod', 'norm_mix_g', 'norm_ffn_g', 'final_norm_g', 'ev_w_in', 'ev_w_out', 'ssd_conv_w',
           'ssd_conv_b', 'ssd_dt_bias', 'ssd_a_log', 'ssd_d', 'ssd_norm_g', 'lru_conv_w', 'lru_conv_b', 'lru_w_a', 'lru_b_a',
           'lru_w_i', 'lru_b_i', 'lru_lam', 'od_w_in', 'od_w_out', 'hg_lb_logits', 'hg_norm_g', 's5_lam_re', 's5_lam_im',
           's5_log_step', 's5_b_re', 's5_b_im', 's5_c_re', 's5_c_im', 's5_d', 's5_glu_w', 's5_glu_b', 'ffn_w_gate', 'ffn_w_up',
           'ffn_conv_w', 'ffn_conv_b', 'ffn_w_down']
SHARD_AXIS = {'w_mod': 2, 'ev_w_in': 2, 'ev_w_out': 1, 'ssd_conv_w': 2, 'lru_conv_w': 2, 'lru_b_a': 2, 'lru_b_i': 2,
              'lru_lam': 2, 'od_w_in': 2, 'od_w_out': 1, 's5_d': 1, 's5_glu_w': 1, 's5_glu_b': 1, 'ffn_w_gate': 2,
              'ffn_w_up': 2, 'ffn_conv_w': 3, 'ffn_w_down': 1}
MATMUL_WEIGHTS = ('w_mod', 'ev_w_in', 'ev_w_out', 'od_w_in', 'od_w_out', 'ffn_w_gate', 'ffn_w_up', 'ffn_w_down')


def _cparams(sem=None):
    return pltpu.CompilerParams(vmem_limit_bytes=V7X_VMEM_LIMIT, dimension_semantics=sem)


def _pick(n, cands):
    for c in cands:
        if n % c == 0:
            return c
    return n


def tile_fwd(name, f, grid, ins, outs):
    n_in = len(ins)

    def body(*refs):
        res = f(*[r[...] for r in refs[:n_in]])
        if not isinstance(res, (tuple, list)):
            res = (res,)
        for r, o in zip(res, refs[n_in:]):
            o[...] = r.astype(o.dtype)

    res = pl.pallas_call(
        body, grid=grid,
        in_specs=[pl.BlockSpec(b, m) for _, b, m in ins],
        out_specs=[pl.BlockSpec(b, m) for _, _, b, m in outs],
        out_shape=[jax.ShapeDtypeStruct(s, d) for s, d, _, _ in outs],
        name=name, compiler_params=_cparams(("arbitrary",) * len(grid)),
    )(*[a for a, _, _ in ins])
    return res


def tile_bwd(name, f, grid, ins, cts, grads, prims=()):
    n_in = len(ins)
    ct_flat = [p for c in cts for p in c]
    n_ct = len(ct_flat)
    didx = [g[0] for g in grads]

    def body(*refs):
        in_refs, ct_refs = refs[:n_in], refs[n_in:n_in + n_ct]
        g_refs = refs[n_in + n_ct:n_in + n_ct + len(grads)]
        p_refs = refs[n_in + n_ct + len(grads):]
        vals = [r[...] for r in in_refs]

        def fd(*dv):
            full = list(vals)
            for i, v in zip(didx, dv):
                full[i] = v
            res = f(*full)
            return tuple(res) if isinstance(res, (tuple, list)) else (res,)

        out, vjp = jax.vjp(fd, *[vals[i] for i in didx])
        ctv, k = [], 0
        for o, c in zip(out, cts):
            acc = None
            for _ in c:
                piece = ct_refs[k][...].astype(o.dtype)
                acc = piece if acc is None else acc + piece
                k += 1
            ctv.append(jnp.zeros_like(o) if acc is None else acc.reshape(o.shape))
        gs = vjp(tuple(ctv))
        ids = [pl.program_id(a) for a in range(len(grid))]

        def emit(ref, val, first):
            if first is None:
                ref[...] = val.astype(ref.dtype)
            else:
                is_first = first(*ids)

                @pl.when(is_first)
                def _():
                    ref[...] = val.astype(ref.dtype)

                @pl.when(jnp.logical_not(is_first))
                def _():
                    ref[...] += val.astype(ref.dtype)

        for g, spec, ref in zip(gs, grads, g_refs):
            emit(ref, g, spec[5])
        for spec, ref in zip(prims, p_refs):
            emit(ref, out[spec[0]], spec[5])

    specs = list(grads) + list(prims)
    res = pl.pallas_call(
        body, grid=grid,
        in_specs=[pl.BlockSpec(b, m) for _, b, m in list(ins) + ct_flat],
        out_specs=[pl.BlockSpec(s[3], s[4]) for s in specs],
        out_shape=[jax.ShapeDtypeStruct(s[1], s[2]) for s in specs],
        name=name, compiler_params=_cparams(("arbitrary",) * len(grid)),
    )(*[a for a, _, _ in list(ins) + ct_flat])
    return res


def mm(name, pairs, ta=False, tb=False, out_dtype=F32):
    a0, b0, _ = pairs[0]
    m = a0.shape[1] if ta else a0.shape[0]
    n = b0.shape[0] if tb else b0.shape[1]
    cands = (1024, 1408, 768, 512, 256, 128)
    tks, nks = [], []
    for a, b, _ in pairs:
        k = a.shape[0] if ta else a.shape[1]
        tk = _pick(k, cands)
        tks.append(tk)
        nks.append(k // tk)

    def vmem_bytes(tm, tn):
        tiles = sum(2 * tk * (tm * a.dtype.itemsize + tn * b.dtype.itemsize) for (a, b, _), tk in zip(pairs, tks))
        return tiles + tm * tn * (4 + 2 * jnp.dtype(out_dtype).itemsize)

    tm_c = [c_ for c_ in cands if m % c_ == 0] or [m]
    tn_c = [c_ for c_ in cands if n % c_ == 0] or [n]
    tm, tn = tm_c[0], tn_c[0]
    while vmem_bytes(tm, tn) > MM_VMEM_BUDGET and (len(tm_c) > 1 or len(tn_c) > 1):
        if len(tm_c) > 1 and (tm >= tn or len(tn_c) == 1):
            tm_c = tm_c[1:]
        else:
            tn_c = tn_c[1:]
        tm, tn = tm_c[0], tn_c[0]
    starts = [sum(nks[:p]) for p in range(len(pairs))]
    nk = sum(nks)
    np_ = len(pairs)

    def body(*refs):
        o_ref, acc = refs[2 * np_], refs[2 * np_ + 1]
        kk = pl.program_id(2)

        @pl.when(kk == 0)
        def _():
            acc[...] = jnp.zeros_like(acc)

        for p in range(np_):
            def add(p=p):
                a = refs[2 * p][...].astype(BF16)
                b = refs[2 * p + 1][...].astype(BF16)
                dn = (((0 if ta else 1,), (1 if tb else 0,)), ((), ()))
                acc[...] += lax.dot_general(a, b, dn, preferred_element_type=F32)
            if np_ == 1:
                add()
            else:
                pl.when((kk >= starts[p]) & (kk < starts[p] + nks[p]))(add)

        @pl.when(kk == nk - 1)
        def _():
            o_ref[...] = acc[...].astype(o_ref.dtype)

    in_specs, args = [], []
    for p, (a, b, off) in enumerate(pairs):
        tk, s0, nkp = tks[p], starts[p], nks[p]
        assert off % tk == 0
        boff = off // tk

        def kloc(k, s0=s0, nkp=nkp):
            return jnp.clip(k - s0, 0, nkp - 1)
        if ta:
            in_specs.append(pl.BlockSpec((tk, tm), lambda i, j, k, kloc=kloc: (kloc(k), i)))
        else:
            in_specs.append(pl.BlockSpec((tm, tk), lambda i, j, k, kloc=kloc: (i, kloc(k))))
        if tb:
            in_specs.append(pl.BlockSpec((tn, tk), lambda i, j, k, kloc=kloc, boff=boff: (j, boff + kloc(k))))
        else:
            in_specs.append(pl.BlockSpec((tk, tn), lambda i, j, k, kloc=kloc, boff=boff: (boff + kloc(k), j)))
        args += [a, b]
    return pl.pallas_call(
        body, grid=(m // tm, n // tn, nk), in_specs=in_specs,
        out_specs=pl.BlockSpec((tm, tn), lambda i, j, k: (i, j)),
        out_shape=jax.ShapeDtypeStruct((m, n), out_dtype),
        scratch_shapes=[pltpu.VMEM((tm, tn), F32)],
        name=name, compiler_params=_cparams(("arbitrary", "arbitrary", "arbitrary")),
    )(*args)


def _rms(x, g):
    return x * lax.rsqrt(jnp.mean(x * x, axis=-1, keepdims=True) + RMS_EPS) * g


def f_nm0(x, g, sh, sc):
    return _rms(x, g) * (1.0 + sc) + sh


def f_nm(xp, o, gate, g, sh, sc):
    x = xp + gate * o
    return x, _rms(x, g) * (1.0 + sc) + sh


def f_final(xp, o, gate, g, tgt, valid):
    x = xp + gate * o
    e = (_rms(x, g) - tgt) * valid
    return jnp.sum(e * e, axis=0, keepdims=True) * (0.5 / D)


@functools.partial(jax.custom_vjp, nondiff_argnums=(1,))
def _sroll(x, s):
    return pltpu.roll(x, s, 0)


def _sroll_fwd(x, s):
    return pltpu.roll(x, s, 0), None


def _sroll_bwd(s, _, g):
    return (pltpu.roll(g, (g.shape[0] - s) % g.shape[0], 0),)


_sroll.defvjp(_sroll_fwd, _sroll_bwd)


def _shifted(x, o):
    n = x.shape[0]
    return x if o == 0 else _sroll(x, (n - o) % n)


def f_conv1d(x, w, b, *, lc, act):
    n = x.shape[0]
    pos = lax.broadcasted_iota(jnp.int32, (n, 1), 0)
    lo = jnp.where(pos < lc, 0, lc)
    hi = jnp.where(pos < lc, lc, n)
    y = x * w[1:2] + b
    for k, o in ((0, -1), (2, 1), (3, 2)):
        src = pos + o
        valid = (src >= lo) & (src < hi)
        y = y + jnp.where(valid, _shifted(x, o), 0.0) * w[k:k + 1]
    return jax.nn.silu(y) if act else y


def f_ffnconv(a, up, w, b, *, lc):
    n = a.shape[0]
    rows = (n - lc) // GRID_W
    pos = lax.broadcasted_iota(jnp.int32, (n, 1), 0)
    is_ctx = pos < lc
    tl = pos - lc
    r = tl // GRID_W
    cc = tl - r * GRID_W
    y = jnp.zeros_like(a) + b
    for dr in (-1, 0, 1):
        for dc in (-1, 0, 1):
            v_lat = jnp.logical_not(is_ctx) & (r + dr >= 0) & (r + dr < rows) & (cc + dc >= 0) & (cc + dc < GRID_W)
            if dr == 0:
                valid = v_lat | (is_ctx & (pos + dc >= 0) & (pos + dc < lc))
            else:
                valid = v_lat
            k = 3 * (dr + 1) + (dc + 1)
            y = y + jnp.where(valid, _shifted(a, GRID_W * dr + dc), 0.0) * w[k:k + 1]
    return jax.nn.silu(y) * up


def f_ssd(xs, bc, dtraw, bias, alog, st, *, d, reverse):
    L = xs.shape[0]
    dtv = jax.nn.softplus(dtraw + bias)
    la = dtv * (-jnp.exp(alog))
    ri = lax.broadcasted_iota(jnp.int32, (L, L), 0)
    ci = lax.broadcasted_iota(jnp.int32, (L, L), 1)
    mask = (ci >= ri) if reverse else (ci <= ri)
    cum = jnp.dot(mask.astype(F32), la, precision=HI, preferred_element_type=F32)
    cum_t = cum.T
    tot = cum[0:1] if reverse else cum[L - 1:L]
    lo = lax.broadcasted_iota(jnp.int32, (1, 128), 1) < 64
    rlo = lax.broadcasted_iota(jnp.int32, (128, 1), 0) < 64
    ys, new = [], []
    cbs = {}
    for j in range(8):
        g = j // 4
        bg = bc[:, g * 128:(g + 1) * 128].astype(BF16)
        cg = bc[:, 256 + g * 128:256 + (g + 1) * 128].astype(BF16)
        if g not in cbs:
            cbs[g] = lax.dot_general(cg, bg, (((1,), (1,)), ((), ())), preferred_element_type=F32)
        cb = cbs[g]
        x = xs[:, j * 128:(j + 1) * 128]
        k1 = 16 * d + 2 * j
        k2 = k1 + 1
        c1, c2 = cum[:, k1:k1 + 1], cum[:, k2:k2 + 1]
        m1 = cb * jnp.exp(jnp.where(mask, c1 - cum_t[k1:k1 + 1, :], -1e30))
        m2 = cb * jnp.exp(jnp.where(mask, c2 - cum_t[k2:k2 + 1, :], -1e30))
        xdt = x * jnp.where(lo, dtv[:, k1:k1 + 1], dtv[:, k2:k2 + 1])
        xb = xdt.astype(BF16)
        y = jnp.where(lo, jnp.dot(m1.astype(BF16), xb, preferred_element_type=F32),
                      jnp.dot(m2.astype(BF16), xb, preferred_element_type=F32))
        sj = st[j]
        ch = lax.dot_general(cg, sj.astype(BF16), (((1,), (1,)), ((), ())), preferred_element_type=F32)
        y = y + ch * jnp.where(lo, jnp.exp(c1), jnp.exp(c2))
        t1, t2 = tot[:, k1:k1 + 1], tot[:, k2:k2 + 1]
        xe = (xdt * jnp.where(lo, jnp.exp(t1 - c1), jnp.exp(t2 - c2))).astype(BF16)
        upd = lax.dot_general(xe, bg, (((0,), (0,)), ((), ())), preferred_element_type=F32)
        new.append(sj * jnp.where(rlo, jnp.exp(t1), jnp.exp(t2)) + upd)
        ys.append(y)
    return jnp.concatenate(ys, axis=1), jnp.stack(new)


def f_hgrn(q_raw, f_raw, v, lb, zt, *, reverse):
    n = q_raw.shape[0]
    c = HG_CHUNK
    qa = jax.nn.silu(q_raw)
    logf = jnp.log(lb + (1.0 - lb) * jax.nn.sigmoid(f_raw))
    kk = (1.0 - lb) * jax.nn.sigmoid(-f_raw)
    ri = lax.broadcasted_iota(jnp.int32, (c, c), 0)
    ci = lax.broadcasted_iota(jnp.int32, (c, c), 1)
    tmat = ((ci >= ri) if reverse else (ci <= ri)).astype(F32)
    r3 = lax.broadcasted_iota(jnp.int32, (c, c, 128), 0)
    c3 = lax.broadcasted_iota(jnp.int32, (c, c, 128), 1)
    mask3 = (c3 >= r3) if reverse else (c3 <= r3)
    nch = n // c
    outs = [None] * nch
    for chn in (reversed(range(nch)) if reverse else range(nch)):
        sl = slice(chn * c, (chn + 1) * c)
        q, k, vv, lf = qa[sl], kk[sl], v[sl], logf[sl]
        cum = jnp.dot(tmat, lf, precision=HI, preferred_element_type=F32)
        dec = jnp.exp(jnp.where(mask3, cum[:, None, :] - cum[None, :, :], -1e30))
        att = jnp.sum(q[:, None, :] * dec * k[None, :, :], axis=-1)
        y = jnp.dot(att.astype(BF16), vv.astype(BF16), preferred_element_type=F32)
        y = y + lax.dot_general((q * jnp.exp(cum)).astype(BF16), zt.astype(BF16), (((1,), (1,)), ((), ())),
                                preferred_element_type=F32)
        tot = cum[0:1] if reverse else cum[c - 1:c]
        kd = (k * jnp.exp(tot - cum)).astype(BF16)
        zt = zt * jnp.exp(tot) + lax.dot_general(vv.astype(BF16), kd, (((0,), (0,)), ((), ())), preferred_element_type=F32)
        outs[chn] = y
    return jnp.concatenate(outs, axis=0), zt


def _expm1(x):
    poly = x * (1.0 + x * (0.5 + x * (1.0 / 6 + x * (1.0 / 24 + x * (1.0 / 120 + x * (1.0 / 720))))))
    return jnp.where(jnp.abs(x) < 0.3, poly, jnp.exp(x) - 1.0)


def f_gates(u, wa, ba, wi, bi, lam):
    rs, is_ = [], []
    for nb in range(8):
        un = u[:, nb * 128:(nb + 1) * 128].astype(BF16)
        rs.append(jnp.dot(un, wa[nb].astype(BF16), preferred_element_type=F32))
        is_.append(jnp.dot(un, wi[nb].astype(BF16), preferred_element_type=F32))
    r = jax.nn.sigmoid(jnp.concatenate(rs, axis=1) + ba)
    i = jax.nn.sigmoid(jnp.concatenate(is_, axis=1) + bi)
    log_a = -LRU_C * jax.nn.softplus(-lam) * r
    return jnp.exp(log_a), jnp.sqrt(-_expm1(2.0 * log_a)) * (i * u)


def f_ssdfin(y0, y1, xs, z, gy, h0, h1, dpad, ng):
    kk = lax.broadcasted_iota(jnp.int32, (128, D), 0)
    ch = lax.broadcasted_iota(jnp.int32, (128, D), 1)
    expand = (ch // 64 == kk).astype(F32)
    dvec = jnp.dot(dpad, expand, precision=HI, preferred_element_type=F32)[0:1]
    y = y0 + y1 + dvec * xs
    yn = _rms(y * jax.nn.silu(z), ng)
    r = (h0 + h1) * jax.nn.gelu(gy)
    return jnp.concatenate([yn, r], axis=1)


def f_oddfin(o0, o1, g, y0, y1, u, hn, sd, gw, gb):
    parts = []
    for h in range(HG_HEADS):
        sl = slice(h * 128, (h + 1) * 128)
        parts.append(_rms(o0[:, sl] + o1[:, sl], hn[h:h + 1]) * jax.nn.silu(g[:, sl]))
    y = jax.nn.gelu(y0 + y1 + sd * u)
    y = y * jax.nn.sigmoid(jnp.dot(y.astype(BF16), gw.astype(BF16), preferred_element_type=F32) + gb)
    return jnp.concatenate(parts + [y], axis=1)


def f_s5p(lre, lim, lstep, btr, bti):
    step = jnp.exp(lstep)
    mag = jnp.exp(lre * step)
    ar, ai = mag * jnp.cos(lim * step), mag * jnp.sin(lim * step)
    den = lre * lre + lim * lim
    zr = ((ar - 1.0) * lre + ai * lim) / den
    zi = (ai * lre - (ar - 1.0) * lim) / den
    bbr = zr[:, None, :] * btr - zi[:, None, :] * bti
    bbi = zr[:, None, :] * bti + zi[:, None, :] * btr
    return ar, ai, bbr, bbi


def f_lb(logits):
    m = jnp.max(logits, axis=0, keepdims=True)
    e = jnp.exp(logits - m)
    p = e / jnp.sum(e, axis=0, keepdims=True)
    return p[1:2], p[1:2] + p[2:3] + p[3:4]


def f_adamw(w, m, v, *gs):
    g = gs[0]
    for t in gs[1:]:
        g = g + t
    m = ADAM_B1 * m + (1.0 - ADAM_B1) * g
    v = ADAM_B2 * v + (1.0 - ADAM_B2) * jnp.square(g)
    m_hat = m / (1.0 - ADAM_B1 ** ADAM_STEP)
    v_hat = v / (1.0 - ADAM_B2 ** ADAM_STEP)
    delta = -ADAM_LR * (m_hat / (jnp.sqrt(v_hat) + ADAM_EPS) + ADAM_WD * w)
    return g, delta, m, v


def scan_fwd(name, f, grid, ins, y_out, st_out, state_shape, is_first):
    n_in = len(ins)

    def body(*refs):
        y_ref, so_ref, st = refs[n_in], refs[n_in + 1], refs[n_in + 2]
        ids = [pl.program_id(a) for a in range(len(grid))]

        @pl.when(is_first(*ids))
        def _():
            st[...] = jnp.zeros_like(st)

        s = st[...]
        so_ref[...] = s
        y, new = f(*[r[...] for r in refs[:n_in]], s)
        y_ref[...] = y.astype(y_ref.dtype)
        st[...] = new

    return pl.pallas_call(
        body, grid=grid,
        in_specs=[pl.BlockSpec(b, m) for _, b, m in ins],
        out_specs=[pl.BlockSpec(y_out[2], y_out[3]), pl.BlockSpec(st_out[2], st_out[3])],
        out_shape=[jax.ShapeDtypeStruct(y_out[0], y_out[1]), jax.ShapeDtypeStruct(st_out[0], st_out[1])],
        scratch_shapes=[pltpu.VMEM(state_shape, F32)],
        name=name, compiler_params=_cparams(("arbitrary",) * len(grid)),
    )(*[a for a, _, _ in ins])


def scan_bwd(name, f, grid, ins, st_in, dy, grads, state_shape, is_first):
    n_in = len(ins)
    didx = [g[0] for g in grads]

    def body(*refs):
        s_ref, dy_ref = refs[n_in], refs[n_in + 1]
        g_refs = refs[n_in + 2:n_in + 2 + len(grads)]
        dst = refs[n_in + 2 + len(grads)]
        ids = [pl.program_id(a) for a in range(len(grid))]

        @pl.when(is_first(*ids))
        def _():
            dst[...] = jnp.zeros_like(dst)

        vals = [r[...] for r in refs[:n_in]]

        def fd(s, *dv):
            full = list(vals)
            for i, v in zip(didx, dv):
                full[i] = v
            return f(*full, s)

        (y, _), vjp = jax.vjp(fd, s_ref[...], *[vals[i] for i in didx])
        gs = vjp((dy_ref[...].astype(y.dtype), dst[...]))
        dst[...] = gs[0]
        for g, spec, ref in zip(gs[1:], grads, g_refs):
            first = spec[5]
            if first is None:
                ref[...] = g.astype(ref.dtype)
            else:
                fst = first(*ids)

                @pl.when(fst)
                def _(ref=ref, g=g):
                    ref[...] = g.astype(ref.dtype)

                @pl.when(jnp.logical_not(fst))
                def _(ref=ref, g=g):
                    ref[...] += g.astype(ref.dtype)

    allin = list(ins) + [st_in, dy]
    return pl.pallas_call(
        body, grid=grid,
        in_specs=[pl.BlockSpec(b, m) for _, b, m in allin],
        out_specs=[pl.BlockSpec(s[3], s[4]) for s in grads],
        out_shape=[jax.ShapeDtypeStruct(s[1], s[2]) for s in grads],
        scratch_shapes=[pltpu.VMEM(state_shape, F32)],
        name=name, compiler_params=_cparams(("arbitrary",) * len(grid)),
    )(*[a for a, _, _ in allin])


def _tile_order(order, nt):
    if order == 'F':
        return (lambda j: j), True
    if order == 'Fb':
        return (lambda j: nt - 1 - j), False
    if order == 'R':
        return (lambda j: jnp.where(j == 0, 0, nt - j)), False
    return (lambda j: jnp.where(j == nt - 1, 0, j + 1)), True


def linrec(name, a, b, order):
    bsz, tt = a.shape[:2]
    tq = _pick(tt, (256, 128))
    nt = tt // tq
    phys, asc = _tile_order(order, nt)

    def body(a_ref, b_ref, h_ref, hp_ref, hs):
        @pl.when(pl.program_id(0) == 0)
        def _():
            hs[...] = jnp.zeros_like(hs)

        def step(i, hcur):
            t = i if asc else tq - 1 - i
            out = []
            for bi in range(bsz):
                hp_ref[bi, t] = hcur[bi]
                hn = a_ref[bi, t] * hcur[bi] + b_ref[bi, t]
                h_ref[bi, t] = hn
                out.append(hn)
            return tuple(out)

        fin = lax.fori_loop(0, tq, step, tuple(hs[bi] for bi in range(bsz)))
        for bi in range(bsz):
            hs[bi] = fin[bi]

    spec = pl.BlockSpec((bsz, tq, 8, 128), lambda j: (0, phys(j), 0, 0))
    return pl.pallas_call(
        body, grid=(nt,), in_specs=[spec, spec], out_specs=[spec, spec],
        out_shape=[jax.ShapeDtypeStruct(a.shape, F32)] * 2,
        scratch_shapes=[pltpu.VMEM((bsz, 8, 128), F32)],
        name=name, compiler_params=_cparams(("arbitrary",)),
    )(a, b)


def linrec_bwd(name, a, dh, hprev, order):
    bsz, tt = a.shape[:2]
    tq = _pick(tt, (256, 128))
    nt = tt // tq
    phys, asc = _tile_order(order, nt)

    def body(a_ref, dh_ref, hp_ref, g_ref, ga_ref, gs, as_):
        @pl.when(pl.program_id(0) == 0)
        def _():
            gs[...] = jnp.zeros_like(gs)
            as_[...] = jnp.zeros_like(as_)

        def step(i, carry):
            t = i if asc else tq - 1 - i
            out = []
            for bi in range(bsz):
                gcur, acur = carry[bi]
                gn = acur * gcur + dh_ref[bi, t]
                g_ref[bi, t] = gn
                ga_ref[bi, t] = gn * hp_ref[bi, t]
                out.append((gn, a_ref[bi, t]))
            return tuple(out)

        fin = lax.fori_loop(0, tq, step, tuple((gs[bi], as_[bi]) for bi in range(bsz)))
        for bi in range(bsz):
            gs[bi] = fin[bi][0]
            as_[bi] = fin[bi][1]

    spec = pl.BlockSpec((bsz, tq, 8, 128), lambda j: (0, phys(j), 0, 0))
    return pl.pallas_call(
        body, grid=(nt,), in_specs=[spec, spec, spec], out_specs=[spec, spec],
        out_shape=[jax.ShapeDtypeStruct(a.shape, F32)] * 2,
        scratch_shapes=[pltpu.VMEM((bsz, 8, 128), F32), pltpu.VMEM((bsz, 8, 128), F32)],
        name=name, compiler_params=_cparams(("arbitrary",)),
    )(a, dh, hprev)


def clinrec(name, x, coef, order):
    bsz, tt = x.shape[:2]
    tq = _pick(tt, (256, 128))
    nt = tt // tq
    phys, asc = _tile_order(order, nt)

    def body(x_ref, c_ref, h_ref, hp_ref, hs):
        @pl.when(pl.program_id(0) == 0)
        def _():
            hs[...] = jnp.zeros_like(hs)

        ar, ai = c_ref[0], c_ref[1]

        def step(i, carry):
            t = i if asc else tq - 1 - i
            out = []
            for bi in range(bsz):
                hr, hi = carry[bi]
                hp_ref[bi, t, 0] = hr
                hp_ref[bi, t, 1] = hi
                nr = ar * hr - ai * hi + x_ref[bi, t, 0]
                ni = ar * hi + ai * hr + x_ref[bi, t, 1]
                h_ref[bi, t, 0] = nr
                h_ref[bi, t, 1] = ni
                out.append((nr, ni))
            return tuple(out)

        fin = lax.fori_loop(0, tq, step, tuple((hs[bi, 0], hs[bi, 1]) for bi in range(bsz)))
        for bi in range(bsz):
            hs[bi, 0] = fin[bi][0]
            hs[bi, 1] = fin[bi][1]

    spec = pl.BlockSpec((bsz, tq, 2, 8, 128), lambda j: (0, phys(j), 0, 0, 0))
    cspec = pl.BlockSpec((2, 8, 128), lambda j: (0, 0, 0))
    return pl.pallas_call(
        body, grid=(nt,), in_specs=[spec, cspec], out_specs=[spec, spec],
        out_shape=[jax.ShapeDtypeStruct(x.shape, F32)] * 2,
        scratch_shapes=[pltpu.VMEM((bsz, 2, 8, 128), F32)],
        name=name, compiler_params=_cparams(("arbitrary",)),
    )(x, coef)


def clinrec_bwd(name, dh, hprev, coef, order):
    bsz, tt = dh.shape[:2]
    tq = _pick(tt, (256, 128))
    nt = tt // tq
    phys, asc = _tile_order(order, nt)

    def body(d_ref, hp_ref, c_ref, g_ref, dc_ref, gs):
        @pl.when(pl.program_id(0) == 0)
        def _():
            gs[...] = jnp.zeros_like(gs)
            dc_ref[...] = jnp.zeros_like(dc_ref)

        ar, ai = c_ref[0], c_ref[1]

        def step(i, carry):
            t = i if asc else tq - 1 - i
            gcar, dar, dai = carry
            out = []
            for bi in range(bsz):
                gr, gi = gcar[bi]
                nr = ar * gr + ai * gi + d_ref[bi, t, 0]
                ni = ar * gi - ai * gr + d_ref[bi, t, 1]
                g_ref[bi, t, 0] = nr
                g_ref[bi, t, 1] = ni
                hpr, hpi = hp_ref[bi, t, 0], hp_ref[bi, t, 1]
                dar = dar + nr * hpr + ni * hpi
                dai = dai + ni * hpr - nr * hpi
                out.append((nr, ni))
            return tuple(out), dar, dai

        z = jnp.zeros((8, 128), F32)
        fin, dar, dai = lax.fori_loop(0, tq, step, (tuple((gs[bi, 0], gs[bi, 1]) for bi in range(bsz)), z, z))
        for bi in range(bsz):
            gs[bi, 0] = fin[bi][0]
            gs[bi, 1] = fin[bi][1]
        dc_ref[0] += dar
        dc_ref[1] += dai

    spec = pl.BlockSpec((bsz, tq, 2, 8, 128), lambda j: (0, phys(j), 0, 0, 0))
    cspec = pl.BlockSpec((2, 8, 128), lambda j: (0, 0, 0))
    return pl.pallas_call(
        body, grid=(nt,), in_specs=[spec, spec, cspec], out_specs=[spec, cspec],
        out_shape=[jax.ShapeDtypeStruct(dh.shape, F32), jax.ShapeDtypeStruct((2, 8, 128), F32)],
        scratch_shapes=[pltpu.VMEM((bsz, 2, 8, 128), F32)],
        name=name, compiler_params=_cparams(("arbitrary",)),
    )(dh, hprev, coef)


def _blockdiag(bb):
    eye = jnp.eye(16, dtype=bb.dtype)
    return (bb[:, :, None, :] * eye[:, None, :, None]).reshape(256, 1024)


def _blockdiag_t(c):
    eye = jnp.eye(16, dtype=c.dtype)
    return (jnp.swapaxes(c, 1, 2)[:, :, None, :] * eye[:, None, :, None]).reshape(1024, 256)


def _unblockdiag(m):
    eye = jnp.eye(16, dtype=m.dtype)
    return jnp.sum(m.reshape(16, 16, 16, 64) * eye[:, None, :, None], axis=2)


def _unblockdiag_t(m):
    eye = jnp.eye(16, dtype=m.dtype)
    return jnp.swapaxes(jnp.sum(m.reshape(16, 64, 16, 16) * eye[:, None, :, None], axis=2), 1, 2)


def _pad_rows(v, rows=8, cols=128):
    out = jnp.zeros((rows, cols), F32)
    return out.at[0, :v.shape[0]].set(v)


def local_step(x, c, ctx, c_ctx, target, W):
    B, Tx, _ = x.shape
    Lc = ctx.shape[1]
    Tt = Lc + Tx
    tb = Lc
    nt = Tt // tb
    M = B * Tt
    nc = Tt // SSD_CHUNK
    ncc = Lc // SSD_CHUNK
    G = {}

    def add_grad(name, idx, val):
        G.setdefault(name, {})[idx] = val

    def tok(a, cb=None, off=0):
        cb = a.shape[-1] if cb is None else cb
        return (a, (None, tb, cb), lambda b, j, off=off: (b, j, off))

    def tok_out(cols, dtype=F32):
        return ((B, Tt, cols), dtype, (None, tb, cols), lambda b, j: (b, j, 0))

    def vec(a):
        return (a, a.shape, lambda *ids, n=a.ndim: (0,) * n)

    def vec_acc(shape):
        return (shape, F32, shape, lambda *ids, n=len(shape): (0,) * n, lambda *ids: functools.reduce(jnp.logical_and, [i == 0 for i in ids]))

    def modv(l, which):
        return (modr, (None, None, None, 1, D), lambda b, j, l=l, which=which: (l, jnp.where(j == 0, 4, b), which, 0, 0))

    dmod_spec = ((B, 2, 1, D), F32, (None, None, 1, D), lambda b, j: (b, jnp.where(j == 0, 0, 1), 0, 0), lambda b, j: j <= 1)

    def phys_chunk(n_all, n_ctx, reverse):
        if not reverse:
            return lambda s: s
        return lambda s: jnp.where(s < n_ctx, n_ctx - 1 - s, n_all - 1 - (s - n_ctx))

    cc = jnp.zeros((8, D), F32).at[:B].set(c).at[4].set(c_ctx)
    nmc = N_MOD * D // 1536

    def f_mod(ccv, w, b):
        return jnp.dot(jax.nn.silu(ccv).astype(BF16), w, preferred_element_type=F32) + b

    mod = tile_fwd('mod_fwd', f_mod, (DEPTH, nmc),
                   [(cc, (8, D), lambda l, n: (0, 0)), (W['w_mod'], (None, D, 1536), lambda l, n: (l, 0, n)),
                    (W['b_mod'].reshape(DEPTH, 1, N_MOD * D), (None, 1, 1536), lambda l, n: (l, 0, n))],
                   [((DEPTH, 8, N_MOD * D), F32, (None, 8, 1536), lambda l, n: (l, 0, n))])[0]
    modr = mod.reshape(DEPTH, 8, N_MOD, 1, D)
    dmods = {}

    lb1, lb3 = tile_fwd('lb_fwd', f_lb, (1,), [vec(W['hg_lb_logits'])],
                        [((1, HG_W), F32, (1, HG_W), lambda i: (0, 0))] * 2)
    dlb = {1: [], 3: []}

    x0 = jnp.concatenate([ctx, x], axis=1)
    R = [dict() for _ in range(DEPTH)]

    xprev, oprev = x0, None
    for l in range(DEPTH):
        r = R[l]
        j = l // 2
        ng = W['norm_mix_g'][l][None]
        if l == 0:
            h1 = tile_fwd(f'nm0_fwd', f_nm0, (B, nt), [tok(xprev), vec(ng), modv(l, 0), modv(l, 1)], [tok_out(D, BF16)])[0]
            xa = xprev
        else:
            xa, h1 = tile_fwd(f'nm_mix_fwd{l}', f_nm, (B, nt),
                              [tok(xprev), tok(oprev), modv(l - 1, 5), vec(ng), modv(l, 0), modv(l, 1)],
                              [tok_out(D), tok_out(D, BF16)])
        r['xin'], r['oin'], r['xa'], r['h1'] = xprev, oprev, xa, h1
        h1m = h1.reshape(M, D)
        if l % 2 == 0:
            win = W['ev_w_in_p'][j]
            proj = mm(f'ev_proj{l}', [(h1m, win, 0)]).reshape(B, Tt, EV_PAD)
            r['proj'] = proj
            scw, scb = W['ssd_conv_w'][j], W['ssd_conv_b'][j][None]
            lcw, lcb = W['lru_conv_w'][j], W['lru_conv_b'][j][None]

            def conv_call(name, colblk0, w, b, wblk0, ncols, act):
                return tile_fwd(name, functools.partial(f_conv1d, lc=Lc, act=act), (ncols // 256, B),
                                [(proj, (None, Tt, 256), lambda cb, bi: (bi, 0, colblk0 + cb)),
                                 (w, (4, 256), lambda cb, bi: (0, wblk0 + cb)), (b, (1, 256), lambda cb, bi: (0, wblk0 + cb))],
                                [((B, Tt, ncols), F32, (None, Tt, 256), lambda cb, bi: (bi, 0, cb))])[0]
            xs_c = conv_call(f'conv_xs{l}', 12, scw, scb, 0, 1024, True)
            bc_c = conv_call(f'conv_bc{l}', 16, scw, scb, 4, 512, True)
            u_c = conv_call(f'conv_u{l}', 8, lcw, lcb, 0, 1024, False)
            r['xs'], r['bc'], r['u'] = xs_c, bc_c, u_c
            bias = _pad_rows(W['ssd_dt_bias'][j].reshape(-1), 1)
            alog = _pad_rows(W['ssd_a_log'][j].reshape(-1), 1)
            r['bias'], r['alog'] = bias, alog
            r['y'], r['st'], r['a4'], r['hp4'], r['h'] = [], [], [], [], []
            for d in range(2):
                ph = phys_chunk(nc, ncc, d == 1)
                y, st = scan_fwd(
                    f'ssd_fwd{l}_{d}', functools.partial(f_ssd, d=d, reverse=(d == 1)), (B, nc),
                    [(xs_c, (None, SSD_CHUNK, D), lambda b, s, ph=ph: (b, ph(s), 0)),
                     (bc_c, (None, SSD_CHUNK, 512), lambda b, s, ph=ph: (b, ph(s), 0)),
                     (proj, (None, SSD_CHUNK, 128), lambda b, s, ph=ph: (b, ph(s), 36)),
                     vec(bias), vec(alog)],
                    ((B, Tt, D), F32, (None, SSD_CHUNK, D), lambda b, s, ph=ph: (b, ph(s), 0)),
                    ((B, nc, 8, 128, 128), F32, (None, None, 8, 128, 128), lambda b, s: (b, s, 0, 0, 0)),
                    (8, 128, 128), lambda b, s: s == 0)
                r['y'].append(y)
                r['st'].append(st)
                a_d, bx_d = tile_fwd(
                    f'gates_fwd{l}_{d}', f_gates, (B, nt),
                    [tok(u_c), vec(W['lru_w_a'][j, d]), vec(W['lru_b_a'][j, d][None]), vec(W['lru_w_i'][j, d]),
                     vec(W['lru_b_i'][j, d][None]), vec(W['lru_lam'][j, d][None])],
                    [tok_out(D), tok_out(D)])
                a4 = a_d.reshape(B, Tt, 8, 128)
                h4, hp4 = linrec(f'lru_fwd{l}_{d}', a4, bx_d.reshape(B, Tt, 8, 128), 'F' if d == 0 else 'R')
                r['a4'].append(a4)
                r['hp4'].append(hp4)
                r['h'].append(h4.reshape(B, Tt, D))
            dpad = _pad_rows(W['ssd_d'][j])
            sng = W['ssd_norm_g'][j][None]
            r['dpad'], r['sng'] = dpad, sng
            mix = tile_fwd(f'ssdfin_fwd{l}', f_ssdfin, (B, nt),
                           [tok(r['y'][0]), tok(r['y'][1]), tok(xs_c), tok(proj, D, 0), tok(proj, D, 1), tok(r['h'][0]),
                            tok(r['h'][1]), vec(dpad), vec(sng)], [tok_out(2 * D, BF16)])[0]
            wout = W['ev_w_out'][j]
        else:
            win = W['od_w_in'][j]
            proj = mm(f'od_proj{l}', [(h1m, win, 0)]).reshape(B, Tt, 4096)
            r['proj'] = proj
            lbv = lb1 if l == 1 else lb3
            ns = nc
            r['o'], r['zst'], r['coef'], r['bcat'], r['ccat'], r['hp5'], r['hcat'], r['yd'], r['s5in'] = [], [], [], [], [], [], [], [], []
            u2 = proj[:, :, 3840:].reshape(M, S5_W)
            r['u2'] = u2
            for d in range(2):
                ph = phys_chunk(ns, ncc, d == 1)
                o_d, zst = scan_fwd(
                    f'hgrn_fwd{l}_{d}', functools.partial(f_hgrn, reverse=(d == 1)), (HG_HEADS, B, ns),
                    [(proj, (None, 128, 128), lambda h, b, s, ph=ph: (b, ph(s), h)),
                     (proj, (None, 128, 128), lambda h, b, s, ph=ph, d=d: (b, ph(s), 6 + 6 * d + h)),
                     (proj, (None, 128, 128), lambda h, b, s, ph=ph: (b, ph(s), 18 + h)),
                     (lbv, (1, 128), lambda h, b, s: (0, h))],
                    ((B, Tt, HG_W), F32, (None, 128, 128), lambda h, b, s, ph=ph: (b, ph(s), h)),
                    ((B, HG_HEADS, ns, 128, 128), F32, (None, None, None, 128, 128), lambda h, b, s: (b, h, s, 0, 0)),
                    (128, 128), lambda h, b, s: s == 0)
                r['o'].append(o_d)
                r['zst'].append(zst)
                s5in = [W['s5_lam_re'][j, d], W['s5_lam_im'][j, d], W['s5_log_step'][j, d].reshape(16, 1),
                        jnp.swapaxes(W['s5_b_re'][j], 1, 2), jnp.swapaxes(W['s5_b_im'][j], 1, 2)]
                r['s5in'].append(s5in)
                ar, ai, bbr, bbi = tile_fwd(f's5p_fwd{l}_{d}', f_s5p, (1,), [vec(t) for t in s5in],
                                            [((16, 64), F32, (16, 64), lambda i: (0, 0))] * 2
                                            + [((16, 16, 64), F32, (16, 16, 64), lambda i: (0, 0, 0))] * 2)
                coef = jnp.stack([ar.reshape(8, 128), ai.reshape(8, 128)])
                bcat = jnp.concatenate([_blockdiag(bbr), _blockdiag(bbi)], axis=1).astype(BF16)
                ccat = jnp.concatenate([_blockdiag_t(W['s5_c_re'][j, d]), -_blockdiag_t(W['s5_c_im'][j, d])], axis=0).astype(BF16)
                xcat = mm(f's5_in{l}_{d}', [(u2, bcat, 0)])
                h5, hp5 = clinrec(f's5_fwd{l}_{d}', xcat.reshape(B, Tt, 2, 8, 128), coef, 'F' if d == 0 else 'R')
                hcat = h5.reshape(M, 2 * D)
                yd = mm(f's5_out{l}_{d}', [(hcat, ccat, 0)]).reshape(B, Tt, S5_W)
                r['coef'].append(coef)
                r['bcat'].append(bcat)
                r['ccat'].append(ccat)
                r['hp5'].append(hp5)
                r['hcat'].append(hcat)
                r['yd'].append(yd)
            hn = jnp.zeros((8, 128), F32).at[:HG_HEADS].set(W['hg_norm_g'][j])
            sd, gw, gb = W['s5_d'][j][None], W['s5_glu_w'][j], W['s5_glu_b'][j][None]
            r['fin_par'] = (hn, sd, gw, gb)
            mix = tile_fwd(f'oddfin_fwd{l}', f_oddfin, (B, nt),
                           [tok(r['o'][0]), tok(r['o'][1]), tok(proj, HG_W, 4), tok(r['yd'][0]), tok(r['yd'][1]),
                            tok(proj, S5_W, 15), vec(hn), vec(sd), vec(gw), vec(gb)], [tok_out(D, BF16)])[0]
            wout = W['od_w_out'][j]
        r['mix'] = mix
        o1 = mm(f'mix_out{l}', [(mix.reshape(M, -1), wout, 0)]).reshape(B, Tt, D)
        r['o1'] = o1
        fg = W['norm_ffn_g'][l][None]
        xb, h2 = tile_fwd(f'nm_ffn_fwd{l}', f_nm, (B, nt), [tok(xa), tok(o1), modv(l, 2), vec(fg), modv(l, 3), modv(l, 4)],
                          [tok_out(D), tok_out(D, BF16)])
        r['h2'] = h2
        h2m = h2.reshape(M, D)
        a = mm(f'ffn_gate{l}', [(h2m, W['ffn_w_gate'][l], 0)]).reshape(B, Tt, D_FF)
        up = mm(f'ffn_up{l}', [(h2m, W['ffn_w_up'][l], 0)]).reshape(B, Tt, D_FF)
        w9 = W['ffn_conv_w'][l].reshape(9, D_FF)
        cbias = W['ffn_conv_b'][l][None]
        r['a'], r['up'], r['w9'], r['cbias'] = a, up, w9, cbias
        act = tile_fwd(f'ffnconv_fwd{l}', functools.partial(f_ffnconv, lc=Lc), (D_FF // 128, B),
                       [(a, (None, Tt, 128), lambda cb, bi: (bi, 0, cb)), (up, (None, Tt, 128), lambda cb, bi: (bi, 0, cb)),
                        (w9, (9, 128), lambda cb, bi: (0, cb)), (cbias, (1, 128), lambda cb, bi: (0, cb))],
                       [((B, Tt, D_FF), BF16, (None, Tt, 128), lambda cb, bi: (bi, 0, cb))])[0]
        r['act'] = act
        o2 = mm(f'ffn_down{l}', [(act.reshape(M, D_FF), W['ffn_w_down'][l], 0)]).reshape(B, Tt, D)
        xprev, oprev = xb, o2

    vmask = jnp.ones((nt, 1, D), F32).at[0].set(0.0)
    ones = jnp.ones((1, D), F32)
    fng = W['final_norm_g'][None]
    d_xp, d_o2, dg5, dfng, loss_vec = tile_bwd(
        'loss_head', f_final, (B, nt),
        [tok(xprev), tok(oprev), modv(DEPTH - 1, 5), vec(fng),
         (target, (None, tb, D), lambda b, j: (b, jnp.maximum(j - 1, 0), 0)), (vmask, (None, 1, D), lambda b, j: (j, 0, 0))],
        [[vec(ones)]],
        [(0,) + tok_out(D) + (None,), (1,) + tok_out(D) + (None,), (2,) + dmod_spec, (3,) + vec_acc((1, D))],
        prims=[(0,) + vec_acc((1, D))])
    loss = jnp.sum(loss_vec)
    add_grad('final_norm_g', None, dfng[0])
    dmods[(DEPTH - 1, 5)] = dg5

    for l in reversed(range(DEPTH)):
        r = R[l]
        j = l // 2
        d_o2m = d_o2.reshape(M, D)
        d_act = mm(f'ffn_down_dx{l}', [(d_o2m, W['ffn_w_down'][l], 0)], tb=True).reshape(B, Tt, D_FF)
        add_grad('ffn_w_down', l, mm(f'ffn_down_dw{l}', [(r['act'].reshape(M, D_FF), d_o2m, 0)], ta=True))
        d_a, d_up, dw9, dcb = tile_bwd(
            f'ffnconv_bwd{l}', functools.partial(f_ffnconv, lc=Lc), (D_FF // 128, B),
            [(r['a'], (None, Tt, 128), lambda cb, bi: (bi, 0, cb)), (r['up'], (None, Tt, 128), lambda cb, bi: (bi, 0, cb)),
             (r['w9'], (9, 128), lambda cb, bi: (0, cb)), (r['cbias'], (1, 128), lambda cb, bi: (0, cb))],
            [[(d_act, (None, Tt, 128), lambda cb, bi: (bi, 0, cb))]],
            [(0, (B, Tt, D_FF), F32, (None, Tt, 128), lambda cb, bi: (bi, 0, cb), None),
             (1, (B, Tt, D_FF), F32, (None, Tt, 128), lambda cb, bi: (bi, 0, cb), None),
             (2, (9, D_FF), F32, (9, 128), lambda cb, bi: (0, cb), lambda cb, bi: bi == 0),
             (3, (1, D_FF), F32, (1, 128), lambda cb, bi: (0, cb), lambda cb, bi: bi == 0)])
        add_grad('ffn_conv_w', l, dw9.reshape(3, 3, D_FF))
        add_grad('ffn_conv_b', l, dcb[0])
        d_am, d_upm = d_a.reshape(M, D_FF), d_up.reshape(M, D_FF)
        h2m = r['h2'].reshape(M, D)
        d_h2 = mm(f'ffn_in_dx{l}', [(d_am, W['ffn_w_gate'][l], 0), (d_upm, W['ffn_w_up'][l], 0)], tb=True).reshape(B, Tt, D)
        add_grad('ffn_w_gate', l, mm(f'ffn_gate_dw{l}', [(h2m, d_am, 0)], ta=True))
        add_grad('ffn_w_up', l, mm(f'ffn_up_dw{l}', [(h2m, d_upm, 0)], ta=True))
        fg = W['norm_ffn_g'][l][None]
        d_xa, d_o1, dgate, dfg, dsh, dsc = tile_bwd(
            f'nm_ffn_bwd{l}', f_nm, (B, nt), [tok(r['xa']), tok(r['o1']), modv(l, 2), vec(fg), modv(l, 3), modv(l, 4)],
            [[tok(d_xp)], [tok(d_h2)]],
            [(0,) + tok_out(D) + (None,), (1,) + tok_out(D) + (None,), (2,) + dmod_spec, (3,) + vec_acc((1, D)),
             (4,) + dmod_spec, (5,) + dmod_spec])
        add_grad('norm_ffn_g', l, dfg[0])
        dmods[(l, 2)], dmods[(l, 3)], dmods[(l, 4)] = dgate, dsh, dsc
        d_o1m = d_o1.reshape(M, D)
        h1m = r['h1'].reshape(M, D)
        proj = r['proj']
        if l % 2 == 0:
            wout, win = W['ev_w_out'][j], W['ev_w_in_p'][j]
            d_mix = mm(f'mix_out_dx{l}', [(d_o1m, wout, 0)], tb=True).reshape(B, Tt, 2 * D)
            add_grad('ev_w_out', j, mm(f'mix_out_dw{l}', [(r['mix'].reshape(M, 2 * D), d_o1m, 0)], ta=True))
            d_y, d_xs_fin, d_z, d_gy, d_h, ddpad, dsng = tile_bwd(
                f'ssdfin_bwd{l}', f_ssdfin, (B, nt),
                [tok(r['y'][0]), tok(r['y'][1]), tok(r['xs']), tok(proj, D, 0), tok(proj, D, 1), tok(r['h'][0]), tok(r['h'][1]),
                 vec(r['dpad']), vec(r['sng'])],
                [[tok(d_mix)]],
                [(0,) + tok_out(D) + (None,), (2,) + tok_out(D) + (None,), (3,) + tok_out(D) + (None,), (4,) + tok_out(D) + (None,),
                 (5,) + tok_out(D) + (None,), (7,) + vec_acc((8, 128)), (8,) + vec_acc((1, D))])
            add_grad('ssd_d', j, ddpad[0, :SSD_HEADS])
            add_grad('ssd_norm_g', j, dsng[0])
            d_xs_parts, d_bc_parts, d_dt_parts, d_u_parts = [d_xs_fin], [], [], []
            dbias_t, dalog_t = [], []
            dh4 = d_h.reshape(B, Tt, 8, 128)
            for d in range(2):
                ph0 = phys_chunk(nc, ncc, d == 1)

                def ph(s, ph0=ph0):
                    return ph0(nc - 1 - s)
                dxs_d, dbc_d, ddt_d, dbias, dalog = scan_bwd(
                    f'ssd_bwd{l}_{d}', functools.partial(f_ssd, d=d, reverse=(d == 1)), (B, nc),
                    [(r['xs'], (None, SSD_CHUNK, D), lambda b, s, ph=ph: (b, ph(s), 0)),
                     (r['bc'], (None, SSD_CHUNK, 512), lambda b, s, ph=ph: (b, ph(s), 0)),
                     (proj, (None, SSD_CHUNK, 128), lambda b, s, ph=ph: (b, ph(s), 36)),
                     vec(r['bias']), vec(r['alog'])],
                    (r['st'][d], (None, None, 8, 128, 128), lambda b, s: (b, nc - 1 - s, 0, 0, 0)),
                    (d_y, (None, SSD_CHUNK, D), lambda b, s, ph=ph: (b, ph(s), 0)),
                    [(0, (B, Tt, D), F32, (None, SSD_CHUNK, D), lambda b, s, ph=ph: (b, ph(s), 0), None),
                     (1, (B, Tt, 512), F32, (None, SSD_CHUNK, 512), lambda b, s, ph=ph: (b, ph(s), 0), None),
                     (2, (B, Tt, 128), F32, (None, SSD_CHUNK, 128), lambda b, s, ph=ph: (b, ph(s), 0), None),
                     (3,) + vec_acc((1, 128)), (4,) + vec_acc((1, 128))],
                    (8, 128, 128), lambda b, s: s == 0)
                d_xs_parts.append(dxs_d)
                d_bc_parts.append(dbc_d)
                d_dt_parts.append(ddt_d)
                dbias_t.append(dbias)
                dalog_t.append(dalog)
                g4, ga4 = linrec_bwd(f'lru_bwd{l}_{d}', r['a4'][d], dh4, r['hp4'][d], 'Fb' if d == 0 else 'Rb')
                du_g, dwa, dba, dwi, dbi, dlam = tile_bwd(
                    f'gates_bwd{l}_{d}', f_gates, (B, nt),
                    [tok(r['u']), vec(W['lru_w_a'][j, d]), vec(W['lru_b_a'][j, d][None]), vec(W['lru_w_i'][j, d]),
                     vec(W['lru_b_i'][j, d][None]), vec(W['lru_lam'][j, d][None])],
                    [[tok(ga4.reshape(B, Tt, D))], [tok(g4.reshape(B, Tt, D))]],
                    [(0,) + tok_out(D) + (None,), (1,) + vec_acc((8, 128, 128)), (2,) + vec_acc((1, D)), (3,) + vec_acc((8, 128, 128)),
                     (4,) + vec_acc((1, D)), (5,) + vec_acc((1, D))])
                d_u_parts.append(du_g)
                add_grad('lru_w_a', (j, d), dwa)
                add_grad('lru_b_a', (j, d), dba[0])
                add_grad('lru_w_i', (j, d), dwi)
                add_grad('lru_b_i', (j, d), dbi[0])
                add_grad('lru_lam', (j, d), dlam[0])
            add_grad('ssd_dt_bias', j, (dbias_t[0] + dbias_t[1])[0, :32].reshape(2, SSD_HEADS))
            add_grad('ssd_a_log', j, (dalog_t[0] + dalog_t[1])[0, :32].reshape(2, SSD_HEADS))
            scw, scb = W['ssd_conv_w'][j], W['ssd_conv_b'][j][None]
            lcw, lcb = W['lru_conv_w'][j], W['lru_conv_b'][j][None]

            def conv_bwd(name, colblk0, w, b, wblk0, ncols, act, parts):
                return tile_bwd(
                    name, functools.partial(f_conv1d, lc=Lc, act=act), (ncols // 256, B),
                    [(proj, (None, Tt, 256), lambda cb, bi: (bi, 0, colblk0 + cb)),
                     (w, (4, 256), lambda cb, bi: (0, wblk0 + cb)), (b, (1, 256), lambda cb, bi: (0, wblk0 + cb))],
                    [[(p, (None, Tt, 256), lambda cb, bi: (bi, 0, cb)) for p in parts]],
                    [(0, (B, Tt, ncols), F32, (None, Tt, 256), lambda cb, bi: (bi, 0, cb), None),
                     (1, (4, ncols), F32, (4, 256), lambda cb, bi: (0, cb), lambda cb, bi: bi == 0),
                     (2, (1, ncols), F32, (1, 256), lambda cb, bi: (0, cb), lambda cb, bi: bi == 0)])
            d_xs_raw, dw_xs, db_xs = conv_bwd(f'conv_xs_bwd{l}', 12, scw, scb, 0, 1024, True, d_xs_parts)
            d_bc_raw, dw_bc, db_bc = conv_bwd(f'conv_bc_bwd{l}', 16, scw, scb, 4, 512, True, d_bc_parts)
            d_u_raw, dw_u, db_u = conv_bwd(f'conv_u_bwd{l}', 8, lcw, lcb, 0, 1024, False, d_u_parts)
            add_grad('ssd_conv_w', j, jnp.concatenate([dw_xs, dw_bc], axis=1))
            add_grad('ssd_conv_b', j, jnp.concatenate([db_xs, db_bc], axis=1)[0])
            add_grad('lru_conv_w', j, dw_u)
            add_grad('lru_conv_b', j, db_u[0])
            segs = [(d_z.reshape(M, D), 0), (d_gy.reshape(M, D), 1024), (d_u_raw.reshape(M, D), 2048), (d_xs_raw.reshape(M, D), 3072),
                    (d_bc_raw.reshape(M, 512), 4096), (d_dt_parts[0].reshape(M, 128), 4608), (d_dt_parts[1].reshape(M, 128), 4608)]
            d_h1 = mm(f'ev_proj_dx{l}', [(a_, win, off) for a_, off in segs], tb=True).reshape(B, Tt, D)
            dws = [mm(f'ev_proj_dw{l}_{i}', [(h1m, a_, 0)], ta=True) for i, (a_, _) in enumerate(segs[:5])]
            dw_dt = mm(f'ev_proj_dw{l}_dt', [(h1m, segs[5][0], 0), (h1m, segs[6][0], 0)], ta=True)
            dz_, dgy_, du_, dxs_, dbc_ = dws
            add_grad('ev_w_in', j, jnp.concatenate([dz_, dxs_, dbc_, dw_dt[:, :32], dgy_, du_], axis=1))
        else:
            wout, win = W['od_w_out'][j], W['od_w_in'][j]
            d_mix = mm(f'mix_out_dx{l}', [(d_o1m, wout, 0)], tb=True).reshape(B, Tt, D)
            add_grad('od_w_out', j, mm(f'mix_out_dw{l}', [(r['mix'].reshape(M, D), d_o1m, 0)], ta=True))
            hn, sd, gw, gb = r['fin_par']
            d_o, d_g, d_yv, d_u_fin, dhn, dsd, dgw, dgb = tile_bwd(
                f'oddfin_bwd{l}', f_oddfin, (B, nt),
                [tok(r['o'][0]), tok(r['o'][1]), tok(proj, HG_W, 4), tok(r['yd'][0]), tok(r['yd'][1]), tok(proj, S5_W, 15),
                 vec(hn), vec(sd), vec(gw), vec(gb)],
                [[tok(d_mix)]],
                [(0,) + tok_out(HG_W) + (None,), (2,) + tok_out(HG_W) + (None,), (3,) + tok_out(S5_W) + (None,),
                 (5,) + tok_out(S5_W) + (None,), (6,) + vec_acc((8, 128)), (7,) + vec_acc((1, S5_W)), (8,) + vec_acc((S5_W, S5_W)),
                 (9,) + vec_acc((1, S5_W))])
            add_grad('hg_norm_g', j, dhn[:HG_HEADS])
            add_grad('s5_d', j, dsd[0])
            add_grad('s5_glu_w', j, dgw)
            add_grad('s5_glu_b', j, dgb[0])
            lbv = lb1 if l == 1 else lb3
            ns = nc
            dq, df, dv, du_s5 = [], [], [], []
            d_ym = d_yv.reshape(M, S5_W)
            dbt_re, dbt_im = [], []
            for d in range(2):
                ph0 = phys_chunk(ns, ncc, d == 1)

                def ph(s, ph0=ph0):
                    return ph0(ns - 1 - s)
                dq_d, df_d, dv_d, dlb_d = scan_bwd(
                    f'hgrn_bwd{l}_{d}', functools.partial(f_hgrn, reverse=(d == 1)), (HG_HEADS, B, ns),
                    [(proj, (None, 128, 128), lambda h, b, s, ph=ph: (b, ph(s), h)),
                     (proj, (None, 128, 128), lambda h, b, s, ph=ph, d=d: (b, ph(s), 6 + 6 * d + h)),
                     (proj, (None, 128, 128), lambda h, b, s, ph=ph: (b, ph(s), 18 + h)),
                     (lbv, (1, 128), lambda h, b, s: (0, h))],
                    (r['zst'][d], (None, None, None, 128, 128), lambda h, b, s: (b, h, ns - 1 - s, 0, 0)),
                    (d_o, (None, 128, 128), lambda h, b, s, ph=ph: (b, ph(s), h)),
                    [(0, (B, Tt, HG_W), F32, (None, 128, 128), lambda h, b, s, ph=ph: (b, ph(s), h), None),
                     (1, (B, Tt, HG_W), F32, (None, 128, 128), lambda h, b, s, ph=ph: (b, ph(s), h), None),
                     (2, (B, Tt, HG_W), F32, (None, 128, 128), lambda h, b, s, ph=ph: (b, ph(s), h), None),
                     (3, (1, HG_W), F32, (1, 128), lambda h, b, s: (0, h), lambda h, b, s: (b == 0) & (s == 0))],
                    (128, 128), lambda h, b, s: s == 0)
                dq.append(dq_d.reshape(M, HG_W))
                df.append(df_d.reshape(M, HG_W))
                dv.append(dv_d.reshape(M, HG_W))
                dlb[l].append(dlb_d)
                d_hcat = mm(f's5_out_dx{l}_{d}', [(d_ym, r['ccat'][d], 0)], tb=True)
                dccat = mm(f's5_out_dw{l}_{d}', [(r['hcat'][d], d_ym, 0)], ta=True)
                add_grad('s5_c_re', (j, d), _unblockdiag_t(dccat[:D]))
                add_grad('s5_c_im', (j, d), -_unblockdiag_t(dccat[D:]))
                g5, dcoef = clinrec_bwd(f's5_bwd{l}_{d}', d_hcat.reshape(B, Tt, 2, 8, 128), r['hp5'][d], r['coef'][d],
                                        'Fb' if d == 0 else 'Rb')
                gcat = g5.reshape(M, 2 * D)
                dbcat = mm(f's5_in_dw{l}_{d}', [(r['u2'], gcat, 0)], ta=True)
                du_s5.append(mm(f's5_in_dx{l}_{d}', [(gcat, r['bcat'][d], 0)], tb=True))
                cts5 = [dcoef[0].reshape(16, 64), dcoef[1].reshape(16, 64), _unblockdiag(dbcat[:, :D]), _unblockdiag(dbcat[:, D:])]
                dlre, dlim, dlst, dbtr, dbti = tile_bwd(
                    f's5p_bwd{l}_{d}', f_s5p, (1,), [vec(t) for t in r['s5in'][d]], [[vec(t)] for t in cts5],
                    [(i, t.shape, F32, t.shape, (lambda *ids, n=t.ndim: (0,) * n), None) for i, t in enumerate(r['s5in'][d])])
                add_grad('s5_lam_re', (j, d), dlre)
                add_grad('s5_lam_im', (j, d), dlim)
                add_grad('s5_log_step', (j, d), dlst[:, 0])
                dbt_re.append(dbtr)
                dbt_im.append(dbti)
            add_grad('s5_b_re', j, jnp.swapaxes(dbt_re[0] + dbt_re[1], 1, 2))
            add_grad('s5_b_im', j, jnp.swapaxes(dbt_im[0] + dbt_im[1], 1, 2))
            d_gm, d_ufm = d_g.reshape(M, HG_W), d_u_fin.reshape(M, S5_W)
            segs = [(dq[0], 0), (dq[1], 0), (df[0], 768), (df[1], 1536), (dv[0], 2304), (dv[1], 2304), (d_gm, 3072),
                    (d_ufm, 3840), (du_s5[0], 3840), (du_s5[1], 3840)]
            d_h1 = mm(f'od_proj_dx{l}', [(a_, win, off) for a_, off in segs], tb=True).reshape(B, Tt, D)
            dwq = mm(f'od_proj_dw{l}_q', [(h1m, dq[0], 0), (h1m, dq[1], 0)], ta=True)
            dwf0 = mm(f'od_proj_dw{l}_f0', [(h1m, df[0], 0)], ta=True)
            dwf1 = mm(f'od_proj_dw{l}_f1', [(h1m, df[1], 0)], ta=True)
            dwv = mm(f'od_proj_dw{l}_v', [(h1m, dv[0], 0), (h1m, dv[1], 0)], ta=True)
            dwg = mm(f'od_proj_dw{l}_g', [(h1m, d_gm, 0)], ta=True)
            dwu = mm(f'od_proj_dw{l}_u', [(h1m, d_ufm, 0), (h1m, du_s5[0], 0), (h1m, du_s5[1], 0)], ta=True)
            add_grad('od_w_in', j, jnp.concatenate([dwq, dwf0, dwf1, dwv, dwg, dwu], axis=1))
        ng = W['norm_mix_g'][l][None]
        if l == 0:
            d_x0, dng, dsh, dsc = tile_bwd(
                'nm0_bwd', lambda xv, g, sh, sc: (xv, f_nm0(xv, g, sh, sc)), (B, nt),
                [tok(r['xin']), vec(ng), modv(l, 0), modv(l, 1)], [[tok(d_xa)], [tok(d_h1)]],
                [(0,) + tok_out(D) + (None,), (1,) + vec_acc((1, D)), (2,) + dmod_spec, (3,) + dmod_spec])
        else:
            d_xp, d_o2, dgate, dng, dsh, dsc = tile_bwd(
                f'nm_mix_bwd{l}', f_nm, (B, nt),
                [tok(r['xin']), tok(r['oin']), modv(l - 1, 5), vec(ng), modv(l, 0), modv(l, 1)],
                [[tok(d_xa)], [tok(d_h1)]],
                [(0,) + tok_out(D) + (None,), (1,) + tok_out(D) + (None,), (2,) + dmod_spec, (3,) + vec_acc((1, D)),
                 (4,) + dmod_spec, (5,) + dmod_spec])
            dmods[(l - 1, 5)] = dgate
        add_grad('norm_mix_g', l, dng[0])
        dmods[(l, 0)], dmods[(l, 1)] = dsh, dsc

    grad_x = d_x0[:, Lc:, :]

    (dlogits,) = tile_bwd('lb_bwd', f_lb, (1,), [vec(W['hg_lb_logits'])],
                          [[vec(t) for t in dlb[1]], [vec(t) for t in dlb[3]]],
                          [(0, (DEPTH, HG_W), F32, (DEPTH, HG_W), lambda i: (0, 0), None)])
    add_grad('hg_lb_logits', None, dlogits)

    dm = jnp.stack([jnp.stack([dmods[(l, w)] for w in range(N_MOD)]) for l in range(DEPTH)])
    dlat = jnp.transpose(dm[:, :, :, 1, 0, :], (0, 2, 1, 3)).reshape(DEPTH, B, N_MOD * D)
    dctx = jnp.transpose(dm[:, :, :, 0, 0, :], (0, 2, 1, 3)).reshape(DEPTH, B, N_MOD * D)
    dlat = jnp.zeros((DEPTH, 8, N_MOD * D), F32).at[:, :B].set(dlat)
    dctx = jnp.zeros((DEPTH, 8, N_MOD * D), F32).at[:, :B].set(dctx)

    def mod_bwd_body(cc_ref, w_ref, dl_ref, dc_ref, dw_ref, db_ref, dcc_ref):
        row = lax.broadcasted_iota(jnp.int32, (8, 1), 0)
        dall = dl_ref[...] + jnp.where(row == 4, jnp.sum(dc_ref[...], axis=0, keepdims=True), 0.0)
        s, vjp = jax.vjp(jax.nn.silu, cc_ref[...])
        db16 = dall.astype(BF16)
        dw_ref[...] = lax.dot_general(s.astype(BF16), db16, (((0,), (0,)), ((), ())), preferred_element_type=F32)
        db_ref[...] = jnp.sum(dall, axis=0, keepdims=True)
        ds = lax.dot_general(db16, w_ref[...], (((1,), (1,)), ((), ())), preferred_element_type=F32)
        (dcc,) = vjp(ds)
        first = (pl.program_id(0) == 0) & (pl.program_id(1) == 0)

        @pl.when(first)
        def _():
            dcc_ref[...] = dcc

        @pl.when(jnp.logical_not(first))
        def _():
            dcc_ref[...] += dcc

    dwmod, dbmod, dcc = pl.pallas_call(
        mod_bwd_body, grid=(DEPTH, nmc),
        in_specs=[pl.BlockSpec((8, D), lambda l, n: (0, 0)), pl.BlockSpec((None, D, 1536), lambda l, n: (l, 0, n)),
                  pl.BlockSpec((None, 8, 1536), lambda l, n: (l, 0, n)), pl.BlockSpec((None, 8, 1536), lambda l, n: (l, 0, n))],
        out_specs=[pl.BlockSpec((None, D, 1536), lambda l, n: (l, 0, n)), pl.BlockSpec((None, 1, 1536), lambda l, n: (l, 0, n)),
                   pl.BlockSpec((8, D), lambda l, n: (0, 0))],
        out_shape=[jax.ShapeDtypeStruct((DEPTH, D, N_MOD * D), F32), jax.ShapeDtypeStruct((DEPTH, 1, N_MOD * D), F32),
                   jax.ShapeDtypeStruct((8, D), F32)],
        name='mod_bwd', compiler_params=_cparams(("arbitrary", "arbitrary")),
    )(cc, W['w_mod'], dlat, dctx)
    add_grad('w_mod', None, dwmod)
    add_grad('b_mod', None, dbmod[:, 0])
    add_grad('c_ctx', None, dcc[4])
    return loss, grad_x, G


def assemble_grads(G, like):
    out = {}
    for name, parts in G.items():
        shape = like[name].shape
        if None in parts:
            g = parts[None]
        elif isinstance(next(iter(parts)), tuple):
            g = jnp.stack([jnp.stack([parts[(j, d)] for d in range(2)]) for j in range(shape[0])])
        else:
            g = jnp.stack([parts[i] for i in range(shape[0])])
        out[name] = g.reshape(shape)
    return out


XY_RELS = ((1, 0, 0), (0, 1, 0), (1, 1, 0))
ALL_RELS = tuple((dx, dy, dc) for dx in (0, 1) for dy in (0, 1) for dc in (0, 1))[1:]


def exchange(name, src, out_shape, sends, local=None):
    n = len(sends)

    def body(src_ref, out_ref, send_sems, recv_sems, local_sem):
        me = (lax.axis_index("x"), lax.axis_index("y"), lax.axis_index("c"))

        def pick(ref, sel, tgt):
            return ref if sel is None else ref.at[sel(me, tgt)]

        copies = []
        for k, (rel, ssel, dsel) in enumerate(sends):
            tgt = tuple(1 - m if f else m for m, f in zip(me, rel))
            cp = pltpu.make_async_remote_copy(
                src_ref=pick(src_ref, ssel, tgt), dst_ref=pick(out_ref, dsel, tgt),
                send_sem=send_sems.at[k], recv_sem=recv_sems.at[k], device_id=tgt, device_id_type=MESH)
            cp.start()
            copies.append(cp)
        if local is not None:
            lc = pltpu.make_async_copy(pick(src_ref, local[0], me), pick(out_ref, local[1], me), local_sem)
            lc.start()
        for cp in copies:
            cp.wait()
        if local is not None:
            lc.wait()

    return pl.pallas_call(
        body, out_shape=jax.ShapeDtypeStruct(out_shape, src.dtype),
        in_specs=[pl.BlockSpec(memory_space=pl.ANY)], out_specs=pl.BlockSpec(memory_space=pl.ANY),
        scratch_shapes=[pltpu.SemaphoreType.DMA((n,)), pltpu.SemaphoreType.DMA((n,)), pltpu.SemaphoreType.DMA(())],
        name=name,
    )(src)


def _xy_index(dev):
    return 2 * dev[0] + dev[1]


def all_gather_xy(name, shard):
    return exchange(name, shard, (4,) + shard.shape,
                    [(rel, None, lambda me, tgt: _xy_index(me)) for rel in XY_RELS],
                    local=(None, lambda me, tgt: _xy_index(me)))


def reduce_scatter_xy(name, g4):
    return exchange(name, g4, g4.shape,
                    [(rel, (lambda me, tgt: _xy_index(tgt)), (lambda me, tgt, k=k: k)) for k, rel in enumerate(XY_RELS)],
                    local=((lambda me, tgt: _xy_index(me)), (lambda me, tgt: 3)))


def sibling_swap(name, v):
    return exchange(name, v, v.shape, [((0, 0, 1), None, None)])


def all_gather_all(name, v):
    idx = lambda me, tgt: 4 * me[0] + 2 * me[1] + me[2]
    return exchange(name, v, (8,) + v.shape, [(rel, None, idx) for rel in ALL_RELS], local=(None, idx))


def _rows_view(shape):
    cols = shape[-1] if len(shape) else 1
    rows = 1
    for s in shape[:-1]:
        rows *= s
    return rows, cols


def _row_block(rows, cols, n_arrays):
    budget = (24 * 1024 * 1024) // (8 * n_arrays * cols)
    if rows <= max(budget, 8):
        return rows
    br = (min(budget, rows) // 8) * 8
    while br > 8 and rows % br:
        br -= 8
    return br if rows % br == 0 else rows


def sum_slots(name, stacked):
    k = stacked.shape[0]
    rows, cols = _rows_view(stacked.shape[1:])
    br = _row_block(rows, cols, k + 1)

    def f(s):
        parts = [s[i] for i in range(k)]
        while len(parts) > 1:
            parts = [parts[i] + parts[i + 1] for i in range(0, len(parts), 2)]
        return parts[0]

    out = tile_fwd(name, f, (rows // br,), [(stacked.reshape(k, rows, cols), (k, br, cols), lambda i: (0, i, 0))],
                   [((rows, cols), F32, (br, cols), lambda i: (i, 0))])[0]
    return out.reshape(stacked.shape[1:])


def adamw(name, w, m, v, gs):
    rows, cols = _rows_view(w.shape)
    br = _row_block(rows, cols, 7 + len(gs))
    spec = lambda a: (a.reshape(rows, cols), (br, cols), lambda i: (i, 0))
    outs = tile_fwd(name, f_adamw, (rows // br,), [spec(t) for t in (w, m, v) + tuple(gs)],
                    [((rows, cols), F32, (br, cols), lambda i: (i, 0))] * 4)
    return [o.reshape(w.shape) for o in outs]


IN_NAMES = ['x', 'c', 'ctx'] + W_NAMES + ['loss_target'] + ['m_' + n for n in W_NAMES] + ['v_' + n for n in W_NAMES]
SMALL_PAD = 128 * 1024


def kernel(x, c, ctx, c_ctx, w_mod, b_mod, norm_mix_g, norm_ffn_g, final_norm_g, ev_w_in, ev_w_out, ssd_conv_w, ssd_conv_b, ssd_dt_bias, ssd_a_log, ssd_d, ssd_norm_g, lru_conv_w, lru_conv_b, lru_w_a, lru_b_a, lru_w_i, lru_b_i, lru_lam, od_w_in, od_w_out, hg_lb_logits, hg_norm_g, s5_lam_re, s5_lam_im, s5_log_step, s5_b_re, s5_b_im, s5_c_re, s5_c_im, s5_d, s5_glu_w, s5_glu_b, ffn_w_gate, ffn_w_up, ffn_conv_w, ffn_conv_b, ffn_w_down, loss_target, m_c_ctx, m_w_mod, m_b_mod, m_norm_mix_g, m_norm_ffn_g, m_final_norm_g, m_ev_w_in, m_ev_w_out, m_ssd_conv_w, m_ssd_conv_b, m_ssd_dt_bias, m_ssd_a_log, m_ssd_d, m_ssd_norm_g, m_lru_conv_w, m_lru_conv_b, m_lru_w_a, m_lru_b_a, m_lru_w_i, m_lru_b_i, m_lru_lam, m_od_w_in, m_od_w_out, m_hg_lb_logits, m_hg_norm_g, m_s5_lam_re, m_s5_lam_im, m_s5_log_step, m_s5_b_re, m_s5_b_im, m_s5_c_re, m_s5_c_im, m_s5_d, m_s5_glu_w, m_s5_glu_b, m_ffn_w_gate, m_ffn_w_up, m_ffn_conv_w, m_ffn_conv_b, m_ffn_w_down, v_c_ctx, v_w_mod, v_b_mod, v_norm_mix_g, v_norm_ffn_g, v_final_norm_g, v_ev_w_in, v_ev_w_out, v_ssd_conv_w, v_ssd_conv_b, v_ssd_dt_bias, v_ssd_a_log, v_ssd_d, v_ssd_norm_g, v_lru_conv_w, v_lru_conv_b, v_lru_w_a, v_lru_b_a, v_lru_w_i, v_lru_b_i, v_lru_lam, v_od_w_in, v_od_w_out, v_hg_lb_logits, v_hg_norm_g, v_s5_lam_re, v_s5_lam_im, v_s5_log_step, v_s5_b_re, v_s5_b_im, v_s5_c_re, v_s5_c_im, v_s5_d, v_s5_glu_w, v_s5_glu_b, v_ffn_w_gate, v_ffn_w_up, v_ffn_conv_w, v_ffn_conv_b, v_ffn_w_down):
    a = dict(locals())
    W = {}
    for n in W_NAMES:
        w = a[n]
        if n in SHARD_AXIS:
            ax = SHARD_AXIS[n]
            g4 = all_gather_xy('ag_' + n, w.astype(BF16) if n in MATMUL_WEIGHTS else w)
            shape = list(w.shape)
            shape[ax] *= 4
            W[n] = jnp.moveaxis(g4, 0, ax).reshape(shape)
        else:
            W[n] = w
    e = W['ev_w_in']
    W['ev_w_in_p'] = jnp.concatenate(
        [e[:, :, 0:1024], e[:, :, 2592:3616], e[:, :, 3616:4640], e[:, :, 1024:2560], e[:, :, 2560:2592],
         jnp.zeros((e.shape[0], D, EV_PAD - 4640), e.dtype)], axis=2)

    loss_local, grad_x, G = local_step(a['x'], a['c'], a['ctx'], W['c_ctx'], a['loss_target'], W)
    grads = assemble_grads(G, W)
    loss = lax.psum(loss_local, ("x", "y", "c"))

    res = {}
    for n in W_NAMES:
        if n not in SHARD_AXIS:
            continue
        ax = SHARD_AXIS[n]
        w = a[n]
        gf = grads[n]
        split = gf.reshape(gf.shape[:ax] + (4, w.shape[ax]) + gf.shape[ax + 1:])
        g4 = jnp.moveaxis(split, ax, 0)
        mine = sum_slots('gsum_' + n, reduce_scatter_xy('rs_' + n, g4))
        other = sibling_swap('sw_' + n, mine)
        res[n] = adamw('adamw_' + n, w, a['m_' + n], a['v_' + n], (mine, other))
    small = [n for n in W_NAMES if n not in SHARD_AXIS]
    flat = jnp.concatenate([grads[n].reshape(-1) for n in small])
    total = flat.shape[0]
    padded = -(-total // SMALL_PAD) * SMALL_PAD
    flat = jnp.concatenate([flat, jnp.zeros((padded - total,), F32)]).reshape(padded // 128, 128)
    summed = sum_slots('gsum_small', all_gather_all('ag_small', flat)).reshape(-1)
    off = 0
    for n in small:
        size = math.prod(a[n].shape)
        g = summed[off:off + size].reshape(a[n].shape)
        off += size
        res[n] = adamw('adamw_' + n, a[n], a['m_' + n], a['v_' + n], (g,))
    outs = [loss, grad_x]
    for k in range(4):
        outs += [res[n][k] for n in W_NAMES]
    return tuple(outs)
```

```python
import functools
import math

import jax
import jax.numpy as jnp
from jax import lax
from jax.experimental import pallas as pl
from jax.experimental.pallas import tpu as pltpu

F32 = jnp.float32
BF16 = jnp.bfloat16
HI = lax.Precision.HIGHEST
MESH = pl.DeviceIdType.MESH

D = 1024
DEPTH = 4
N_MOD = 6
RMS_EPS = 1e-6
GRID_W = 64
SSD_HEADS = 16
SSD_CHUNK = 128
HG_W = 768
HG_HEADS = 6
HG_CHUNK = 16
S5_W = 256
D_FF = 2816
EV_PAD = 5120
LRU_C = 8.0
V7X_VMEM_LIMIT = 56 * 1024 * 1024
MM_VMEM_BUDGET = 36 * 1024 * 1024

ADAM_LR, ADAM_B1, ADAM_B2, ADAM_EPS, ADAM_WD, ADAM_STEP = 0.001, 0.9, 0.999, 1e-08, 0.01, 10

W_NAMES = ['c_ctx', 'w_mod', 'b_mod', 'norm_mix_g', 'norm_ffn_g', 'final_norm_g', 'ev_w_in', 'ev_w_out', 'ssd_conv_w',
           'ssd_conv_b', 'ssd_dt_bias', 'ssd_a_log', 'ssd_d', 'ssd_norm_g', 'lru_conv_w', 'lru_conv_b', 'lru_w_a', 'lru_b_a',
           'lru_w_i', 'lru_b_i', 'lru_lam', 'od_w_in', 'od_w_out', 'hg_lb_logits', 'hg_norm_g', 's5_lam_re', 's5_lam_im',
           's5_log_step', 's5_b_re', 's5_b_im', 's5_c_re', 's5_c_im', 's5_d', 's5_glu_w', 's5_glu_b', 'ffn_w_gate', 'ffn_w_up',
           'ffn_conv_w', 'ffn_conv_b', 'ffn_w_down']
SHARD_AXIS = {'w_mod': 2, 'ev_w_in': 2, 'ev_w_out': 1, 'ssd_conv_w': 2, 'lru_conv_w': 2, 'lru_b_a': 2, 'lru_b_i': 2,
              'lru_lam': 2, 'od_w_in': 2, 'od_w_out': 1, 's5_d': 1, 's5_glu_w': 1, 's5_glu_b': 1, 'ffn_w_gate': 2,
              'ffn_w_up': 2, 'ffn_conv_w': 3, 'ffn_w_down': 1}
MATMUL_WEIGHTS = ('w_mod', 'ev_w_in', 'ev_w_out', 'od_w_in', 'od_w_out', 'ffn_w_gate', 'ffn_w_up', 'ffn_w_down')


def _cparams(sem=None):
    return pltpu.CompilerParams(vmem_limit_bytes=V7X_VMEM_LIMIT, dimension_semantics=sem)


def _pick(n, cands):
    for c in cands:
        if n % c == 0:
            return c
    return n


def tile_fwd(name, f, grid, ins, outs):
    n_in = len(ins)

    def body(*refs):
        res = f(*[r[...] for r in refs[:n_in]])
        if not isinstance(res, (tuple, list)):
            res = (res,)
        for r, o in zip(res, refs[n_in:]):
            o[...] = r.astype(o.dtype)

    res = pl.pallas_call(
        body, grid=grid,
        in_specs=[pl.BlockSpec(b, m) for _, b, m in ins],
        out_specs=[pl.BlockSpec(b, m) for _, _, b, m in outs],
        out_shape=[jax.ShapeDtypeStruct(s, d) for s, d, _, _ in outs],
        name=name, compiler_params=_cparams(("arbitrary",) * len(grid)),
    )(*[a for a, _, _ in ins])
    return res


def tile_bwd(name, f, grid, ins, cts, grads, prims=()):
    n_in = len(ins)
    ct_flat = [p for c in cts for p in c]
    n_ct = len(ct_flat)
    didx = [g[0] for g in grads]

    def body(*refs):
        in_refs, ct_refs = refs[:n_in], refs[n_in:n_in + n_ct]
        g_refs = refs[n_in + n_ct:n_in + n_ct + len(grads)]
        p_refs = refs[n_in + n_ct + len(grads):]
        vals = [r[...] for r in in_refs]

        def fd(*dv):
            full = list(vals)
            for i, v in zip(didx, dv):
                full[i] = v
            res = f(*full)
            return tuple(res) if isinstance(res, (tuple, list)) else (res,)

        out, vjp = jax.vjp(fd, *[vals[i] for i in didx])
        ctv, k = [], 0
        for o, c in zip(out, cts):
            acc = None
            for _ in c:
                piece = ct_refs[k][...].astype(o.dtype)
                acc = piece if acc is None else acc + piece
                k += 1
            ctv.append(jnp.zeros_like(o) if acc is None else acc.reshape(o.shape))
        gs = vjp(tuple(ctv))
        ids = [pl.program_id(a) for a in range(len(grid))]

        def emit(ref, val, first):
            if first is None:
                ref[...] = val.astype(ref.dtype)
            else:
                is_first = first(*ids)

                @pl.when(is_first)
                def _():
                    ref[...] = val.astype(ref.dtype)

                @pl.when(jnp.logical_not(is_first))
                def _():
                    ref[...] += val.astype(ref.dtype)

        for g, spec, ref in zip(gs, grads, g_refs):
            emit(ref, g, spec[5])
        for spec, ref in zip(prims, p_refs):
            emit(ref, out[spec[0]], spec[5])

    specs = list(grads) + list(prims)
    res = pl.pallas_call(
        body, grid=grid,
        in_specs=[pl.BlockSpec(b, m) for _, b, m in list(ins) + ct_flat],
        out_specs=[pl.BlockSpec(s[3], s[4]) for s in specs],
        out_shape=[jax.ShapeDtypeStruct(s[1], s[2]) for s in specs],
        name=name, compiler_params=_cparams(("arbitrary",) * len(grid)),
    )(*[a for a, _, _ in list(ins) + ct_flat])
    return res


def mm(name, pairs, ta=False, tb=False, out_dtype=F32):
    a0, b0, _ = pairs[0]
    m = a0.shape[1] if ta else a0.shape[0]
    n = b0.shape[0] if tb else b0.shape[1]
    cands = (1024, 1408, 768, 512, 256, 128)
    tks, nks = [], []
    for a, b, _ in pairs:
        k = a.shape[0] if ta else a.shape[1]
        tk = _pick(k, cands)
        tks.append(tk)
        nks.append(k // tk)

    def vmem_bytes(tm, tn):
        tiles = sum(2 * tk * (tm * a.dtype.itemsize + tn * b.dtype.itemsize) for (a, b, _), tk in zip(pairs, tks))
        return tiles + tm * tn * (4 + 2 * jnp.dtype(out_dtype).itemsize)

    tm_c = [c_ for c_ in cands if m % c_ == 0] or [m]
    tn_c = [c_ for c_ in cands if n % c_ == 0] or [n]
    tm, tn = tm_c[0], tn_c[0]
    while vmem_bytes(tm, tn) > MM_VMEM_BUDGET and (len(tm_c) > 1 or len(tn_c) > 1):
        if len(tm_c) > 1 and (tm >= tn or len(tn_c) == 1):
            tm_c = tm_c[1:]
        else:
            tn_c = tn_c[1:]
        tm, tn = tm_c[0], tn_c[0]
    starts = [sum(nks[:p]) for p in range(len(pairs))]
    nk = sum(nks)
    np_ = len(pairs)

    def body(*refs):
        o_ref, acc = refs[2 * np_], refs[2 * np_ + 1]
        kk = pl.program_id(2)

        @pl.when(kk == 0)
        def _():
            acc[...] = jnp.zeros_like(acc)

        for p in range(np_):
            def add(p=p):
                a = refs[2 * p][...].astype(BF16)
                b = refs[2 * p + 1][...].astype(BF16)
                dn = (((0 if ta else 1,), (1 if tb else 0,)), ((), ()))
                acc[...] += lax.dot_general(a, b, dn, preferred_element_type=F32)
            if np_ == 1:
                add()
            else:
                pl.when((kk >= starts[p]) & (kk < starts[p] + nks[p]))(add)

        @pl.when(kk == nk - 1)
        def _():
            o_ref[...] = acc[...].astype(o_ref.dtype)

    in_specs, args = [], []
    for p, (a, b, off) in enumerate(pairs):
        tk, s0, nkp = tks[p], starts[p], nks[p]
        assert off % tk == 0
        boff = off // tk

        def kloc(k, s0=s0, nkp=nkp):
            return jnp.clip(k - s0, 0, nkp - 1)
        if ta:
            in_specs.append(pl.BlockSpec((tk, tm), lambda i, j, k, kloc=kloc: (kloc(k), i)))
        else:
            in_specs.append(pl.BlockSpec((tm, tk), lambda i, j, k, kloc=kloc: (i, kloc(k))))
        if tb:
            in_specs.append(pl.BlockSpec((tn, tk), lambda i, j, k, kloc=kloc, boff=boff: (j, boff + kloc(k))))
        else:
            in_specs.append(pl.BlockSpec((tk, tn), lambda i, j, k, kloc=kloc, boff=boff: (boff + kloc(k), j)))
        args += [a, b]
    return pl.pallas_call(
        body, grid=(m // tm, n // tn, nk), in_specs=in_specs,
        out_specs=pl.BlockSpec((tm, tn), lambda i, j, k: (i, j)),
        out_shape=jax.ShapeDtypeStruct((m, n), out_dtype),
        scratch_shapes=[pltpu.VMEM((tm, tn), F32)],
        name=name, compiler_params=_cparams(("arbitrary", "arbitrary", "arbitrary")),
    )(*args)


def _rms(x, g):
    return x * lax.rsqrt(jnp.mean(x * x, axis=-1, keepdims=True) + RMS_EPS) * g


def f_nm0(x, g, sh, sc):
    return _rms(x, g) * (1.0 + sc) + sh


def f_nm(xp, o, gate, g, sh, sc):
    x = xp + gate * o
    return x, _rms(x, g) * (1.0 + sc) + sh


def f_final(xp, o, gate, g, tgt, valid):
    x = xp + gate * o
    e = (_rms(x, g) - tgt) * valid
    return jnp.sum(e * e, axis=0, keepdims=True) * (0.5 / D)


@functools.partial(jax.custom_vjp, nondiff_argnums=(1,))
def _sroll(x, s):
    return pltpu.roll(x, s, 0)


def _sroll_fwd(x, s):
    return pltpu.roll(x, s, 0), None


def _sroll_bwd(s, _, g):
    return (pltpu.roll(g, (g.shape[0] - s) % g.shape[0], 0),)


_sroll.defvjp(_sroll_fwd, _sroll_bwd)


def _shifted(x, o):
    n = x.shape[0]
    return x if o == 0 else _sroll(x, (n - o) % n)


def f_conv1d(x, w, b, *, lc, act):
    n = x.shape[0]
    pos = lax.broadcasted_iota(jnp.int32, (n, 1), 0)
    lo = jnp.where(pos < lc, 0, lc)
    hi = jnp.where(pos < lc, lc, n)
    y = x * w[1:2] + b
    for k, o in ((0, -1), (2, 1), (3, 2)):
        src = pos + o
        valid = (src >= lo) & (src < hi)
        y = y + jnp.where(valid, _shifted(x, o), 0.0) * w[k:k + 1]
    return jax.nn.silu(y) if act else y


def f_ffnconv(a, up, w, b, *, lc):
    n = a.shape[0]
    rows = (n - lc) // GRID_W
    pos = lax.broadcasted_iota(jnp.int32, (n, 1), 0)
    is_ctx = pos < lc
    tl = pos - lc
    r = tl // GRID_W
    cc = tl - r * GRID_W
    y = jnp.zeros_like(a) + b
    for dr in (-1, 0, 1):
        for dc in (-1, 0, 1):
            v_lat = jnp.logical_not(is_ctx) & (r + dr >= 0) & (r + dr < rows) & (cc + dc >= 0) & (cc + dc < GRID_W)
            if dr == 0:
                valid = v_lat | (is_ctx & (pos + dc >= 0) & (pos + dc < lc))
            else:
                valid = v_lat
            k = 3 * (dr + 1) + (dc + 1)
            y = y + jnp.where(valid, _shifted(a, GRID_W * dr + dc), 0.0) * w[k:k + 1]
    return jax.nn.silu(y) * up


def f_ssd(xs, bc, dtraw, bias, alog, st, *, d, reverse):
    L = xs.shape[0]
    dtv = jax.nn.softplus(dtraw + bias)
    la = dtv * (-jnp.exp(alog))
    ri = lax.broadcasted_iota(jnp.int32, (L, L), 0)
    ci = lax.broadcasted_iota(jnp.int32, (L, L), 1)
    mask = (ci >= ri) if reverse else (ci <= ri)
    cum = jnp.dot(mask.astype(F32), la, precision=HI, preferred_element_type=F32)
    cum_t = cum.T
    tot = cum[0:1] if reverse else cum[L - 1:L]
    lo = lax.broadcasted_iota(jnp.int32, (1, 128), 1) < 64
    rlo = lax.broadcasted_iota(jnp.int32, (128, 1), 0) < 64
    ys, new = [], []
    cbs = {}
    for j in range(8):
        g = j // 4
        bg = bc[:, g * 128:(g + 1) * 128].astype(BF16)
        cg = bc[:, 256 + g * 128:256 + (g + 1) * 128].astype(BF16)
        if g not in cbs:
            cbs[g] = lax.dot_general(cg, bg, (((1,), (1,)), ((), ())), preferred_element_type=F32)
        cb = cbs[g]
        x = xs[:, j * 128:(j + 1) * 128]
        k1 = 16 * d + 2 * j
        k2 = k1 + 1
        c1, c2 = cum[:, k1:k1 + 1], cum[:, k2:k2 + 1]
        m1 = cb * jnp.exp(jnp.where(mask, c1 - cum_t[k1:k1 + 1, :], -1e30))
        m2 = cb * jnp.exp(jnp.where(mask, c2 - cum_t[k2:k2 + 1, :], -1e30))
        xdt = x * jnp.where(lo, dtv[:, k1:k1 + 1], dtv[:, k2:k2 + 1])
        xb = xdt.astype(BF16)
        y = jnp.where(lo, jnp.dot(m1.astype(BF16), xb, preferred_element_type=F32),
                      jnp.dot(m2.astype(BF16), xb, preferred_element_type=F32))
        sj = st[j]
        ch = lax.dot_general(cg, sj.astype(BF16), (((1,), (1,)), ((), ())), preferred_element_type=F32)
        y = y + ch * jnp.where(lo, jnp.exp(c1), jnp.exp(c2))
        t1, t2 = tot[:, k1:k1 + 1], tot[:, k2:k2 + 1]
        xe = (xdt * jnp.where(lo, jnp.exp(t1 - c1), jnp.exp(t2 - c2))).astype(BF16)
        upd = lax.dot_general(xe, bg, (((0,), (0,)), ((), ())), preferred_element_type=F32)
        new.append(sj * jnp.where(rlo, jnp.exp(t1), jnp.exp(t2)) + upd)
        ys.append(y)
    return jnp.concatenate(ys, axis=1), jnp.stack(new)


def f_hgrn(q_raw, f_raw, v, lb, zt, *, reverse):
    n = q_raw.shape[0]
    c = HG_CHUNK
    qa = jax.nn.silu(q_raw)
    logf = jnp.log(lb + (1.0 - lb) * jax.nn.sigmoid(f_raw))
    kk = (1.0 - lb) * jax.nn.sigmoid(-f_raw)
    ri = lax.broadcasted_iota(jnp.int32, (n, n), 0)
    ci = lax.broadcasted_iota(jnp.int32, (n, n), 1)
    tmat = ((ri // c == ci // c) & ((ci >= ri) if reverse else (ci <= ri))).astype(F32)
    cum_all = jnp.dot(tmat, logf, precision=HI, preferred_element_type=F32)
    r3 = lax.broadcasted_iota(jnp.int32, (c, c, 128), 0)
    c3 = lax.broadcasted_iota(jnp.int32, (c, c, 128), 1)
    mask3 = (c3 >= r3) if reverse else (c3 <= r3)
    nch = n // c
    outs = [None] * nch
    for chn in (reversed(range(nch)) if reverse else range(nch)):
        sl = slice(chn * c, (chn + 1) * c)
        q, k, vv, cum = qa[sl], kk[sl], v[sl], cum_all[sl]
        dec = jnp.exp(jnp.where(mask3, cum[:, None, :] - cum[None, :, :], -1e30))
        att = jnp.sum(q[:, None, :] * dec * k[None, :, :], axis=-1)
        y = jnp.dot(att.astype(BF16), vv.astype(BF16), preferred_element_type=F32)
        y = y + lax.dot_general((q * jnp.exp(cum)).astype(BF16), zt.astype(BF16), (((1,), (1,)), ((), ())),
                                preferred_element_type=F32)
        tot = cum[0:1] if reverse else cum[c - 1:c]
        kd = (k * jnp.exp(tot - cum)).astype(BF16)
        zt = zt * jnp.exp(tot) + lax.dot_general(vv.astype(BF16), kd, (((0,), (0,)), ((), ())), preferred_element_type=F32)
        outs[chn] = y
    return jnp.concatenate(outs, axis=0), zt


def _expm1(x):
    poly = x * (1.0 + x * (0.5 + x * (1.0 / 6 + x * (1.0 / 24 + x * (1.0 / 120 + x * (1.0 / 720))))))
    return jnp.where(jnp.abs(x) < 0.3, poly, jnp.exp(x) - 1.0)


def f_gates(u, wa, ba, wi, bi, lam):
    rs, is_ = [], []
    for nb in range(8):
        un = u[:, nb * 128:(nb + 1) * 128].astype(BF16)
        rs.append(jnp.dot(un, wa[nb].astype(BF16), preferred_element_type=F32))
        is_.append(jnp.dot(un, wi[nb].astype(BF16), preferred_element_type=F32))
    r = jax.nn.sigmoid(jnp.concatenate(rs, axis=1) + ba)
    i = jax.nn.sigmoid(jnp.concatenate(is_, axis=1) + bi)
    log_a = -LRU_C * jax.nn.softplus(-lam) * r
    return jnp.exp(log_a), jnp.sqrt(-_expm1(2.0 * log_a)) * (i * u)


def f_ssdfin(y0, y1, xs, z, gy, h0, h1, dpad, ng):
    kk = lax.broadcasted_iota(jnp.int32, (128, D), 0)
    ch = lax.broadcasted_iota(jnp.int32, (128, D), 1)
    expand = (ch // 64 == kk).astype(F32)
    dvec = jnp.dot(dpad, expand, precision=HI, preferred_element_type=F32)[0:1]
    y = y0 + y1 + dvec * xs
    yn = _rms(y * jax.nn.silu(z), ng)
    r = (h0 + h1) * jax.nn.gelu(gy)
    return jnp.concatenate([yn, r], axis=1)


def f_oddfin(o0, o1, g, y0, y1, u, hn, sd, gw, gb):
    parts = []
    for h in range(HG_HEADS):
        sl = slice(h * 128, (h + 1) * 128)
        parts.append(_rms(o0[:, sl] + o1[:, sl], hn[h:h + 1]) * jax.nn.silu(g[:, sl]))
    y = jax.nn.gelu(y0 + y1 + sd * u)
    y = y * jax.nn.sigmoid(jnp.dot(y.astype(BF16), gw.astype(BF16), preferred_element_type=F32) + gb)
    return jnp.concatenate(parts + [y], axis=1)


def f_s5p(lre, lim, lstep, btr, bti):
    step = jnp.exp(lstep)
    mag = jnp.exp(lre * step)
    ar, ai = mag * jnp.cos(lim * step), mag * jnp.sin(lim * step)
    den = lre * lre + lim * lim
    zr = ((ar - 1.0) * lre + ai * lim) / den
    zi = (ai * lre - (ar - 1.0) * lim) / den
    bbr = zr[:, None, :] * btr - zi[:, None, :] * bti
    bbi = zr[:, None, :] * bti + zi[:, None, :] * btr
    return ar, ai, bbr, bbi


def f_lb(logits):
    m = jnp.max(logits, axis=0, keepdims=True)
    e = jnp.exp(logits - m)
    p = e / jnp.sum(e, axis=0, keepdims=True)
    return p[1:2], p[1:2] + p[2:3] + p[3:4]


def f_adamw(w, m, v, *gs):
    g = gs[0]
    for t in gs[1:]:
        g = g + t
    m = ADAM_B1 * m + (1.0 - ADAM_B1) * g
    v = ADAM_B2 * v + (1.0 - ADAM_B2) * jnp.square(g)
    m_hat = m / (1.0 - ADAM_B1 ** ADAM_STEP)
    v_hat = v / (1.0 - ADAM_B2 ** ADAM_STEP)
    delta = -ADAM_LR * (m_hat / (jnp.sqrt(v_hat) + ADAM_EPS) + ADAM_WD * w)
    return g, delta, m, v


def scan_fwd(name, f, grid, ins, y_out, st_out, state_shape, is_first):
    n_in = len(ins)

    def body(*refs):
        y_ref, so_ref, st = refs[n_in], refs[n_in + 1], refs[n_in + 2]
        ids = [pl.program_id(a) for a in range(len(grid))]

        @pl.when(is_first(*ids))
        def _():
            st[...] = jnp.zeros_like(st)

        s = st[...]
        so_ref[...] = s
        y, new = f(*[r[...] for r in refs[:n_in]], s)
        y_ref[...] = y.astype(y_ref.dtype)
        st[...] = new

    return pl.pallas_call(
        body, grid=grid,
        in_specs=[pl.BlockSpec(b, m) for _, b, m in ins],
        out_specs=[pl.BlockSpec(y_out[2], y_out[3]), pl.BlockSpec(st_out[2], st_out[3])],
        out_shape=[jax.ShapeDtypeStruct(y_out[0], y_out[1]), jax.ShapeDtypeStruct(st_out[0], st_out[1])],
        scratch_shapes=[pltpu.VMEM(state_shape, F32)],
        name=name, compiler_params=_cparams(("arbitrary",) * len(grid)),
    )(*[a for a, _, _ in ins])


def scan_bwd(name, f, grid, ins, st_in, dy, grads, state_shape, is_first):
    n_in = len(ins)
    didx = [g[0] for g in grads]

    def body(*refs):
        s_ref, dy_ref = refs[n_in], refs[n_in + 1]
        g_refs = refs[n_in + 2:n_in + 2 + len(grads)]
        dst = refs[n_in + 2 + len(grads)]
        ids = [pl.program_id(a) for a in range(len(grid))]

        @pl.when(is_first(*ids))
        def _():
            dst[...] = jnp.zeros_like(dst)

        vals = [r[...] for r in refs[:n_in]]

        def fd(s, *dv):
            full = list(vals)
            for i, v in zip(didx, dv):
                full[i] = v
            return f(*full, s)

        (y, _), vjp = jax.vjp(fd, s_ref[...], *[vals[i] for i in didx])
        gs = vjp((dy_ref[...].astype(y.dtype), dst[...]))
        dst[...] = gs[0]
        for g, spec, ref in zip(gs[1:], grads, g_refs):
            first = spec[5]
            if first is None:
                ref[...] = g.astype(ref.dtype)
            else:
                fst = first(*ids)

                @pl.when(fst)
                def _(ref=ref, g=g):
                    ref[...] = g.astype(ref.dtype)

                @pl.when(jnp.logical_not(fst))
                def _(ref=ref, g=g):
                    ref[...] += g.astype(ref.dtype)

    allin = list(ins) + [st_in, dy]
    return pl.pallas_call(
        body, grid=grid,
        in_specs=[pl.BlockSpec(b, m) for _, b, m in allin],
        out_specs=[pl.BlockSpec(s[3], s[4]) for s in grads],
        out_shape=[jax.ShapeDtypeStruct(s[1], s[2]) for s in grads],
        scratch_shapes=[pltpu.VMEM(state_shape, F32)],
        name=name, compiler_params=_cparams(("arbitrary",) * len(grid)),
    )(*[a for a, _, _ in allin])


def _tile_order(order, nt):
    if order == 'F':
        return (lambda j: j), True
    if order == 'Fb':
        return (lambda j: nt - 1 - j), False
    if order == 'R':
        return (lambda j: jnp.where(j == 0, 0, nt - j)), False
    return (lambda j: jnp.where(j == nt - 1, 0, j + 1)), True


def linrec(name, a, b, order):
    bsz, tt = a.shape[:2]
    tq = _pick(tt, (256, 128))
    nt = tt // tq
    phys, asc = _tile_order(order, nt)

    def body(a_ref, b_ref, h_ref, hp_ref, hs):
        @pl.when(pl.program_id(0) == 0)
        def _():
            hs[...] = jnp.zeros_like(hs)

        def step(i, hcur):
            t = i if asc else tq - 1 - i
            out = []
            for bi in range(bsz):
                hp_ref[bi, t] = hcur[bi]
                hn = a_ref[bi, t] * hcur[bi] + b_ref[bi, t]
                h_ref[bi, t] = hn
                out.append(hn)
            return tuple(out)

        fin = lax.fori_loop(0, tq, step, tuple(hs[bi] for bi in range(bsz)))
        for bi in range(bsz):
            hs[bi] = fin[bi]

    spec = pl.BlockSpec((bsz, tq, 8, 128), lambda j: (0, phys(j), 0, 0))
    return pl.pallas_call(
        body, grid=(nt,), in_specs=[spec, spec], out_specs=[spec, spec],
        out_shape=[jax.ShapeDtypeStruct(a.shape, F32)] * 2,
        scratch_shapes=[pltpu.VMEM((bsz, 8, 128), F32)],
        name=name, compiler_params=_cparams(("arbitrary",)),
    )(a, b)


def linrec_bwd(name, a, dh, hprev, order):
    bsz, tt = a.shape[:2]
    tq = _pick(tt, (256, 128))
    nt = tt // tq
    phys, asc = _tile_order(order, nt)

    def body(a_ref, dh_ref, hp_ref, g_ref, ga_ref, gs, as_):
        @pl.when(pl.program_id(0) == 0)
        def _():
            gs[...] = jnp.zeros_like(gs)
            as_[...] = jnp.zeros_like(as_)

        def step(i, carry):
            t = i if asc else tq - 1 - i
            out = []
            for bi in range(bsz):
                gcur, acur = carry[bi]
                gn = acur * gcur + dh_ref[bi, t]
                g_ref[bi, t] = gn
                ga_ref[bi, t] = gn * hp_ref[bi, t]
                out.append((gn, a_ref[bi, t]))
            return tuple(out)

        fin = lax.fori_loop(0, tq, step, tuple((gs[bi], as_[bi]) for bi in range(bsz)))
        for bi in range(bsz):
            gs[bi] = fin[bi][0]
            as_[bi] = fin[bi][1]

    spec = pl.BlockSpec((bsz, tq, 8, 128), lambda j: (0, phys(j), 0, 0))
    return pl.pallas_call(
        body, grid=(nt,), in_specs=[spec, spec, spec], out_specs=[spec, spec],
        out_shape=[jax.ShapeDtypeStruct(a.shape, F32)] * 2,
        scratch_shapes=[pltpu.VMEM((bsz, 8, 128), F32), pltpu.VMEM((bsz, 8, 128), F32)],
        name=name, compiler_params=_cparams(("arbitrary",)),
    )(a, dh, hprev)


def clinrec(name, x, coef, order):
    bsz, tt = x.shape[:2]
    tq = _pick(tt, (256, 128))
    nt = tt // tq
    phys, asc = _tile_order(order, nt)

    def body(x_ref, c_ref, h_ref, hp_ref, hs):
        @pl.when(pl.program_id(0) == 0)
        def _():
            hs[...] = jnp.zeros_like(hs)

        ar, ai = c_ref[0], c_ref[1]

        def step(i, carry):
            t = i if asc else tq - 1 - i
            out = []
            for bi in range(bsz):
                hr, hi = carry[bi]
                hp_ref[bi, t, 0] = hr
                hp_ref[bi, t, 1] = hi
                nr = ar * hr - ai * hi + x_ref[bi, t, 0]
                ni = ar * hi + ai * hr + x_ref[bi, t, 1]
                h_ref[bi, t, 0] = nr
                h_ref[bi, t, 1] = ni
                out.append((nr, ni))
            return tuple(out)

        fin = lax.fori_loop(0, tq, step, tuple((hs[bi, 0], hs[bi, 1]) for bi in range(bsz)))
        for bi in range(bsz):
            hs[bi, 0] = fin[bi][0]
            hs[bi, 1] = fin[bi][1]

    spec = pl.BlockSpec((bsz, tq, 2, 8, 128), lambda j: (0, phys(j), 0, 0, 0))
    cspec = pl.BlockSpec((2, 8, 128), lambda j: (0, 0, 0))
    return pl.pallas_call(
        body, grid=(nt,), in_specs=[spec, cspec], out_specs=[spec, spec],
        out_shape=[jax.ShapeDtypeStruct(x.shape, F32)] * 2,
        scratch_shapes=[pltpu.VMEM((bsz, 2, 8, 128), F32)],
        name=name, compiler_params=_cparams(("arbitrary",)),
    )(x, coef)


def clinrec_bwd(name, dh, hprev, coef, order):
    bsz, tt = dh.shape[:2]
    tq = _pick(tt, (256, 128))
    nt = tt // tq
    phys, asc = _tile_order(order, nt)

    def body(d_ref, hp_ref, c_ref, g_ref, dc_ref, gs):
        @pl.when(pl.program_id(0) == 0)
        def _():
            gs[...] = jnp.zeros_like(gs)
            dc_ref[...] = jnp.zeros_like(dc_ref)

        ar, ai = c_ref[0], c_ref[1]

        def step(i, carry):
            t = i if asc else tq - 1 - i
            gcar, dar, dai = carry
            out = []
            for bi in range(bsz):
                gr, gi = gcar[bi]
                nr = ar * gr + ai * gi + d_ref[bi, t, 0]
                ni = ar * gi - ai * gr + d_ref[bi, t, 1]
                g_ref[bi, t, 0] = nr
                g_ref[bi, t, 1] = ni
                hpr, hpi = hp_ref[bi, t, 0], hp_ref[bi, t, 1]
                dar = dar + nr * hpr + ni * hpi
                dai = dai + ni * hpr - nr * hpi
                out.append((nr, ni))
            return tuple(out), dar, dai

        z = jnp.zeros((8, 128), F32)
        fin, dar, dai = lax.fori_loop(0, tq, step, (tuple((gs[bi, 0], gs[bi, 1]) for bi in range(bsz)), z, z))
        for bi in range(bsz):
            gs[bi, 0] = fin[bi][0]
            gs[bi, 1] = fin[bi][1]
        dc_ref[0] += dar
        dc_ref[1] += dai

    spec = pl.BlockSpec((bsz, tq, 2, 8, 128), lambda j: (0, phys(j), 0, 0, 0))
    cspec = pl.BlockSpec((2, 8, 128), lambda j: (0, 0, 0))
    return pl.pallas_call(
        body, grid=(nt,), in_specs=[spec, spec, cspec], out_specs=[spec, cspec],
        out_shape=[jax.ShapeDtypeStruct(dh.shape, F32), jax.ShapeDtypeStruct((2, 8, 128), F32)],
        scratch_shapes=[pltpu.VMEM((bsz, 2, 8, 128), F32)],
        name=name, compiler_params=_cparams(("arbitrary",)),
    )(dh, hprev, coef)


def _blockdiag(bb):
    eye = jnp.eye(16, dtype=bb.dtype)
    return (bb[:, :, None, :] * eye[:, None, :, None]).reshape(256, 1024)


def _blockdiag_t(c):
    eye = jnp.eye(16, dtype=c.dtype)
    return (jnp.swapaxes(c, 1, 2)[:, :, None, :] * eye[:, None, :, None]).reshape(1024, 256)


def _unblockdiag(m):
    eye = jnp.eye(16, dtype=m.dtype)
    return jnp.sum(m.reshape(16, 16, 16, 64) * eye[:, None, :, None], axis=2)


def _unblockdiag_t(m):
    eye = jnp.eye(16, dtype=m.dtype)
    return jnp.swapaxes(jnp.sum(m.reshape(16, 64, 16, 16) * eye[:, None, :, None], axis=2), 1, 2)


def _pad_rows(v, rows=8, cols=128):
    out = jnp.zeros((rows, cols), F32)
    return out.at[0, :v.shape[0]].set(v)


def local_step(x, c, ctx, c_ctx, target, W):
    B, Tx, _ = x.shape
    Lc = ctx.shape[1]
    Tt = Lc + Tx
    tb = Lc
    nt = Tt // tb
    M = B * Tt
    nc = Tt // SSD_CHUNK
    ncc = Lc // SSD_CHUNK
    G = {}

    def add_grad(name, idx, val):
        G.setdefault(name, {})[idx] = val

    def tok(a, cb=None, off=0):
        cb = a.shape[-1] if cb is None else cb
        return (a, (None, tb, cb), lambda b, j, off=off: (b, j, off))

    def tok_out(cols, dtype=F32):
        return ((B, Tt, cols), dtype, (None, tb, cols), lambda b, j: (b, j, 0))

    def vec(a):
        return (a, a.shape, lambda *ids, n=a.ndim: (0,) * n)

    def vec_acc(shape):
        return (shape, F32, shape, lambda *ids, n=len(shape): (0,) * n, lambda *ids: functools.reduce(jnp.logical_and, [i == 0 for i in ids]))

    def modv(l, which):
        return (modr, (None, None, None, 1, D), lambda b, j, l=l, which=which: (l, jnp.where(j == 0, 4, b), which, 0, 0))

    dmod_spec = ((B, 2, 1, D), F32, (None, None, 1, D), lambda b, j: (b, jnp.where(j == 0, 0, 1), 0, 0), lambda b, j: j <= 1)

    def phys_chunk(n_all, n_ctx, reverse):
        if not reverse:
            return lambda s: s
        return lambda s: jnp.where(s < n_ctx, n_ctx - 1 - s, n_all - 1 - (s - n_ctx))

    cc = jnp.zeros((8, D), F32).at[:B].set(c).at[4].set(c_ctx)
    nmc = N_MOD * D // 1536

    def f_mod(ccv, w, b):
        return jnp.dot(jax.nn.silu(ccv).astype(BF16), w, preferred_element_type=F32) + b

    mod = tile_fwd('mod_fwd', f_mod, (DEPTH, nmc),
                   [(cc, (8, D), lambda l, n: (0, 0)), (W['w_mod'], (None, D, 1536), lambda l, n: (l, 0, n)),
                    (W['b_mod'].reshape(DEPTH, 1, N_MOD * D), (None, 1, 1536), lambda l, n: (l, 0, n))],
                   [((DEPTH, 8, N_MOD * D), F32, (None, 8, 1536), lambda l, n: (l, 0, n))])[0]
    modr = mod.reshape(DEPTH, 8, N_MOD, 1, D)
    dmods = {}

    lb1, lb3 = tile_fwd('lb_fwd', f_lb, (1,), [vec(W['hg_lb_logits'])],
                        [((1, HG_W), F32, (1, HG_W), lambda i: (0, 0))] * 2)
    dlb = {1: [], 3: []}

    x0 = jnp.concatenate([ctx, x], axis=1)
    R = [dict() for _ in range(DEPTH)]

    xprev, oprev = x0, None
    for l in range(DEPTH):
        r = R[l]
        j = l // 2
        ng = W['norm_mix_g'][l][None]
        if l == 0:
            h1 = tile_fwd(f'nm0_fwd', f_nm0, (B, nt), [tok(xprev), vec(ng), modv(l, 0), modv(l, 1)], [tok_out(D, BF16)])[0]
            xa = xprev
        else:
            xa, h1 = tile_fwd(f'nm_mix_fwd{l}', f_nm, (B, nt),
                              [tok(xprev), tok(oprev), modv(l - 1, 5), vec(ng), modv(l, 0), modv(l, 1)],
                              [tok_out(D), tok_out(D, BF16)])
        r['xin'], r['oin'], r['xa'], r['h1'] = xprev, oprev, xa, h1
        h1m = h1.reshape(M, D)
        if l % 2 == 0:
            win = W['ev_w_in_p'][j]
            proj = mm(f'ev_proj{l}', [(h1m, win, 0)]).reshape(B, Tt, EV_PAD)
            r['proj'] = proj
            scw, scb = W['ssd_conv_w'][j], W['ssd_conv_b'][j][None]
            lcw, lcb = W['lru_conv_w'][j], W['lru_conv_b'][j][None]

            def conv_call(name, colblk0, w, b, wblk0, ncols, act):
                return tile_fwd(name, functools.partial(f_conv1d, lc=Lc, act=act), (ncols // 256, B),
                                [(proj, (None, Tt, 256), lambda cb, bi: (bi, 0, colblk0 + cb)),
                                 (w, (4, 256), lambda cb, bi: (0, wblk0 + cb)), (b, (1, 256), lambda cb, bi: (0, wblk0 + cb))],
                                [((B, Tt, ncols), F32, (None, Tt, 256), lambda cb, bi: (bi, 0, cb))])[0]
            xs_c = conv_call(f'conv_xs{l}', 12, scw, scb, 0, 1024, True)
            bc_c = conv_call(f'conv_bc{l}', 16, scw, scb, 4, 512, True)
            u_c = conv_call(f'conv_u{l}', 8, lcw, lcb, 0, 1024, False)
            r['xs'], r['bc'], r['u'] = xs_c, bc_c, u_c
            bias = _pad_rows(W['ssd_dt_bias'][j].reshape(-1), 1)
            alog = _pad_rows(W['ssd_a_log'][j].reshape(-1), 1)
            r['bias'], r['alog'] = bias, alog
            r['y'], r['st'], r['a4'], r['hp4'], r['h'] = [], [], [], [], []
            for d in range(2):
                ph = phys_chunk(nc, ncc, d == 1)
                y, st = scan_fwd(
                    f'ssd_fwd{l}_{d}', functools.partial(f_ssd, d=d, reverse=(d == 1)), (B, nc),
                    [(xs_c, (None, SSD_CHUNK, D), lambda b, s, ph=ph: (b, ph(s), 0)),
                     (bc_c, (None, SSD_CHUNK, 512), lambda b, s, ph=ph: (b, ph(s), 0)),
                     (proj, (None, SSD_CHUNK, 128), lambda b, s, ph=ph: (b, ph(s), 36)),
                     vec(bias), vec(alog)],
                    ((B, Tt, D), F32, (None, SSD_CHUNK, D), lambda b, s, ph=ph: (b, ph(s), 0)),
                    ((B, nc, 8, 128, 128), F32, (None, None, 8, 128, 128), lambda b, s: (b, s, 0, 0, 0)),
                    (8, 128, 128), lambda b, s: s == 0)
                r['y'].append(y)
                r['st'].append(st)
                a_d, bx_d = tile_fwd(
                    f'gates_fwd{l}_{d}', f_gates, (B, nt),
                    [tok(u_c), vec(W['lru_w_a'][j, d]), vec(W['lru_b_a'][j, d][None]), vec(W['lru_w_i'][j, d]),
                     vec(W['lru_b_i'][j, d][None]), vec(W['lru_lam'][j, d][None])],
                    [tok_out(D), tok_out(D)])
                a4 = a_d.reshape(B, Tt, 8, 128)
                h4, hp4 = linrec(f'lru_fwd{l}_{d}', a4, bx_d.reshape(B, Tt, 8, 128), 'F' if d == 0 else 'R')
                r['a4'].append(a4)
                r['hp4'].append(hp4)
                r['h'].append(h4.reshape(B, Tt, D))
            dpad = _pad_rows(W['ssd_d'][j])
            sng = W['ssd_norm_g'][j][None]
            r['dpad'], r['sng'] = dpad, sng
            mix = tile_fwd(f'ssdfin_fwd{l}', f_ssdfin, (B, nt),
                           [tok(r['y'][0]), tok(r['y'][1]), tok(xs_c), tok(proj, D, 0), tok(proj, D, 1), tok(r['h'][0]),
                            tok(r['h'][1]), vec(dpad), vec(sng)], [tok_out(2 * D, BF16)])[0]
            wout = W['ev_w_out'][j]
        else:
            win = W['od_w_in'][j]
            proj = mm(f'od_proj{l}', [(h1m, win, 0)]).reshape(B, Tt, 4096)
            r['proj'] = proj
            lbv = lb1 if l == 1 else lb3
            ns = nc
            r['o'], r['zst'], r['coef'], r['bcat'], r['ccat'], r['hp5'], r['hcat'], r['yd'], r['s5in'] = [], [], [], [], [], [], [], [], []
            u2 = proj[:, :, 3840:].reshape(M, S5_W)
            r['u2'] = u2
            for d in range(2):
                ph = phys_chunk(ns, ncc, d == 1)
                o_d, zst = scan_fwd(
                    f'hgrn_fwd{l}_{d}', functools.partial(f_hgrn, reverse=(d == 1)), (HG_HEADS, B, ns),
                    [(proj, (None, 128, 128), lambda h, b, s, ph=ph: (b, ph(s), h)),
                     (proj, (None, 128, 128), lambda h, b, s, ph=ph, d=d: (b, ph(s), 6 + 6 * d + h)),
                     (proj, (None, 128, 128), lambda h, b, s, ph=ph: (b, ph(s), 18 + h)),
                     (lbv, (1, 128), lambda h, b, s: (0, h))],
                    ((B, Tt, HG_W), F32, (None, 128, 128), lambda h, b, s, ph=ph: (b, ph(s), h)),
                    ((B, HG_HEADS, ns, 128, 128), F32, (None, None, None, 128, 128), lambda h, b, s: (b, h, s, 0, 0)),
                    (128, 128), lambda h, b, s: s == 0)
                r['o'].append(o_d)
                r['zst'].append(zst)
                s5in = [W['s5_lam_re'][j, d], W['s5_lam_im'][j, d], W['s5_log_step'][j, d].reshape(16, 1),
                        jnp.swapaxes(W['s5_b_re'][j], 1, 2), jnp.swapaxes(W['s5_b_im'][j], 1, 2)]
                r['s5in'].append(s5in)
                ar, ai, bbr, bbi = tile_fwd(f's5p_fwd{l}_{d}', f_s5p, (1,), [vec(t) for t in s5in],
                                            [((16, 64), F32, (16, 64), lambda i: (0, 0))] * 2
                                            + [((16, 16, 64), F32, (16, 16, 64), lambda i: (0, 0, 0))] * 2)
                coef = jnp.stack([ar.reshape(8, 128), ai.reshape(8, 128)])
                bcat = jnp.concatenate([_blockdiag(bbr), _blockdiag(bbi)], axis=1).astype(BF16)
                ccat = jnp.concatenate([_blockdiag_t(W['s5_c_re'][j, d]), -_blockdiag_t(W['s5_c_im'][j, d])], axis=0).astype(BF16)
                xcat = mm(f's5_in{l}_{d}', [(u2, bcat, 0)])
                h5, hp5 = clinrec(f's5_fwd{l}_{d}', xcat.reshape(B, Tt, 2, 8, 128), coef, 'F' if d == 0 else 'R')
                hcat = h5.reshape(M, 2 * D)
                yd = mm(f's5_out{l}_{d}', [(hcat, ccat, 0)]).reshape(B, Tt, S5_W)
                r['coef'].append(coef)
                r['bcat'].append(bcat)
                r['ccat'].append(ccat)
                r['hp5'].append(hp5)
                r['hcat'].append(hcat)
                r['yd'].append(yd)
            hn = jnp.zeros((8, 128), F32).at[:HG_HEADS].set(W['hg_norm_g'][j])
            sd, gw, gb = W['s5_d'][j][None], W['s5_glu_w'][j], W['s5_glu_b'][j][None]
            r['fin_par'] = (hn, sd, gw, gb)
            mix = tile_fwd(f'oddfin_fwd{l}', f_oddfin, (B, nt),
                           [tok(r['o'][0]), tok(r['o'][1]), tok(proj, HG_W, 4), tok(r['yd'][0]), tok(r['yd'][1]),
                            tok(proj, S5_W, 15), vec(hn), vec(sd), vec(gw), vec(gb)], [tok_out(D, BF16)])[0]
            wout = W['od_w_out'][j]
        r['mix'] = mix
        o1 = mm(f'mix_out{l}', [(mix.reshape(M, -1), wout, 0)]).reshape(B, Tt, D)
        r['o1'] = o1
        fg = W['norm_ffn_g'][l][None]
        xb, h2 = tile_fwd(f'nm_ffn_fwd{l}', f_nm, (B, nt), [tok(xa), tok(o1), modv(l, 2), vec(fg), modv(l, 3), modv(l, 4)],
                          [tok_out(D), tok_out(D, BF16)])
        r['h2'] = h2
        h2m = h2.reshape(M, D)
        a = mm(f'ffn_gate{l}', [(h2m, W['ffn_w_gate'][l], 0)]).reshape(B, Tt, D_FF)
        up = mm(f'ffn_up{l}', [(h2m, W['ffn_w_up'][l], 0)]).reshape(B, Tt, D_FF)
        w9 = W['ffn_conv_w'][l].reshape(9, D_FF)
        cbias = W['ffn_conv_b'][l][None]
        r['a'], r['up'], r['w9'], r['cbias'] = a, up, w9, cbias
        act = tile_fwd(f'ffnconv_fwd{l}', functools.partial(f_ffnconv, lc=Lc), (D_FF // 128, B),
                       [(a, (None, Tt, 128), lambda cb, bi: (bi, 0, cb)), (up, (None, Tt, 128), lambda cb, bi: (bi, 0, cb)),
                        (w9, (9, 128), lambda cb, bi: (0, cb)), (cbias, (1, 128), lambda cb, bi: (0, cb))],
                       [((B, Tt, D_FF), BF16, (None, Tt, 128), lambda cb, bi: (bi, 0, cb))])[0]
        r['act'] = act
        o2 = mm(f'ffn_down{l}', [(act.reshape(M, D_FF), W['ffn_w_down'][l], 0)]).reshape(B, Tt, D)
        xprev, oprev = xb, o2

    vmask = jnp.ones((nt, 1, D), F32).at[0].set(0.0)
    ones = jnp.ones((1, D), F32)
    fng = W['final_norm_g'][None]
    d_xp, d_o2, dg5, dfng, loss_vec = tile_bwd(
        'loss_head', f_final, (B, nt),
        [tok(xprev), tok(oprev), modv(DEPTH - 1, 5), vec(fng),
         (target, (None, tb, D), lambda b, j: (b, jnp.maximum(j - 1, 0), 0)), (vmask, (None, 1, D), lambda b, j: (j, 0, 0))],
        [[vec(ones)]],
        [(0,) + tok_out(D) + (None,), (1,) + tok_out(D, BF16) + (None,), (2,) + dmod_spec, (3,) + vec_acc((1, D))],
        prims=[(0,) + vec_acc((1, D))])
    loss = jnp.sum(loss_vec)
    add_grad('final_norm_g', None, dfng[0])
    dmods[(DEPTH - 1, 5)] = dg5

    for l in reversed(range(DEPTH)):
        r = R[l]
        j = l // 2
        d_o2m = d_o2.reshape(M, D)
        d_act = mm(f'ffn_down_dx{l}', [(d_o2m, W['ffn_w_down'][l], 0)], tb=True).reshape(B, Tt, D_FF)
        add_grad('ffn_w_down', l, mm(f'ffn_down_dw{l}', [(r['act'].reshape(M, D_FF), d_o2m, 0)], ta=True))
        d_a, d_up, dw9, dcb = tile_bwd(
            f'ffnconv_bwd{l}', functools.partial(f_ffnconv, lc=Lc), (D_FF // 128, B),
            [(r['a'], (None, Tt, 128), lambda cb, bi: (bi, 0, cb)), (r['up'], (None, Tt, 128), lambda cb, bi: (bi, 0, cb)),
             (r['w9'], (9, 128), lambda cb, bi: (0, cb)), (r['cbias'], (1, 128), lambda cb, bi: (0, cb))],
            [[(d_act, (None, Tt, 128), lambda cb, bi: (bi, 0, cb))]],
            [(0, (B, Tt, D_FF), BF16, (None, Tt, 128), lambda cb, bi: (bi, 0, cb), None),
             (1, (B, Tt, D_FF), BF16, (None, Tt, 128), lambda cb, bi: (bi, 0, cb), None),
             (2, (9, D_FF), F32, (9, 128), lambda cb, bi: (0, cb), lambda cb, bi: bi == 0),
             (3, (1, D_FF), F32, (1, 128), lambda cb, bi: (0, cb), lambda cb, bi: bi == 0)])
        add_grad('ffn_conv_w', l, dw9.reshape(3, 3, D_FF))
        add_grad('ffn_conv_b', l, dcb[0])
        d_am, d_upm = d_a.reshape(M, D_FF), d_up.reshape(M, D_FF)
        h2m = r['h2'].reshape(M, D)
        d_h2 = mm(f'ffn_in_dx{l}', [(d_am, W['ffn_w_gate'][l], 0), (d_upm, W['ffn_w_up'][l], 0)], tb=True).reshape(B, Tt, D)
        add_grad('ffn_w_gate', l, mm(f'ffn_gate_dw{l}', [(h2m, d_am, 0)], ta=True))
        add_grad('ffn_w_up', l, mm(f'ffn_up_dw{l}', [(h2m, d_upm, 0)], ta=True))
        fg = W['norm_ffn_g'][l][None]
        d_xa, d_o1, dgate, dfg, dsh, dsc = tile_bwd(
            f'nm_ffn_bwd{l}', f_nm, (B, nt), [tok(r['xa']), tok(r['o1']), modv(l, 2), vec(fg), modv(l, 3), modv(l, 4)],
            [[tok(d_xp)], [tok(d_h2)]],
            [(0,) + tok_out(D) + (None,), (1,) + tok_out(D, BF16) + (None,), (2,) + dmod_spec, (3,) + vec_acc((1, D)),
             (4,) + dmod_spec, (5,) + dmod_spec])
        add_grad('norm_ffn_g', l, dfg[0])
        dmods[(l, 2)], dmods[(l, 3)], dmods[(l, 4)] = dgate, dsh, dsc
        d_o1m = d_o1.reshape(M, D)
        h1m = r['h1'].reshape(M, D)
        proj = r['proj']
        if l % 2 == 0:
            wout, win = W['ev_w_out'][j], W['ev_w_in_p'][j]
            d_mix = mm(f'mix_out_dx{l}', [(d_o1m, wout, 0)], tb=True).reshape(B, Tt, 2 * D)
            add_grad('ev_w_out', j, mm(f'mix_out_dw{l}', [(r['mix'].reshape(M, 2 * D), d_o1m, 0)], ta=True))
            d_y, d_xs_fin, d_z, d_gy, d_h, ddpad, dsng = tile_bwd(
                f'ssdfin_bwd{l}', f_ssdfin, (B, nt),
                [tok(r['y'][0]), tok(r['y'][1]), tok(r['xs']), tok(proj, D, 0), tok(proj, D, 1), tok(r['h'][0]), tok(r['h'][1]),
                 vec(r['dpad']), vec(r['sng'])],
                [[tok(d_mix)]],
                [(0,) + tok_out(D) + (None,), (2,) + tok_out(D) + (None,), (3,) + tok_out(D) + (None,), (4,) + tok_out(D) + (None,),
                 (5,) + tok_out(D) + (None,), (7,) + vec_acc((8, 128)), (8,) + vec_acc((1, D))])
            add_grad('ssd_d', j, ddpad[0, :SSD_HEADS])
            add_grad('ssd_norm_g', j, dsng[0])
            d_xs_parts, d_bc_parts, d_dt_parts, d_u_parts = [d_xs_fin], [], [], []
            dbias_t, dalog_t = [], []
            dh4 = d_h.reshape(B, Tt, 8, 128)
            for d in range(2):
                ph0 = phys_chunk(nc, ncc, d == 1)

                def ph(s, ph0=ph0):
                    return ph0(nc - 1 - s)
                dxs_d, dbc_d, ddt_d, dbias, dalog = scan_bwd(
                    f'ssd_bwd{l}_{d}', functools.partial(f_ssd, d=d, reverse=(d == 1)), (B, nc),
                    [(r['xs'], (None, SSD_CHUNK, D), lambda b, s, ph=ph: (b, ph(s), 0)),
                     (r['bc'], (None, SSD_CHUNK, 512), lambda b, s, ph=ph: (b, ph(s), 0)),
                     (proj, (None, SSD_CHUNK, 128), lambda b, s, ph=ph: (b, ph(s), 36)),
                     vec(r['bias']), vec(r['alog'])],
                    (r['st'][d], (None, None, 8, 128, 128), lambda b, s: (b, nc - 1 - s, 0, 0, 0)),
                    (d_y, (None, SSD_CHUNK, D), lambda b, s, ph=ph: (b, ph(s), 0)),
                    [(0, (B, Tt, D), F32, (None, SSD_CHUNK, D), lambda b, s, ph=ph: (b, ph(s), 0), None),
                     (1, (B, Tt, 512), F32, (None, SSD_CHUNK, 512), lambda b, s, ph=ph: (b, ph(s), 0), None),
                     (2, (B, Tt, 128), F32, (None, SSD_CHUNK, 128), lambda b, s, ph=ph: (b, ph(s), 0), None),
                     (3,) + vec_acc((1, 128)), (4,) + vec_acc((1, 128))],
                    (8, 128, 128), lambda b, s: s == 0)
                d_xs_parts.append(dxs_d)
                d_bc_parts.append(dbc_d)
                d_dt_parts.append(ddt_d)
                dbias_t.append(dbias)
                dalog_t.append(dalog)
                g4, ga4 = linrec_bwd(f'lru_bwd{l}_{d}', r['a4'][d], dh4, r['hp4'][d], 'Fb' if d == 0 else 'Rb')
                du_g, dwa, dba, dwi, dbi, dlam = tile_bwd(
                    f'gates_bwd{l}_{d}', f_gates, (B, nt),
                    [tok(r['u']), vec(W['lru_w_a'][j, d]), vec(W['lru_b_a'][j, d][None]), vec(W['lru_w_i'][j, d]),
                     vec(W['lru_b_i'][j, d][None]), vec(W['lru_lam'][j, d][None])],
                    [[tok(ga4.reshape(B, Tt, D))], [tok(g4.reshape(B, Tt, D))]],
                    [(0,) + tok_out(D) + (None,), (1,) + vec_acc((8, 128, 128)), (2,) + vec_acc((1, D)), (3,) + vec_acc((8, 128, 128)),
                     (4,) + vec_acc((1, D)), (5,) + vec_acc((1, D))])
                d_u_parts.append(du_g)
                add_grad('lru_w_a', (j, d), dwa)
                add_grad('lru_b_a', (j, d), dba[0])
                add_grad('lru_w_i', (j, d), dwi)
                add_grad('lru_b_i', (j, d), dbi[0])
                add_grad('lru_lam', (j, d), dlam[0])
            add_grad('ssd_dt_bias', j, (dbias_t[0] + dbias_t[1])[0, :32].reshape(2, SSD_HEADS))
            add_grad('ssd_a_log', j, (dalog_t[0] + dalog_t[1])[0, :32].reshape(2, SSD_HEADS))
            scw, scb = W['ssd_conv_w'][j], W['ssd_conv_b'][j][None]
            lcw, lcb = W['lru_conv_w'][j], W['lru_conv_b'][j][None]

            def conv_bwd(name, colblk0, w, b, wblk0, ncols, act, parts):
                return tile_bwd(
                    name, functools.partial(f_conv1d, lc=Lc, act=act), (ncols // 256, B),
                    [(proj, (None, Tt, 256), lambda cb, bi: (bi, 0, colblk0 + cb)),
                     (w, (4, 256), lambda cb, bi: (0, wblk0 + cb)), (b, (1, 256), lambda cb, bi: (0, wblk0 + cb))],
                    [[(p, (None, Tt, 256), lambda cb, bi: (bi, 0, cb)) for p in parts]],
                    [(0, (B, Tt, ncols), F32, (None, Tt, 256), lambda cb, bi: (bi, 0, cb), None),
                     (1, (4, ncols), F32, (4, 256), lambda cb, bi: (0, cb), lambda cb, bi: bi == 0),
                     (2, (1, ncols), F32, (1, 256), lambda cb, bi: (0, cb), lambda cb, bi: bi == 0)])
            d_xs_raw, dw_xs, db_xs = conv_bwd(f'conv_xs_bwd{l}', 12, scw, scb, 0, 1024, True, d_xs_parts)
            d_bc_raw, dw_bc, db_bc = conv_bwd(f'conv_bc_bwd{l}', 16, scw, scb, 4, 512, True, d_bc_parts)
            d_u_raw, dw_u, db_u = conv_bwd(f'conv_u_bwd{l}', 8, lcw, lcb, 0, 1024, False, d_u_parts)
            add_grad('ssd_conv_w', j, jnp.concatenate([dw_xs, dw_bc], axis=1))
            add_grad('ssd_conv_b', j, jnp.concatenate([db_xs, db_bc], axis=1)[0])
            add_grad('lru_conv_w', j, dw_u)
            add_grad('lru_conv_b', j, db_u[0])
            def f_ev_dproj(z_, gy_, u_, xs_, bc_, t0, t1):
                pad = jnp.zeros((z_.shape[0], EV_PAD - 4736), F32)
                return jnp.concatenate([z_, gy_, u_, xs_, bc_, t0 + t1, pad], axis=1)
            dproj = tile_fwd(f'ev_dproj{l}', f_ev_dproj, (B, nt),
                             [tok(d_z), tok(d_gy), tok(d_u_raw), tok(d_xs_raw), tok(d_bc_raw), tok(d_dt_parts[0]), tok(d_dt_parts[1])],
                             [tok_out(EV_PAD, BF16)])[0].reshape(M, EV_PAD)
            d_h1 = mm(f'ev_proj_dx{l}', [(dproj, win, 0)], tb=True).reshape(B, Tt, D)
            dwp = mm(f'ev_proj_dw{l}', [(h1m, dproj, 0)], ta=True)
            add_grad('ev_w_in', j, jnp.concatenate([dwp[:, 0:1024], dwp[:, 3072:4640], dwp[:, 1024:3072]], axis=1))
        else:
            wout, win = W['od_w_out'][j], W['od_w_in'][j]
            d_mix = mm(f'mix_out_dx{l}', [(d_o1m, wout, 0)], tb=True).reshape(B, Tt, D)
            add_grad('od_w_out', j, mm(f'mix_out_dw{l}', [(r['mix'].reshape(M, D), d_o1m, 0)], ta=True))
            hn, sd, gw, gb = r['fin_par']
            d_o, d_g, d_yv, d_u_fin, dhn, dsd, dgw, dgb = tile_bwd(
                f'oddfin_bwd{l}', f_oddfin, (B, nt),
                [tok(r['o'][0]), tok(r['o'][1]), tok(proj, HG_W, 4), tok(r['yd'][0]), tok(r['yd'][1]), tok(proj, S5_W, 15),
                 vec(hn), vec(sd), vec(gw), vec(gb)],
                [[tok(d_mix)]],
                [(0,) + tok_out(HG_W) + (None,), (2,) + tok_out(HG_W) + (None,), (3,) + tok_out(S5_W) + (None,),
                 (5,) + tok_out(S5_W) + (None,), (6,) + vec_acc((8, 128)), (7,) + vec_acc((1, S5_W)), (8,) + vec_acc((S5_W, S5_W)),
                 (9,) + vec_acc((1, S5_W))])
            add_grad('hg_norm_g', j, dhn[:HG_HEADS])
            add_grad('s5_d', j, dsd[0])
            add_grad('s5_glu_w', j, dgw)
            add_grad('s5_glu_b', j, dgb[0])
            lbv = lb1 if l == 1 else lb3
            ns = nc
            dq, df, dv, du_s5 = [], [], [], []
            d_ym = d_yv.reshape(M, S5_W)
            dbt_re, dbt_im = [], []
            for d in range(2):
                ph0 = phys_chunk(ns, ncc, d == 1)

                def ph(s, ph0=ph0):
                    return ph0(ns - 1 - s)
                dq_d, df_d, dv_d, dlb_d = scan_bwd(
                    f'hgrn_bwd{l}_{d}', functools.partial(f_hgrn, reverse=(d == 1)), (HG_HEADS, B, ns),
                    [(proj, (None, 128, 128), lambda h, b, s, ph=ph: (b, ph(s), h)),
                     (proj, (None, 128, 128), lambda h, b, s, ph=ph, d=d: (b, ph(s), 6 + 6 * d + h)),
                     (proj, (None, 128, 128), lambda h, b, s, ph=ph: (b, ph(s), 18 + h)),
                     (lbv, (1, 128), lambda h, b, s: (0, h))],
                    (r['zst'][d], (None, None, None, 128, 128), lambda h, b, s: (b, h, ns - 1 - s, 0, 0)),
                    (d_o, (None, 128, 128), lambda h, b, s, ph=ph: (b, ph(s), h)),
                    [(0, (B, Tt, HG_W), F32, (None, 128, 128), lambda h, b, s, ph=ph: (b, ph(s), h), None),
                     (1, (B, Tt, HG_W), F32, (None, 128, 128), lambda h, b, s, ph=ph: (b, ph(s), h), None),
                     (2, (B, Tt, HG_W), F32, (None, 128, 128), lambda h, b, s, ph=ph: (b, ph(s), h), None),
                     (3, (1, HG_W), F32, (1, 128), lambda h, b, s: (0, h), lambda h, b, s: (b == 0) & (s == 0))],
                    (128, 128), lambda h, b, s: s == 0)
                dq.append(dq_d)
                df.append(df_d)
                dv.append(dv_d)
                dlb[l].append(dlb_d)
                d_hcat = mm(f's5_out_dx{l}_{d}', [(d_ym, r['ccat'][d], 0)], tb=True)
                dccat = mm(f's5_out_dw{l}_{d}', [(r['hcat'][d], d_ym, 0)], ta=True)
                add_grad('s5_c_re', (j, d), _unblockdiag_t(dccat[:D]))
                add_grad('s5_c_im', (j, d), -_unblockdiag_t(dccat[D:]))
                g5, dcoef = clinrec_bwd(f's5_bwd{l}_{d}', d_hcat.reshape(B, Tt, 2, 8, 128), r['hp5'][d], r['coef'][d],
                                        'Fb' if d == 0 else 'Rb')
                gcat = g5.reshape(M, 2 * D)
                dbcat = mm(f's5_in_dw{l}_{d}', [(r['u2'], gcat, 0)], ta=True)
                du_s5.append(mm(f's5_in_dx{l}_{d}', [(gcat, r['bcat'][d], 0)], tb=True))
                cts5 = [dcoef[0].reshape(16, 64), dcoef[1].reshape(16, 64), _unblockdiag(dbcat[:, :D]), _unblockdiag(dbcat[:, D:])]
                dlre, dlim, dlst, dbtr, dbti = tile_bwd(
                    f's5p_bwd{l}_{d}', f_s5p, (1,), [vec(t) for t in r['s5in'][d]], [[vec(t)] for t in cts5],
                    [(i, t.shape, F32, t.shape, (lambda *ids, n=t.ndim: (0,) * n), None) for i, t in enumerate(r['s5in'][d])])
                add_grad('s5_lam_re', (j, d), dlre)
                add_grad('s5_lam_im', (j, d), dlim)
                add_grad('s5_log_step', (j, d), dlst[:, 0])
                dbt_re.append(dbtr)
                dbt_im.append(dbti)
            add_grad('s5_b_re', j, jnp.swapaxes(dbt_re[0] + dbt_re[1], 1, 2))
            add_grad('s5_b_im', j, jnp.swapaxes(dbt_im[0] + dbt_im[1], 1, 2))
            def f_od_dproj(q0, q1, f0, f1, v0, v1, g_, u0, u1, u2):
                return jnp.concatenate([q0 + q1, f0, f1, v0 + v1, g_, u0 + u1 + u2], axis=1)
            parts = [dq[0], dq[1], df[0], df[1], dv[0], dv[1], d_g, d_u_fin, du_s5[0].reshape(B, Tt, S5_W),
                     du_s5[1].reshape(B, Tt, S5_W)]
            dproj = tile_fwd(f'od_dproj{l}', f_od_dproj, (B, nt), [tok(t) for t in parts],
                             [tok_out(4096, BF16)])[0].reshape(M, 4096)
            d_h1 = mm(f'od_proj_dx{l}', [(dproj, win, 0)], tb=True).reshape(B, Tt, D)
            add_grad('od_w_in', j, mm(f'od_proj_dw{l}', [(h1m, dproj, 0)], ta=True))
        ng = W['norm_mix_g'][l][None]
        if l == 0:
            d_x0, dng, dsh, dsc = tile_bwd(
                'nm0_bwd', lambda xv, g, sh, sc: (xv, f_nm0(xv, g, sh, sc)), (B, nt),
                [tok(r['xin']), vec(ng), modv(l, 0), modv(l, 1)], [[tok(d_xa)], [tok(d_h1)]],
                [(0,) + tok_out(D) + (None,), (1,) + vec_acc((1, D)), (2,) + dmod_spec, (3,) + dmod_spec])
        else:
            d_xp, d_o2, dgate, dng, dsh, dsc = tile_bwd(
                f'nm_mix_bwd{l}', f_nm, (B, nt),
                [tok(r['xin']), tok(r['oin']), modv(l - 1, 5), vec(ng), modv(l, 0), modv(l, 1)],
                [[tok(d_xa)], [tok(d_h1)]],
                [(0,) + tok_out(D) + (None,), (1,) + tok_out(D, BF16) + (None,), (2,) + dmod_spec, (3,) + vec_acc((1, D)),
                 (4,) + dmod_spec, (5,) + dmod_spec])
            dmods[(l - 1, 5)] = dgate
        add_grad('norm_mix_g', l, dng[0])
        dmods[(l, 0)], dmods[(l, 1)] = dsh, dsc

    grad_x = d_x0[:, Lc:, :]

    (dlogits,) = tile_bwd('lb_bwd', f_lb, (1,), [vec(W['hg_lb_logits'])],
                          [[vec(t) for t in dlb[1]], [vec(t) for t in dlb[3]]],
                          [(0, (DEPTH, HG_W), F32, (DEPTH, HG_W), lambda i: (0, 0), None)])
    add_grad('hg_lb_logits', None, dlogits)

    dm = jnp.stack([jnp.stack([dmods[(l, w)] for w in range(N_MOD)]) for l in range(DEPTH)])
    dlat = jnp.transpose(dm[:, :, :, 1, 0, :], (0, 2, 1, 3)).reshape(DEPTH, B, N_MOD * D)
    dctx = jnp.transpose(dm[:, :, :, 0, 0, :], (0, 2, 1, 3)).reshape(DEPTH, B, N_MOD * D)
    dlat = jnp.zeros((DEPTH, 8, N_MOD * D), F32).at[:, :B].set(dlat)
    dctx = jnp.zeros((DEPTH, 8, N_MOD * D), F32).at[:, :B].set(dctx)

    def mod_bwd_body(cc_ref, w_ref, dl_ref, dc_ref, dw_ref, db_ref, dcc_ref):
        row = lax.broadcasted_iota(jnp.int32, (8, 1), 0)
        dall = dl_ref[...] + jnp.where(row == 4, jnp.sum(dc_ref[...], axis=0, keepdims=True), 0.0)
        s, vjp = jax.vjp(jax.nn.silu, cc_ref[...])
        db16 = dall.astype(BF16)
        dw_ref[...] = lax.dot_general(s.astype(BF16), db16, (((0,), (0,)), ((), ())), preferred_element_type=F32)
        db_ref[...] = jnp.sum(dall, axis=0, keepdims=True)
        ds = lax.dot_general(db16, w_ref[...], (((1,), (1,)), ((), ())), preferred_element_type=F32)
        (dcc,) = vjp(ds)
        first = (pl.program_id(0) == 0) & (pl.program_id(1) == 0)

        @pl.when(first)
        def _():
            dcc_ref[...] = dcc

        @pl.when(jnp.logical_not(first))
        def _():
            dcc_ref[...] += dcc

    dwmod, dbmod, dcc = pl.pallas_call(
        mod_bwd_body, grid=(DEPTH, nmc),
        in_specs=[pl.BlockSpec((8, D), lambda l, n: (0, 0)), pl.BlockSpec((None, D, 1536), lambda l, n: (l, 0, n)),
                  pl.BlockSpec((None, 8, 1536), lambda l, n: (l, 0, n)), pl.BlockSpec((None, 8, 1536), lambda l, n: (l, 0, n))],
        out_specs=[pl.BlockSpec((None, D, 1536), lambda l, n: (l, 0, n)), pl.BlockSpec((None, 1, 1536), lambda l, n: (l, 0, n)),
                   pl.BlockSpec((8, D), lambda l, n: (0, 0))],
        out_shape=[jax.ShapeDtypeStruct((DEPTH, D, N_MOD * D), F32), jax.ShapeDtypeStruct((DEPTH, 1, N_MOD * D), F32),
                   jax.ShapeDtypeStruct((8, D), F32)],
        name='mod_bwd', compiler_params=_cparams(("arbitrary", "arbitrary")),
    )(cc, W['w_mod'], dlat, dctx)
    add_grad('w_mod', None, dwmod)
    add_grad('b_mod', None, dbmod[:, 0])
    add_grad('c_ctx', None, dcc[4])
    return loss, grad_x, G


def assemble_grads(G, like):
    out = {}
    for name, parts in G.items():
        shape = like[name].shape
        if None in parts:
            g = parts[None]
        elif isinstance(next(iter(parts)), tuple):
            g = jnp.stack([jnp.stack([parts[(j, d)] for d in range(2)]) for j in range(shape[0])])
        else:
            g = jnp.stack([parts[i] for i in range(shape[0])])
        out[name] = g.reshape(shape)
    return out


XY_RELS = ((1, 0, 0), (0, 1, 0), (1, 1, 0))
ALL_RELS = tuple((dx, dy, dc) for dx in (0, 1) for dy in (0, 1) for dc in (0, 1))[1:]


def exchange(name, src, out_shape, sends, local=None):
    n = len(sends)

    def body(src_ref, out_ref, send_sems, recv_sems, local_sem):
        me = (lax.axis_index("x"), lax.axis_index("y"), lax.axis_index("c"))

        def pick(ref, sel, tgt):
            return ref if sel is None else ref.at[sel(me, tgt)]

        copies = []
        for k, (rel, ssel, dsel) in enumerate(sends):
            tgt = tuple(1 - m if f else m for m, f in zip(me, rel))
            cp = pltpu.make_async_remote_copy(
                src_ref=pick(src_ref, ssel, tgt), dst_ref=pick(out_ref, dsel, tgt),
                send_sem=send_sems.at[k], recv_sem=recv_sems.at[k], device_id=tgt, device_id_type=MESH)
            cp.start()
            copies.append(cp)
        if local is not None:
            lc = pltpu.make_async_copy(pick(src_ref, local[0], me), pick(out_ref, local[1], me), local_sem)
            lc.start()
        for cp in copies:
            cp.wait()
        if local is not None:
            lc.wait()

    return pl.pallas_call(
        body, out_shape=jax.ShapeDtypeStruct(out_shape, src.dtype),
        in_specs=[pl.BlockSpec(memory_space=pl.ANY)], out_specs=pl.BlockSpec(memory_space=pl.ANY),
        scratch_shapes=[pltpu.SemaphoreType.DMA((n,)), pltpu.SemaphoreType.DMA((n,)), pltpu.SemaphoreType.DMA(())],
        name=name,
    )(src)


def _xy_index(dev):
    return 2 * dev[0] + dev[1]


def all_gather_xy(name, shard):
    return exchange(name, shard, (4,) + shard.shape,
                    [(rel, None, lambda me, tgt: _xy_index(me)) for rel in XY_RELS],
                    local=(None, lambda me, tgt: _xy_index(me)))


def reduce_scatter_xy(name, g4):
    return exchange(name, g4, g4.shape,
                    [(rel, (lambda me, tgt: _xy_index(tgt)), (lambda me, tgt, k=k: k)) for k, rel in enumerate(XY_RELS)],
                    local=((lambda me, tgt: _xy_index(me)), (lambda me, tgt: 3)))


def sibling_swap(name, v):
    return exchange(name, v, v.shape, [((0, 0, 1), None, None)])


def all_gather_all(name, v):
    idx = lambda me, tgt: 4 * me[0] + 2 * me[1] + me[2]
    return exchange(name, v, (8,) + v.shape, [(rel, None, idx) for rel in ALL_RELS], local=(None, idx))


def all_gather_xy_halves(name, shard):
    half = shard.shape[0] // 2

    def body(src_ref, out_ref, send_sems, recv_sems, local_sem):
        x, y, c = lax.axis_index("x"), lax.axis_index("y"), lax.axis_index("c")
        mine = pl.ds(c * half, half)
        peers = [(1 - x, y), (x, 1 - y), (1 - x, 1 - y)]

        def copy(k, src, dst, to):
            return pltpu.make_async_remote_copy(src_ref=src, dst_ref=dst, send_sem=send_sems.at[k], recv_sem=recv_sems.at[k],
                                                device_id=to, device_id_type=MESH)

        lc = pltpu.make_async_copy(src_ref, out_ref.at[2 * x + y], local_sem)
        lc.start()
        first = [copy(k, src_ref.at[mine], out_ref.at[2 * x + y, mine], (px, py, c)) for k, (px, py) in enumerate(peers)]
        for cp in first:
            cp.start()
        passed = []
        for k, (px, py) in enumerate(peers):
            first[k].wait_recv()
            landed = out_ref.at[2 * px + py, mine]
            fw = copy(3 + k, landed, landed, (x, y, 1 - c))
            fw.start()
            passed.append(fw)
        for fw in passed:
            fw.wait_recv()
        for cp in first + passed:
            cp.wait_send()
        lc.wait()

    return pl.pallas_call(
        body, out_shape=jax.ShapeDtypeStruct((4,) + shard.shape, shard.dtype),
        in_specs=[pl.BlockSpec(memory_space=pl.ANY)], out_specs=pl.BlockSpec(memory_space=pl.ANY),
        scratch_shapes=[pltpu.SemaphoreType.DMA((6,)), pltpu.SemaphoreType.DMA((6,)), pltpu.SemaphoreType.DMA(())],
        name=name,
    )(shard)


def sibling_split(name, g4):
    half = g4.shape[1] // 2
    return exchange(name, g4, (2, 4, half) + g4.shape[2:],
                    [((0, 0, 1), (lambda me, tgt: (slice(None), pl.ds(tgt[2] * half, half))), (lambda me, tgt: 1))],
                    local=((lambda me, tgt: (slice(None), pl.ds(me[2] * half, half))), (lambda me, tgt: 0)))


def sibling_join(name, q):
    half = q.shape[0]
    sel = lambda me, tgt: pl.ds(me[2] * half, half)
    return exchange(name, q, (2 * half,) + q.shape[1:], [((0, 0, 1), None, sel)], local=(None, sel))


def _rows_view(shape):
    cols = shape[-1] if len(shape) else 1
    rows = 1
    for s in shape[:-1]:
        rows *= s
    return rows, cols


def _row_block(rows, cols, n_arrays):
    budget = (24 * 1024 * 1024) // (8 * n_arrays * cols)
    if rows <= max(budget, 16):
        return rows
    br = (min(budget, rows) // 16) * 16
    while br > 16 and rows % br:
        br -= 16
    return br if rows % br == 0 else rows


def sum_slots(name, stacked, out_dtype=F32):
    k = stacked.shape[0]
    rows, cols = _rows_view(stacked.shape[1:])
    br = _row_block(rows, cols, k + 1)

    def f(s):
        parts = [s[i].astype(F32) for i in range(k)]
        while len(parts) > 1:
            parts = [parts[i] + parts[i + 1] for i in range(0, len(parts), 2)]
        return parts[0]

    out = tile_fwd(name, f, (rows // br,), [(stacked.reshape(k, rows, cols), (k, br, cols), lambda i: (0, i, 0))],
                   [((rows, cols), out_dtype, (br, cols), lambda i: (i, 0))])[0]
    return out.reshape(stacked.shape[1:])


def adamw(name, w, m, v, gs):
    rows, cols = _rows_view(w.shape)
    br = _row_block(rows, cols, 7 + len(gs))
    spec = lambda a: (a.reshape(rows, cols), (br, cols), lambda i: (i, 0))
    outs = tile_fwd(name, f_adamw, (rows // br,), [spec(t) for t in (w, m, v) + tuple(gs)],
                    [((rows, cols), F32, (br, cols), lambda i: (i, 0))] * 4)
    return [o.reshape(w.shape) for o in outs]


IN_NAMES = ['x', 'c', 'ctx'] + W_NAMES + ['loss_target'] + ['m_' + n for n in W_NAMES] + ['v_' + n for n in W_NAMES]
SMALL_PAD = 128 * 1024


def kernel(x, c, ctx, c_ctx, w_mod, b_mod, norm_mix_g, norm_ffn_g, final_norm_g, ev_w_in, ev_w_out, ssd_conv_w, ssd_conv_b, ssd_dt_bias, ssd_a_log, ssd_d, ssd_norm_g, lru_conv_w, lru_conv_b, lru_w_a, lru_b_a, lru_w_i, lru_b_i, lru_lam, od_w_in, od_w_out, hg_lb_logits, hg_norm_g, s5_lam_re, s5_lam_im, s5_log_step, s5_b_re, s5_b_im, s5_c_re, s5_c_im, s5_d, s5_glu_w, s5_glu_b, ffn_w_gate, ffn_w_up, ffn_conv_w, ffn_conv_b, ffn_w_down, loss_target, m_c_ctx, m_w_mod, m_b_mod, m_norm_mix_g, m_norm_ffn_g, m_final_norm_g, m_ev_w_in, m_ev_w_out, m_ssd_conv_w, m_ssd_conv_b, m_ssd_dt_bias, m_ssd_a_log, m_ssd_d, m_ssd_norm_g, m_lru_conv_w, m_lru_conv_b, m_lru_w_a, m_lru_b_a, m_lru_w_i, m_lru_b_i, m_lru_lam, m_od_w_in, m_od_w_out, m_hg_lb_logits, m_hg_norm_g, m_s5_lam_re, m_s5_lam_im, m_s5_log_step, m_s5_b_re, m_s5_b_im, m_s5_c_re, m_s5_c_im, m_s5_d, m_s5_glu_w, m_s5_glu_b, m_ffn_w_gate, m_ffn_w_up, m_ffn_conv_w, m_ffn_conv_b, m_ffn_w_down, v_c_ctx, v_w_mod, v_b_mod, v_norm_mix_g, v_norm_ffn_g, v_final_norm_g, v_ev_w_in, v_ev_w_out, v_ssd_conv_w, v_ssd_conv_b, v_ssd_dt_bias, v_ssd_a_log, v_ssd_d, v_ssd_norm_g, v_lru_conv_w, v_lru_conv_b, v_lru_w_a, v_lru_b_a, v_lru_w_i, v_lru_b_i, v_lru_lam, v_od_w_in, v_od_w_out, v_hg_lb_logits, v_hg_norm_g, v_s5_lam_re, v_s5_lam_im, v_s5_log_step, v_s5_b_re, v_s5_b_im, v_s5_c_re, v_s5_c_im, v_s5_d, v_s5_glu_w, v_s5_glu_b, v_ffn_w_gate, v_ffn_w_up, v_ffn_conv_w, v_ffn_conv_b, v_ffn_w_down):
    a = dict(locals())
    W = {}
    for n in W_NAMES:
        w = a[n]
        if n in SHARD_AXIS:
            ax = SHARD_AXIS[n]
            if n in MATMUL_WEIGHTS:
                g4 = all_gather_xy_halves('ag_' + n, w.astype(BF16))
            else:
                g4 = all_gather_xy('ag_' + n, w)
            shape = list(w.shape)
            shape[ax] *= 4
            W[n] = jnp.moveaxis(g4, 0, ax).reshape(shape)
        else:
            W[n] = w
    e = W['ev_w_in']
    W['ev_w_in_p'] = jnp.concatenate(
        [e[:, :, 0:1024], e[:, :, 2592:3616], e[:, :, 3616:4640], e[:, :, 1024:2560], e[:, :, 2560:2592],
         jnp.zeros((e.shape[0], D, EV_PAD - 4640), e.dtype)], axis=2)

    loss_local, grad_x, G = local_step(a['x'], a['c'], a['ctx'], W['c_ctx'], a['loss_target'], W)
    grads = assemble_grads(G, W)
    loss = lax.psum(loss_local, ("x", "y", "c"))

    res = {}
    for n in W_NAMES:
        if n not in SHARD_AXIS:
            continue
        ax = SHARD_AXIS[n]
        w = a[n]
        gf = grads[n]
        split = gf.reshape(gf.shape[:ax] + (4, w.shape[ax]) + gf.shape[ax + 1:])
        g4 = jnp.moveaxis(split, ax, 0)
        if n in MATMUL_WEIGHTS:
            part = sum_slots('csum_' + n, sibling_split('rsc_' + n, g4), BF16)
            half = sum_slots('gsum_' + n, reduce_scatter_xy('rs_' + n, part))
            res[n] = adamw('adamw_' + n, w, a['m_' + n], a['v_' + n], (sibling_join('agc_' + n, half),))
        else:
            mine = sum_slots('gsum_' + n, reduce_scatter_xy('rs_' + n, g4))
            other = sibling_swap('sw_' + n, mine)
            res[n] = adamw('adamw_' + n, w, a['m_' + n], a['v_' + n], (mine, other))
    small = [n for n in W_NAMES if n not in SHARD_AXIS]
    flat = jnp.concatenate([grads[n].reshape(-1) for n in small])
    total = flat.shape[0]
    padded = -(-total // SMALL_PAD) * SMALL_PAD
    flat = jnp.concatenate([flat, jnp.zeros((padded - total,), F32)]).reshape(padded // 128, 128)
    summed = sum_slots('gsum_small', all_gather_all('ag_small', flat)).reshape(-1)
    off = 0
    for n in small:
        size = math.prod(a[n].shape)
        g = summed[off:off + size].reshape(a[n].shape)
        off += size
        res[n] = adamw('adamw_' + n, a[n], a['m_' + n], a['v_' + n], (g,))
    outs = [loss, grad_x]
    for k in range(4):
        outs += [res[n][k] for n in W_NAMES]
    return tuple(outs)
```

```python
import functools
import math

import jax
import jax.numpy as jnp
from jax import lax
from jax.experimental import pallas as pl
from jax.experimental.pallas import tpu as pltpu

F32 = jnp.float32
BF16 = jnp.bfloat16
HI = lax.Precision.HIGHEST
MESH = pl.DeviceIdType.MESH

D = 1024
DEPTH = 4
N_MOD = 6
RMS_EPS = 1e-6
GRID_W = 64
SSD_HEADS = 16
SSD_CHUNK = 128
HG_W = 768
HG_HEADS = 6
HG_CHUNK = 16
HG_PER_STEP = 3
HG_GROUPS = HG_HEADS // HG_PER_STEP
HG_GW = 128 * HG_PER_STEP
S5_W = 256
D_FF = 2816
EV_PAD = 5120
LRU_C = 8.0
V7X_VMEM_LIMIT = 56 * 1024 * 1024
MM_VMEM_BUDGET = 36 * 1024 * 1024

ADAM_LR, ADAM_B1, ADAM_B2, ADAM_EPS, ADAM_WD, ADAM_STEP = 0.001, 0.9, 0.999, 1e-08, 0.01, 10

W_NAMES = ['c_ctx', 'w_mod', 'b_mod', 'norm_mix_g', 'norm_ffn_g', 'final_norm_g', 'ev_w_in', 'ev_w_out', 'ssd_conv_w',
           'ssd_conv_b', 'ssd_dt_bias', 'ssd_a_log', 'ssd_d', 'ssd_norm_g', 'lru_conv_w', 'lru_conv_b', 'lru_w_a', 'lru_b_a',
           'lru_w_i', 'lru_b_i', 'lru_lam', 'od_w_in', 'od_w_out', 'hg_lb_logits', 'hg_norm_g', 's5_lam_re', 's5_lam_im',
           's5_log_step', 's5_b_re', 's5_b_im', 's5_c_re', 's5_c_im', 's5_d', 's5_glu_w', 's5_glu_b', 'ffn_w_gate', 'ffn_w_up',
           'ffn_conv_w', 'ffn_conv_b', 'ffn_w_down']
SHARD_AXIS = {'w_mod': 2, 'ev_w_in': 2, 'ev_w_out': 1, 'ssd_conv_w': 2, 'lru_conv_w': 2, 'lru_b_a': 2, 'lru_b_i': 2,
              'lru_lam': 2, 'od_w_in': 2, 'od_w_out': 1, 's5_d': 1, 's5_glu_w': 1, 's5_glu_b': 1, 'ffn_w_gate': 2,
              'ffn_w_up': 2, 'ffn_conv_w': 3, 'ffn_w_down': 1}
MATMUL_WEIGHTS = ('w_mod', 'ev_w_in', 'ev_w_out', 'od_w_in', 'od_w_out', 'ffn_w_gate', 'ffn_w_up', 'ffn_w_down')


def _cparams(sem=None):
    return pltpu.CompilerParams(vmem_limit_bytes=V7X_VMEM_LIMIT, dimension_semantics=sem)


def _pick(n, cands):
    for c in cands:
        if n % c == 0:
            return c
    return n


def tile_fwd(name, f, grid, ins, outs):
    n_in = len(ins)

    def body(*refs):
        res = f(*[r[...] for r in refs[:n_in]])
        if not isinstance(res, (tuple, list)):
            res = (res,)
        for r, o in zip(res, refs[n_in:]):
            o[...] = r.astype(o.dtype)

    res = pl.pallas_call(
        body, grid=grid,
        in_specs=[pl.BlockSpec(b, m) for _, b, m in ins],
        out_specs=[pl.BlockSpec(b, m) for _, _, b, m in outs],
        out_shape=[jax.ShapeDtypeStruct(s, d) for s, d, _, _ in outs],
        name=name, compiler_params=_cparams(("arbitrary",) * len(grid)),
    )(*[a for a, _, _ in ins])
    return res


def tile_bwd(name, f, grid, ins, cts, grads, prims=()):
    n_in = len(ins)
    ct_flat = [p for c in cts for p in c]
    n_ct = len(ct_flat)
    didx = [g[0] for g in grads]

    def body(*refs):
        in_refs, ct_refs = refs[:n_in], refs[n_in:n_in + n_ct]
        g_refs = refs[n_in + n_ct:n_in + n_ct + len(grads)]
        p_refs = refs[n_in + n_ct + len(grads):]
        vals = [r[...] for r in in_refs]

        def fd(*dv):
            full = list(vals)
            for i, v in zip(didx, dv):
                full[i] = v
            res = f(*full)
            return tuple(res) if isinstance(res, (tuple, list)) else (res,)

        out, vjp = jax.vjp(fd, *[vals[i] for i in didx])
        ctv, k = [], 0
        for o, c in zip(out, cts):
            acc = None
            for _ in c:
                piece = ct_refs[k][...].astype(o.dtype)
                acc = piece if acc is None else acc + piece
                k += 1
            ctv.append(jnp.zeros_like(o) if acc is None else acc.reshape(o.shape))
        gs = vjp(tuple(ctv))
        ids = [pl.program_id(a) for a in range(len(grid))]

        def emit(ref, val, first):
            if first is None:
                ref[...] = val.astype(ref.dtype)
            else:
                is_first = first(*ids)

                @pl.when(is_first)
                def _():
                    ref[...] = val.astype(ref.dtype)

                @pl.when(jnp.logical_not(is_first))
                def _():
                    ref[...] += val.astype(ref.dtype)

        for g, spec, ref in zip(gs, grads, g_refs):
            emit(ref, g, spec[5])
        for spec, ref in zip(prims, p_refs):
            emit(ref, out[spec[0]], spec[5])

    specs = list(grads) + list(prims)
    res = pl.pallas_call(
        body, grid=grid,
        in_specs=[pl.BlockSpec(b, m) for _, b, m in list(ins) + ct_flat],
        out_specs=[pl.BlockSpec(s[3], s[4]) for s in specs],
        out_shape=[jax.ShapeDtypeStruct(s[1], s[2]) for s in specs],
        name=name, compiler_params=_cparams(("arbitrary",) * len(grid)),
    )(*[a for a, _, _ in list(ins) + ct_flat])
    return res


def mm(name, pairs, ta=False, tb=False, out_dtype=F32):
    a0, b0, _ = pairs[0]
    m = a0.shape[1] if ta else a0.shape[0]
    n = b0.shape[0] if tb else b0.shape[1]
    cands = (1024, 1408, 768, 512, 256, 128)
    tks, nks = [], []
    for a, b, _ in pairs:
        k = a.shape[0] if ta else a.shape[1]
        tk = _pick(k, cands)
        tks.append(tk)
        nks.append(k // tk)

    def vmem_bytes(tm, tn):
        tiles = sum(2 * tk * (tm * a.dtype.itemsize + tn * b.dtype.itemsize) for (a, b, _), tk in zip(pairs, tks))
        return tiles + tm * tn * (4 + 2 * jnp.dtype(out_dtype).itemsize)

    tm_c = [c_ for c_ in cands if m % c_ == 0] or [m]
    tn_c = [c_ for c_ in cands if n % c_ == 0] or [n]
    tm, tn = tm_c[0], tn_c[0]
    while vmem_bytes(tm, tn) > MM_VMEM_BUDGET and (len(tm_c) > 1 or len(tn_c) > 1):
        if len(tm_c) > 1 and (tm >= tn or len(tn_c) == 1):
            tm_c = tm_c[1:]
        else:
            tn_c = tn_c[1:]
        tm, tn = tm_c[0], tn_c[0]
    starts = [sum(nks[:p]) for p in range(len(pairs))]
    nk = sum(nks)
    np_ = len(pairs)

    def body(*refs):
        o_ref, acc = refs[2 * np_], refs[2 * np_ + 1]
        kk = pl.program_id(2)

        @pl.when(kk == 0)
        def _():
            acc[...] = jnp.zeros_like(acc)

        for p in range(np_):
            def add(p=p):
                a = refs[2 * p][...].astype(BF16)
                b = refs[2 * p + 1][...].astype(BF16)
                dn = (((0 if ta else 1,), (1 if tb else 0,)), ((), ()))
                acc[...] += lax.dot_general(a, b, dn, preferred_element_type=F32)
            if np_ == 1:
                add()
            else:
                pl.when((kk >= starts[p]) & (kk < starts[p] + nks[p]))(add)

        @pl.when(kk == nk - 1)
        def _():
            o_ref[...] = acc[...].astype(o_ref.dtype)

    in_specs, args = [], []
    for p, (a, b, off) in enumerate(pairs):
        tk, s0, nkp = tks[p], starts[p], nks[p]
        assert off % tk == 0
        boff = off // tk

        def kloc(k, s0=s0, nkp=nkp):
            return jnp.clip(k - s0, 0, nkp - 1)
        if ta:
            in_specs.append(pl.BlockSpec((tk, tm), lambda i, j, k, kloc=kloc: (kloc(k), i)))
        else:
            in_specs.append(pl.BlockSpec((tm, tk), lambda i, j, k, kloc=kloc: (i, kloc(k))))
        if tb:
            in_specs.append(pl.BlockSpec((tn, tk), lambda i, j, k, kloc=kloc, boff=boff: (j, boff + kloc(k))))
        else:
            in_specs.append(pl.BlockSpec((tk, tn), lambda i, j, k, kloc=kloc, boff=boff: (boff + kloc(k), j)))
        args += [a, b]
    return pl.pallas_call(
        body, grid=(m // tm, n // tn, nk), in_specs=in_specs,
        out_specs=pl.BlockSpec((tm, tn), lambda i, j, k: (i, j)),
        out_shape=jax.ShapeDtypeStruct((m, n), out_dtype),
        scratch_shapes=[pltpu.VMEM((tm, tn), F32)],
        name=name, compiler_params=_cparams(("arbitrary", "arbitrary", "arbitrary")),
    )(*args)


def _rms(x, g):
    return x * lax.rsqrt(jnp.mean(x * x, axis=-1, keepdims=True) + RMS_EPS) * g


def f_nm0(x, g, sh, sc):
    return _rms(x, g) * (1.0 + sc) + sh


def f_nm(xp, o, gate, g, sh, sc):
    x = xp + gate * o
    return x, _rms(x, g) * (1.0 + sc) + sh


def f_final(xp, o, gate, g, tgt, valid):
    x = xp + gate * o
    e = (_rms(x, g) - tgt) * valid
    return jnp.sum(e * e, axis=0, keepdims=True) * (0.5 / D)


@functools.partial(jax.custom_vjp, nondiff_argnums=(1,))
def _sroll(x, s):
    return pltpu.roll(x, s, 0)


def _sroll_fwd(x, s):
    return pltpu.roll(x, s, 0), None


def _sroll_bwd(s, _, g):
    return (pltpu.roll(g, (g.shape[0] - s) % g.shape[0], 0),)


_sroll.defvjp(_sroll_fwd, _sroll_bwd)


def _shifted(x, o):
    n = x.shape[0]
    return x if o == 0 else _sroll(x, (n - o) % n)


def f_conv1d(x, w, b, *, lc, act):
    n = x.shape[0]
    pos = lax.broadcasted_iota(jnp.int32, (n, 1), 0)
    lo = jnp.where(pos < lc, 0, lc)
    hi = jnp.where(pos < lc, lc, n)
    y = x * w[1:2] + b
    for k, o in ((0, -1), (2, 1), (3, 2)):
        src = pos + o
        valid = (src >= lo) & (src < hi)
        y = y + jnp.where(valid, _shifted(x, o), 0.0) * w[k:k + 1]
    return jax.nn.silu(y) if act else y


def ffnconv_masks(n, lc):
    pos = lax.broadcasted_iota(jnp.int32, (n, 128), 0)
    is_ctx = pos < lc
    tl = pos - lc
    r = tl // GRID_W
    cc = tl - r * GRID_W
    rows = (n - lc) // GRID_W
    left = jnp.where(is_ctx, pos >= 1, cc >= 1)
    right = jnp.where(is_ctx, pos < lc - 1, cc < GRID_W - 1)
    above = jnp.logical_not(is_ctx) & (r >= 1)
    below = jnp.logical_not(is_ctx) & (r < rows - 1)
    return jnp.stack([left, right, above, below]).astype(F32)


def f_ffnconv(a, up, w, b, mk):
    cols = (mk[0] * _shifted(a, -1), a, mk[1] * _shifted(a, 1))
    y = b
    for dr in (-1, 0, 1):
        k = 3 * (dr + 1)
        inner = cols[0] * w[k:k + 1] + cols[1] * w[k + 1:k + 2] + cols[2] * w[k + 2:k + 3]
        y = y + (inner if dr == 0 else mk[2 + (dr > 0)] * _shifted(inner, GRID_W * dr))
    return jax.nn.silu(y) * up


def f_ssd(xs, bc, dtraw, bias, alog, st, *, d, reverse):
    L = xs.shape[0]
    dtv = jax.nn.softplus(dtraw + bias)
    la = dtv * (-jnp.exp(alog))
    ri = lax.broadcasted_iota(jnp.int32, (L, L), 0)
    ci = lax.broadcasted_iota(jnp.int32, (L, L), 1)
    mask = (ci >= ri) if reverse else (ci <= ri)
    cum = jnp.dot(mask.astype(F32), la, precision=HI, preferred_element_type=F32)
    cum_t = cum.T
    tot = cum[0:1] if reverse else cum[L - 1:L]
    lo = lax.broadcasted_iota(jnp.int32, (1, 128), 1) < 64
    rlo = lax.broadcasted_iota(jnp.int32, (128, 1), 0) < 64
    ys, new = [], []
    cbs = {}
    for j in range(8):
        g = j // 4
        bg = bc[:, g * 128:(g + 1) * 128].astype(BF16)
        cg = bc[:, 256 + g * 128:256 + (g + 1) * 128].astype(BF16)
        if g not in cbs:
            cbs[g] = lax.dot_general(cg, bg, (((1,), (1,)), ((), ())), preferred_element_type=F32)
        cb = cbs[g]
        x = xs[:, j * 128:(j + 1) * 128]
        k1 = 16 * d + 2 * j
        k2 = k1 + 1
        c1, c2 = cum[:, k1:k1 + 1], cum[:, k2:k2 + 1]
        m1 = cb * jnp.exp(jnp.where(mask, c1 - cum_t[k1:k1 + 1, :], -1e30))
        m2 = cb * jnp.exp(jnp.where(mask, c2 - cum_t[k2:k2 + 1, :], -1e30))
        xdt = x * jnp.where(lo, dtv[:, k1:k1 + 1], dtv[:, k2:k2 + 1])
        xb = xdt.astype(BF16)
        y = jnp.where(lo, jnp.dot(m1.astype(BF16), xb, preferred_element_type=F32),
                      jnp.dot(m2.astype(BF16), xb, preferred_element_type=F32))
        sj = st[j]
        ch = lax.dot_general(cg, sj.astype(BF16), (((1,), (1,)), ((), ())), preferred_element_type=F32)
        y = y + ch * jnp.where(lo, jnp.exp(c1), jnp.exp(c2))
        t1, t2 = tot[:, k1:k1 + 1], tot[:, k2:k2 + 1]
        xe = (xdt * jnp.where(lo, jnp.exp(t1 - c1), jnp.exp(t2 - c2))).astype(BF16)
        upd = lax.dot_general(xe, bg, (((0,), (0,)), ((), ())), preferred_element_type=F32)
        new.append(sj * jnp.where(rlo, jnp.exp(t1), jnp.exp(t2)) + upd)
        ys.append(y)
    return jnp.concatenate(ys, axis=1), jnp.stack(new)


def f_hgrn(q_raw, f_raw, v, lb, zt, *, reverse):
    n = q_raw.shape[0]
    c = HG_CHUNK
    qa = jax.nn.silu(q_raw)
    logf = jnp.log(lb + (1.0 - lb) * jax.nn.sigmoid(f_raw))
    kk = (1.0 - lb) * jax.nn.sigmoid(-f_raw)
    ri = lax.broadcasted_iota(jnp.int32, (n, n), 0)
    ci = lax.broadcasted_iota(jnp.int32, (n, n), 1)
    tmat = ((ri // c == ci // c) & ((ci >= ri) if reverse else (ci <= ri))).astype(F32)
    cum_all = jnp.dot(tmat, logf, precision=HI, preferred_element_type=F32)
    r3 = lax.broadcasted_iota(jnp.int32, (c, c, 128), 0)
    c3 = lax.broadcasted_iota(jnp.int32, (c, c, 128), 1)
    mask3 = (c3 >= r3) if reverse else (c3 <= r3)
    nch = n // c
    outs = [None] * nch
    for chn in (reversed(range(nch)) if reverse else range(nch)):
        sl = slice(chn * c, (chn + 1) * c)
        q, k, vv, cum = qa[sl], kk[sl], v[sl], cum_all[sl]
        dec = jnp.exp(jnp.where(mask3, cum[:, None, :] - cum[None, :, :], -1e30))
        att = jnp.sum(q[:, None, :] * dec * k[None, :, :], axis=-1)
        y = jnp.dot(att.astype(BF16), vv.astype(BF16), preferred_element_type=F32)
        y = y + lax.dot_general((q * jnp.exp(cum)).astype(BF16), zt.astype(BF16), (((1,), (1,)), ((), ())),
                                preferred_element_type=F32)
        tot = cum[0:1] if reverse else cum[c - 1:c]
        kd = (k * jnp.exp(tot - cum)).astype(BF16)
        zt = zt * jnp.exp(tot) + lax.dot_general(vv.astype(BF16), kd, (((0,), (0,)), ((), ())), preferred_element_type=F32)
        outs[chn] = y
    return jnp.concatenate(outs, axis=0), zt


def f_hgrn_group(q_raw, f_raw, v, lb, zt, *, reverse):
    ys, zs = [], []
    for h in range(HG_PER_STEP):
        sl = slice(h * 128, (h + 1) * 128)
        y, z = f_hgrn(q_raw[:, sl], f_raw[:, sl], v[:, sl], lb[:, sl], zt[sl], reverse=reverse)
        ys.append(y)
        zs.append(z)
    return jnp.concatenate(ys, axis=1), jnp.concatenate(zs, axis=0)


def _expm1(x):
    poly = x * (1.0 + x * (0.5 + x * (1.0 / 6 + x * (1.0 / 24 + x * (1.0 / 120 + x * (1.0 / 720))))))
    return jnp.where(jnp.abs(x) < 0.3, poly, jnp.exp(x) - 1.0)


def f_gates(u, wa, ba, wi, bi, lam):
    rs, is_ = [], []
    for nb in range(8):
        un = u[:, nb * 128:(nb + 1) * 128].astype(BF16)
        rs.append(jnp.dot(un, wa[nb].astype(BF16), preferred_element_type=F32))
        is_.append(jnp.dot(un, wi[nb].astype(BF16), preferred_element_type=F32))
    r = jax.nn.sigmoid(jnp.concatenate(rs, axis=1) + ba)
    i = jax.nn.sigmoid(jnp.concatenate(is_, axis=1) + bi)
    log_a = -LRU_C * jax.nn.softplus(-lam) * r
    return jnp.exp(log_a), jnp.sqrt(-_expm1(2.0 * log_a)) * (i * u)


def f_ssdfin(y0, y1, xs, z, gy, h0, h1, dpad, ng):
    kk = lax.broadcasted_iota(jnp.int32, (128, D), 0)
    ch = lax.broadcasted_iota(jnp.int32, (128, D), 1)
    expand = (ch // 64 == kk).astype(F32)
    dvec = jnp.dot(dpad, expand, precision=HI, preferred_element_type=F32)[0:1]
    y = y0 + y1 + dvec * xs
    yn = _rms(y * jax.nn.silu(z), ng)
    r = (h0 + h1) * jax.nn.gelu(gy)
    return jnp.concatenate([yn, r], axis=1)


def f_oddfin(o0, o1, g, y0, y1, u, hn, sd, gw, gb):
    parts = []
    for h in range(HG_HEADS):
        sl = slice(h * 128, (h + 1) * 128)
        parts.append(_rms(o0[:, sl] + o1[:, sl], hn[h:h + 1]) * jax.nn.silu(g[:, sl]))
    y = jax.nn.gelu(y0 + y1 + sd * u)
    y = y * jax.nn.sigmoid(jnp.dot(y.astype(BF16), gw.astype(BF16), preferred_element_type=F32) + gb)
    return jnp.concatenate(parts + [y], axis=1)


def f_s5p(lre, lim, lstep, btr, bti):
    step = jnp.exp(lstep)
    mag = jnp.exp(lre * step)
    ar, ai = mag * jnp.cos(lim * step), mag * jnp.sin(lim * step)
    den = lre * lre + lim * lim
    zr = ((ar - 1.0) * lre + ai * lim) / den
    zi = (ai * lre - (ar - 1.0) * lim) / den
    bbr = zr[:, None, :] * btr - zi[:, None, :] * bti
    bbi = zr[:, None, :] * bti + zi[:, None, :] * btr
    return ar, ai, bbr, bbi


def f_lb(logits):
    m = jnp.max(logits, axis=0, keepdims=True)
    e = jnp.exp(logits - m)
    p = e / jnp.sum(e, axis=0, keepdims=True)
    return p[1:2], p[1:2] + p[2:3] + p[3:4]


def f_adamw(w, m, v, *gs):
    g = gs[0]
    for t in gs[1:]:
        g = g + t
    m = ADAM_B1 * m + (1.0 - ADAM_B1) * g
    v = ADAM_B2 * v + (1.0 - ADAM_B2) * jnp.square(g)
    m_hat = m / (1.0 - ADAM_B1 ** ADAM_STEP)
    v_hat = v / (1.0 - ADAM_B2 ** ADAM_STEP)
    delta = -ADAM_LR * (m_hat / (jnp.sqrt(v_hat) + ADAM_EPS) + ADAM_WD * w)
    return g, delta, m, v


def scan_fwd(name, f, grid, ins, y_out, st_out, state_shape, is_first):
    n_in = len(ins)

    def body(*refs):
        y_ref, so_ref, st = refs[n_in], refs[n_in + 1], refs[n_in + 2]
        ids = [pl.program_id(a) for a in range(len(grid))]

        @pl.when(is_first(*ids))
        def _():
            st[...] = jnp.zeros_like(st)

        s = st[...]
        so_ref[...] = s
        y, new = f(*[r[...] for r in refs[:n_in]], s)
        y_ref[...] = y.astype(y_ref.dtype)
        st[...] = new

    return pl.pallas_call(
        body, grid=grid,
        in_specs=[pl.BlockSpec(b, m) for _, b, m in ins],
        out_specs=[pl.BlockSpec(y_out[2], y_out[3]), pl.BlockSpec(st_out[2], st_out[3])],
        out_shape=[jax.ShapeDtypeStruct(y_out[0], y_out[1]), jax.ShapeDtypeStruct(st_out[0], st_out[1])],
        scratch_shapes=[pltpu.VMEM(state_shape, F32)],
        name=name, compiler_params=_cparams(("arbitrary",) * len(grid)),
    )(*[a for a, _, _ in ins])


def scan_bwd(name, f, grid, ins, st_in, dy, grads, state_shape, is_first):
    n_in = len(ins)
    didx = [g[0] for g in grads]

    def body(*refs):
        s_ref, dy_ref = refs[n_in], refs[n_in + 1]
        g_refs = refs[n_in + 2:n_in + 2 + len(grads)]
        dst = refs[n_in + 2 + len(grads)]
        ids = [pl.program_id(a) for a in range(len(grid))]

        @pl.when(is_first(*ids))
        def _():
            dst[...] = jnp.zeros_like(dst)

        vals = [r[...] for r in refs[:n_in]]

        def fd(s, *dv):
            full = list(vals)
            for i, v in zip(didx, dv):
                full[i] = v
            return f(*full, s)

        (y, _), vjp = jax.vjp(fd, s_ref[...], *[vals[i] for i in didx])
        gs = vjp((dy_ref[...].astype(y.dtype), dst[...]))
        dst[...] = gs[0]
        for g, spec, ref in zip(gs[1:], grads, g_refs):
            first = spec[5]
            if first is None:
                ref[...] = g.astype(ref.dtype)
            else:
                fst = first(*ids)

                @pl.when(fst)
                def _(ref=ref, g=g):
                    ref[...] = g.astype(ref.dtype)

                @pl.when(jnp.logical_not(fst))
                def _(ref=ref, g=g):
                    ref[...] += g.astype(ref.dtype)

    allin = list(ins) + [st_in, dy]
    return pl.pallas_call(
        body, grid=grid,
        in_specs=[pl.BlockSpec(b, m) for _, b, m in allin],
        out_specs=[pl.BlockSpec(s[3], s[4]) for s in grads],
        out_shape=[jax.ShapeDtypeStruct(s[1], s[2]) for s in grads],
        scratch_shapes=[pltpu.VMEM(state_shape, F32)],
        name=name, compiler_params=_cparams(("arbitrary",) * len(grid)),
    )(*[a for a, _, _ in allin])


def _tile_order(order, nt):
    if order == 'F':
        return (lambda j: j), True
    if order == 'Fb':
        return (lambda j: nt - 1 - j), False
    if order == 'R':
        return (lambda j: jnp.where(j == 0, 0, nt - j)), False
    return (lambda j: jnp.where(j == nt - 1, 0, j + 1)), True


def linrec(name, a, b, order):
    bsz, tt = a.shape[:2]
    tq = _pick(tt, (256, 128))
    nt = tt // tq
    phys, asc = _tile_order(order, nt)

    def body(a_ref, b_ref, h_ref, hp_ref, hs):
        @pl.when(pl.program_id(0) == 0)
        def _():
            hs[...] = jnp.zeros_like(hs)

        def step(i, hcur):
            t = i if asc else tq - 1 - i
            out = []
            for bi in range(bsz):
                hp_ref[bi, t] = hcur[bi]
                hn = a_ref[bi, t] * hcur[bi] + b_ref[bi, t]
                h_ref[bi, t] = hn
                out.append(hn)
            return tuple(out)

        fin = lax.fori_loop(0, tq, step, tuple(hs[bi] for bi in range(bsz)))
        for bi in range(bsz):
            hs[bi] = fin[bi]

    spec = pl.BlockSpec((bsz, tq, 8, 128), lambda j: (0, phys(j), 0, 0))
    return pl.pallas_call(
        body, grid=(nt,), in_specs=[spec, spec], out_specs=[spec, spec],
        out_shape=[jax.ShapeDtypeStruct(a.shape, F32)] * 2,
        scratch_shapes=[pltpu.VMEM((bsz, 8, 128), F32)],
        name=name, compiler_params=_cparams(("arbitrary",)),
    )(a, b)


def linrec_bwd(name, a, dh, hprev, order):
    bsz, tt = a.shape[:2]
    tq = _pick(tt, (256, 128))
    nt = tt // tq
    phys, asc = _tile_order(order, nt)

    def body(a_ref, dh_ref, hp_ref, g_ref, ga_ref, gs, as_):
        @pl.when(pl.program_id(0) == 0)
        def _():
            gs[...] = jnp.zeros_like(gs)
            as_[...] = jnp.zeros_like(as_)

        def step(i, carry):
            t = i if asc else tq - 1 - i
            out = []
            for bi in range(bsz):
                gcur, acur = carry[bi]
                gn = acur * gcur + dh_ref[bi, t]
                g_ref[bi, t] = gn
                ga_ref[bi, t] = gn * hp_ref[bi, t]
                out.append((gn, a_ref[bi, t]))
            return tuple(out)

        fin = lax.fori_loop(0, tq, step, tuple((gs[bi], as_[bi]) for bi in range(bsz)))
        for bi in range(bsz):
            gs[bi] = fin[bi][0]
            as_[bi] = fin[bi][1]

    spec = pl.BlockSpec((bsz, tq, 8, 128), lambda j: (0, phys(j), 0, 0))
    return pl.pallas_call(
        body, grid=(nt,), in_specs=[spec, spec, spec], out_specs=[spec, spec],
        out_shape=[jax.ShapeDtypeStruct(a.shape, F32)] * 2,
        scratch_shapes=[pltpu.VMEM((bsz, 8, 128), F32), pltpu.VMEM((bsz, 8, 128), F32)],
        name=name, compiler_params=_cparams(("arbitrary",)),
    )(a, dh, hprev)


def clinrec(name, x, coef, order):
    bsz, tt = x.shape[:2]
    tq = _pick(tt, (256, 128))
    nt = tt // tq
    phys, asc = _tile_order(order, nt)

    def body(x_ref, c_ref, h_ref, hp_ref, hs):
        @pl.when(pl.program_id(0) == 0)
        def _():
            hs[...] = jnp.zeros_like(hs)

        ar, ai = c_ref[0], c_ref[1]

        def step(i, carry):
            t = i if asc else tq - 1 - i
            out = []
            for bi in range(bsz):
                hr, hi = carry[bi]
                hp_ref[bi, t, 0] = hr
                hp_ref[bi, t, 1] = hi
                nr = ar * hr - ai * hi + x_ref[bi, t, 0]
                ni = ar * hi + ai * hr + x_ref[bi, t, 1]
                h_ref[bi, t, 0] = nr
                h_ref[bi, t, 1] = ni
                out.append((nr, ni))
            return tuple(out)

        fin = lax.fori_loop(0, tq, step, tuple((hs[bi, 0], hs[bi, 1]) for bi in range(bsz)))
        for bi in range(bsz):
            hs[bi, 0] = fin[bi][0]
            hs[bi, 1] = fin[bi][1]

    spec = pl.BlockSpec((bsz, tq, 2, 8, 128), lambda j: (0, phys(j), 0, 0, 0))
    cspec = pl.BlockSpec((2, 8, 128), lambda j: (0, 0, 0))
    return pl.pallas_call(
        body, grid=(nt,), in_specs=[spec, cspec], out_specs=[spec, spec],
        out_shape=[jax.ShapeDtypeStruct(x.shape, F32)] * 2,
        scratch_shapes=[pltpu.VMEM((bsz, 2, 8, 128), F32)],
        name=name, compiler_params=_cparams(("arbitrary",)),
    )(x, coef)


def clinrec_bwd(name, dh, hprev, coef, order):
    bsz, tt = dh.shape[:2]
    tq = _pick(tt, (256, 128))
    nt = tt // tq
    phys, asc = _tile_order(order, nt)

    def body(d_ref, hp_ref, c_ref, g_ref, dc_ref, gs):
        @pl.when(pl.program_id(0) == 0)
        def _():
            gs[...] = jnp.zeros_like(gs)
            dc_ref[...] = jnp.zeros_like(dc_ref)

        ar, ai = c_ref[0], c_ref[1]

        def step(i, carry):
            t = i if asc else tq - 1 - i
            gcar, dar, dai = carry
            out = []
            for bi in range(bsz):
                gr, gi = gcar[bi]
                nr = ar * gr + ai * gi + d_ref[bi, t, 0]
                ni = ar * gi - ai * gr + d_ref[bi, t, 1]
                g_ref[bi, t, 0] = nr
                g_ref[bi, t, 1] = ni
                hpr, hpi = hp_ref[bi, t, 0], hp_ref[bi, t, 1]
                dar = dar + nr * hpr + ni * hpi
                dai = dai + ni * hpr - nr * hpi
                out.append((nr, ni))
            return tuple(out), dar, dai

        z = jnp.zeros((8, 128), F32)
        fin, dar, dai = lax.fori_loop(0, tq, step, (tuple((gs[bi, 0], gs[bi, 1]) for bi in range(bsz)), z, z))
        for bi in range(bsz):
            gs[bi, 0] = fin[bi][0]
            gs[bi, 1] = fin[bi][1]
        dc_ref[0] += dar
        dc_ref[1] += dai

    spec = pl.BlockSpec((bsz, tq, 2, 8, 128), lambda j: (0, phys(j), 0, 0, 0))
    cspec = pl.BlockSpec((2, 8, 128), lambda j: (0, 0, 0))
    return pl.pallas_call(
        body, grid=(nt,), in_specs=[spec, spec, cspec], out_specs=[spec, cspec],
        out_shape=[jax.ShapeDtypeStruct(dh.shape, F32), jax.ShapeDtypeStruct((2, 8, 128), F32)],
        scratch_shapes=[pltpu.VMEM((bsz, 2, 8, 128), F32)],
        name=name, compiler_params=_cparams(("arbitrary",)),
    )(dh, hprev, coef)


def _blockdiag(bb):
    eye = jnp.eye(16, dtype=bb.dtype)
    return (bb[:, :, None, :] * eye[:, None, :, None]).reshape(256, 1024)


def _blockdiag_t(c):
    eye = jnp.eye(16, dtype=c.dtype)
    return (jnp.swapaxes(c, 1, 2)[:, :, None, :] * eye[:, None, :, None]).reshape(1024, 256)


def _unblockdiag(m):
    eye = jnp.eye(16, dtype=m.dtype)
    return jnp.sum(m.reshape(16, 16, 16, 64) * eye[:, None, :, None], axis=2)


def _unblockdiag_t(m):
    eye = jnp.eye(16, dtype=m.dtype)
    return jnp.swapaxes(jnp.sum(m.reshape(16, 64, 16, 16) * eye[:, None, :, None], axis=2), 1, 2)


def _pad_rows(v, rows=8, cols=128):
    out = jnp.zeros((rows, cols), F32)
    return out.at[0, :v.shape[0]].set(v)


def local_step(x, c, ctx, c_ctx, target, W):
    B, Tx, _ = x.shape
    Lc = ctx.shape[1]
    Tt = Lc + Tx
    tb = Lc
    nt = Tt // tb
    M = B * Tt
    nc = Tt // SSD_CHUNK
    ncc = Lc // SSD_CHUNK
    G = {}

    def add_grad(name, idx, val):
        G.setdefault(name, {})[idx] = val

    def tok(a, cb=None, off=0):
        cb = a.shape[-1] if cb is None else cb
        return (a, (None, tb, cb), lambda b, j, off=off: (b, j, off))

    def tok_out(cols, dtype=F32):
        return ((B, Tt, cols), dtype, (None, tb, cols), lambda b, j: (b, j, 0))

    def vec(a):
        return (a, a.shape, lambda *ids, n=a.ndim: (0,) * n)

    def vec_acc(shape):
        return (shape, F32, shape, lambda *ids, n=len(shape): (0,) * n, lambda *ids: functools.reduce(jnp.logical_and, [i == 0 for i in ids]))

    def modv(l, which):
        return (modr, (None, None, None, 1, D), lambda b, j, l=l, which=which: (l, jnp.where(j == 0, 4, b), which, 0, 0))

    dmod_spec = ((B, 2, 1, D), F32, (None, None, 1, D), lambda b, j: (b, jnp.where(j == 0, 0, 1), 0, 0), lambda b, j: j <= 1)

    def phys_chunk(n_all, n_ctx, reverse):
        if not reverse:
            return lambda s: s
        return lambda s: jnp.where(s < n_ctx, n_ctx - 1 - s, n_all - 1 - (s - n_ctx))

    cc = jnp.zeros((8, D), F32).at[:B].set(c).at[4].set(c_ctx)
    nmc = N_MOD * D // 1536

    def f_mod(ccv, w, b):
        return jnp.dot(jax.nn.silu(ccv).astype(BF16), w, preferred_element_type=F32) + b

    mod = tile_fwd('mod_fwd', f_mod, (DEPTH, nmc),
                   [(cc, (8, D), lambda l, n: (0, 0)), (W['w_mod'], (None, D, 1536), lambda l, n: (l, 0, n)),
                    (W['b_mod'].reshape(DEPTH, 1, N_MOD * D), (None, 1, 1536), lambda l, n: (l, 0, n))],
                   [((DEPTH, 8, N_MOD * D), F32, (None, 8, 1536), lambda l, n: (l, 0, n))])[0]
    modr = mod.reshape(DEPTH, 8, N_MOD, 1, D)
    dmods = {}

    lb1, lb3 = tile_fwd('lb_fwd', f_lb, (1,), [vec(W['hg_lb_logits'])],
                        [((1, HG_W), F32, (1, HG_W), lambda i: (0, 0))] * 2)
    dlb = {1: [], 3: []}

    x0 = jnp.concatenate([ctx, x], axis=1)
    conv_mk = ffnconv_masks(Tt, Lc)
    R = [dict() for _ in range(DEPTH)]

    xprev, oprev = x0, None
    for l in range(DEPTH):
        r = R[l]
        j = l // 2
        ng = W['norm_mix_g'][l][None]
        if l == 0:
            h1 = tile_fwd(f'nm0_fwd', f_nm0, (B, nt), [tok(xprev), vec(ng), modv(l, 0), modv(l, 1)], [tok_out(D, BF16)])[0]
            xa = xprev
        else:
            xa, h1 = tile_fwd(f'nm_mix_fwd{l}', f_nm, (B, nt),
                              [tok(xprev), tok(oprev), modv(l - 1, 5), vec(ng), modv(l, 0), modv(l, 1)],
                              [tok_out(D), tok_out(D, BF16)])
        r['xin'], r['oin'], r['xa'], r['h1'] = xprev, oprev, xa, h1
        h1m = h1.reshape(M, D)
        if l % 2 == 0:
            win = W['ev_w_in_p'][j]
            proj = mm(f'ev_proj{l}', [(h1m, win, 0)]).reshape(B, Tt, EV_PAD)
            r['proj'] = proj
            scw, scb = W['ssd_conv_w'][j], W['ssd_conv_b'][j][None]
            lcw, lcb = W['lru_conv_w'][j], W['lru_conv_b'][j][None]

            def conv_call(name, colblk0, w, b, wblk0, ncols, act):
                return tile_fwd(name, functools.partial(f_conv1d, lc=Lc, act=act), (ncols // 256, B),
                                [(proj, (None, Tt, 256), lambda cb, bi: (bi, 0, colblk0 + cb)),
                                 (w, (4, 256), lambda cb, bi: (0, wblk0 + cb)), (b, (1, 256), lambda cb, bi: (0, wblk0 + cb))],
                                [((B, Tt, ncols), F32, (None, Tt, 256), lambda cb, bi: (bi, 0, cb))])[0]
            xs_c = conv_call(f'conv_xs{l}', 12, scw, scb, 0, 1024, True)
            bc_c = conv_call(f'conv_bc{l}', 16, scw, scb, 4, 512, True)
            u_c = conv_call(f'conv_u{l}', 8, lcw, lcb, 0, 1024, False)
            r['xs'], r['bc'], r['u'] = xs_c, bc_c, u_c
            bias = _pad_rows(W['ssd_dt_bias'][j].reshape(-1), 1)
            alog = _pad_rows(W['ssd_a_log'][j].reshape(-1), 1)
            r['bias'], r['alog'] = bias, alog
            r['y'], r['st'], r['a4'], r['hp4'], r['h'] = [], [], [], [], []
            for d in range(2):
                ph = phys_chunk(nc, ncc, d == 1)
                y, st = scan_fwd(
                    f'ssd_fwd{l}_{d}', functools.partial(f_ssd, d=d, reverse=(d == 1)), (B, nc),
                    [(xs_c, (None, SSD_CHUNK, D), lambda b, s, ph=ph: (b, ph(s), 0)),
                     (bc_c, (None, SSD_CHUNK, 512), lambda b, s, ph=ph: (b, ph(s), 0)),
                     (proj, (None, SSD_CHUNK, 128), lambda b, s, ph=ph: (b, ph(s), 36)),
                     vec(bias), vec(alog)],
                    ((B, Tt, D), F32, (None, SSD_CHUNK, D), lambda b, s, ph=ph: (b, ph(s), 0)),
                    ((B, nc, 8, 128, 128), F32, (None, None, 8, 128, 128), lambda b, s: (b, s, 0, 0, 0)),
                    (8, 128, 128), lambda b, s: s == 0)
                r['y'].append(y)
                r['st'].append(st)
                a_d, bx_d = tile_fwd(
                    f'gates_fwd{l}_{d}', f_gates, (B, nt),
                    [tok(u_c), vec(W['lru_w_a'][j, d]), vec(W['lru_b_a'][j, d][None]), vec(W['lru_w_i'][j, d]),
                     vec(W['lru_b_i'][j, d][None]), vec(W['lru_lam'][j, d][None])],
                    [tok_out(D), tok_out(D)])
                a4 = a_d.reshape(B, Tt, 8, 128)
                h4, hp4 = linrec(f'lru_fwd{l}_{d}', a4, bx_d.reshape(B, Tt, 8, 128), 'F' if d == 0 else 'R')
                r['a4'].append(a4)
                r['hp4'].append(hp4)
                r['h'].append(h4.reshape(B, Tt, D))
            dpad = _pad_rows(W['ssd_d'][j])
            sng = W['ssd_norm_g'][j][None]
            r['dpad'], r['sng'] = dpad, sng
            mix = tile_fwd(f'ssdfin_fwd{l}', f_ssdfin, (B, nt),
                           [tok(r['y'][0]), tok(r['y'][1]), tok(xs_c), tok(proj, D, 0), tok(proj, D, 1), tok(r['h'][0]),
                            tok(r['h'][1]), vec(dpad), vec(sng)], [tok_out(2 * D, BF16)])[0]
            wout = W['ev_w_out'][j]
        else:
            win = W['od_w_in'][j]
            proj = mm(f'od_proj{l}', [(h1m, win, 0)]).reshape(B, Tt, 4096)
            r['proj'] = proj
            lbv = lb1 if l == 1 else lb3
            ns = nc
            r['o'], r['zst'], r['coef'], r['bcat'], r['ccat'], r['hp5'], r['hcat'], r['yd'], r['s5in'] = [], [], [], [], [], [], [], [], []
            u2 = proj[:, :, 3840:].reshape(M, S5_W)
            r['u2'] = u2
            for d in range(2):
                ph = phys_chunk(ns, ncc, d == 1)
                o_d, zst = scan_fwd(
                    f'hgrn_fwd{l}_{d}', functools.partial(f_hgrn_group, reverse=(d == 1)), (HG_GROUPS, B, ns),
                    [(proj, (None, 128, HG_GW), lambda h, b, s, ph=ph: (b, ph(s), h)),
                     (proj, (None, 128, HG_GW), lambda h, b, s, ph=ph, d=d: (b, ph(s), (1 + d) * HG_GROUPS + h)),
                     (proj, (None, 128, HG_GW), lambda h, b, s, ph=ph: (b, ph(s), 3 * HG_GROUPS + h)),
                     (lbv, (1, HG_GW), lambda h, b, s: (0, h))],
                    ((B, Tt, HG_W), F32, (None, 128, HG_GW), lambda h, b, s, ph=ph: (b, ph(s), h)),
                    ((B, HG_GROUPS, ns, HG_GW, 128), F32, (None, None, None, HG_GW, 128), lambda h, b, s: (b, h, s, 0, 0)),
                    (HG_GW, 128), lambda h, b, s: s == 0)
                r['o'].append(o_d)
                r['zst'].append(zst)
                s5in = [W['s5_lam_re'][j, d], W['s5_lam_im'][j, d], W['s5_log_step'][j, d].reshape(16, 1),
                        jnp.swapaxes(W['s5_b_re'][j], 1, 2), jnp.swapaxes(W['s5_b_im'][j], 1, 2)]
                r['s5in'].append(s5in)
                ar, ai, bbr, bbi = tile_fwd(f's5p_fwd{l}_{d}', f_s5p, (1,), [vec(t) for t in s5in],
                                            [((16, 64), F32, (16, 64), lambda i: (0, 0))] * 2
                                            + [((16, 16, 64), F32, (16, 16, 64), lambda i: (0, 0, 0))] * 2)
                coef = jnp.stack([ar.reshape(8, 128), ai.reshape(8, 128)])
                bcat = jnp.concatenate([_blockdiag(bbr), _blockdiag(bbi)], axis=1).astype(BF16)
                ccat = jnp.concatenate([_blockdiag_t(W['s5_c_re'][j, d]), -_blockdiag_t(W['s5_c_im'][j, d])], axis=0).astype(BF16)
                xcat = mm(f's5_in{l}_{d}', [(u2, bcat, 0)])
                h5, hp5 = clinrec(f's5_fwd{l}_{d}', xcat.reshape(B, Tt, 2, 8, 128), coef, 'F' if d == 0 else 'R')
                hcat = h5.reshape(M, 2 * D)
                yd = mm(f's5_out{l}_{d}', [(hcat, ccat, 0)]).reshape(B, Tt, S5_W)
                r['coef'].append(coef)
                r['bcat'].append(bcat)
                r['ccat'].append(ccat)
                r['hp5'].append(hp5)
                r['hcat'].append(hcat)
                r['yd'].append(yd)
            hn = jnp.zeros((8, 128), F32).at[:HG_HEADS].set(W['hg_norm_g'][j])
            sd, gw, gb = W['s5_d'][j][None], W['s5_glu_w'][j], W['s5_glu_b'][j][None]
            r['fin_par'] = (hn, sd, gw, gb)
            mix = tile_fwd(f'oddfin_fwd{l}', f_oddfin, (B, nt),
                           [tok(r['o'][0]), tok(r['o'][1]), tok(proj, HG_W, 4), tok(r['yd'][0]), tok(r['yd'][1]),
                            tok(proj, S5_W, 15), vec(hn), vec(sd), vec(gw), vec(gb)], [tok_out(D, BF16)])[0]
            wout = W['od_w_out'][j]
        r['mix'] = mix
        o1 = mm(f'mix_out{l}', [(mix.reshape(M, -1), wout, 0)]).reshape(B, Tt, D)
        r['o1'] = o1
        fg = W['norm_ffn_g'][l][None]
        xb, h2 = tile_fwd(f'nm_ffn_fwd{l}', f_nm, (B, nt), [tok(xa), tok(o1), modv(l, 2), vec(fg), modv(l, 3), modv(l, 4)],
                          [tok_out(D), tok_out(D, BF16)])
        r['h2'] = h2
        h2m = h2.reshape(M, D)
        a = mm(f'ffn_gate{l}', [(h2m, W['ffn_w_gate'][l], 0)]).reshape(B, Tt, D_FF)
        up = mm(f'ffn_up{l}', [(h2m, W['ffn_w_up'][l], 0)]).reshape(B, Tt, D_FF)
        w9 = W['ffn_conv_w'][l].reshape(9, D_FF)
        cbias = W['ffn_conv_b'][l][None]
        r['a'], r['up'], r['w9'], r['cbias'] = a, up, w9, cbias
        act = tile_fwd(f'ffnconv_fwd{l}', f_ffnconv, (D_FF // 128, B),
                       [(a, (None, Tt, 128), lambda cb, bi: (bi, 0, cb)), (up, (None, Tt, 128), lambda cb, bi: (bi, 0, cb)),
                        (w9, (9, 128), lambda cb, bi: (0, cb)), (cbias, (1, 128), lambda cb, bi: (0, cb)), vec(conv_mk)],
                       [((B, Tt, D_FF), BF16, (None, Tt, 128), lambda cb, bi: (bi, 0, cb))])[0]
        r['act'] = act
        o2 = mm(f'ffn_down{l}', [(act.reshape(M, D_FF), W['ffn_w_down'][l], 0)]).reshape(B, Tt, D)
        xprev, oprev = xb, o2

    vmask = jnp.ones((nt, 1, D), F32).at[0].set(0.0)
    ones = jnp.ones((1, D), F32)
    fng = W['final_norm_g'][None]
    d_xp, d_o2, dg5, dfng, loss_vec = tile_bwd(
        'loss_head', f_final, (B, nt),
        [tok(xprev), tok(oprev), modv(DEPTH - 1, 5), vec(fng),
         (target, (None, tb, D), lambda b, j: (b, jnp.maximum(j - 1, 0), 0)), (vmask, (None, 1, D), lambda b, j: (j, 0, 0))],
        [[vec(ones)]],
        [(0,) + tok_out(D) + (None,), (1,) + tok_out(D, BF16) + (None,), (2,) + dmod_spec, (3,) + vec_acc((1, D))],
        prims=[(0,) + vec_acc((1, D))])
    loss = jnp.sum(loss_vec)
    add_grad('final_norm_g', None, dfng[0])
    dmods[(DEPTH - 1, 5)] = dg5

    for l in reversed(range(DEPTH)):
        r = R[l]
        j = l // 2
        d_o2m = d_o2.reshape(M, D)
        d_act = mm(f'ffn_down_dx{l}', [(d_o2m, W['ffn_w_down'][l], 0)], tb=True).reshape(B, Tt, D_FF)
        add_grad('ffn_w_down', l, mm(f'ffn_down_dw{l}', [(r['act'].reshape(M, D_FF), d_o2m, 0)], ta=True))
        d_a, d_up, dw9, dcb = tile_bwd(
            f'ffnconv_bwd{l}', f_ffnconv, (D_FF // 128, B),
            [(r['a'], (None, Tt, 128), lambda cb, bi: (bi, 0, cb)), (r['up'], (None, Tt, 128), lambda cb, bi: (bi, 0, cb)),
             (r['w9'], (9, 128), lambda cb, bi: (0, cb)), (r['cbias'], (1, 128), lambda cb, bi: (0, cb)), vec(conv_mk)],
            [[(d_act, (None, Tt, 128), lambda cb, bi: (bi, 0, cb))]],
            [(0, (B, Tt, D_FF), BF16, (None, Tt, 128), lambda cb, bi: (bi, 0, cb), None),
             (1, (B, Tt, D_FF), BF16, (None, Tt, 128), lambda cb, bi: (bi, 0, cb), None),
             (2, (9, D_FF), F32, (9, 128), lambda cb, bi: (0, cb), lambda cb, bi: bi == 0),
             (3, (1, D_FF), F32, (1, 128), lambda cb, bi: (0, cb), lambda cb, bi: bi == 0)])
        add_grad('ffn_conv_w', l, dw9.reshape(3, 3, D_FF))
        add_grad('ffn_conv_b', l, dcb[0])
        d_am, d_upm = d_a.reshape(M, D_FF), d_up.reshape(M, D_FF)
        h2m = r['h2'].reshape(M, D)
        d_h2 = mm(f'ffn_in_dx{l}', [(d_am, W['ffn_w_gate'][l], 0), (d_upm, W['ffn_w_up'][l], 0)], tb=True).reshape(B, Tt, D)
        add_grad('ffn_w_gate', l, mm(f'ffn_gate_dw{l}', [(h2m, d_am, 0)], ta=True))
        add_grad('ffn_w_up', l, mm(f'ffn_up_dw{l}', [(h2m, d_upm, 0)], ta=True))
        fg = W['norm_ffn_g'][l][None]
        d_xa, d_o1, dgate, dfg, dsh, dsc = tile_bwd(
            f'nm_ffn_bwd{l}', f_nm, (B, nt), [tok(r['xa']), tok(r['o1']), modv(l, 2), vec(fg), modv(l, 3), modv(l, 4)],
            [[tok(d_xp)], [tok(d_h2)]],
            [(0,) + tok_out(D) + (None,), (1,) + tok_out(D, BF16) + (None,), (2,) + dmod_spec, (3,) + vec_acc((1, D)),
             (4,) + dmod_spec, (5,) + dmod_spec])
        add_grad('norm_ffn_g', l, dfg[0])
        dmods[(l, 2)], dmods[(l, 3)], dmods[(l, 4)] = dgate, dsh, dsc
        d_o1m = d_o1.reshape(M, D)
        h1m = r['h1'].reshape(M, D)
        proj = r['proj']
        if l % 2 == 0:
            wout, win = W['ev_w_out'][j], W['ev_w_in_p'][j]
            d_mix = mm(f'mix_out_dx{l}', [(d_o1m, wout, 0)], tb=True).reshape(B, Tt, 2 * D)
            add_grad('ev_w_out', j, mm(f'mix_out_dw{l}', [(r['mix'].reshape(M, 2 * D), d_o1m, 0)], ta=True))
            d_y, d_xs_fin, d_z, d_gy, d_h, ddpad, dsng = tile_bwd(
                f'ssdfin_bwd{l}', f_ssdfin, (B, nt),
                [tok(r['y'][0]), tok(r['y'][1]), tok(r['xs']), tok(proj, D, 0), tok(proj, D, 1), tok(r['h'][0]), tok(r['h'][1]),
                 vec(r['dpad']), vec(r['sng'])],
                [[tok(d_mix)]],
                [(0,) + tok_out(D) + (None,), (2,) + tok_out(D) + (None,), (3,) + tok_out(D) + (None,), (4,) + tok_out(D) + (None,),
                 (5,) + tok_out(D) + (None,), (7,) + vec_acc((8, 128)), (8,) + vec_acc((1, D))])
            add_grad('ssd_d', j, ddpad[0, :SSD_HEADS])
            add_grad('ssd_norm_g', j, dsng[0])
            d_xs_parts, d_bc_parts, d_dt_parts, d_u_parts = [d_xs_fin], [], [], []
            dbias_t, dalog_t = [], []
            dh4 = d_h.reshape(B, Tt, 8, 128)
            for d in range(2):
                ph0 = phys_chunk(nc, ncc, d == 1)

                def ph(s, ph0=ph0):
                    return ph0(nc - 1 - s)
                dxs_d, dbc_d, ddt_d, dbias, dalog = scan_bwd(
                    f'ssd_bwd{l}_{d}', functools.partial(f_ssd, d=d, reverse=(d == 1)), (B, nc),
                    [(r['xs'], (None, SSD_CHUNK, D), lambda b, s, ph=ph: (b, ph(s), 0)),
                     (r['bc'], (None, SSD_CHUNK, 512), lambda b, s, ph=ph: (b, ph(s), 0)),
                     (proj, (None, SSD_CHUNK, 128), lambda b, s, ph=ph: (b, ph(s), 36)),
                     vec(r['bias']), vec(r['alog'])],
                    (r['st'][d], (None, None, 8, 128, 128), lambda b, s: (b, nc - 1 - s, 0, 0, 0)),
                    (d_y, (None, SSD_CHUNK, D), lambda b, s, ph=ph: (b, ph(s), 0)),
                    [(0, (B, Tt, D), F32, (None, SSD_CHUNK, D), lambda b, s, ph=ph: (b, ph(s), 0), None),
                     (1, (B, Tt, 512), F32, (None, SSD_CHUNK, 512), lambda b, s, ph=ph: (b, ph(s), 0), None),
                     (2, (B, Tt, 128), F32, (None, SSD_CHUNK, 128), lambda b, s, ph=ph: (b, ph(s), 0), None),
                     (3,) + vec_acc((1, 128)), (4,) + vec_acc((1, 128))],
                    (8, 128, 128), lambda b, s: s == 0)
                d_xs_parts.append(dxs_d)
                d_bc_parts.append(dbc_d)
                d_dt_parts.append(ddt_d)
                dbias_t.append(dbias)
                dalog_t.append(dalog)
                g4, ga4 = linrec_bwd(f'lru_bwd{l}_{d}', r['a4'][d], dh4, r['hp4'][d], 'Fb' if d == 0 else 'Rb')
                du_g, dwa, dba, dwi, dbi, dlam = tile_bwd(
                    f'gates_bwd{l}_{d}', f_gates, (B, nt),
                    [tok(r['u']), vec(W['lru_w_a'][j, d]), vec(W['lru_b_a'][j, d][None]), vec(W['lru_w_i'][j, d]),
                     vec(W['lru_b_i'][j, d][None]), vec(W['lru_lam'][j, d][None])],
                    [[tok(ga4.reshape(B, Tt, D))], [tok(g4.reshape(B, Tt, D))]],
                    [(0,) + tok_out(D) + (None,), (1,) + vec_acc((8, 128, 128)), (2,) + vec_acc((1, D)), (3,) + vec_acc((8, 128, 128)),
                     (4,) + vec_acc((1, D)), (5,) + vec_acc((1, D))])
                d_u_parts.append(du_g)
                add_grad('lru_w_a', (j, d), dwa)
                add_grad('lru_b_a', (j, d), dba[0])
                add_grad('lru_w_i', (j, d), dwi)
                add_grad('lru_b_i', (j, d), dbi[0])
                add_grad('lru_lam', (j, d), dlam[0])
            add_grad('ssd_dt_bias', j, (dbias_t[0] + dbias_t[1])[0, :32].reshape(2, SSD_HEADS))
            add_grad('ssd_a_log', j, (dalog_t[0] + dalog_t[1])[0, :32].reshape(2, SSD_HEADS))
            scw, scb = W['ssd_conv_w'][j], W['ssd_conv_b'][j][None]
            lcw, lcb = W['lru_conv_w'][j], W['lru_conv_b'][j][None]

            def conv_bwd(name, colblk0, w, b, wblk0, ncols, act, parts):
                return tile_bwd(
                    name, functools.partial(f_conv1d, lc=Lc, act=act), (ncols // 256, B),
                    [(proj, (None, Tt, 256), lambda cb, bi: (bi, 0, colblk0 + cb)),
                     (w, (4, 256), lambda cb, bi: (0, wblk0 + cb)), (b, (1, 256), lambda cb, bi: (0, wblk0 + cb))],
                    [[(p, (None, Tt, 256), lambda cb, bi: (bi, 0, cb)) for p in parts]],
                    [(0, (B, Tt, ncols), F32, (None, Tt, 256), lambda cb, bi: (bi, 0, cb), None),
                     (1, (4, ncols), F32, (4, 256), lambda cb, bi: (0, cb), lambda cb, bi: bi == 0),
                     (2, (1, ncols), F32, (1, 256), lambda cb, bi: (0, cb), lambda cb, bi: bi == 0)])
            d_xs_raw, dw_xs, db_xs = conv_bwd(f'conv_xs_bwd{l}', 12, scw, scb, 0, 1024, True, d_xs_parts)
            d_bc_raw, dw_bc, db_bc = conv_bwd(f'conv_bc_bwd{l}', 16, scw, scb, 4, 512, True, d_bc_parts)
            d_u_raw, dw_u, db_u = conv_bwd(f'conv_u_bwd{l}', 8, lcw, lcb, 0, 1024, False, d_u_parts)
            add_grad('ssd_conv_w', j, jnp.concatenate([dw_xs, dw_bc], axis=1))
            add_grad('ssd_conv_b', j, jnp.concatenate([db_xs, db_bc], axis=1)[0])
            add_grad('lru_conv_w', j, dw_u)
            add_grad('lru_conv_b', j, db_u[0])
            def f_ev_dproj(z_, gy_, u_, xs_, bc_, t0, t1):
                pad = jnp.zeros((z_.shape[0], EV_PAD - 4736), F32)
                return jnp.concatenate([z_, gy_, u_, xs_, bc_, t0 + t1, pad], axis=1)
            dproj = tile_fwd(f'ev_dproj{l}', f_ev_dproj, (B, nt),
                             [tok(d_z), tok(d_gy), tok(d_u_raw), tok(d_xs_raw), tok(d_bc_raw), tok(d_dt_parts[0]), tok(d_dt_parts[1])],
                             [tok_out(EV_PAD, BF16)])[0].reshape(M, EV_PAD)
            d_h1 = mm(f'ev_proj_dx{l}', [(dproj, win, 0)], tb=True).reshape(B, Tt, D)
            dwp = mm(f'ev_proj_dw{l}', [(h1m, dproj, 0)], ta=True)
            add_grad('ev_w_in', j, jnp.concatenate([dwp[:, 0:1024], dwp[:, 3072:4640], dwp[:, 1024:3072]], axis=1))
        else:
            wout, win = W['od_w_out'][j], W['od_w_in'][j]
            d_mix = mm(f'mix_out_dx{l}', [(d_o1m, wout, 0)], tb=True).reshape(B, Tt, D)
            add_grad('od_w_out', j, mm(f'mix_out_dw{l}', [(r['mix'].reshape(M, D), d_o1m, 0)], ta=True))
            hn, sd, gw, gb = r['fin_par']
            d_o, d_g, d_yv, d_u_fin, dhn, dsd, dgw, dgb = tile_bwd(
                f'oddfin_bwd{l}', f_oddfin, (B, nt),
                [tok(r['o'][0]), tok(r['o'][1]), tok(proj, HG_W, 4), tok(r['yd'][0]), tok(r['yd'][1]), tok(proj, S5_W, 15),
                 vec(hn), vec(sd), vec(gw), vec(gb)],
                [[tok(d_mix)]],
                [(0,) + tok_out(HG_W) + (None,), (2,) + tok_out(HG_W) + (None,), (3,) + tok_out(S5_W) + (None,),
                 (5,) + tok_out(S5_W) + (None,), (6,) + vec_acc((8, 128)), (7,) + vec_acc((1, S5_W)), (8,) + vec_acc((S5_W, S5_W)),
                 (9,) + vec_acc((1, S5_W))])
            add_grad('hg_norm_g', j, dhn[:HG_HEADS])
            add_grad('s5_d', j, dsd[0])
            add_grad('s5_glu_w', j, dgw)
            add_grad('s5_glu_b', j, dgb[0])
            lbv = lb1 if l == 1 else lb3
            ns = nc
            dq, df, dv, du_s5 = [], [], [], []
            d_ym = d_yv.reshape(M, S5_W)
            dbt_re, dbt_im = [], []
            for d in range(2):
                ph0 = phys_chunk(ns, ncc, d == 1)

                def ph(s, ph0=ph0):
                    return ph0(ns - 1 - s)
                dq_d, df_d, dv_d, dlb_d = scan_bwd(
                    f'hgrn_bwd{l}_{d}', functools.partial(f_hgrn_group, reverse=(d == 1)), (HG_GROUPS, B, ns),
                    [(proj, (None, 128, HG_GW), lambda h, b, s, ph=ph: (b, ph(s), h)),
                     (proj, (None, 128, HG_GW), lambda h, b, s, ph=ph, d=d: (b, ph(s), (1 + d) * HG_GROUPS + h)),
                     (proj, (None, 128, HG_GW), lambda h, b, s, ph=ph: (b, ph(s), 3 * HG_GROUPS + h)),
                     (lbv, (1, HG_GW), lambda h, b, s: (0, h))],
                    (r['zst'][d], (None, None, None, HG_GW, 128), lambda h, b, s: (b, h, ns - 1 - s, 0, 0)),
                    (d_o, (None, 128, HG_GW), lambda h, b, s, ph=ph: (b, ph(s), h)),
                    [(0, (B, Tt, HG_W), F32, (None, 128, HG_GW), lambda h, b, s, ph=ph: (b, ph(s), h), None),
                     (1, (B, Tt, HG_W), F32, (None, 128, HG_GW), lambda h, b, s, ph=ph: (b, ph(s), h), None),
                     (2, (B, Tt, HG_W), F32, (None, 128, HG_GW), lambda h, b, s, ph=ph: (b, ph(s), h), None),
                     (3, (1, HG_W), F32, (1, HG_GW), lambda h, b, s: (0, h), lambda h, b, s: (b == 0) & (s == 0))],
                    (HG_GW, 128), lambda h, b, s: s == 0)
                dq.append(dq_d)
                df.append(df_d)
                dv.append(dv_d)
                dlb[l].append(dlb_d)
                d_hcat = mm(f's5_out_dx{l}_{d}', [(d_ym, r['ccat'][d], 0)], tb=True)
                dccat = mm(f's5_out_dw{l}_{d}', [(r['hcat'][d], d_ym, 0)], ta=True)
                add_grad('s5_c_re', (j, d), _unblockdiag_t(dccat[:D]))
                add_grad('s5_c_im', (j, d), -_unblockdiag_t(dccat[D:]))
                g5, dcoef = clinrec_bwd(f's5_bwd{l}_{d}', d_hcat.reshape(B, Tt, 2, 8, 128), r['hp5'][d], r['coef'][d],
                                        'Fb' if d == 0 else 'Rb')
                gcat = g5.reshape(M, 2 * D)
                dbcat = mm(f's5_in_dw{l}_{d}', [(r['u2'], gcat, 0)], ta=True)
                du_s5.append(mm(f's5_in_dx{l}_{d}', [(gcat, r['bcat'][d], 0)], tb=True))
                cts5 = [dcoef[0].reshape(16, 64), dcoef[1].reshape(16, 64), _unblockdiag(dbcat[:, :D]), _unblockdiag(dbcat[:, D:])]
                dlre, dlim, dlst, dbtr, dbti = tile_bwd(
                    f's5p_bwd{l}_{d}', f_s5p, (1,), [vec(t) for t in r['s5in'][d]], [[vec(t)] for t in cts5],
                    [(i, t.shape, F32, t.shape, (lambda *ids, n=t.ndim: (0,) * n), None) for i, t in enumerate(r['s5in'][d])])
                add_grad('s5_lam_re', (j, d), dlre)
                add_grad('s5_lam_im', (j, d), dlim)
                add_grad('s5_log_step', (j, d), dlst[:, 0])
                dbt_re.append(dbtr)
                dbt_im.append(dbti)
            add_grad('s5_b_re', j, jnp.swapaxes(dbt_re[0] + dbt_re[1], 1, 2))
            add_grad('s5_b_im', j, jnp.swapaxes(dbt_im[0] + dbt_im[1], 1, 2))
            def f_od_dproj(q0, q1, f0, f1, v0, v1, g_, u0, u1, u2):
                return jnp.concatenate([q0 + q1, f0, f1, v0 + v1, g_, u0 + u1 + u2], axis=1)
            parts = [dq[0], dq[1], df[0], df[1], dv[0], dv[1], d_g, d_u_fin, du_s5[0].reshape(B, Tt, S5_W),
                     du_s5[1].reshape(B, Tt, S5_W)]
            dproj = tile_fwd(f'od_dproj{l}', f_od_dproj, (B, nt), [tok(t) for t in parts],
                             [tok_out(4096, BF16)])[0].reshape(M, 4096)
            d_h1 = mm(f'od_proj_dx{l}', [(dproj, win, 0)], tb=True).reshape(B, Tt, D)
            add_grad('od_w_in', j, mm(f'od_proj_dw{l}', [(h1m, dproj, 0)], ta=True))
        ng = W['norm_mix_g'][l][None]
        if l == 0:
            d_x0, dng, dsh, dsc = tile_bwd(
                'nm0_bwd', lambda xv, g, sh, sc: (xv, f_nm0(xv, g, sh, sc)), (B, nt),
                [tok(r['xin']), vec(ng), modv(l, 0), modv(l, 1)], [[tok(d_xa)], [tok(d_h1)]],
                [(0,) + tok_out(D) + (None,), (1,) + vec_acc((1, D)), (2,) + dmod_spec, (3,) + dmod_spec])
        else:
            d_xp, d_o2, dgate, dng, dsh, dsc = tile_bwd(
                f'nm_mix_bwd{l}', f_nm, (B, nt),
                [tok(r['xin']), tok(r['oin']), modv(l - 1, 5), vec(ng), modv(l, 0), modv(l, 1)],
                [[tok(d_xa)], [tok(d_h1)]],
                [(0,) + tok_out(D) + (None,), (1,) + tok_out(D, BF16) + (None,), (2,) + dmod_spec, (3,) + vec_acc((1, D)),
                 (4,) + dmod_spec, (5,) + dmod_spec])
            dmods[(l - 1, 5)] = dgate
        add_grad('norm_mix_g', l, dng[0])
        dmods[(l, 0)], dmods[(l, 1)] = dsh, dsc

    grad_x = d_x0[:, Lc:, :]

    (dlogits,) = tile_bwd('lb_bwd', f_lb, (1,), [vec(W['hg_lb_logits'])],
                          [[vec(t) for t in dlb[1]], [vec(t) for t in dlb[3]]],
                          [(0, (DEPTH, HG_W), F32, (DEPTH, HG_W), lambda i: (0, 0), None)])
    add_grad('hg_lb_logits', None, dlogits)

    dm = jnp.stack([jnp.stack([dmods[(l, w)] for w in range(N_MOD)]) for l in range(DEPTH)])
    dlat = jnp.transpose(dm[:, :, :, 1, 0, :], (0, 2, 1, 3)).reshape(DEPTH, B, N_MOD * D)
    dctx = jnp.transpose(dm[:, :, :, 0, 0, :], (0, 2, 1, 3)).reshape(DEPTH, B, N_MOD * D)
    dlat = jnp.zeros((DEPTH, 8, N_MOD * D), F32).at[:, :B].set(dlat)
    dctx = jnp.zeros((DEPTH, 8, N_MOD * D), F32).at[:, :B].set(dctx)

    def mod_bwd_body(cc_ref, w_ref, dl_ref, dc_ref, dw_ref, db_ref, dcc_ref):
        row = lax.broadcasted_iota(jnp.int32, (8, 1), 0)
        dall = dl_ref[...] + jnp.where(row == 4, jnp.sum(dc_ref[...], axis=0, keepdims=True), 0.0)
        s, vjp = jax.vjp(jax.nn.silu, cc_ref[...])
        db16 = dall.astype(BF16)
        dw_ref[...] = lax.dot_general(s.astype(BF16), db16, (((0,), (0,)), ((), ())), preferred_element_type=F32)
        db_ref[...] = jnp.sum(dall, axis=0, keepdims=True)
        ds = lax.dot_general(db16, w_ref[...], (((1,), (1,)), ((), ())), preferred_element_type=F32)
        (dcc,) = vjp(ds)
        first = (pl.program_id(0) == 0) & (pl.program_id(1) == 0)

        @pl.when(first)
        def _():
            dcc_ref[...] = dcc

        @pl.when(jnp.logical_not(first))
        def _():
            dcc_ref[...] += dcc

    dwmod, dbmod, dcc = pl.pallas_call(
        mod_bwd_body, grid=(DEPTH, nmc),
        in_specs=[pl.BlockSpec((8, D), lambda l, n: (0, 0)), pl.BlockSpec((None, D, 1536), lambda l, n: (l, 0, n)),
                  pl.BlockSpec((None, 8, 1536), lambda l, n: (l, 0, n)), pl.BlockSpec((None, 8, 1536), lambda l, n: (l, 0, n))],
        out_specs=[pl.BlockSpec((None, D, 1536), lambda l, n: (l, 0, n)), pl.BlockSpec((None, 1, 1536), lambda l, n: (l, 0, n)),
                   pl.BlockSpec((8, D), lambda l, n: (0, 0))],
        out_shape=[jax.ShapeDtypeStruct((DEPTH, D, N_MOD * D), F32), jax.ShapeDtypeStruct((DEPTH, 1, N_MOD * D), F32),
                   jax.ShapeDtypeStruct((8, D), F32)],
        name='mod_bwd', compiler_params=_cparams(("arbitrary", "arbitrary")),
    )(cc, W['w_mod'], dlat, dctx)
    add_grad('w_mod', None, dwmod)
    add_grad('b_mod', None, dbmod[:, 0])
    add_grad('c_ctx', None, dcc[4])
    return loss, grad_x, G


def assemble_grads(G, like):
    out = {}
    for name, parts in G.items():
        shape = like[name].shape
        if None in parts:
            g = parts[None]
        elif isinstance(next(iter(parts)), tuple):
            g = jnp.stack([jnp.stack([parts[(j, d)] for d in range(2)]) for j in range(shape[0])])
        else:
            g = jnp.stack([parts[i] for i in range(shape[0])])
        out[name] = g.reshape(shape)
    return out


XY_RELS = ((1, 0, 0), (0, 1, 0), (1, 1, 0))
ALL_RELS = tuple((dx, dy, dc) for dx in (0, 1) for dy in (0, 1) for dc in (0, 1))[1:]


def exchange(name, src, out_shape, sends):
    n = len(sends)

    def body(src_ref, out_ref, send_sems, recv_sems):
        me = (lax.axis_index("x"), lax.axis_index("y"), lax.axis_index("c"))

        def pick(ref, sel, tgt):
            return ref if sel is None else ref.at[sel(me, tgt)]

        copies = []
        for k, (rel, ssel, dsel) in enumerate(sends):
            tgt = tuple(1 - m if f else m for m, f in zip(me, rel))
            cp = pltpu.make_async_remote_copy(
                src_ref=pick(src_ref, ssel, tgt), dst_ref=pick(out_ref, dsel, tgt),
                send_sem=send_sems.at[k], recv_sem=recv_sems.at[k], device_id=tgt, device_id_type=MESH)
            cp.start()
            copies.append(cp)
        for cp in copies:
            cp.wait()

    return pl.pallas_call(
        body, out_shape=jax.ShapeDtypeStruct(out_shape, src.dtype),
        in_specs=[pl.BlockSpec(memory_space=pl.ANY)], out_specs=pl.BlockSpec(memory_space=pl.ANY),
        scratch_shapes=[pltpu.SemaphoreType.DMA((n,)), pltpu.SemaphoreType.DMA((n,))],
        name=name,
    )(src)


def _xy_index(dev):
    return 2 * dev[0] + dev[1]


def _my_xy():
    return 2 * lax.axis_index("x") + lax.axis_index("y")


def all_gather_xy(name, shard):
    got = exchange(name, shard, (4,) + shard.shape, [(rel, None, lambda me, tgt: _xy_index(me)) for rel in XY_RELS])
    return lax.dynamic_update_index_in_dim(got, shard, _my_xy(), 0)


def reduce_scatter_xy(name, g4):
    got = exchange(name, g4, (3,) + g4.shape[1:],
                   [(rel, (lambda me, tgt: _xy_index(tgt)), (lambda me, tgt, k=k: k)) for k, rel in enumerate(XY_RELS)])
    return got, lax.dynamic_index_in_dim(g4, _my_xy(), 0, keepdims=False)


def sibling_swap(name, v):
    return exchange(name, v, v.shape, [((0, 0, 1), None, None)])


def all_gather_all(name, v):
    got = exchange(name, v, (8,) + v.shape, [(rel, None, lambda me, tgt: 4 * me[0] + 2 * me[1] + me[2]) for rel in ALL_RELS])
    return lax.dynamic_update_index_in_dim(got, v, 2 * _my_xy() + lax.axis_index("c"), 0)


def all_gather_xy_halves(name, shard):
    half = shard.shape[0] // 2

    def body(src_ref, out_ref, send_sems, recv_sems):
        x, y, c = lax.axis_index("x"), lax.axis_index("y"), lax.axis_index("c")
        mine = pl.ds(c * half, half)
        peers = [(1 - x, y), (x, 1 - y), (1 - x, 1 - y)]

        def copy(k, src, dst, to):
            return pltpu.make_async_remote_copy(src_ref=src, dst_ref=dst, send_sem=send_sems.at[k], recv_sem=recv_sems.at[k],
                                                device_id=to, device_id_type=MESH)

        first = [copy(k, src_ref.at[mine], out_ref.at[2 * x + y, mine], (px, py, c)) for k, (px, py) in enumerate(peers)]
        for cp in first:
            cp.start()
        passed = []
        for k, (px, py) in enumerate(peers):
            first[k].wait_recv()
            landed = out_ref.at[2 * px + py, mine]
            fw = copy(3 + k, landed, landed, (x, y, 1 - c))
            fw.start()
            passed.append(fw)
        for fw in passed:
            fw.wait_recv()
        for cp in first + passed:
            cp.wait_send()

    got = pl.pallas_call(
        body, out_shape=jax.ShapeDtypeStruct((4,) + shard.shape, shard.dtype),
        in_specs=[pl.BlockSpec(memory_space=pl.ANY)], out_specs=pl.BlockSpec(memory_space=pl.ANY),
        scratch_shapes=[pltpu.SemaphoreType.DMA((6,)), pltpu.SemaphoreType.DMA((6,))],
        name=name,
    )(shard)
    return lax.dynamic_update_index_in_dim(got, shard, _my_xy(), 0)


def sibling_split(name, g4):
    half = g4.shape[1] // 2
    got = exchange(name, g4, (4, half) + g4.shape[2:],
                   [((0, 0, 1), (lambda me, tgt: (slice(None), pl.ds(tgt[2] * half, half))), None)])
    return got, lax.dynamic_slice_in_dim(g4, lax.axis_index("c") * half, half, axis=1)


def sibling_join(name, q):
    half = q.shape[0]
    got = exchange(name, q, (2 * half,) + q.shape[1:], [((0, 0, 1), None, lambda me, tgt: pl.ds(me[2] * half, half))])
    return lax.dynamic_update_slice_in_dim(got, q, lax.axis_index("c") * half, axis=0)


def _rows_view(shape):
    cols = shape[-1] if len(shape) else 1
    rows = 1
    for s in shape[:-1]:
        rows *= s
    return rows, cols


def _row_block(rows, cols, n_arrays):
    budget = (24 * 1024 * 1024) // (8 * n_arrays * cols)
    if rows <= max(budget, 16):
        return rows
    br = (min(budget, rows) // 16) * 16
    while br > 16 and rows % br:
        br -= 16
    return br if rows % br == 0 else rows


def sum_slots(name, stacked, extra=(), out_dtype=F32):
    k = stacked.shape[0]
    rows, cols = _rows_view(stacked.shape[1:])
    br = _row_block(rows, cols, k + len(extra) + 1)

    def f(s, *more):
        parts = [s[i].astype(F32) for i in range(k)] + [m.astype(F32) for m in more]
        while len(parts) > 1:
            parts = [parts[i] + parts[i + 1] for i in range(0, len(parts) - 1, 2)] + ([parts[-1]] if len(parts) % 2 else [])
        return parts[0]

    out = tile_fwd(name, f, (rows // br,),
                   [(stacked.reshape(k, rows, cols), (k, br, cols), lambda i: (0, i, 0))]
                   + [(e.reshape(rows, cols), (br, cols), lambda i: (i, 0)) for e in extra],
                   [((rows, cols), out_dtype, (br, cols), lambda i: (i, 0))])[0]
    return out.reshape(stacked.shape[1:])


def adamw(name, w, m, v, gs):
    rows, cols = _rows_view(w.shape)
    br = _row_block(rows, cols, 7 + len(gs))
    spec = lambda a: (a.reshape(rows, cols), (br, cols), lambda i: (i, 0))
    outs = tile_fwd(name, f_adamw, (rows // br,), [spec(t) for t in (w, m, v) + tuple(gs)],
                    [((rows, cols), F32, (br, cols), lambda i: (i, 0))] * 4)
    return [o.reshape(w.shape) for o in outs]


IN_NAMES = ['x', 'c', 'ctx'] + W_NAMES + ['loss_target'] + ['m_' + n for n in W_NAMES] + ['v_' + n for n in W_NAMES]
SMALL_PAD = 128 * 1024


def kernel(x, c, ctx, c_ctx, w_mod, b_mod, norm_mix_g, norm_ffn_g, final_norm_g, ev_w_in, ev_w_out, ssd_conv_w, ssd_conv_b, ssd_dt_bias, ssd_a_log, ssd_d, ssd_norm_g, lru_conv_w, lru_conv_b, lru_w_a, lru_b_a, lru_w_i, lru_b_i, lru_lam, od_w_in, od_w_out, hg_lb_logits, hg_norm_g, s5_lam_re, s5_lam_im, s5_log_step, s5_b_re, s5_b_im, s5_c_re, s5_c_im, s5_d, s5_glu_w, s5_glu_b, ffn_w_gate, ffn_w_up, ffn_conv_w, ffn_conv_b, ffn_w_down, loss_target, m_c_ctx, m_w_mod, m_b_mod, m_norm_mix_g, m_norm_ffn_g, m_final_norm_g, m_ev_w_in, m_ev_w_out, m_ssd_conv_w, m_ssd_conv_b, m_ssd_dt_bias, m_ssd_a_log, m_ssd_d, m_ssd_norm_g, m_lru_conv_w, m_lru_conv_b, m_lru_w_a, m_lru_b_a, m_lru_w_i, m_lru_b_i, m_lru_lam, m_od_w_in, m_od_w_out, m_hg_lb_logits, m_hg_norm_g, m_s5_lam_re, m_s5_lam_im, m_s5_log_step, m_s5_b_re, m_s5_b_im, m_s5_c_re, m_s5_c_im, m_s5_d, m_s5_glu_w, m_s5_glu_b, m_ffn_w_gate, m_ffn_w_up, m_ffn_conv_w, m_ffn_conv_b, m_ffn_w_down, v_c_ctx, v_w_mod, v_b_mod, v_norm_mix_g, v_norm_ffn_g, v_final_norm_g, v_ev_w_in, v_ev_w_out, v_ssd_conv_w, v_ssd_conv_b, v_ssd_dt_bias, v_ssd_a_log, v_ssd_d, v_ssd_norm_g, v_lru_conv_w, v_lru_conv_b, v_lru_w_a, v_lru_b_a, v_lru_w_i, v_lru_b_i, v_lru_lam, v_od_w_in, v_od_w_out, v_hg_lb_logits, v_hg_norm_g, v_s5_lam_re, v_s5_lam_im, v_s5_log_step, v_s5_b_re, v_s5_b_im, v_s5_c_re, v_s5_c_im, v_s5_d, v_s5_glu_w, v_s5_glu_b, v_ffn_w_gate, v_ffn_w_up, v_ffn_conv_w, v_ffn_conv_b, v_ffn_w_down):
    a = dict(locals())
    W = {}
    for n in W_NAMES:
        w = a[n]
        if n in SHARD_AXIS:
            ax = SHARD_AXIS[n]
            if n in MATMUL_WEIGHTS:
                g4 = all_gather_xy_halves('ag_' + n, w.astype(BF16))
            else:
                g4 = all_gather_xy('ag_' + n, w)
            shape = list(w.shape)
            shape[ax] *= 4
            W[n] = jnp.moveaxis(g4, 0, ax).reshape(shape)
        else:
            W[n] = w
    e = W['ev_w_in']
    W['ev_w_in_p'] = jnp.concatenate(
        [e[:, :, 0:1024], e[:, :, 2592:3616], e[:, :, 3616:4640], e[:, :, 1024:2560], e[:, :, 2560:2592],
         jnp.zeros((e.shape[0], D, EV_PAD - 4640), e.dtype)], axis=2)

    loss_local, grad_x, G = local_step(a['x'], a['c'], a['ctx'], W['c_ctx'], a['loss_target'], W)
    grads = assemble_grads(G, W)
    loss = lax.psum(loss_local, ("x", "y", "c"))

    res = {}
    for n in W_NAMES:
        if n not in SHARD_AXIS:
            continue
        ax = SHARD_AXIS[n]
        w = a[n]
        gf = grads[n]
        split = gf.reshape(gf.shape[:ax] + (4, w.shape[ax]) + gf.shape[ax + 1:])
        g4 = jnp.moveaxis(split, ax, 0)
        if n in MATMUL_WEIGHTS:
            theirs, ours = sibling_split('rsc_' + n, g4)
            part = sum_slots('csum_' + n, theirs[None], (ours,), BF16)
            got, own = reduce_scatter_xy('rs_' + n, part)
            half = sum_slots('gsum_' + n, got, (own,))
            res[n] = adamw('adamw_' + n, w, a['m_' + n], a['v_' + n], (sibling_join('agc_' + n, half),))
        else:
            got, own = reduce_scatter_xy('rs_' + n, g4)
            mine = sum_slots('gsum_' + n, got, (own,))
            other = sibling_swap('sw_' + n, mine)
            res[n] = adamw('adamw_' + n, w, a['m_' + n], a['v_' + n], (mine, other))
    small = [n for n in W_NAMES if n not in SHARD_AXIS]
    flat = jnp.concatenate([grads[n].reshape(-1) for n in small])
    total = flat.shape[0]
    padded = -(-total // SMALL_PAD) * SMALL_PAD
    flat = jnp.concatenate([flat, jnp.zeros((padded - total,), F32)]).reshape(padded // 128, 128)
    summed = sum_slots('gsum_small', all_gather_all('ag_small', flat)).reshape(-1)
    off = 0
    for n in small:
        size = math.prod(a[n].shape)
        g = summed[off:off + size].reshape(a[n].shape)
        off += size
        res[n] = adamw('adamw_' + n, a[n], a['m_' + n], a['v_' + n], (g,))
    outs = [loss, grad_x]
    for k in range(4):
        outs += [res[n][k] for n in W_NAMES]
    return tuple(outs)
```

```python
import functools
import math

import jax
import jax.numpy as jnp
from jax import lax
from jax.experimental import pallas as pl
from jax.experimental.pallas import tpu as pltpu

F32 = jnp.float32
BF16 = jnp.bfloat16
HI = lax.Precision.HIGHEST
MESH = pl.DeviceIdType.MESH

D = 1024
DEPTH = 4
N_MOD = 6
RMS_EPS = 1e-6
GRID_W = 64
SSD_HEADS = 16
SSD_CHUNK = 128
HG_W = 768
HG_HEADS = 6
HG_CHUNK = 16
HG_PER_STEP = 3
HG_GROUPS = HG_HEADS // HG_PER_STEP
HG_GW = 128 * HG_PER_STEP
S5_W = 256
D_FF = 2816
EV_PAD = 5120
LRU_C = 8.0
V7X_VMEM_LIMIT = 56 * 1024 * 1024
MM_VMEM_BUDGET = 36 * 1024 * 1024

ADAM_LR, ADAM_B1, ADAM_B2, ADAM_EPS, ADAM_WD, ADAM_STEP = 0.001, 0.9, 0.999, 1e-08, 0.01, 10

W_NAMES = ['c_ctx', 'w_mod', 'b_mod', 'norm_mix_g', 'norm_ffn_g', 'final_norm_g', 'ev_w_in', 'ev_w_out', 'ssd_conv_w',
           'ssd_conv_b', 'ssd_dt_bias', 'ssd_a_log', 'ssd_d', 'ssd_norm_g', 'lru_conv_w', 'lru_conv_b', 'lru_w_a', 'lru_b_a',
           'lru_w_i', 'lru_b_i', 'lru_lam', 'od_w_in', 'od_w_out', 'hg_lb_logits', 'hg_norm_g', 's5_lam_re', 's5_lam_im',
           's5_log_step', 's5_b_re', 's5_b_im', 's5_c_re', 's5_c_im', 's5_d', 's5_glu_w', 's5_glu_b', 'ffn_w_gate', 'ffn_w_up',
           'ffn_conv_w', 'ffn_conv_b', 'ffn_w_down']
SHARD_AXIS = {'w_mod': 2, 'ev_w_in': 2, 'ev_w_out': 1, 'ssd_conv_w': 2, 'lru_conv_w': 2, 'lru_b_a': 2, 'lru_b_i': 2,
              'lru_lam': 2, 'od_w_in': 2, 'od_w_out': 1, 's5_d': 1, 's5_glu_w': 1, 's5_glu_b': 1, 'ffn_w_gate': 2,
              'ffn_w_up': 2, 'ffn_conv_w': 3, 'ffn_w_down': 1}
MATMUL_WEIGHTS = ('w_mod', 'ev_w_in', 'ev_w_out', 'od_w_in', 'od_w_out', 'ffn_w_gate', 'ffn_w_up', 'ffn_w_down')


def _cparams(sem=None):
    return pltpu.CompilerParams(vmem_limit_bytes=V7X_VMEM_LIMIT, dimension_semantics=sem)


def _pick(n, cands):
    for c in cands:
        if n % c == 0:
            return c
    return n


def tile_fwd(name, f, grid, ins, outs):
    n_in = len(ins)

    def body(*refs):
        res = f(*[r[...] for r in refs[:n_in]])
        if not isinstance(res, (tuple, list)):
            res = (res,)
        for r, o in zip(res, refs[n_in:]):
            o[...] = r.astype(o.dtype)

    res = pl.pallas_call(
        body, grid=grid,
        in_specs=[pl.BlockSpec(b, m) for _, b, m in ins],
        out_specs=[pl.BlockSpec(b, m) for _, _, b, m in outs],
        out_shape=[jax.ShapeDtypeStruct(s, d) for s, d, _, _ in outs],
        name=name, compiler_params=_cparams(("arbitrary",) * len(grid)),
    )(*[a for a, _, _ in ins])
    return res


def tile_bwd(name, f, grid, ins, cts, grads, prims=()):
    n_in = len(ins)
    ct_flat = [p for c in cts for p in c]
    n_ct = len(ct_flat)
    didx = [g[0] for g in grads]

    def body(*refs):
        in_refs, ct_refs = refs[:n_in], refs[n_in:n_in + n_ct]
        g_refs = refs[n_in + n_ct:n_in + n_ct + len(grads)]
        p_refs = refs[n_in + n_ct + len(grads):]
        vals = [r[...] for r in in_refs]

        def fd(*dv):
            full = list(vals)
            for i, v in zip(didx, dv):
                full[i] = v
            res = f(*full)
            return tuple(res) if isinstance(res, (tuple, list)) else (res,)

        out, vjp = jax.vjp(fd, *[vals[i] for i in didx])
        ctv, k = [], 0
        for o, c in zip(out, cts):
            acc = None
            for _ in c:
                piece = ct_refs[k][...].astype(o.dtype)
                acc = piece if acc is None else acc + piece
                k += 1
            ctv.append(jnp.zeros_like(o) if acc is None else acc.reshape(o.shape))
        gs = vjp(tuple(ctv))
        ids = [pl.program_id(a) for a in range(len(grid))]

        def emit(ref, val, first):
            if first is None:
                ref[...] = val.astype(ref.dtype)
            else:
                is_first = first(*ids)

                @pl.when(is_first)
                def _():
                    ref[...] = val.astype(ref.dtype)

                @pl.when(jnp.logical_not(is_first))
                def _():
                    ref[...] += val.astype(ref.dtype)

        for g, spec, ref in zip(gs, grads, g_refs):
            emit(ref, g, spec[5])
        for spec, ref in zip(prims, p_refs):
            emit(ref, out[spec[0]], spec[5])

    specs = list(grads) + list(prims)
    res = pl.pallas_call(
        body, grid=grid,
        in_specs=[pl.BlockSpec(b, m) for _, b, m in list(ins) + ct_flat],
        out_specs=[pl.BlockSpec(s[3], s[4]) for s in specs],
        out_shape=[jax.ShapeDtypeStruct(s[1], s[2]) for s in specs],
        name=name, compiler_params=_cparams(("arbitrary",) * len(grid)),
    )(*[a for a, _, _ in list(ins) + ct_flat])
    return res


def mm(name, pairs, ta=False, tb=False, out_dtype=F32):
    a0, b0, _ = pairs[0]
    m = a0.shape[1] if ta else a0.shape[0]
    n = b0.shape[0] if tb else b0.shape[1]
    cands = (1024, 1408, 768, 512, 256, 128)
    tks, nks = [], []
    for a, b, _ in pairs:
        k = a.shape[0] if ta else a.shape[1]
        tk = _pick(k, cands)
        tks.append(tk)
        nks.append(k // tk)

    def vmem_bytes(tm, tn):
        tiles = sum(2 * tk * (tm * a.dtype.itemsize + tn * b.dtype.itemsize) for (a, b, _), tk in zip(pairs, tks))
        return tiles + tm * tn * (4 + 2 * jnp.dtype(out_dtype).itemsize)

    tm_c = [c_ for c_ in cands if m % c_ == 0] or [m]
    tn_c = [c_ for c_ in cands if n % c_ == 0] or [n]
    tm, tn = tm_c[0], tn_c[0]
    while vmem_bytes(tm, tn) > MM_VMEM_BUDGET and (len(tm_c) > 1 or len(tn_c) > 1):
        if len(tm_c) > 1 and (tm >= tn or len(tn_c) == 1):
            tm_c = tm_c[1:]
        else:
            tn_c = tn_c[1:]
        tm, tn = tm_c[0], tn_c[0]
    starts = [sum(nks[:p]) for p in range(len(pairs))]
    nk = sum(nks)
    np_ = len(pairs)

    def body(*refs):
        o_ref, acc = refs[2 * np_], refs[2 * np_ + 1]
        kk = pl.program_id(2)

        @pl.when(kk == 0)
        def _():
            acc[...] = jnp.zeros_like(acc)

        for p in range(np_):
            def add(p=p):
                a = refs[2 * p][...].astype(BF16)
                b = refs[2 * p + 1][...].astype(BF16)
                dn = (((0 if ta else 1,), (1 if tb else 0,)), ((), ()))
                acc[...] += lax.dot_general(a, b, dn, preferred_element_type=F32)
            if np_ == 1:
                add()
            else:
                pl.when((kk >= starts[p]) & (kk < starts[p] + nks[p]))(add)

        @pl.when(kk == nk - 1)
        def _():
            o_ref[...] = acc[...].astype(o_ref.dtype)

    in_specs, args = [], []
    for p, (a, b, off) in enumerate(pairs):
        tk, s0, nkp = tks[p], starts[p], nks[p]
        assert off % tk == 0
        boff = off // tk

        def kloc(k, s0=s0, nkp=nkp):
            return jnp.clip(k - s0, 0, nkp - 1)
        if ta:
            in_specs.append(pl.BlockSpec((tk, tm), lambda i, j, k, kloc=kloc: (kloc(k), i)))
        else:
            in_specs.append(pl.BlockSpec((tm, tk), lambda i, j, k, kloc=kloc: (i, kloc(k))))
        if tb:
            in_specs.append(pl.BlockSpec((tn, tk), lambda i, j, k, kloc=kloc, boff=boff: (j, boff + kloc(k))))
        else:
            in_specs.append(pl.BlockSpec((tk, tn), lambda i, j, k, kloc=kloc, boff=boff: (boff + kloc(k), j)))
        args += [a, b]
    return pl.pallas_call(
        body, grid=(m // tm, n // tn, nk), in_specs=in_specs,
        out_specs=pl.BlockSpec((tm, tn), lambda i, j, k: (i, j)),
        out_shape=jax.ShapeDtypeStruct((m, n), out_dtype),
        scratch_shapes=[pltpu.VMEM((tm, tn), F32)],
        name=name, compiler_params=_cparams(("arbitrary", "arbitrary", "arbitrary")),
    )(*args)


def _rms(x, g):
    return x * lax.rsqrt(jnp.mean(x * x, axis=-1, keepdims=True) + RMS_EPS) * g


def f_nm0(x, g, sh, sc):
    return _rms(x, g) * (1.0 + sc) + sh


def f_nm(xp, o, gate, g, sh, sc):
    x = xp + gate * o
    return x, _rms(x, g) * (1.0 + sc) + sh


def f_final(xp, o, gate, g, tgt, valid):
    x = xp + gate * o
    e = (_rms(x, g) - tgt) * valid
    return jnp.sum(e * e, axis=0, keepdims=True) * (0.5 / D)


@functools.partial(jax.custom_vjp, nondiff_argnums=(1,))
def _sroll(x, s):
    return pltpu.roll(x, s, 0)


def _sroll_fwd(x, s):
    return pltpu.roll(x, s, 0), None


def _sroll_bwd(s, _, g):
    return (pltpu.roll(g, (g.shape[0] - s) % g.shape[0], 0),)


_sroll.defvjp(_sroll_fwd, _sroll_bwd)


def _shifted(x, o):
    n = x.shape[0]
    return x if o == 0 else _sroll(x, (n - o) % n)


def f_conv1d(x, w, b, *, lc, act):
    n = x.shape[0]
    pos = lax.broadcasted_iota(jnp.int32, (n, 1), 0)
    lo = jnp.where(pos < lc, 0, lc)
    hi = jnp.where(pos < lc, lc, n)
    y = x * w[1:2] + b
    for k, o in ((0, -1), (2, 1), (3, 2)):
        src = pos + o
        valid = (src >= lo) & (src < hi)
        y = y + jnp.where(valid, _shifted(x, o), 0.0) * w[k:k + 1]
    return jax.nn.silu(y) if act else y


def ffnconv_masks(n, lc):
    pos = lax.broadcasted_iota(jnp.int32, (n, 128), 0)
    is_ctx = pos < lc
    tl = pos - lc
    r = tl // GRID_W
    cc = tl - r * GRID_W
    rows = (n - lc) // GRID_W
    left = jnp.where(is_ctx, pos >= 1, cc >= 1)
    right = jnp.where(is_ctx, pos < lc - 1, cc < GRID_W - 1)
    above = jnp.logical_not(is_ctx) & (r >= 1)
    below = jnp.logical_not(is_ctx) & (r < rows - 1)
    return jnp.stack([left, right, above, below]).astype(F32)


def f_ffnconv(a, up, w, b, mk):
    cols = (mk[0] * _shifted(a, -1), a, mk[1] * _shifted(a, 1))
    y = b
    for dr in (-1, 0, 1):
        k = 3 * (dr + 1)
        inner = cols[0] * w[k:k + 1] + cols[1] * w[k + 1:k + 2] + cols[2] * w[k + 2:k + 3]
        y = y + (inner if dr == 0 else mk[2 + (dr > 0)] * _shifted(inner, GRID_W * dr))
    return jax.nn.silu(y) * up


def f_ssd(xs, bc, dtraw, bias, alog, st, *, d, reverse):
    L = xs.shape[0]
    dtv = jax.nn.softplus(dtraw + bias)
    la = dtv * (-jnp.exp(alog))
    ri = lax.broadcasted_iota(jnp.int32, (L, L), 0)
    ci = lax.broadcasted_iota(jnp.int32, (L, L), 1)
    mask = (ci >= ri) if reverse else (ci <= ri)
    cum = jnp.dot(mask.astype(F32), la, precision=HI, preferred_element_type=F32)
    cum_t = cum.T
    tot = cum[0:1] if reverse else cum[L - 1:L]
    lo = lax.broadcasted_iota(jnp.int32, (1, 128), 1) < 64
    rlo = lax.broadcasted_iota(jnp.int32, (128, 1), 0) < 64
    ys, new = [], []
    for g in range(2):
        bg = bc[:, g * 128:(g + 1) * 128].astype(BF16)
        cg = bc[:, 256 + g * 128:256 + (g + 1) * 128].astype(BF16)
        cb = lax.dot_general(cg, bg, (((1,), (1,)), ((), ())), preferred_element_type=F32)
        sg = st[4 * g:4 * g + 4].reshape(4 * 128, 128)
        ch_all = lax.dot_general(cg, sg.astype(BF16), (((1,), (1,)), ((), ())), preferred_element_type=F32)
        xes, dcols = [], []
        for jj in range(4):
            j = 4 * g + jj
            x = xs[:, j * 128:(j + 1) * 128]
            k1 = 16 * d + 2 * j
            k2 = k1 + 1
            c1, c2 = cum[:, k1:k1 + 1], cum[:, k2:k2 + 1]
            m1 = cb * jnp.exp(jnp.where(mask, c1 - cum_t[k1:k1 + 1, :], -1e30))
            m2 = cb * jnp.exp(jnp.where(mask, c2 - cum_t[k2:k2 + 1, :], -1e30))
            xdt = x * jnp.where(lo, dtv[:, k1:k1 + 1], dtv[:, k2:k2 + 1])
            mcat = jnp.concatenate([m1, m2], axis=1).astype(BF16)
            xcat = jnp.concatenate([jnp.where(lo, xdt, 0.0), jnp.where(lo, 0.0, xdt)], axis=0).astype(BF16)
            y = jnp.dot(mcat, xcat, preferred_element_type=F32)
            y = y + ch_all[:, jj * 128:(jj + 1) * 128] * jnp.where(lo, jnp.exp(c1), jnp.exp(c2))
            t1, t2 = tot[:, k1:k1 + 1], tot[:, k2:k2 + 1]
            xes.append((xdt * jnp.where(lo, jnp.exp(t1 - c1), jnp.exp(t2 - c2))).astype(BF16))
            dcols.append(jnp.where(rlo, jnp.exp(t1), jnp.exp(t2)))
            ys.append(y)
        upd = lax.dot_general(jnp.concatenate(xes, axis=1), bg, (((0,), (0,)), ((), ())), preferred_element_type=F32)
        new.append((sg * jnp.concatenate(dcols, axis=0) + upd).reshape(4, 128, 128))
    return jnp.concatenate(ys, axis=1), jnp.concatenate(new, axis=0)


def f_hgrn(q_raw, f_raw, v, lb, zt, *, reverse):
    n = q_raw.shape[0]
    c = HG_CHUNK
    qa = jax.nn.silu(q_raw)
    logf = jnp.log(lb + (1.0 - lb) * jax.nn.sigmoid(f_raw))
    kk = (1.0 - lb) * jax.nn.sigmoid(-f_raw)
    ri = lax.broadcasted_iota(jnp.int32, (n, n), 0)
    ci = lax.broadcasted_iota(jnp.int32, (n, n), 1)
    tmat = ((ri // c == ci // c) & ((ci >= ri) if reverse else (ci <= ri))).astype(F32)
    cum_all = jnp.dot(tmat, logf, precision=HI, preferred_element_type=F32)
    r3 = lax.broadcasted_iota(jnp.int32, (c, c, 128), 0)
    c3 = lax.broadcasted_iota(jnp.int32, (c, c, 128), 1)
    mask3 = (c3 >= r3) if reverse else (c3 <= r3)
    nch = n // c
    outs = [None] * nch
    for chn in (reversed(range(nch)) if reverse else range(nch)):
        sl = slice(chn * c, (chn + 1) * c)
        q, k, vv, cum = qa[sl], kk[sl], v[sl], cum_all[sl]
        dec = jnp.exp(jnp.where(mask3, cum[:, None, :] - cum[None, :, :], -1e30))
        att = jnp.sum(q[:, None, :] * dec * k[None, :, :], axis=-1, keepdims=True)
        y = jnp.sum(att * vv[None, :, :], axis=1)
        y = y + lax.dot_general((q * jnp.exp(cum)).astype(BF16), zt.astype(BF16), (((1,), (1,)), ((), ())),
                                preferred_element_type=F32)
        tot = cum[0:1] if reverse else cum[c - 1:c]
        kd = (k * jnp.exp(tot - cum)).astype(BF16)
        zt = zt * jnp.exp(tot) + lax.dot_general(vv.astype(BF16), kd, (((0,), (0,)), ((), ())), preferred_element_type=F32)
        outs[chn] = y
    return jnp.concatenate(outs, axis=0), zt


def f_hgrn_group(q_raw, f_raw, v, lb, zt, *, reverse):
    ys, zs = [], []
    for h in range(HG_PER_STEP):
        sl = slice(h * 128, (h + 1) * 128)
        y, z = f_hgrn(q_raw[:, sl], f_raw[:, sl], v[:, sl], lb[:, sl], zt[sl], reverse=reverse)
        ys.append(y)
        zs.append(z)
    return jnp.concatenate(ys, axis=1), jnp.concatenate(zs, axis=0)


def _expm1(x):
    poly = x * (1.0 + x * (0.5 + x * (1.0 / 6 + x * (1.0 / 24 + x * (1.0 / 120 + x * (1.0 / 720))))))
    return jnp.where(jnp.abs(x) < 0.3, poly, jnp.exp(x) - 1.0)


def f_gates(u, wa, ba, wi, bi, lam):
    rs, is_ = [], []
    for nb in range(8):
        un = u[:, nb * 128:(nb + 1) * 128].astype(BF16)
        rs.append(jnp.dot(un, wa[nb].astype(BF16), preferred_element_type=F32))
        is_.append(jnp.dot(un, wi[nb].astype(BF16), preferred_element_type=F32))
    r = jax.nn.sigmoid(jnp.concatenate(rs, axis=1) + ba)
    i = jax.nn.sigmoid(jnp.concatenate(is_, axis=1) + bi)
    log_a = -LRU_C * jax.nn.softplus(-lam) * r
    return jnp.exp(log_a), jnp.sqrt(-_expm1(2.0 * log_a)) * (i * u)


def f_ssdfin(y0, y1, xs, z, gy, h0, h1, dpad, ng):
    kk = lax.broadcasted_iota(jnp.int32, (128, D), 0)
    ch = lax.broadcasted_iota(jnp.int32, (128, D), 1)
    expand = (ch // 64 == kk).astype(F32)
    dvec = jnp.dot(dpad, expand, precision=HI, preferred_element_type=F32)[0:1]
    y = y0 + y1 + dvec * xs
    yn = _rms(y * jax.nn.silu(z), ng)
    r = (h0 + h1) * jax.nn.gelu(gy)
    return jnp.concatenate([yn, r], axis=1)


def f_oddfin(o0, o1, g, y0, y1, u, hn, sd, gw, gb):
    parts = []
    for h in range(HG_HEADS):
        sl = slice(h * 128, (h + 1) * 128)
        parts.append(_rms(o0[:, sl] + o1[:, sl], hn[h:h + 1]) * jax.nn.silu(g[:, sl]))
    y = jax.nn.gelu(y0 + y1 + sd * u)
    y = y * jax.nn.sigmoid(jnp.dot(y.astype(BF16), gw.astype(BF16), preferred_element_type=F32) + gb)
    return jnp.concatenate(parts + [y], axis=1)


def f_s5p(lre, lim, lstep, btr, bti):
    step = jnp.exp(lstep)
    mag = jnp.exp(lre * step)
    ar, ai = mag * jnp.cos(lim * step), mag * jnp.sin(lim * step)
    den = lre * lre + lim * lim
    zr = ((ar - 1.0) * lre + ai * lim) / den
    zi = (ai * lre - (ar - 1.0) * lim) / den
    bbr = zr[:, None, :] * btr - zi[:, None, :] * bti
    bbi = zr[:, None, :] * bti + zi[:, None, :] * btr
    return ar, ai, bbr, bbi


def f_lb(logits):
    m = jnp.max(logits, axis=0, keepdims=True)
    e = jnp.exp(logits - m)
    p = e / jnp.sum(e, axis=0, keepdims=True)
    return p[1:2], p[1:2] + p[2:3] + p[3:4]


def f_adamw(w, m, v, *gs):
    g = gs[0]
    for t in gs[1:]:
        g = g + t
    m = ADAM_B1 * m + (1.0 - ADAM_B1) * g
    v = ADAM_B2 * v + (1.0 - ADAM_B2) * jnp.square(g)
    m_hat = m / (1.0 - ADAM_B1 ** ADAM_STEP)
    v_hat = v / (1.0 - ADAM_B2 ** ADAM_STEP)
    delta = -ADAM_LR * (m_hat / (jnp.sqrt(v_hat) + ADAM_EPS) + ADAM_WD * w)
    return g, delta, m, v


def scan_fwd(name, f, grid, ins, y_out, st_out, state_shape, is_first):
    n_in = len(ins)

    def body(*refs):
        y_ref, so_ref, st = refs[n_in], refs[n_in + 1], refs[n_in + 2]
        ids = [pl.program_id(a) for a in range(len(grid))]

        @pl.when(is_first(*ids))
        def _():
            st[...] = jnp.zeros_like(st)

        s = st[...]
        so_ref[...] = s
        y, new = f(*[r[...] for r in refs[:n_in]], s)
        y_ref[...] = y.astype(y_ref.dtype)
        st[...] = new

    return pl.pallas_call(
        body, grid=grid,
        in_specs=[pl.BlockSpec(b, m) for _, b, m in ins],
        out_specs=[pl.BlockSpec(y_out[2], y_out[3]), pl.BlockSpec(st_out[2], st_out[3])],
        out_shape=[jax.ShapeDtypeStruct(y_out[0], y_out[1]), jax.ShapeDtypeStruct(st_out[0], st_out[1])],
        scratch_shapes=[pltpu.VMEM(state_shape, F32)],
        name=name, compiler_params=_cparams(("arbitrary",) * len(grid)),
    )(*[a for a, _, _ in ins])


def scan_bwd(name, f, grid, ins, st_in, dy, grads, state_shape, is_first):
    n_in = len(ins)
    didx = [g[0] for g in grads]

    def body(*refs):
        s_ref, dy_ref = refs[n_in], refs[n_in + 1]
        g_refs = refs[n_in + 2:n_in + 2 + len(grads)]
        dst = refs[n_in + 2 + len(grads)]
        ids = [pl.program_id(a) for a in range(len(grid))]

        @pl.when(is_first(*ids))
        def _():
            dst[...] = jnp.zeros_like(dst)

        vals = [r[...] for r in refs[:n_in]]

        def fd(s, *dv):
            full = list(vals)
            for i, v in zip(didx, dv):
                full[i] = v
            return f(*full, s)

        (y, _), vjp = jax.vjp(fd, s_ref[...], *[vals[i] for i in didx])
        gs = vjp((dy_ref[...].astype(y.dtype), dst[...]))
        dst[...] = gs[0]
        for g, spec, ref in zip(gs[1:], grads, g_refs):
            first = spec[5]
            if first is None:
                ref[...] = g.astype(ref.dtype)
            else:
                fst = first(*ids)

                @pl.when(fst)
                def _(ref=ref, g=g):
                    ref[...] = g.astype(ref.dtype)

                @pl.when(jnp.logical_not(fst))
                def _(ref=ref, g=g):
                    ref[...] += g.astype(ref.dtype)

    allin = list(ins) + [st_in, dy]
    return pl.pallas_call(
        body, grid=grid,
        in_specs=[pl.BlockSpec(b, m) for _, b, m in allin],
        out_specs=[pl.BlockSpec(s[3], s[4]) for s in grads],
        out_shape=[jax.ShapeDtypeStruct(s[1], s[2]) for s in grads],
        scratch_shapes=[pltpu.VMEM(state_shape, F32)],
        name=name, compiler_params=_cparams(("arbitrary",) * len(grid)),
    )(*[a for a, _, _ in allin])


def _tile_order(order, nt):
    if order == 'F':
        return (lambda j: j), True
    if order == 'Fb':
        return (lambda j: nt - 1 - j), False
    if order == 'R':
        return (lambda j: jnp.where(j == 0, 0, nt - j)), False
    return (lambda j: jnp.where(j == nt - 1, 0, j + 1)), True


def linrec(name, a, b, order):
    bsz, tt = a.shape[:2]
    tq = _pick(tt, (256, 128))
    nt = tt // tq
    phys, asc = _tile_order(order, nt)

    def body(a_ref, b_ref, h_ref, hp_ref, hs):
        @pl.when(pl.program_id(0) == 0)
        def _():
            hs[...] = jnp.zeros_like(hs)

        def step(i, hcur):
            t = i if asc else tq - 1 - i
            out = []
            for bi in range(bsz):
                hp_ref[bi, t] = hcur[bi]
                hn = a_ref[bi, t] * hcur[bi] + b_ref[bi, t]
                h_ref[bi, t] = hn
                out.append(hn)
            return tuple(out)

        fin = lax.fori_loop(0, tq, step, tuple(hs[bi] for bi in range(bsz)))
        for bi in range(bsz):
            hs[bi] = fin[bi]

    spec = pl.BlockSpec((bsz, tq, 8, 128), lambda j: (0, phys(j), 0, 0))
    return pl.pallas_call(
        body, grid=(nt,), in_specs=[spec, spec], out_specs=[spec, spec],
        out_shape=[jax.ShapeDtypeStruct(a.shape, F32)] * 2,
        scratch_shapes=[pltpu.VMEM((bsz, 8, 128), F32)],
        name=name, compiler_params=_cparams(("arbitrary",)),
    )(a, b)


def linrec_bwd(name, a, dh, hprev, order):
    bsz, tt = a.shape[:2]
    tq = _pick(tt, (256, 128))
    nt = tt // tq
    phys, asc = _tile_order(order, nt)

    def body(a_ref, dh_ref, hp_ref, g_ref, ga_ref, gs, as_):
        @pl.when(pl.program_id(0) == 0)
        def _():
            gs[...] = jnp.zeros_like(gs)
            as_[...] = jnp.zeros_like(as_)

        def step(i, carry):
            t = i if asc else tq - 1 - i
            out = []
            for bi in range(bsz):
                gcur, acur = carry[bi]
                gn = acur * gcur + dh_ref[bi, t]
                g_ref[bi, t] = gn
                ga_ref[bi, t] = gn * hp_ref[bi, t]
                out.append((gn, a_ref[bi, t]))
            return tuple(out)

        fin = lax.fori_loop(0, tq, step, tuple((gs[bi], as_[bi]) for bi in range(bsz)))
        for bi in range(bsz):
            gs[bi] = fin[bi][0]
            as_[bi] = fin[bi][1]

    spec = pl.BlockSpec((bsz, tq, 8, 128), lambda j: (0, phys(j), 0, 0))
    return pl.pallas_call(
        body, grid=(nt,), in_specs=[spec, spec, spec], out_specs=[spec, spec],
        out_shape=[jax.ShapeDtypeStruct(a.shape, F32)] * 2,
        scratch_shapes=[pltpu.VMEM((bsz, 8, 128), F32), pltpu.VMEM((bsz, 8, 128), F32)],
        name=name, compiler_params=_cparams(("arbitrary",)),
    )(a, dh, hprev)


def clinrec(name, x, coef, order):
    bsz, tt = x.shape[:2]
    tq = _pick(tt, (256, 128))
    nt = tt // tq
    phys, asc = _tile_order(order, nt)

    def body(x_ref, c_ref, h_ref, hp_ref, hs):
        @pl.when(pl.program_id(0) == 0)
        def _():
            hs[...] = jnp.zeros_like(hs)

        ar, ai = c_ref[0], c_ref[1]

        def step(i, carry):
            t = i if asc else tq - 1 - i
            out = []
            for bi in range(bsz):
                hr, hi = carry[bi]
                hp_ref[bi, t, 0] = hr
                hp_ref[bi, t, 1] = hi
                nr = ar * hr - ai * hi + x_ref[bi, t, 0]
                ni = ar * hi + ai * hr + x_ref[bi, t, 1]
                h_ref[bi, t, 0] = nr
                h_ref[bi, t, 1] = ni
                out.append((nr, ni))
            return tuple(out)

        fin = lax.fori_loop(0, tq, step, tuple((hs[bi, 0], hs[bi, 1]) for bi in range(bsz)))
        for bi in range(bsz):
            hs[bi, 0] = fin[bi][0]
            hs[bi, 1] = fin[bi][1]

    spec = pl.BlockSpec((bsz, tq, 2, 8, 128), lambda j: (0, phys(j), 0, 0, 0))
    cspec = pl.BlockSpec((2, 8, 128), lambda j: (0, 0, 0))
    return pl.pallas_call(
        body, grid=(nt,), in_specs=[spec, cspec], out_specs=[spec, spec],
        out_shape=[jax.ShapeDtypeStruct(x.shape, F32)] * 2,
        scratch_shapes=[pltpu.VMEM((bsz, 2, 8, 128), F32)],
        name=name, compiler_params=_cparams(("arbitrary",)),
    )(x, coef)


def clinrec_bwd(name, dh, hprev, coef, order):
    bsz, tt = dh.shape[:2]
    tq = _pick(tt, (256, 128))
    nt = tt // tq
    phys, asc = _tile_order(order, nt)

    def body(d_ref, hp_ref, c_ref, g_ref, dc_ref, gs):
        @pl.when(pl.program_id(0) == 0)
        def _():
            gs[...] = jnp.zeros_like(gs)
            dc_ref[...] = jnp.zeros_like(dc_ref)

        ar, ai = c_ref[0], c_ref[1]

        def step(i, carry):
            t = i if asc else tq - 1 - i
            gcar, dar, dai = carry
            out = []
            for bi in range(bsz):
                gr, gi = gcar[bi]
                nr = ar * gr + ai * gi + d_ref[bi, t, 0]
                ni = ar * gi - ai * gr + d_ref[bi, t, 1]
                g_ref[bi, t, 0] = nr
                g_ref[bi, t, 1] = ni
                hpr, hpi = hp_ref[bi, t, 0], hp_ref[bi, t, 1]
                dar = dar + nr * hpr + ni * hpi
                dai = dai + ni * hpr - nr * hpi
                out.append((nr, ni))
            return tuple(out), dar, dai

        z = jnp.zeros((8, 128), F32)
        fin, dar, dai = lax.fori_loop(0, tq, step, (tuple((gs[bi, 0], gs[bi, 1]) for bi in range(bsz)), z, z))
        for bi in range(bsz):
            gs[bi, 0] = fin[bi][0]
            gs[bi, 1] = fin[bi][1]
        dc_ref[0] += dar
        dc_ref[1] += dai

    spec = pl.BlockSpec((bsz, tq, 2, 8, 128), lambda j: (0, phys(j), 0, 0, 0))
    cspec = pl.BlockSpec((2, 8, 128), lambda j: (0, 0, 0))
    return pl.pallas_call(
        body, grid=(nt,), in_specs=[spec, spec, cspec], out_specs=[spec, cspec],
        out_shape=[jax.ShapeDtypeStruct(dh.shape, F32), jax.ShapeDtypeStruct((2, 8, 128), F32)],
        scratch_shapes=[pltpu.VMEM((bsz, 2, 8, 128), F32)],
        name=name, compiler_params=_cparams(("arbitrary",)),
    )(dh, hprev, coef)


def _blockdiag(bb):
    eye = jnp.eye(16, dtype=bb.dtype)
    return (bb[:, :, None, :] * eye[:, None, :, None]).reshape(256, 1024)


def _blockdiag_t(c):
    eye = jnp.eye(16, dtype=c.dtype)
    return (jnp.swapaxes(c, 1, 2)[:, :, None, :] * eye[:, None, :, None]).reshape(1024, 256)


def _unblockdiag(m):
    eye = jnp.eye(16, dtype=m.dtype)
    return jnp.sum(m.reshape(16, 16, 16, 64) * eye[:, None, :, None], axis=2)


def _unblockdiag_t(m):
    eye = jnp.eye(16, dtype=m.dtype)
    return jnp.swapaxes(jnp.sum(m.reshape(16, 64, 16, 16) * eye[:, None, :, None], axis=2), 1, 2)


def _pad_rows(v, rows=8, cols=128):
    out = jnp.zeros((rows, cols), F32)
    return out.at[0, :v.shape[0]].set(v)


def local_step(x, c, ctx, c_ctx, target, W):
    B, Tx, _ = x.shape
    Lc = ctx.shape[1]
    Tt = Lc + Tx
    tb = Lc
    nt = Tt // tb
    M = B * Tt
    nc = Tt // SSD_CHUNK
    ncc = Lc // SSD_CHUNK
    G = {}

    def add_grad(name, idx, val):
        G.setdefault(name, {})[idx] = val

    def tok(a, cb=None, off=0):
        cb = a.shape[-1] if cb is None else cb
        return (a, (None, tb, cb), lambda b, j, off=off: (b, j, off))

    def tok_out(cols, dtype=F32):
        return ((B, Tt, cols), dtype, (None, tb, cols), lambda b, j: (b, j, 0))

    def vec(a):
        return (a, a.shape, lambda *ids, n=a.ndim: (0,) * n)

    def vec_acc(shape):
        return (shape, F32, shape, lambda *ids, n=len(shape): (0,) * n, lambda *ids: functools.reduce(jnp.logical_and, [i == 0 for i in ids]))

    def modv(l, which):
        return (modr, (None, None, None, 1, D), lambda b, j, l=l, which=which: (l, jnp.where(j == 0, 4, b), which, 0, 0))

    dmod_spec = ((B, 2, 1, D), F32, (None, None, 1, D), lambda b, j: (b, jnp.where(j == 0, 0, 1), 0, 0), lambda b, j: j <= 1)

    def phys_chunk(n_all, n_ctx, reverse):
        if not reverse:
            return lambda s: s
        return lambda s: jnp.where(s < n_ctx, n_ctx - 1 - s, n_all - 1 - (s - n_ctx))

    cc = jnp.zeros((8, D), F32).at[:B].set(c).at[4].set(c_ctx)
    nmc = N_MOD * D // 1536

    def f_mod(ccv, w, b):
        return jnp.dot(jax.nn.silu(ccv).astype(BF16), w, preferred_element_type=F32) + b

    mod = tile_fwd('mod_fwd', f_mod, (DEPTH, nmc),
                   [(cc, (8, D), lambda l, n: (0, 0)), (W['w_mod'], (None, D, 1536), lambda l, n: (l, 0, n)),
                    (W['b_mod'].reshape(DEPTH, 1, N_MOD * D), (None, 1, 1536), lambda l, n: (l, 0, n))],
                   [((DEPTH, 8, N_MOD * D), F32, (None, 8, 1536), lambda l, n: (l, 0, n))])[0]
    modr = mod.reshape(DEPTH, 8, N_MOD, 1, D)
    dmods = {}

    lb1, lb3 = tile_fwd('lb_fwd', f_lb, (1,), [vec(W['hg_lb_logits'])],
                        [((1, HG_W), F32, (1, HG_W), lambda i: (0, 0))] * 2)
    dlb = {1: [], 3: []}

    x0 = jnp.concatenate([ctx, x], axis=1)
    conv_mk = ffnconv_masks(Tt, Lc)
    R = [dict() for _ in range(DEPTH)]

    xprev, oprev = x0, None
    for l in range(DEPTH):
        r = R[l]
        j = l // 2
        ng = W['norm_mix_g'][l][None]
        if l == 0:
            h1 = tile_fwd(f'nm0_fwd', f_nm0, (B, nt), [tok(xprev), vec(ng), modv(l, 0), modv(l, 1)], [tok_out(D, BF16)])[0]
            xa = xprev
        else:
            xa, h1 = tile_fwd(f'nm_mix_fwd{l}', f_nm, (B, nt),
                              [tok(xprev), tok(oprev), modv(l - 1, 5), vec(ng), modv(l, 0), modv(l, 1)],
                              [tok_out(D), tok_out(D, BF16)])
        r['xin'], r['oin'], r['xa'], r['h1'] = xprev, oprev, xa, h1
        h1m = h1.reshape(M, D)
        if l % 2 == 0:
            win = W['ev_w_in_p'][j]
            proj = mm(f'ev_proj{l}', [(h1m, win, 0)]).reshape(B, Tt, EV_PAD)
            r['proj'] = proj
            scw, scb = W['ssd_conv_w'][j], W['ssd_conv_b'][j][None]
            lcw, lcb = W['lru_conv_w'][j], W['lru_conv_b'][j][None]

            def conv_call(name, colblk0, w, b, wblk0, ncols, act):
                return tile_fwd(name, functools.partial(f_conv1d, lc=Lc, act=act), (ncols // 256, B),
                                [(proj, (None, Tt, 256), lambda cb, bi: (bi, 0, colblk0 + cb)),
                                 (w, (4, 256), lambda cb, bi: (0, wblk0 + cb)), (b, (1, 256), lambda cb, bi: (0, wblk0 + cb))],
                                [((B, Tt, ncols), F32, (None, Tt, 256), lambda cb, bi: (bi, 0, cb))])[0]
            xs_c = conv_call(f'conv_xs{l}', 12, scw, scb, 0, 1024, True)
            bc_c = conv_call(f'conv_bc{l}', 16, scw, scb, 4, 512, True)
            u_c = conv_call(f'conv_u{l}', 8, lcw, lcb, 0, 1024, False)
            r['xs'], r['bc'], r['u'] = xs_c, bc_c, u_c
            bias = _pad_rows(W['ssd_dt_bias'][j].reshape(-1), 1)
            alog = _pad_rows(W['ssd_a_log'][j].reshape(-1), 1)
            r['bias'], r['alog'] = bias, alog
            r['y'], r['st'], r['a4'], r['hp4'], r['h'] = [], [], [], [], []
            for d in range(2):
                ph = phys_chunk(nc, ncc, d == 1)
                y, st = scan_fwd(
                    f'ssd_fwd{l}_{d}', functools.partial(f_ssd, d=d, reverse=(d == 1)), (B, nc),
                    [(xs_c, (None, SSD_CHUNK, D), lambda b, s, ph=ph: (b, ph(s), 0)),
                     (bc_c, (None, SSD_CHUNK, 512), lambda b, s, ph=ph: (b, ph(s), 0)),
                     (proj, (None, SSD_CHUNK, 128), lambda b, s, ph=ph: (b, ph(s), 36)),
                     vec(bias), vec(alog)],
                    ((B, Tt, D), F32, (None, SSD_CHUNK, D), lambda b, s, ph=ph: (b, ph(s), 0)),
                    ((B, nc, 8, 128, 128), F32, (None, None, 8, 128, 128), lambda b, s: (b, s, 0, 0, 0)),
                    (8, 128, 128), lambda b, s: s == 0)
                r['y'].append(y)
                r['st'].append(st)
                a_d, bx_d = tile_fwd(
                    f'gates_fwd{l}_{d}', f_gates, (B, nt),
                    [tok(u_c), vec(W['lru_w_a'][j, d]), vec(W['lru_b_a'][j, d][None]), vec(W['lru_w_i'][j, d]),
                     vec(W['lru_b_i'][j, d][None]), vec(W['lru_lam'][j, d][None])],
                    [tok_out(D), tok_out(D)])
                a4 = a_d.reshape(B, Tt, 8, 128)
                h4, hp4 = linrec(f'lru_fwd{l}_{d}', a4, bx_d.reshape(B, Tt, 8, 128), 'F' if d == 0 else 'R')
                r['a4'].append(a4)
                r['hp4'].append(hp4)
                r['h'].append(h4.reshape(B, Tt, D))
            dpad = _pad_rows(W['ssd_d'][j])
            sng = W['ssd_norm_g'][j][None]
            r['dpad'], r['sng'] = dpad, sng
            mix = tile_fwd(f'ssdfin_fwd{l}', f_ssdfin, (B, nt),
                           [tok(r['y'][0]), tok(r['y'][1]), tok(xs_c), tok(proj, D, 0), tok(proj, D, 1), tok(r['h'][0]),
                            tok(r['h'][1]), vec(dpad), vec(sng)], [tok_out(2 * D, BF16)])[0]
            wout = W['ev_w_out'][j]
        else:
            win = W['od_w_in'][j]
            proj = mm(f'od_proj{l}', [(h1m, win, 0)]).reshape(B, Tt, 4096)
            r['proj'] = proj
            lbv = lb1 if l == 1 else lb3
            ns = nc
            r['o'], r['zst'], r['coef'], r['bcat'], r['ccat'], r['hp5'], r['hcat'], r['yd'], r['s5in'] = [], [], [], [], [], [], [], [], []
            u2 = proj[:, :, 3840:].reshape(M, S5_W)
            r['u2'] = u2
            for d in range(2):
                ph = phys_chunk(ns, ncc, d == 1)
                o_d, zst = scan_fwd(
                    f'hgrn_fwd{l}_{d}', functools.partial(f_hgrn_group, reverse=(d == 1)), (HG_GROUPS, B, ns),
                    [(proj, (None, 128, HG_GW), lambda h, b, s, ph=ph: (b, ph(s), h)),
                     (proj, (None, 128, HG_GW), lambda h, b, s, ph=ph, d=d: (b, ph(s), (1 + d) * HG_GROUPS + h)),
                     (proj, (None, 128, HG_GW), lambda h, b, s, ph=ph: (b, ph(s), 3 * HG_GROUPS + h)),
                     (lbv, (1, HG_GW), lambda h, b, s: (0, h))],
                    ((B, Tt, HG_W), F32, (None, 128, HG_GW), lambda h, b, s, ph=ph: (b, ph(s), h)),
                    ((B, HG_GROUPS, ns, HG_GW, 128), F32, (None, None, None, HG_GW, 128), lambda h, b, s: (b, h, s, 0, 0)),
                    (HG_GW, 128), lambda h, b, s: s == 0)
                r['o'].append(o_d)
                r['zst'].append(zst)
                s5in = [W['s5_lam_re'][j, d], W['s5_lam_im'][j, d], W['s5_log_step'][j, d].reshape(16, 1),
                        jnp.swapaxes(W['s5_b_re'][j], 1, 2), jnp.swapaxes(W['s5_b_im'][j], 1, 2)]
                r['s5in'].append(s5in)
                ar, ai, bbr, bbi = tile_fwd(f's5p_fwd{l}_{d}', f_s5p, (1,), [vec(t) for t in s5in],
                                            [((16, 64), F32, (16, 64), lambda i: (0, 0))] * 2
                                            + [((16, 16, 64), F32, (16, 16, 64), lambda i: (0, 0, 0))] * 2)
                coef = jnp.stack([ar.reshape(8, 128), ai.reshape(8, 128)])
                bcat = jnp.concatenate([_blockdiag(bbr), _blockdiag(bbi)], axis=1).astype(BF16)
                ccat = jnp.concatenate([_blockdiag_t(W['s5_c_re'][j, d]), -_blockdiag_t(W['s5_c_im'][j, d])], axis=0).astype(BF16)
                xcat = mm(f's5_in{l}_{d}', [(u2, bcat, 0)])
                h5, hp5 = clinrec(f's5_fwd{l}_{d}', xcat.reshape(B, Tt, 2, 8, 128), coef, 'F' if d == 0 else 'R')
                hcat = h5.reshape(M, 2 * D)
                yd = mm(f's5_out{l}_{d}', [(hcat, ccat, 0)]).reshape(B, Tt, S5_W)
                r['coef'].append(coef)
                r['bcat'].append(bcat)
                r['ccat'].append(ccat)
                r['hp5'].append(hp5)
                r['hcat'].append(hcat)
                r['yd'].append(yd)
            hn = jnp.zeros((8, 128), F32).at[:HG_HEADS].set(W['hg_norm_g'][j])
            sd, gw, gb = W['s5_d'][j][None], W['s5_glu_w'][j], W['s5_glu_b'][j][None]
            r['fin_par'] = (hn, sd, gw, gb)
            mix = tile_fwd(f'oddfin_fwd{l}', f_oddfin, (B, nt),
                           [tok(r['o'][0]), tok(r['o'][1]), tok(proj, HG_W, 4), tok(r['yd'][0]), tok(r['yd'][1]),
                            tok(proj, S5_W, 15), vec(hn), vec(sd), vec(gw), vec(gb)], [tok_out(D, BF16)])[0]
            wout = W['od_w_out'][j]
        r['mix'] = mix
        o1 = mm(f'mix_out{l}', [(mix.reshape(M, -1), wout, 0)]).reshape(B, Tt, D)
        r['o1'] = o1
        fg = W['norm_ffn_g'][l][None]
        xb, h2 = tile_fwd(f'nm_ffn_fwd{l}', f_nm, (B, nt), [tok(xa), tok(o1), modv(l, 2), vec(fg), modv(l, 3), modv(l, 4)],
                          [tok_out(D), tok_out(D, BF16)])
        r['h2'] = h2
        h2m = h2.reshape(M, D)
        a = mm(f'ffn_gate{l}', [(h2m, W['ffn_w_gate'][l], 0)]).reshape(B, Tt, D_FF)
        up = mm(f'ffn_up{l}', [(h2m, W['ffn_w_up'][l], 0)]).reshape(B, Tt, D_FF)
        w9 = W['ffn_conv_w'][l].reshape(9, D_FF)
        cbias = W['ffn_conv_b'][l][None]
        r['a'], r['up'], r['w9'], r['cbias'] = a, up, w9, cbias
        act = tile_fwd(f'ffnconv_fwd{l}', f_ffnconv, (D_FF // 128, B),
                       [(a, (None, Tt, 128), lambda cb, bi: (bi, 0, cb)), (up, (None, Tt, 128), lambda cb, bi: (bi, 0, cb)),
                        (w9, (9, 128), lambda cb, bi: (0, cb)), (cbias, (1, 128), lambda cb, bi: (0, cb)), vec(conv_mk)],
                       [((B, Tt, D_FF), BF16, (None, Tt, 128), lambda cb, bi: (bi, 0, cb))])[0]
        r['act'] = act
        o2 = mm(f'ffn_down{l}', [(act.reshape(M, D_FF), W['ffn_w_down'][l], 0)]).reshape(B, Tt, D)
        xprev, oprev = xb, o2

    vmask = jnp.ones((nt, 1, D), F32).at[0].set(0.0)
    ones = jnp.ones((1, D), F32)
    fng = W['final_norm_g'][None]
    d_xp, d_o2, dg5, dfng, loss_vec = tile_bwd(
        'loss_head', f_final, (B, nt),
        [tok(xprev), tok(oprev), modv(DEPTH - 1, 5), vec(fng),
         (target, (None, tb, D), lambda b, j: (b, jnp.maximum(j - 1, 0), 0)), (vmask, (None, 1, D), lambda b, j: (j, 0, 0))],
        [[vec(ones)]],
        [(0,) + tok_out(D) + (None,), (1,) + tok_out(D, BF16) + (None,), (2,) + dmod_spec, (3,) + vec_acc((1, D))],
        prims=[(0,) + vec_acc((1, D))])
    loss = jnp.sum(loss_vec)
    add_grad('final_norm_g', None, dfng[0])
    dmods[(DEPTH - 1, 5)] = dg5

    for l in reversed(range(DEPTH)):
        r = R[l]
        j = l // 2
        d_o2m = d_o2.reshape(M, D)
        d_act = mm(f'ffn_down_dx{l}', [(d_o2m, W['ffn_w_down'][l], 0)], tb=True).reshape(B, Tt, D_FF)
        add_grad('ffn_w_down', l, mm(f'ffn_down_dw{l}', [(r['act'].reshape(M, D_FF), d_o2m, 0)], ta=True))
        d_a, d_up, dw9, dcb = tile_bwd(
            f'ffnconv_bwd{l}', f_ffnconv, (D_FF // 128, B),
            [(r['a'], (None, Tt, 128), lambda cb, bi: (bi, 0, cb)), (r['up'], (None, Tt, 128), lambda cb, bi: (bi, 0, cb)),
             (r['w9'], (9, 128), lambda cb, bi: (0, cb)), (r['cbias'], (1, 128), lambda cb, bi: (0, cb)), vec(conv_mk)],
            [[(d_act, (None, Tt, 128), lambda cb, bi: (bi, 0, cb))]],
            [(0, (B, Tt, D_FF), BF16, (None, Tt, 128), lambda cb, bi: (bi, 0, cb), None),
             (1, (B, Tt, D_FF), BF16, (None, Tt, 128), lambda cb, bi: (bi, 0, cb), None),
             (2, (9, D_FF), F32, (9, 128), lambda cb, bi: (0, cb), lambda cb, bi: bi == 0),
             (3, (1, D_FF), F32, (1, 128), lambda cb, bi: (0, cb), lambda cb, bi: bi == 0)])
        add_grad('ffn_conv_w', l, dw9.reshape(3, 3, D_FF))
        add_grad('ffn_conv_b', l, dcb[0])
        d_am, d_upm = d_a.reshape(M, D_FF), d_up.reshape(M, D_FF)
        h2m = r['h2'].reshape(M, D)
        d_h2 = mm(f'ffn_in_dx{l}', [(d_am, W['ffn_w_gate'][l], 0), (d_upm, W['ffn_w_up'][l], 0)], tb=True).reshape(B, Tt, D)
        add_grad('ffn_w_gate', l, mm(f'ffn_gate_dw{l}', [(h2m, d_am, 0)], ta=True))
        add_grad('ffn_w_up', l, mm(f'ffn_up_dw{l}', [(h2m, d_upm, 0)], ta=True))
        fg = W['norm_ffn_g'][l][None]
        d_xa, d_o1, dgate, dfg, dsh, dsc = tile_bwd(
            f'nm_ffn_bwd{l}', f_nm, (B, nt), [tok(r['xa']), tok(r['o1']), modv(l, 2), vec(fg), modv(l, 3), modv(l, 4)],
            [[tok(d_xp)], [tok(d_h2)]],
            [(0,) + tok_out(D) + (None,), (1,) + tok_out(D, BF16) + (None,), (2,) + dmod_spec, (3,) + vec_acc((1, D)),
             (4,) + dmod_spec, (5,) + dmod_spec])
        add_grad('norm_ffn_g', l, dfg[0])
        dmods[(l, 2)], dmods[(l, 3)], dmods[(l, 4)] = dgate, dsh, dsc
        d_o1m = d_o1.reshape(M, D)
        h1m = r['h1'].reshape(M, D)
        proj = r['proj']
        if l % 2 == 0:
            wout, win = W['ev_w_out'][j], W['ev_w_in_p'][j]
            d_mix = mm(f'mix_out_dx{l}', [(d_o1m, wout, 0)], tb=True).reshape(B, Tt, 2 * D)
            add_grad('ev_w_out', j, mm(f'mix_out_dw{l}', [(r['mix'].reshape(M, 2 * D), d_o1m, 0)], ta=True))
            d_y, d_xs_fin, d_z, d_gy, d_h, ddpad, dsng = tile_bwd(
                f'ssdfin_bwd{l}', f_ssdfin, (B, nt),
                [tok(r['y'][0]), tok(r['y'][1]), tok(r['xs']), tok(proj, D, 0), tok(proj, D, 1), tok(r['h'][0]), tok(r['h'][1]),
                 vec(r['dpad']), vec(r['sng'])],
                [[tok(d_mix)]],
                [(0,) + tok_out(D) + (None,), (2,) + tok_out(D) + (None,), (3,) + tok_out(D) + (None,), (4,) + tok_out(D) + (None,),
                 (5,) + tok_out(D) + (None,), (7,) + vec_acc((8, 128)), (8,) + vec_acc((1, D))])
            add_grad('ssd_d', j, ddpad[0, :SSD_HEADS])
            add_grad('ssd_norm_g', j, dsng[0])
            d_xs_parts, d_bc_parts, d_dt_parts, d_u_parts = [d_xs_fin], [], [], []
            dbias_t, dalog_t = [], []
            dh4 = d_h.reshape(B, Tt, 8, 128)
            for d in range(2):
                ph0 = phys_chunk(nc, ncc, d == 1)

                def ph(s, ph0=ph0):
                    return ph0(nc - 1 - s)
                dxs_d, dbc_d, ddt_d, dbias, dalog = scan_bwd(
                    f'ssd_bwd{l}_{d}', functools.partial(f_ssd, d=d, reverse=(d == 1)), (B, nc),
                    [(r['xs'], (None, SSD_CHUNK, D), lambda b, s, ph=ph: (b, ph(s), 0)),
                     (r['bc'], (None, SSD_CHUNK, 512), lambda b, s, ph=ph: (b, ph(s), 0)),
                     (proj, (None, SSD_CHUNK, 128), lambda b, s, ph=ph: (b, ph(s), 36)),
                     vec(r['bias']), vec(r['alog'])],
                    (r['st'][d], (None, None, 8, 128, 128), lambda b, s: (b, nc - 1 - s, 0, 0, 0)),
                    (d_y, (None, SSD_CHUNK, D), lambda b, s, ph=ph: (b, ph(s), 0)),
                    [(0, (B, Tt, D), F32, (None, SSD_CHUNK, D), lambda b, s, ph=ph: (b, ph(s), 0), None),
                     (1, (B, Tt, 512), F32, (None, SSD_CHUNK, 512), lambda b, s, ph=ph: (b, ph(s), 0), None),
                     (2, (B, Tt, 128), F32, (None, SSD_CHUNK, 128), lambda b, s, ph=ph: (b, ph(s), 0), None),
                     (3,) + vec_acc((1, 128)), (4,) + vec_acc((1, 128))],
                    (8, 128, 128), lambda b, s: s == 0)
                d_xs_parts.append(dxs_d)
                d_bc_parts.append(dbc_d)
                d_dt_parts.append(ddt_d)
                dbias_t.append(dbias)
                dalog_t.append(dalog)
                g4, ga4 = linrec_bwd(f'lru_bwd{l}_{d}', r['a4'][d], dh4, r['hp4'][d], 'Fb' if d == 0 else 'Rb')
                du_g, dwa, dba, dwi, dbi, dlam = tile_bwd(
                    f'gates_bwd{l}_{d}', f_gates, (B, nt),
                    [tok(r['u']), vec(W['lru_w_a'][j, d]), vec(W['lru_b_a'][j, d][None]), vec(W['lru_w_i'][j, d]),
                     vec(W['lru_b_i'][j, d][None]), vec(W['lru_lam'][j, d][None])],
                    [[tok(ga4.reshape(B, Tt, D))], [tok(g4.reshape(B, Tt, D))]],
                    [(0,) + tok_out(D) + (None,), (1,) + vec_acc((8, 128, 128)), (2,) + vec_acc((1, D)), (3,) + vec_acc((8, 128, 128)),
                     (4,) + vec_acc((1, D)), (5,) + vec_acc((1, D))])
                d_u_parts.append(du_g)
                add_grad('lru_w_a', (j, d), dwa)
                add_grad('lru_b_a', (j, d), dba[0])
                add_grad('lru_w_i', (j, d), dwi)
                add_grad('lru_b_i', (j, d), dbi[0])
                add_grad('lru_lam', (j, d), dlam[0])
            add_grad('ssd_dt_bias', j, (dbias_t[0] + dbias_t[1])[0, :32].reshape(2, SSD_HEADS))
            add_grad('ssd_a_log', j, (dalog_t[0] + dalog_t[1])[0, :32].reshape(2, SSD_HEADS))
            scw, scb = W['ssd_conv_w'][j], W['ssd_conv_b'][j][None]
            lcw, lcb = W['lru_conv_w'][j], W['lru_conv_b'][j][None]

            def conv_bwd(name, colblk0, w, b, wblk0, ncols, act, parts):
                return tile_bwd(
                    name, functools.partial(f_conv1d, lc=Lc, act=act), (ncols // 256, B),
                    [(proj, (None, Tt, 256), lambda cb, bi: (bi, 0, colblk0 + cb)),
                     (w, (4, 256), lambda cb, bi: (0, wblk0 + cb)), (b, (1, 256), lambda cb, bi: (0, wblk0 + cb))],
                    [[(p, (None, Tt, 256), lambda cb, bi: (bi, 0, cb)) for p in parts]],
                    [(0, (B, Tt, ncols), F32, (None, Tt, 256), lambda cb, bi: (bi, 0, cb), None),
                     (1, (4, ncols), F32, (4, 256), lambda cb, bi: (0, cb), lambda cb, bi: bi == 0),
                     (2, (1, ncols), F32, (1, 256), lambda cb, bi: (0, cb), lambda cb, bi: bi == 0)])
            d_xs_raw, dw_xs, db_xs = conv_bwd(f'conv_xs_bwd{l}', 12, scw, scb, 0, 1024, True, d_xs_parts)
            d_bc_raw, dw_bc, db_bc = conv_bwd(f'conv_bc_bwd{l}', 16, scw, scb, 4, 512, True, d_bc_parts)
            d_u_raw, dw_u, db_u = conv_bwd(f'conv_u_bwd{l}', 8, lcw, lcb, 0, 1024, False, d_u_parts)
            add_grad('ssd_conv_w', j, jnp.concatenate([dw_xs, dw_bc], axis=1))
            add_grad('ssd_conv_b', j, jnp.concatenate([db_xs, db_bc], axis=1)[0])
            add_grad('lru_conv_w', j, dw_u)
            add_grad('lru_conv_b', j, db_u[0])
            def f_ev_dproj(z_, gy_, u_, xs_, bc_, t0, t1):
                pad = jnp.zeros((z_.shape[0], EV_PAD - 4736), F32)
                return jnp.concatenate([z_, gy_, u_, xs_, bc_, t0 + t1, pad], axis=1)
            dproj = tile_fwd(f'ev_dproj{l}', f_ev_dproj, (B, nt),
                             [tok(d_z), tok(d_gy), tok(d_u_raw), tok(d_xs_raw), tok(d_bc_raw), tok(d_dt_parts[0]), tok(d_dt_parts[1])],
                             [tok_out(EV_PAD, BF16)])[0].reshape(M, EV_PAD)
            d_h1 = mm(f'ev_proj_dx{l}', [(dproj, win, 0)], tb=True).reshape(B, Tt, D)
            dwp = mm(f'ev_proj_dw{l}', [(h1m, dproj, 0)], ta=True)
            add_grad('ev_w_in', j, jnp.concatenate([dwp[:, 0:1024], dwp[:, 3072:4640], dwp[:, 1024:3072]], axis=1))
        else:
            wout, win = W['od_w_out'][j], W['od_w_in'][j]
            d_mix = mm(f'mix_out_dx{l}', [(d_o1m, wout, 0)], tb=True).reshape(B, Tt, D)
            add_grad('od_w_out', j, mm(f'mix_out_dw{l}', [(r['mix'].reshape(M, D), d_o1m, 0)], ta=True))
            hn, sd, gw, gb = r['fin_par']
            d_o, d_g, d_yv, d_u_fin, dhn, dsd, dgw, dgb = tile_bwd(
                f'oddfin_bwd{l}', f_oddfin, (B, nt),
                [tok(r['o'][0]), tok(r['o'][1]), tok(proj, HG_W, 4), tok(r['yd'][0]), tok(r['yd'][1]), tok(proj, S5_W, 15),
                 vec(hn), vec(sd), vec(gw), vec(gb)],
                [[tok(d_mix)]],
                [(0,) + tok_out(HG_W) + (None,), (2,) + tok_out(HG_W) + (None,), (3,) + tok_out(S5_W) + (None,),
                 (5,) + tok_out(S5_W) + (None,), (6,) + vec_acc((8, 128)), (7,) + vec_acc((1, S5_W)), (8,) + vec_acc((S5_W, S5_W)),
                 (9,) + vec_acc((1, S5_W))])
            add_grad('hg_norm_g', j, dhn[:HG_HEADS])
            add_grad('s5_d', j, dsd[0])
            add_grad('s5_glu_w', j, dgw)
            add_grad('s5_glu_b', j, dgb[0])
            lbv = lb1 if l == 1 else lb3
            ns = nc
            dq, df, dv, du_s5 = [], [], [], []
            d_ym = d_yv.reshape(M, S5_W)
            dbt_re, dbt_im = [], []
            for d in range(2):
                ph0 = phys_chunk(ns, ncc, d == 1)

                def ph(s, ph0=ph0):
                    return ph0(ns - 1 - s)
                dq_d, df_d, dv_d, dlb_d = scan_bwd(
                    f'hgrn_bwd{l}_{d}', functools.partial(f_hgrn_group, reverse=(d == 1)), (HG_GROUPS, B, ns),
                    [(proj, (None, 128, HG_GW), lambda h, b, s, ph=ph: (b, ph(s), h)),
                     (proj, (None, 128, HG_GW), lambda h, b, s, ph=ph, d=d: (b, ph(s), (1 + d) * HG_GROUPS + h)),
                     (proj, (None, 128, HG_GW), lambda h, b, s, ph=ph: (b, ph(s), 3 * HG_GROUPS + h)),
                     (lbv, (1, HG_GW), lambda h, b, s: (0, h))],
                    (r['zst'][d], (None, None, None, HG_GW, 128), lambda h, b, s: (b, h, ns - 1 - s, 0, 0)),
                    (d_o, (None, 128, HG_GW), lambda h, b, s, ph=ph: (b, ph(s), h)),
                    [(0, (B, Tt, HG_W), F32, (None, 128, HG_GW), lambda h, b, s, ph=ph: (b, ph(s), h), None),
                     (1, (B, Tt, HG_W), F32, (None, 128, HG_GW), lambda h, b, s, ph=ph: (b, ph(s), h), None),
                     (2, (B, Tt, HG_W), F32, (None, 128, HG_GW), lambda h, b, s, ph=ph: (b, ph(s), h), None),
                     (3, (1, HG_W), F32, (1, HG_GW), lambda h, b, s: (0, h), lambda h, b, s: (b == 0) & (s == 0))],
                    (HG_GW, 128), lambda h, b, s: s == 0)
                dq.append(dq_d)
                df.append(df_d)
                dv.append(dv_d)
                dlb[l].append(dlb_d)
                d_hcat = mm(f's5_out_dx{l}_{d}', [(d_ym, r['ccat'][d], 0)], tb=True)
                dccat = mm(f's5_out_dw{l}_{d}', [(r['hcat'][d], d_ym, 0)], ta=True)
                add_grad('s5_c_re', (j, d), _unblockdiag_t(dccat[:D]))
                add_grad('s5_c_im', (j, d), -_unblockdiag_t(dccat[D:]))
                g5, dcoef = clinrec_bwd(f's5_bwd{l}_{d}', d_hcat.reshape(B, Tt, 2, 8, 128), r['hp5'][d], r['coef'][d],
                                        'Fb' if d == 0 else 'Rb')
                gcat = g5.reshape(M, 2 * D)
                dbcat = mm(f's5_in_dw{l}_{d}', [(r['u2'], gcat, 0)], ta=True)
                du_s5.append(mm(f's5_in_dx{l}_{d}', [(gcat, r['bcat'][d], 0)], tb=True))
                cts5 = [dcoef[0].reshape(16, 64), dcoef[1].reshape(16, 64), _unblockdiag(dbcat[:, :D]), _unblockdiag(dbcat[:, D:])]
                dlre, dlim, dlst, dbtr, dbti = tile_bwd(
                    f's5p_bwd{l}_{d}', f_s5p, (1,), [vec(t) for t in r['s5in'][d]], [[vec(t)] for t in cts5],
                    [(i, t.shape, F32, t.shape, (lambda *ids, n=t.ndim: (0,) * n), None) for i, t in enumerate(r['s5in'][d])])
                add_grad('s5_lam_re', (j, d), dlre)
                add_grad('s5_lam_im', (j, d), dlim)
                add_grad('s5_log_step', (j, d), dlst[:, 0])
                dbt_re.append(dbtr)
                dbt_im.append(dbti)
            add_grad('s5_b_re', j, jnp.swapaxes(dbt_re[0] + dbt_re[1], 1, 2))
            add_grad('s5_b_im', j, jnp.swapaxes(dbt_im[0] + dbt_im[1], 1, 2))
            def f_od_dproj(q0, q1, f0, f1, v0, v1, g_, u0, u1, u2):
                return jnp.concatenate([q0 + q1, f0, f1, v0 + v1, g_, u0 + u1 + u2], axis=1)
            parts = [dq[0], dq[1], df[0], df[1], dv[0], dv[1], d_g, d_u_fin, du_s5[0].reshape(B, Tt, S5_W),
                     du_s5[1].reshape(B, Tt, S5_W)]
            dproj = tile_fwd(f'od_dproj{l}', f_od_dproj, (B, nt), [tok(t) for t in parts],
                             [tok_out(4096, BF16)])[0].reshape(M, 4096)
            d_h1 = mm(f'od_proj_dx{l}', [(dproj, win, 0)], tb=True).reshape(B, Tt, D)
            add_grad('od_w_in', j, mm(f'od_proj_dw{l}', [(h1m, dproj, 0)], ta=True))
        ng = W['norm_mix_g'][l][None]
        if l == 0:
            d_x0, dng, dsh, dsc = tile_bwd(
                'nm0_bwd', lambda xv, g, sh, sc: (xv, f_nm0(xv, g, sh, sc)), (B, nt),
                [tok(r['xin']), vec(ng), modv(l, 0), modv(l, 1)], [[tok(d_xa)], [tok(d_h1)]],
                [(0,) + tok_out(D) + (None,), (1,) + vec_acc((1, D)), (2,) + dmod_spec, (3,) + dmod_spec])
        else:
            d_xp, d_o2, dgate, dng, dsh, dsc = tile_bwd(
                f'nm_mix_bwd{l}', f_nm, (B, nt),
                [tok(r['xin']), tok(r['oin']), modv(l - 1, 5), vec(ng), modv(l, 0), modv(l, 1)],
                [[tok(d_xa)], [tok(d_h1)]],
                [(0,) + tok_out(D) + (None,), (1,) + tok_out(D, BF16) + (None,), (2,) + dmod_spec, (3,) + vec_acc((1, D)),
                 (4,) + dmod_spec, (5,) + dmod_spec])
            dmods[(l - 1, 5)] = dgate
        add_grad('norm_mix_g', l, dng[0])
        dmods[(l, 0)], dmods[(l, 1)] = dsh, dsc

    grad_x = d_x0[:, Lc:, :]

    (dlogits,) = tile_bwd('lb_bwd', f_lb, (1,), [vec(W['hg_lb_logits'])],
                          [[vec(t) for t in dlb[1]], [vec(t) for t in dlb[3]]],
                          [(0, (DEPTH, HG_W), F32, (DEPTH, HG_W), lambda i: (0, 0), None)])
    add_grad('hg_lb_logits', None, dlogits)

    dm = jnp.stack([jnp.stack([dmods[(l, w)] for w in range(N_MOD)]) for l in range(DEPTH)])
    dlat = jnp.transpose(dm[:, :, :, 1, 0, :], (0, 2, 1, 3)).reshape(DEPTH, B, N_MOD * D)
    dctx = jnp.transpose(dm[:, :, :, 0, 0, :], (0, 2, 1, 3)).reshape(DEPTH, B, N_MOD * D)
    dlat = jnp.zeros((DEPTH, 8, N_MOD * D), F32).at[:, :B].set(dlat)
    dctx = jnp.zeros((DEPTH, 8, N_MOD * D), F32).at[:, :B].set(dctx)

    def mod_bwd_body(cc_ref, w_ref, dl_ref, dc_ref, dw_ref, db_ref, dcc_ref):
        row = lax.broadcasted_iota(jnp.int32, (8, 1), 0)
        dall = dl_ref[...] + jnp.where(row == 4, jnp.sum(dc_ref[...], axis=0, keepdims=True), 0.0)
        s, vjp = jax.vjp(jax.nn.silu, cc_ref[...])
        db16 = dall.astype(BF16)
        dw_ref[...] = lax.dot_general(s.astype(BF16), db16, (((0,), (0,)), ((), ())), preferred_element_type=F32)
        db_ref[...] = jnp.sum(dall, axis=0, keepdims=True)
        ds = lax.dot_general(db16, w_ref[...], (((1,), (1,)), ((), ())), preferred_element_type=F32)
        (dcc,) = vjp(ds)
        first = (pl.program_id(0) == 0) & (pl.program_id(1) == 0)

        @pl.when(first)
        def _():
            dcc_ref[...] = dcc

        @pl.when(jnp.logical_not(first))
        def _():
            dcc_ref[...] += dcc

    dwmod, dbmod, dcc = pl.pallas_call(
        mod_bwd_body, grid=(DEPTH, nmc),
        in_specs=[pl.BlockSpec((8, D), lambda l, n: (0, 0)), pl.BlockSpec((None, D, 1536), lambda l, n: (l, 0, n)),
                  pl.BlockSpec((None, 8, 1536), lambda l, n: (l, 0, n)), pl.BlockSpec((None, 8, 1536), lambda l, n: (l, 0, n))],
        out_specs=[pl.BlockSpec((None, D, 1536), lambda l, n: (l, 0, n)), pl.BlockSpec((None, 1, 1536), lambda l, n: (l, 0, n)),
                   pl.BlockSpec((8, D), lambda l, n: (0, 0))],
        out_shape=[jax.ShapeDtypeStruct((DEPTH, D, N_MOD * D), F32), jax.ShapeDtypeStruct((DEPTH, 1, N_MOD * D), F32),
                   jax.ShapeDtypeStruct((8, D), F32)],
        name='mod_bwd', compiler_params=_cparams(("arbitrary", "arbitrary")),
    )(cc, W['w_mod'], dlat, dctx)
    add_grad('w_mod', None, dwmod)
    add_grad('b_mod', None, dbmod[:, 0])
    add_grad('c_ctx', None, dcc[4])
    return loss, grad_x, G


def assemble_grads(G, like):
    out = {}
    for name, parts in G.items():
        shape = like[name].shape
        if None in parts:
            g = parts[None]
        elif isinstance(next(iter(parts)), tuple):
            g = jnp.stack([jnp.stack([parts[(j, d)] for d in range(2)]) for j in range(shape[0])])
        else:
            g = jnp.stack([parts[i] for i in range(shape[0])])
        out[name] = g.reshape(shape)
    return out


XY_RELS = ((1, 0, 0), (0, 1, 0), (1, 1, 0))
ALL_RELS = tuple((dx, dy, dc) for dx in (0, 1) for dy in (0, 1) for dc in (0, 1))[1:]


def exchange(name, src, out_shape, sends):
    n = len(sends)

    def body(src_ref, out_ref, send_sems, recv_sems):
        me = (lax.axis_index("x"), lax.axis_index("y"), lax.axis_index("c"))

        def pick(ref, sel, tgt):
            return ref if sel is None else ref.at[sel(me, tgt)]

        copies = []
        for k, (rel, ssel, dsel) in enumerate(sends):
            tgt = tuple(1 - m if f else m for m, f in zip(me, rel))
            cp = pltpu.make_async_remote_copy(
                src_ref=pick(src_ref, ssel, tgt), dst_ref=pick(out_ref, dsel, tgt),
                send_sem=send_sems.at[k], recv_sem=recv_sems.at[k], device_id=tgt, device_id_type=MESH)
            cp.start()
            copies.append(cp)
        for cp in copies:
            cp.wait()

    return pl.pallas_call(
        body, out_shape=jax.ShapeDtypeStruct(out_shape, src.dtype),
        in_specs=[pl.BlockSpec(memory_space=pl.ANY)], out_specs=pl.BlockSpec(memory_space=pl.ANY),
        scratch_shapes=[pltpu.SemaphoreType.DMA((n,)), pltpu.SemaphoreType.DMA((n,))],
        name=name,
    )(src)


def _xy_index(dev):
    return 2 * dev[0] + dev[1]


def _my_xy():
    return 2 * lax.axis_index("x") + lax.axis_index("y")


def all_gather_xy(name, shard):
    got = exchange(name, shard, (4,) + shard.shape, [(rel, None, lambda me, tgt: _xy_index(me)) for rel in XY_RELS])
    return lax.dynamic_update_index_in_dim(got, shard, _my_xy(), 0)


def reduce_scatter_xy(name, g4):
    got = exchange(name, g4, (3,) + g4.shape[1:],
                   [(rel, (lambda me, tgt: _xy_index(tgt)), (lambda me, tgt, k=k: k)) for k, rel in enumerate(XY_RELS)])
    return got, lax.dynamic_index_in_dim(g4, _my_xy(), 0, keepdims=False)


def sibling_swap(name, v):
    return exchange(name, v, v.shape, [((0, 0, 1), None, None)])


def all_gather_all(name, v):
    got = exchange(name, v, (8,) + v.shape, [(rel, None, lambda me, tgt: 4 * me[0] + 2 * me[1] + me[2]) for rel in ALL_RELS])
    return lax.dynamic_update_index_in_dim(got, v, 2 * _my_xy() + lax.axis_index("c"), 0)


def all_gather_xy_halves(name, shard):
    half = shard.shape[0] // 2

    def body(src_ref, out_ref, send_sems, recv_sems):
        x, y, c = lax.axis_index("x"), lax.axis_index("y"), lax.axis_index("c")
        mine = pl.ds(c * half, half)
        peers = [(1 - x, y), (x, 1 - y), (1 - x, 1 - y)]

        def copy(k, src, dst, to):
            return pltpu.make_async_remote_copy(src_ref=src, dst_ref=dst, send_sem=send_sems.at[k], recv_sem=recv_sems.at[k],
                                                device_id=to, device_id_type=MESH)

        first = [copy(k, src_ref.at[mine], out_ref.at[2 * x + y, mine], (px, py, c)) for k, (px, py) in enumerate(peers)]
        for cp in first:
            cp.start()
        passed = []
        for k, (px, py) in enumerate(peers):
            first[k].wait_recv()
            landed = out_ref.at[2 * px + py, mine]
            fw = copy(3 + k, landed, landed, (x, y, 1 - c))
            fw.start()
            passed.append(fw)
        for fw in passed:
            fw.wait_recv()
        for cp in first + passed:
            cp.wait_send()

    got = pl.pallas_call(
        body, out_shape=jax.ShapeDtypeStruct((4,) + shard.shape, shard.dtype),
        in_specs=[pl.BlockSpec(memory_space=pl.ANY)], out_specs=pl.BlockSpec(memory_space=pl.ANY),
        scratch_shapes=[pltpu.SemaphoreType.DMA((6,)), pltpu.SemaphoreType.DMA((6,))],
        name=name,
    )(shard)
    return lax.dynamic_update_index_in_dim(got, shard, _my_xy(), 0)


def sibling_split(name, g4):
    half = g4.shape[1] // 2
    got = exchange(name, g4, (4, half) + g4.shape[2:],
                   [((0, 0, 1), (lambda me, tgt: (slice(None), pl.ds(tgt[2] * half, half))), None)])
    return got, lax.dynamic_slice_in_dim(g4, lax.axis_index("c") * half, half, axis=1)


def sibling_join(name, q):
    half = q.shape[0]
    got = exchange(name, q, (2 * half,) + q.shape[1:], [((0, 0, 1), None, lambda me, tgt: pl.ds(me[2] * half, half))])
    return lax.dynamic_update_slice_in_dim(got, q, lax.axis_index("c") * half, axis=0)


def _rows_view(shape):
    cols = shape[-1] if len(shape) else 1
    rows = 1
    for s in shape[:-1]:
        rows *= s
    return rows, cols


def _row_block(rows, cols, n_arrays):
    budget = (24 * 1024 * 1024) // (8 * n_arrays * cols)
    if rows <= max(budget, 16):
        return rows
    br = (min(budget, rows) // 16) * 16
    while br > 16 and rows % br:
        br -= 16
    return br if rows % br == 0 else rows


def sum_slots(name, stacked, extra=(), out_dtype=F32):
    k = stacked.shape[0]
    rows, cols = _rows_view(stacked.shape[1:])
    br = _row_block(rows, cols, k + len(extra) + 1)

    def f(s, *more):
        parts = [s[i].astype(F32) for i in range(k)] + [m.astype(F32) for m in more]
        while len(parts) > 1:
            parts = [parts[i] + parts[i + 1] for i in range(0, len(parts) - 1, 2)] + ([parts[-1]] if len(parts) % 2 else [])
        return parts[0]

    out = tile_fwd(name, f, (rows // br,),
                   [(stacked.reshape(k, rows, cols), (k, br, cols), lambda i: (0, i, 0))]
                   + [(e.reshape(rows, cols), (br, cols), lambda i: (i, 0)) for e in extra],
                   [((rows, cols), out_dtype, (br, cols), lambda i: (i, 0))])[0]
    return out.reshape(stacked.shape[1:])


def adamw(name, w, m, v, gs):
    rows, cols = _rows_view(w.shape)
    br = _row_block(rows, cols, 7 + len(gs))
    spec = lambda a: (a.reshape(rows, cols), (br, cols), lambda i: (i, 0))
    outs = tile_fwd(name, f_adamw, (rows // br,), [spec(t) for t in (w, m, v) + tuple(gs)],
                    [((rows, cols), F32, (br, cols), lambda i: (i, 0))] * 4)
    return [o.reshape(w.shape) for o in outs]


IN_NAMES = ['x', 'c', 'ctx'] + W_NAMES + ['loss_target'] + ['m_' + n for n in W_NAMES] + ['v_' + n for n in W_NAMES]
SMALL_PAD = 128 * 1024


def kernel(x, c, ctx, c_ctx, w_mod, b_mod, norm_mix_g, norm_ffn_g, final_norm_g, ev_w_in, ev_w_out, ssd_conv_w, ssd_conv_b, ssd_dt_bias, ssd_a_log, ssd_d, ssd_norm_g, lru_conv_w, lru_conv_b, lru_w_a, lru_b_a, lru_w_i, lru_b_i, lru_lam, od_w_in, od_w_out, hg_lb_logits, hg_norm_g, s5_lam_re, s5_lam_im, s5_log_step, s5_b_re, s5_b_im, s5_c_re, s5_c_im, s5_d, s5_glu_w, s5_glu_b, ffn_w_gate, ffn_w_up, ffn_conv_w, ffn_conv_b, ffn_w_down, loss_target, m_c_ctx, m_w_mod, m_b_mod, m_norm_mix_g, m_norm_ffn_g, m_final_norm_g, m_ev_w_in, m_ev_w_out, m_ssd_conv_w, m_ssd_conv_b, m_ssd_dt_bias, m_ssd_a_log, m_ssd_d, m_ssd_norm_g, m_lru_conv_w, m_lru_conv_b, m_lru_w_a, m_lru_b_a, m_lru_w_i, m_lru_b_i, m_lru_lam, m_od_w_in, m_od_w_out, m_hg_lb_logits, m_hg_norm_g, m_s5_lam_re, m_s5_lam_im, m_s5_log_step, m_s5_b_re, m_s5_b_im, m_s5_c_re, m_s5_c_im, m_s5_d, m_s5_glu_w, m_s5_glu_b, m_ffn_w_gate, m_ffn_w_up, m_ffn_conv_w, m_ffn_conv_b, m_ffn_w_down, v_c_ctx, v_w_mod, v_b_mod, v_norm_mix_g, v_norm_ffn_g, v_final_norm_g, v_ev_w_in, v_ev_w_out, v_ssd_conv_w, v_ssd_conv_b, v_ssd_dt_bias, v_ssd_a_log, v_ssd_d, v_ssd_norm_g, v_lru_conv_w, v_lru_conv_b, v_lru_w_a, v_lru_b_a, v_lru_w_i, v_lru_b_i, v_lru_lam, v_od_w_in, v_od_w_out, v_hg_lb_logits, v_hg_norm_g, v_s5_lam_re, v_s5_lam_im, v_s5_log_step, v_s5_b_re, v_s5_b_im, v_s5_c_re, v_s5_c_im, v_s5_d, v_s5_glu_w, v_s5_glu_b, v_ffn_w_gate, v_ffn_w_up, v_ffn_conv_w, v_ffn_conv_b, v_ffn_w_down):
    a = dict(locals())
    W = {}
    for n in W_NAMES:
        w = a[n]
        if n in SHARD_AXIS:
            ax = SHARD_AXIS[n]
            if n in MATMUL_WEIGHTS:
                g4 = all_gather_xy_halves('ag_' + n, w.astype(BF16))
            else:
                g4 = all_gather_xy('ag_' + n, w)
            shape = list(w.shape)
            shape[ax] *= 4
            W[n] = jnp.moveaxis(g4, 0, ax).reshape(shape)
        else:
            W[n] = w
    e = W['ev_w_in']
    W['ev_w_in_p'] = jnp.concatenate(
        [e[:, :, 0:1024], e[:, :, 2592:3616], e[:, :, 3616:4640], e[:, :, 1024:2560], e[:, :, 2560:2592],
         jnp.zeros((e.shape[0], D, EV_PAD - 4640), e.dtype)], axis=2)

    loss_local, grad_x, G = local_step(a['x'], a['c'], a['ctx'], W['c_ctx'], a['loss_target'], W)
    grads = assemble_grads(G, W)
    loss = lax.psum(loss_local, ("x", "y", "c"))

    res = {}
    for n in W_NAMES:
        if n not in SHARD_AXIS:
            continue
        ax = SHARD_AXIS[n]
        w = a[n]
        gf = grads[n]
        split = gf.reshape(gf.shape[:ax] + (4, w.shape[ax]) + gf.shape[ax + 1:])
        g4 = jnp.moveaxis(split, ax, 0)
        if n in MATMUL_WEIGHTS:
            theirs, ours = sibling_split('rsc_' + n, g4)
            part = sum_slots('csum_' + n, theirs[None], (ours,), BF16)
            got, own = reduce_scatter_xy('rs_' + n, part)
            half = sum_slots('gsum_' + n, got, (own,))
            res[n] = adamw('adamw_' + n, w, a['m_' + n], a['v_' + n], (sibling_join('agc_' + n, half),))
        else:
            got, own = reduce_scatter_xy('rs_' + n, g4)
            mine = sum_slots('gsum_' + n, got, (own,))
            other = sibling_swap('sw_' + n, mine)
            res[n] = adamw('adamw_' + n, w, a['m_' + n], a['v_' + n], (mine, other))
    small = [n for n in W_NAMES if n not in SHARD_AXIS]
    flat = jnp.concatenate([grads[n].reshape(-1) for n in small])
    total = flat.shape[0]
    padded = -(-total // SMALL_PAD) * SMALL_PAD
    flat = jnp.concatenate([flat, jnp.zeros((padded - total,), F32)]).reshape(padded // 128, 128)
    pair = sum_slots('csum_small', flat[None], (sibling_swap('sw_small', flat),))
    summed = sum_slots('gsum_small', all_gather_xy('ag_small', pair)).reshape(-1)
    off = 0
    for n in small:
        size = math.prod(a[n].shape)
        g = summed[off:off + size].reshape(a[n].shape)
        off += size
        res[n] = adamw('adamw_' + n, a[n], a['m_' + n], a['v_' + n], (g,))
    outs = [loss, grad_x]
    for k in range(4):
        outs += [res[n][k] for n in W_NAMES]
    return tuple(outs)
```

```python
import functools
import math

import jax
import jax.numpy as jnp
from jax import lax
from jax.experimental import pallas as pl
from jax.experimental.pallas import tpu as pltpu

F32 = jnp.float32
BF16 = jnp.bfloat16
HI = lax.Precision.HIGHEST
MESH = pl.DeviceIdType.MESH

D = 1024
DEPTH = 4
N_MOD = 6
RMS_EPS = 1e-6
GRID_W = 64
SSD_HEADS = 16
SSD_CHUNK = 128
HG_W = 768
HG_HEADS = 6
HG_CHUNK = 16
HG_PER_STEP = 3
HG_GROUPS = HG_HEADS // HG_PER_STEP
HG_GW = 128 * HG_PER_STEP
S5_W = 256
D_FF = 2816
EV_PAD = 5120
LRU_C = 8.0
V7X_VMEM_LIMIT = 56 * 1024 * 1024
MM_VMEM_BUDGET = 36 * 1024 * 1024

ADAM_LR, ADAM_B1, ADAM_B2, ADAM_EPS, ADAM_WD, ADAM_STEP = 0.001, 0.9, 0.999, 1e-08, 0.01, 10

W_NAMES = ['c_ctx', 'w_mod', 'b_mod', 'norm_mix_g', 'norm_ffn_g', 'final_norm_g', 'ev_w_in', 'ev_w_out', 'ssd_conv_w',
           'ssd_conv_b', 'ssd_dt_bias', 'ssd_a_log', 'ssd_d', 'ssd_norm_g', 'lru_conv_w', 'lru_conv_b', 'lru_w_a', 'lru_b_a',
           'lru_w_i', 'lru_b_i', 'lru_lam', 'od_w_in', 'od_w_out', 'hg_lb_logits', 'hg_norm_g', 's5_lam_re', 's5_lam_im',
           's5_log_step', 's5_b_re', 's5_b_im', 's5_c_re', 's5_c_im', 's5_d', 's5_glu_w', 's5_glu_b', 'ffn_w_gate', 'ffn_w_up',
           'ffn_conv_w', 'ffn_conv_b', 'ffn_w_down']
SHARD_AXIS = {'w_mod': 2, 'ev_w_in': 2, 'ev_w_out': 1, 'ssd_conv_w': 2, 'lru_conv_w': 2, 'lru_b_a': 2, 'lru_b_i': 2,
              'lru_lam': 2, 'od_w_in': 2, 'od_w_out': 1, 's5_d': 1, 's5_glu_w': 1, 's5_glu_b': 1, 'ffn_w_gate': 2,
              'ffn_w_up': 2, 'ffn_conv_w': 3, 'ffn_w_down': 1}
MATMUL_WEIGHTS = ('w_mod', 'ev_w_in', 'ev_w_out', 'od_w_in', 'od_w_out', 'ffn_w_gate', 'ffn_w_up', 'ffn_w_down')


def _cparams(sem=None):
    return pltpu.CompilerParams(vmem_limit_bytes=V7X_VMEM_LIMIT, dimension_semantics=sem)


def _pick(n, cands):
    for c in cands:
        if n % c == 0:
            return c
    return n


def tile_fwd(name, f, grid, ins, outs):
    n_in = len(ins)

    def body(*refs):
        res = f(*[r[...] for r in refs[:n_in]])
        if not isinstance(res, (tuple, list)):
            res = (res,)
        for r, o in zip(res, refs[n_in:]):
            o[...] = r.astype(o.dtype)

    res = pl.pallas_call(
        body, grid=grid,
        in_specs=[pl.BlockSpec(b, m) for _, b, m in ins],
        out_specs=[pl.BlockSpec(b, m) for _, _, b, m in outs],
        out_shape=[jax.ShapeDtypeStruct(s, d) for s, d, _, _ in outs],
        name=name, compiler_params=_cparams(("arbitrary",) * len(grid)),
    )(*[a for a, _, _ in ins])
    return res


def tile_bwd(name, f, grid, ins, cts, grads, prims=()):
    n_in = len(ins)
    ct_flat = [p for c in cts for p in c]
    n_ct = len(ct_flat)
    didx = [g[0] for g in grads]

    def body(*refs):
        in_refs, ct_refs = refs[:n_in], refs[n_in:n_in + n_ct]
        g_refs = refs[n_in + n_ct:n_in + n_ct + len(grads)]
        p_refs = refs[n_in + n_ct + len(grads):]
        vals = [r[...] for r in in_refs]

        def fd(*dv):
            full = list(vals)
            for i, v in zip(didx, dv):
                full[i] = v
            res = f(*full)
            return tuple(res) if isinstance(res, (tuple, list)) else (res,)

        out, vjp = jax.vjp(fd, *[vals[i] for i in didx])
        ctv, k = [], 0
        for o, c in zip(out, cts):
            acc = None
            for _ in c:
                piece = ct_refs[k][...].astype(o.dtype)
                acc = piece if acc is None else acc + piece
                k += 1
            ctv.append(jnp.zeros_like(o) if acc is None else acc.reshape(o.shape))
        gs = vjp(tuple(ctv))
        ids = [pl.program_id(a) for a in range(len(grid))]

        def emit(ref, val, first):
            if first is None:
                ref[...] = val.astype(ref.dtype)
            else:
                is_first = first(*ids)

                @pl.when(is_first)
                def _():
                    ref[...] = val.astype(ref.dtype)

                @pl.when(jnp.logical_not(is_first))
                def _():
                    ref[...] += val.astype(ref.dtype)

        for g, spec, ref in zip(gs, grads, g_refs):
            emit(ref, g, spec[5])
        for spec, ref in zip(prims, p_refs):
            emit(ref, out[spec[0]], spec[5])

    specs = list(grads) + list(prims)
    res = pl.pallas_call(
        body, grid=grid,
        in_specs=[pl.BlockSpec(b, m) for _, b, m in list(ins) + ct_flat],
        out_specs=[pl.BlockSpec(s[3], s[4]) for s in specs],
        out_shape=[jax.ShapeDtypeStruct(s[1], s[2]) for s in specs],
        name=name, compiler_params=_cparams(("arbitrary",) * len(grid)),
    )(*[a for a, _, _ in list(ins) + ct_flat])
    return res


def mm(name, pairs, ta=False, tb=False, out_dtype=F32):
    a0, b0, _ = pairs[0]
    m = a0.shape[1] if ta else a0.shape[0]
    n = b0.shape[0] if tb else b0.shape[1]
    cands = (1024, 1408, 768, 512, 256, 128)
    tks, nks = [], []
    for a, b, _ in pairs:
        k = a.shape[0] if ta else a.shape[1]
        tk = _pick(k, cands)
        tks.append(tk)
        nks.append(k // tk)

    def vmem_bytes(tm, tn):
        tiles = sum(2 * tk * (tm * a.dtype.itemsize + tn * b.dtype.itemsize) for (a, b, _), tk in zip(pairs, tks))
        return tiles + tm * tn * (4 + 2 * jnp.dtype(out_dtype).itemsize)

    tm_c = [c_ for c_ in cands if m % c_ == 0] or [m]
    tn_c = [c_ for c_ in cands if n % c_ == 0] or [n]
    tm, tn = tm_c[0], tn_c[0]
    while vmem_bytes(tm, tn) > MM_VMEM_BUDGET and (len(tm_c) > 1 or len(tn_c) > 1):
        if len(tm_c) > 1 and (tm >= tn or len(tn_c) == 1):
            tm_c = tm_c[1:]
        else:
            tn_c = tn_c[1:]
        tm, tn = tm_c[0], tn_c[0]
    starts = [sum(nks[:p]) for p in range(len(pairs))]
    nk = sum(nks)
    np_ = len(pairs)

    def body(*refs):
        o_ref, acc = refs[2 * np_], refs[2 * np_ + 1]
        kk = pl.program_id(2)

        @pl.when(kk == 0)
        def _():
            acc[...] = jnp.zeros_like(acc)

        for p in range(np_):
            def add(p=p):
                a = refs[2 * p][...].astype(BF16)
                b = refs[2 * p + 1][...].astype(BF16)
                dn = (((0 if ta else 1,), (1 if tb else 0,)), ((), ()))
                acc[...] += lax.dot_general(a, b, dn, preferred_element_type=F32)
            if np_ == 1:
                add()
            else:
                pl.when((kk >= starts[p]) & (kk < starts[p] + nks[p]))(add)

        @pl.when(kk == nk - 1)
        def _():
            o_ref[...] = acc[...].astype(o_ref.dtype)

    in_specs, args = [], []
    for p, (a, b, off) in enumerate(pairs):
        tk, s0, nkp = tks[p], starts[p], nks[p]
        assert off % tk == 0
        boff = off // tk

        def kloc(k, s0=s0, nkp=nkp):
            return jnp.clip(k - s0, 0, nkp - 1)
        if ta:
            in_specs.append(pl.BlockSpec((tk, tm), lambda i, j, k, kloc=kloc: (kloc(k), i)))
        else:
            in_specs.append(pl.BlockSpec((tm, tk), lambda i, j, k, kloc=kloc: (i, kloc(k))))
        if tb:
            in_specs.append(pl.BlockSpec((tn, tk), lambda i, j, k, kloc=kloc, boff=boff: (j, boff + kloc(k))))
        else:
            in_specs.append(pl.BlockSpec((tk, tn), lambda i, j, k, kloc=kloc, boff=boff: (boff + kloc(k), j)))
        args += [a, b]
    return pl.pallas_call(
        body, grid=(m // tm, n // tn, nk), in_specs=in_specs,
        out_specs=pl.BlockSpec((tm, tn), lambda i, j, k: (i, j)),
        out_shape=jax.ShapeDtypeStruct((m, n), out_dtype),
        scratch_shapes=[pltpu.VMEM((tm, tn), F32)],
        name=name, compiler_params=_cparams(("arbitrary", "arbitrary", "arbitrary")),
    )(*args)


def _rms(x, g):
    return x * lax.rsqrt(jnp.mean(x * x, axis=-1, keepdims=True) + RMS_EPS) * g


def f_nm0(x, g, sh, sc):
    return _rms(x, g) * (1.0 + sc) + sh


def f_nm(xp, o, gate, g, sh, sc):
    x = xp + gate * o
    return x, _rms(x, g) * (1.0 + sc) + sh


def f_final(xp, o, gate, g, tgt, valid):
    x = xp + gate * o
    e = (_rms(x, g) - tgt) * valid
    return jnp.sum(e * e, axis=0, keepdims=True) * (0.5 / D)


@functools.partial(jax.custom_vjp, nondiff_argnums=(1,))
def _sroll(x, s):
    return pltpu.roll(x, s, 0)


def _sroll_fwd(x, s):
    return pltpu.roll(x, s, 0), None


def _sroll_bwd(s, _, g):
    return (pltpu.roll(g, (g.shape[0] - s) % g.shape[0], 0),)


_sroll.defvjp(_sroll_fwd, _sroll_bwd)


def _shifted(x, o):
    n = x.shape[0]
    return x if o == 0 else _sroll(x, (n - o) % n)


def f_conv1d(x, w, b, *, lc, act):
    n = x.shape[0]
    pos = lax.broadcasted_iota(jnp.int32, (n, 1), 0)
    lo = jnp.where(pos < lc, 0, lc)
    hi = jnp.where(pos < lc, lc, n)
    y = x * w[1:2] + b
    for k, o in ((0, -1), (2, 1), (3, 2)):
        src = pos + o
        valid = (src >= lo) & (src < hi)
        y = y + jnp.where(valid, _shifted(x, o), 0.0) * w[k:k + 1]
    return jax.nn.silu(y) if act else y


def ffnconv_masks(n, lc):
    pos = lax.broadcasted_iota(jnp.int32, (n, 128), 0)
    is_ctx = pos < lc
    tl = pos - lc
    r = tl // GRID_W
    cc = tl - r * GRID_W
    rows = (n - lc) // GRID_W
    left = jnp.where(is_ctx, pos >= 1, cc >= 1)
    right = jnp.where(is_ctx, pos < lc - 1, cc < GRID_W - 1)
    above = jnp.logical_not(is_ctx) & (r >= 1)
    below = jnp.logical_not(is_ctx) & (r < rows - 1)
    return jnp.stack([left, right, above, below]).astype(F32)


def f_ffnconv(a, up, w, b, mk):
    cols = (mk[0] * _shifted(a, -1), a, mk[1] * _shifted(a, 1))
    y = b
    for dr in (-1, 0, 1):
        k = 3 * (dr + 1)
        inner = cols[0] * w[k:k + 1] + cols[1] * w[k + 1:k + 2] + cols[2] * w[k + 2:k + 3]
        y = y + (inner if dr == 0 else mk[2 + (dr > 0)] * _shifted(inner, GRID_W * dr))
    return jax.nn.silu(y) * up


def f_ssd(xs, bc, dtraw, bias, alog, st, *, d, reverse):
    L = xs.shape[0]
    dtv = jax.nn.softplus(dtraw + bias)
    la = dtv * (-jnp.exp(alog))
    ri = lax.broadcasted_iota(jnp.int32, (L, L), 0)
    ci = lax.broadcasted_iota(jnp.int32, (L, L), 1)
    mask = (ci >= ri) if reverse else (ci <= ri)
    cum = jnp.dot(mask.astype(F32), la, precision=HI, preferred_element_type=F32)
    cum_t = cum.T
    tot = cum[0:1] if reverse else cum[L - 1:L]
    lo = lax.broadcasted_iota(jnp.int32, (1, 128), 1) < 64
    rlo = lax.broadcasted_iota(jnp.int32, (128, 1), 0) < 64
    ys, new = [], []
    for g in range(2):
        bg = bc[:, g * 128:(g + 1) * 128].astype(BF16)
        cg = bc[:, 256 + g * 128:256 + (g + 1) * 128].astype(BF16)
        cb = lax.dot_general(cg, bg, (((1,), (1,)), ((), ())), preferred_element_type=F32)
        sg = st[4 * g:4 * g + 4].reshape(4 * 128, 128)
        ch_all = lax.dot_general(cg, sg.astype(BF16), (((1,), (1,)), ((), ())), preferred_element_type=F32)
        xes, dcols = [], []
        for jj in range(4):
            j = 4 * g + jj
            x = xs[:, j * 128:(j + 1) * 128]
            k1 = 16 * d + 2 * j
            k2 = k1 + 1
            c1, c2 = cum[:, k1:k1 + 1], cum[:, k2:k2 + 1]
            m1 = cb * jnp.exp(jnp.where(mask, c1 - cum_t[k1:k1 + 1, :], -1e30))
            m2 = cb * jnp.exp(jnp.where(mask, c2 - cum_t[k2:k2 + 1, :], -1e30))
            xdt = x * jnp.where(lo, dtv[:, k1:k1 + 1], dtv[:, k2:k2 + 1])
            mcat = jnp.concatenate([m1, m2], axis=1).astype(BF16)
            xcat = jnp.concatenate([jnp.where(lo, xdt, 0.0), jnp.where(lo, 0.0, xdt)], axis=0).astype(BF16)
            y = jnp.dot(mcat, xcat, preferred_element_type=F32)
            y = y + ch_all[:, jj * 128:(jj + 1) * 128] * jnp.where(lo, jnp.exp(c1), jnp.exp(c2))
            t1, t2 = tot[:, k1:k1 + 1], tot[:, k2:k2 + 1]
            xes.append((xdt * jnp.where(lo, jnp.exp(t1 - c1), jnp.exp(t2 - c2))).astype(BF16))
            dcols.append(jnp.where(rlo, jnp.exp(t1), jnp.exp(t2)))
            ys.append(y)
        upd = lax.dot_general(jnp.concatenate(xes, axis=1), bg, (((0,), (0,)), ((), ())), preferred_element_type=F32)
        new.append((sg * jnp.concatenate(dcols, axis=0) + upd).reshape(4, 128, 128))
    return jnp.concatenate(ys, axis=1), jnp.concatenate(new, axis=0)


def f_hgrn(q_raw, f_raw, v, lb, zt, *, reverse):
    n = q_raw.shape[0]
    c = HG_CHUNK
    qa = jax.nn.silu(q_raw)
    logf = jnp.log(lb + (1.0 - lb) * jax.nn.sigmoid(f_raw))
    kk = (1.0 - lb) * jax.nn.sigmoid(-f_raw)
    ri = lax.broadcasted_iota(jnp.int32, (n, n), 0)
    ci = lax.broadcasted_iota(jnp.int32, (n, n), 1)
    tmat = ((ri // c == ci // c) & ((ci >= ri) if reverse else (ci <= ri))).astype(F32)
    cum_all = jnp.dot(tmat, logf, precision=HI, preferred_element_type=F32)
    r3 = lax.broadcasted_iota(jnp.int32, (c, c, 128), 0)
    c3 = lax.broadcasted_iota(jnp.int32, (c, c, 128), 1)
    mask3 = (c3 >= r3) if reverse else (c3 <= r3)
    nch = n // c
    outs = [None] * nch
    for chn in (reversed(range(nch)) if reverse else range(nch)):
        sl = slice(chn * c, (chn + 1) * c)
        q, k, vv, cum = qa[sl], kk[sl], v[sl], cum_all[sl]
        dec = jnp.exp(jnp.where(mask3, cum[:, None, :] - cum[None, :, :], -1e30))
        att = jnp.sum(q[:, None, :] * dec * k[None, :, :], axis=-1, keepdims=True)
        y = jnp.sum(att * vv[None, :, :], axis=1)
        y = y + lax.dot_general((q * jnp.exp(cum)).astype(BF16), zt.astype(BF16), (((1,), (1,)), ((), ())),
                                preferred_element_type=F32)
        tot = cum[0:1] if reverse else cum[c - 1:c]
        kd = (k * jnp.exp(tot - cum)).astype(BF16)
        zt = zt * jnp.exp(tot) + lax.dot_general(vv.astype(BF16), kd, (((0,), (0,)), ((), ())), preferred_element_type=F32)
        outs[chn] = y
    return jnp.concatenate(outs, axis=0), zt


def f_hgrn_group(q_raw, f_raw, v, lb, zt, *, reverse):
    ys, zs = [], []
    for h in range(HG_PER_STEP):
        sl = slice(h * 128, (h + 1) * 128)
        y, z = f_hgrn(q_raw[:, sl], f_raw[:, sl], v[:, sl], lb[:, sl], zt[sl], reverse=reverse)
        ys.append(y)
        zs.append(z)
    return jnp.concatenate(ys, axis=1), jnp.concatenate(zs, axis=0)


def _expm1(x):
    poly = x * (1.0 + x * (0.5 + x * (1.0 / 6 + x * (1.0 / 24 + x * (1.0 / 120 + x * (1.0 / 720))))))
    return jnp.where(jnp.abs(x) < 0.3, poly, jnp.exp(x) - 1.0)


def f_gates(u, wa, ba, wi, bi, lam):
    rs, is_ = [], []
    for nb in range(8):
        un = u[:, nb * 128:(nb + 1) * 128].astype(BF16)
        rs.append(jnp.dot(un, wa[nb].astype(BF16), preferred_element_type=F32))
        is_.append(jnp.dot(un, wi[nb].astype(BF16), preferred_element_type=F32))
    r = jax.nn.sigmoid(jnp.concatenate(rs, axis=1) + ba)
    i = jax.nn.sigmoid(jnp.concatenate(is_, axis=1) + bi)
    log_a = -LRU_C * jax.nn.softplus(-lam) * r
    return jnp.exp(log_a), jnp.sqrt(-_expm1(2.0 * log_a)) * (i * u)


def f_ssdfin(y0, y1, xs, z, gy, h0, h1, dpad, ng):
    kk = lax.broadcasted_iota(jnp.int32, (128, D), 0)
    ch = lax.broadcasted_iota(jnp.int32, (128, D), 1)
    expand = (ch // 64 == kk).astype(F32)
    dvec = jnp.dot(dpad, expand, precision=HI, preferred_element_type=F32)[0:1]
    y = y0 + y1 + dvec * xs
    yn = _rms(y * jax.nn.silu(z), ng)
    r = (h0 + h1) * jax.nn.gelu(gy)
    return jnp.concatenate([yn, r], axis=1)


def f_oddfin(o0, o1, g, y0, y1, u, hn, sd, gw, gb):
    parts = []
    for h in range(HG_HEADS):
        sl = slice(h * 128, (h + 1) * 128)
        parts.append(_rms(o0[:, sl] + o1[:, sl], hn[h:h + 1]) * jax.nn.silu(g[:, sl]))
    y = jax.nn.gelu(y0 + y1 + sd * u)
    y = y * jax.nn.sigmoid(jnp.dot(y.astype(BF16), gw.astype(BF16), preferred_element_type=F32) + gb)
    return jnp.concatenate(parts + [y], axis=1)


def f_s5p(lre, lim, lstep, btr, bti):
    step = jnp.exp(lstep)
    mag = jnp.exp(lre * step)
    ar, ai = mag * jnp.cos(lim * step), mag * jnp.sin(lim * step)
    den = lre * lre + lim * lim
    zr = ((ar - 1.0) * lre + ai * lim) / den
    zi = (ai * lre - (ar - 1.0) * lim) / den
    bbr = zr[:, None, :] * btr - zi[:, None, :] * bti
    bbi = zr[:, None, :] * bti + zi[:, None, :] * btr
    return ar, ai, bbr, bbi


def f_lb(logits):
    m = jnp.max(logits, axis=0, keepdims=True)
    e = jnp.exp(logits - m)
    p = e / jnp.sum(e, axis=0, keepdims=True)
    return p[1:2], p[1:2] + p[2:3] + p[3:4]


def f_adamw(w, m, v, *gs):
    g = gs[0]
    for t in gs[1:]:
        g = g + t
    m = ADAM_B1 * m + (1.0 - ADAM_B1) * g
    v = ADAM_B2 * v + (1.0 - ADAM_B2) * jnp.square(g)
    m_hat = m / (1.0 - ADAM_B1 ** ADAM_STEP)
    v_hat = v / (1.0 - ADAM_B2 ** ADAM_STEP)
    delta = -ADAM_LR * (m_hat / (jnp.sqrt(v_hat) + ADAM_EPS) + ADAM_WD * w)
    return g, delta, m, v


def scan_fwd(name, f, grid, ins, y_out, st_out, state_shape, is_first):
    n_in = len(ins)

    def body(*refs):
        y_ref, so_ref, st = refs[n_in], refs[n_in + 1], refs[n_in + 2]
        ids = [pl.program_id(a) for a in range(len(grid))]

        @pl.when(is_first(*ids))
        def _():
            st[...] = jnp.zeros_like(st)

        s = st[...]
        so_ref[...] = s
        y, new = f(*[r[...] for r in refs[:n_in]], s)
        y_ref[...] = y.astype(y_ref.dtype)
        st[...] = new

    return pl.pallas_call(
        body, grid=grid,
        in_specs=[pl.BlockSpec(b, m) for _, b, m in ins],
        out_specs=[pl.BlockSpec(y_out[2], y_out[3]), pl.BlockSpec(st_out[2], st_out[3])],
        out_shape=[jax.ShapeDtypeStruct(y_out[0], y_out[1]), jax.ShapeDtypeStruct(st_out[0], st_out[1])],
        scratch_shapes=[pltpu.VMEM(state_shape, F32)],
        name=name, compiler_params=_cparams(("arbitrary",) * len(grid)),
    )(*[a for a, _, _ in ins])


def scan_bwd(name, f, grid, ins, st_in, dy, grads, state_shape, is_first):
    n_in = len(ins)
    didx = [g[0] for g in grads]

    def body(*refs):
        s_ref, dy_ref = refs[n_in], refs[n_in + 1]
        g_refs = refs[n_in + 2:n_in + 2 + len(grads)]
        dst = refs[n_in + 2 + len(grads)]
        ids = [pl.program_id(a) for a in range(len(grid))]

        @pl.when(is_first(*ids))
        def _():
            dst[...] = jnp.zeros_like(dst)

        vals = [r[...] for r in refs[:n_in]]

        def fd(s, *dv):
            full = list(vals)
            for i, v in zip(didx, dv):
                full[i] = v
            return f(*full, s)

        (y, _), vjp = jax.vjp(fd, s_ref[...], *[vals[i] for i in didx])
        gs = vjp((dy_ref[...].astype(y.dtype), dst[...]))
        dst[...] = gs[0]
        for g, spec, ref in zip(gs[1:], grads, g_refs):
            first = spec[5]
            if first is None:
                ref[...] = g.astype(ref.dtype)
            else:
                fst = first(*ids)

                @pl.when(fst)
                def _(ref=ref, g=g):
                    ref[...] = g.astype(ref.dtype)

                @pl.when(jnp.logical_not(fst))
                def _(ref=ref, g=g):
                    ref[...] += g.astype(ref.dtype)

    allin = list(ins) + [st_in, dy]
    return pl.pallas_call(
        body, grid=grid,
        in_specs=[pl.BlockSpec(b, m) for _, b, m in allin],
        out_specs=[pl.BlockSpec(s[3], s[4]) for s in grads],
        out_shape=[jax.ShapeDtypeStruct(s[1], s[2]) for s in grads],
        scratch_shapes=[pltpu.VMEM(state_shape, F32)],
        name=name, compiler_params=_cparams(("arbitrary",) * len(grid)),
    )(*[a for a, _, _ in allin])


def _tile_order(order, nt, nctx=1):
    rev = lambda j: jnp.where(j < nctx, nctx - 1 - j, nt - 1 - (j - nctx))
    if order == 'F':
        return (lambda j: j), True
    if order == 'Fb':
        return (lambda j: nt - 1 - j), False
    if order == 'R':
        return rev, False
    return (lambda j: rev(nt - 1 - j)), True


def _scan8(coef, val, sub, asc):
    for step in (1, 2, 4):
        shift = step if asc else 8 - step
        keep = (sub >= step) if asc else (sub < 8 - step)
        val = jnp.where(keep, coef * pltpu.roll(val, shift, 0) + val, val)
        coef = jnp.where(keep, coef * pltpu.roll(coef, shift, 0), coef)
    return coef, val


def _prev_rows(tile, carry, sub, asc):
    return jnp.where(sub == 0, carry, pltpu.roll(tile, 1, 0)) if asc else jnp.where(sub == 7, carry, pltpu.roll(tile, 7, 0))


def _last_row(tile, asc):
    return jnp.broadcast_to(tile[7:8] if asc else tile[0:1], tile.shape)


def linrec(name, a, b, order):
    bsz, tt, cols = a.shape
    tq = _pick(tt, (256, 128))
    nt, ng, nj = tt // tq, tq // 8, cols // 128
    phys, asc = _tile_order(order, nt)

    def body(a_ref, b_ref, h_ref, hp_ref, hc):
        @pl.when(pl.program_id(0) == 0)
        def _():
            hc[...] = jnp.zeros_like(hc)

        sub = lax.broadcasted_iota(jnp.int32, (8, 128), 0)

        def group(i, carry):
            rows = pl.ds(pl.multiple_of((i if asc else ng - 1 - i) * 8, 8), 8)
            for bi in range(bsz):
                for j in range(nj):
                    cs = slice(j * 128, (j + 1) * 128)
                    h_in = hc[bi, j]
                    ca, cv = _scan8(a_ref[bi, rows, cs], b_ref[bi, rows, cs], sub, asc)
                    h = ca * h_in + cv
                    h_ref[bi, rows, cs] = h
                    hp_ref[bi, rows, cs] = _prev_rows(h, h_in, sub, asc)
                    hc[bi, j] = _last_row(h, asc)
            return carry

        lax.fori_loop(0, ng, group, 0)

    spec = pl.BlockSpec((bsz, tq, cols), lambda j: (0, phys(j), 0))
    return pl.pallas_call(
        body, grid=(nt,), in_specs=[spec, spec], out_specs=[spec, spec],
        out_shape=[jax.ShapeDtypeStruct(a.shape, F32)] * 2,
        scratch_shapes=[pltpu.VMEM((bsz, nj, 8, 128), F32)],
        name=name, compiler_params=_cparams(("arbitrary",)),
    )(a, b)


def linrec_bwd(name, a, dh, hprev, order):
    bsz, tt, cols = a.shape
    tq = _pick(tt, (256, 128))
    nt, ng, nj = tt // tq, tq // 8, cols // 128
    phys, asc = _tile_order(order, nt)

    def body(a_ref, dh_ref, hp_ref, g_ref, ga_ref, gc, ac):
        @pl.when(pl.program_id(0) == 0)
        def _():
            gc[...] = jnp.zeros_like(gc)
            ac[...] = jnp.zeros_like(ac)

        sub = lax.broadcasted_iota(jnp.int32, (8, 128), 0)

        def group(i, carry):
            rows = pl.ds(pl.multiple_of((i if asc else ng - 1 - i) * 8, 8), 8)
            for bi in range(bsz):
                for j in range(nj):
                    cs = slice(j * 128, (j + 1) * 128)
                    a_tile = a_ref[bi, rows, cs]
                    ca, cv = _scan8(_prev_rows(a_tile, ac[bi, j], sub, asc), dh_ref[bi, rows, cs], sub, asc)
                    g = ca * gc[bi, j] + cv
                    g_ref[bi, rows, cs] = g
                    ga_ref[bi, rows, cs] = g * hp_ref[bi, rows, cs]
                    gc[bi, j] = _last_row(g, asc)
                    ac[bi, j] = _last_row(a_tile, asc)
            return carry

        lax.fori_loop(0, ng, group, 0)

    spec = pl.BlockSpec((bsz, tq, cols), lambda j: (0, phys(j), 0))
    return pl.pallas_call(
        body, grid=(nt,), in_specs=[spec, spec, spec], out_specs=[spec, spec],
        out_shape=[jax.ShapeDtypeStruct(a.shape, F32)] * 2,
        scratch_shapes=[pltpu.VMEM((bsz, nj, 8, 128), F32), pltpu.VMEM((bsz, nj, 8, 128), F32)],
        name=name, compiler_params=_cparams(("arbitrary",)),
    )(a, dh, hprev)


def _cmul(a, b):
    return a[0] * b[0] - a[1] * b[1], a[0] * b[1] + a[1] * b[0]


def _cpow_tables(ar, ai, asc):
    pows = [(ar, ai)]
    for _ in range(7):
        pows.append(_cmul(pows[-1], (ar, ai)))
    tile = lambda p: jnp.broadcast_to(p[:, None, :], (8, 8, 128))
    steps = jnp.stack([jnp.stack([tile(pows[s - 1][0]), tile(pows[s - 1][1])]) for s in (1, 2, 4)])
    order = range(8) if asc else range(7, -1, -1)
    carry = jnp.stack([jnp.stack([pows[i][c] for i in order], axis=1) for c in (0, 1)])
    return steps, carry


def _cscan8(xr, xi, st_ref, j, sub, asc):
    for s, step in enumerate((1, 2, 4)):
        shift = step if asc else 8 - step
        keep = (sub >= step) if asc else (sub < 8 - step)
        pr, pi = st_ref[s, 0, j], st_ref[s, 1, j]
        rr, ri = pltpu.roll(xr, shift, 0), pltpu.roll(xi, shift, 0)
        xr, xi = jnp.where(keep, xr + pr * rr - pi * ri, xr), jnp.where(keep, xi + pr * ri + pi * rr, xi)
    return xr, xi


def clinrec(name, x, coef, order, lc, conj=False, hprev=None):
    btot, tt, cols2 = x.shape
    cols = cols2 // 2
    bsz, ngrp = btot, 1
    tq = 128
    nt, ng, nj = tt // tq, tq // 8, cols // 128
    phys, asc = _tile_order(order, nt, lc // tq)
    steps, carry = _cpow_tables(coef[0], -coef[1] if conj else coef[1], asc)
    adjoint = hprev is not None

    def body(*refs):
        if adjoint:
            x_ref, hp_ref, st_ref, cr_ref, h_ref, dc_ref, hc = refs
        else:
            x_ref, st_ref, cr_ref, h_ref, hp_ref, hc = refs

        @pl.when(pl.program_id(1) == 0)
        def _():
            hc[...] = jnp.zeros_like(hc)

        if adjoint:
            @pl.when((pl.program_id(0) == 0) & (pl.program_id(1) == 0))
            def _():
                dc_ref[...] = jnp.zeros_like(dc_ref)

        sub = lax.broadcasted_iota(jnp.int32, (8, 128), 0)

        def group(i, c_):
            rows = pl.ds(pl.multiple_of((i if asc else ng - 1 - i) * 8, 8), 8)
            for bi in range(bsz):
                for j in range(nj):
                    cr, ci = slice(j * 128, (j + 1) * 128), slice(cols + j * 128, cols + (j + 1) * 128)
                    sr, si = _cscan8(x_ref[bi, rows, cr], x_ref[bi, rows, ci], st_ref, j, sub, asc)
                    in_r, in_i = hc[bi, 0, j], hc[bi, 1, j]
                    pr, pi = cr_ref[0, j], cr_ref[1, j]
                    hr = sr + pr * in_r - pi * in_i
                    hi = si + pr * in_i + pi * in_r
                    h_ref[bi, rows, cr] = hr
                    h_ref[bi, rows, ci] = hi
                    if adjoint:
                        qr, qi = hp_ref[bi, rows, cr], hp_ref[bi, rows, ci]
                        dc_ref[0, j] += hr * qr + hi * qi
                        dc_ref[1, j] += hi * qr - hr * qi
                    else:
                        hp_ref[bi, rows, cr] = _prev_rows(hr, in_r, sub, asc)
                        hp_ref[bi, rows, ci] = _prev_rows(hi, in_i, sub, asc)
                    hc[bi, 0, j] = _last_row(hr, asc)
                    hc[bi, 1, j] = _last_row(hi, asc)
            return c_

        lax.fori_loop(0, ng, group, 0)

    spec = pl.BlockSpec((bsz, tq, cols2), lambda g, j: (g, phys(j), 0))
    full = lambda t: pl.BlockSpec(t.shape, lambda g, j, n=t.ndim: (0,) * n)
    dc_shape = (2, nj, 8, 128)
    if adjoint:
        ins, in_specs = (x, hprev, steps, carry), [spec, spec, full(steps), full(carry)]
        out_specs = [spec, pl.BlockSpec(dc_shape, lambda g, j: (0, 0, 0, 0))]
        out_shape = [jax.ShapeDtypeStruct(x.shape, F32), jax.ShapeDtypeStruct(dc_shape, F32)]
    else:
        ins, in_specs = (x, steps, carry), [spec, full(steps), full(carry)]
        out_specs = [spec, spec]
        out_shape = [jax.ShapeDtypeStruct(x.shape, F32)] * 2
    return pl.pallas_call(
        body, grid=(ngrp, nt), in_specs=in_specs, out_specs=out_specs, out_shape=out_shape,
        scratch_shapes=[pltpu.VMEM((bsz, 2, nj, 8, 128), F32)],
        name=name, compiler_params=_cparams(("arbitrary", "arbitrary")),
    )(*ins)


def _blockdiag(bb):
    eye = jnp.eye(16, dtype=bb.dtype)
    return (bb[:, :, None, :] * eye[:, None, :, None]).reshape(256, 1024)


def _blockdiag_t(c):
    eye = jnp.eye(16, dtype=c.dtype)
    return (jnp.swapaxes(c, 1, 2)[:, :, None, :] * eye[:, None, :, None]).reshape(1024, 256)


def _unblockdiag(m):
    eye = jnp.eye(16, dtype=m.dtype)
    return jnp.sum(m.reshape(16, 16, 16, 64) * eye[:, None, :, None], axis=2)


def _unblockdiag_t(m):
    eye = jnp.eye(16, dtype=m.dtype)
    return jnp.swapaxes(jnp.sum(m.reshape(16, 64, 16, 16) * eye[:, None, :, None], axis=2), 1, 2)


def _pad_rows(v, rows=8, cols=128):
    out = jnp.zeros((rows, cols), F32)
    return out.at[0, :v.shape[0]].set(v)


def local_step(x, c, ctx, c_ctx, target, W):
    B, Tx, _ = x.shape
    Lc = ctx.shape[1]
    Tt = Lc + Tx
    tb = Lc
    nt = Tt // tb
    M = B * Tt
    nc = Tt // SSD_CHUNK
    ncc = Lc // SSD_CHUNK
    G = {}

    def add_grad(name, idx, val):
        G.setdefault(name, {})[idx] = val

    def tok(a, cb=None, off=0):
        cb = a.shape[-1] if cb is None else cb
        return (a, (None, tb, cb), lambda b, j, off=off: (b, j, off))

    def tok_out(cols, dtype=F32):
        return ((B, Tt, cols), dtype, (None, tb, cols), lambda b, j: (b, j, 0))

    def vec(a):
        return (a, a.shape, lambda *ids, n=a.ndim: (0,) * n)

    def vec_acc(shape):
        return (shape, F32, shape, lambda *ids, n=len(shape): (0,) * n, lambda *ids: functools.reduce(jnp.logical_and, [i == 0 for i in ids]))

    def modv(l, which):
        return (modr, (None, None, None, 1, D), lambda b, j, l=l, which=which: (l, jnp.where(j == 0, 4, b), which, 0, 0))

    dmod_spec = ((B, 2, 1, D), F32, (None, None, 1, D), lambda b, j: (b, jnp.where(j == 0, 0, 1), 0, 0), lambda b, j: j <= 1)

    def phys_chunk(n_all, n_ctx, reverse):
        if not reverse:
            return lambda s: s
        return lambda s: jnp.where(s < n_ctx, n_ctx - 1 - s, n_all - 1 - (s - n_ctx))

    cc = jnp.zeros((8, D), F32).at[:B].set(c).at[4].set(c_ctx)
    nmc = N_MOD * D // 1536

    def f_mod(ccv, w, b):
        return jnp.dot(jax.nn.silu(ccv).astype(BF16), w, preferred_element_type=F32) + b

    mod = tile_fwd('mod_fwd', f_mod, (DEPTH, nmc),
                   [(cc, (8, D), lambda l, n: (0, 0)), (W['w_mod'], (None, D, 1536), lambda l, n: (l, 0, n)),
                    (W['b_mod'].reshape(DEPTH, 1, N_MOD * D), (None, 1, 1536), lambda l, n: (l, 0, n))],
                   [((DEPTH, 8, N_MOD * D), F32, (None, 8, 1536), lambda l, n: (l, 0, n))])[0]
    modr = mod.reshape(DEPTH, 8, N_MOD, 1, D)
    dmods = {}

    lb1, lb3 = tile_fwd('lb_fwd', f_lb, (1,), [vec(W['hg_lb_logits'])],
                        [((1, HG_W), F32, (1, HG_W), lambda i: (0, 0))] * 2)
    dlb = {1: [], 3: []}

    x0 = jnp.concatenate([ctx, x], axis=1)
    conv_mk = ffnconv_masks(Tt, Lc)
    R = [dict() for _ in range(DEPTH)]

    xprev, oprev = x0, None
    for l in range(DEPTH):
        r = R[l]
        j = l // 2
        ng = W['norm_mix_g'][l][None]
        if l == 0:
            h1 = tile_fwd(f'nm0_fwd', f_nm0, (B, nt), [tok(xprev), vec(ng), modv(l, 0), modv(l, 1)], [tok_out(D, BF16)])[0]
            xa = xprev
        else:
            xa, h1 = tile_fwd(f'nm_mix_fwd{l}', f_nm, (B, nt),
                              [tok(xprev), tok(oprev), modv(l - 1, 5), vec(ng), modv(l, 0), modv(l, 1)],
                              [tok_out(D), tok_out(D, BF16)])
        r['xin'], r['oin'], r['xa'], r['h1'] = xprev, oprev, xa, h1
        h1m = h1.reshape(M, D)
        if l % 2 == 0:
            win = W['ev_w_in_p'][j]
            proj = mm(f'ev_proj{l}', [(h1m, win, 0)]).reshape(B, Tt, EV_PAD)
            r['proj'] = proj
            scw, scb = W['ssd_conv_w'][j], W['ssd_conv_b'][j][None]
            lcw, lcb = W['lru_conv_w'][j], W['lru_conv_b'][j][None]

            def conv_call(name, colblk0, w, b, wblk0, ncols, act):
                return tile_fwd(name, functools.partial(f_conv1d, lc=Lc, act=act), (ncols // 256, B),
                                [(proj, (None, Tt, 256), lambda cb, bi: (bi, 0, colblk0 + cb)),
                                 (w, (4, 256), lambda cb, bi: (0, wblk0 + cb)), (b, (1, 256), lambda cb, bi: (0, wblk0 + cb))],
                                [((B, Tt, ncols), F32, (None, Tt, 256), lambda cb, bi: (bi, 0, cb))])[0]
            xs_c = conv_call(f'conv_xs{l}', 12, scw, scb, 0, 1024, True)
            bc_c = conv_call(f'conv_bc{l}', 16, scw, scb, 4, 512, True)
            u_c = conv_call(f'conv_u{l}', 8, lcw, lcb, 0, 1024, False)
            r['xs'], r['bc'], r['u'] = xs_c, bc_c, u_c
            bias = _pad_rows(W['ssd_dt_bias'][j].reshape(-1), 1)
            alog = _pad_rows(W['ssd_a_log'][j].reshape(-1), 1)
            r['bias'], r['alog'] = bias, alog
            r['y'], r['st'], r['a4'], r['hp4'], r['h'] = [], [], [], [], []
            for d in range(2):
                ph = phys_chunk(nc, ncc, d == 1)
                y, st = scan_fwd(
                    f'ssd_fwd{l}_{d}', functools.partial(f_ssd, d=d, reverse=(d == 1)), (B, nc),
                    [(xs_c, (None, SSD_CHUNK, D), lambda b, s, ph=ph: (b, ph(s), 0)),
                     (bc_c, (None, SSD_CHUNK, 512), lambda b, s, ph=ph: (b, ph(s), 0)),
                     (proj, (None, SSD_CHUNK, 128), lambda b, s, ph=ph: (b, ph(s), 36)),
                     vec(bias), vec(alog)],
                    ((B, Tt, D), F32, (None, SSD_CHUNK, D), lambda b, s, ph=ph: (b, ph(s), 0)),
                    ((B, nc, 8, 128, 128), F32, (None, None, 8, 128, 128), lambda b, s: (b, s, 0, 0, 0)),
                    (8, 128, 128), lambda b, s: s == 0)
                r['y'].append(y)
                r['st'].append(st)
                a_d, bx_d = tile_fwd(
                    f'gates_fwd{l}_{d}', f_gates, (B, nt),
                    [tok(u_c), vec(W['lru_w_a'][j, d]), vec(W['lru_b_a'][j, d][None]), vec(W['lru_w_i'][j, d]),
                     vec(W['lru_b_i'][j, d][None]), vec(W['lru_lam'][j, d][None])],
                    [tok_out(D), tok_out(D)])
                h_d, hp_d = linrec(f'lru_fwd{l}_{d}', a_d, bx_d, 'F' if d == 0 else 'R')
                r['a4'].append(a_d)
                r['hp4'].append(hp_d)
                r['h'].append(h_d)
            dpad = _pad_rows(W['ssd_d'][j])
            sng = W['ssd_norm_g'][j][None]
            r['dpad'], r['sng'] = dpad, sng
            mix = tile_fwd(f'ssdfin_fwd{l}', f_ssdfin, (B, nt),
                           [tok(r['y'][0]), tok(r['y'][1]), tok(xs_c), tok(proj, D, 0), tok(proj, D, 1), tok(r['h'][0]),
                            tok(r['h'][1]), vec(dpad), vec(sng)], [tok_out(2 * D, BF16)])[0]
            wout = W['ev_w_out'][j]
        else:
            win = W['od_w_in'][j]
            proj = mm(f'od_proj{l}', [(h1m, win, 0)]).reshape(B, Tt, 4096)
            r['proj'] = proj
            lbv = lb1 if l == 1 else lb3
            ns = nc
            r['o'], r['zst'], r['coef'], r['bcat'], r['ccat'], r['hp5'], r['hcat'], r['yd'], r['s5in'] = [], [], [], [], [], [], [], [], []
            u2 = proj[:, :, 3840:].reshape(M, S5_W)
            r['u2'] = u2
            for d in range(2):
                ph = phys_chunk(ns, ncc, d == 1)
                o_d, zst = scan_fwd(
                    f'hgrn_fwd{l}_{d}', functools.partial(f_hgrn_group, reverse=(d == 1)), (HG_GROUPS, B, ns),
                    [(proj, (None, 128, HG_GW), lambda h, b, s, ph=ph: (b, ph(s), h)),
                     (proj, (None, 128, HG_GW), lambda h, b, s, ph=ph, d=d: (b, ph(s), (1 + d) * HG_GROUPS + h)),
                     (proj, (None, 128, HG_GW), lambda h, b, s, ph=ph: (b, ph(s), 3 * HG_GROUPS + h)),
                     (lbv, (1, HG_GW), lambda h, b, s: (0, h))],
                    ((B, Tt, HG_W), F32, (None, 128, HG_GW), lambda h, b, s, ph=ph: (b, ph(s), h)),
                    ((B, HG_GROUPS, ns, HG_GW, 128), F32, (None, None, None, HG_GW, 128), lambda h, b, s: (b, h, s, 0, 0)),
                    (HG_GW, 128), lambda h, b, s: s == 0)
                r['o'].append(o_d)
                r['zst'].append(zst)
                s5in = [W['s5_lam_re'][j, d], W['s5_lam_im'][j, d], W['s5_log_step'][j, d].reshape(16, 1),
                        jnp.swapaxes(W['s5_b_re'][j], 1, 2), jnp.swapaxes(W['s5_b_im'][j], 1, 2)]
                r['s5in'].append(s5in)
                ar, ai, bbr, bbi = tile_fwd(f's5p_fwd{l}_{d}', f_s5p, (1,), [vec(t) for t in s5in],
                                            [((16, 64), F32, (16, 64), lambda i: (0, 0))] * 2
                                            + [((16, 16, 64), F32, (16, 16, 64), lambda i: (0, 0, 0))] * 2)
                coef = jnp.stack([ar.reshape(8, 128), ai.reshape(8, 128)])
                bcat = jnp.concatenate([_blockdiag(bbr), _blockdiag(bbi)], axis=1).astype(BF16)
                ccat = jnp.concatenate([_blockdiag_t(W['s5_c_re'][j, d]), -_blockdiag_t(W['s5_c_im'][j, d])], axis=0).astype(BF16)
                xcat = mm(f's5_in{l}_{d}', [(u2, bcat, 0)])
                h5, hp5 = clinrec(f's5_fwd{l}_{d}', xcat.reshape(B, Tt, 2 * D), coef, 'F' if d == 0 else 'R', Lc)
                hcat = h5.reshape(M, 2 * D)
                yd = mm(f's5_out{l}_{d}', [(hcat, ccat, 0)]).reshape(B, Tt, S5_W)
                r['coef'].append(coef)
                r['bcat'].append(bcat)
                r['ccat'].append(ccat)
                r['hp5'].append(hp5)
                r['hcat'].append(hcat)
                r['yd'].append(yd)
            hn = jnp.zeros((8, 128), F32).at[:HG_HEADS].set(W['hg_norm_g'][j])
            sd, gw, gb = W['s5_d'][j][None], W['s5_glu_w'][j], W['s5_glu_b'][j][None]
            r['fin_par'] = (hn, sd, gw, gb)
            mix = tile_fwd(f'oddfin_fwd{l}', f_oddfin, (B, nt),
                           [tok(r['o'][0]), tok(r['o'][1]), tok(proj, HG_W, 4), tok(r['yd'][0]), tok(r['yd'][1]),
                            tok(proj, S5_W, 15), vec(hn), vec(sd), vec(gw), vec(gb)], [tok_out(D, BF16)])[0]
            wout = W['od_w_out'][j]
        r['mix'] = mix
        o1 = mm(f'mix_out{l}', [(mix.reshape(M, -1), wout, 0)]).reshape(B, Tt, D)
        r['o1'] = o1
        fg = W['norm_ffn_g'][l][None]
        xb, h2 = tile_fwd(f'nm_ffn_fwd{l}', f_nm, (B, nt), [tok(xa), tok(o1), modv(l, 2), vec(fg), modv(l, 3), modv(l, 4)],
                          [tok_out(D), tok_out(D, BF16)])
        r['h2'] = h2
        h2m = h2.reshape(M, D)
        a = mm(f'ffn_gate{l}', [(h2m, W['ffn_w_gate'][l], 0)]).reshape(B, Tt, D_FF)
        up = mm(f'ffn_up{l}', [(h2m, W['ffn_w_up'][l], 0)]).reshape(B, Tt, D_FF)
        w9 = W['ffn_conv_w'][l].reshape(9, D_FF)
        cbias = W['ffn_conv_b'][l][None]
        r['a'], r['up'], r['w9'], r['cbias'] = a, up, w9, cbias
        act = tile_fwd(f'ffnconv_fwd{l}', f_ffnconv, (D_FF // 128, B),
                       [(a, (None, Tt, 128), lambda cb, bi: (bi, 0, cb)), (up, (None, Tt, 128), lambda cb, bi: (bi, 0, cb)),
                        (w9, (9, 128), lambda cb, bi: (0, cb)), (cbias, (1, 128), lambda cb, bi: (0, cb)), vec(conv_mk)],
                       [((B, Tt, D_FF), BF16, (None, Tt, 128), lambda cb, bi: (bi, 0, cb))])[0]
        r['act'] = act
        o2 = mm(f'ffn_down{l}', [(act.reshape(M, D_FF), W['ffn_w_down'][l], 0)]).reshape(B, Tt, D)
        xprev, oprev = xb, o2

    vmask = jnp.ones((nt, 1, D), F32).at[0].set(0.0)
    ones = jnp.ones((1, D), F32)
    fng = W['final_norm_g'][None]
    d_xp, d_o2, dg5, dfng, loss_vec = tile_bwd(
        'loss_head', f_final, (B, nt),
        [tok(xprev), tok(oprev), modv(DEPTH - 1, 5), vec(fng),
         (target, (None, tb, D), lambda b, j: (b, jnp.maximum(j - 1, 0), 0)), (vmask, (None, 1, D), lambda b, j: (j, 0, 0))],
        [[vec(ones)]],
        [(0,) + tok_out(D) + (None,), (1,) + tok_out(D, BF16) + (None,), (2,) + dmod_spec, (3,) + vec_acc((1, D))],
        prims=[(0,) + vec_acc((1, D))])
    loss = jnp.sum(loss_vec)
    add_grad('final_norm_g', None, dfng[0])
    dmods[(DEPTH - 1, 5)] = dg5

    for l in reversed(range(DEPTH)):
        r = R[l]
        j = l // 2
        d_o2m = d_o2.reshape(M, D)
        d_act = mm(f'ffn_down_dx{l}', [(d_o2m, W['ffn_w_down'][l], 0)], tb=True).reshape(B, Tt, D_FF)
        add_grad('ffn_w_down', l, mm(f'ffn_down_dw{l}', [(r['act'].reshape(M, D_FF), d_o2m, 0)], ta=True))
        d_a, d_up, dw9, dcb = tile_bwd(
            f'ffnconv_bwd{l}', f_ffnconv, (D_FF // 128, B),
            [(r['a'], (None, Tt, 128), lambda cb, bi: (bi, 0, cb)), (r['up'], (None, Tt, 128), lambda cb, bi: (bi, 0, cb)),
             (r['w9'], (9, 128), lambda cb, bi: (0, cb)), (r['cbias'], (1, 128), lambda cb, bi: (0, cb)), vec(conv_mk)],
            [[(d_act, (None, Tt, 128), lambda cb, bi: (bi, 0, cb))]],
            [(0, (B, Tt, D_FF), BF16, (None, Tt, 128), lambda cb, bi: (bi, 0, cb), None),
             (1, (B, Tt, D_FF), BF16, (None, Tt, 128), lambda cb, bi: (bi, 0, cb), None),
             (2, (9, D_FF), F32, (9, 128), lambda cb, bi: (0, cb), lambda cb, bi: bi == 0),
             (3, (1, D_FF), F32, (1, 128), lambda cb, bi: (0, cb), lambda cb, bi: bi == 0)])
        add_grad('ffn_conv_w', l, dw9.reshape(3, 3, D_FF))
        add_grad('ffn_conv_b', l, dcb[0])
        d_am, d_upm = d_a.reshape(M, D_FF), d_up.reshape(M, D_FF)
        h2m = r['h2'].reshape(M, D)
        d_h2 = mm(f'ffn_in_dx{l}', [(d_am, W['ffn_w_gate'][l], 0), (d_upm, W['ffn_w_up'][l], 0)], tb=True).reshape(B, Tt, D)
        add_grad('ffn_w_gate', l, mm(f'ffn_gate_dw{l}', [(h2m, d_am, 0)], ta=True))
        add_grad('ffn_w_up', l, mm(f'ffn_up_dw{l}', [(h2m, d_upm, 0)], ta=True))
        fg = W['norm_ffn_g'][l][None]
        d_xa, d_o1, dgate, dfg, dsh, dsc = tile_bwd(
            f'nm_ffn_bwd{l}', f_nm, (B, nt), [tok(r['xa']), tok(r['o1']), modv(l, 2), vec(fg), modv(l, 3), modv(l, 4)],
            [[tok(d_xp)], [tok(d_h2)]],
            [(0,) + tok_out(D) + (None,), (1,) + tok_out(D, BF16) + (None,), (2,) + dmod_spec, (3,) + vec_acc((1, D)),
             (4,) + dmod_spec, (5,) + dmod_spec])
        add_grad('norm_ffn_g', l, dfg[0])
        dmods[(l, 2)], dmods[(l, 3)], dmods[(l, 4)] = dgate, dsh, dsc
        d_o1m = d_o1.reshape(M, D)
        h1m = r['h1'].reshape(M, D)
        proj = r['proj']
        if l % 2 == 0:
            wout, win = W['ev_w_out'][j], W['ev_w_in_p'][j]
            d_mix = mm(f'mix_out_dx{l}', [(d_o1m, wout, 0)], tb=True).reshape(B, Tt, 2 * D)
            add_grad('ev_w_out', j, mm(f'mix_out_dw{l}', [(r['mix'].reshape(M, 2 * D), d_o1m, 0)], ta=True))
            d_y, d_xs_fin, d_z, d_gy, d_h, ddpad, dsng = tile_bwd(
                f'ssdfin_bwd{l}', f_ssdfin, (B, nt),
                [tok(r['y'][0]), tok(r['y'][1]), tok(r['xs']), tok(proj, D, 0), tok(proj, D, 1), tok(r['h'][0]), tok(r['h'][1]),
                 vec(r['dpad']), vec(r['sng'])],
                [[tok(d_mix)]],
                [(0,) + tok_out(D) + (None,), (2,) + tok_out(D) + (None,), (3,) + tok_out(D) + (None,), (4,) + tok_out(D) + (None,),
                 (5,) + tok_out(D) + (None,), (7,) + vec_acc((8, 128)), (8,) + vec_acc((1, D))])
            add_grad('ssd_d', j, ddpad[0, :SSD_HEADS])
            add_grad('ssd_norm_g', j, dsng[0])
            d_xs_parts, d_bc_parts, d_dt_parts, d_u_parts = [d_xs_fin], [], [], []
            dbias_t, dalog_t = [], []
            dh4 = d_h
            for d in range(2):
                ph0 = phys_chunk(nc, ncc, d == 1)

                def ph(s, ph0=ph0):
                    return ph0(nc - 1 - s)
                dxs_d, dbc_d, ddt_d, dbias, dalog = scan_bwd(
                    f'ssd_bwd{l}_{d}', functools.partial(f_ssd, d=d, reverse=(d == 1)), (B, nc),
                    [(r['xs'], (None, SSD_CHUNK, D), lambda b, s, ph=ph: (b, ph(s), 0)),
                     (r['bc'], (None, SSD_CHUNK, 512), lambda b, s, ph=ph: (b, ph(s), 0)),
                     (proj, (None, SSD_CHUNK, 128), lambda b, s, ph=ph: (b, ph(s), 36)),
                     vec(r['bias']), vec(r['alog'])],
                    (r['st'][d], (None, None, 8, 128, 128), lambda b, s: (b, nc - 1 - s, 0, 0, 0)),
                    (d_y, (None, SSD_CHUNK, D), lambda b, s, ph=ph: (b, ph(s), 0)),
                    [(0, (B, Tt, D), F32, (None, SSD_CHUNK, D), lambda b, s, ph=ph: (b, ph(s), 0), None),
                     (1, (B, Tt, 512), F32, (None, SSD_CHUNK, 512), lambda b, s, ph=ph: (b, ph(s), 0), None),
                     (2, (B, Tt, 128), F32, (None, SSD_CHUNK, 128), lambda b, s, ph=ph: (b, ph(s), 0), None),
                     (3,) + vec_acc((1, 128)), (4,) + vec_acc((1, 128))],
                    (8, 128, 128), lambda b, s: s == 0)
                d_xs_parts.append(dxs_d)
                d_bc_parts.append(dbc_d)
                d_dt_parts.append(ddt_d)
                dbias_t.append(dbias)
                dalog_t.append(dalog)
                g4, ga4 = linrec_bwd(f'lru_bwd{l}_{d}', r['a4'][d], dh4, r['hp4'][d], 'Fb' if d == 0 else 'Rb')
                du_g, dwa, dba, dwi, dbi, dlam = tile_bwd(
                    f'gates_bwd{l}_{d}', f_gates, (B, nt),
                    [tok(r['u']), vec(W['lru_w_a'][j, d]), vec(W['lru_b_a'][j, d][None]), vec(W['lru_w_i'][j, d]),
                     vec(W['lru_b_i'][j, d][None]), vec(W['lru_lam'][j, d][None])],
                    [[tok(ga4)], [tok(g4)]],
                    [(0,) + tok_out(D) + (None,), (1,) + vec_acc((8, 128, 128)), (2,) + vec_acc((1, D)), (3,) + vec_acc((8, 128, 128)),
                     (4,) + vec_acc((1, D)), (5,) + vec_acc((1, D))])
                d_u_parts.append(du_g)
                add_grad('lru_w_a', (j, d), dwa)
                add_grad('lru_b_a', (j, d), dba[0])
                add_grad('lru_w_i', (j, d), dwi)
                add_grad('lru_b_i', (j, d), dbi[0])
                add_grad('lru_lam', (j, d), dlam[0])
            add_grad('ssd_dt_bias', j, (dbias_t[0] + dbias_t[1])[0, :32].reshape(2, SSD_HEADS))
            add_grad('ssd_a_log', j, (dalog_t[0] + dalog_t[1])[0, :32].reshape(2, SSD_HEADS))
            scw, scb = W['ssd_conv_w'][j], W['ssd_conv_b'][j][None]
            lcw, lcb = W['lru_conv_w'][j], W['lru_conv_b'][j][None]

            def conv_bwd(name, colblk0, w, b, wblk0, ncols, act, parts):
                return tile_bwd(
                    name, functools.partial(f_conv1d, lc=Lc, act=act), (ncols // 256, B),
                    [(proj, (None, Tt, 256), lambda cb, bi: (bi, 0, colblk0 + cb)),
                     (w, (4, 256), lambda cb, bi: (0, wblk0 + cb)), (b, (1, 256), lambda cb, bi: (0, wblk0 + cb))],
                    [[(p, (None, Tt, 256), lambda cb, bi: (bi, 0, cb)) for p in parts]],
                    [(0, (B, Tt, ncols), F32, (None, Tt, 256), lambda cb, bi: (bi, 0, cb), None),
                     (1, (4, ncols), F32, (4, 256), lambda cb, bi: (0, cb), lambda cb, bi: bi == 0),
                     (2, (1, ncols), F32, (1, 256), lambda cb, bi: (0, cb), lambda cb, bi: bi == 0)])
            d_xs_raw, dw_xs, db_xs = conv_bwd(f'conv_xs_bwd{l}', 12, scw, scb, 0, 1024, True, d_xs_parts)
            d_bc_raw, dw_bc, db_bc = conv_bwd(f'conv_bc_bwd{l}', 16, scw, scb, 4, 512, True, d_bc_parts)
            d_u_raw, dw_u, db_u = conv_bwd(f'conv_u_bwd{l}', 8, lcw, lcb, 0, 1024, False, d_u_parts)
            add_grad('ssd_conv_w', j, jnp.concatenate([dw_xs, dw_bc], axis=1))
            add_grad('ssd_conv_b', j, jnp.concatenate([db_xs, db_bc], axis=1)[0])
            add_grad('lru_conv_w', j, dw_u)
            add_grad('lru_conv_b', j, db_u[0])
            def f_ev_dproj(z_, gy_, u_, xs_, bc_, t0, t1):
                pad = jnp.zeros((z_.shape[0], EV_PAD - 4736), F32)
                return jnp.concatenate([z_, gy_, u_, xs_, bc_, t0 + t1, pad], axis=1)
            dproj = tile_fwd(f'ev_dproj{l}', f_ev_dproj, (B, nt),
                             [tok(d_z), tok(d_gy), tok(d_u_raw), tok(d_xs_raw), tok(d_bc_raw), tok(d_dt_parts[0]), tok(d_dt_parts[1])],
                             [tok_out(EV_PAD, BF16)])[0].reshape(M, EV_PAD)
            d_h1 = mm(f'ev_proj_dx{l}', [(dproj, win, 0)], tb=True).reshape(B, Tt, D)
            dwp = mm(f'ev_proj_dw{l}', [(h1m, dproj, 0)], ta=True)
            add_grad('ev_w_in', j, jnp.concatenate([dwp[:, 0:1024], dwp[:, 3072:4640], dwp[:, 1024:3072]], axis=1))
        else:
            wout, win = W['od_w_out'][j], W['od_w_in'][j]
            d_mix = mm(f'mix_out_dx{l}', [(d_o1m, wout, 0)], tb=True).reshape(B, Tt, D)
            add_grad('od_w_out', j, mm(f'mix_out_dw{l}', [(r['mix'].reshape(M, D), d_o1m, 0)], ta=True))
            hn, sd, gw, gb = r['fin_par']
            d_o, d_g, d_yv, d_u_fin, dhn, dsd, dgw, dgb = tile_bwd(
                f'oddfin_bwd{l}', f_oddfin, (B, nt),
                [tok(r['o'][0]), tok(r['o'][1]), tok(proj, HG_W, 4), tok(r['yd'][0]), tok(r['yd'][1]), tok(proj, S5_W, 15),
                 vec(hn), vec(sd), vec(gw), vec(gb)],
                [[tok(d_mix)]],
                [(0,) + tok_out(HG_W) + (None,), (2,) + tok_out(HG_W) + (None,), (3,) + tok_out(S5_W) + (None,),
                 (5,) + tok_out(S5_W) + (None,), (6,) + vec_acc((8, 128)), (7,) + vec_acc((1, S5_W)), (8,) + vec_acc((S5_W, S5_W)),
                 (9,) + vec_acc((1, S5_W))])
            add_grad('hg_norm_g', j, dhn[:HG_HEADS])
            add_grad('s5_d', j, dsd[0])
            add_grad('s5_glu_w', j, dgw)
            add_grad('s5_glu_b', j, dgb[0])
            lbv = lb1 if l == 1 else lb3
            ns = nc
            dq, df, dv, du_s5 = [], [], [], []
            d_ym = d_yv.reshape(M, S5_W)
            dbt_re, dbt_im = [], []
            for d in range(2):
                ph0 = phys_chunk(ns, ncc, d == 1)

                def ph(s, ph0=ph0):
                    return ph0(ns - 1 - s)
                dq_d, df_d, dv_d, dlb_d = scan_bwd(
                    f'hgrn_bwd{l}_{d}', functools.partial(f_hgrn_group, reverse=(d == 1)), (HG_GROUPS, B, ns),
                    [(proj, (None, 128, HG_GW), lambda h, b, s, ph=ph: (b, ph(s), h)),
                     (proj, (None, 128, HG_GW), lambda h, b, s, ph=ph, d=d: (b, ph(s), (1 + d) * HG_GROUPS + h)),
                     (proj, (None, 128, HG_GW), lambda h, b, s, ph=ph: (b, ph(s), 3 * HG_GROUPS + h)),
                     (lbv, (1, HG_GW), lambda h, b, s: (0, h))],
                    (r['zst'][d], (None, None, None, HG_GW, 128), lambda h, b, s: (b, h, ns - 1 - s, 0, 0)),
                    (d_o, (None, 128, HG_GW), lambda h, b, s, ph=ph: (b, ph(s), h)),
                    [(0, (B, Tt, HG_W), F32, (None, 128, HG_GW), lambda h, b, s, ph=ph: (b, ph(s), h), None),
                     (1, (B, Tt, HG_W), F32, (None, 128, HG_GW), lambda h, b, s, ph=ph: (b, ph(s), h), None),
                     (2, (B, Tt, HG_W), F32, (None, 128, HG_GW), lambda h, b, s, ph=ph: (b, ph(s), h), None),
                     (3, (1, HG_W), F32, (1, HG_GW), lambda h, b, s: (0, h), lambda h, b, s: (b == 0) & (s == 0))],
                    (HG_GW, 128), lambda h, b, s: s == 0)
                dq.append(dq_d)
                df.append(df_d)
                dv.append(dv_d)
                dlb[l].append(dlb_d)
                d_hcat = mm(f's5_out_dx{l}_{d}', [(d_ym, r['ccat'][d], 0)], tb=True)
                dccat = mm(f's5_out_dw{l}_{d}', [(r['hcat'][d], d_ym, 0)], ta=True)
                add_grad('s5_c_re', (j, d), _unblockdiag_t(dccat[:D]))
                add_grad('s5_c_im', (j, d), -_unblockdiag_t(dccat[D:]))
                g5, dcoef8 = clinrec(f's5_bwd{l}_{d}', d_hcat.reshape(B, Tt, 2 * D), r['coef'][d], 'Fb' if d == 0 else 'Rb',
                                     Lc, conj=True, hprev=r['hp5'][d])
                dcoef = jnp.sum(dcoef8, axis=2)
                gcat = g5.reshape(M, 2 * D)
                dbcat = mm(f's5_in_dw{l}_{d}', [(r['u2'], gcat, 0)], ta=True)
                du_s5.append(mm(f's5_in_dx{l}_{d}', [(gcat, r['bcat'][d], 0)], tb=True))
                cts5 = [dcoef[0].reshape(16, 64), dcoef[1].reshape(16, 64), _unblockdiag(dbcat[:, :D]), _unblockdiag(dbcat[:, D:])]
                dlre, dlim, dlst, dbtr, dbti = tile_bwd(
                    f's5p_bwd{l}_{d}', f_s5p, (1,), [vec(t) for t in r['s5in'][d]], [[vec(t)] for t in cts5],
                    [(i, t.shape, F32, t.shape, (lambda *ids, n=t.ndim: (0,) * n), None) for i, t in enumerate(r['s5in'][d])])
                add_grad('s5_lam_re', (j, d), dlre)
                add_grad('s5_lam_im', (j, d), dlim)
                add_grad('s5_log_step', (j, d), dlst[:, 0])
                dbt_re.append(dbtr)
                dbt_im.append(dbti)
            add_grad('s5_b_re', j, jnp.swapaxes(dbt_re[0] + dbt_re[1], 1, 2))
            add_grad('s5_b_im', j, jnp.swapaxes(dbt_im[0] + dbt_im[1], 1, 2))
            def f_od_dproj(q0, q1, f0, f1, v0, v1, g_, u0, u1, u2):
                return jnp.concatenate([q0 + q1, f0, f1, v0 + v1, g_, u0 + u1 + u2], axis=1)
            parts = [dq[0], dq[1], df[0], df[1], dv[0], dv[1], d_g, d_u_fin, du_s5[0].reshape(B, Tt, S5_W),
                     du_s5[1].reshape(B, Tt, S5_W)]
            dproj = tile_fwd(f'od_dproj{l}', f_od_dproj, (B, nt), [tok(t) for t in parts],
                             [tok_out(4096, BF16)])[0].reshape(M, 4096)
            d_h1 = mm(f'od_proj_dx{l}', [(dproj, win, 0)], tb=True).reshape(B, Tt, D)
            add_grad('od_w_in', j, mm(f'od_proj_dw{l}', [(h1m, dproj, 0)], ta=True))
        ng = W['norm_mix_g'][l][None]
        if l == 0:
            d_x0, dng, dsh, dsc = tile_bwd(
                'nm0_bwd', lambda xv, g, sh, sc: (xv, f_nm0(xv, g, sh, sc)), (B, nt),
                [tok(r['xin']), vec(ng), modv(l, 0), modv(l, 1)], [[tok(d_xa)], [tok(d_h1)]],
                [(0,) + tok_out(D) + (None,), (1,) + vec_acc((1, D)), (2,) + dmod_spec, (3,) + dmod_spec])
        else:
            d_xp, d_o2, dgate, dng, dsh, dsc = tile_bwd(
                f'nm_mix_bwd{l}', f_nm, (B, nt),
                [tok(r['xin']), tok(r['oin']), modv(l - 1, 5), vec(ng), modv(l, 0), modv(l, 1)],
                [[tok(d_xa)], [tok(d_h1)]],
                [(0,) + tok_out(D) + (None,), (1,) + tok_out(D, BF16) + (None,), (2,) + dmod_spec, (3,) + vec_acc((1, D)),
                 (4,) + dmod_spec, (5,) + dmod_spec])
            dmods[(l - 1, 5)] = dgate
        add_grad('norm_mix_g', l, dng[0])
        dmods[(l, 0)], dmods[(l, 1)] = dsh, dsc

    grad_x = d_x0[:, Lc:, :]

    (dlogits,) = tile_bwd('lb_bwd', f_lb, (1,), [vec(W['hg_lb_logits'])],
                          [[vec(t) for t in dlb[1]], [vec(t) for t in dlb[3]]],
                          [(0, (DEPTH, HG_W), F32, (DEPTH, HG_W), lambda i: (0, 0), None)])
    add_grad('hg_lb_logits', None, dlogits)

    dm = jnp.stack([jnp.stack([dmods[(l, w)] for w in range(N_MOD)]) for l in range(DEPTH)])
    dlat = jnp.transpose(dm[:, :, :, 1, 0, :], (0, 2, 1, 3)).reshape(DEPTH, B, N_MOD * D)
    dctx = jnp.transpose(dm[:, :, :, 0, 0, :], (0, 2, 1, 3)).reshape(DEPTH, B, N_MOD * D)
    dlat = jnp.zeros((DEPTH, 8, N_MOD * D), F32).at[:, :B].set(dlat)
    dctx = jnp.zeros((DEPTH, 8, N_MOD * D), F32).at[:, :B].set(dctx)

    def mod_bwd_body(cc_ref, w_ref, dl_ref, dc_ref, dw_ref, db_ref, dcc_ref):
        row = lax.broadcasted_iota(jnp.int32, (8, 1), 0)
        dall = dl_ref[...] + jnp.where(row == 4, jnp.sum(dc_ref[...], axis=0, keepdims=True), 0.0)
        s, vjp = jax.vjp(jax.nn.silu, cc_ref[...])
        db16 = dall.astype(BF16)
        dw_ref[...] = lax.dot_general(s.astype(BF16), db16, (((0,), (0,)), ((), ())), preferred_element_type=F32)
        db_ref[...] = jnp.sum(dall, axis=0, keepdims=True)
        ds = lax.dot_general(db16, w_ref[...], (((1,), (1,)), ((), ())), preferred_element_type=F32)
        (dcc,) = vjp(ds)
        first = (pl.program_id(0) == 0) & (pl.program_id(1) == 0)

        @pl.when(first)
        def _():
            dcc_ref[...] = dcc

        @pl.when(jnp.logical_not(first))
        def _():
            dcc_ref[...] += dcc

    dwmod, dbmod, dcc = pl.pallas_call(
        mod_bwd_body, grid=(DEPTH, nmc),
        in_specs=[pl.BlockSpec((8, D), lambda l, n: (0, 0)), pl.BlockSpec((None, D, 1536), lambda l, n: (l, 0, n)),
                  pl.BlockSpec((None, 8, 1536), lambda l, n: (l, 0, n)), pl.BlockSpec((None, 8, 1536), lambda l, n: (l, 0, n))],
        out_specs=[pl.BlockSpec((None, D, 1536), lambda l, n: (l, 0, n)), pl.BlockSpec((None, 1, 1536), lambda l, n: (l, 0, n)),
                   pl.BlockSpec((8, D), lambda l, n: (0, 0))],
        out_shape=[jax.ShapeDtypeStruct((DEPTH, D, N_MOD * D), F32), jax.ShapeDtypeStruct((DEPTH, 1, N_MOD * D), F32),
                   jax.ShapeDtypeStruct((8, D), F32)],
        name='mod_bwd', compiler_params=_cparams(("arbitrary", "arbitrary")),
    )(cc, W['w_mod'], dlat, dctx)
    add_grad('w_mod', None, dwmod)
    add_grad('b_mod', None, dbmod[:, 0])
    add_grad('c_ctx', None, dcc[4])
    return loss, grad_x, G


def assemble_grads(G, like):
    out = {}
    for name, parts in G.items():
        shape = like[name].shape
        if None in parts:
            g = parts[None]
        elif isinstance(next(iter(parts)), tuple):
            g = jnp.stack([jnp.stack([parts[(j, d)] for d in range(2)]) for j in range(shape[0])])
        else:
            g = jnp.stack([parts[i] for i in range(shape[0])])
        out[name] = g.reshape(shape)
    return out


XY_RELS = ((1, 0, 0), (0, 1, 0), (1, 1, 0))
ALL_RELS = tuple((dx, dy, dc) for dx in (0, 1) for dy in (0, 1) for dc in (0, 1))[1:]


def exchange(name, src, out_shape, sends):
    n = len(sends)

    def body(src_ref, out_ref, send_sems, recv_sems):
        me = (lax.axis_index("x"), lax.axis_index("y"), lax.axis_index("c"))

        def pick(ref, sel, tgt):
            return ref if sel is None else ref.at[sel(me, tgt)]

        copies = []
        for k, (rel, ssel, dsel) in enumerate(sends):
            tgt = tuple(1 - m if f else m for m, f in zip(me, rel))
            cp = pltpu.make_async_remote_copy(
                src_ref=pick(src_ref, ssel, tgt), dst_ref=pick(out_ref, dsel, tgt),
                send_sem=send_sems.at[k], recv_sem=recv_sems.at[k], device_id=tgt, device_id_type=MESH)
            cp.start()
            copies.append(cp)
        for cp in copies:
            cp.wait()

    return pl.pallas_call(
        body, out_shape=jax.ShapeDtypeStruct(out_shape, src.dtype),
        in_specs=[pl.BlockSpec(memory_space=pl.ANY)], out_specs=pl.BlockSpec(memory_space=pl.ANY),
        scratch_shapes=[pltpu.SemaphoreType.DMA((n,)), pltpu.SemaphoreType.DMA((n,))],
        name=name,
    )(src)


def _xy_index(dev):
    return 2 * dev[0] + dev[1]


def _my_xy():
    return 2 * lax.axis_index("x") + lax.axis_index("y")


def all_gather_xy(name, shard):
    got = exchange(name, shard, (4,) + shard.shape, [(rel, None, lambda me, tgt: _xy_index(me)) for rel in XY_RELS])
    return lax.dynamic_update_index_in_dim(got, shard, _my_xy(), 0)


def reduce_scatter_xy(name, g4):
    got = exchange(name, g4, (3,) + g4.shape[1:],
                   [(rel, (lambda me, tgt: _xy_index(tgt)), (lambda me, tgt, k=k: k)) for k, rel in enumerate(XY_RELS)])
    return got, lax.dynamic_index_in_dim(g4, _my_xy(), 0, keepdims=False)


def sibling_swap(name, v):
    return exchange(name, v, v.shape, [((0, 0, 1), None, None)])


def all_gather_all(name, v):
    got = exchange(name, v, (8,) + v.shape, [(rel, None, lambda me, tgt: 4 * me[0] + 2 * me[1] + me[2]) for rel in ALL_RELS])
    return lax.dynamic_update_index_in_dim(got, v, 2 * _my_xy() + lax.axis_index("c"), 0)


def all_gather_xy_halves(name, shard):
    half = shard.shape[0] // 2

    def body(src_ref, out_ref, send_sems, recv_sems):
        x, y, c = lax.axis_index("x"), lax.axis_index("y"), lax.axis_index("c")
        mine = pl.ds(c * half, half)
        peers = [(1 - x, y), (x, 1 - y), (1 - x, 1 - y)]

        def copy(k, src, dst, to):
            return pltpu.make_async_remote_copy(src_ref=src, dst_ref=dst, send_sem=send_sems.at[k], recv_sem=recv_sems.at[k],
                                                device_id=to, device_id_type=MESH)

        first = [copy(k, src_ref.at[mine], out_ref.at[2 * x + y, mine], (px, py, c)) for k, (px, py) in enumerate(peers)]
        for cp in first:
            cp.start()
        passed = []
        for k, (px, py) in enumerate(peers):
            first[k].wait_recv()
            landed = out_ref.at[2 * px + py, mine]
            fw = copy(3 + k, landed, landed, (x, y, 1 - c))
            fw.start()
            passed.append(fw)
        for fw in passed:
            fw.wait_recv()
        for cp in first + passed:
            cp.wait_send()

    got = pl.pallas_call(
        body, out_shape=jax.ShapeDtypeStruct((4,) + shard.shape, shard.dtype),
        in_specs=[pl.BlockSpec(memory_space=pl.ANY)], out_specs=pl.BlockSpec(memory_space=pl.ANY),
        scratch_shapes=[pltpu.SemaphoreType.DMA((6,)), pltpu.SemaphoreType.DMA((6,))],
        name=name,
    )(shard)
    return lax.dynamic_update_index_in_dim(got, shard, _my_xy(), 0)


def sibling_split(name, g4):
    half = g4.shape[1] // 2
    got = exchange(name, g4, (4, half) + g4.shape[2:],
                   [((0, 0, 1), (lambda me, tgt: (slice(None), pl.ds(tgt[2] * half, half))), None)])
    return got, lax.dynamic_slice_in_dim(g4, lax.axis_index("c") * half, half, axis=1)


def sibling_join(name, q):
    half = q.shape[0]
    got = exchange(name, q, (2 * half,) + q.shape[1:], [((0, 0, 1), None, lambda me, tgt: pl.ds(me[2] * half, half))])
    return lax.dynamic_update_slice_in_dim(got, q, lax.axis_index("c") * half, axis=0)


def _rows_view(shape):
    cols = shape[-1] if len(shape) else 1
    rows = 1
    for s in shape[:-1]:
        rows *= s
    return rows, cols


def _row_block(rows, cols, n_arrays):
    budget = (24 * 1024 * 1024) // (8 * n_arrays * cols)
    if rows <= max(budget, 16):
        return rows
    br = (min(budget, rows) // 16) * 16
    while br > 16 and rows % br:
        br -= 16
    return br if rows % br == 0 else rows


def sum_slots(name, stacked, extra=(), out_dtype=F32):
    k = stacked.shape[0]
    rows, cols = _rows_view(stacked.shape[1:])
    br = _row_block(rows, cols, k + len(extra) + 1)

    def f(s, *more):
        parts = [s[i].astype(F32) for i in range(k)] + [m.astype(F32) for m in more]
        while len(parts) > 1:
            parts = [parts[i] + parts[i + 1] for i in range(0, len(parts) - 1, 2)] + ([parts[-1]] if len(parts) % 2 else [])
        return parts[0]

    out = tile_fwd(name, f, (rows // br,),
                   [(stacked.reshape(k, rows, cols), (k, br, cols), lambda i: (0, i, 0))]
                   + [(e.reshape(rows, cols), (br, cols), lambda i: (i, 0)) for e in extra],
                   [((rows, cols), out_dtype, (br, cols), lambda i: (i, 0))])[0]
    return out.reshape(stacked.shape[1:])


def adamw(name, w, m, v, gs):
    rows, cols = _rows_view(w.shape)
    br = _row_block(rows, cols, 7 + len(gs))
    spec = lambda a: (a.reshape(rows, cols), (br, cols), lambda i: (i, 0))
    outs = tile_fwd(name, f_adamw, (rows // br,), [spec(t) for t in (w, m, v) + tuple(gs)],
                    [((rows, cols), F32, (br, cols), lambda i: (i, 0))] * 4)
    return [o.reshape(w.shape) for o in outs]


IN_NAMES = ['x', 'c', 'ctx'] + W_NAMES + ['loss_target'] + ['m_' + n for n in W_NAMES] + ['v_' + n for n in W_NAMES]
SMALL_PAD = 128 * 1024


def kernel(x, c, ctx, c_ctx, w_mod, b_mod, norm_mix_g, norm_ffn_g, final_norm_g, ev_w_in, ev_w_out, ssd_conv_w, ssd_conv_b, ssd_dt_bias, ssd_a_log, ssd_d, ssd_norm_g, lru_conv_w, lru_conv_b, lru_w_a, lru_b_a, lru_w_i, lru_b_i, lru_lam, od_w_in, od_w_out, hg_lb_logits, hg_norm_g, s5_lam_re, s5_lam_im, s5_log_step, s5_b_re, s5_b_im, s5_c_re, s5_c_im, s5_d, s5_glu_w, s5_glu_b, ffn_w_gate, ffn_w_up, ffn_conv_w, ffn_conv_b, ffn_w_down, loss_target, m_c_ctx, m_w_mod, m_b_mod, m_norm_mix_g, m_norm_ffn_g, m_final_norm_g, m_ev_w_in, m_ev_w_out, m_ssd_conv_w, m_ssd_conv_b, m_ssd_dt_bias, m_ssd_a_log, m_ssd_d, m_ssd_norm_g, m_lru_conv_w, m_lru_conv_b, m_lru_w_a, m_lru_b_a, m_lru_w_i, m_lru_b_i, m_lru_lam, m_od_w_in, m_od_w_out, m_hg_lb_logits, m_hg_norm_g, m_s5_lam_re, m_s5_lam_im, m_s5_log_step, m_s5_b_re, m_s5_b_im, m_s5_c_re, m_s5_c_im, m_s5_d, m_s5_glu_w, m_s5_glu_b, m_ffn_w_gate, m_ffn_w_up, m_ffn_conv_w, m_ffn_conv_b, m_ffn_w_down, v_c_ctx, v_w_mod, v_b_mod, v_norm_mix_g, v_norm_ffn_g, v_final_norm_g, v_ev_w_in, v_ev_w_out, v_ssd_conv_w, v_ssd_conv_b, v_ssd_dt_bias, v_ssd_a_log, v_ssd_d, v_ssd_norm_g, v_lru_conv_w, v_lru_conv_b, v_lru_w_a, v_lru_b_a, v_lru_w_i, v_lru_b_i, v_lru_lam, v_od_w_in, v_od_w_out, v_hg_lb_logits, v_hg_norm_g, v_s5_lam_re, v_s5_lam_im, v_s5_log_step, v_s5_b_re, v_s5_b_im, v_s5_c_re, v_s5_c_im, v_s5_d, v_s5_glu_w, v_s5_glu_b, v_ffn_w_gate, v_ffn_w_up, v_ffn_conv_w, v_ffn_conv_b, v_ffn_w_down):
    a = dict(locals())
    W = {}
    for n in W_NAMES:
        w = a[n]
        if n in SHARD_AXIS:
            ax = SHARD_AXIS[n]
            if n in MATMUL_WEIGHTS:
                g4 = all_gather_xy_halves('ag_' + n, w.astype(BF16))
            else:
                g4 = all_gather_xy('ag_' + n, w)
            shape = list(w.shape)
            shape[ax] *= 4
            W[n] = jnp.moveaxis(g4, 0, ax).reshape(shape)
        else:
            W[n] = w
    e = W['ev_w_in']
    W['ev_w_in_p'] = jnp.concatenate(
        [e[:, :, 0:1024], e[:, :, 2592:3616], e[:, :, 3616:4640], e[:, :, 1024:2560], e[:, :, 2560:2592],
         jnp.zeros((e.shape[0], D, EV_PAD - 4640), e.dtype)], axis=2)

    loss_local, grad_x, G = local_step(a['x'], a['c'], a['ctx'], W['c_ctx'], a['loss_target'], W)
    grads = assemble_grads(G, W)
    loss = lax.psum(loss_local, ("x", "y", "c"))

    res = {}
    for n in W_NAMES:
        if n not in SHARD_AXIS:
            continue
        ax = SHARD_AXIS[n]
        w = a[n]
        gf = grads[n]
        split = gf.reshape(gf.shape[:ax] + (4, w.shape[ax]) + gf.shape[ax + 1:])
        g4 = jnp.moveaxis(split, ax, 0)
        if n in MATMUL_WEIGHTS:
            theirs, ours = sibling_split('rsc_' + n, g4)
            part = sum_slots('csum_' + n, theirs[None], (ours,), BF16)
            got, own = reduce_scatter_xy('rs_' + n, part)
            half = sum_slots('gsum_' + n, got, (own,))
            res[n] = adamw('adamw_' + n, w, a['m_' + n], a['v_' + n], (sibling_join('agc_' + n, half),))
        else:
            got, own = reduce_scatter_xy('rs_' + n, g4)
            mine = sum_slots('gsum_' + n, got, (own,))
            other = sibling_swap('sw_' + n, mine)
            res[n] = adamw('adamw_' + n, w, a['m_' + n], a['v_' + n], (mine, other))
    small = [n for n in W_NAMES if n not in SHARD_AXIS]
    flat = jnp.concatenate([grads[n].reshape(-1) for n in small])
    total = flat.shape[0]
    padded = -(-total // SMALL_PAD) * SMALL_PAD
    flat = jnp.concatenate([flat, jnp.zeros((padded - total,), F32)]).reshape(padded // 128, 128)
    pair = sum_slots('csum_small', flat[None], (sibling_swap('sw_small', flat),))
    summed = sum_slots('gsum_small', all_gather_xy('ag_small', pair)).reshape(-1)
    off = 0
    for n in small:
        size = math.prod(a[n].shape)
        g = summed[off:off + size].reshape(a[n].shape)
        off += size
        res[n] = adamw('adamw_' + n, a[n], a['m_' + n], a['v_' + n], (g,))
    outs = [loss, grad_x]
    for k in range(4):
        outs += [res[n][k] for n in W_NAMES]
    return tuple(outs)
```

```python
import functools
import math

import jax
import jax.numpy as jnp
from jax import lax
from jax.experimental import pallas as pl
from jax.experimental.pallas import tpu as pltpu

F32 = jnp.float32
BF16 = jnp.bfloat16
HI = lax.Precision.HIGHEST
MESH = pl.DeviceIdType.MESH

D = 1024
DEPTH = 4
N_MOD = 6
RMS_EPS = 1e-6
GRID_W = 64
SSD_HEADS = 16
SSD_CHUNK = 128
HG_W = 768
HG_HEADS = 6
HG_CHUNK = 16
HG_PER_STEP = 3
HG_GROUPS = HG_HEADS // HG_PER_STEP
HG_GW = 128 * HG_PER_STEP
S5_W = 256
D_FF = 2816
EV_PAD = 5120
LRU_C = 8.0
V7X_VMEM_LIMIT = 56 * 1024 * 1024
MM_VMEM_BUDGET = 36 * 1024 * 1024

ADAM_LR, ADAM_B1, ADAM_B2, ADAM_EPS, ADAM_WD, ADAM_STEP = 0.001, 0.9, 0.999, 1e-08, 0.01, 10

W_NAMES = ['c_ctx', 'w_mod', 'b_mod', 'norm_mix_g', 'norm_ffn_g', 'final_norm_g', 'ev_w_in', 'ev_w_out', 'ssd_conv_w',
           'ssd_conv_b', 'ssd_dt_bias', 'ssd_a_log', 'ssd_d', 'ssd_norm_g', 'lru_conv_w', 'lru_conv_b', 'lru_w_a', 'lru_b_a',
           'lru_w_i', 'lru_b_i', 'lru_lam', 'od_w_in', 'od_w_out', 'hg_lb_logits', 'hg_norm_g', 's5_lam_re', 's5_lam_im',
           's5_log_step', 's5_b_re', 's5_b_im', 's5_c_re', 's5_c_im', 's5_d', 's5_glu_w', 's5_glu_b', 'ffn_w_gate', 'ffn_w_up',
           'ffn_conv_w', 'ffn_conv_b', 'ffn_w_down']
SHARD_AXIS = {'w_mod': 2, 'ev_w_in': 2, 'ev_w_out': 1, 'ssd_conv_w': 2, 'lru_conv_w': 2, 'lru_b_a': 2, 'lru_b_i': 2,
              'lru_lam': 2, 'od_w_in': 2, 'od_w_out': 1, 's5_d': 1, 's5_glu_w': 1, 's5_glu_b': 1, 'ffn_w_gate': 2,
              'ffn_w_up': 2, 'ffn_conv_w': 3, 'ffn_w_down': 1}
MATMUL_WEIGHTS = ('w_mod', 'ev_w_in', 'ev_w_out', 'od_w_in', 'od_w_out', 'ffn_w_gate', 'ffn_w_up', 'ffn_w_down')


def _cparams(sem=None):
    return pltpu.CompilerParams(vmem_limit_bytes=V7X_VMEM_LIMIT, dimension_semantics=sem)


def _pick(n, cands):
    for c in cands:
        if n % c == 0:
            return c
    return n


def tile_fwd(name, f, grid, ins, outs):
    n_in = len(ins)

    def body(*refs):
        res = f(*[r[...] for r in refs[:n_in]])
        if not isinstance(res, (tuple, list)):
            res = (res,)
        for r, o in zip(res, refs[n_in:]):
            o[...] = r.astype(o.dtype)

    res = pl.pallas_call(
        body, grid=grid,
        in_specs=[pl.BlockSpec(b, m) for _, b, m in ins],
        out_specs=[pl.BlockSpec(b, m) for _, _, b, m in outs],
        out_shape=[jax.ShapeDtypeStruct(s, d) for s, d, _, _ in outs],
        name=name, compiler_params=_cparams(("arbitrary",) * len(grid)),
    )(*[a for a, _, _ in ins])
    return res


def tile_bwd(name, f, grid, ins, cts, grads, prims=()):
    n_in = len(ins)
    ct_flat = [p for c in cts for p in c]
    n_ct = len(ct_flat)
    didx = [g[0] for g in grads]

    def body(*refs):
        in_refs, ct_refs = refs[:n_in], refs[n_in:n_in + n_ct]
        g_refs = refs[n_in + n_ct:n_in + n_ct + len(grads)]
        p_refs = refs[n_in + n_ct + len(grads):]
        vals = [r[...] for r in in_refs]

        def fd(*dv):
            full = list(vals)
            for i, v in zip(didx, dv):
                full[i] = v
            res = f(*full)
            return tuple(res) if isinstance(res, (tuple, list)) else (res,)

        out, vjp = jax.vjp(fd, *[vals[i] for i in didx])
        ctv, k = [], 0
        for o, c in zip(out, cts):
            acc = None
            for _ in c:
                piece = ct_refs[k][...].astype(o.dtype)
                acc = piece if acc is None else acc + piece
                k += 1
            ctv.append(jnp.zeros_like(o) if acc is None else acc.reshape(o.shape))
        gs = vjp(tuple(ctv))
        ids = [pl.program_id(a) for a in range(len(grid))]

        def emit(ref, val, first):
            if first is None:
                ref[...] = val.astype(ref.dtype)
            else:
                is_first = first(*ids)

                @pl.when(is_first)
                def _():
                    ref[...] = val.astype(ref.dtype)

                @pl.when(jnp.logical_not(is_first))
                def _():
                    ref[...] += val.astype(ref.dtype)

        for g, spec, ref in zip(gs, grads, g_refs):
            emit(ref, g, spec[5])
        for spec, ref in zip(prims, p_refs):
            emit(ref, out[spec[0]], spec[5])

    specs = list(grads) + list(prims)
    res = pl.pallas_call(
        body, grid=grid,
        in_specs=[pl.BlockSpec(b, m) for _, b, m in list(ins) + ct_flat],
        out_specs=[pl.BlockSpec(s[3], s[4]) for s in specs],
        out_shape=[jax.ShapeDtypeStruct(s[1], s[2]) for s in specs],
        name=name, compiler_params=_cparams(("arbitrary",) * len(grid)),
    )(*[a for a, _, _ in list(ins) + ct_flat])
    return res


def mm(name, pairs, ta=False, tb=False, out_dtype=F32):
    a0, b0, _ = pairs[0]
    m = a0.shape[1] if ta else a0.shape[0]
    n = b0.shape[0] if tb else b0.shape[1]
    cands = (1024, 1408, 768, 512, 256, 128)
    tks, nks = [], []
    for a, b, _ in pairs:
        k = a.shape[0] if ta else a.shape[1]
        tk = _pick(k, cands)
        tks.append(tk)
        nks.append(k // tk)

    def vmem_bytes(tm, tn):
        tiles = sum(2 * tk * (tm * a.dtype.itemsize + tn * b.dtype.itemsize) for (a, b, _), tk in zip(pairs, tks))
        return tiles + tm * tn * (4 + 2 * jnp.dtype(out_dtype).itemsize)

    tm_c = [c_ for c_ in cands if m % c_ == 0] or [m]
    tn_c = [c_ for c_ in cands if n % c_ == 0] or [n]
    tm, tn = tm_c[0], tn_c[0]
    while vmem_bytes(tm, tn) > MM_VMEM_BUDGET and (len(tm_c) > 1 or len(tn_c) > 1):
        if len(tm_c) > 1 and (tm >= tn or len(tn_c) == 1):
            tm_c = tm_c[1:]
        else:
            tn_c = tn_c[1:]
        tm, tn = tm_c[0], tn_c[0]
    starts = [sum(nks[:p]) for p in range(len(pairs))]
    nk = sum(nks)
    np_ = len(pairs)

    def body(*refs):
        o_ref, acc = refs[2 * np_], refs[2 * np_ + 1]
        kk = pl.program_id(2)

        @pl.when(kk == 0)
        def _():
            acc[...] = jnp.zeros_like(acc)

        for p in range(np_):
            def add(p=p):
                a = refs[2 * p][...].astype(BF16)
                b = refs[2 * p + 1][...].astype(BF16)
                dn = (((0 if ta else 1,), (1 if tb else 0,)), ((), ()))
                acc[...] += lax.dot_general(a, b, dn, preferred_element_type=F32)
            if np_ == 1:
                add()
            else:
                pl.when((kk >= starts[p]) & (kk < starts[p] + nks[p]))(add)

        @pl.when(kk == nk - 1)
        def _():
            o_ref[...] = acc[...].astype(o_ref.dtype)

    in_specs, args = [], []
    for p, (a, b, off) in enumerate(pairs):
        tk, s0, nkp = tks[p], starts[p], nks[p]
        assert off % tk == 0
        boff = off // tk

        def kloc(k, s0=s0, nkp=nkp):
            return jnp.clip(k - s0, 0, nkp - 1)
        if ta:
            in_specs.append(pl.BlockSpec((tk, tm), lambda i, j, k, kloc=kloc: (kloc(k), i)))
        else:
            in_specs.append(pl.BlockSpec((tm, tk), lambda i, j, k, kloc=kloc: (i, kloc(k))))
        if tb:
            in_specs.append(pl.BlockSpec((tn, tk), lambda i, j, k, kloc=kloc, boff=boff: (j, boff + kloc(k))))
        else:
            in_specs.append(pl.BlockSpec((tk, tn), lambda i, j, k, kloc=kloc, boff=boff: (boff + kloc(k), j)))
        args += [a, b]
    return pl.pallas_call(
        body, grid=(m // tm, n // tn, nk), in_specs=in_specs,
        out_specs=pl.BlockSpec((tm, tn), lambda i, j, k: (i, j)),
        out_shape=jax.ShapeDtypeStruct((m, n), out_dtype),
        scratch_shapes=[pltpu.VMEM((tm, tn), F32)],
        name=name, compiler_params=_cparams(("arbitrary", "arbitrary", "arbitrary")),
    )(*args)


def _rms(x, g):
    return x * lax.rsqrt(jnp.mean(x * x, axis=-1, keepdims=True) + RMS_EPS) * g


def f_nm0(x, g, sh, sc):
    return _rms(x, g) * (1.0 + sc) + sh


def f_nm(xp, o, gate, g, sh, sc):
    x = xp + gate * o
    return x, _rms(x, g) * (1.0 + sc) + sh


def f_final(xp, o, gate, g, tgt, valid):
    x = xp + gate * o
    e = (_rms(x, g) - tgt) * valid
    return jnp.sum(e * e, axis=0, keepdims=True) * (0.5 / D)


@functools.partial(jax.custom_vjp, nondiff_argnums=(1,))
def _sroll(x, s):
    return pltpu.roll(x, s, 0)


def _sroll_fwd(x, s):
    return pltpu.roll(x, s, 0), None


def _sroll_bwd(s, _, g):
    return (pltpu.roll(g, (g.shape[0] - s) % g.shape[0], 0),)


_sroll.defvjp(_sroll_fwd, _sroll_bwd)


def _shifted(x, o):
    n = x.shape[0]
    return x if o == 0 else _sroll(x, (n - o) % n)


def f_conv1d(x, w, b, *, lc, act):
    n = x.shape[0]
    pos = lax.broadcasted_iota(jnp.int32, (n, 1), 0)
    lo = jnp.where(pos < lc, 0, lc)
    hi = jnp.where(pos < lc, lc, n)
    y = x * w[1:2] + b
    for k, o in ((0, -1), (2, 1), (3, 2)):
        src = pos + o
        valid = (src >= lo) & (src < hi)
        y = y + jnp.where(valid, _shifted(x, o), 0.0) * w[k:k + 1]
    return jax.nn.silu(y) if act else y


def ffnconv_masks(n, lc):
    pos = lax.broadcasted_iota(jnp.int32, (n, 128), 0)
    is_ctx = pos < lc
    tl = pos - lc
    r = tl // GRID_W
    cc = tl - r * GRID_W
    rows = (n - lc) // GRID_W
    left = jnp.where(is_ctx, pos >= 1, cc >= 1)
    right = jnp.where(is_ctx, pos < lc - 1, cc < GRID_W - 1)
    above = jnp.logical_not(is_ctx) & (r >= 1)
    below = jnp.logical_not(is_ctx) & (r < rows - 1)
    return jnp.stack([left, right, above, below]).astype(F32)


def f_ffnconv(a, up, w, b, mk):
    cols = (mk[0] * _shifted(a, -1), a, mk[1] * _shifted(a, 1))
    y = b
    for dr in (-1, 0, 1):
        k = 3 * (dr + 1)
        inner = cols[0] * w[k:k + 1] + cols[1] * w[k + 1:k + 2] + cols[2] * w[k + 2:k + 3]
        y = y + (inner if dr == 0 else mk[2 + (dr > 0)] * _shifted(inner, GRID_W * dr))
    return jax.nn.silu(y) * up


def f_ssd(xs, bc, dtraw, bias, alog, st, *, d, reverse):
    L = xs.shape[0]
    dtv = jax.nn.softplus(dtraw + bias)
    la = dtv * (-jnp.exp(alog))
    ri = lax.broadcasted_iota(jnp.int32, (L, L), 0)
    ci = lax.broadcasted_iota(jnp.int32, (L, L), 1)
    mask = (ci >= ri) if reverse else (ci <= ri)
    cum = jnp.dot(mask.astype(F32), la, precision=HI, preferred_element_type=F32)
    cum_t = cum.T
    tot = cum[0:1] if reverse else cum[L - 1:L]
    lo = lax.broadcasted_iota(jnp.int32, (1, 128), 1) < 64
    rlo = lax.broadcasted_iota(jnp.int32, (128, 1), 0) < 64
    ys, new = [], []
    for g in range(2):
        bg = bc[:, g * 128:(g + 1) * 128].astype(BF16)
        cg = bc[:, 256 + g * 128:256 + (g + 1) * 128].astype(BF16)
        cb = lax.dot_general(cg, bg, (((1,), (1,)), ((), ())), preferred_element_type=F32)
        sg = st[4 * g:4 * g + 4].reshape(4 * 128, 128)
        ch_all = lax.dot_general(cg, sg.astype(BF16), (((1,), (1,)), ((), ())), preferred_element_type=F32)
        xes, dcols = [], []
        for jj in range(4):
            j = 4 * g + jj
            x = xs[:, j * 128:(j + 1) * 128]
            k1 = 16 * d + 2 * j
            k2 = k1 + 1
            c1, c2 = cum[:, k1:k1 + 1], cum[:, k2:k2 + 1]
            m1 = cb * jnp.exp(jnp.where(mask, c1 - cum_t[k1:k1 + 1, :], -1e30))
            m2 = cb * jnp.exp(jnp.where(mask, c2 - cum_t[k2:k2 + 1, :], -1e30))
            xdt = x * jnp.where(lo, dtv[:, k1:k1 + 1], dtv[:, k2:k2 + 1])
            mcat = jnp.concatenate([m1, m2], axis=1).astype(BF16)
            xcat = jnp.concatenate([jnp.where(lo, xdt, 0.0), jnp.where(lo, 0.0, xdt)], axis=0).astype(BF16)
            y = jnp.dot(mcat, xcat, preferred_element_type=F32)
            y = y + ch_all[:, jj * 128:(jj + 1) * 128] * jnp.where(lo, jnp.exp(c1), jnp.exp(c2))
            t1, t2 = tot[:, k1:k1 + 1], tot[:, k2:k2 + 1]
            xes.append((xdt * jnp.where(lo, jnp.exp(t1 - c1), jnp.exp(t2 - c2))).astype(BF16))
            dcols.append(jnp.where(rlo, jnp.exp(t1), jnp.exp(t2)))
            ys.append(y)
        upd = lax.dot_general(jnp.concatenate(xes, axis=1), bg, (((0,), (0,)), ((), ())), preferred_element_type=F32)
        new.append((sg * jnp.concatenate(dcols, axis=0) + upd).reshape(4, 128, 128))
    return jnp.concatenate(ys, axis=1), jnp.concatenate(new, axis=0)


def f_hgrn(q_raw, f_raw, v, lb, zt, *, reverse):
    n = q_raw.shape[0]
    c = HG_CHUNK
    qa = jax.nn.silu(q_raw)
    logf = jnp.log(lb + (1.0 - lb) * jax.nn.sigmoid(f_raw))
    kk = (1.0 - lb) * jax.nn.sigmoid(-f_raw)
    ri = lax.broadcasted_iota(jnp.int32, (n, n), 0)
    ci = lax.broadcasted_iota(jnp.int32, (n, n), 1)
    tmat = ((ri // c == ci // c) & ((ci >= ri) if reverse else (ci <= ri))).astype(F32)
    cum_all = jnp.dot(tmat, logf, precision=HI, preferred_element_type=F32)
    r3 = lax.broadcasted_iota(jnp.int32, (c, c, 128), 0)
    c3 = lax.broadcasted_iota(jnp.int32, (c, c, 128), 1)
    mask3 = (c3 >= r3) if reverse else (c3 <= r3)
    nch = n // c
    outs = [None] * nch
    for chn in (reversed(range(nch)) if reverse else range(nch)):
        sl = slice(chn * c, (chn + 1) * c)
        q, k, vv, cum = qa[sl], kk[sl], v[sl], cum_all[sl]
        dec = jnp.exp(jnp.where(mask3, cum[:, None, :] - cum[None, :, :], -1e30))
        att = jnp.sum(q[:, None, :] * dec * k[None, :, :], axis=-1, keepdims=True)
        y = jnp.sum(att * vv[None, :, :], axis=1)
        y = y + lax.dot_general((q * jnp.exp(cum)).astype(BF16), zt.astype(BF16), (((1,), (1,)), ((), ())),
                                preferred_element_type=F32)
        tot = cum[0:1] if reverse else cum[c - 1:c]
        kd = (k * jnp.exp(tot - cum)).astype(BF16)
        zt = zt * jnp.exp(tot) + lax.dot_general(vv.astype(BF16), kd, (((0,), (0,)), ((), ())), preferred_element_type=F32)
        outs[chn] = y
    return jnp.concatenate(outs, axis=0), zt


def f_hgrn_group(q_raw, f_raw, v, lb, zt, *, reverse):
    ys, zs = [], []
    for h in range(HG_PER_STEP):
        sl = slice(h * 128, (h + 1) * 128)
        y, z = f_hgrn(q_raw[:, sl], f_raw[:, sl], v[:, sl], lb[:, sl], zt[sl], reverse=reverse)
        ys.append(y)
        zs.append(z)
    return jnp.concatenate(ys, axis=1), jnp.concatenate(zs, axis=0)


def _expm1(x):
    poly = x * (1.0 + x * (0.5 + x * (1.0 / 6 + x * (1.0 / 24 + x * (1.0 / 120 + x * (1.0 / 720))))))
    return jnp.where(jnp.abs(x) < 0.3, poly, jnp.exp(x) - 1.0)


def f_gates(u, wa, ba, wi, bi, lam):
    rs, is_ = [], []
    for nb in range(8):
        un = u[:, nb * 128:(nb + 1) * 128].astype(BF16)
        rs.append(jnp.dot(un, wa[nb].astype(BF16), preferred_element_type=F32))
        is_.append(jnp.dot(un, wi[nb].astype(BF16), preferred_element_type=F32))
    r = jax.nn.sigmoid(jnp.concatenate(rs, axis=1) + ba)
    i = jax.nn.sigmoid(jnp.concatenate(is_, axis=1) + bi)
    log_a = -LRU_C * jax.nn.softplus(-lam) * r
    return jnp.exp(log_a), jnp.sqrt(-_expm1(2.0 * log_a)) * (i * u)


def f_ssdfin(y0, y1, xs, z, gy, h0, h1, dpad, ng):
    kk = lax.broadcasted_iota(jnp.int32, (128, D), 0)
    ch = lax.broadcasted_iota(jnp.int32, (128, D), 1)
    expand = (ch // 64 == kk).astype(F32)
    dvec = jnp.dot(dpad, expand, precision=HI, preferred_element_type=F32)[0:1]
    y = y0 + y1 + dvec * xs
    yn = _rms(y * jax.nn.silu(z), ng)
    r = (h0 + h1) * jax.nn.gelu(gy)
    return jnp.concatenate([yn, r], axis=1)


def f_oddfin(o0, o1, g, y0, y1, u, hn, sd, gw, gb):
    parts = []
    for h in range(HG_HEADS):
        sl = slice(h * 128, (h + 1) * 128)
        parts.append(_rms(o0[:, sl] + o1[:, sl], hn[h:h + 1]) * jax.nn.silu(g[:, sl]))
    y = jax.nn.gelu(y0 + y1 + sd * u)
    y = y * jax.nn.sigmoid(jnp.dot(y.astype(BF16), gw.astype(BF16), preferred_element_type=F32) + gb)
    return jnp.concatenate(parts + [y], axis=1)


def f_s5p(lre, lim, lstep, btr, bti):
    step = jnp.exp(lstep)
    mag = jnp.exp(lre * step)
    ar, ai = mag * jnp.cos(lim * step), mag * jnp.sin(lim * step)
    den = lre * lre + lim * lim
    zr = ((ar - 1.0) * lre + ai * lim) / den
    zi = (ai * lre - (ar - 1.0) * lim) / den
    bbr = zr[:, None, :] * btr - zi[:, None, :] * bti
    bbi = zr[:, None, :] * bti + zi[:, None, :] * btr
    return ar, ai, bbr, bbi


def f_lb(logits):
    m = jnp.max(logits, axis=0, keepdims=True)
    e = jnp.exp(logits - m)
    p = e / jnp.sum(e, axis=0, keepdims=True)
    return p[1:2], p[1:2] + p[2:3] + p[3:4]


def f_adamw(w, m, v, *gs):
    g = gs[0]
    for t in gs[1:]:
        g = g + t
    m = ADAM_B1 * m + (1.0 - ADAM_B1) * g
    v = ADAM_B2 * v + (1.0 - ADAM_B2) * jnp.square(g)
    m_hat = m / (1.0 - ADAM_B1 ** ADAM_STEP)
    v_hat = v / (1.0 - ADAM_B2 ** ADAM_STEP)
    delta = -ADAM_LR * (m_hat / (jnp.sqrt(v_hat) + ADAM_EPS) + ADAM_WD * w)
    return g, delta, m, v


def scan_fwd(name, f, grid, ins, y_out, st_out, state_shape, is_first):
    n_in = len(ins)

    def body(*refs):
        y_ref, so_ref, st = refs[n_in], refs[n_in + 1], refs[n_in + 2]
        ids = [pl.program_id(a) for a in range(len(grid))]

        @pl.when(is_first(*ids))
        def _():
            st[...] = jnp.zeros_like(st)

        s = st[...]
        so_ref[...] = s
        y, new = f(*[r[...] for r in refs[:n_in]], s)
        y_ref[...] = y.astype(y_ref.dtype)
        st[...] = new

    return pl.pallas_call(
        body, grid=grid,
        in_specs=[pl.BlockSpec(b, m) for _, b, m in ins],
        out_specs=[pl.BlockSpec(y_out[2], y_out[3]), pl.BlockSpec(st_out[2], st_out[3])],
        out_shape=[jax.ShapeDtypeStruct(y_out[0], y_out[1]), jax.ShapeDtypeStruct(st_out[0], st_out[1])],
        scratch_shapes=[pltpu.VMEM(state_shape, F32)],
        name=name, compiler_params=_cparams(("arbitrary",) * len(grid)),
    )(*[a for a, _, _ in ins])


def scan_bwd(name, f, grid, ins, st_in, dy, grads, state_shape, is_first):
    n_in = len(ins)
    didx = [g[0] for g in grads]

    def body(*refs):
        s_ref, dy_ref = refs[n_in], refs[n_in + 1]
        g_refs = refs[n_in + 2:n_in + 2 + len(grads)]
        dst = refs[n_in + 2 + len(grads)]
        ids = [pl.program_id(a) for a in range(len(grid))]

        @pl.when(is_first(*ids))
        def _():
            dst[...] = jnp.zeros_like(dst)

        vals = [r[...] for r in refs[:n_in]]

        def fd(s, *dv):
            full = list(vals)
            for i, v in zip(didx, dv):
                full[i] = v
            return f(*full, s)

        (y, _), vjp = jax.vjp(fd, s_ref[...], *[vals[i] for i in didx])
        gs = vjp((dy_ref[...].astype(y.dtype), dst[...]))
        dst[...] = gs[0]
        for g, spec, ref in zip(gs[1:], grads, g_refs):
            first = spec[5]
            if first is None:
                ref[...] = g.astype(ref.dtype)
            else:
                fst = first(*ids)

                @pl.when(fst)
                def _(ref=ref, g=g):
                    ref[...] = g.astype(ref.dtype)

                @pl.when(jnp.logical_not(fst))
                def _(ref=ref, g=g):
                    ref[...] += g.astype(ref.dtype)

    allin = list(ins) + [st_in, dy]
    return pl.pallas_call(
        body, grid=grid,
        in_specs=[pl.BlockSpec(b, m) for _, b, m in allin],
        out_specs=[pl.BlockSpec(s[3], s[4]) for s in grads],
        out_shape=[jax.ShapeDtypeStruct(s[1], s[2]) for s in grads],
        scratch_shapes=[pltpu.VMEM(state_shape, F32)],
        name=name, compiler_params=_cparams(("arbitrary",) * len(grid)),
    )(*[a for a, _, _ in allin])


def _tile_order(order, nt, nctx=1):
    rev = lambda j: jnp.where(j < nctx, nctx - 1 - j, nt - 1 - (j - nctx))
    if order == 'F':
        return (lambda j: j), True
    if order == 'Fb':
        return (lambda j: nt - 1 - j), False
    if order == 'R':
        return rev, False
    return (lambda j: rev(nt - 1 - j)), True


def _scan8(coef, val, sub, asc):
    for step in (1, 2, 4):
        shift = step if asc else 8 - step
        keep = (sub >= step) if asc else (sub < 8 - step)
        val = jnp.where(keep, coef * pltpu.roll(val, shift, 0) + val, val)
        coef = jnp.where(keep, coef * pltpu.roll(coef, shift, 0), coef)
    return coef, val


def _prev_rows(tile, carry, sub, asc):
    return jnp.where(sub == 0, carry, pltpu.roll(tile, 1, 0)) if asc else jnp.where(sub == 7, carry, pltpu.roll(tile, 7, 0))


def _last_row(tile, asc):
    return jnp.broadcast_to(tile[7:8] if asc else tile[0:1], tile.shape)


def linrec(name, a, b, order):
    bsz, tt, cols = a.shape
    tq = _pick(tt, (256, 128))
    nt, ng, nj = tt // tq, tq // 8, cols // 128
    phys, asc = _tile_order(order, nt)

    def body(a_ref, b_ref, h_ref, hp_ref, hc):
        @pl.when(pl.program_id(0) == 0)
        def _():
            hc[...] = jnp.zeros_like(hc)

        sub = lax.broadcasted_iota(jnp.int32, (8, 128), 0)

        def group(i, carry):
            rows = pl.ds(pl.multiple_of((i if asc else ng - 1 - i) * 8, 8), 8)
            for bi in range(bsz):
                for j in range(nj):
                    cs = slice(j * 128, (j + 1) * 128)
                    h_in = hc[bi, j]
                    ca, cv = _scan8(a_ref[bi, rows, cs], b_ref[bi, rows, cs], sub, asc)
                    h = ca * h_in + cv
                    h_ref[bi, rows, cs] = h
                    hp_ref[bi, rows, cs] = _prev_rows(h, h_in, sub, asc)
                    hc[bi, j] = _last_row(h, asc)
            return carry

        lax.fori_loop(0, ng, group, 0)

    spec = pl.BlockSpec((bsz, tq, cols), lambda j: (0, phys(j), 0))
    return pl.pallas_call(
        body, grid=(nt,), in_specs=[spec, spec], out_specs=[spec, spec],
        out_shape=[jax.ShapeDtypeStruct(a.shape, F32)] * 2,
        scratch_shapes=[pltpu.VMEM((bsz, nj, 8, 128), F32)],
        name=name, compiler_params=_cparams(("arbitrary",)),
    )(a, b)


def linrec_bwd(name, a, dh, hprev, order):
    bsz, tt, cols = a.shape
    tq = _pick(tt, (256, 128))
    nt, ng, nj = tt // tq, tq // 8, cols // 128
    phys, asc = _tile_order(order, nt)

    def body(a_ref, dh_ref, hp_ref, g_ref, ga_ref, gc, ac):
        @pl.when(pl.program_id(0) == 0)
        def _():
            gc[...] = jnp.zeros_like(gc)
            ac[...] = jnp.zeros_like(ac)

        sub = lax.broadcasted_iota(jnp.int32, (8, 128), 0)

        def group(i, carry):
            rows = pl.ds(pl.multiple_of((i if asc else ng - 1 - i) * 8, 8), 8)
            for bi in range(bsz):
                for j in range(nj):
                    cs = slice(j * 128, (j + 1) * 128)
                    a_tile = a_ref[bi, rows, cs]
                    ca, cv = _scan8(_prev_rows(a_tile, ac[bi, j], sub, asc), dh_ref[bi, rows, cs], sub, asc)
                    g = ca * gc[bi, j] + cv
                    g_ref[bi, rows, cs] = g
                    ga_ref[bi, rows, cs] = g * hp_ref[bi, rows, cs]
                    gc[bi, j] = _last_row(g, asc)
                    ac[bi, j] = _last_row(a_tile, asc)
            return carry

        lax.fori_loop(0, ng, group, 0)

    spec = pl.BlockSpec((bsz, tq, cols), lambda j: (0, phys(j), 0))
    return pl.pallas_call(
        body, grid=(nt,), in_specs=[spec, spec, spec], out_specs=[spec, spec],
        out_shape=[jax.ShapeDtypeStruct(a.shape, F32)] * 2,
        scratch_shapes=[pltpu.VMEM((bsz, nj, 8, 128), F32), pltpu.VMEM((bsz, nj, 8, 128), F32)],
        name=name, compiler_params=_cparams(("arbitrary",)),
    )(a, dh, hprev)


def _cmul(a, b):
    return a[0] * b[0] - a[1] * b[1], a[0] * b[1] + a[1] * b[0]


def _cpow_tables(ar, ai, asc):
    pows = [(ar, ai)]
    for _ in range(7):
        pows.append(_cmul(pows[-1], (ar, ai)))
    tile = lambda p: jnp.broadcast_to(p[:, None, :], (8, 8, 128))
    steps = jnp.stack([jnp.stack([tile(pows[s - 1][0]), tile(pows[s - 1][1])]) for s in (1, 2, 4)])
    order = range(8) if asc else range(7, -1, -1)
    carry = jnp.stack([jnp.stack([pows[i][c] for i in order], axis=1) for c in (0, 1)])
    return steps, carry


def _cscan8(xr, xi, st_ref, j, sub, asc):
    for s, step in enumerate((1, 2, 4)):
        shift = step if asc else 8 - step
        keep = (sub >= step) if asc else (sub < 8 - step)
        pr, pi = st_ref[s, 0, j], st_ref[s, 1, j]
        rr, ri = pltpu.roll(xr, shift, 0), pltpu.roll(xi, shift, 0)
        xr, xi = jnp.where(keep, xr + pr * rr - pi * ri, xr), jnp.where(keep, xi + pr * ri + pi * rr, xi)
    return xr, xi


def clinrec(name, x, coef, order, lc, conj=False, hprev=None):
    btot, tt, cols2 = x.shape
    cols = cols2 // 2
    bsz, ngrp = btot, 1
    tq = 128
    nt, ng, nj = tt // tq, tq // 8, cols // 128
    phys, asc = _tile_order(order, nt, lc // tq)
    steps, carry = _cpow_tables(coef[0], -coef[1] if conj else coef[1], asc)
    adjoint = hprev is not None

    def body(*refs):
        if adjoint:
            x_ref, hp_ref, st_ref, cr_ref, h_ref, dc_ref, hc = refs
        else:
            x_ref, st_ref, cr_ref, h_ref, hp_ref, hc = refs

        @pl.when(pl.program_id(1) == 0)
        def _():
            hc[...] = jnp.zeros_like(hc)

        if adjoint:
            @pl.when((pl.program_id(0) == 0) & (pl.program_id(1) == 0))
            def _():
                dc_ref[...] = jnp.zeros_like(dc_ref)

        sub = lax.broadcasted_iota(jnp.int32, (8, 128), 0)

        def group(i, c_):
            rows = pl.ds(pl.multiple_of((i if asc else ng - 1 - i) * 8, 8), 8)
            for bi in range(bsz):
                for j in range(nj):
                    cr, ci = slice(j * 128, (j + 1) * 128), slice(cols + j * 128, cols + (j + 1) * 128)
                    sr, si = _cscan8(x_ref[bi, rows, cr], x_ref[bi, rows, ci], st_ref, j, sub, asc)
                    in_r, in_i = hc[bi, 0, j], hc[bi, 1, j]
                    pr, pi = cr_ref[0, j], cr_ref[1, j]
                    hr = sr + pr * in_r - pi * in_i
                    hi = si + pr * in_i + pi * in_r
                    h_ref[bi, rows, cr] = hr
                    h_ref[bi, rows, ci] = hi
                    if adjoint:
                        qr, qi = hp_ref[bi, rows, cr], hp_ref[bi, rows, ci]
                        dc_ref[0, j] += hr * qr + hi * qi
                        dc_ref[1, j] += hi * qr - hr * qi
                    else:
                        hp_ref[bi, rows, cr] = _prev_rows(hr, in_r, sub, asc)
                        hp_ref[bi, rows, ci] = _prev_rows(hi, in_i, sub, asc)
                    hc[bi, 0, j] = _last_row(hr, asc)
                    hc[bi, 1, j] = _last_row(hi, asc)
            return c_

        lax.fori_loop(0, ng, group, 0)

    spec = pl.BlockSpec((bsz, tq, cols2), lambda g, j: (g, phys(j), 0))
    full = lambda t: pl.BlockSpec(t.shape, lambda g, j, n=t.ndim: (0,) * n)
    dc_shape = (2, nj, 8, 128)
    if adjoint:
        ins, in_specs = (x, hprev, steps, carry), [spec, spec, full(steps), full(carry)]
        out_specs = [spec, pl.BlockSpec(dc_shape, lambda g, j: (0, 0, 0, 0))]
        out_shape = [jax.ShapeDtypeStruct(x.shape, F32), jax.ShapeDtypeStruct(dc_shape, F32)]
    else:
        ins, in_specs = (x, steps, carry), [spec, full(steps), full(carry)]
        out_specs = [spec, spec]
        out_shape = [jax.ShapeDtypeStruct(x.shape, F32)] * 2
    return pl.pallas_call(
        body, grid=(ngrp, nt), in_specs=in_specs, out_specs=out_specs, out_shape=out_shape,
        scratch_shapes=[pltpu.VMEM((bsz, 2, nj, 8, 128), F32)],
        name=name, compiler_params=_cparams(("arbitrary", "arbitrary")),
    )(*ins)


def _blockdiag(bb):
    eye = jnp.eye(16, dtype=bb.dtype)
    return (bb[:, :, None, :] * eye[:, None, :, None]).reshape(256, 1024)


def _blockdiag_t(c):
    eye = jnp.eye(16, dtype=c.dtype)
    return (jnp.swapaxes(c, 1, 2)[:, :, None, :] * eye[:, None, :, None]).reshape(1024, 256)


def _unblockdiag(m):
    eye = jnp.eye(16, dtype=m.dtype)
    return jnp.sum(m.reshape(16, 16, 16, 64) * eye[:, None, :, None], axis=2)


def _unblockdiag_t(m):
    eye = jnp.eye(16, dtype=m.dtype)
    return jnp.swapaxes(jnp.sum(m.reshape(16, 64, 16, 16) * eye[:, None, :, None], axis=2), 1, 2)


def _pad_rows(v, rows=8, cols=128):
    out = jnp.zeros((rows, cols), F32)
    return out.at[0, :v.shape[0]].set(v)


def local_step(x, c, ctx, c_ctx, target, W):
    B, Tx, _ = x.shape
    Lc = ctx.shape[1]
    Tt = Lc + Tx
    tb = Lc
    nt = Tt // tb
    M = B * Tt
    nc = Tt // SSD_CHUNK
    ncc = Lc // SSD_CHUNK
    G = {}

    def add_grad(name, idx, val):
        G.setdefault(name, {})[idx] = val

    def tok(a, cb=None, off=0):
        cb = a.shape[-1] if cb is None else cb
        return (a, (None, tb, cb), lambda b, j, off=off: (b, j, off))

    def tok_out(cols, dtype=F32):
        return ((B, Tt, cols), dtype, (None, tb, cols), lambda b, j: (b, j, 0))

    def vec(a):
        return (a, a.shape, lambda *ids, n=a.ndim: (0,) * n)

    def vec_acc(shape):
        return (shape, F32, shape, lambda *ids, n=len(shape): (0,) * n, lambda *ids: functools.reduce(jnp.logical_and, [i == 0 for i in ids]))

    def modv(l, which):
        return (modr, (None, None, None, 1, D), lambda b, j, l=l, which=which: (l, jnp.where(j == 0, 4, b), which, 0, 0))

    dmod_spec = ((B, 2, 1, D), F32, (None, None, 1, D), lambda b, j: (b, jnp.where(j == 0, 0, 1), 0, 0), lambda b, j: j <= 1)

    def phys_chunk(n_all, n_ctx, reverse):
        if not reverse:
            return lambda s: s
        return lambda s: jnp.where(s < n_ctx, n_ctx - 1 - s, n_all - 1 - (s - n_ctx))

    cc = jnp.zeros((8, D), F32).at[:B].set(c).at[4].set(c_ctx)
    nmc = N_MOD * D // 1536

    def f_mod(ccv, w, b):
        return jnp.dot(jax.nn.silu(ccv).astype(BF16), w, preferred_element_type=F32) + b

    mod = tile_fwd('mod_fwd', f_mod, (DEPTH, nmc),
                   [(cc, (8, D), lambda l, n: (0, 0)), (W['w_mod'], (None, D, 1536), lambda l, n: (l, 0, n)),
                    (W['b_mod'].reshape(DEPTH, 1, N_MOD * D), (None, 1, 1536), lambda l, n: (l, 0, n))],
                   [((DEPTH, 8, N_MOD * D), F32, (None, 8, 1536), lambda l, n: (l, 0, n))])[0]
    modr = mod.reshape(DEPTH, 8, N_MOD, 1, D)
    dmods = {}

    lb1, lb3 = tile_fwd('lb_fwd', f_lb, (1,), [vec(W['hg_lb_logits'])],
                        [((1, HG_W), F32, (1, HG_W), lambda i: (0, 0))] * 2)
    dlb = {1: [], 3: []}

    x0 = jnp.concatenate([ctx, x], axis=1)
    conv_mk = ffnconv_masks(Tt, Lc)
    R = [dict() for _ in range(DEPTH)]

    xprev, oprev = x0, None
    for l in range(DEPTH):
        r = R[l]
        j = l // 2
        ng = W['norm_mix_g'][l][None]
        if l == 0:
            h1 = tile_fwd(f'nm0_fwd', f_nm0, (B, nt), [tok(xprev), vec(ng), modv(l, 0), modv(l, 1)], [tok_out(D, BF16)])[0]
            xa = xprev
        else:
            xa, h1 = tile_fwd(f'nm_mix_fwd{l}', f_nm, (B, nt),
                              [tok(xprev), tok(oprev), modv(l - 1, 5), vec(ng), modv(l, 0), modv(l, 1)],
                              [tok_out(D), tok_out(D, BF16)])
        r['xin'], r['oin'], r['xa'], r['h1'] = xprev, oprev, xa, h1
        h1m = h1.reshape(M, D)
        if l % 2 == 0:
            win = W['ev_w_in_p'][j]
            proj = mm(f'ev_proj{l}', [(h1m, win, 0)]).reshape(B, Tt, EV_PAD)
            r['proj'] = proj
            scw, scb = W['ssd_conv_w'][j], W['ssd_conv_b'][j][None]
            lcw, lcb = W['lru_conv_w'][j], W['lru_conv_b'][j][None]

            def conv_call(name, colblk0, w, b, wblk0, ncols, act):
                return tile_fwd(name, functools.partial(f_conv1d, lc=Lc, act=act), (ncols // 256, B),
                                [(proj, (None, Tt, 256), lambda cb, bi: (bi, 0, colblk0 + cb)),
                                 (w, (4, 256), lambda cb, bi: (0, wblk0 + cb)), (b, (1, 256), lambda cb, bi: (0, wblk0 + cb))],
                                [((B, Tt, ncols), F32, (None, Tt, 256), lambda cb, bi: (bi, 0, cb))])[0]
            xs_c = conv_call(f'conv_xs{l}', 12, scw, scb, 0, 1024, True)
            bc_c = conv_call(f'conv_bc{l}', 16, scw, scb, 4, 512, True)
            u_c = conv_call(f'conv_u{l}', 8, lcw, lcb, 0, 1024, False)
            r['xs'], r['bc'], r['u'] = xs_c, bc_c, u_c
            bias = _pad_rows(W['ssd_dt_bias'][j].reshape(-1), 1)
            alog = _pad_rows(W['ssd_a_log'][j].reshape(-1), 1)
            r['bias'], r['alog'] = bias, alog
            r['y'], r['st'], r['a4'], r['hp4'], r['h'] = [], [], [], [], []
            for d in range(2):
                ph = phys_chunk(nc, ncc, d == 1)
                y, st = scan_fwd(
                    f'ssd_fwd{l}_{d}', functools.partial(f_ssd, d=d, reverse=(d == 1)), (B, nc),
                    [(xs_c, (None, SSD_CHUNK, D), lambda b, s, ph=ph: (b, ph(s), 0)),
                     (bc_c, (None, SSD_CHUNK, 512), lambda b, s, ph=ph: (b, ph(s), 0)),
                     (proj, (None, SSD_CHUNK, 128), lambda b, s, ph=ph: (b, ph(s), 36)),
                     vec(bias), vec(alog)],
                    ((B, Tt, D), F32, (None, SSD_CHUNK, D), lambda b, s, ph=ph: (b, ph(s), 0)),
                    ((B, nc, 8, 128, 128), F32, (None, None, 8, 128, 128), lambda b, s: (b, s, 0, 0, 0)),
                    (8, 128, 128), lambda b, s: s == 0)
                r['y'].append(y)
                r['st'].append(st)
                a_d, bx_d = tile_fwd(
                    f'gates_fwd{l}_{d}', f_gates, (B, nt),
                    [tok(u_c), vec(W['lru_w_a'][j, d]), vec(W['lru_b_a'][j, d][None]), vec(W['lru_w_i'][j, d]),
                     vec(W['lru_b_i'][j, d][None]), vec(W['lru_lam'][j, d][None])],
                    [tok_out(D), tok_out(D)])
                h_d, hp_d = linrec(f'lru_fwd{l}_{d}', a_d, bx_d, 'F' if d == 0 else 'R')
                r['a4'].append(a_d)
                r['hp4'].append(hp_d)
                r['h'].append(h_d)
            dpad = _pad_rows(W['ssd_d'][j])
            sng = W['ssd_norm_g'][j][None]
            r['dpad'], r['sng'] = dpad, sng
            mix = tile_fwd(f'ssdfin_fwd{l}', f_ssdfin, (B, nt),
                           [tok(r['y'][0]), tok(r['y'][1]), tok(xs_c), tok(proj, D, 0), tok(proj, D, 1), tok(r['h'][0]),
                            tok(r['h'][1]), vec(dpad), vec(sng)], [tok_out(2 * D, BF16)])[0]
            wout = W['ev_w_out'][j]
        else:
            win = W['od_w_in'][j]
            proj = mm(f'od_proj{l}', [(h1m, win, 0)]).reshape(B, Tt, 4096)
            r['proj'] = proj
            lbv = lb1 if l == 1 else lb3
            ns = nc
            r['o'], r['zst'], r['coef'], r['bcat'], r['ccat'], r['hp5'], r['hcat'], r['yd'], r['s5in'] = [], [], [], [], [], [], [], [], []
            u2 = proj[:, :, 3840:].reshape(M, S5_W)
            r['u2'] = u2
            for d in range(2):
                ph = phys_chunk(ns, ncc, d == 1)
                o_d, zst = scan_fwd(
                    f'hgrn_fwd{l}_{d}', functools.partial(f_hgrn_group, reverse=(d == 1)), (HG_GROUPS, B, ns),
                    [(proj, (None, 128, HG_GW), lambda h, b, s, ph=ph: (b, ph(s), h)),
                     (proj, (None, 128, HG_GW), lambda h, b, s, ph=ph, d=d: (b, ph(s), (1 + d) * HG_GROUPS + h)),
                     (proj, (None, 128, HG_GW), lambda h, b, s, ph=ph: (b, ph(s), 3 * HG_GROUPS + h)),
                     (lbv, (1, HG_GW), lambda h, b, s: (0, h))],
                    ((B, Tt, HG_W), F32, (None, 128, HG_GW), lambda h, b, s, ph=ph: (b, ph(s), h)),
                    ((B, HG_GROUPS, ns, HG_GW, 128), F32, (None, None, None, HG_GW, 128), lambda h, b, s: (b, h, s, 0, 0)),
                    (HG_GW, 128), lambda h, b, s: s == 0)
                r['o'].append(o_d)
                r['zst'].append(zst)
                s5in = [W['s5_lam_re'][j, d], W['s5_lam_im'][j, d], W['s5_log_step'][j, d].reshape(16, 1),
                        jnp.swapaxes(W['s5_b_re'][j], 1, 2), jnp.swapaxes(W['s5_b_im'][j], 1, 2)]
                r['s5in'].append(s5in)
                ar, ai, bbr, bbi = tile_fwd(f's5p_fwd{l}_{d}', f_s5p, (1,), [vec(t) for t in s5in],
                                            [((16, 64), F32, (16, 64), lambda i: (0, 0))] * 2
                                            + [((16, 16, 64), F32, (16, 16, 64), lambda i: (0, 0, 0))] * 2)
                coef = jnp.stack([ar.reshape(8, 128), ai.reshape(8, 128)])
                bcat = jnp.concatenate([_blockdiag(bbr), _blockdiag(bbi)], axis=1).astype(BF16)
                ccat = jnp.concatenate([_blockdiag_t(W['s5_c_re'][j, d]), -_blockdiag_t(W['s5_c_im'][j, d])], axis=0).astype(BF16)
                xcat = mm(f's5_in{l}_{d}', [(u2, bcat, 0)])
                h5, hp5 = clinrec(f's5_fwd{l}_{d}', xcat.reshape(B, Tt, 2 * D), coef, 'F' if d == 0 else 'R', Lc)
                hcat = h5.reshape(M, 2 * D)
                yd = mm(f's5_out{l}_{d}', [(hcat, ccat, 0)]).reshape(B, Tt, S5_W)
                r['coef'].append(coef)
                r['bcat'].append(bcat)
                r['ccat'].append(ccat)
                r['hp5'].append(hp5)
                r['hcat'].append(hcat)
                r['yd'].append(yd)
            hn = jnp.zeros((8, 128), F32).at[:HG_HEADS].set(W['hg_norm_g'][j])
            sd, gw, gb = W['s5_d'][j][None], W['s5_glu_w'][j], W['s5_glu_b'][j][None]
            r['fin_par'] = (hn, sd, gw, gb)
            mix = tile_fwd(f'oddfin_fwd{l}', f_oddfin, (B, nt),
                           [tok(r['o'][0]), tok(r['o'][1]), tok(proj, HG_W, 4), tok(r['yd'][0]), tok(r['yd'][1]),
                            tok(proj, S5_W, 15), vec(hn), vec(sd), vec(gw), vec(gb)], [tok_out(D, BF16)])[0]
            wout = W['od_w_out'][j]
        r['mix'] = mix
        o1 = mm(f'mix_out{l}', [(mix.reshape(M, -1), wout, 0)]).reshape(B, Tt, D)
        r['o1'] = o1
        fg = W['norm_ffn_g'][l][None]
        xb, h2 = tile_fwd(f'nm_ffn_fwd{l}', f_nm, (B, nt), [tok(xa), tok(o1), modv(l, 2), vec(fg), modv(l, 3), modv(l, 4)],
                          [tok_out(D), tok_out(D, BF16)])
        r['h2'] = h2
        h2m = h2.reshape(M, D)
        a = mm(f'ffn_gate{l}', [(h2m, W['ffn_w_gate'][l], 0)]).reshape(B, Tt, D_FF)
        up = mm(f'ffn_up{l}', [(h2m, W['ffn_w_up'][l], 0)]).reshape(B, Tt, D_FF)
        w9 = W['ffn_conv_w'][l].reshape(9, D_FF)
        cbias = W['ffn_conv_b'][l][None]
        r['a'], r['up'], r['w9'], r['cbias'] = a, up, w9, cbias
        act = tile_fwd(f'ffnconv_fwd{l}', f_ffnconv, (D_FF // 128, B),
                       [(a, (None, Tt, 128), lambda cb, bi: (bi, 0, cb)), (up, (None, Tt, 128), lambda cb, bi: (bi, 0, cb)),
                        (w9, (9, 128), lambda cb, bi: (0, cb)), (cbias, (1, 128), lambda cb, bi: (0, cb)), vec(conv_mk)],
                       [((B, Tt, D_FF), BF16, (None, Tt, 128), lambda cb, bi: (bi, 0, cb))])[0]
        r['act'] = act
        o2 = mm(f'ffn_down{l}', [(act.reshape(M, D_FF), W['ffn_w_down'][l], 0)]).reshape(B, Tt, D)
        xprev, oprev = xb, o2

    vmask = jnp.ones((nt, 1, D), F32).at[0].set(0.0)
    ones = jnp.ones((1, D), F32)
    fng = W['final_norm_g'][None]
    d_xp, d_o2, dg5, dfng, loss_vec = tile_bwd(
        'loss_head', f_final, (B, nt),
        [tok(xprev), tok(oprev), modv(DEPTH - 1, 5), vec(fng),
         (target, (None, tb, D), lambda b, j: (b, jnp.maximum(j - 1, 0), 0)), (vmask, (None, 1, D), lambda b, j: (j, 0, 0))],
        [[vec(ones)]],
        [(0,) + tok_out(D) + (None,), (1,) + tok_out(D, BF16) + (None,), (2,) + dmod_spec, (3,) + vec_acc((1, D))],
        prims=[(0,) + vec_acc((1, D))])
    loss = jnp.sum(loss_vec)
    add_grad('final_norm_g', None, dfng[0])
    dmods[(DEPTH - 1, 5)] = dg5

    for l in reversed(range(DEPTH)):
        r = R[l]
        j = l // 2
        d_o2m = d_o2.reshape(M, D)
        d_act = mm(f'ffn_down_dx{l}', [(d_o2m, W['ffn_w_down'][l], 0)], tb=True).reshape(B, Tt, D_FF)
        add_grad('ffn_w_down', l, mm(f'ffn_down_dw{l}', [(r['act'].reshape(M, D_FF), d_o2m, 0)], ta=True))
        d_a, d_up, dw9, dcb = tile_bwd(
            f'ffnconv_bwd{l}', f_ffnconv, (D_FF // 128, B),
            [(r['a'], (None, Tt, 128), lambda cb, bi: (bi, 0, cb)), (r['up'], (None, Tt, 128), lambda cb, bi: (bi, 0, cb)),
             (r['w9'], (9, 128), lambda cb, bi: (0, cb)), (r['cbias'], (1, 128), lambda cb, bi: (0, cb)), vec(conv_mk)],
            [[(d_act, (None, Tt, 128), lambda cb, bi: (bi, 0, cb))]],
            [(0, (B, Tt, D_FF), BF16, (None, Tt, 128), lambda cb, bi: (bi, 0, cb), None),
             (1, (B, Tt, D_FF), BF16, (None, Tt, 128), lambda cb, bi: (bi, 0, cb), None),
             (2, (9, D_FF), F32, (9, 128), lambda cb, bi: (0, cb), lambda cb, bi: bi == 0),
             (3, (1, D_FF), F32, (1, 128), lambda cb, bi: (0, cb), lambda cb, bi: bi == 0)])
        add_grad('ffn_conv_w', l, dw9.reshape(3, 3, D_FF))
        add_grad('ffn_conv_b', l, dcb[0])
        d_am, d_upm = d_a.reshape(M, D_FF), d_up.reshape(M, D_FF)
        h2m = r['h2'].reshape(M, D)
        d_h2 = mm(f'ffn_in_dx{l}', [(d_am, W['ffn_w_gate'][l], 0), (d_upm, W['ffn_w_up'][l], 0)], tb=True).reshape(B, Tt, D)
        add_grad('ffn_w_gate', l, mm(f'ffn_gate_dw{l}', [(h2m, d_am, 0)], ta=True))
        add_grad('ffn_w_up', l, mm(f'ffn_up_dw{l}', [(h2m, d_upm, 0)], ta=True))
        fg = W['norm_ffn_g'][l][None]
        d_xa, d_o1, dgate, dfg, dsh, dsc = tile_bwd(
            f'nm_ffn_bwd{l}', f_nm, (B, nt), [tok(r['xa']), tok(r['o1']), modv(l, 2), vec(fg), modv(l, 3), modv(l, 4)],
            [[tok(d_xp)], [tok(d_h2)]],
            [(0,) + tok_out(D) + (None,), (1,) + tok_out(D, BF16) + (None,), (2,) + dmod_spec, (3,) + vec_acc((1, D)),
             (4,) + dmod_spec, (5,) + dmod_spec])
        add_grad('norm_ffn_g', l, dfg[0])
        dmods[(l, 2)], dmods[(l, 3)], dmods[(l, 4)] = dgate, dsh, dsc
        d_o1m = d_o1.reshape(M, D)
        h1m = r['h1'].reshape(M, D)
        proj = r['proj']
        if l % 2 == 0:
            wout, win = W['ev_w_out'][j], W['ev_w_in_p'][j]
            d_mix = mm(f'mix_out_dx{l}', [(d_o1m, wout, 0)], tb=True).reshape(B, Tt, 2 * D)
            add_grad('ev_w_out', j, mm(f'mix_out_dw{l}', [(r['mix'].reshape(M, 2 * D), d_o1m, 0)], ta=True))
            d_y, d_xs_fin, d_z, d_gy, d_h, ddpad, dsng = tile_bwd(
                f'ssdfin_bwd{l}', f_ssdfin, (B, nt),
                [tok(r['y'][0]), tok(r['y'][1]), tok(r['xs']), tok(proj, D, 0), tok(proj, D, 1), tok(r['h'][0]), tok(r['h'][1]),
                 vec(r['dpad']), vec(r['sng'])],
                [[tok(d_mix)]],
                [(0,) + tok_out(D) + (None,), (2,) + tok_out(D) + (None,), (3,) + tok_out(D) + (None,), (4,) + tok_out(D) + (None,),
                 (5,) + tok_out(D) + (None,), (7,) + vec_acc((8, 128)), (8,) + vec_acc((1, D))])
            add_grad('ssd_d', j, ddpad[0, :SSD_HEADS])
            add_grad('ssd_norm_g', j, dsng[0])
            d_xs_parts, d_bc_parts, d_dt_parts, d_u_parts = [d_xs_fin], [], [], []
            dbias_t, dalog_t = [], []
            dh4 = d_h
            for d in range(2):
                ph0 = phys_chunk(nc, ncc, d == 1)

                def ph(s, ph0=ph0):
                    return ph0(nc - 1 - s)
                dxs_d, dbc_d, ddt_d, dbias, dalog = scan_bwd(
                    f'ssd_bwd{l}_{d}', functools.partial(f_ssd, d=d, reverse=(d == 1)), (B, nc),
                    [(r['xs'], (None, SSD_CHUNK, D), lambda b, s, ph=ph: (b, ph(s), 0)),
                     (r['bc'], (None, SSD_CHUNK, 512), lambda b, s, ph=ph: (b, ph(s), 0)),
                     (proj, (None, SSD_CHUNK, 128), lambda b, s, ph=ph: (b, ph(s), 36)),
                     vec(r['bias']), vec(r['alog'])],
                    (r['st'][d], (None, None, 8, 128, 128), lambda b, s: (b, nc - 1 - s, 0, 0, 0)),
                    (d_y, (None, SSD_CHUNK, D), lambda b, s, ph=ph: (b, ph(s), 0)),
                    [(0, (B, Tt, D), F32, (None, SSD_CHUNK, D), lambda b, s, ph=ph: (b, ph(s), 0), None),
                     (1, (B, Tt, 512), F32, (None, SSD_CHUNK, 512), lambda b, s, ph=ph: (b, ph(s), 0), None),
                     (2, (B, Tt, 128), F32, (None, SSD_CHUNK, 128), lambda b, s, ph=ph: (b, ph(s), 0), None),
                     (3,) + vec_acc((1, 128)), (4,) + vec_acc((1, 128))],
                    (8, 128, 128), lambda b, s: s == 0)
                d_xs_parts.append(dxs_d)
                d_bc_parts.append(dbc_d)
                d_dt_parts.append(ddt_d)
                dbias_t.append(dbias)
                dalog_t.append(dalog)
                g4, ga4 = linrec_bwd(f'lru_bwd{l}_{d}', r['a4'][d], dh4, r['hp4'][d], 'Fb' if d == 0 else 'Rb')
                du_g, dwa, dba, dwi, dbi, dlam = tile_bwd(
                    f'gates_bwd{l}_{d}', f_gates, (B, nt),
                    [tok(r['u']), vec(W['lru_w_a'][j, d]), vec(W['lru_b_a'][j, d][None]), vec(W['lru_w_i'][j, d]),
                     vec(W['lru_b_i'][j, d][None]), vec(W['lru_lam'][j, d][None])],
                    [[tok(ga4)], [tok(g4)]],
                    [(0,) + tok_out(D) + (None,), (1,) + vec_acc((8, 128, 128)), (2,) + vec_acc((1, D)), (3,) + vec_acc((8, 128, 128)),
                     (4,) + vec_acc((1, D)), (5,) + vec_acc((1, D))])
                d_u_parts.append(du_g)
                add_grad('lru_w_a', (j, d), dwa)
                add_grad('lru_b_a', (j, d), dba[0])
                add_grad('lru_w_i', (j, d), dwi)
                add_grad('lru_b_i', (j, d), dbi[0])
                add_grad('lru_lam', (j, d), dlam[0])
            add_grad('ssd_dt_bias', j, (dbias_t[0] + dbias_t[1])[0, :32].reshape(2, SSD_HEADS))
            add_grad('ssd_a_log', j, (dalog_t[0] + dalog_t[1])[0, :32].reshape(2, SSD_HEADS))
            scw, scb = W['ssd_conv_w'][j], W['ssd_conv_b'][j][None]
            lcw, lcb = W['lru_conv_w'][j], W['lru_conv_b'][j][None]

            def conv_bwd(name, colblk0, w, b, wblk0, ncols, act, parts):
                return tile_bwd(
                    name, functools.partial(f_conv1d, lc=Lc, act=act), (ncols // 256, B),
                    [(proj, (None, Tt, 256), lambda cb, bi: (bi, 0, colblk0 + cb)),
                     (w, (4, 256), lambda cb, bi: (0, wblk0 + cb)), (b, (1, 256), lambda cb, bi: (0, wblk0 + cb))],
                    [[(p, (None, Tt, 256), lambda cb, bi: (bi, 0, cb)) for p in parts]],
                    [(0, (B, Tt, ncols), F32, (None, Tt, 256), lambda cb, bi: (bi, 0, cb), None),
                     (1, (4, ncols), F32, (4, 256), lambda cb, bi: (0, cb), lambda cb, bi: bi == 0),
                     (2, (1, ncols), F32, (1, 256), lambda cb, bi: (0, cb), lambda cb, bi: bi == 0)])
            d_xs_raw, dw_xs, db_xs = conv_bwd(f'conv_xs_bwd{l}', 12, scw, scb, 0, 1024, True, d_xs_parts)
            d_bc_raw, dw_bc, db_bc = conv_bwd(f'conv_bc_bwd{l}', 16, scw, scb, 4, 512, True, d_bc_parts)
            d_u_raw, dw_u, db_u = conv_bwd(f'conv_u_bwd{l}', 8, lcw, lcb, 0, 1024, False, d_u_parts)
            add_grad('ssd_conv_w', j, jnp.concatenate([dw_xs, dw_bc], axis=1))
            add_grad('ssd_conv_b', j, jnp.concatenate([db_xs, db_bc], axis=1)[0])
            add_grad('lru_conv_w', j, dw_u)
            add_grad('lru_conv_b', j, db_u[0])
            def f_ev_dproj(z_, gy_, u_, xs_, bc_, t0, t1):
                pad = jnp.zeros((z_.shape[0], EV_PAD - 4736), F32)
                return jnp.concatenate([z_, gy_, u_, xs_, bc_, t0 + t1, pad], axis=1)
            dproj = tile_fwd(f'ev_dproj{l}', f_ev_dproj, (B, nt),
                             [tok(d_z), tok(d_gy), tok(d_u_raw), tok(d_xs_raw), tok(d_bc_raw), tok(d_dt_parts[0]), tok(d_dt_parts[1])],
                             [tok_out(EV_PAD, BF16)])[0].reshape(M, EV_PAD)
            d_h1 = mm(f'ev_proj_dx{l}', [(dproj, win, 0)], tb=True).reshape(B, Tt, D)
            dwp = mm(f'ev_proj_dw{l}', [(h1m, dproj, 0)], ta=True)
            add_grad('ev_w_in', j, jnp.concatenate([dwp[:, 0:1024], dwp[:, 3072:4640], dwp[:, 1024:3072]], axis=1))
        else:
            wout, win = W['od_w_out'][j], W['od_w_in'][j]
            d_mix = mm(f'mix_out_dx{l}', [(d_o1m, wout, 0)], tb=True).reshape(B, Tt, D)
            add_grad('od_w_out', j, mm(f'mix_out_dw{l}', [(r['mix'].reshape(M, D), d_o1m, 0)], ta=True))
            hn, sd, gw, gb = r['fin_par']
            d_o, d_g, d_yv, d_u_fin, dhn, dsd, dgw, dgb = tile_bwd(
                f'oddfin_bwd{l}', f_oddfin, (B, nt),
                [tok(r['o'][0]), tok(r['o'][1]), tok(proj, HG_W, 4), tok(r['yd'][0]), tok(r['yd'][1]), tok(proj, S5_W, 15),
                 vec(hn), vec(sd), vec(gw), vec(gb)],
                [[tok(d_mix)]],
                [(0,) + tok_out(HG_W) + (None,), (2,) + tok_out(HG_W) + (None,), (3,) + tok_out(S5_W) + (None,),
                 (5,) + tok_out(S5_W) + (None,), (6,) + vec_acc((8, 128)), (7,) + vec_acc((1, S5_W)), (8,) + vec_acc((S5_W, S5_W)),
                 (9,) + vec_acc((1, S5_W))])
            add_grad('hg_norm_g', j, dhn[:HG_HEADS])
            add_grad('s5_d', j, dsd[0])
            add_grad('s5_glu_w', j, dgw)
            add_grad('s5_glu_b', j, dgb[0])
            lbv = lb1 if l == 1 else lb3
            ns = nc
            dq, df, dv, du_s5 = [], [], [], []
            d_ym = d_yv.reshape(M, S5_W)
            dbt_re, dbt_im = [], []
            for d in range(2):
                ph0 = phys_chunk(ns, ncc, d == 1)

                def ph(s, ph0=ph0):
                    return ph0(ns - 1 - s)
                dq_d, df_d, dv_d, dlb_d = scan_bwd(
                    f'hgrn_bwd{l}_{d}', functools.partial(f_hgrn_group, reverse=(d == 1)), (HG_GROUPS, B, ns),
                    [(proj, (None, 128, HG_GW), lambda h, b, s, ph=ph: (b, ph(s), h)),
                     (proj, (None, 128, HG_GW), lambda h, b, s, ph=ph, d=d: (b, ph(s), (1 + d) * HG_GROUPS + h)),
                     (proj, (None, 128, HG_GW), lambda h, b, s, ph=ph: (b, ph(s), 3 * HG_GROUPS + h)),
                     (lbv, (1, HG_GW), lambda h, b, s: (0, h))],
                    (r['zst'][d], (None, None, None, HG_GW, 128), lambda h, b, s: (b, h, ns - 1 - s, 0, 0)),
                    (d_o, (None, 128, HG_GW), lambda h, b, s, ph=ph: (b, ph(s), h)),
                    [(0, (B, Tt, HG_W), F32, (None, 128, HG_GW), lambda h, b, s, ph=ph: (b, ph(s), h), None),
                     (1, (B, Tt, HG_W), F32, (None, 128, HG_GW), lambda h, b, s, ph=ph: (b, ph(s), h), None),
                     (2, (B, Tt, HG_W), F32, (None, 128, HG_GW), lambda h, b, s, ph=ph: (b, ph(s), h), None),
                     (3, (1, HG_W), F32, (1, HG_GW), lambda h, b, s: (0, h), lambda h, b, s: (b == 0) & (s == 0))],
                    (HG_GW, 128), lambda h, b, s: s == 0)
                dq.append(dq_d)
                df.append(df_d)
                dv.append(dv_d)
                dlb[l].append(dlb_d)
                d_hcat = mm(f's5_out_dx{l}_{d}', [(d_ym, r['ccat'][d], 0)], tb=True)
                dccat = mm(f's5_out_dw{l}_{d}', [(r['hcat'][d], d_ym, 0)], ta=True)
                add_grad('s5_c_re', (j, d), _unblockdiag_t(dccat[:D]))
                add_grad('s5_c_im', (j, d), -_unblockdiag_t(dccat[D:]))
                g5, dcoef8 = clinrec(f's5_bwd{l}_{d}', d_hcat.reshape(B, Tt, 2 * D), r['coef'][d], 'Fb' if d == 0 else 'Rb',
                                     Lc, conj=True, hprev=r['hp5'][d])
                dcoef = jnp.sum(dcoef8, axis=2)
                gcat = g5.reshape(M, 2 * D)
                dbcat = mm(f's5_in_dw{l}_{d}', [(r['u2'], gcat, 0)], ta=True)
                du_s5.append(mm(f's5_in_dx{l}_{d}', [(gcat, r['bcat'][d], 0)], tb=True))
                cts5 = [dcoef[0].reshape(16, 64), dcoef[1].reshape(16, 64), _unblockdiag(dbcat[:, :D]), _unblockdiag(dbcat[:, D:])]
                dlre, dlim, dlst, dbtr, dbti = tile_bwd(
                    f's5p_bwd{l}_{d}', f_s5p, (1,), [vec(t) for t in r['s5in'][d]], [[vec(t)] for t in cts5],
                    [(i, t.shape, F32, t.shape, (lambda *ids, n=t.ndim: (0,) * n), None) for i, t in enumerate(r['s5in'][d])])
                add_grad('s5_lam_re', (j, d), dlre)
                add_grad('s5_lam_im', (j, d), dlim)
                add_grad('s5_log_step', (j, d), dlst[:, 0])
                dbt_re.append(dbtr)
                dbt_im.append(dbti)
            add_grad('s5_b_re', j, jnp.swapaxes(dbt_re[0] + dbt_re[1], 1, 2))
            add_grad('s5_b_im', j, jnp.swapaxes(dbt_im[0] + dbt_im[1], 1, 2))
            def f_od_dproj(q0, q1, f0, f1, v0, v1, g_, u0, u1, u2):
                return jnp.concatenate([q0 + q1, f0, f1, v0 + v1, g_, u0 + u1 + u2], axis=1)
            parts = [dq[0], dq[1], df[0], df[1], dv[0], dv[1], d_g, d_u_fin, du_s5[0].reshape(B, Tt, S5_W),
                     du_s5[1].reshape(B, Tt, S5_W)]
            dproj = tile_fwd(f'od_dproj{l}', f_od_dproj, (B, nt), [tok(t) for t in parts],
                             [tok_out(4096, BF16)])[0].reshape(M, 4096)
            d_h1 = mm(f'od_proj_dx{l}', [(dproj, win, 0)], tb=True).reshape(B, Tt, D)
            add_grad('od_w_in', j, mm(f'od_proj_dw{l}', [(h1m, dproj, 0)], ta=True))
        ng = W['norm_mix_g'][l][None]
        if l == 0:
            d_x0, dng, dsh, dsc = tile_bwd(
                'nm0_bwd', lambda xv, g, sh, sc: (xv, f_nm0(xv, g, sh, sc)), (B, nt),
                [tok(r['xin']), vec(ng), modv(l, 0), modv(l, 1)], [[tok(d_xa)], [tok(d_h1)]],
                [(0,) + tok_out(D) + (None,), (1,) + vec_acc((1, D)), (2,) + dmod_spec, (3,) + dmod_spec])
        else:
            d_xp, d_o2, dgate, dng, dsh, dsc = tile_bwd(
                f'nm_mix_bwd{l}', f_nm, (B, nt),
                [tok(r['xin']), tok(r['oin']), modv(l - 1, 5), vec(ng), modv(l, 0), modv(l, 1)],
                [[tok(d_xa)], [tok(d_h1)]],
                [(0,) + tok_out(D) + (None,), (1,) + tok_out(D, BF16) + (None,), (2,) + dmod_spec, (3,) + vec_acc((1, D)),
                 (4,) + dmod_spec, (5,) + dmod_spec])
            dmods[(l - 1, 5)] = dgate
        add_grad('norm_mix_g', l, dng[0])
        dmods[(l, 0)], dmods[(l, 1)] = dsh, dsc

    grad_x = d_x0[:, Lc:, :]

    (dlogits,) = tile_bwd('lb_bwd', f_lb, (1,), [vec(W['hg_lb_logits'])],
                          [[vec(t) for t in dlb[1]], [vec(t) for t in dlb[3]]],
                          [(0, (DEPTH, HG_W), F32, (DEPTH, HG_W), lambda i: (0, 0), None)])
    add_grad('hg_lb_logits', None, dlogits)

    dm = jnp.stack([jnp.stack([dmods[(l, w)] for w in range(N_MOD)]) for l in range(DEPTH)])
    dlat = jnp.transpose(dm[:, :, :, 1, 0, :], (0, 2, 1, 3)).reshape(DEPTH, B, N_MOD * D)
    dctx = jnp.transpose(dm[:, :, :, 0, 0, :], (0, 2, 1, 3)).reshape(DEPTH, B, N_MOD * D)
    dlat = jnp.zeros((DEPTH, 8, N_MOD * D), F32).at[:, :B].set(dlat)
    dctx = jnp.zeros((DEPTH, 8, N_MOD * D), F32).at[:, :B].set(dctx)

    def mod_bwd_body(cc_ref, w_ref, dl_ref, dc_ref, dw_ref, db_ref, dcc_ref):
        row = lax.broadcasted_iota(jnp.int32, (8, 1), 0)
        dall = dl_ref[...] + jnp.where(row == 4, jnp.sum(dc_ref[...], axis=0, keepdims=True), 0.0)
        s, vjp = jax.vjp(jax.nn.silu, cc_ref[...])
        db16 = dall.astype(BF16)
        dw_ref[...] = lax.dot_general(s.astype(BF16), db16, (((0,), (0,)), ((), ())), preferred_element_type=F32)
        db_ref[...] = jnp.sum(dall, axis=0, keepdims=True)
        ds = lax.dot_general(db16, w_ref[...], (((1,), (1,)), ((), ())), preferred_element_type=F32)
        (dcc,) = vjp(ds)
        first = (pl.program_id(0) == 0) & (pl.program_id(1) == 0)

        @pl.when(first)
        def _():
            dcc_ref[...] = dcc

        @pl.when(jnp.logical_not(first))
        def _():
            dcc_ref[...] += dcc

    dwmod, dbmod, dcc = pl.pallas_call(
        mod_bwd_body, grid=(DEPTH, nmc),
        in_specs=[pl.BlockSpec((8, D), lambda l, n: (0, 0)), pl.BlockSpec((None, D, 1536), lambda l, n: (l, 0, n)),
                  pl.BlockSpec((None, 8, 1536), lambda l, n: (l, 0, n)), pl.BlockSpec((None, 8, 1536), lambda l, n: (l, 0, n))],
        out_specs=[pl.BlockSpec((None, D, 1536), lambda l, n: (l, 0, n)), pl.BlockSpec((None, 1, 1536), lambda l, n: (l, 0, n)),
                   pl.BlockSpec((8, D), lambda l, n: (0, 0))],
        out_shape=[jax.ShapeDtypeStruct((DEPTH, D, N_MOD * D), F32), jax.ShapeDtypeStruct((DEPTH, 1, N_MOD * D), F32),
                   jax.ShapeDtypeStruct((8, D), F32)],
        name='mod_bwd', compiler_params=_cparams(("arbitrary", "arbitrary")),
    )(cc, W['w_mod'], dlat, dctx)
    add_grad('w_mod', None, dwmod)
    add_grad('b_mod', None, dbmod[:, 0])
    add_grad('c_ctx', None, dcc[4])
    return loss, grad_x, G


def assemble_grads(G, like):
    out = {}
    for name, parts in G.items():
        shape = like[name].shape
        if None in parts:
            g = parts[None]
        elif isinstance(next(iter(parts)), tuple):
            g = jnp.stack([jnp.stack([parts[(j, d)] for d in range(2)]) for j in range(shape[0])])
        else:
            g = jnp.stack([parts[i] for i in range(shape[0])])
        out[name] = g.reshape(shape)
    return out


XY_RELS = ((1, 0, 0), (0, 1, 0), (1, 1, 0))
ALL_RELS = tuple((dx, dy, dc) for dx in (0, 1) for dy in (0, 1) for dc in (0, 1))[1:]


def exchange(name, src, out_shape, sends):
    return exchange_many(name, [(src, out_shape, sends)])[0]


def exchange_many(name, items):
    na = len(items)
    n = sum(len(sends) for _, _, sends in items)

    def body(*refs):
        src_refs, out_refs, send_sems, recv_sems = refs[:na], refs[na:2 * na], refs[2 * na], refs[2 * na + 1]
        me = (lax.axis_index("x"), lax.axis_index("y"), lax.axis_index("c"))
        copies, k = [], 0
        for (_, _, sends), src_ref, out_ref in zip(items, src_refs, out_refs):
            for rel, ssel, dsel in sends:
                tgt = tuple(1 - m if f else m for m, f in zip(me, rel))
                cp = pltpu.make_async_remote_copy(
                    src_ref=src_ref if ssel is None else src_ref.at[ssel(me, tgt)],
                    dst_ref=out_ref if dsel is None else out_ref.at[dsel(me, tgt)],
                    send_sem=send_sems.at[k], recv_sem=recv_sems.at[k], device_id=tgt, device_id_type=MESH)
                cp.start()
                copies.append(cp)
                k += 1
        for cp in copies:
            cp.wait()

    return pl.pallas_call(
        body, out_shape=[jax.ShapeDtypeStruct(shape, src.dtype) for src, shape, _ in items],
        in_specs=[pl.BlockSpec(memory_space=pl.ANY)] * na, out_specs=[pl.BlockSpec(memory_space=pl.ANY)] * na,
        scratch_shapes=[pltpu.SemaphoreType.DMA((n,)), pltpu.SemaphoreType.DMA((n,))],
        name=name,
    )(*[src for src, _, _ in items])


def _xy_index(dev):
    return 2 * dev[0] + dev[1]


def _my_xy():
    return 2 * lax.axis_index("x") + lax.axis_index("y")


def all_gather_xy(name, shard):
    got = exchange(name, shard, (4,) + shard.shape, [(rel, None, lambda me, tgt: _xy_index(me)) for rel in XY_RELS])
    return lax.dynamic_update_index_in_dim(got, shard, _my_xy(), 0)


def reduce_scatter_xy(name, g4):
    got = exchange(name, g4, (3,) + g4.shape[1:],
                   [(rel, (lambda me, tgt: _xy_index(tgt)), (lambda me, tgt, k=k: k)) for k, rel in enumerate(XY_RELS)])
    return got, lax.dynamic_index_in_dim(g4, _my_xy(), 0, keepdims=False)


def sibling_swap(name, v):
    return exchange(name, v, v.shape, [((0, 0, 1), None, None)])


def all_gather_all(name, v):
    got = exchange(name, v, (8,) + v.shape, [(rel, None, lambda me, tgt: 4 * me[0] + 2 * me[1] + me[2]) for rel in ALL_RELS])
    return lax.dynamic_update_index_in_dim(got, v, 2 * _my_xy() + lax.axis_index("c"), 0)


def all_gather_xy_halves(name, shards):
    na = len(shards)

    def body(*refs):
        src_refs, out_refs, send_sems, recv_sems = refs[:na], refs[na:2 * na], refs[2 * na], refs[2 * na + 1]
        x, y, c = lax.axis_index("x"), lax.axis_index("y"), lax.axis_index("c")
        peers = [(1 - x, y), (x, 1 - y), (1 - x, 1 - y)]

        def copy(k, src, dst, to):
            return pltpu.make_async_remote_copy(src_ref=src, dst_ref=dst, send_sem=send_sems.at[k], recv_sem=recv_sems.at[k],
                                                device_id=to, device_id_type=MESH)

        halves = [pl.ds(c * (s.shape[0] // 2), s.shape[0] // 2) for s in shards]
        first = [[copy(6 * i + k, src_refs[i].at[halves[i]], out_refs[i].at[2 * x + y, halves[i]], (px, py, c))
                  for k, (px, py) in enumerate(peers)] for i in range(na)]
        for row in first:
            for cp in row:
                cp.start()
        passed = []
        for i in range(na):
            for k, (px, py) in enumerate(peers):
                first[i][k].wait_recv()
                landed = out_refs[i].at[2 * px + py, halves[i]]
                fw = copy(6 * i + 3 + k, landed, landed, (x, y, 1 - c))
                fw.start()
                passed.append(fw)
        for fw in passed:
            fw.wait_recv()
        for cp in [cp for row in first for cp in row] + passed:
            cp.wait_send()

    got = pl.pallas_call(
        body, out_shape=[jax.ShapeDtypeStruct((4,) + s.shape, s.dtype) for s in shards],
        in_specs=[pl.BlockSpec(memory_space=pl.ANY)] * na, out_specs=[pl.BlockSpec(memory_space=pl.ANY)] * na,
        scratch_shapes=[pltpu.SemaphoreType.DMA((6 * na,)), pltpu.SemaphoreType.DMA((6 * na,))],
        name=name,
    )(*shards)
    return [lax.dynamic_update_index_in_dim(g, s, _my_xy(), 0) for g, s in zip(got, shards)]


def reduce_scatter_xy_many(name, g4s):
    got = exchange_many(name, [(g4, (3,) + g4.shape[1:],
                                [(rel, (lambda me, tgt: _xy_index(tgt)), (lambda me, tgt, k=k: k)) for k, rel in enumerate(XY_RELS)])
                               for g4 in g4s])
    return [(g, lax.dynamic_index_in_dim(g4, _my_xy(), 0, keepdims=False)) for g, g4 in zip(got, g4s)]


def sibling_split(name, g4s):
    halves = [g4.shape[1] // 2 for g4 in g4s]
    got = exchange_many(name, [(g4, (4, h) + g4.shape[2:], [((0, 0, 1), (lambda me, tgt, h=h: (slice(None), pl.ds(tgt[2] * h, h))), None)])
                               for g4, h in zip(g4s, halves)])
    return [(g, lax.dynamic_slice_in_dim(g4, lax.axis_index("c") * h, h, axis=1)) for g, g4, h in zip(got, g4s, halves)]


def sibling_join(name, qs):
    got = exchange_many(name, [(q, (2 * q.shape[0],) + q.shape[1:], [((0, 0, 1), None, lambda me, tgt, h=q.shape[0]: pl.ds(me[2] * h, h))])
                               for q in qs])
    return [lax.dynamic_update_slice_in_dim(g, q, lax.axis_index("c") * q.shape[0], axis=0) for g, q in zip(got, qs)]


def _rows_view(shape):
    cols = shape[-1] if len(shape) else 1
    rows = 1
    for s in shape[:-1]:
        rows *= s
    return rows, cols


def _row_block(rows, cols, n_arrays):
    budget = (24 * 1024 * 1024) // (8 * n_arrays * cols)
    if rows <= max(budget, 16):
        return rows
    br = (min(budget, rows) // 16) * 16
    while br > 16 and rows % br:
        br -= 16
    return br if rows % br == 0 else rows


def sum_slots(name, stacked, extra=(), out_dtype=F32):
    k = stacked.shape[0]
    rows, cols = _rows_view(stacked.shape[1:])
    br = _row_block(rows, cols, k + len(extra) + 1)

    def f(s, *more):
        parts = [s[i].astype(F32) for i in range(k)] + [m.astype(F32) for m in more]
        while len(parts) > 1:
            parts = [parts[i] + parts[i + 1] for i in range(0, len(parts) - 1, 2)] + ([parts[-1]] if len(parts) % 2 else [])
        return parts[0]

    out = tile_fwd(name, f, (rows // br,),
                   [(stacked.reshape(k, rows, cols), (k, br, cols), lambda i: (0, i, 0))]
                   + [(e.reshape(rows, cols), (br, cols), lambda i: (i, 0)) for e in extra],
                   [((rows, cols), out_dtype, (br, cols), lambda i: (i, 0))])[0]
    return out.reshape(stacked.shape[1:])


def adamw(name, w, m, v, gs):
    rows, cols = _rows_view(w.shape)
    br = _row_block(rows, cols, 7 + len(gs))
    spec = lambda a: (a.reshape(rows, cols), (br, cols), lambda i: (i, 0))
    outs = tile_fwd(name, f_adamw, (rows // br,), [spec(t) for t in (w, m, v) + tuple(gs)],
                    [((rows, cols), F32, (br, cols), lambda i: (i, 0))] * 4)
    return [o.reshape(w.shape) for o in outs]


IN_NAMES = ['x', 'c', 'ctx'] + W_NAMES + ['loss_target'] + ['m_' + n for n in W_NAMES] + ['v_' + n for n in W_NAMES]
SMALL_PAD = 128 * 1024


def kernel(x, c, ctx, c_ctx, w_mod, b_mod, norm_mix_g, norm_ffn_g, final_norm_g, ev_w_in, ev_w_out, ssd_conv_w, ssd_conv_b, ssd_dt_bias, ssd_a_log, ssd_d, ssd_norm_g, lru_conv_w, lru_conv_b, lru_w_a, lru_b_a, lru_w_i, lru_b_i, lru_lam, od_w_in, od_w_out, hg_lb_logits, hg_norm_g, s5_lam_re, s5_lam_im, s5_log_step, s5_b_re, s5_b_im, s5_c_re, s5_c_im, s5_d, s5_glu_w, s5_glu_b, ffn_w_gate, ffn_w_up, ffn_conv_w, ffn_conv_b, ffn_w_down, loss_target, m_c_ctx, m_w_mod, m_b_mod, m_norm_mix_g, m_norm_ffn_g, m_final_norm_g, m_ev_w_in, m_ev_w_out, m_ssd_conv_w, m_ssd_conv_b, m_ssd_dt_bias, m_ssd_a_log, m_ssd_d, m_ssd_norm_g, m_lru_conv_w, m_lru_conv_b, m_lru_w_a, m_lru_b_a, m_lru_w_i, m_lru_b_i, m_lru_lam, m_od_w_in, m_od_w_out, m_hg_lb_logits, m_hg_norm_g, m_s5_lam_re, m_s5_lam_im, m_s5_log_step, m_s5_b_re, m_s5_b_im, m_s5_c_re, m_s5_c_im, m_s5_d, m_s5_glu_w, m_s5_glu_b, m_ffn_w_gate, m_ffn_w_up, m_ffn_conv_w, m_ffn_conv_b, m_ffn_w_down, v_c_ctx, v_w_mod, v_b_mod, v_norm_mix_g, v_norm_ffn_g, v_final_norm_g, v_ev_w_in, v_ev_w_out, v_ssd_conv_w, v_ssd_conv_b, v_ssd_dt_bias, v_ssd_a_log, v_ssd_d, v_ssd_norm_g, v_lru_conv_w, v_lru_conv_b, v_lru_w_a, v_lru_b_a, v_lru_w_i, v_lru_b_i, v_lru_lam, v_od_w_in, v_od_w_out, v_hg_lb_logits, v_hg_norm_g, v_s5_lam_re, v_s5_lam_im, v_s5_log_step, v_s5_b_re, v_s5_b_im, v_s5_c_re, v_s5_c_im, v_s5_d, v_s5_glu_w, v_s5_glu_b, v_ffn_w_gate, v_ffn_w_up, v_ffn_conv_w, v_ffn_conv_b, v_ffn_w_down):
    a = dict(locals())
    big = [n for n in W_NAMES if n in MATMUL_WEIGHTS]
    minor = [n for n in W_NAMES if n in SHARD_AXIS and n not in MATMUL_WEIGHTS]

    def pack(arrays, lead=()):
        flat = jnp.concatenate([t.reshape(lead + (-1,)) for t in arrays], axis=len(lead))
        pad = -flat.shape[-1] % 1024
        flat = jnp.concatenate([flat, jnp.zeros(lead + (pad,), flat.dtype)], axis=len(lead))
        return flat.reshape(lead + (-1, 128))

    def unpack(packed, names, lead=()):
        flat, out, off = packed.reshape(lead + (-1,)), {}, 0
        for n in names:
            size = math.prod(a[n].shape)
            out[n] = flat[..., off:off + size].reshape(lead + a[n].shape)
            off += size
        return out

    gathered = dict(zip(big, all_gather_xy_halves('ag_big', [a[n].astype(BF16) for n in big])))
    gathered.update(unpack(all_gather_xy('ag_minor', pack([a[n] for n in minor])), minor, (4,)))
    W = {}
    for n in W_NAMES:
        w = a[n]
        if n in SHARD_AXIS:
            ax = SHARD_AXIS[n]
            shape = list(w.shape)
            shape[ax] *= 4
            W[n] = jnp.moveaxis(gathered[n], 0, ax).reshape(shape)
        else:
            W[n] = w
    e = W['ev_w_in']
    W['ev_w_in_p'] = jnp.concatenate(
        [e[:, :, 0:1024], e[:, :, 2592:3616], e[:, :, 3616:4640], e[:, :, 1024:2560], e[:, :, 2560:2592],
         jnp.zeros((e.shape[0], D, EV_PAD - 4640), e.dtype)], axis=2)

    loss_local, grad_x, G = local_step(a['x'], a['c'], a['ctx'], W['c_ctx'], a['loss_target'], W)
    grads = assemble_grads(G, W)
    loss = lax.psum(loss_local, ("x", "y", "c"))

    res = {}
    g4 = {}
    for n in SHARD_AXIS:
        ax = SHARD_AXIS[n]
        gf = grads[n]
        g4[n] = jnp.moveaxis(gf.reshape(gf.shape[:ax] + (4, a[n].shape[ax]) + gf.shape[ax + 1:]), ax, 0)
    parts = [sum_slots('csum_' + n, theirs[None], (ours,), BF16)
             for n, (theirs, ours) in zip(big, sibling_split('rsc_big', [g4[n] for n in big]))]
    halves = [sum_slots('gsum_' + n, got, (own,)) for n, (got, own) in zip(big, reduce_scatter_xy_many('rs_big', parts))]
    for n, g in zip(big, sibling_join('agc_big', halves)):
        res[n] = adamw('adamw_' + n, a[n], a['m_' + n], a['v_' + n], (g,))
    got, own = reduce_scatter_xy('rs_minor', pack([g4[n] for n in minor], (4,)))
    mine = sum_slots('gsum_minor', got, (own,))
    mine_n, other_n = unpack(mine, minor), unpack(sibling_swap('sw_minor', mine), minor)
    for n in minor:
        res[n] = adamw('adamw_' + n, a[n], a['m_' + n], a['v_' + n], (mine_n[n], other_n[n]))
    small = [n for n in W_NAMES if n not in SHARD_AXIS]
    flat = jnp.concatenate([grads[n].reshape(-1) for n in small])
    total = flat.shape[0]
    padded = -(-total // SMALL_PAD) * SMALL_PAD
    flat = jnp.concatenate([flat, jnp.zeros((padded - total,), F32)]).reshape(padded // 128, 128)
    pair = sum_slots('csum_small', flat[None], (sibling_swap('sw_small', flat),))
    summed = sum_slots('gsum_small', all_gather_xy('ag_small', pair)).reshape(-1)
    off = 0
    for n in small:
        size = math.prod(a[n].shape)
        g = summed[off:off + size].reshape(a[n].shape)
        off += size
        res[n] = adamw('adamw_' + n, a[n], a['m_' + n], a['v_' + n], (g,))
    outs = [loss, grad_x]
    for k in range(4):
        outs += [res[n][k] for n in W_NAMES]
    return tuple(outs)
```

```python
import functools
import math

import jax
import jax.numpy as jnp
from jax import lax
from jax.experimental import pallas as pl
from jax.experimental.pallas import tpu as pltpu

F32 = jnp.float32
BF16 = jnp.bfloat16
HI = lax.Precision.HIGHEST
MESH = pl.DeviceIdType.MESH

D = 1024
DEPTH = 4
N_MOD = 6
RMS_EPS = 1e-6
GRID_W = 64
SSD_HEADS = 16
SSD_CHUNK = 128
HG_W = 768
HG_HEADS = 6
HG_FWD = (32, 6)
HG_BWD = (16, 3)
S5_W = 256
D_FF = 2816
EV_PAD = 5120
LRU_C = 8.0
V7X_VMEM_LIMIT = 56 * 1024 * 1024
MM_VMEM_BUDGET = 36 * 1024 * 1024

ADAM_LR, ADAM_B1, ADAM_B2, ADAM_EPS, ADAM_WD, ADAM_STEP = 0.001, 0.9, 0.999, 1e-08, 0.01, 10

W_NAMES = ['c_ctx', 'w_mod', 'b_mod', 'norm_mix_g', 'norm_ffn_g', 'final_norm_g', 'ev_w_in', 'ev_w_out', 'ssd_conv_w',
           'ssd_conv_b', 'ssd_dt_bias', 'ssd_a_log', 'ssd_d', 'ssd_norm_g', 'lru_conv_w', 'lru_conv_b', 'lru_w_a', 'lru_b_a',
           'lru_w_i', 'lru_b_i', 'lru_lam', 'od_w_in', 'od_w_out', 'hg_lb_logits', 'hg_norm_g', 's5_lam_re', 's5_lam_im',
           's5_log_step', 's5_b_re', 's5_b_im', 's5_c_re', 's5_c_im', 's5_d', 's5_glu_w', 's5_glu_b', 'ffn_w_gate', 'ffn_w_up',
           'ffn_conv_w', 'ffn_conv_b', 'ffn_w_down']
SHARD_AXIS = {'w_mod': 2, 'ev_w_in': 2, 'ev_w_out': 1, 'ssd_conv_w': 2, 'lru_conv_w': 2, 'lru_b_a': 2, 'lru_b_i': 2,
              'lru_lam': 2, 'od_w_in': 2, 'od_w_out': 1, 's5_d': 1, 's5_glu_w': 1, 's5_glu_b': 1, 'ffn_w_gate': 2,
              'ffn_w_up': 2, 'ffn_conv_w': 3, 'ffn_w_down': 1}
MATMUL_WEIGHTS = ('w_mod', 'ev_w_in', 'ev_w_out', 'od_w_in', 'od_w_out', 'ffn_w_gate', 'ffn_w_up', 'ffn_w_down')


def _cparams(sem=None):
    return pltpu.CompilerParams(vmem_limit_bytes=V7X_VMEM_LIMIT, dimension_semantics=sem)


def _pick(n, cands):
    for c in cands:
        if n % c == 0:
            return c
    return n


def tile_fwd(name, f, grid, ins, outs):
    n_in = len(ins)

    def body(*refs):
        res = f(*[r[...] for r in refs[:n_in]])
        if not isinstance(res, (tuple, list)):
            res = (res,)
        for r, o in zip(res, refs[n_in:]):
            o[...] = r.astype(o.dtype)

    res = pl.pallas_call(
        body, grid=grid,
        in_specs=[pl.BlockSpec(b, m) for _, b, m in ins],
        out_specs=[pl.BlockSpec(b, m) for _, _, b, m in outs],
        out_shape=[jax.ShapeDtypeStruct(s, d) for s, d, _, _ in outs],
        name=name, compiler_params=_cparams(("arbitrary",) * len(grid)),
    )(*[a for a, _, _ in ins])
    return res


def tile_bwd(name, f, grid, ins, cts, grads, prims=()):
    n_in = len(ins)
    ct_flat = [p for c in cts for p in c]
    n_ct = len(ct_flat)
    didx = [g[0] for g in grads]

    def body(*refs):
        in_refs, ct_refs = refs[:n_in], refs[n_in:n_in + n_ct]
        g_refs = refs[n_in + n_ct:n_in + n_ct + len(grads)]
        p_refs = refs[n_in + n_ct + len(grads):]
        vals = [r[...] for r in in_refs]

        def fd(*dv):
            full = list(vals)
            for i, v in zip(didx, dv):
                full[i] = v
            res = f(*full)
            return tuple(res) if isinstance(res, (tuple, list)) else (res,)

        out, vjp = jax.vjp(fd, *[vals[i] for i in didx])
        ctv, k = [], 0
        for o, c in zip(out, cts):
            acc = None
            for _ in c:
                piece = ct_refs[k][...].astype(o.dtype)
                acc = piece if acc is None else acc + piece
                k += 1
            ctv.append(jnp.zeros_like(o) if acc is None else acc.reshape(o.shape))
        gs = vjp(tuple(ctv))
        ids = [pl.program_id(a) for a in range(len(grid))]

        def emit(ref, val, first):
            if first is None:
                ref[...] = val.astype(ref.dtype)
            else:
                is_first = first(*ids)

                @pl.when(is_first)
                def _():
                    ref[...] = val.astype(ref.dtype)

                @pl.when(jnp.logical_not(is_first))
                def _():
                    ref[...] += val.astype(ref.dtype)

        for g, spec, ref in zip(gs, grads, g_refs):
            emit(ref, g, spec[5])
        for spec, ref in zip(prims, p_refs):
            emit(ref, out[spec[0]], spec[5])

    specs = list(grads) + list(prims)
    res = pl.pallas_call(
        body, grid=grid,
        in_specs=[pl.BlockSpec(b, m) for _, b, m in list(ins) + ct_flat],
        out_specs=[pl.BlockSpec(s[3], s[4]) for s in specs],
        out_shape=[jax.ShapeDtypeStruct(s[1], s[2]) for s in specs],
        name=name, compiler_params=_cparams(("arbitrary",) * len(grid)),
    )(*[a for a, _, _ in list(ins) + ct_flat])
    return res


def mm(name, pairs, ta=False, tb=False, out_dtype=F32):
    a0, b0, _ = pairs[0]
    m = a0.shape[1] if ta else a0.shape[0]
    n = b0.shape[0] if tb else b0.shape[1]
    cands = (1024, 1408, 768, 512, 256, 128)
    tks, nks = [], []
    for a, b, _ in pairs:
        k = a.shape[0] if ta else a.shape[1]
        tk = _pick(k, cands)
        tks.append(tk)
        nks.append(k // tk)

    def vmem_bytes(tm, tn):
        tiles = sum(2 * tk * (tm * a.dtype.itemsize + tn * b.dtype.itemsize) for (a, b, _), tk in zip(pairs, tks))
        return tiles + tm * tn * (4 + 2 * jnp.dtype(out_dtype).itemsize)

    tm_c = [c_ for c_ in cands if m % c_ == 0] or [m]
    tn_c = [c_ for c_ in cands if n % c_ == 0] or [n]
    tm, tn = tm_c[0], tn_c[0]
    while vmem_bytes(tm, tn) > MM_VMEM_BUDGET and (len(tm_c) > 1 or len(tn_c) > 1):
        if len(tm_c) > 1 and (tm >= tn or len(tn_c) == 1):
            tm_c = tm_c[1:]
        else:
            tn_c = tn_c[1:]
        tm, tn = tm_c[0], tn_c[0]
    starts = [sum(nks[:p]) for p in range(len(pairs))]
    nk = sum(nks)
    np_ = len(pairs)

    def body(*refs):
        o_ref, acc = refs[2 * np_], refs[2 * np_ + 1]
        kk = pl.program_id(2)

        @pl.when(kk == 0)
        def _():
            acc[...] = jnp.zeros_like(acc)

        for p in range(np_):
            def add(p=p):
                a = refs[2 * p][...].astype(BF16)
                b = refs[2 * p + 1][...].astype(BF16)
                dn = (((0 if ta else 1,), (1 if tb else 0,)), ((), ()))
                acc[...] += lax.dot_general(a, b, dn, preferred_element_type=F32)
            if np_ == 1:
                add()
            else:
                pl.when((kk >= starts[p]) & (kk < starts[p] + nks[p]))(add)

        @pl.when(kk == nk - 1)
        def _():
            o_ref[...] = acc[...].astype(o_ref.dtype)

    in_specs, args = [], []
    for p, (a, b, off) in enumerate(pairs):
        tk, s0, nkp = tks[p], starts[p], nks[p]
        assert off % tk == 0
        boff = off // tk

        def kloc(k, s0=s0, nkp=nkp):
            return jnp.clip(k - s0, 0, nkp - 1)
        if ta:
            in_specs.append(pl.BlockSpec((tk, tm), lambda i, j, k, kloc=kloc: (kloc(k), i)))
        else:
            in_specs.append(pl.BlockSpec((tm, tk), lambda i, j, k, kloc=kloc: (i, kloc(k))))
        if tb:
            in_specs.append(pl.BlockSpec((tn, tk), lambda i, j, k, kloc=kloc, boff=boff: (j, boff + kloc(k))))
        else:
            in_specs.append(pl.BlockSpec((tk, tn), lambda i, j, k, kloc=kloc, boff=boff: (boff + kloc(k), j)))
        args += [a, b]
    return pl.pallas_call(
        body, grid=(m // tm, n // tn, nk), in_specs=in_specs,
        out_specs=pl.BlockSpec((tm, tn), lambda i, j, k: (i, j)),
        out_shape=jax.ShapeDtypeStruct((m, n), out_dtype),
        scratch_shapes=[pltpu.VMEM((tm, tn), F32)],
        name=name, compiler_params=_cparams(("arbitrary", "arbitrary", "arbitrary")),
    )(*args)


def _rms(x, g):
    return x * lax.rsqrt(jnp.mean(x * x, axis=-1, keepdims=True) + RMS_EPS) * g


def f_nm0(x, g, sh, sc):
    return _rms(x, g) * (1.0 + sc) + sh


def f_nm(xp, o, gate, g, sh, sc):
    x = xp + gate * o
    return x, _rms(x, g) * (1.0 + sc) + sh


def f_final(xp, o, gate, g, tgt, valid):
    x = xp + gate * o
    e = (_rms(x, g) - tgt) * valid
    return jnp.sum(e * e, axis=0, keepdims=True) * (0.5 / D)


@functools.partial(jax.custom_vjp, nondiff_argnums=(1,))
def _sroll(x, s):
    return pltpu.roll(x, s, 0)


def _sroll_fwd(x, s):
    return pltpu.roll(x, s, 0), None


def _sroll_bwd(s, _, g):
    return (pltpu.roll(g, (g.shape[0] - s) % g.shape[0], 0),)


_sroll.defvjp(_sroll_fwd, _sroll_bwd)


def _shifted(x, o):
    n = x.shape[0]
    return x if o == 0 else _sroll(x, (n - o) % n)


def f_conv1d(x, w, b, *, lc, act):
    n = x.shape[0]
    pos = lax.broadcasted_iota(jnp.int32, (n, 1), 0)
    lo = jnp.where(pos < lc, 0, lc)
    hi = jnp.where(pos < lc, lc, n)
    y = x * w[1:2] + b
    for k, o in ((0, -1), (2, 1), (3, 2)):
        src = pos + o
        valid = (src >= lo) & (src < hi)
        y = y + jnp.where(valid, _shifted(x, o), 0.0) * w[k:k + 1]
    return jax.nn.silu(y) if act else y


def ffnconv_masks(n, lc):
    pos = lax.broadcasted_iota(jnp.int32, (n, 128), 0)
    is_ctx = pos < lc
    tl = pos - lc
    r = tl // GRID_W
    cc = tl - r * GRID_W
    rows = (n - lc) // GRID_W
    left = jnp.where(is_ctx, pos >= 1, cc >= 1)
    right = jnp.where(is_ctx, pos < lc - 1, cc < GRID_W - 1)
    above = jnp.logical_not(is_ctx) & (r >= 1)
    below = jnp.logical_not(is_ctx) & (r < rows - 1)
    return jnp.stack([left, right, above, below]).astype(F32)


def f_ffnconv(a, up, w, b, mk):
    cols = (mk[0] * _shifted(a, -1), a, mk[1] * _shifted(a, 1))
    y = b
    for dr in (-1, 0, 1):
        k = 3 * (dr + 1)
        inner = cols[0] * w[k:k + 1] + cols[1] * w[k + 1:k + 2] + cols[2] * w[k + 2:k + 3]
        y = y + (inner if dr == 0 else mk[2 + (dr > 0)] * _shifted(inner, GRID_W * dr))
    return jax.nn.silu(y) * up


def f_ssd(xs, bc, dtraw, bias, alog, st, *, d, reverse):
    L = xs.shape[0]
    dtv = jax.nn.softplus(dtraw + bias)
    la = dtv * (-jnp.exp(alog))
    ri = lax.broadcasted_iota(jnp.int32, (L, L), 0)
    ci = lax.broadcasted_iota(jnp.int32, (L, L), 1)
    mask = (ci >= ri) if reverse else (ci <= ri)
    cum = jnp.dot(mask.astype(F32), la, precision=HI, preferred_element_type=F32)
    cum_t = cum.T
    tot = cum[0:1] if reverse else cum[L - 1:L]
    lo = lax.broadcasted_iota(jnp.int32, (1, 128), 1) < 64
    rlo = lax.broadcasted_iota(jnp.int32, (128, 1), 0) < 64
    ys, new = [], []
    for g in range(2):
        bg = bc[:, g * 128:(g + 1) * 128].astype(BF16)
        cg = bc[:, 256 + g * 128:256 + (g + 1) * 128].astype(BF16)
        cb = lax.dot_general(cg, bg, (((1,), (1,)), ((), ())), preferred_element_type=F32)
        sg = st[4 * g:4 * g + 4].reshape(4 * 128, 128)
        ch_all = lax.dot_general(cg, sg.astype(BF16), (((1,), (1,)), ((), ())), preferred_element_type=F32)
        xes, dcols = [], []
        for jj in range(4):
            j = 4 * g + jj
            x = xs[:, j * 128:(j + 1) * 128]
            k1 = 16 * d + 2 * j
            k2 = k1 + 1
            c1, c2 = cum[:, k1:k1 + 1], cum[:, k2:k2 + 1]
            m1 = cb * jnp.exp(jnp.where(mask, c1 - cum_t[k1:k1 + 1, :], -1e30))
            m2 = cb * jnp.exp(jnp.where(mask, c2 - cum_t[k2:k2 + 1, :], -1e30))
            xdt = x * jnp.where(lo, dtv[:, k1:k1 + 1], dtv[:, k2:k2 + 1])
            mcat = jnp.concatenate([m1, m2], axis=1).astype(BF16)
            xcat = jnp.concatenate([jnp.where(lo, xdt, 0.0), jnp.where(lo, 0.0, xdt)], axis=0).astype(BF16)
            y = jnp.dot(mcat, xcat, preferred_element_type=F32)
            y = y + ch_all[:, jj * 128:(jj + 1) * 128] * jnp.where(lo, jnp.exp(c1), jnp.exp(c2))
            t1, t2 = tot[:, k1:k1 + 1], tot[:, k2:k2 + 1]
            xes.append((xdt * jnp.where(lo, jnp.exp(t1 - c1), jnp.exp(t2 - c2))).astype(BF16))
            dcols.append(jnp.where(rlo, jnp.exp(t1), jnp.exp(t2)))
            ys.append(y)
        upd = lax.dot_general(jnp.concatenate(xes, axis=1), bg, (((0,), (0,)), ((), ())), preferred_element_type=F32)
        new.append((sg * jnp.concatenate(dcols, axis=0) + upd).reshape(4, 128, 128))
    return jnp.concatenate(ys, axis=1), jnp.concatenate(new, axis=0)


def f_hgrn(q_raw, f_raw, v, lb, zt, *, reverse, chunk):
    n = q_raw.shape[0]
    c = chunk
    qa = jax.nn.silu(q_raw)
    logf = jnp.log(lb + (1.0 - lb) * jax.nn.sigmoid(f_raw))
    kk = (1.0 - lb) * jax.nn.sigmoid(-f_raw)
    ri = lax.broadcasted_iota(jnp.int32, (n, n), 0)
    ci = lax.broadcasted_iota(jnp.int32, (n, n), 1)
    tmat = ((ri // c == ci // c) & ((ci >= ri) if reverse else (ci <= ri))).astype(F32)
    cum_all = jnp.dot(tmat, logf, precision=HI, preferred_element_type=F32)
    r3 = lax.broadcasted_iota(jnp.int32, (c, c, 128), 0)
    c3 = lax.broadcasted_iota(jnp.int32, (c, c, 128), 1)
    mask3 = (c3 >= r3) if reverse else (c3 <= r3)
    nch = n // c
    outs = [None] * nch
    for chn in (reversed(range(nch)) if reverse else range(nch)):
        sl = slice(chn * c, (chn + 1) * c)
        q, k, vv, cum = qa[sl], kk[sl], v[sl], cum_all[sl]
        dec = jnp.exp(jnp.where(mask3, cum[:, None, :] - cum[None, :, :], -1e30))
        att = jnp.sum(q[:, None, :] * dec * k[None, :, :], axis=-1, keepdims=True)
        y = jnp.sum(att * vv[None, :, :], axis=1)
        y = y + lax.dot_general((q * jnp.exp(cum)).astype(BF16), zt.astype(BF16), (((1,), (1,)), ((), ())),
                                preferred_element_type=F32)
        tot = cum[0:1] if reverse else cum[c - 1:c]
        kd = (k * jnp.exp(tot - cum)).astype(BF16)
        zt = zt * jnp.exp(tot) + lax.dot_general(vv.astype(BF16), kd, (((0,), (0,)), ((), ())), preferred_element_type=F32)
        outs[chn] = y
    return jnp.concatenate(outs, axis=0), zt


def f_hgrn_group(q_raw, f_raw, v, lb, zt, *, reverse, chunk):
    ys, zs = [], []
    for h in range(q_raw.shape[1] // 128):
        sl = slice(h * 128, (h + 1) * 128)
        y, z = f_hgrn(q_raw[:, sl], f_raw[:, sl], v[:, sl], lb[:, sl], zt[sl], reverse=reverse, chunk=chunk)
        ys.append(y)
        zs.append(z)
    return jnp.concatenate(ys, axis=1), jnp.concatenate(zs, axis=0)


def _expm1(x):
    poly = x * (1.0 + x * (0.5 + x * (1.0 / 6 + x * (1.0 / 24 + x * (1.0 / 120 + x * (1.0 / 720))))))
    return jnp.where(jnp.abs(x) < 0.3, poly, jnp.exp(x) - 1.0)


def f_gates(u, wa, ba, wi, bi, lam):
    rs, is_ = [], []
    for nb in range(8):
        un = u[:, nb * 128:(nb + 1) * 128].astype(BF16)
        rs.append(jnp.dot(un, wa[nb].astype(BF16), preferred_element_type=F32))
        is_.append(jnp.dot(un, wi[nb].astype(BF16), preferred_element_type=F32))
    r = jax.nn.sigmoid(jnp.concatenate(rs, axis=1) + ba)
    i = jax.nn.sigmoid(jnp.concatenate(is_, axis=1) + bi)
    log_a = -LRU_C * jax.nn.softplus(-lam) * r
    return jnp.exp(log_a), jnp.sqrt(-_expm1(2.0 * log_a)) * (i * u)


def f_ssdfin(y0, y1, xs, z, gy, h0, h1, dpad, ng):
    kk = lax.broadcasted_iota(jnp.int32, (128, D), 0)
    ch = lax.broadcasted_iota(jnp.int32, (128, D), 1)
    expand = (ch // 64 == kk).astype(F32)
    dvec = jnp.dot(dpad, expand, precision=HI, preferred_element_type=F32)[0:1]
    y = y0 + y1 + dvec * xs
    yn = _rms(y * jax.nn.silu(z), ng)
    r = (h0 + h1) * jax.nn.gelu(gy)
    return jnp.concatenate([yn, r], axis=1)


def f_oddfin(o0, o1, g, y0, y1, u, hn, sd, gw, gb):
    parts = []
    for h in range(HG_HEADS):
        sl = slice(h * 128, (h + 1) * 128)
        parts.append(_rms(o0[:, sl] + o1[:, sl], hn[h:h + 1]) * jax.nn.silu(g[:, sl]))
    y = jax.nn.gelu(y0 + y1 + sd * u)
    y = y * jax.nn.sigmoid(jnp.dot(y.astype(BF16), gw.astype(BF16), preferred_element_type=F32) + gb)
    return jnp.concatenate(parts + [y], axis=1)


def f_s5p(lre, lim, lstep, btr, bti):
    step = jnp.exp(lstep)
    mag = jnp.exp(lre * step)
    ar, ai = mag * jnp.cos(lim * step), mag * jnp.sin(lim * step)
    den = lre * lre + lim * lim
    zr = ((ar - 1.0) * lre + ai * lim) / den
    zi = (ai * lre - (ar - 1.0) * lim) / den
    bbr = zr[:, None, :] * btr - zi[:, None, :] * bti
    bbi = zr[:, None, :] * bti + zi[:, None, :] * btr
    return ar, ai, bbr, bbi


def f_lb(logits):
    m = jnp.max(logits, axis=0, keepdims=True)
    e = jnp.exp(logits - m)
    p = e / jnp.sum(e, axis=0, keepdims=True)
    return p[1:2], p[1:2] + p[2:3] + p[3:4]


def f_adamw(w, m, v, *gs):
    g = gs[0]
    for t in gs[1:]:
        g = g + t
    m = ADAM_B1 * m + (1.0 - ADAM_B1) * g
    v = ADAM_B2 * v + (1.0 - ADAM_B2) * jnp.square(g)
    m_hat = m / (1.0 - ADAM_B1 ** ADAM_STEP)
    v_hat = v / (1.0 - ADAM_B2 ** ADAM_STEP)
    delta = -ADAM_LR * (m_hat / (jnp.sqrt(v_hat) + ADAM_EPS) + ADAM_WD * w)
    return g, delta, m, v


def scan_fwd(name, f, grid, ins, y_out, st_out, state_shape, is_first):
    n_in = len(ins)

    def body(*refs):
        y_ref, so_ref, st = refs[n_in], refs[n_in + 1], refs[n_in + 2]
        ids = [pl.program_id(a) for a in range(len(grid))]

        @pl.when(is_first(*ids))
        def _():
            st[...] = jnp.zeros_like(st)

        s = st[...]
        so_ref[...] = s
        y, new = f(*[r[...] for r in refs[:n_in]], s)
        y_ref[...] = y.astype(y_ref.dtype)
        st[...] = new

    return pl.pallas_call(
        body, grid=grid,
        in_specs=[pl.BlockSpec(b, m) for _, b, m in ins],
        out_specs=[pl.BlockSpec(y_out[2], y_out[3]), pl.BlockSpec(st_out[2], st_out[3])],
        out_shape=[jax.ShapeDtypeStruct(y_out[0], y_out[1]), jax.ShapeDtypeStruct(st_out[0], st_out[1])],
        scratch_shapes=[pltpu.VMEM(state_shape, F32)],
        name=name, compiler_params=_cparams(("arbitrary",) * len(grid)),
    )(*[a for a, _, _ in ins])


def scan_bwd(name, f, grid, ins, st_in, dy, grads, state_shape, is_first):
    n_in = len(ins)
    didx = [g[0] for g in grads]

    def body(*refs):
        s_ref, dy_ref = refs[n_in], refs[n_in + 1]
        g_refs = refs[n_in + 2:n_in + 2 + len(grads)]
        dst = refs[n_in + 2 + len(grads)]
        ids = [pl.program_id(a) for a in range(len(grid))]

        @pl.when(is_first(*ids))
        def _():
            dst[...] = jnp.zeros_like(dst)

        vals = [r[...] for r in refs[:n_in]]

        def fd(s, *dv):
            full = list(vals)
            for i, v in zip(didx, dv):
                full[i] = v
            return f(*full, s)

        (y, _), vjp = jax.vjp(fd, s_ref[...], *[vals[i] for i in didx])
        gs = vjp((dy_ref[...].astype(y.dtype), dst[...]))
        dst[...] = gs[0]
        for g, spec, ref in zip(gs[1:], grads, g_refs):
            first = spec[5]
            if first is None:
                ref[...] = g.astype(ref.dtype)
            else:
                fst = first(*ids)

                @pl.when(fst)
                def _(ref=ref, g=g):
                    ref[...] = g.astype(ref.dtype)

                @pl.when(jnp.logical_not(fst))
                def _(ref=ref, g=g):
                    ref[...] += g.astype(ref.dtype)

    allin = list(ins) + [st_in, dy]
    return pl.pallas_call(
        body, grid=grid,
        in_specs=[pl.BlockSpec(b, m) for _, b, m in allin],
        out_specs=[pl.BlockSpec(s[3], s[4]) for s in grads],
        out_shape=[jax.ShapeDtypeStruct(s[1], s[2]) for s in grads],
        scratch_shapes=[pltpu.VMEM(state_shape, F32)],
        name=name, compiler_params=_cparams(("arbitrary",) * len(grid)),
    )(*[a for a, _, _ in allin])


def _tile_order(order, nt, nctx=1):
    rev = lambda j: jnp.where(j < nctx, nctx - 1 - j, nt - 1 - (j - nctx))
    if order == 'F':
        return (lambda j: j), True
    if order == 'Fb':
        return (lambda j: nt - 1 - j), False
    if order == 'R':
        return rev, False
    return (lambda j: rev(nt - 1 - j)), True


def _scan8(coef, val, sub, asc):
    for step in (1, 2, 4):
        shift = step if asc else 8 - step
        keep = (sub >= step) if asc else (sub < 8 - step)
        val = jnp.where(keep, coef * pltpu.roll(val, shift, 0) + val, val)
        coef = jnp.where(keep, coef * pltpu.roll(coef, shift, 0), coef)
    return coef, val


def _prev_rows(tile, carry, sub, asc):
    return jnp.where(sub == 0, carry, pltpu.roll(tile, 1, 0)) if asc else jnp.where(sub == 7, carry, pltpu.roll(tile, 7, 0))


def _last_row(tile, asc):
    return jnp.broadcast_to(tile[7:8] if asc else tile[0:1], tile.shape)


def linrec(name, a, b, order):
    bsz, tt, cols = a.shape
    tq = _pick(tt, (256, 128))
    nt, ng, nj = tt // tq, tq // 8, cols // 128
    phys, asc = _tile_order(order, nt)

    def body(a_ref, b_ref, h_ref, hp_ref, hc):
        @pl.when(pl.program_id(0) == 0)
        def _():
            hc[...] = jnp.zeros_like(hc)

        sub = lax.broadcasted_iota(jnp.int32, (8, 128), 0)

        def group(i, carry):
            rows = pl.ds(pl.multiple_of((i if asc else ng - 1 - i) * 8, 8), 8)
            for bi in range(bsz):
                for j in range(nj):
                    cs = slice(j * 128, (j + 1) * 128)
                    h_in = hc[bi, j]
                    ca, cv = _scan8(a_ref[bi, rows, cs], b_ref[bi, rows, cs], sub, asc)
                    h = ca * h_in + cv
                    h_ref[bi, rows, cs] = h
                    hp_ref[bi, rows, cs] = _prev_rows(h, h_in, sub, asc)
                    hc[bi, j] = _last_row(h, asc)
            return carry

        lax.fori_loop(0, ng, group, 0)

    spec = pl.BlockSpec((bsz, tq, cols), lambda j: (0, phys(j), 0))
    return pl.pallas_call(
        body, grid=(nt,), in_specs=[spec, spec], out_specs=[spec, spec],
        out_shape=[jax.ShapeDtypeStruct(a.shape, F32)] * 2,
        scratch_shapes=[pltpu.VMEM((bsz, nj, 8, 128), F32)],
        name=name, compiler_params=_cparams(("arbitrary",)),
    )(a, b)


def linrec_bwd(name, a, dh, hprev, order):
    bsz, tt, cols = a.shape
    tq = _pick(tt, (256, 128))
    nt, ng, nj = tt // tq, tq // 8, cols // 128
    phys, asc = _tile_order(order, nt)

    def body(a_ref, dh_ref, hp_ref, g_ref, ga_ref, gc, ac):
        @pl.when(pl.program_id(0) == 0)
        def _():
            gc[...] = jnp.zeros_like(gc)
            ac[...] = jnp.zeros_like(ac)

        sub = lax.broadcasted_iota(jnp.int32, (8, 128), 0)

        def group(i, carry):
            rows = pl.ds(pl.multiple_of((i if asc else ng - 1 - i) * 8, 8), 8)
            for bi in range(bsz):
                for j in range(nj):
                    cs = slice(j * 128, (j + 1) * 128)
                    a_tile = a_ref[bi, rows, cs]
                    ca, cv = _scan8(_prev_rows(a_tile, ac[bi, j], sub, asc), dh_ref[bi, rows, cs], sub, asc)
                    g = ca * gc[bi, j] + cv
                    g_ref[bi, rows, cs] = g
                    ga_ref[bi, rows, cs] = g * hp_ref[bi, rows, cs]
                    gc[bi, j] = _last_row(g, asc)
                    ac[bi, j] = _last_row(a_tile, asc)
            return carry

        lax.fori_loop(0, ng, group, 0)

    spec = pl.BlockSpec((bsz, tq, cols), lambda j: (0, phys(j), 0))
    return pl.pallas_call(
        body, grid=(nt,), in_specs=[spec, spec, spec], out_specs=[spec, spec],
        out_shape=[jax.ShapeDtypeStruct(a.shape, F32)] * 2,
        scratch_shapes=[pltpu.VMEM((bsz, nj, 8, 128), F32), pltpu.VMEM((bsz, nj, 8, 128), F32)],
        name=name, compiler_params=_cparams(("arbitrary",)),
    )(a, dh, hprev)


def _cmul(a, b):
    return a[0] * b[0] - a[1] * b[1], a[0] * b[1] + a[1] * b[0]


def _cpow_tables(ar, ai, asc):
    pows = [(ar, ai)]
    for _ in range(7):
        pows.append(_cmul(pows[-1], (ar, ai)))
    tile = lambda p: jnp.broadcast_to(p[:, None, :], (8, 8, 128))
    steps = jnp.stack([jnp.stack([tile(pows[s - 1][0]), tile(pows[s - 1][1])]) for s in (1, 2, 4)])
    order = range(8) if asc else range(7, -1, -1)
    carry = jnp.stack([jnp.stack([pows[i][c] for i in order], axis=1) for c in (0, 1)])
    return steps, carry


def _cscan8(xr, xi, st_ref, j, sub, asc):
    for s, step in enumerate((1, 2, 4)):
        shift = step if asc else 8 - step
        keep = (sub >= step) if asc else (sub < 8 - step)
        pr, pi = st_ref[s, 0, j], st_ref[s, 1, j]
        rr, ri = pltpu.roll(xr, shift, 0), pltpu.roll(xi, shift, 0)
        xr, xi = jnp.where(keep, xr + pr * rr - pi * ri, xr), jnp.where(keep, xi + pr * ri + pi * rr, xi)
    return xr, xi


def clinrec(name, x, coef, order, lc, conj=False, hprev=None):
    btot, tt, cols2 = x.shape
    cols = cols2 // 2
    bsz, ngrp = btot, 1
    tq = 128
    nt, ng, nj = tt // tq, tq // 8, cols // 128
    phys, asc = _tile_order(order, nt, lc // tq)
    steps, carry = _cpow_tables(coef[0], -coef[1] if conj else coef[1], asc)
    adjoint = hprev is not None

    def body(*refs):
        if adjoint:
            x_ref, hp_ref, st_ref, cr_ref, h_ref, dc_ref, hc = refs
        else:
            x_ref, st_ref, cr_ref, h_ref, hp_ref, hc = refs

        @pl.when(pl.program_id(1) == 0)
        def _():
            hc[...] = jnp.zeros_like(hc)

        if adjoint:
            @pl.when((pl.program_id(0) == 0) & (pl.program_id(1) == 0))
            def _():
                dc_ref[...] = jnp.zeros_like(dc_ref)

        sub = lax.broadcasted_iota(jnp.int32, (8, 128), 0)

        def group(i, c_):
            rows = pl.ds(pl.multiple_of((i if asc else ng - 1 - i) * 8, 8), 8)
            for bi in range(bsz):
                for j in range(nj):
                    cr, ci = slice(j * 128, (j + 1) * 128), slice(cols + j * 128, cols + (j + 1) * 128)
                    sr, si = _cscan8(x_ref[bi, rows, cr], x_ref[bi, rows, ci], st_ref, j, sub, asc)
                    in_r, in_i = hc[bi, 0, j], hc[bi, 1, j]
                    pr, pi = cr_ref[0, j], cr_ref[1, j]
                    hr = sr + pr * in_r - pi * in_i
                    hi = si + pr * in_i + pi * in_r
                    h_ref[bi, rows, cr] = hr
                    h_ref[bi, rows, ci] = hi
                    if adjoint:
                        qr, qi = hp_ref[bi, rows, cr], hp_ref[bi, rows, ci]
                        dc_ref[0, j] += hr * qr + hi * qi
                        dc_ref[1, j] += hi * qr - hr * qi
                    else:
                        hp_ref[bi, rows, cr] = _prev_rows(hr, in_r, sub, asc)
                        hp_ref[bi, rows, ci] = _prev_rows(hi, in_i, sub, asc)
                    hc[bi, 0, j] = _last_row(hr, asc)
                    hc[bi, 1, j] = _last_row(hi, asc)
            return c_

        lax.fori_loop(0, ng, group, 0)

    spec = pl.BlockSpec((bsz, tq, cols2), lambda g, j: (g, phys(j), 0))
    full = lambda t: pl.BlockSpec(t.shape, lambda g, j, n=t.ndim: (0,) * n)
    dc_shape = (2, nj, 8, 128)
    if adjoint:
        ins, in_specs = (x, hprev, steps, carry), [spec, spec, full(steps), full(carry)]
        out_specs = [spec, pl.BlockSpec(dc_shape, lambda g, j: (0, 0, 0, 0))]
        out_shape = [jax.ShapeDtypeStruct(x.shape, F32), jax.ShapeDtypeStruct(dc_shape, F32)]
    else:
        ins, in_specs = (x, steps, carry), [spec, full(steps), full(carry)]
        out_specs = [spec, spec]
        out_shape = [jax.ShapeDtypeStruct(x.shape, F32)] * 2
    return pl.pallas_call(
        body, grid=(ngrp, nt), in_specs=in_specs, out_specs=out_specs, out_shape=out_shape,
        scratch_shapes=[pltpu.VMEM((bsz, 2, nj, 8, 128), F32)],
        name=name, compiler_params=_cparams(("arbitrary", "arbitrary")),
    )(*ins)


def _blockdiag(bb):
    eye = jnp.eye(16, dtype=bb.dtype)
    return (bb[:, :, None, :] * eye[:, None, :, None]).reshape(256, 1024)


def _blockdiag_t(c):
    eye = jnp.eye(16, dtype=c.dtype)
    return (jnp.swapaxes(c, 1, 2)[:, :, None, :] * eye[:, None, :, None]).reshape(1024, 256)


def _unblockdiag(m):
    eye = jnp.eye(16, dtype=m.dtype)
    return jnp.sum(m.reshape(16, 16, 16, 64) * eye[:, None, :, None], axis=2)


def _unblockdiag_t(m):
    eye = jnp.eye(16, dtype=m.dtype)
    return jnp.swapaxes(jnp.sum(m.reshape(16, 64, 16, 16) * eye[:, None, :, None], axis=2), 1, 2)


def _pad_rows(v, rows=8, cols=128):
    out = jnp.zeros((rows, cols), F32)
    return out.at[0, :v.shape[0]].set(v)


def local_step(x, c, ctx, c_ctx, target, W):
    B, Tx, _ = x.shape
    Lc = ctx.shape[1]
    Tt = Lc + Tx
    tb = Lc
    nt = Tt // tb
    M = B * Tt
    nc = Tt // SSD_CHUNK
    ncc = Lc // SSD_CHUNK
    fgr, fgw = HG_HEADS // HG_FWD[1], 128 * HG_FWD[1]
    bgr, bgw = HG_HEADS // HG_BWD[1], 128 * HG_BWD[1]
    G = {}

    def add_grad(name, idx, val):
        G.setdefault(name, {})[idx] = val

    def tok(a, cb=None, off=0):
        cb = a.shape[-1] if cb is None else cb
        return (a, (None, tb, cb), lambda b, j, off=off: (b, j, off))

    def tok_out(cols, dtype=F32):
        return ((B, Tt, cols), dtype, (None, tb, cols), lambda b, j: (b, j, 0))

    def vec(a):
        return (a, a.shape, lambda *ids, n=a.ndim: (0,) * n)

    def vec_acc(shape):
        return (shape, F32, shape, lambda *ids, n=len(shape): (0,) * n, lambda *ids: functools.reduce(jnp.logical_and, [i == 0 for i in ids]))

    def modv(l, which):
        return (modr, (None, None, None, 1, D), lambda b, j, l=l, which=which: (l, jnp.where(j == 0, 4, b), which, 0, 0))

    dmod_spec = ((B, 2, 1, D), F32, (None, None, 1, D), lambda b, j: (b, jnp.where(j == 0, 0, 1), 0, 0), lambda b, j: j <= 1)

    def phys_chunk(n_all, n_ctx, reverse):
        if not reverse:
            return lambda s: s
        return lambda s: jnp.where(s < n_ctx, n_ctx - 1 - s, n_all - 1 - (s - n_ctx))

    cc = jnp.zeros((8, D), F32).at[:B].set(c).at[4].set(c_ctx)
    nmc = N_MOD * D // 1536

    def f_mod(ccv, w, b):
        return jnp.dot(jax.nn.silu(ccv).astype(BF16), w, preferred_element_type=F32) + b

    mod = tile_fwd('mod_fwd', f_mod, (DEPTH, nmc),
                   [(cc, (8, D), lambda l, n: (0, 0)), (W['w_mod'], (None, D, 1536), lambda l, n: (l, 0, n)),
                    (W['b_mod'].reshape(DEPTH, 1, N_MOD * D), (None, 1, 1536), lambda l, n: (l, 0, n))],
                   [((DEPTH, 8, N_MOD * D), F32, (None, 8, 1536), lambda l, n: (l, 0, n))])[0]
    modr = mod.reshape(DEPTH, 8, N_MOD, 1, D)
    dmods = {}

    lb1, lb3 = tile_fwd('lb_fwd', f_lb, (1,), [vec(W['hg_lb_logits'])],
                        [((1, HG_W), F32, (1, HG_W), lambda i: (0, 0))] * 2)
    dlb = {1: [], 3: []}

    x0 = jnp.concatenate([ctx, x], axis=1)
    conv_mk = ffnconv_masks(Tt, Lc)
    R = [dict() for _ in range(DEPTH)]

    xprev, oprev = x0, None
    for l in range(DEPTH):
        r = R[l]
        j = l // 2
        ng = W['norm_mix_g'][l][None]
        if l == 0:
            h1 = tile_fwd(f'nm0_fwd', f_nm0, (B, nt), [tok(xprev), vec(ng), modv(l, 0), modv(l, 1)], [tok_out(D, BF16)])[0]
            xa = xprev
        else:
            xa, h1 = tile_fwd(f'nm_mix_fwd{l}', f_nm, (B, nt),
                              [tok(xprev), tok(oprev), modv(l - 1, 5), vec(ng), modv(l, 0), modv(l, 1)],
                              [tok_out(D), tok_out(D, BF16)])
        r['xin'], r['oin'], r['xa'], r['h1'] = xprev, oprev, xa, h1
        h1m = h1.reshape(M, D)
        if l % 2 == 0:
            win = W['ev_w_in_p'][j]
            proj = mm(f'ev_proj{l}', [(h1m, win, 0)]).reshape(B, Tt, EV_PAD)
            r['proj'] = proj
            scw, scb = W['ssd_conv_w'][j], W['ssd_conv_b'][j][None]
            lcw, lcb = W['lru_conv_w'][j], W['lru_conv_b'][j][None]

            def conv_call(name, colblk0, w, b, wblk0, ncols, act):
                return tile_fwd(name, functools.partial(f_conv1d, lc=Lc, act=act), (ncols // 256, B),
                                [(proj, (None, Tt, 256), lambda cb, bi: (bi, 0, colblk0 + cb)),
                                 (w, (4, 256), lambda cb, bi: (0, wblk0 + cb)), (b, (1, 256), lambda cb, bi: (0, wblk0 + cb))],
                                [((B, Tt, ncols), F32, (None, Tt, 256), lambda cb, bi: (bi, 0, cb))])[0]
            xs_c = conv_call(f'conv_xs{l}', 12, scw, scb, 0, 1024, True)
            bc_c = conv_call(f'conv_bc{l}', 16, scw, scb, 4, 512, True)
            u_c = conv_call(f'conv_u{l}', 8, lcw, lcb, 0, 1024, False)
            r['xs'], r['bc'], r['u'] = xs_c, bc_c, u_c
            bias = _pad_rows(W['ssd_dt_bias'][j].reshape(-1), 1)
            alog = _pad_rows(W['ssd_a_log'][j].reshape(-1), 1)
            r['bias'], r['alog'] = bias, alog
            r['y'], r['st'], r['a4'], r['hp4'], r['h'] = [], [], [], [], []
            for d in range(2):
                ph = phys_chunk(nc, ncc, d == 1)
                y, st = scan_fwd(
                    f'ssd_fwd{l}_{d}', functools.partial(f_ssd, d=d, reverse=(d == 1)), (B, nc),
                    [(xs_c, (None, SSD_CHUNK, D), lambda b, s, ph=ph: (b, ph(s), 0)),
                     (bc_c, (None, SSD_CHUNK, 512), lambda b, s, ph=ph: (b, ph(s), 0)),
                     (proj, (None, SSD_CHUNK, 128), lambda b, s, ph=ph: (b, ph(s), 36)),
                     vec(bias), vec(alog)],
                    ((B, Tt, D), F32, (None, SSD_CHUNK, D), lambda b, s, ph=ph: (b, ph(s), 0)),
                    ((B, nc, 8, 128, 128), F32, (None, None, 8, 128, 128), lambda b, s: (b, s, 0, 0, 0)),
                    (8, 128, 128), lambda b, s: s == 0)
                r['y'].append(y)
                r['st'].append(st)
                a_d, bx_d = tile_fwd(
                    f'gates_fwd{l}_{d}', f_gates, (B, nt),
                    [tok(u_c), vec(W['lru_w_a'][j, d]), vec(W['lru_b_a'][j, d][None]), vec(W['lru_w_i'][j, d]),
                     vec(W['lru_b_i'][j, d][None]), vec(W['lru_lam'][j, d][None])],
                    [tok_out(D), tok_out(D)])
                h_d, hp_d = linrec(f'lru_fwd{l}_{d}', a_d, bx_d, 'F' if d == 0 else 'R')
                r['a4'].append(a_d)
                r['hp4'].append(hp_d)
                r['h'].append(h_d)
            dpad = _pad_rows(W['ssd_d'][j])
            sng = W['ssd_norm_g'][j][None]
            r['dpad'], r['sng'] = dpad, sng
            mix = tile_fwd(f'ssdfin_fwd{l}', f_ssdfin, (B, nt),
                           [tok(r['y'][0]), tok(r['y'][1]), tok(xs_c), tok(proj, D, 0), tok(proj, D, 1), tok(r['h'][0]),
                            tok(r['h'][1]), vec(dpad), vec(sng)], [tok_out(2 * D, BF16)])[0]
            wout = W['ev_w_out'][j]
        else:
            win = W['od_w_in'][j]
            proj = mm(f'od_proj{l}', [(h1m, win, 0)]).reshape(B, Tt, 4096)
            r['proj'] = proj
            lbv = lb1 if l == 1 else lb3
            ns = nc
            r['o'], r['zst'], r['coef'], r['bcat'], r['ccat'], r['hp5'], r['hcat'], r['yd'], r['s5in'] = [], [], [], [], [], [], [], [], []
            u2 = proj[:, :, 3840:].reshape(M, S5_W)
            r['u2'] = u2
            for d in range(2):
                ph = phys_chunk(ns, ncc, d == 1)
                o_d, zst = scan_fwd(
                    f'hgrn_fwd{l}_{d}', functools.partial(f_hgrn_group, reverse=(d == 1), chunk=HG_FWD[0]), (fgr, B, ns),
                    [(proj, (None, 128, fgw), lambda h, b, s, ph=ph: (b, ph(s), h)),
                     (proj, (None, 128, fgw), lambda h, b, s, ph=ph, d=d: (b, ph(s), (1 + d) * fgr + h)),
                     (proj, (None, 128, fgw), lambda h, b, s, ph=ph: (b, ph(s), 3 * fgr + h)),
                     (lbv, (1, fgw), lambda h, b, s: (0, h))],
                    ((B, Tt, HG_W), F32, (None, 128, fgw), lambda h, b, s, ph=ph: (b, ph(s), h)),
                    ((B, ns, fgr, fgw, 128), F32, (None, None, None, fgw, 128), lambda h, b, s: (b, s, h, 0, 0)),
                    (fgw, 128), lambda h, b, s: s == 0)
                r['o'].append(o_d)
                r['zst'].append(zst)
                s5in = [W['s5_lam_re'][j, d], W['s5_lam_im'][j, d], W['s5_log_step'][j, d].reshape(16, 1),
                        jnp.swapaxes(W['s5_b_re'][j], 1, 2), jnp.swapaxes(W['s5_b_im'][j], 1, 2)]
                r['s5in'].append(s5in)
                ar, ai, bbr, bbi = tile_fwd(f's5p_fwd{l}_{d}', f_s5p, (1,), [vec(t) for t in s5in],
                                            [((16, 64), F32, (16, 64), lambda i: (0, 0))] * 2
                                            + [((16, 16, 64), F32, (16, 16, 64), lambda i: (0, 0, 0))] * 2)
                coef = jnp.stack([ar.reshape(8, 128), ai.reshape(8, 128)])
                bcat = jnp.concatenate([_blockdiag(bbr), _blockdiag(bbi)], axis=1).astype(BF16)
                ccat = jnp.concatenate([_blockdiag_t(W['s5_c_re'][j, d]), -_blockdiag_t(W['s5_c_im'][j, d])], axis=0).astype(BF16)
                xcat = mm(f's5_in{l}_{d}', [(u2, bcat, 0)])
                h5, hp5 = clinrec(f's5_fwd{l}_{d}', xcat.reshape(B, Tt, 2 * D), coef, 'F' if d == 0 else 'R', Lc)
                hcat = h5.reshape(M, 2 * D)
                yd = mm(f's5_out{l}_{d}', [(hcat, ccat, 0)]).reshape(B, Tt, S5_W)
                r['coef'].append(coef)
                r['bcat'].append(bcat)
                r['ccat'].append(ccat)
                r['hp5'].append(hp5)
                r['hcat'].append(hcat)
                r['yd'].append(yd)
            hn = jnp.zeros((8, 128), F32).at[:HG_HEADS].set(W['hg_norm_g'][j])
            sd, gw, gb = W['s5_d'][j][None], W['s5_glu_w'][j], W['s5_glu_b'][j][None]
            r['fin_par'] = (hn, sd, gw, gb)
            mix = tile_fwd(f'oddfin_fwd{l}', f_oddfin, (B, nt),
                           [tok(r['o'][0]), tok(r['o'][1]), tok(proj, HG_W, 4), tok(r['yd'][0]), tok(r['yd'][1]),
                            tok(proj, S5_W, 15), vec(hn), vec(sd), vec(gw), vec(gb)], [tok_out(D, BF16)])[0]
            wout = W['od_w_out'][j]
        r['mix'] = mix
        o1 = mm(f'mix_out{l}', [(mix.reshape(M, -1), wout, 0)]).reshape(B, Tt, D)
        r['o1'] = o1
        fg = W['norm_ffn_g'][l][None]
        xb, h2 = tile_fwd(f'nm_ffn_fwd{l}', f_nm, (B, nt), [tok(xa), tok(o1), modv(l, 2), vec(fg), modv(l, 3), modv(l, 4)],
                          [tok_out(D), tok_out(D, BF16)])
        r['h2'] = h2
        h2m = h2.reshape(M, D)
        a = mm(f'ffn_gate{l}', [(h2m, W['ffn_w_gate'][l], 0)]).reshape(B, Tt, D_FF)
        up = mm(f'ffn_up{l}', [(h2m, W['ffn_w_up'][l], 0)]).reshape(B, Tt, D_FF)
        w9 = W['ffn_conv_w'][l].reshape(9, D_FF)
        cbias = W['ffn_conv_b'][l][None]
        r['a'], r['up'], r['w9'], r['cbias'] = a, up, w9, cbias
        act = tile_fwd(f'ffnconv_fwd{l}', f_ffnconv, (D_FF // 128, B),
                       [(a, (None, Tt, 128), lambda cb, bi: (bi, 0, cb)), (up, (None, Tt, 128), lambda cb, bi: (bi, 0, cb)),
                        (w9, (9, 128), lambda cb, bi: (0, cb)), (cbias, (1, 128), lambda cb, bi: (0, cb)), vec(conv_mk)],
                       [((B, Tt, D_FF), BF16, (None, Tt, 128), lambda cb, bi: (bi, 0, cb))])[0]
        r['act'] = act
        o2 = mm(f'ffn_down{l}', [(act.reshape(M, D_FF), W['ffn_w_down'][l], 0)]).reshape(B, Tt, D)
        xprev, oprev = xb, o2

    vmask = jnp.ones((nt, 1, D), F32).at[0].set(0.0)
    ones = jnp.ones((1, D), F32)
    fng = W['final_norm_g'][None]
    d_xp, d_o2, dg5, dfng, loss_vec = tile_bwd(
        'loss_head', f_final, (B, nt),
        [tok(xprev), tok(oprev), modv(DEPTH - 1, 5), vec(fng),
         (target, (None, tb, D), lambda b, j: (b, jnp.maximum(j - 1, 0), 0)), (vmask, (None, 1, D), lambda b, j: (j, 0, 0))],
        [[vec(ones)]],
        [(0,) + tok_out(D) + (None,), (1,) + tok_out(D, BF16) + (None,), (2,) + dmod_spec, (3,) + vec_acc((1, D))],
        prims=[(0,) + vec_acc((1, D))])
    loss = jnp.sum(loss_vec)
    add_grad('final_norm_g', None, dfng[0])
    dmods[(DEPTH - 1, 5)] = dg5

    for l in reversed(range(DEPTH)):
        r = R[l]
        j = l // 2
        d_o2m = d_o2.reshape(M, D)
        d_act = mm(f'ffn_down_dx{l}', [(d_o2m, W['ffn_w_down'][l], 0)], tb=True).reshape(B, Tt, D_FF)
        add_grad('ffn_w_down', l, mm(f'ffn_down_dw{l}', [(r['act'].reshape(M, D_FF), d_o2m, 0)], ta=True))
        d_a, d_up, dw9, dcb = tile_bwd(
            f'ffnconv_bwd{l}', f_ffnconv, (D_FF // 128, B),
            [(r['a'], (None, Tt, 128), lambda cb, bi: (bi, 0, cb)), (r['up'], (None, Tt, 128), lambda cb, bi: (bi, 0, cb)),
             (r['w9'], (9, 128), lambda cb, bi: (0, cb)), (r['cbias'], (1, 128), lambda cb, bi: (0, cb)), vec(conv_mk)],
            [[(d_act, (None, Tt, 128), lambda cb, bi: (bi, 0, cb))]],
            [(0, (B, Tt, D_FF), BF16, (None, Tt, 128), lambda cb, bi: (bi, 0, cb), None),
             (1, (B, Tt, D_FF), BF16, (None, Tt, 128), lambda cb, bi: (bi, 0, cb), None),
             (2, (9, D_FF), F32, (9, 128), lambda cb, bi: (0, cb), lambda cb, bi: bi == 0),
             (3, (1, D_FF), F32, (1, 128), lambda cb, bi: (0, cb), lambda cb, bi: bi == 0)])
        add_grad('ffn_conv_w', l, dw9.reshape(3, 3, D_FF))
        add_grad('ffn_conv_b', l, dcb[0])
        d_am, d_upm = d_a.reshape(M, D_FF), d_up.reshape(M, D_FF)
        h2m = r['h2'].reshape(M, D)
        d_h2 = mm(f'ffn_in_dx{l}', [(d_am, W['ffn_w_gate'][l], 0), (d_upm, W['ffn_w_up'][l], 0)], tb=True).reshape(B, Tt, D)
        add_grad('ffn_w_gate', l, mm(f'ffn_gate_dw{l}', [(h2m, d_am, 0)], ta=True))
        add_grad('ffn_w_up', l, mm(f'ffn_up_dw{l}', [(h2m, d_upm, 0)], ta=True))
        fg = W['norm_ffn_g'][l][None]
        d_xa, d_o1, dgate, dfg, dsh, dsc = tile_bwd(
            f'nm_ffn_bwd{l}', f_nm, (B, nt), [tok(r['xa']), tok(r['o1']), modv(l, 2), vec(fg), modv(l, 3), modv(l, 4)],
            [[tok(d_xp)], [tok(d_h2)]],
            [(0,) + tok_out(D) + (None,), (1,) + tok_out(D, BF16) + (None,), (2,) + dmod_spec, (3,) + vec_acc((1, D)),
             (4,) + dmod_spec, (5,) + dmod_spec])
        add_grad('norm_ffn_g', l, dfg[0])
        dmods[(l, 2)], dmods[(l, 3)], dmods[(l, 4)] = dgate, dsh, dsc
        d_o1m = d_o1.reshape(M, D)
        h1m = r['h1'].reshape(M, D)
        proj = r['proj']
        if l % 2 == 0:
            wout, win = W['ev_w_out'][j], W['ev_w_in_p'][j]
            d_mix = mm(f'mix_out_dx{l}', [(d_o1m, wout, 0)], tb=True).reshape(B, Tt, 2 * D)
            add_grad('ev_w_out', j, mm(f'mix_out_dw{l}', [(r['mix'].reshape(M, 2 * D), d_o1m, 0)], ta=True))
            d_y, d_xs_fin, d_z, d_gy, d_h, ddpad, dsng = tile_bwd(
                f'ssdfin_bwd{l}', f_ssdfin, (B, nt),
                [tok(r['y'][0]), tok(r['y'][1]), tok(r['xs']), tok(proj, D, 0), tok(proj, D, 1), tok(r['h'][0]), tok(r['h'][1]),
                 vec(r['dpad']), vec(r['sng'])],
                [[tok(d_mix)]],
                [(0,) + tok_out(D) + (None,), (2,) + tok_out(D) + (None,), (3,) + tok_out(D) + (None,), (4,) + tok_out(D) + (None,),
                 (5,) + tok_out(D) + (None,), (7,) + vec_acc((8, 128)), (8,) + vec_acc((1, D))])
            add_grad('ssd_d', j, ddpad[0, :SSD_HEADS])
            add_grad('ssd_norm_g', j, dsng[0])
            d_xs_parts, d_bc_parts, d_dt_parts, d_u_parts = [d_xs_fin], [], [], []
            dbias_t, dalog_t = [], []
            dh4 = d_h
            for d in range(2):
                ph0 = phys_chunk(nc, ncc, d == 1)

                def ph(s, ph0=ph0):
                    return ph0(nc - 1 - s)
                dxs_d, dbc_d, ddt_d, dbias, dalog = scan_bwd(
                    f'ssd_bwd{l}_{d}', functools.partial(f_ssd, d=d, reverse=(d == 1)), (B, nc),
                    [(r['xs'], (None, SSD_CHUNK, D), lambda b, s, ph=ph: (b, ph(s), 0)),
                     (r['bc'], (None, SSD_CHUNK, 512), lambda b, s, ph=ph: (b, ph(s), 0)),
                     (proj, (None, SSD_CHUNK, 128), lambda b, s, ph=ph: (b, ph(s), 36)),
                     vec(r['bias']), vec(r['alog'])],
                    (r['st'][d], (None, None, 8, 128, 128), lambda b, s: (b, nc - 1 - s, 0, 0, 0)),
                    (d_y, (None, SSD_CHUNK, D), lambda b, s, ph=ph: (b, ph(s), 0)),
                    [(0, (B, Tt, D), F32, (None, SSD_CHUNK, D), lambda b, s, ph=ph: (b, ph(s), 0), None),
                     (1, (B, Tt, 512), F32, (None, SSD_CHUNK, 512), lambda b, s, ph=ph: (b, ph(s), 0), None),
                     (2, (B, Tt, 128), F32, (None, SSD_CHUNK, 128), lambda b, s, ph=ph: (b, ph(s), 0), None),
                     (3,) + vec_acc((1, 128)), (4,) + vec_acc((1, 128))],
                    (8, 128, 128), lambda b, s: s == 0)
                d_xs_parts.append(dxs_d)
                d_bc_parts.append(dbc_d)
                d_dt_parts.append(ddt_d)
                dbias_t.append(dbias)
                dalog_t.append(dalog)
                g4, ga4 = linrec_bwd(f'lru_bwd{l}_{d}', r['a4'][d], dh4, r['hp4'][d], 'Fb' if d == 0 else 'Rb')
                du_g, dwa, dba, dwi, dbi, dlam = tile_bwd(
                    f'gates_bwd{l}_{d}', f_gates, (B, nt),
                    [tok(r['u']), vec(W['lru_w_a'][j, d]), vec(W['lru_b_a'][j, d][None]), vec(W['lru_w_i'][j, d]),
                     vec(W['lru_b_i'][j, d][None]), vec(W['lru_lam'][j, d][None])],
                    [[tok(ga4)], [tok(g4)]],
                    [(0,) + tok_out(D) + (None,), (1,) + vec_acc((8, 128, 128)), (2,) + vec_acc((1, D)), (3,) + vec_acc((8, 128, 128)),
                     (4,) + vec_acc((1, D)), (5,) + vec_acc((1, D))])
                d_u_parts.append(du_g)
                add_grad('lru_w_a', (j, d), dwa)
                add_grad('lru_b_a', (j, d), dba[0])
                add_grad('lru_w_i', (j, d), dwi)
                add_grad('lru_b_i', (j, d), dbi[0])
                add_grad('lru_lam', (j, d), dlam[0])
            add_grad('ssd_dt_bias', j, (dbias_t[0] + dbias_t[1])[0, :32].reshape(2, SSD_HEADS))
            add_grad('ssd_a_log', j, (dalog_t[0] + dalog_t[1])[0, :32].reshape(2, SSD_HEADS))
            scw, scb = W['ssd_conv_w'][j], W['ssd_conv_b'][j][None]
            lcw, lcb = W['lru_conv_w'][j], W['lru_conv_b'][j][None]

            def conv_bwd(name, colblk0, w, b, wblk0, ncols, act, parts):
                return tile_bwd(
                    name, functools.partial(f_conv1d, lc=Lc, act=act), (ncols // 256, B),
                    [(proj, (None, Tt, 256), lambda cb, bi: (bi, 0, colblk0 + cb)),
                     (w, (4, 256), lambda cb, bi: (0, wblk0 + cb)), (b, (1, 256), lambda cb, bi: (0, wblk0 + cb))],
                    [[(p, (None, Tt, 256), lambda cb, bi: (bi, 0, cb)) for p in parts]],
                    [(0, (B, Tt, ncols), F32, (None, Tt, 256), lambda cb, bi: (bi, 0, cb), None),
                     (1, (4, ncols), F32, (4, 256), lambda cb, bi: (0, cb), lambda cb, bi: bi == 0),
                     (2, (1, ncols), F32, (1, 256), lambda cb, bi: (0, cb), lambda cb, bi: bi == 0)])
            d_xs_raw, dw_xs, db_xs = conv_bwd(f'conv_xs_bwd{l}', 12, scw, scb, 0, 1024, True, d_xs_parts)
            d_bc_raw, dw_bc, db_bc = conv_bwd(f'conv_bc_bwd{l}', 16, scw, scb, 4, 512, True, d_bc_parts)
            d_u_raw, dw_u, db_u = conv_bwd(f'conv_u_bwd{l}', 8, lcw, lcb, 0, 1024, False, d_u_parts)
            add_grad('ssd_conv_w', j, jnp.concatenate([dw_xs, dw_bc], axis=1))
            add_grad('ssd_conv_b', j, jnp.concatenate([db_xs, db_bc], axis=1)[0])
            add_grad('lru_conv_w', j, dw_u)
            add_grad('lru_conv_b', j, db_u[0])
            def f_ev_dproj(z_, gy_, u_, xs_, bc_, t0, t1):
                pad = jnp.zeros((z_.shape[0], EV_PAD - 4736), F32)
                return jnp.concatenate([z_, gy_, u_, xs_, bc_, t0 + t1, pad], axis=1)
            dproj = tile_fwd(f'ev_dproj{l}', f_ev_dproj, (B, nt),
                             [tok(d_z), tok(d_gy), tok(d_u_raw), tok(d_xs_raw), tok(d_bc_raw), tok(d_dt_parts[0]), tok(d_dt_parts[1])],
                             [tok_out(EV_PAD, BF16)])[0].reshape(M, EV_PAD)
            d_h1 = mm(f'ev_proj_dx{l}', [(dproj, win, 0)], tb=True).reshape(B, Tt, D)
            dwp = mm(f'ev_proj_dw{l}', [(h1m, dproj, 0)], ta=True)
            add_grad('ev_w_in', j, jnp.concatenate([dwp[:, 0:1024], dwp[:, 3072:4640], dwp[:, 1024:3072]], axis=1))
        else:
            wout, win = W['od_w_out'][j], W['od_w_in'][j]
            d_mix = mm(f'mix_out_dx{l}', [(d_o1m, wout, 0)], tb=True).reshape(B, Tt, D)
            add_grad('od_w_out', j, mm(f'mix_out_dw{l}', [(r['mix'].reshape(M, D), d_o1m, 0)], ta=True))
            hn, sd, gw, gb = r['fin_par']
            d_o, d_g, d_yv, d_u_fin, dhn, dsd, dgw, dgb = tile_bwd(
                f'oddfin_bwd{l}', f_oddfin, (B, nt),
                [tok(r['o'][0]), tok(r['o'][1]), tok(proj, HG_W, 4), tok(r['yd'][0]), tok(r['yd'][1]), tok(proj, S5_W, 15),
                 vec(hn), vec(sd), vec(gw), vec(gb)],
                [[tok(d_mix)]],
                [(0,) + tok_out(HG_W) + (None,), (2,) + tok_out(HG_W) + (None,), (3,) + tok_out(S5_W) + (None,),
                 (5,) + tok_out(S5_W) + (None,), (6,) + vec_acc((8, 128)), (7,) + vec_acc((1, S5_W)), (8,) + vec_acc((S5_W, S5_W)),
                 (9,) + vec_acc((1, S5_W))])
            add_grad('hg_norm_g', j, dhn[:HG_HEADS])
            add_grad('s5_d', j, dsd[0])
            add_grad('s5_glu_w', j, dgw)
            add_grad('s5_glu_b', j, dgb[0])
            lbv = lb1 if l == 1 else lb3
            ns = nc
            dq, df, dv, du_s5 = [], [], [], []
            d_ym = d_yv.reshape(M, S5_W)
            dbt_re, dbt_im = [], []
            for d in range(2):
                ph0 = phys_chunk(ns, ncc, d == 1)

                def ph(s, ph0=ph0):
                    return ph0(ns - 1 - s)
                dq_d, df_d, dv_d, dlb_d = scan_bwd(
                    f'hgrn_bwd{l}_{d}', functools.partial(f_hgrn_group, reverse=(d == 1), chunk=HG_BWD[0]), (bgr, B, ns),
                    [(proj, (None, 128, bgw), lambda h, b, s, ph=ph: (b, ph(s), h)),
                     (proj, (None, 128, bgw), lambda h, b, s, ph=ph, d=d: (b, ph(s), (1 + d) * bgr + h)),
                     (proj, (None, 128, bgw), lambda h, b, s, ph=ph: (b, ph(s), 3 * bgr + h)),
                     (lbv, (1, bgw), lambda h, b, s: (0, h))],
                    (r['zst'][d].reshape(B, ns, bgr, bgw, 128), (None, None, None, bgw, 128), lambda h, b, s: (b, ns - 1 - s, h, 0, 0)),
                    (d_o, (None, 128, bgw), lambda h, b, s, ph=ph: (b, ph(s), h)),
                    [(0, (B, Tt, HG_W), F32, (None, 128, bgw), lambda h, b, s, ph=ph: (b, ph(s), h), None),
                     (1, (B, Tt, HG_W), F32, (None, 128, bgw), lambda h, b, s, ph=ph: (b, ph(s), h), None),
                     (2, (B, Tt, HG_W), F32, (None, 128, bgw), lambda h, b, s, ph=ph: (b, ph(s), h), None),
                     (3, (1, HG_W), F32, (1, bgw), lambda h, b, s: (0, h), lambda h, b, s: (b == 0) & (s == 0))],
                    (bgw, 128), lambda h, b, s: s == 0)
                dq.append(dq_d)
                df.append(df_d)
                dv.append(dv_d)
                dlb[l].append(dlb_d)
                d_hcat = mm(f's5_out_dx{l}_{d}', [(d_ym, r['ccat'][d], 0)], tb=True)
                dccat = mm(f's5_out_dw{l}_{d}', [(r['hcat'][d], d_ym, 0)], ta=True)
                add_grad('s5_c_re', (j, d), _unblockdiag_t(dccat[:D]))
                add_grad('s5_c_im', (j, d), -_unblockdiag_t(dccat[D:]))
                g5, dcoef8 = clinrec(f's5_bwd{l}_{d}', d_hcat.reshape(B, Tt, 2 * D), r['coef'][d], 'Fb' if d == 0 else 'Rb',
                                     Lc, conj=True, hprev=r['hp5'][d])
                dcoef = jnp.sum(dcoef8, axis=2)
                gcat = g5.reshape(M, 2 * D)
                dbcat = mm(f's5_in_dw{l}_{d}', [(r['u2'], gcat, 0)], ta=True)
                du_s5.append(mm(f's5_in_dx{l}_{d}', [(gcat, r['bcat'][d], 0)], tb=True))
                cts5 = [dcoef[0].reshape(16, 64), dcoef[1].reshape(16, 64), _unblockdiag(dbcat[:, :D]), _unblockdiag(dbcat[:, D:])]
                dlre, dlim, dlst, dbtr, dbti = tile_bwd(
                    f's5p_bwd{l}_{d}', f_s5p, (1,), [vec(t) for t in r['s5in'][d]], [[vec(t)] for t in cts5],
                    [(i, t.shape, F32, t.shape, (lambda *ids, n=t.ndim: (0,) * n), None) for i, t in enumerate(r['s5in'][d])])
                add_grad('s5_lam_re', (j, d), dlre)
                add_grad('s5_lam_im', (j, d), dlim)
                add_grad('s5_log_step', (j, d), dlst[:, 0])
                dbt_re.append(dbtr)
                dbt_im.append(dbti)
            add_grad('s5_b_re', j, jnp.swapaxes(dbt_re[0] + dbt_re[1], 1, 2))
            add_grad('s5_b_im', j, jnp.swapaxes(dbt_im[0] + dbt_im[1], 1, 2))
            def f_od_dproj(q0, q1, f0, f1, v0, v1, g_, u0, u1, u2):
                return jnp.concatenate([q0 + q1, f0, f1, v0 + v1, g_, u0 + u1 + u2], axis=1)
            parts = [dq[0], dq[1], df[0], df[1], dv[0], dv[1], d_g, d_u_fin, du_s5[0].reshape(B, Tt, S5_W),
                     du_s5[1].reshape(B, Tt, S5_W)]
            dproj = tile_fwd(f'od_dproj{l}', f_od_dproj, (B, nt), [tok(t) for t in parts],
                             [tok_out(4096, BF16)])[0].reshape(M, 4096)
            d_h1 = mm(f'od_proj_dx{l}', [(dproj, win, 0)], tb=True).reshape(B, Tt, D)
            add_grad('od_w_in', j, mm(f'od_proj_dw{l}', [(h1m, dproj, 0)], ta=True))
        ng = W['norm_mix_g'][l][None]
        if l == 0:
            d_x0, dng, dsh, dsc = tile_bwd(
                'nm0_bwd', lambda xv, g, sh, sc: (xv, f_nm0(xv, g, sh, sc)), (B, nt),
                [tok(r['xin']), vec(ng), modv(l, 0), modv(l, 1)], [[tok(d_xa)], [tok(d_h1)]],
                [(0,) + tok_out(D) + (None,), (1,) + vec_acc((1, D)), (2,) + dmod_spec, (3,) + dmod_spec])
        else:
            d_xp, d_o2, dgate, dng, dsh, dsc = tile_bwd(
                f'nm_mix_bwd{l}', f_nm, (B, nt),
                [tok(r['xin']), tok(r['oin']), modv(l - 1, 5), vec(ng), modv(l, 0), modv(l, 1)],
                [[tok(d_xa)], [tok(d_h1)]],
                [(0,) + tok_out(D) + (None,), (1,) + tok_out(D, BF16) + (None,), (2,) + dmod_spec, (3,) + vec_acc((1, D)),
                 (4,) + dmod_spec, (5,) + dmod_spec])
            dmods[(l - 1, 5)] = dgate
        add_grad('norm_mix_g', l, dng[0])
        dmods[(l, 0)], dmods[(l, 1)] = dsh, dsc

    grad_x = d_x0[:, Lc:, :]

    (dlogits,) = tile_bwd('lb_bwd', f_lb, (1,), [vec(W['hg_lb_logits'])],
                          [[vec(t) for t in dlb[1]], [vec(t) for t in dlb[3]]],
                          [(0, (DEPTH, HG_W), F32, (DEPTH, HG_W), lambda i: (0, 0), None)])
    add_grad('hg_lb_logits', None, dlogits)

    dm = jnp.stack([jnp.stack([dmods[(l, w)] for w in range(N_MOD)]) for l in range(DEPTH)])
    dlat = jnp.transpose(dm[:, :, :, 1, 0, :], (0, 2, 1, 3)).reshape(DEPTH, B, N_MOD * D)
    dctx = jnp.transpose(dm[:, :, :, 0, 0, :], (0, 2, 1, 3)).reshape(DEPTH, B, N_MOD * D)
    dlat = jnp.zeros((DEPTH, 8, N_MOD * D), F32).at[:, :B].set(dlat)
    dctx = jnp.zeros((DEPTH, 8, N_MOD * D), F32).at[:, :B].set(dctx)

    def mod_bwd_body(cc_ref, w_ref, dl_ref, dc_ref, dw_ref, db_ref, dcc_ref):
        row = lax.broadcasted_iota(jnp.int32, (8, 1), 0)
        dall = dl_ref[...] + jnp.where(row == 4, jnp.sum(dc_ref[...], axis=0, keepdims=True), 0.0)
        s, vjp = jax.vjp(jax.nn.silu, cc_ref[...])
        db16 = dall.astype(BF16)
        dw_ref[...] = lax.dot_general(s.astype(BF16), db16, (((0,), (0,)), ((), ())), preferred_element_type=F32)
        db_ref[...] = jnp.sum(dall, axis=0, keepdims=True)
        ds = lax.dot_general(db16, w_ref[...], (((1,), (1,)), ((), ())), preferred_element_type=F32)
        (dcc,) = vjp(ds)
        first = (pl.program_id(0) == 0) & (pl.program_id(1) == 0)

        @pl.when(first)
        def _():
            dcc_ref[...] = dcc

        @pl.when(jnp.logical_not(first))
        def _():
            dcc_ref[...] += dcc

    dwmod, dbmod, dcc = pl.pallas_call(
        mod_bwd_body, grid=(DEPTH, nmc),
        in_specs=[pl.BlockSpec((8, D), lambda l, n: (0, 0)), pl.BlockSpec((None, D, 1536), lambda l, n: (l, 0, n)),
                  pl.BlockSpec((None, 8, 1536), lambda l, n: (l, 0, n)), pl.BlockSpec((None, 8, 1536), lambda l, n: (l, 0, n))],
        out_specs=[pl.BlockSpec((None, D, 1536), lambda l, n: (l, 0, n)), pl.BlockSpec((None, 1, 1536), lambda l, n: (l, 0, n)),
                   pl.BlockSpec((8, D), lambda l, n: (0, 0))],
        out_shape=[jax.ShapeDtypeStruct((DEPTH, D, N_MOD * D), F32), jax.ShapeDtypeStruct((DEPTH, 1, N_MOD * D), F32),
                   jax.ShapeDtypeStruct((8, D), F32)],
        name='mod_bwd', compiler_params=_cparams(("arbitrary", "arbitrary")),
    )(cc, W['w_mod'], dlat, dctx)
    add_grad('w_mod', None, dwmod)
    add_grad('b_mod', None, dbmod[:, 0])
    add_grad('c_ctx', None, dcc[4])
    return loss, grad_x, G


def assemble_grads(G, like):
    out = {}
    for name, parts in G.items():
        shape = like[name].shape
        if None in parts:
            g = parts[None]
        elif isinstance(next(iter(parts)), tuple):
            g = jnp.stack([jnp.stack([parts[(j, d)] for d in range(2)]) for j in range(shape[0])])
        else:
            g = jnp.stack([parts[i] for i in range(shape[0])])
        out[name] = g.reshape(shape)
    return out


XY_RELS = ((1, 0, 0), (0, 1, 0), (1, 1, 0))
ALL_RELS = tuple((dx, dy, dc) for dx in (0, 1) for dy in (0, 1) for dc in (0, 1))[1:]


def exchange(name, src, out_shape, sends):
    return exchange_many(name, [(src, out_shape, sends)])[0]


def exchange_many(name, items):
    na = len(items)
    n = sum(len(sends) for _, _, sends in items)

    def body(*refs):
        src_refs, out_refs, send_sems, recv_sems = refs[:na], refs[na:2 * na], refs[2 * na], refs[2 * na + 1]
        me = (lax.axis_index("x"), lax.axis_index("y"), lax.axis_index("c"))
        copies, k = [], 0
        for (_, _, sends), src_ref, out_ref in zip(items, src_refs, out_refs):
            for rel, ssel, dsel in sends:
                tgt = tuple(1 - m if f else m for m, f in zip(me, rel))
                cp = pltpu.make_async_remote_copy(
                    src_ref=src_ref if ssel is None else src_ref.at[ssel(me, tgt)],
                    dst_ref=out_ref if dsel is None else out_ref.at[dsel(me, tgt)],
                    send_sem=send_sems.at[k], recv_sem=recv_sems.at[k], device_id=tgt, device_id_type=MESH)
                cp.start()
                copies.append(cp)
                k += 1
        for cp in copies:
            cp.wait()

    return pl.pallas_call(
        body, out_shape=[jax.ShapeDtypeStruct(shape, src.dtype) for src, shape, _ in items],
        in_specs=[pl.BlockSpec(memory_space=pl.ANY)] * na, out_specs=[pl.BlockSpec(memory_space=pl.ANY)] * na,
        scratch_shapes=[pltpu.SemaphoreType.DMA((n,)), pltpu.SemaphoreType.DMA((n,))],
        name=name,
    )(*[src for src, _, _ in items])


def _xy_index(dev):
    return 2 * dev[0] + dev[1]


def _my_xy():
    return 2 * lax.axis_index("x") + lax.axis_index("y")


def all_gather_xy(name, shard):
    got = exchange(name, shard, (4,) + shard.shape, [(rel, None, lambda me, tgt: _xy_index(me)) for rel in XY_RELS])
    return lax.dynamic_update_index_in_dim(got, shard, _my_xy(), 0)


def reduce_scatter_xy(name, g4):
    got = exchange(name, g4, (3,) + g4.shape[1:],
                   [(rel, (lambda me, tgt: _xy_index(tgt)), (lambda me, tgt, k=k: k)) for k, rel in enumerate(XY_RELS)])
    return got, lax.dynamic_index_in_dim(g4, _my_xy(), 0, keepdims=False)


def sibling_swap(name, v):
    return exchange(name, v, v.shape, [((0, 0, 1), None, None)])


def all_gather_all(name, v):
    got = exchange(name, v, (8,) + v.shape, [(rel, None, lambda me, tgt: 4 * me[0] + 2 * me[1] + me[2]) for rel in ALL_RELS])
    return lax.dynamic_update_index_in_dim(got, v, 2 * _my_xy() + lax.axis_index("c"), 0)


def all_gather_xy_halves(name, shards):
    na = len(shards)

    def body(*refs):
        src_refs, out_refs, send_sems, recv_sems = refs[:na], refs[na:2 * na], refs[2 * na], refs[2 * na + 1]
        x, y, c = lax.axis_index("x"), lax.axis_index("y"), lax.axis_index("c")
        peers = [(1 - x, y), (x, 1 - y), (1 - x, 1 - y)]

        def copy(k, src, dst, to):
            return pltpu.make_async_remote_copy(src_ref=src, dst_ref=dst, send_sem=send_sems.at[k], recv_sem=recv_sems.at[k],
                                                device_id=to, device_id_type=MESH)

        halves = [pl.ds(c * (s.shape[0] // 2), s.shape[0] // 2) for s in shards]
        first = [[copy(6 * i + k, src_refs[i].at[halves[i]], out_refs[i].at[2 * x + y, halves[i]], (px, py, c))
                  for k, (px, py) in enumerate(peers)] for i in range(na)]
        for row in first:
            for cp in row:
                cp.start()
        passed = []
        for i in range(na):
            for k, (px, py) in enumerate(peers):
                first[i][k].wait_recv()
                landed = out_refs[i].at[2 * px + py, halves[i]]
                fw = copy(6 * i + 3 + k, landed, landed, (x, y, 1 - c))
                fw.start()
                passed.append(fw)
        for fw in passed:
            fw.wait_recv()
        for cp in [cp for row in first for cp in row] + passed:
            cp.wait_send()

    got = pl.pallas_call(
        body, out_shape=[jax.ShapeDtypeStruct((4,) + s.shape, s.dtype) for s in shards],
        in_specs=[pl.BlockSpec(memory_space=pl.ANY)] * na, out_specs=[pl.BlockSpec(memory_space=pl.ANY)] * na,
        scratch_shapes=[pltpu.SemaphoreType.DMA((6 * na,)), pltpu.SemaphoreType.DMA((6 * na,))],
        name=name,
    )(*shards)
    return [lax.dynamic_update_index_in_dim(g, s, _my_xy(), 0) for g, s in zip(got, shards)]


def reduce_scatter_xy_many(name, g4s):
    got = exchange_many(name, [(g4, (3,) + g4.shape[1:],
                                [(rel, (lambda me, tgt: _xy_index(tgt)), (lambda me, tgt, k=k: k)) for k, rel in enumerate(XY_RELS)])
                               for g4 in g4s])
    return [(g, lax.dynamic_index_in_dim(g4, _my_xy(), 0, keepdims=False)) for g, g4 in zip(got, g4s)]


def sibling_split(name, g4s):
    halves = [g4.shape[1] // 2 for g4 in g4s]
    got = exchange_many(name, [(g4, (4, h) + g4.shape[2:], [((0, 0, 1), (lambda me, tgt, h=h: (slice(None), pl.ds(tgt[2] * h, h))), None)])
                               for g4, h in zip(g4s, halves)])
    return [(g, lax.dynamic_slice_in_dim(g4, lax.axis_index("c") * h, h, axis=1)) for g, g4, h in zip(got, g4s, halves)]


def sibling_join(name, qs):
    got = exchange_many(name, [(q, (2 * q.shape[0],) + q.shape[1:], [((0, 0, 1), None, lambda me, tgt, h=q.shape[0]: pl.ds(me[2] * h, h))])
                               for q in qs])
    return [lax.dynamic_update_slice_in_dim(g, q, lax.axis_index("c") * q.shape[0], axis=0) for g, q in zip(got, qs)]


def _rows_view(shape):
    cols = shape[-1] if len(shape) else 1
    rows = 1
    for s in shape[:-1]:
        rows *= s
    return rows, cols


def _row_block(rows, cols, n_arrays):
    budget = (24 * 1024 * 1024) // (8 * n_arrays * cols)
    if rows <= max(budget, 16):
        return rows
    br = (min(budget, rows) // 16) * 16
    while br > 16 and rows % br:
        br -= 16
    return br if rows % br == 0 else rows


def sum_slots(name, stacked, extra=(), out_dtype=F32):
    k = stacked.shape[0]
    rows, cols = _rows_view(stacked.shape[1:])
    br = _row_block(rows, cols, k + len(extra) + 1)

    def f(s, *more):
        parts = [s[i].astype(F32) for i in range(k)] + [m.astype(F32) for m in more]
        while len(parts) > 1:
            parts = [parts[i] + parts[i + 1] for i in range(0, len(parts) - 1, 2)] + ([parts[-1]] if len(parts) % 2 else [])
        return parts[0]

    out = tile_fwd(name, f, (rows // br,),
                   [(stacked.reshape(k, rows, cols), (k, br, cols), lambda i: (0, i, 0))]
                   + [(e.reshape(rows, cols), (br, cols), lambda i: (i, 0)) for e in extra],
                   [((rows, cols), out_dtype, (br, cols), lambda i: (i, 0))])[0]
    return out.reshape(stacked.shape[1:])


def adamw(name, w, m, v, gs):
    rows, cols = _rows_view(w.shape)
    br = _row_block(rows, cols, 7 + len(gs))
    spec = lambda a: (a.reshape(rows, cols), (br, cols), lambda i: (i, 0))
    outs = tile_fwd(name, f_adamw, (rows // br,), [spec(t) for t in (w, m, v) + tuple(gs)],
                    [((rows, cols), F32, (br, cols), lambda i: (i, 0))] * 4)
    return [o.reshape(w.shape) for o in outs]


IN_NAMES = ['x', 'c', 'ctx'] + W_NAMES + ['loss_target'] + ['m_' + n for n in W_NAMES] + ['v_' + n for n in W_NAMES]
SMALL_PAD = 128 * 1024


def kernel(x, c, ctx, c_ctx, w_mod, b_mod, norm_mix_g, norm_ffn_g, final_norm_g, ev_w_in, ev_w_out, ssd_conv_w, ssd_conv_b, ssd_dt_bias, ssd_a_log, ssd_d, ssd_norm_g, lru_conv_w, lru_conv_b, lru_w_a, lru_b_a, lru_w_i, lru_b_i, lru_lam, od_w_in, od_w_out, hg_lb_logits, hg_norm_g, s5_lam_re, s5_lam_im, s5_log_step, s5_b_re, s5_b_im, s5_c_re, s5_c_im, s5_d, s5_glu_w, s5_glu_b, ffn_w_gate, ffn_w_up, ffn_conv_w, ffn_conv_b, ffn_w_down, loss_target, m_c_ctx, m_w_mod, m_b_mod, m_norm_mix_g, m_norm_ffn_g, m_final_norm_g, m_ev_w_in, m_ev_w_out, m_ssd_conv_w, m_ssd_conv_b, m_ssd_dt_bias, m_ssd_a_log, m_ssd_d, m_ssd_norm_g, m_lru_conv_w, m_lru_conv_b, m_lru_w_a, m_lru_b_a, m_lru_w_i, m_lru_b_i, m_lru_lam, m_od_w_in, m_od_w_out, m_hg_lb_logits, m_hg_norm_g, m_s5_lam_re, m_s5_lam_im, m_s5_log_step, m_s5_b_re, m_s5_b_im, m_s5_c_re, m_s5_c_im, m_s5_d, m_s5_glu_w, m_s5_glu_b, m_ffn_w_gate, m_ffn_w_up, m_ffn_conv_w, m_ffn_conv_b, m_ffn_w_down, v_c_ctx, v_w_mod, v_b_mod, v_norm_mix_g, v_norm_ffn_g, v_final_norm_g, v_ev_w_in, v_ev_w_out, v_ssd_conv_w, v_ssd_conv_b, v_ssd_dt_bias, v_ssd_a_log, v_ssd_d, v_ssd_norm_g, v_lru_conv_w, v_lru_conv_b, v_lru_w_a, v_lru_b_a, v_lru_w_i, v_lru_b_i, v_lru_lam, v_od_w_in, v_od_w_out, v_hg_lb_logits, v_hg_norm_g, v_s5_lam_re, v_s5_lam_im, v_s5_log_step, v_s5_b_re, v_s5_b_im, v_s5_c_re, v_s5_c_im, v_s5_d, v_s5_glu_w, v_s5_glu_b, v_ffn_w_gate, v_ffn_w_up, v_ffn_conv_w, v_ffn_conv_b, v_ffn_w_down):
    a = dict(locals())
    big = [n for n in W_NAMES if n in MATMUL_WEIGHTS]
    minor = [n for n in W_NAMES if n in SHARD_AXIS and n not in MATMUL_WEIGHTS]

    def pack(arrays, lead=()):
        flat = jnp.concatenate([t.reshape(lead + (-1,)) for t in arrays], axis=len(lead))
        pad = -flat.shape[-1] % 1024
        flat = jnp.concatenate([flat, jnp.zeros(lead + (pad,), flat.dtype)], axis=len(lead))
        return flat.reshape(lead + (-1, 128))

    def unpack(packed, names, lead=()):
        flat, out, off = packed.reshape(lead + (-1,)), {}, 0
        for n in names:
            size = math.prod(a[n].shape)
            out[n] = flat[..., off:off + size].reshape(lead + a[n].shape)
            off += size
        return out

    gathered = dict(zip(big, all_gather_xy_halves('ag_big', [a[n].astype(BF16) for n in big])))
    gathered.update(unpack(all_gather_xy('ag_minor', pack([a[n] for n in minor])), minor, (4,)))
    W = {}
    for n in W_NAMES:
        w = a[n]
        if n in SHARD_AXIS:
            ax = SHARD_AXIS[n]
            shape = list(w.shape)
            shape[ax] *= 4
            W[n] = jnp.moveaxis(gathered[n], 0, ax).reshape(shape)
        else:
            W[n] = w
    e = W['ev_w_in']
    W['ev_w_in_p'] = jnp.concatenate(
        [e[:, :, 0:1024], e[:, :, 2592:3616], e[:, :, 3616:4640], e[:, :, 1024:2560], e[:, :, 2560:2592],
         jnp.zeros((e.shape[0], D, EV_PAD - 4640), e.dtype)], axis=2)

    loss_local, grad_x, G = local_step(a['x'], a['c'], a['ctx'], W['c_ctx'], a['loss_target'], W)
    grads = assemble_grads(G, W)
    loss = lax.psum(loss_local, ("x", "y", "c"))

    res = {}
    g4 = {}
    for n in SHARD_AXIS:
        ax = SHARD_AXIS[n]
        gf = grads[n]
        g4[n] = jnp.moveaxis(gf.reshape(gf.shape[:ax] + (4, a[n].shape[ax]) + gf.shape[ax + 1:]), ax, 0)
    parts = [sum_slots('csum_' + n, theirs[None], (ours,), BF16)
             for n, (theirs, ours) in zip(big, sibling_split('rsc_big', [g4[n] for n in big]))]
    halves = [sum_slots('gsum_' + n, got, (own,)) for n, (got, own) in zip(big, reduce_scatter_xy_many('rs_big', parts))]
    for n, g in zip(big, sibling_join('agc_big', halves)):
        res[n] = adamw('adamw_' + n, a[n], a['m_' + n], a['v_' + n], (g,))
    got, own = reduce_scatter_xy('rs_minor', pack([g4[n] for n in minor], (4,)))
    mine = sum_slots('gsum_minor', got, (own,))
    mine_n, other_n = unpack(mine, minor), unpack(sibling_swap('sw_minor', mine), minor)
    for n in minor:
        res[n] = adamw('adamw_' + n, a[n], a['m_' + n], a['v_' + n], (mine_n[n], other_n[n]))
    small = [n for n in W_NAMES if n not in SHARD_AXIS]
    flat = jnp.concatenate([grads[n].reshape(-1) for n in small])
    total = flat.shape[0]
    padded = -(-total // SMALL_PAD) * SMALL_PAD
    flat = jnp.concatenate([flat, jnp.zeros((padded - total,), F32)]).reshape(padded // 128, 128)
    pair = sum_slots('csum_small', flat[None], (sibling_swap('sw_small', flat),))
    summed = sum_slots('gsum_small', all_gather_xy('ag_small', pair)).reshape(-1)
    off = 0
    for n in small:
        size = math.prod(a[n].shape)
        g = summed[off:off + size].reshape(a[n].shape)
        off += size
        res[n] = adamw('adamw_' + n, a[n], a['m_' + n], a['v_' + n], (g,))
    outs = [loss, grad_x]
    for k in range(4):
        outs += [res[n][k] for n in W_NAMES]
    return tuple(outs)
```

```python
import functools
import math

import jax
import jax.numpy as jnp
from jax import lax
from jax.experimental import pallas as pl
from jax.experimental.pallas import tpu as pltpu

F32 = jnp.float32
BF16 = jnp.bfloat16
HI = lax.Precision.HIGHEST
MESH = pl.DeviceIdType.MESH

D = 1024
DEPTH = 4
N_MOD = 6
RMS_EPS = 1e-6
GRID_W = 64
SSD_HEADS = 16
SSD_CHUNK = 128
HG_W = 768
HG_HEADS = 6
HG_FWD = (32, 6)
HG_BWD = (16, 3)
S5_W = 256
D_FF = 2816
EV_PAD = 5120
LRU_C = 8.0
V7X_VMEM_LIMIT = 56 * 1024 * 1024
MM_VMEM_BUDGET = 36 * 1024 * 1024

ADAM_LR, ADAM_B1, ADAM_B2, ADAM_EPS, ADAM_WD, ADAM_STEP = 0.001, 0.9, 0.999, 1e-08, 0.01, 10

W_NAMES = ['c_ctx', 'w_mod', 'b_mod', 'norm_mix_g', 'norm_ffn_g', 'final_norm_g', 'ev_w_in', 'ev_w_out', 'ssd_conv_w',
           'ssd_conv_b', 'ssd_dt_bias', 'ssd_a_log', 'ssd_d', 'ssd_norm_g', 'lru_conv_w', 'lru_conv_b', 'lru_w_a', 'lru_b_a',
           'lru_w_i', 'lru_b_i', 'lru_lam', 'od_w_in', 'od_w_out', 'hg_lb_logits', 'hg_norm_g', 's5_lam_re', 's5_lam_im',
           's5_log_step', 's5_b_re', 's5_b_im', 's5_c_re', 's5_c_im', 's5_d', 's5_glu_w', 's5_glu_b', 'ffn_w_gate', 'ffn_w_up',
           'ffn_conv_w', 'ffn_conv_b', 'ffn_w_down']
SHARD_AXIS = {'w_mod': 2, 'ev_w_in': 2, 'ev_w_out': 1, 'ssd_conv_w': 2, 'lru_conv_w': 2, 'lru_b_a': 2, 'lru_b_i': 2,
              'lru_lam': 2, 'od_w_in': 2, 'od_w_out': 1, 's5_d': 1, 's5_glu_w': 1, 's5_glu_b': 1, 'ffn_w_gate': 2,
              'ffn_w_up': 2, 'ffn_conv_w': 3, 'ffn_w_down': 1}
MATMUL_WEIGHTS = ('w_mod', 'ev_w_in', 'ev_w_out', 'od_w_in', 'od_w_out', 'ffn_w_gate', 'ffn_w_up', 'ffn_w_down')


def _cparams(sem=None):
    return pltpu.CompilerParams(vmem_limit_bytes=V7X_VMEM_LIMIT, dimension_semantics=sem)


def _pick(n, cands):
    for c in cands:
        if n % c == 0:
            return c
    return n


def tile_fwd(name, f, grid, ins, outs):
    n_in = len(ins)

    def body(*refs):
        res = f(*[r[...] for r in refs[:n_in]])
        if not isinstance(res, (tuple, list)):
            res = (res,)
        for r, o in zip(res, refs[n_in:]):
            o[...] = r.astype(o.dtype)

    res = pl.pallas_call(
        body, grid=grid,
        in_specs=[pl.BlockSpec(b, m) for _, b, m in ins],
        out_specs=[pl.BlockSpec(b, m) for _, _, b, m in outs],
        out_shape=[jax.ShapeDtypeStruct(s, d) for s, d, _, _ in outs],
        name=name, compiler_params=_cparams(("arbitrary",) * len(grid)),
    )(*[a for a, _, _ in ins])
    return res


def tile_bwd(name, f, grid, ins, cts, grads, prims=()):
    n_in = len(ins)
    ct_flat = [p for c in cts for p in c]
    n_ct = len(ct_flat)
    didx = [g[0] for g in grads]

    def body(*refs):
        in_refs, ct_refs = refs[:n_in], refs[n_in:n_in + n_ct]
        g_refs = refs[n_in + n_ct:n_in + n_ct + len(grads)]
        p_refs = refs[n_in + n_ct + len(grads):]
        vals = [r[...] for r in in_refs]

        def fd(*dv):
            full = list(vals)
            for i, v in zip(didx, dv):
                full[i] = v
            res = f(*full)
            return tuple(res) if isinstance(res, (tuple, list)) else (res,)

        out, vjp = jax.vjp(fd, *[vals[i] for i in didx])
        ctv, k = [], 0
        for o, c in zip(out, cts):
            acc = None
            for _ in c:
                piece = ct_refs[k][...].astype(o.dtype)
                acc = piece if acc is None else acc + piece
                k += 1
            ctv.append(jnp.zeros_like(o) if acc is None else acc.reshape(o.shape))
        gs = vjp(tuple(ctv))
        ids = [pl.program_id(a) for a in range(len(grid))]

        def emit(ref, val, first):
            if first is None:
                ref[...] = val.astype(ref.dtype)
            else:
                is_first = first(*ids)

                @pl.when(is_first)
                def _():
                    ref[...] = val.astype(ref.dtype)

                @pl.when(jnp.logical_not(is_first))
                def _():
                    ref[...] += val.astype(ref.dtype)

        for g, spec, ref in zip(gs, grads, g_refs):
            emit(ref, g, spec[5])
        for spec, ref in zip(prims, p_refs):
            emit(ref, out[spec[0]], spec[5])

    specs = list(grads) + list(prims)
    res = pl.pallas_call(
        body, grid=grid,
        in_specs=[pl.BlockSpec(b, m) for _, b, m in list(ins) + ct_flat],
        out_specs=[pl.BlockSpec(s[3], s[4]) for s in specs],
        out_shape=[jax.ShapeDtypeStruct(s[1], s[2]) for s in specs],
        name=name, compiler_params=_cparams(("arbitrary",) * len(grid)),
    )(*[a for a, _, _ in list(ins) + ct_flat])
    return res


def mm(name, pairs, ta=False, tb=False, out_dtype=F32):
    a0, b0, _ = pairs[0]
    m = a0.shape[1] if ta else a0.shape[0]
    n = b0.shape[0] if tb else b0.shape[1]
    cands = (1024, 1408, 768, 512, 256, 128)
    tks, nks = [], []
    for a, b, _ in pairs:
        k = a.shape[0] if ta else a.shape[1]
        tk = _pick(k, cands)
        tks.append(tk)
        nks.append(k // tk)

    def vmem_bytes(tm, tn):
        tiles = sum(2 * tk * (tm * a.dtype.itemsize + tn * b.dtype.itemsize) for (a, b, _), tk in zip(pairs, tks))
        return tiles + tm * tn * (4 + 2 * jnp.dtype(out_dtype).itemsize)

    tm_c = [c_ for c_ in cands if m % c_ == 0] or [m]
    tn_c = [c_ for c_ in cands if n % c_ == 0] or [n]
    tm, tn = tm_c[0], tn_c[0]
    while vmem_bytes(tm, tn) > MM_VMEM_BUDGET and (len(tm_c) > 1 or len(tn_c) > 1):
        if len(tm_c) > 1 and (tm >= tn or len(tn_c) == 1):
            tm_c = tm_c[1:]
        else:
            tn_c = tn_c[1:]
        tm, tn = tm_c[0], tn_c[0]
    starts = [sum(nks[:p]) for p in range(len(pairs))]
    nk = sum(nks)
    np_ = len(pairs)

    def body(*refs):
        o_ref, acc = refs[2 * np_], refs[2 * np_ + 1]
        kk = pl.program_id(2)

        @pl.when(kk == 0)
        def _():
            acc[...] = jnp.zeros_like(acc)

        for p in range(np_):
            def add(p=p):
                a = refs[2 * p][...].astype(BF16)
                b = refs[2 * p + 1][...].astype(BF16)
                dn = (((0 if ta else 1,), (1 if tb else 0,)), ((), ()))
                acc[...] += lax.dot_general(a, b, dn, preferred_element_type=F32)
            if np_ == 1:
                add()
            else:
                pl.when((kk >= starts[p]) & (kk < starts[p] + nks[p]))(add)

        @pl.when(kk == nk - 1)
        def _():
            o_ref[...] = acc[...].astype(o_ref.dtype)

    in_specs, args = [], []
    for p, (a, b, off) in enumerate(pairs):
        tk, s0, nkp = tks[p], starts[p], nks[p]
        assert off % tk == 0
        boff = off // tk

        def kloc(k, s0=s0, nkp=nkp):
            return jnp.clip(k - s0, 0, nkp - 1)
        if ta:
            in_specs.append(pl.BlockSpec((tk, tm), lambda i, j, k, kloc=kloc: (kloc(k), i)))
        else:
            in_specs.append(pl.BlockSpec((tm, tk), lambda i, j, k, kloc=kloc: (i, kloc(k))))
        if tb:
            in_specs.append(pl.BlockSpec((tn, tk), lambda i, j, k, kloc=kloc, boff=boff: (j, boff + kloc(k))))
        else:
            in_specs.append(pl.BlockSpec((tk, tn), lambda i, j, k, kloc=kloc, boff=boff: (boff + kloc(k), j)))
        args += [a, b]
    return pl.pallas_call(
        body, grid=(m // tm, n // tn, nk), in_specs=in_specs,
        out_specs=pl.BlockSpec((tm, tn), lambda i, j, k: (i, j)),
        out_shape=jax.ShapeDtypeStruct((m, n), out_dtype),
        scratch_shapes=[pltpu.VMEM((tm, tn), F32)],
        name=name, compiler_params=_cparams(("arbitrary", "arbitrary", "arbitrary")),
    )(*args)


def _rms(x, g):
    return x * lax.rsqrt(jnp.mean(x * x, axis=-1, keepdims=True) + RMS_EPS) * g


def f_nm0(x, g, sh, sc):
    return _rms(x, g) * (1.0 + sc) + sh


def f_nm(xp, o, gate, g, sh, sc):
    x = xp + gate * o
    return x, _rms(x, g) * (1.0 + sc) + sh


def f_final(xp, o, gate, g, tgt, valid):
    x = xp + gate * o
    e = (_rms(x, g) - tgt) * valid
    return jnp.sum(e * e, axis=0, keepdims=True) * (0.5 / D)


@functools.partial(jax.custom_vjp, nondiff_argnums=(1,))
def _sroll(x, s):
    return pltpu.roll(x, s, 0)


def _sroll_fwd(x, s):
    return pltpu.roll(x, s, 0), None


def _sroll_bwd(s, _, g):
    return (pltpu.roll(g, (g.shape[0] - s) % g.shape[0], 0),)


_sroll.defvjp(_sroll_fwd, _sroll_bwd)


def _shifted(x, o):
    n = x.shape[0]
    return x if o == 0 else _sroll(x, (n - o) % n)


def f_conv1d(x, w, b, *, lc, act):
    n = x.shape[0]
    pos = lax.broadcasted_iota(jnp.int32, (n, 1), 0)
    lo = jnp.where(pos < lc, 0, lc)
    hi = jnp.where(pos < lc, lc, n)
    y = x * w[1:2] + b
    for k, o in ((0, -1), (2, 1), (3, 2)):
        src = pos + o
        valid = (src >= lo) & (src < hi)
        y = y + jnp.where(valid, _shifted(x, o), 0.0) * w[k:k + 1]
    return jax.nn.silu(y) if act else y


def ffnconv_masks(n, lc):
    pos = lax.broadcasted_iota(jnp.int32, (n, 128), 0)
    is_ctx = pos < lc
    tl = pos - lc
    r = tl // GRID_W
    cc = tl - r * GRID_W
    rows = (n - lc) // GRID_W
    left = jnp.where(is_ctx, pos >= 1, cc >= 1)
    right = jnp.where(is_ctx, pos < lc - 1, cc < GRID_W - 1)
    above = jnp.logical_not(is_ctx) & (r >= 1)
    below = jnp.logical_not(is_ctx) & (r < rows - 1)
    return jnp.stack([left, right, above, below]).astype(F32)


def f_ffnconv(a, up, w, b, mk):
    cols = (mk[0] * _shifted(a, -1), a, mk[1] * _shifted(a, 1))
    y = b
    for dr in (-1, 0, 1):
        k = 3 * (dr + 1)
        inner = cols[0] * w[k:k + 1] + cols[1] * w[k + 1:k + 2] + cols[2] * w[k + 2:k + 3]
        y = y + (inner if dr == 0 else mk[2 + (dr > 0)] * _shifted(inner, GRID_W * dr))
    return jax.nn.silu(y) * up


def f_ssd(xs, bc, dtraw, bias, alog, st, *, d, reverse):
    L = xs.shape[0]
    dtv = jax.nn.softplus(dtraw + bias)
    la = dtv * (-jnp.exp(alog))
    ri = lax.broadcasted_iota(jnp.int32, (L, L), 0)
    ci = lax.broadcasted_iota(jnp.int32, (L, L), 1)
    mask = (ci >= ri) if reverse else (ci <= ri)
    cum = jnp.dot(mask.astype(F32), la, precision=HI, preferred_element_type=F32)
    cum_t = cum.T
    tot = cum[0:1] if reverse else cum[L - 1:L]
    lo = lax.broadcasted_iota(jnp.int32, (1, 128), 1) < 64
    rlo = lax.broadcasted_iota(jnp.int32, (128, 1), 0) < 64
    ys, new = [], []
    for g in range(2):
        bg = bc[:, g * 128:(g + 1) * 128].astype(BF16)
        cg = bc[:, 256 + g * 128:256 + (g + 1) * 128].astype(BF16)
        cb = lax.dot_general(cg, bg, (((1,), (1,)), ((), ())), preferred_element_type=F32)
        sg = st[4 * g:4 * g + 4].reshape(4 * 128, 128)
        ch_all = lax.dot_general(cg, sg.astype(BF16), (((1,), (1,)), ((), ())), preferred_element_type=F32)
        xes, dcols = [], []
        for jj in range(4):
            j = 4 * g + jj
            x = xs[:, j * 128:(j + 1) * 128]
            k1 = 16 * d + 2 * j
            k2 = k1 + 1
            c1, c2 = cum[:, k1:k1 + 1], cum[:, k2:k2 + 1]
            m1 = cb * jnp.exp(jnp.where(mask, c1 - cum_t[k1:k1 + 1, :], -1e30))
            m2 = cb * jnp.exp(jnp.where(mask, c2 - cum_t[k2:k2 + 1, :], -1e30))
            xdt = x * jnp.where(lo, dtv[:, k1:k1 + 1], dtv[:, k2:k2 + 1])
            mcat = jnp.concatenate([m1, m2], axis=1).astype(BF16)
            xcat = jnp.concatenate([jnp.where(lo, xdt, 0.0), jnp.where(lo, 0.0, xdt)], axis=0).astype(BF16)
            y = jnp.dot(mcat, xcat, preferred_element_type=F32)
            y = y + ch_all[:, jj * 128:(jj + 1) * 128] * jnp.where(lo, jnp.exp(c1), jnp.exp(c2))
            t1, t2 = tot[:, k1:k1 + 1], tot[:, k2:k2 + 1]
            xes.append((xdt * jnp.where(lo, jnp.exp(t1 - c1), jnp.exp(t2 - c2))).astype(BF16))
            dcols.append(jnp.where(rlo, jnp.exp(t1), jnp.exp(t2)))
            ys.append(y)
        upd = lax.dot_general(jnp.concatenate(xes, axis=1), bg, (((0,), (0,)), ((), ())), preferred_element_type=F32)
        new.append((sg * jnp.concatenate(dcols, axis=0) + upd).reshape(4, 128, 128))
    return jnp.concatenate(ys, axis=1), jnp.concatenate(new, axis=0)


def f_hgrn(q_raw, f_raw, v, lb, zt, *, reverse, chunk):
    n = q_raw.shape[0]
    c = chunk
    qa = jax.nn.silu(q_raw)
    logf = jnp.log(lb + (1.0 - lb) * jax.nn.sigmoid(f_raw))
    kk = (1.0 - lb) * jax.nn.sigmoid(-f_raw)
    ri = lax.broadcasted_iota(jnp.int32, (n, n), 0)
    ci = lax.broadcasted_iota(jnp.int32, (n, n), 1)
    tmat = ((ri // c == ci // c) & ((ci >= ri) if reverse else (ci <= ri))).astype(F32)
    cum_all = jnp.dot(tmat, logf, precision=HI, preferred_element_type=F32)
    r3 = lax.broadcasted_iota(jnp.int32, (c, c, 128), 0)
    c3 = lax.broadcasted_iota(jnp.int32, (c, c, 128), 1)
    mask3 = (c3 >= r3) if reverse else (c3 <= r3)
    nch = n // c
    outs = [None] * nch
    for chn in (reversed(range(nch)) if reverse else range(nch)):
        sl = slice(chn * c, (chn + 1) * c)
        q, k, vv, cum = qa[sl], kk[sl], v[sl], cum_all[sl]
        dec = jnp.exp(jnp.where(mask3, cum[:, None, :] - cum[None, :, :], -1e30))
        att = jnp.sum(q[:, None, :] * dec * k[None, :, :], axis=-1, keepdims=True)
        y = jnp.sum(att * vv[None, :, :], axis=1)
        y = y + lax.dot_general((q * jnp.exp(cum)).astype(BF16), zt.astype(BF16), (((1,), (1,)), ((), ())),
                                preferred_element_type=F32)
        tot = cum[0:1] if reverse else cum[c - 1:c]
        kd = (k * jnp.exp(tot - cum)).astype(BF16)
        zt = zt * jnp.exp(tot) + lax.dot_general(vv.astype(BF16), kd, (((0,), (0,)), ((), ())), preferred_element_type=F32)
        outs[chn] = y
    return jnp.concatenate(outs, axis=0), zt


def f_hgrn_group(q_raw, f_raw, v, lb, zt, *, reverse, chunk):
    ys, zs = [], []
    for h in range(q_raw.shape[1] // 128):
        sl = slice(h * 128, (h + 1) * 128)
        y, z = f_hgrn(q_raw[:, sl], f_raw[:, sl], v[:, sl], lb[:, sl], zt[sl], reverse=reverse, chunk=chunk)
        ys.append(y)
        zs.append(z)
    return jnp.concatenate(ys, axis=1), jnp.concatenate(zs, axis=0)


def _expm1(x):
    poly = x * (1.0 + x * (0.5 + x * (1.0 / 6 + x * (1.0 / 24 + x * (1.0 / 120 + x * (1.0 / 720))))))
    return jnp.where(jnp.abs(x) < 0.3, poly, jnp.exp(x) - 1.0)


def f_gates(u, wa, ba, wi, bi, lam):
    rs, is_ = [], []
    for nb in range(8):
        un = u[:, nb * 128:(nb + 1) * 128].astype(BF16)
        rs.append(jnp.dot(un, wa[nb].astype(BF16), preferred_element_type=F32))
        is_.append(jnp.dot(un, wi[nb].astype(BF16), preferred_element_type=F32))
    r = jax.nn.sigmoid(jnp.concatenate(rs, axis=1) + ba)
    i = jax.nn.sigmoid(jnp.concatenate(is_, axis=1) + bi)
    log_a = -LRU_C * jax.nn.softplus(-lam) * r
    return jnp.exp(log_a), jnp.sqrt(-_expm1(2.0 * log_a)) * (i * u)


def f_ssdfin(y0, y1, xs, z, gy, h0, h1, dpad, ng):
    kk = lax.broadcasted_iota(jnp.int32, (128, D), 0)
    ch = lax.broadcasted_iota(jnp.int32, (128, D), 1)
    expand = (ch // 64 == kk).astype(F32)
    dvec = jnp.dot(dpad, expand, precision=HI, preferred_element_type=F32)[0:1]
    y = y0 + y1 + dvec * xs
    yn = _rms(y * jax.nn.silu(z), ng)
    r = (h0 + h1) * jax.nn.gelu(gy)
    return jnp.concatenate([yn, r], axis=1)


def f_oddfin(o0, o1, g, y0, y1, u, hn, sd, gw, gb):
    parts = []
    for h in range(HG_HEADS):
        sl = slice(h * 128, (h + 1) * 128)
        parts.append(_rms(o0[:, sl] + o1[:, sl], hn[h:h + 1]) * jax.nn.silu(g[:, sl]))
    y = jax.nn.gelu(y0 + y1 + sd * u)
    y = y * jax.nn.sigmoid(jnp.dot(y.astype(BF16), gw.astype(BF16), preferred_element_type=F32) + gb)
    return jnp.concatenate(parts + [y], axis=1)


def f_s5p(lre, lim, lstep, btr, bti):
    step = jnp.exp(lstep)
    mag = jnp.exp(lre * step)
    ar, ai = mag * jnp.cos(lim * step), mag * jnp.sin(lim * step)
    den = lre * lre + lim * lim
    zr = ((ar - 1.0) * lre + ai * lim) / den
    zi = (ai * lre - (ar - 1.0) * lim) / den
    bbr = zr[:, None, :] * btr - zi[:, None, :] * bti
    bbi = zr[:, None, :] * bti + zi[:, None, :] * btr
    return ar, ai, bbr, bbi


def f_lb(logits):
    m = jnp.max(logits, axis=0, keepdims=True)
    e = jnp.exp(logits - m)
    p = e / jnp.sum(e, axis=0, keepdims=True)
    return p[1:2], p[1:2] + p[2:3] + p[3:4]


def f_adamw(w, m, v, *gs):
    g = gs[0]
    for t in gs[1:]:
        g = g + t
    m = ADAM_B1 * m + (1.0 - ADAM_B1) * g
    v = ADAM_B2 * v + (1.0 - ADAM_B2) * jnp.square(g)
    m_hat = m / (1.0 - ADAM_B1 ** ADAM_STEP)
    v_hat = v / (1.0 - ADAM_B2 ** ADAM_STEP)
    delta = -ADAM_LR * (m_hat / (jnp.sqrt(v_hat) + ADAM_EPS) + ADAM_WD * w)
    return g, delta, m, v


def scan_fwd(name, f, grid, ins, y_out, st_out, state_shape, is_first):
    n_in = len(ins)

    def body(*refs):
        y_ref, so_ref, st = refs[n_in], refs[n_in + 1], refs[n_in + 2]
        ids = [pl.program_id(a) for a in range(len(grid))]

        @pl.when(is_first(*ids))
        def _():
            st[...] = jnp.zeros_like(st)

        s = st[...]
        so_ref[...] = s
        y, new = f(*[r[...] for r in refs[:n_in]], s)
        y_ref[...] = y.astype(y_ref.dtype)
        st[...] = new

    return pl.pallas_call(
        body, grid=grid,
        in_specs=[pl.BlockSpec(b, m) for _, b, m in ins],
        out_specs=[pl.BlockSpec(y_out[2], y_out[3]), pl.BlockSpec(st_out[2], st_out[3])],
        out_shape=[jax.ShapeDtypeStruct(y_out[0], y_out[1]), jax.ShapeDtypeStruct(st_out[0], st_out[1])],
        scratch_shapes=[pltpu.VMEM(state_shape, F32)],
        name=name, compiler_params=_cparams(("arbitrary",) * len(grid)),
    )(*[a for a, _, _ in ins])


def scan_bwd(name, f, grid, ins, st_in, dy, grads, state_shape, is_first):
    n_in = len(ins)
    didx = [g[0] for g in grads]

    def body(*refs):
        s_ref, dy_ref = refs[n_in], refs[n_in + 1]
        g_refs = refs[n_in + 2:n_in + 2 + len(grads)]
        dst = refs[n_in + 2 + len(grads)]
        ids = [pl.program_id(a) for a in range(len(grid))]

        @pl.when(is_first(*ids))
        def _():
            dst[...] = jnp.zeros_like(dst)

        vals = [r[...] for r in refs[:n_in]]

        def fd(s, *dv):
            full = list(vals)
            for i, v in zip(didx, dv):
                full[i] = v
            return f(*full, s)

        (y, _), vjp = jax.vjp(fd, s_ref[...], *[vals[i] for i in didx])
        gs = vjp((dy_ref[...].astype(y.dtype), dst[...]))
        dst[...] = gs[0]
        for g, spec, ref in zip(gs[1:], grads, g_refs):
            first = spec[5]
            if first is None:
                ref[...] = g.astype(ref.dtype)
            else:
                fst = first(*ids)

                @pl.when(fst)
                def _(ref=ref, g=g):
                    ref[...] = g.astype(ref.dtype)

                @pl.when(jnp.logical_not(fst))
                def _(ref=ref, g=g):
                    ref[...] += g.astype(ref.dtype)

    allin = list(ins) + [st_in, dy]
    return pl.pallas_call(
        body, grid=grid,
        in_specs=[pl.BlockSpec(b, m) for _, b, m in allin],
        out_specs=[pl.BlockSpec(s[3], s[4]) for s in grads],
        out_shape=[jax.ShapeDtypeStruct(s[1], s[2]) for s in grads],
        scratch_shapes=[pltpu.VMEM(state_shape, F32)],
        name=name, compiler_params=_cparams(("arbitrary",) * len(grid)),
    )(*[a for a, _, _ in allin])


def _tile_order(order, nt, nctx=1):
    rev = lambda j: jnp.where(j < nctx, nctx - 1 - j, nt - 1 - (j - nctx))
    if order == 'F':
        return (lambda j: j), True
    if order == 'Fb':
        return (lambda j: nt - 1 - j), False
    if order == 'R':
        return rev, False
    return (lambda j: rev(nt - 1 - j)), True


def _scan8(coef, val, sub, asc):
    for step in (1, 2, 4):
        shift = step if asc else 8 - step
        keep = (sub >= step) if asc else (sub < 8 - step)
        val = jnp.where(keep, coef * pltpu.roll(val, shift, 0) + val, val)
        coef = jnp.where(keep, coef * pltpu.roll(coef, shift, 0), coef)
    return coef, val


def _prev_rows(tile, carry, sub, asc):
    return jnp.where(sub == 0, carry, pltpu.roll(tile, 1, 0)) if asc else jnp.where(sub == 7, carry, pltpu.roll(tile, 7, 0))


def _last_row(tile, asc):
    return jnp.broadcast_to(tile[7:8] if asc else tile[0:1], tile.shape)


def linrec(name, a, b, order):
    bsz, tt, cols = a.shape
    tq = _pick(tt, (256, 128))
    nt, ng, nj = tt // tq, tq // 8, cols // 128
    phys, asc = _tile_order(order, nt)

    def body(a_ref, b_ref, h_ref, hp_ref, hc):
        @pl.when(pl.program_id(0) == 0)
        def _():
            hc[...] = jnp.zeros_like(hc)

        sub = lax.broadcasted_iota(jnp.int32, (8, 128), 0)

        def group(i, carry):
            rows = pl.ds(pl.multiple_of((i if asc else ng - 1 - i) * 8, 8), 8)
            for bi in range(bsz):
                for j in range(nj):
                    cs = slice(j * 128, (j + 1) * 128)
                    h_in = hc[bi, j]
                    ca, cv = _scan8(a_ref[bi, rows, cs], b_ref[bi, rows, cs], sub, asc)
                    h = ca * h_in + cv
                    h_ref[bi, rows, cs] = h
                    hp_ref[bi, rows, cs] = _prev_rows(h, h_in, sub, asc)
                    hc[bi, j] = _last_row(h, asc)
            return carry

        lax.fori_loop(0, ng, group, 0)

    spec = pl.BlockSpec((bsz, tq, cols), lambda j: (0, phys(j), 0))
    return pl.pallas_call(
        body, grid=(nt,), in_specs=[spec, spec], out_specs=[spec, spec],
        out_shape=[jax.ShapeDtypeStruct(a.shape, F32)] * 2,
        scratch_shapes=[pltpu.VMEM((bsz, nj, 8, 128), F32)],
        name=name, compiler_params=_cparams(("arbitrary",)),
    )(a, b)


def linrec_bwd(name, a, dh, hprev, order):
    bsz, tt, cols = a.shape
    tq = _pick(tt, (256, 128))
    nt, ng, nj = tt // tq, tq // 8, cols // 128
    phys, asc = _tile_order(order, nt)

    def body(a_ref, dh_ref, hp_ref, g_ref, ga_ref, gc, ac):
        @pl.when(pl.program_id(0) == 0)
        def _():
            gc[...] = jnp.zeros_like(gc)
            ac[...] = jnp.zeros_like(ac)

        sub = lax.broadcasted_iota(jnp.int32, (8, 128), 0)

        def group(i, carry):
            rows = pl.ds(pl.multiple_of((i if asc else ng - 1 - i) * 8, 8), 8)
            for bi in range(bsz):
                for j in range(nj):
                    cs = slice(j * 128, (j + 1) * 128)
                    a_tile = a_ref[bi, rows, cs]
                    ca, cv = _scan8(_prev_rows(a_tile, ac[bi, j], sub, asc), dh_ref[bi, rows, cs], sub, asc)
                    g = ca * gc[bi, j] + cv
                    g_ref[bi, rows, cs] = g
                    ga_ref[bi, rows, cs] = g * hp_ref[bi, rows, cs]
                    gc[bi, j] = _last_row(g, asc)
                    ac[bi, j] = _last_row(a_tile, asc)
            return carry

        lax.fori_loop(0, ng, group, 0)

    spec = pl.BlockSpec((bsz, tq, cols), lambda j: (0, phys(j), 0))
    return pl.pallas_call(
        body, grid=(nt,), in_specs=[spec, spec, spec], out_specs=[spec, spec],
        out_shape=[jax.ShapeDtypeStruct(a.shape, F32)] * 2,
        scratch_shapes=[pltpu.VMEM((bsz, nj, 8, 128), F32), pltpu.VMEM((bsz, nj, 8, 128), F32)],
        name=name, compiler_params=_cparams(("arbitrary",)),
    )(a, dh, hprev)


def _cmul(a, b):
    return a[0] * b[0] - a[1] * b[1], a[0] * b[1] + a[1] * b[0]


def _cpow_tables(ar, ai, asc):
    pows = [(ar, ai)]
    for _ in range(7):
        pows.append(_cmul(pows[-1], (ar, ai)))
    tile = lambda p: jnp.broadcast_to(p[:, None, :], (8, 8, 128))
    steps = jnp.stack([jnp.stack([tile(pows[s - 1][0]), tile(pows[s - 1][1])]) for s in (1, 2, 4)])
    order = range(8) if asc else range(7, -1, -1)
    carry = jnp.stack([jnp.stack([pows[i][c] for i in order], axis=1) for c in (0, 1)])
    return steps, carry


def _cscan8(xr, xi, st_ref, j, sub, asc):
    for s, step in enumerate((1, 2, 4)):
        shift = step if asc else 8 - step
        keep = (sub >= step) if asc else (sub < 8 - step)
        pr, pi = st_ref[s, 0, j], st_ref[s, 1, j]
        rr, ri = pltpu.roll(xr, shift, 0), pltpu.roll(xi, shift, 0)
        xr, xi = jnp.where(keep, xr + pr * rr - pi * ri, xr), jnp.where(keep, xi + pr * ri + pi * rr, xi)
    return xr, xi


def clinrec(name, x, coef, order, lc, conj=False, hprev=None):
    btot, tt, cols2 = x.shape
    cols = cols2 // 2
    bsz, ngrp = btot, 1
    tq = 128
    nt, ng, nj = tt // tq, tq // 8, cols // 128
    phys, asc = _tile_order(order, nt, lc // tq)
    steps, carry = _cpow_tables(coef[0], -coef[1] if conj else coef[1], asc)
    adjoint = hprev is not None

    def body(*refs):
        if adjoint:
            x_ref, hp_ref, st_ref, cr_ref, h_ref, dc_ref, hc = refs
        else:
            x_ref, st_ref, cr_ref, h_ref, hp_ref, hc = refs

        @pl.when(pl.program_id(1) == 0)
        def _():
            hc[...] = jnp.zeros_like(hc)

        if adjoint:
            @pl.when((pl.program_id(0) == 0) & (pl.program_id(1) == 0))
            def _():
                dc_ref[...] = jnp.zeros_like(dc_ref)

        sub = lax.broadcasted_iota(jnp.int32, (8, 128), 0)

        def group(i, c_):
            rows = pl.ds(pl.multiple_of((i if asc else ng - 1 - i) * 8, 8), 8)
            for bi in range(bsz):
                for j in range(nj):
                    cr, ci = slice(j * 128, (j + 1) * 128), slice(cols + j * 128, cols + (j + 1) * 128)
                    sr, si = _cscan8(x_ref[bi, rows, cr], x_ref[bi, rows, ci], st_ref, j, sub, asc)
                    in_r, in_i = hc[bi, 0, j], hc[bi, 1, j]
                    pr, pi = cr_ref[0, j], cr_ref[1, j]
                    hr = sr + pr * in_r - pi * in_i
                    hi = si + pr * in_i + pi * in_r
                    h_ref[bi, rows, cr] = hr
                    h_ref[bi, rows, ci] = hi
                    if adjoint:
                        qr, qi = hp_ref[bi, rows, cr], hp_ref[bi, rows, ci]
                        dc_ref[0, j] += hr * qr + hi * qi
                        dc_ref[1, j] += hi * qr - hr * qi
                    else:
                        hp_ref[bi, rows, cr] = _prev_rows(hr, in_r, sub, asc)
                        hp_ref[bi, rows, ci] = _prev_rows(hi, in_i, sub, asc)
                    hc[bi, 0, j] = _last_row(hr, asc)
                    hc[bi, 1, j] = _last_row(hi, asc)
            return c_

        lax.fori_loop(0, ng, group, 0)

    spec = pl.BlockSpec((bsz, tq, cols2), lambda g, j: (g, phys(j), 0))
    full = lambda t: pl.BlockSpec(t.shape, lambda g, j, n=t.ndim: (0,) * n)
    dc_shape = (2, nj, 8, 128)
    if adjoint:
        ins, in_specs = (x, hprev, steps, carry), [spec, spec, full(steps), full(carry)]
        out_specs = [spec, pl.BlockSpec(dc_shape, lambda g, j: (0, 0, 0, 0))]
        out_shape = [jax.ShapeDtypeStruct(x.shape, F32), jax.ShapeDtypeStruct(dc_shape, F32)]
    else:
        ins, in_specs = (x, steps, carry), [spec, full(steps), full(carry)]
        out_specs = [spec, spec]
        out_shape = [jax.ShapeDtypeStruct(x.shape, F32)] * 2
    return pl.pallas_call(
        body, grid=(ngrp, nt), in_specs=in_specs, out_specs=out_specs, out_shape=out_shape,
        scratch_shapes=[pltpu.VMEM((bsz, 2, nj, 8, 128), F32)],
        name=name, compiler_params=_cparams(("arbitrary", "arbitrary")),
    )(*ins)


def _blockdiag(bb):
    eye = jnp.eye(16, dtype=bb.dtype)
    return (bb[:, :, None, :] * eye[:, None, :, None]).reshape(256, 1024)


def _blockdiag_t(c):
    eye = jnp.eye(16, dtype=c.dtype)
    return (jnp.swapaxes(c, 1, 2)[:, :, None, :] * eye[:, None, :, None]).reshape(1024, 256)


def _unblockdiag(m):
    eye = jnp.eye(16, dtype=m.dtype)
    return jnp.sum(m.reshape(16, 16, 16, 64) * eye[:, None, :, None], axis=2)


def _unblockdiag_t(m):
    eye = jnp.eye(16, dtype=m.dtype)
    return jnp.swapaxes(jnp.sum(m.reshape(16, 64, 16, 16) * eye[:, None, :, None], axis=2), 1, 2)


def _pad_rows(v, rows=8, cols=128):
    out = jnp.zeros((rows, cols), F32)
    return out.at[0, :v.shape[0]].set(v)


def local_step(x, c, ctx, c_ctx, target, W):
    B, Tx, _ = x.shape
    Lc = ctx.shape[1]
    Tt = Lc + Tx
    tb = Lc
    nt = Tt // tb
    M = B * Tt
    nc = Tt // SSD_CHUNK
    ncc = Lc // SSD_CHUNK
    fgr, fgw = HG_HEADS // HG_FWD[1], 128 * HG_FWD[1]
    bgr, bgw = HG_HEADS // HG_BWD[1], 128 * HG_BWD[1]
    G = {}

    def add_grad(name, idx, val):
        G.setdefault(name, {})[idx] = val

    def tok(a, cb=None, off=0):
        cb = a.shape[-1] if cb is None else cb
        return (a, (None, tb, cb), lambda b, j, off=off: (b, j, off))

    def tok_out(cols, dtype=F32):
        return ((B, Tt, cols), dtype, (None, tb, cols), lambda b, j: (b, j, 0))

    def vec(a):
        return (a, a.shape, lambda *ids, n=a.ndim: (0,) * n)

    def vec_acc(shape):
        return (shape, F32, shape, lambda *ids, n=len(shape): (0,) * n, lambda *ids: functools.reduce(jnp.logical_and, [i == 0 for i in ids]))

    def modv(l, which):
        return (modr, (None, None, None, 1, D), lambda b, j, l=l, which=which: (l, jnp.where(j == 0, 4, b), which, 0, 0))

    dmod_spec = ((B, 2, 1, D), F32, (None, None, 1, D), lambda b, j: (b, jnp.where(j == 0, 0, 1), 0, 0), lambda b, j: j <= 1)

    def phys_chunk(n_all, n_ctx, reverse):
        if not reverse:
            return lambda s: s
        return lambda s: jnp.where(s < n_ctx, n_ctx - 1 - s, n_all - 1 - (s - n_ctx))

    cc = jnp.zeros((8, D), F32).at[:B].set(c).at[4].set(c_ctx)
    nmc = N_MOD * D // 1536

    def f_mod(ccv, w, b):
        return jnp.dot(jax.nn.silu(ccv).astype(BF16), w, preferred_element_type=F32) + b

    mod = tile_fwd('mod_fwd', f_mod, (DEPTH, nmc),
                   [(cc, (8, D), lambda l, n: (0, 0)), (W['w_mod'], (None, D, 1536), lambda l, n: (l, 0, n)),
                    (W['b_mod'].reshape(DEPTH, 1, N_MOD * D), (None, 1, 1536), lambda l, n: (l, 0, n))],
                   [((DEPTH, 8, N_MOD * D), F32, (None, 8, 1536), lambda l, n: (l, 0, n))])[0]
    modr = mod.reshape(DEPTH, 8, N_MOD, 1, D)
    dmods = {}

    lb1, lb3 = tile_fwd('lb_fwd', f_lb, (1,), [vec(W['hg_lb_logits'])],
                        [((1, HG_W), F32, (1, HG_W), lambda i: (0, 0))] * 2)
    dlb = {1: [], 3: []}

    x0 = jnp.concatenate([ctx, x], axis=1)
    conv_mk = ffnconv_masks(Tt, Lc)
    R = [dict() for _ in range(DEPTH)]

    xprev, oprev = x0, None
    for l in range(DEPTH):
        r = R[l]
        j = l // 2
        ng = W['norm_mix_g'][l][None]
        if l == 0:
            h1 = tile_fwd(f'nm0_fwd', f_nm0, (B, nt), [tok(xprev), vec(ng), modv(l, 0), modv(l, 1)], [tok_out(D, BF16)])[0]
            xa = xprev
        else:
            xa, h1 = tile_fwd(f'nm_mix_fwd{l}', f_nm, (B, nt),
                              [tok(xprev), tok(oprev), modv(l - 1, 5), vec(ng), modv(l, 0), modv(l, 1)],
                              [tok_out(D), tok_out(D, BF16)])
        r['xin'], r['oin'], r['xa'], r['h1'] = xprev, oprev, xa, h1
        h1m = h1.reshape(M, D)
        if l % 2 == 0:
            win = W['ev_w_in_p'][j]
            proj = mm(f'ev_proj{l}', [(h1m, win, 0)]).reshape(B, Tt, EV_PAD)
            r['proj'] = proj
            scw, scb = W['ssd_conv_w'][j], W['ssd_conv_b'][j][None]
            lcw, lcb = W['lru_conv_w'][j], W['lru_conv_b'][j][None]

            def conv_call(name, colblk0, w, b, wblk0, ncols, act):
                return tile_fwd(name, functools.partial(f_conv1d, lc=Lc, act=act), (ncols // 256, B),
                                [(proj, (None, Tt, 256), lambda cb, bi: (bi, 0, colblk0 + cb)),
                                 (w, (4, 256), lambda cb, bi: (0, wblk0 + cb)), (b, (1, 256), lambda cb, bi: (0, wblk0 + cb))],
                                [((B, Tt, ncols), F32, (None, Tt, 256), lambda cb, bi: (bi, 0, cb))])[0]
            xs_c = conv_call(f'conv_xs{l}', 12, scw, scb, 0, 1024, True)
            bc_c = conv_call(f'conv_bc{l}', 16, scw, scb, 4, 512, True)
            u_c = conv_call(f'conv_u{l}', 8, lcw, lcb, 0, 1024, False)
            r['xs'], r['bc'], r['u'] = xs_c, bc_c, u_c
            bias = _pad_rows(W['ssd_dt_bias'][j].reshape(-1), 1)
            alog = _pad_rows(W['ssd_a_log'][j].reshape(-1), 1)
            r['bias'], r['alog'] = bias, alog
            r['y'], r['st'], r['a4'], r['hp4'], r['h'] = [], [], [], [], []
            for d in range(2):
                ph = phys_chunk(nc, ncc, d == 1)
                y, st = scan_fwd(
                    f'ssd_fwd{l}_{d}', functools.partial(f_ssd, d=d, reverse=(d == 1)), (B, nc),
                    [(xs_c, (None, SSD_CHUNK, D), lambda b, s, ph=ph: (b, ph(s), 0)),
                     (bc_c, (None, SSD_CHUNK, 512), lambda b, s, ph=ph: (b, ph(s), 0)),
                     (proj, (None, SSD_CHUNK, 128), lambda b, s, ph=ph: (b, ph(s), 36)),
                     vec(bias), vec(alog)],
                    ((B, Tt, D), F32, (None, SSD_CHUNK, D), lambda b, s, ph=ph: (b, ph(s), 0)),
                    ((B, nc, 8, 128, 128), F32, (None, None, 8, 128, 128), lambda b, s: (b, s, 0, 0, 0)),
                    (8, 128, 128), lambda b, s: s == 0)
                r['y'].append(y)
                r['st'].append(st)
                a_d, bx_d = tile_fwd(
                    f'gates_fwd{l}_{d}', f_gates, (B, nt),
                    [tok(u_c), vec(W['lru_w_a'][j, d]), vec(W['lru_b_a'][j, d][None]), vec(W['lru_w_i'][j, d]),
                     vec(W['lru_b_i'][j, d][None]), vec(W['lru_lam'][j, d][None])],
                    [tok_out(D), tok_out(D)])
                h_d, hp_d = linrec(f'lru_fwd{l}_{d}', a_d, bx_d, 'F' if d == 0 else 'R')
                r['a4'].append(a_d)
                r['hp4'].append(hp_d)
                r['h'].append(h_d)
            dpad = _pad_rows(W['ssd_d'][j])
            sng = W['ssd_norm_g'][j][None]
            r['dpad'], r['sng'] = dpad, sng
            mix = tile_fwd(f'ssdfin_fwd{l}', f_ssdfin, (B, nt),
                           [tok(r['y'][0]), tok(r['y'][1]), tok(xs_c), tok(proj, D, 0), tok(proj, D, 1), tok(r['h'][0]),
                            tok(r['h'][1]), vec(dpad), vec(sng)], [tok_out(2 * D, BF16)])[0]
            wout = W['ev_w_out'][j]
        else:
            win = W['od_w_in'][j]
            proj = mm(f'od_proj{l}', [(h1m, win, 0)]).reshape(B, Tt, 4096)
            r['proj'] = proj
            lbv = lb1 if l == 1 else lb3
            ns = nc
            r['o'], r['zst'], r['coef'], r['bcat'], r['ccat'], r['hp5'], r['hcat'], r['yd'], r['s5in'] = [], [], [], [], [], [], [], [], []
            u2 = proj[:, :, 3840:].reshape(M, S5_W)
            r['u2'] = u2
            for d in range(2):
                ph = phys_chunk(ns, ncc, d == 1)
                o_d, zst = scan_fwd(
                    f'hgrn_fwd{l}_{d}', functools.partial(f_hgrn_group, reverse=(d == 1), chunk=HG_FWD[0]), (fgr, B, ns),
                    [(proj, (None, 128, fgw), lambda h, b, s, ph=ph: (b, ph(s), h)),
                     (proj, (None, 128, fgw), lambda h, b, s, ph=ph, d=d: (b, ph(s), (1 + d) * fgr + h)),
                     (proj, (None, 128, fgw), lambda h, b, s, ph=ph: (b, ph(s), 3 * fgr + h)),
                     (lbv, (1, fgw), lambda h, b, s: (0, h))],
                    ((B, Tt, HG_W), F32, (None, 128, fgw), lambda h, b, s, ph=ph: (b, ph(s), h)),
                    ((B, ns, fgr, fgw, 128), F32, (None, None, None, fgw, 128), lambda h, b, s: (b, s, h, 0, 0)),
                    (fgw, 128), lambda h, b, s: s == 0)
                r['o'].append(o_d)
                r['zst'].append(zst)
                s5in = [W['s5_lam_re'][j, d], W['s5_lam_im'][j, d], W['s5_log_step'][j, d].reshape(16, 1),
                        jnp.swapaxes(W['s5_b_re'][j], 1, 2), jnp.swapaxes(W['s5_b_im'][j], 1, 2)]
                r['s5in'].append(s5in)
                ar, ai, bbr, bbi = tile_fwd(f's5p_fwd{l}_{d}', f_s5p, (1,), [vec(t) for t in s5in],
                                            [((16, 64), F32, (16, 64), lambda i: (0, 0))] * 2
                                            + [((16, 16, 64), F32, (16, 16, 64), lambda i: (0, 0, 0))] * 2)
                coef = jnp.stack([ar.reshape(8, 128), ai.reshape(8, 128)])
                bcat = jnp.concatenate([_blockdiag(bbr), _blockdiag(bbi)], axis=1).astype(BF16)
                ccat = jnp.concatenate([_blockdiag_t(W['s5_c_re'][j, d]), -_blockdiag_t(W['s5_c_im'][j, d])], axis=0).astype(BF16)
                xcat = mm(f's5_in{l}_{d}', [(u2, bcat, 0)])
                h5, hp5 = clinrec(f's5_fwd{l}_{d}', xcat.reshape(B, Tt, 2 * D), coef, 'F' if d == 0 else 'R', Lc)
                hcat = h5.reshape(M, 2 * D)
                yd = mm(f's5_out{l}_{d}', [(hcat, ccat, 0)]).reshape(B, Tt, S5_W)
                r['coef'].append(coef)
                r['bcat'].append(bcat)
                r['ccat'].append(ccat)
                r['hp5'].append(hp5)
                r['hcat'].append(hcat)
                r['yd'].append(yd)
            hn = jnp.zeros((8, 128), F32).at[:HG_HEADS].set(W['hg_norm_g'][j])
            sd, gw, gb = W['s5_d'][j][None], W['s5_glu_w'][j], W['s5_glu_b'][j][None]
            r['fin_par'] = (hn, sd, gw, gb)
            mix = tile_fwd(f'oddfin_fwd{l}', f_oddfin, (B, nt),
                           [tok(r['o'][0]), tok(r['o'][1]), tok(proj, HG_W, 4), tok(r['yd'][0]), tok(r['yd'][1]),
                            tok(proj, S5_W, 15), vec(hn), vec(sd), vec(gw), vec(gb)], [tok_out(D, BF16)])[0]
            wout = W['od_w_out'][j]
        r['mix'] = mix
        o1 = mm(f'mix_out{l}', [(mix.reshape(M, -1), wout, 0)]).reshape(B, Tt, D)
        r['o1'] = o1
        fg = W['norm_ffn_g'][l][None]
        xb, h2 = tile_fwd(f'nm_ffn_fwd{l}', f_nm, (B, nt), [tok(xa), tok(o1), modv(l, 2), vec(fg), modv(l, 3), modv(l, 4)],
                          [tok_out(D), tok_out(D, BF16)])
        r['h2'] = h2
        h2m = h2.reshape(M, D)
        a = mm(f'ffn_gate{l}', [(h2m, W['ffn_w_gate'][l], 0)]).reshape(B, Tt, D_FF)
        up = mm(f'ffn_up{l}', [(h2m, W['ffn_w_up'][l], 0)]).reshape(B, Tt, D_FF)
        w9 = W['ffn_conv_w'][l].reshape(9, D_FF)
        cbias = W['ffn_conv_b'][l][None]
        r['a'], r['up'], r['w9'], r['cbias'] = a, up, w9, cbias
        act = tile_fwd(f'ffnconv_fwd{l}', f_ffnconv, (D_FF // 128, B),
                       [(a, (None, Tt, 128), lambda cb, bi: (bi, 0, cb)), (up, (None, Tt, 128), lambda cb, bi: (bi, 0, cb)),
                        (w9, (9, 128), lambda cb, bi: (0, cb)), (cbias, (1, 128), lambda cb, bi: (0, cb)), vec(conv_mk)],
                       [((B, Tt, D_FF), BF16, (None, Tt, 128), lambda cb, bi: (bi, 0, cb))])[0]
        r['act'] = act
        o2 = mm(f'ffn_down{l}', [(act.reshape(M, D_FF), W['ffn_w_down'][l], 0)]).reshape(B, Tt, D)
        xprev, oprev = xb, o2

    vmask = jnp.ones((nt, 1, D), F32).at[0].set(0.0)
    ones = jnp.ones((1, D), F32)
    fng = W['final_norm_g'][None]
    d_xp, d_o2, dg5, dfng, loss_vec = tile_bwd(
        'loss_head', f_final, (B, nt),
        [tok(xprev), tok(oprev), modv(DEPTH - 1, 5), vec(fng),
         (target, (None, tb, D), lambda b, j: (b, jnp.maximum(j - 1, 0), 0)), (vmask, (None, 1, D), lambda b, j: (j, 0, 0))],
        [[vec(ones)]],
        [(0,) + tok_out(D) + (None,), (1,) + tok_out(D, BF16) + (None,), (2,) + dmod_spec, (3,) + vec_acc((1, D))],
        prims=[(0,) + vec_acc((1, D))])
    loss = jnp.sum(loss_vec)
    add_grad('final_norm_g', None, dfng[0])
    dmods[(DEPTH - 1, 5)] = dg5

    for l in reversed(range(DEPTH)):
        r = R[l]
        j = l // 2
        d_o2m = d_o2.reshape(M, D)
        d_act = mm(f'ffn_down_dx{l}', [(d_o2m, W['ffn_w_down'][l], 0)], tb=True).reshape(B, Tt, D_FF)
        add_grad('ffn_w_down', l, mm(f'ffn_down_dw{l}', [(r['act'].reshape(M, D_FF), d_o2m, 0)], ta=True))
        d_a, d_up, dw9, dcb = tile_bwd(
            f'ffnconv_bwd{l}', f_ffnconv, (D_FF // 128, B),
            [(r['a'], (None, Tt, 128), lambda cb, bi: (bi, 0, cb)), (r['up'], (None, Tt, 128), lambda cb, bi: (bi, 0, cb)),
             (r['w9'], (9, 128), lambda cb, bi: (0, cb)), (r['cbias'], (1, 128), lambda cb, bi: (0, cb)), vec(conv_mk)],
            [[(d_act, (None, Tt, 128), lambda cb, bi: (bi, 0, cb))]],
            [(0, (B, Tt, D_FF), BF16, (None, Tt, 128), lambda cb, bi: (bi, 0, cb), None),
             (1, (B, Tt, D_FF), BF16, (None, Tt, 128), lambda cb, bi: (bi, 0, cb), None),
             (2, (9, D_FF), F32, (9, 128), lambda cb, bi: (0, cb), lambda cb, bi: bi == 0),
             (3, (1, D_FF), F32, (1, 128), lambda cb, bi: (0, cb), lambda cb, bi: bi == 0)])
        add_grad('ffn_conv_w', l, dw9.reshape(3, 3, D_FF))
        add_grad('ffn_conv_b', l, dcb[0])
        d_am, d_upm = d_a.reshape(M, D_FF), d_up.reshape(M, D_FF)
        h2m = r['h2'].reshape(M, D)
        d_h2 = mm(f'ffn_in_dx{l}', [(d_am, W['ffn_w_gate'][l], 0), (d_upm, W['ffn_w_up'][l], 0)], tb=True).reshape(B, Tt, D)
        add_grad('ffn_w_gate', l, mm(f'ffn_gate_dw{l}', [(h2m, d_am, 0)], ta=True))
        add_grad('ffn_w_up', l, mm(f'ffn_up_dw{l}', [(h2m, d_upm, 0)], ta=True))
        fg = W['norm_ffn_g'][l][None]
        d_xa, d_o1, dgate, dfg, dsh, dsc = tile_bwd(
            f'nm_ffn_bwd{l}', f_nm, (B, nt), [tok(r['xa']), tok(r['o1']), modv(l, 2), vec(fg), modv(l, 3), modv(l, 4)],
            [[tok(d_xp)], [tok(d_h2)]],
            [(0,) + tok_out(D) + (None,), (1,) + tok_out(D, BF16) + (None,), (2,) + dmod_spec, (3,) + vec_acc((1, D)),
             (4,) + dmod_spec, (5,) + dmod_spec])
        add_grad('norm_ffn_g', l, dfg[0])
        dmods[(l, 2)], dmods[(l, 3)], dmods[(l, 4)] = dgate, dsh, dsc
        d_o1m = d_o1.reshape(M, D)
        h1m = r['h1'].reshape(M, D)
        proj = r['proj']
        if l % 2 == 0:
            wout, win = W['ev_w_out'][j], W['ev_w_in_p'][j]
            d_mix = mm(f'mix_out_dx{l}', [(d_o1m, wout, 0)], tb=True).reshape(B, Tt, 2 * D)
            add_grad('ev_w_out', j, mm(f'mix_out_dw{l}', [(r['mix'].reshape(M, 2 * D), d_o1m, 0)], ta=True))
            d_y, d_xs_fin, d_z, d_gy, d_h, ddpad, dsng = tile_bwd(
                f'ssdfin_bwd{l}', f_ssdfin, (B, nt),
                [tok(r['y'][0]), tok(r['y'][1]), tok(r['xs']), tok(proj, D, 0), tok(proj, D, 1), tok(r['h'][0]), tok(r['h'][1]),
                 vec(r['dpad']), vec(r['sng'])],
                [[tok(d_mix)]],
                [(0,) + tok_out(D) + (None,), (2,) + tok_out(D) + (None,), (3,) + tok_out(D) + (None,), (4,) + tok_out(D) + (None,),
                 (5,) + tok_out(D) + (None,), (7,) + vec_acc((8, 128)), (8,) + vec_acc((1, D))])
            add_grad('ssd_d', j, ddpad[0, :SSD_HEADS])
            add_grad('ssd_norm_g', j, dsng[0])
            d_xs_parts, d_bc_parts, d_dt_parts, d_u_parts = [d_xs_fin], [], [], []
            dbias_t, dalog_t = [], []
            dh4 = d_h
            for d in range(2):
                ph0 = phys_chunk(nc, ncc, d == 1)

                def ph(s, ph0=ph0):
                    return ph0(nc - 1 - s)
                dxs_d, dbc_d, ddt_d, dbias, dalog = scan_bwd(
                    f'ssd_bwd{l}_{d}', functools.partial(f_ssd, d=d, reverse=(d == 1)), (B, nc),
                    [(r['xs'], (None, SSD_CHUNK, D), lambda b, s, ph=ph: (b, ph(s), 0)),
                     (r['bc'], (None, SSD_CHUNK, 512), lambda b, s, ph=ph: (b, ph(s), 0)),
                     (proj, (None, SSD_CHUNK, 128), lambda b, s, ph=ph: (b, ph(s), 36)),
                     vec(r['bias']), vec(r['alog'])],
                    (r['st'][d], (None, None, 8, 128, 128), lambda b, s: (b, nc - 1 - s, 0, 0, 0)),
                    (d_y, (None, SSD_CHUNK, D), lambda b, s, ph=ph: (b, ph(s), 0)),
                    [(0, (B, Tt, D), F32, (None, SSD_CHUNK, D), lambda b, s, ph=ph: (b, ph(s), 0), None),
                     (1, (B, Tt, 512), F32, (None, SSD_CHUNK, 512), lambda b, s, ph=ph: (b, ph(s), 0), None),
                     (2, (B, Tt, 128), F32, (None, SSD_CHUNK, 128), lambda b, s, ph=ph: (b, ph(s), 0), None),
                     (3,) + vec_acc((1, 128)), (4,) + vec_acc((1, 128))],
                    (8, 128, 128), lambda b, s: s == 0)
                d_xs_parts.append(dxs_d)
                d_bc_parts.append(dbc_d)
                d_dt_parts.append(ddt_d)
                dbias_t.append(dbias)
                dalog_t.append(dalog)
                g4, ga4 = linrec_bwd(f'lru_bwd{l}_{d}', r['a4'][d], dh4, r['hp4'][d], 'Fb' if d == 0 else 'Rb')
                du_g, dwa, dba, dwi, dbi, dlam = tile_bwd(
                    f'gates_bwd{l}_{d}', f_gates, (B, nt),
                    [tok(r['u']), vec(W['lru_w_a'][j, d]), vec(W['lru_b_a'][j, d][None]), vec(W['lru_w_i'][j, d]),
                     vec(W['lru_b_i'][j, d][None]), vec(W['lru_lam'][j, d][None])],
                    [[tok(ga4)], [tok(g4)]],
                    [(0,) + tok_out(D) + (None,), (1,) + vec_acc((8, 128, 128)), (2,) + vec_acc((1, D)), (3,) + vec_acc((8, 128, 128)),
                     (4,) + vec_acc((1, D)), (5,) + vec_acc((1, D))])
                d_u_parts.append(du_g)
                add_grad('lru_w_a', (j, d), dwa)
                add_grad('lru_b_a', (j, d), dba[0])
                add_grad('lru_w_i', (j, d), dwi)
                add_grad('lru_b_i', (j, d), dbi[0])
                add_grad('lru_lam', (j, d), dlam[0])
            add_grad('ssd_dt_bias', j, (dbias_t[0] + dbias_t[1])[0, :32].reshape(2, SSD_HEADS))
            add_grad('ssd_a_log', j, (dalog_t[0] + dalog_t[1])[0, :32].reshape(2, SSD_HEADS))
            scw, scb = W['ssd_conv_w'][j], W['ssd_conv_b'][j][None]
            lcw, lcb = W['lru_conv_w'][j], W['lru_conv_b'][j][None]

            def conv_bwd(name, colblk0, w, b, wblk0, ncols, act, parts):
                return tile_bwd(
                    name, functools.partial(f_conv1d, lc=Lc, act=act), (ncols // 256, B),
                    [(proj, (None, Tt, 256), lambda cb, bi: (bi, 0, colblk0 + cb)),
                     (w, (4, 256), lambda cb, bi: (0, wblk0 + cb)), (b, (1, 256), lambda cb, bi: (0, wblk0 + cb))],
                    [[(p, (None, Tt, 256), lambda cb, bi: (bi, 0, cb)) for p in parts]],
                    [(0, (B, Tt, ncols), F32, (None, Tt, 256), lambda cb, bi: (bi, 0, cb), None),
                     (1, (4, ncols), F32, (4, 256), lambda cb, bi: (0, cb), lambda cb, bi: bi == 0),
                     (2, (1, ncols), F32, (1, 256), lambda cb, bi: (0, cb), lambda cb, bi: bi == 0)])
            d_xs_raw, dw_xs, db_xs = conv_bwd(f'conv_xs_bwd{l}', 12, scw, scb, 0, 1024, True, d_xs_parts)
            d_bc_raw, dw_bc, db_bc = conv_bwd(f'conv_bc_bwd{l}', 16, scw, scb, 4, 512, True, d_bc_parts)
            d_u_raw, dw_u, db_u = conv_bwd(f'conv_u_bwd{l}', 8, lcw, lcb, 0, 1024, False, d_u_parts)
            add_grad('ssd_conv_w', j, jnp.concatenate([dw_xs, dw_bc], axis=1))
            add_grad('ssd_conv_b', j, jnp.concatenate([db_xs, db_bc], axis=1)[0])
            add_grad('lru_conv_w', j, dw_u)
            add_grad('lru_conv_b', j, db_u[0])
            def f_ev_dproj(z_, gy_, u_, xs_, bc_, t0, t1):
                pad = jnp.zeros((z_.shape[0], EV_PAD - 4736), F32)
                return jnp.concatenate([z_, gy_, u_, xs_, bc_, t0 + t1, pad], axis=1)
            dproj = tile_fwd(f'ev_dproj{l}', f_ev_dproj, (B, nt),
                             [tok(d_z), tok(d_gy), tok(d_u_raw), tok(d_xs_raw), tok(d_bc_raw), tok(d_dt_parts[0]), tok(d_dt_parts[1])],
                             [tok_out(EV_PAD, BF16)])[0].reshape(M, EV_PAD)
            d_h1 = mm(f'ev_proj_dx{l}', [(dproj, win, 0)], tb=True).reshape(B, Tt, D)
            dwp = mm(f'ev_proj_dw{l}', [(h1m, dproj, 0)], ta=True)
            add_grad('ev_w_in', j, jnp.concatenate([dwp[:, 0:1024], dwp[:, 3072:4640], dwp[:, 1024:3072]], axis=1))
        else:
            wout, win = W['od_w_out'][j], W['od_w_in'][j]
            d_mix = mm(f'mix_out_dx{l}', [(d_o1m, wout, 0)], tb=True).reshape(B, Tt, D)
            add_grad('od_w_out', j, mm(f'mix_out_dw{l}', [(r['mix'].reshape(M, D), d_o1m, 0)], ta=True))
            hn, sd, gw, gb = r['fin_par']
            d_o, d_g, d_yv, d_u_fin, dhn, dsd, dgw, dgb = tile_bwd(
                f'oddfin_bwd{l}', f_oddfin, (B, nt),
                [tok(r['o'][0]), tok(r['o'][1]), tok(proj, HG_W, 4), tok(r['yd'][0]), tok(r['yd'][1]), tok(proj, S5_W, 15),
                 vec(hn), vec(sd), vec(gw), vec(gb)],
                [[tok(d_mix)]],
                [(0,) + tok_out(HG_W) + (None,), (2,) + tok_out(HG_W) + (None,), (3,) + tok_out(S5_W) + (None,),
                 (5,) + tok_out(S5_W) + (None,), (6,) + vec_acc((8, 128)), (7,) + vec_acc((1, S5_W)), (8,) + vec_acc((S5_W, S5_W)),
                 (9,) + vec_acc((1, S5_W))])
            add_grad('hg_norm_g', j, dhn[:HG_HEADS])
            add_grad('s5_d', j, dsd[0])
            add_grad('s5_glu_w', j, dgw)
            add_grad('s5_glu_b', j, dgb[0])
            lbv = lb1 if l == 1 else lb3
            ns = nc
            dq, df, dv, du_s5 = [], [], [], []
            d_ym = d_yv.reshape(M, S5_W)
            dbt_re, dbt_im = [], []
            for d in range(2):
                ph0 = phys_chunk(ns, ncc, d == 1)

                def ph(s, ph0=ph0):
                    return ph0(ns - 1 - s)
                dq_d, df_d, dv_d, dlb_d = scan_bwd(
                    f'hgrn_bwd{l}_{d}', functools.partial(f_hgrn_group, reverse=(d == 1), chunk=HG_BWD[0]), (bgr, B, ns),
                    [(proj, (None, 128, bgw), lambda h, b, s, ph=ph: (b, ph(s), h)),
                     (proj, (None, 128, bgw), lambda h, b, s, ph=ph, d=d: (b, ph(s), (1 + d) * bgr + h)),
                     (proj, (None, 128, bgw), lambda h, b, s, ph=ph: (b, ph(s), 3 * bgr + h)),
                     (lbv, (1, bgw), lambda h, b, s: (0, h))],
                    (r['zst'][d].reshape(B, ns, bgr, bgw, 128), (None, None, None, bgw, 128), lambda h, b, s: (b, ns - 1 - s, h, 0, 0)),
                    (d_o, (None, 128, bgw), lambda h, b, s, ph=ph: (b, ph(s), h)),
                    [(0, (B, Tt, HG_W), F32, (None, 128, bgw), lambda h, b, s, ph=ph: (b, ph(s), h), None),
                     (1, (B, Tt, HG_W), F32, (None, 128, bgw), lambda h, b, s, ph=ph: (b, ph(s), h), None),
                     (2, (B, Tt, HG_W), F32, (None, 128, bgw), lambda h, b, s, ph=ph: (b, ph(s), h), None),
                     (3, (1, HG_W), F32, (1, bgw), lambda h, b, s: (0, h), lambda h, b, s: (b == 0) & (s == 0))],
                    (bgw, 128), lambda h, b, s: s == 0)
                dq.append(dq_d)
                df.append(df_d)
                dv.append(dv_d)
                dlb[l].append(dlb_d)
                d_hcat = mm(f's5_out_dx{l}_{d}', [(d_ym, r['ccat'][d], 0)], tb=True)
                dccat = mm(f's5_out_dw{l}_{d}', [(r['hcat'][d], d_ym, 0)], ta=True)
                add_grad('s5_c_re', (j, d), _unblockdiag_t(dccat[:D]))
                add_grad('s5_c_im', (j, d), -_unblockdiag_t(dccat[D:]))
                g5, dcoef8 = clinrec(f's5_bwd{l}_{d}', d_hcat.reshape(B, Tt, 2 * D), r['coef'][d], 'Fb' if d == 0 else 'Rb',
                                     Lc, conj=True, hprev=r['hp5'][d])
                dcoef = jnp.sum(dcoef8, axis=2)
                gcat = g5.reshape(M, 2 * D)
                dbcat = mm(f's5_in_dw{l}_{d}', [(r['u2'], gcat, 0)], ta=True)
                du_s5.append(mm(f's5_in_dx{l}_{d}', [(gcat, r['bcat'][d], 0)], tb=True))
                cts5 = [dcoef[0].reshape(16, 64), dcoef[1].reshape(16, 64), _unblockdiag(dbcat[:, :D]), _unblockdiag(dbcat[:, D:])]
                dlre, dlim, dlst, dbtr, dbti = tile_bwd(
                    f's5p_bwd{l}_{d}', f_s5p, (1,), [vec(t) for t in r['s5in'][d]], [[vec(t)] for t in cts5],
                    [(i, t.shape, F32, t.shape, (lambda *ids, n=t.ndim: (0,) * n), None) for i, t in enumerate(r['s5in'][d])])
                add_grad('s5_lam_re', (j, d), dlre)
                add_grad('s5_lam_im', (j, d), dlim)
                add_grad('s5_log_step', (j, d), dlst[:, 0])
                dbt_re.append(dbtr)
                dbt_im.append(dbti)
            add_grad('s5_b_re', j, jnp.swapaxes(dbt_re[0] + dbt_re[1], 1, 2))
            add_grad('s5_b_im', j, jnp.swapaxes(dbt_im[0] + dbt_im[1], 1, 2))
            def f_od_dproj(q0, q1, f0, f1, v0, v1, g_, u0, u1, u2):
                return jnp.concatenate([q0 + q1, f0, f1, v0 + v1, g_, u0 + u1 + u2], axis=1)
            parts = [dq[0], dq[1], df[0], df[1], dv[0], dv[1], d_g, d_u_fin, du_s5[0].reshape(B, Tt, S5_W),
                     du_s5[1].reshape(B, Tt, S5_W)]
            dproj = tile_fwd(f'od_dproj{l}', f_od_dproj, (B, nt), [tok(t) for t in parts],
                             [tok_out(4096, BF16)])[0].reshape(M, 4096)
            d_h1 = mm(f'od_proj_dx{l}', [(dproj, win, 0)], tb=True).reshape(B, Tt, D)
            add_grad('od_w_in', j, mm(f'od_proj_dw{l}', [(h1m, dproj, 0)], ta=True))
        ng = W['norm_mix_g'][l][None]
        if l == 0:
            d_x0, dng, dsh, dsc = tile_bwd(
                'nm0_bwd', lambda xv, g, sh, sc: (xv, f_nm0(xv, g, sh, sc)), (B, nt),
                [tok(r['xin']), vec(ng), modv(l, 0), modv(l, 1)], [[tok(d_xa)], [tok(d_h1)]],
                [(0,) + tok_out(D) + (None,), (1,) + vec_acc((1, D)), (2,) + dmod_spec, (3,) + dmod_spec])
        else:
            d_xp, d_o2, dgate, dng, dsh, dsc = tile_bwd(
                f'nm_mix_bwd{l}', f_nm, (B, nt),
                [tok(r['xin']), tok(r['oin']), modv(l - 1, 5), vec(ng), modv(l, 0), modv(l, 1)],
                [[tok(d_xa)], [tok(d_h1)]],
                [(0,) + tok_out(D) + (None,), (1,) + tok_out(D, BF16) + (None,), (2,) + dmod_spec, (3,) + vec_acc((1, D)),
                 (4,) + dmod_spec, (5,) + dmod_spec])
            dmods[(l - 1, 5)] = dgate
        add_grad('norm_mix_g', l, dng[0])
        dmods[(l, 0)], dmods[(l, 1)] = dsh, dsc

    grad_x = d_x0[:, Lc:, :]

    (dlogits,) = tile_bwd('lb_bwd', f_lb, (1,), [vec(W['hg_lb_logits'])],
                          [[vec(t) for t in dlb[1]], [vec(t) for t in dlb[3]]],
                          [(0, (DEPTH, HG_W), F32, (DEPTH, HG_W), lambda i: (0, 0), None)])
    add_grad('hg_lb_logits', None, dlogits)

    dm = jnp.stack([jnp.stack([dmods[(l, w)] for w in range(N_MOD)]) for l in range(DEPTH)])
    dlat = jnp.transpose(dm[:, :, :, 1, 0, :], (0, 2, 1, 3)).reshape(DEPTH, B, N_MOD * D)
    dctx = jnp.transpose(dm[:, :, :, 0, 0, :], (0, 2, 1, 3)).reshape(DEPTH, B, N_MOD * D)
    dlat = jnp.zeros((DEPTH, 8, N_MOD * D), F32).at[:, :B].set(dlat)
    dctx = jnp.zeros((DEPTH, 8, N_MOD * D), F32).at[:, :B].set(dctx)

    def mod_bwd_body(cc_ref, w_ref, dl_ref, dc_ref, dall_ref, db_ref, dcc_ref):
        row = lax.broadcasted_iota(jnp.int32, (8, 1), 0)
        dall = dl_ref[...] + jnp.where(row == 4, jnp.sum(dc_ref[...], axis=0, keepdims=True), 0.0)
        s, vjp = jax.vjp(jax.nn.silu, cc_ref[...])
        db16 = dall.astype(BF16)
        dall_ref[...] = dall
        db_ref[...] = jnp.sum(dall, axis=0, keepdims=True)
        ds = lax.dot_general(db16, w_ref[...], (((1,), (1,)), ((), ())), preferred_element_type=F32)
        (dcc,) = vjp(ds)
        first = (pl.program_id(0) == 0) & (pl.program_id(1) == 0)

        @pl.when(first)
        def _():
            dcc_ref[...] = dcc

        @pl.when(jnp.logical_not(first))
        def _():
            dcc_ref[...] += dcc

    dmod, dbmod, dcc = pl.pallas_call(
        mod_bwd_body, grid=(DEPTH, nmc),
        in_specs=[pl.BlockSpec((8, D), lambda l, n: (0, 0)), pl.BlockSpec((None, D, 1536), lambda l, n: (l, 0, n)),
                  pl.BlockSpec((None, 8, 1536), lambda l, n: (l, 0, n)), pl.BlockSpec((None, 8, 1536), lambda l, n: (l, 0, n))],
        out_specs=[pl.BlockSpec((None, 8, 1536), lambda l, n: (l, 0, n)), pl.BlockSpec((None, 1, 1536), lambda l, n: (l, 0, n)),
                   pl.BlockSpec((8, D), lambda l, n: (0, 0))],
        out_shape=[jax.ShapeDtypeStruct((DEPTH, 8, N_MOD * D), F32), jax.ShapeDtypeStruct((DEPTH, 1, N_MOD * D), F32),
                   jax.ShapeDtypeStruct((8, D), F32)],
        name='mod_bwd', compiler_params=_cparams(("arbitrary", "arbitrary")),
    )(cc, W['w_mod'], dlat, dctx)
    add_grad('b_mod', None, dbmod[:, 0])
    add_grad('c_ctx', None, dcc[4])
    return loss, grad_x, G, (cc, dmod)


def mod_weight_grad(name, cc_rows, dmod_rows):
    nl, nr, cols = dmod_rows.shape

    def f(ccv, dv):
        return lax.dot_general(jax.nn.silu(ccv).astype(BF16), dv.astype(BF16), (((0,), (0,)), ((), ())),
                               preferred_element_type=F32)

    return tile_fwd(name, f, (nl,), [(cc_rows, (nr, D), lambda l: (0, 0)), (dmod_rows, (None, nr, cols), lambda l: (l, 0, 0))],
                    [((nl, D, cols), F32, (None, D, cols), lambda l: (l, 0, 0))])[0]


def assemble_grads(G, like):
    out = {}
    for name, parts in G.items():
        shape = like[name].shape
        if None in parts:
            g = parts[None]
        elif isinstance(next(iter(parts)), tuple):
            g = jnp.stack([jnp.stack([parts[(j, d)] for d in range(2)]) for j in range(shape[0])])
        else:
            g = jnp.stack([parts[i] for i in range(shape[0])])
        out[name] = g.reshape(shape)
    return out


XY_RELS = ((1, 0, 0), (0, 1, 0), (1, 1, 0))
ALL_RELS = tuple((dx, dy, dc) for dx in (0, 1) for dy in (0, 1) for dc in (0, 1))[1:]


def exchange(name, src, out_shape, sends):
    return exchange_many(name, [(src, out_shape, sends)])[0]


def exchange_many(name, items):
    na = len(items)
    n = sum(len(sends) for _, _, sends in items)

    def body(*refs):
        src_refs, out_refs, send_sems, recv_sems = refs[:na], refs[na:2 * na], refs[2 * na], refs[2 * na + 1]
        me = (lax.axis_index("x"), lax.axis_index("y"), lax.axis_index("c"))
        copies, k = [], 0
        for (_, _, sends), src_ref, out_ref in zip(items, src_refs, out_refs):
            for rel, ssel, dsel in sends:
                tgt = tuple(1 - m if f else m for m, f in zip(me, rel))
                cp = pltpu.make_async_remote_copy(
                    src_ref=src_ref if ssel is None else src_ref.at[ssel(me, tgt)],
                    dst_ref=out_ref if dsel is None else out_ref.at[dsel(me, tgt)],
                    send_sem=send_sems.at[k], recv_sem=recv_sems.at[k], device_id=tgt, device_id_type=MESH)
                cp.start()
                copies.append(cp)
                k += 1
        for cp in copies:
            cp.wait()

    return pl.pallas_call(
        body, out_shape=[jax.ShapeDtypeStruct(shape, src.dtype) for src, shape, _ in items],
        in_specs=[pl.BlockSpec(memory_space=pl.ANY)] * na, out_specs=[pl.BlockSpec(memory_space=pl.ANY)] * na,
        scratch_shapes=[pltpu.SemaphoreType.DMA((n,)), pltpu.SemaphoreType.DMA((n,))],
        name=name,
    )(*[src for src, _, _ in items])


def _xy_index(dev):
    return 2 * dev[0] + dev[1]


def _my_xy():
    return 2 * lax.axis_index("x") + lax.axis_index("y")


def all_gather_xy(name, shard):
    got = exchange(name, shard, (4,) + shard.shape, [(rel, None, lambda me, tgt: _xy_index(me)) for rel in XY_RELS])
    return lax.dynamic_update_index_in_dim(got, shard, _my_xy(), 0)


def reduce_scatter_xy(name, g4):
    got = exchange(name, g4, (3,) + g4.shape[1:],
                   [(rel, (lambda me, tgt: _xy_index(tgt)), (lambda me, tgt, k=k: k)) for k, rel in enumerate(XY_RELS)])
    return got, lax.dynamic_index_in_dim(g4, _my_xy(), 0, keepdims=False)


def sibling_swap(name, v):
    return exchange(name, v, v.shape, [((0, 0, 1), None, None)])


def all_gather_all(name, v):
    got = exchange(name, v, (8,) + v.shape, [(rel, None, lambda me, tgt: 4 * me[0] + 2 * me[1] + me[2]) for rel in ALL_RELS])
    return lax.dynamic_update_index_in_dim(got, v, 2 * _my_xy() + lax.axis_index("c"), 0)


def all_gather_xy_halves(name, shards):
    na = len(shards)

    def body(*refs):
        src_refs, out_refs, send_sems, recv_sems = refs[:na], refs[na:2 * na], refs[2 * na], refs[2 * na + 1]
        x, y, c = lax.axis_index("x"), lax.axis_index("y"), lax.axis_index("c")
        peers = [(1 - x, y), (x, 1 - y), (1 - x, 1 - y)]

        def copy(k, src, dst, to):
            return pltpu.make_async_remote_copy(src_ref=src, dst_ref=dst, send_sem=send_sems.at[k], recv_sem=recv_sems.at[k],
                                                device_id=to, device_id_type=MESH)

        halves = [pl.ds(c * (s.shape[0] // 2), s.shape[0] // 2) for s in shards]
        first = [[copy(6 * i + k, src_refs[i].at[halves[i]], out_refs[i].at[2 * x + y, halves[i]], (px, py, c))
                  for k, (px, py) in enumerate(peers)] for i in range(na)]
        for row in first:
            for cp in row:
                cp.start()
        passed = []
        for i in range(na):
            for k, (px, py) in enumerate(peers):
                first[i][k].wait_recv()
                landed = out_refs[i].at[2 * px + py, halves[i]]
                fw = copy(6 * i + 3 + k, landed, landed, (x, y, 1 - c))
                fw.start()
                passed.append(fw)
        for fw in passed:
            fw.wait_recv()
        for cp in [cp for row in first for cp in row] + passed:
            cp.wait_send()

    got = pl.pallas_call(
        body, out_shape=[jax.ShapeDtypeStruct((4,) + s.shape, s.dtype) for s in shards],
        in_specs=[pl.BlockSpec(memory_space=pl.ANY)] * na, out_specs=[pl.BlockSpec(memory_space=pl.ANY)] * na,
        scratch_shapes=[pltpu.SemaphoreType.DMA((6 * na,)), pltpu.SemaphoreType.DMA((6 * na,))],
        name=name,
    )(*shards)
    return [lax.dynamic_update_index_in_dim(g, s, _my_xy(), 0) for g, s in zip(got, shards)]


def reduce_scatter_xy_many(name, g4s):
    got = exchange_many(name, [(g4, (3,) + g4.shape[1:],
                                [(rel, (lambda me, tgt: _xy_index(tgt)), (lambda me, tgt, k=k: k)) for k, rel in enumerate(XY_RELS)])
                               for g4 in g4s])
    return [(g, lax.dynamic_index_in_dim(g4, _my_xy(), 0, keepdims=False)) for g, g4 in zip(got, g4s)]


def sibling_split(name, g4s):
    halves = [g4.shape[1] // 2 for g4 in g4s]
    got = exchange_many(name, [(g4, (4, h) + g4.shape[2:], [((0, 0, 1), (lambda me, tgt, h=h: (slice(None), pl.ds(tgt[2] * h, h))), None)])
                               for g4, h in zip(g4s, halves)])
    return [(g, lax.dynamic_slice_in_dim(g4, lax.axis_index("c") * h, h, axis=1)) for g, g4, h in zip(got, g4s, halves)]


def sibling_join(name, qs):
    got = exchange_many(name, [(q, (2 * q.shape[0],) + q.shape[1:], [((0, 0, 1), None, lambda me, tgt, h=q.shape[0]: pl.ds(me[2] * h, h))])
                               for q in qs])
    return [lax.dynamic_update_slice_in_dim(g, q, lax.axis_index("c") * q.shape[0], axis=0) for g, q in zip(got, qs)]


def _rows_view(shape):
    cols = shape[-1] if len(shape) else 1
    rows = 1
    for s in shape[:-1]:
        rows *= s
    return rows, cols


def _row_block(rows, cols, n_arrays):
    budget = (24 * 1024 * 1024) // (8 * n_arrays * cols)
    if rows <= max(budget, 16):
        return rows
    br = (min(budget, rows) // 16) * 16
    while br > 16 and rows % br:
        br -= 16
    return br if rows % br == 0 else rows


def sum_slots(name, stacked, extra=(), out_dtype=F32):
    k = stacked.shape[0]
    rows, cols = _rows_view(stacked.shape[1:])
    br = _row_block(rows, cols, k + len(extra) + 1)

    def f(s, *more):
        parts = [s[i].astype(F32) for i in range(k)] + [m.astype(F32) for m in more]
        while len(parts) > 1:
            parts = [parts[i] + parts[i + 1] for i in range(0, len(parts) - 1, 2)] + ([parts[-1]] if len(parts) % 2 else [])
        return parts[0]

    out = tile_fwd(name, f, (rows // br,),
                   [(stacked.reshape(k, rows, cols), (k, br, cols), lambda i: (0, i, 0))]
                   + [(e.reshape(rows, cols), (br, cols), lambda i: (i, 0)) for e in extra],
                   [((rows, cols), out_dtype, (br, cols), lambda i: (i, 0))])[0]
    return out.reshape(stacked.shape[1:])


def adamw(name, w, m, v, gs):
    rows, cols = _rows_view(w.shape)
    br = _row_block(rows, cols, 7 + len(gs))
    spec = lambda a: (a.reshape(rows, cols), (br, cols), lambda i: (i, 0))
    outs = tile_fwd(name, f_adamw, (rows // br,), [spec(t) for t in (w, m, v) + tuple(gs)],
                    [((rows, cols), F32, (br, cols), lambda i: (i, 0))] * 4)
    return [o.reshape(w.shape) for o in outs]


IN_NAMES = ['x', 'c', 'ctx'] + W_NAMES + ['loss_target'] + ['m_' + n for n in W_NAMES] + ['v_' + n for n in W_NAMES]
SMALL_PAD = 128 * 1024


def kernel(x, c, ctx, c_ctx, w_mod, b_mod, norm_mix_g, norm_ffn_g, final_norm_g, ev_w_in, ev_w_out, ssd_conv_w, ssd_conv_b, ssd_dt_bias, ssd_a_log, ssd_d, ssd_norm_g, lru_conv_w, lru_conv_b, lru_w_a, lru_b_a, lru_w_i, lru_b_i, lru_lam, od_w_in, od_w_out, hg_lb_logits, hg_norm_g, s5_lam_re, s5_lam_im, s5_log_step, s5_b_re, s5_b_im, s5_c_re, s5_c_im, s5_d, s5_glu_w, s5_glu_b, ffn_w_gate, ffn_w_up, ffn_conv_w, ffn_conv_b, ffn_w_down, loss_target, m_c_ctx, m_w_mod, m_b_mod, m_norm_mix_g, m_norm_ffn_g, m_final_norm_g, m_ev_w_in, m_ev_w_out, m_ssd_conv_w, m_ssd_conv_b, m_ssd_dt_bias, m_ssd_a_log, m_ssd_d, m_ssd_norm_g, m_lru_conv_w, m_lru_conv_b, m_lru_w_a, m_lru_b_a, m_lru_w_i, m_lru_b_i, m_lru_lam, m_od_w_in, m_od_w_out, m_hg_lb_logits, m_hg_norm_g, m_s5_lam_re, m_s5_lam_im, m_s5_log_step, m_s5_b_re, m_s5_b_im, m_s5_c_re, m_s5_c_im, m_s5_d, m_s5_glu_w, m_s5_glu_b, m_ffn_w_gate, m_ffn_w_up, m_ffn_conv_w, m_ffn_conv_b, m_ffn_w_down, v_c_ctx, v_w_mod, v_b_mod, v_norm_mix_g, v_norm_ffn_g, v_final_norm_g, v_ev_w_in, v_ev_w_out, v_ssd_conv_w, v_ssd_conv_b, v_ssd_dt_bias, v_ssd_a_log, v_ssd_d, v_ssd_norm_g, v_lru_conv_w, v_lru_conv_b, v_lru_w_a, v_lru_b_a, v_lru_w_i, v_lru_b_i, v_lru_lam, v_od_w_in, v_od_w_out, v_hg_lb_logits, v_hg_norm_g, v_s5_lam_re, v_s5_lam_im, v_s5_log_step, v_s5_b_re, v_s5_b_im, v_s5_c_re, v_s5_c_im, v_s5_d, v_s5_glu_w, v_s5_glu_b, v_ffn_w_gate, v_ffn_w_up, v_ffn_conv_w, v_ffn_conv_b, v_ffn_w_down):
    a = dict(locals())
    big = [n for n in W_NAMES if n in MATMUL_WEIGHTS]
    minor = [n for n in W_NAMES if n in SHARD_AXIS and n not in MATMUL_WEIGHTS]

    def pack(arrays, lead=()):
        flat = jnp.concatenate([t.reshape(lead + (-1,)) for t in arrays], axis=len(lead))
        pad = -flat.shape[-1] % 1024
        flat = jnp.concatenate([flat, jnp.zeros(lead + (pad,), flat.dtype)], axis=len(lead))
        return flat.reshape(lead + (-1, 128))

    def unpack(packed, names, lead=()):
        flat, out, off = packed.reshape(lead + (-1,)), {}, 0
        for n in names:
            size = math.prod(a[n].shape)
            out[n] = flat[..., off:off + size].reshape(lead + a[n].shape)
            off += size
        return out

    gathered = dict(zip(big, all_gather_xy_halves('ag_big', [a[n].astype(BF16) for n in big])))
    gathered.update(unpack(all_gather_xy('ag_minor', pack([a[n] for n in minor])), minor, (4,)))
    W = {}
    for n in W_NAMES:
        w = a[n]
        if n in SHARD_AXIS:
            ax = SHARD_AXIS[n]
            shape = list(w.shape)
            shape[ax] *= 4
            W[n] = jnp.moveaxis(gathered[n], 0, ax).reshape(shape)
        else:
            W[n] = w
    e = W['ev_w_in']
    W['ev_w_in_p'] = jnp.concatenate(
        [e[:, :, 0:1024], e[:, :, 2592:3616], e[:, :, 3616:4640], e[:, :, 1024:2560], e[:, :, 2560:2592],
         jnp.zeros((e.shape[0], D, EV_PAD - 4640), e.dtype)], axis=2)

    loss_local, grad_x, G, (cc, dmod) = local_step(a['x'], a['c'], a['ctx'], W['c_ctx'], a['loss_target'], W)
    grads = assemble_grads(G, W)
    loss = lax.psum(loss_local, ("x", "y", "c"))

    res = {}
    cc_all = all_gather_all('ag_c', cc).reshape(8 * 8, D)
    dm_all = jnp.moveaxis(all_gather_all('ag_dmod', dmod), 0, 1).reshape(DEPTH, 8 * 8, N_MOD * D)
    mcols = a['w_mod'].shape[2]
    g_wmod = mod_weight_grad('mod_dw', cc_all, lax.dynamic_slice_in_dim(dm_all, _my_xy() * mcols, mcols, axis=2))
    res['w_mod'] = adamw('adamw_w_mod', a['w_mod'], a['m_w_mod'], a['v_w_mod'], (g_wmod,))
    big = [n for n in big if n != 'w_mod']
    g4 = {}
    for n in big + minor:
        ax = SHARD_AXIS[n]
        gf = grads[n]
        g4[n] = jnp.moveaxis(gf.reshape(gf.shape[:ax] + (4, a[n].shape[ax]) + gf.shape[ax + 1:]), ax, 0)
    parts = [sum_slots('csum_' + n, theirs[None], (ours,), BF16)
             for n, (theirs, ours) in zip(big, sibling_split('rsc_big', [g4[n] for n in big]))]
    halves = [sum_slots('gsum_' + n, got, (own,)) for n, (got, own) in zip(big, reduce_scatter_xy_many('rs_big', parts))]
    for n, g in zip(big, sibling_join('agc_big', halves)):
        res[n] = adamw('adamw_' + n, a[n], a['m_' + n], a['v_' + n], (g,))
    got, own = reduce_scatter_xy('rs_minor', pack([g4[n] for n in minor], (4,)))
    mine = sum_slots('gsum_minor', got, (own,))
    mine_n, other_n = unpack(mine, minor), unpack(sibling_swap('sw_minor', mine), minor)
    for n in minor:
        res[n] = adamw('adamw_' + n, a[n], a['m_' + n], a['v_' + n], (mine_n[n], other_n[n]))
    small = [n for n in W_NAMES if n not in SHARD_AXIS]
    flat = jnp.concatenate([grads[n].reshape(-1) for n in small])
    total = flat.shape[0]
    padded = -(-total // SMALL_PAD) * SMALL_PAD
    flat = jnp.concatenate([flat, jnp.zeros((padded - total,), F32)]).reshape(padded // 128, 128)
    pair = sum_slots('csum_small', flat[None], (sibling_swap('sw_small', flat),))
    summed = sum_slots('gsum_small', all_gather_xy('ag_small', pair)).reshape(-1)
    off = 0
    for n in small:
        size = math.prod(a[n].shape)
        g = summed[off:off + size].reshape(a[n].shape)
        off += size
        res[n] = adamw('adamw_' + n, a[n], a['m_' + n], a['v_' + n], (g,))
    outs = [loss, grad_x]
    for k in range(4):
        outs += [res[n][k] for n in W_NAMES]
    return tuple(outs)
```

```python
import functools
import math

import jax
import jax.numpy as jnp
from jax import lax
from jax.experimental import pallas as pl
from jax.experimental.pallas import tpu as pltpu

F32 = jnp.float32
BF16 = jnp.bfloat16
HI = lax.Precision.HIGHEST
MESH = pl.DeviceIdType.MESH

D = 1024
DEPTH = 4
N_MOD = 6
RMS_EPS = 1e-6
GRID_W = 64
SSD_HEADS = 16
SSD_CHUNK = 128
HG_W = 768
HG_HEADS = 6
HG_FWD = (32, 6)
HG_BWD = (16, 3)
S5_W = 256
D_FF = 2816
EV_PAD = 5120
LRU_C = 8.0
V7X_VMEM_LIMIT = 56 * 1024 * 1024
MM_VMEM_BUDGET = 36 * 1024 * 1024

ADAM_LR, ADAM_B1, ADAM_B2, ADAM_EPS, ADAM_WD, ADAM_STEP = 0.001, 0.9, 0.999, 1e-08, 0.01, 10

W_NAMES = ['c_ctx', 'w_mod', 'b_mod', 'norm_mix_g', 'norm_ffn_g', 'final_norm_g', 'ev_w_in', 'ev_w_out', 'ssd_conv_w',
           'ssd_conv_b', 'ssd_dt_bias', 'ssd_a_log', 'ssd_d', 'ssd_norm_g', 'lru_conv_w', 'lru_conv_b', 'lru_w_a', 'lru_b_a',
           'lru_w_i', 'lru_b_i', 'lru_lam', 'od_w_in', 'od_w_out', 'hg_lb_logits', 'hg_norm_g', 's5_lam_re', 's5_lam_im',
           's5_log_step', 's5_b_re', 's5_b_im', 's5_c_re', 's5_c_im', 's5_d', 's5_glu_w', 's5_glu_b', 'ffn_w_gate', 'ffn_w_up',
           'ffn_conv_w', 'ffn_conv_b', 'ffn_w_down']
SHARD_AXIS = {'w_mod': 2, 'ev_w_in': 2, 'ev_w_out': 1, 'ssd_conv_w': 2, 'lru_conv_w': 2, 'lru_b_a': 2, 'lru_b_i': 2,
              'lru_lam': 2, 'od_w_in': 2, 'od_w_out': 1, 's5_d': 1, 's5_glu_w': 1, 's5_glu_b': 1, 'ffn_w_gate': 2,
              'ffn_w_up': 2, 'ffn_conv_w': 3, 'ffn_w_down': 1}
MATMUL_WEIGHTS = ('w_mod', 'ev_w_in', 'ev_w_out', 'od_w_in', 'od_w_out', 'ffn_w_gate', 'ffn_w_up', 'ffn_w_down')


def _cparams(sem=None):
    return pltpu.CompilerParams(vmem_limit_bytes=V7X_VMEM_LIMIT, dimension_semantics=sem)


def _pick(n, cands):
    for c in cands:
        if n % c == 0:
            return c
    return n


def tile_fwd(name, f, grid, ins, outs):
    n_in = len(ins)

    def body(*refs):
        res = f(*[r[...] for r in refs[:n_in]])
        if not isinstance(res, (tuple, list)):
            res = (res,)
        for r, o in zip(res, refs[n_in:]):
            o[...] = r.astype(o.dtype)

    res = pl.pallas_call(
        body, grid=grid,
        in_specs=[pl.BlockSpec(b, m) for _, b, m in ins],
        out_specs=[pl.BlockSpec(b, m) for _, _, b, m in outs],
        out_shape=[jax.ShapeDtypeStruct(s, d) for s, d, _, _ in outs],
        name=name, compiler_params=_cparams(("arbitrary",) * len(grid)),
    )(*[a for a, _, _ in ins])
    return res


def tile_bwd(name, f, grid, ins, cts, grads, prims=()):
    n_in = len(ins)
    ct_flat = [p for c in cts for p in c]
    n_ct = len(ct_flat)
    didx = [g[0] for g in grads]

    def body(*refs):
        in_refs, ct_refs = refs[:n_in], refs[n_in:n_in + n_ct]
        g_refs = refs[n_in + n_ct:n_in + n_ct + len(grads)]
        p_refs = refs[n_in + n_ct + len(grads):]
        vals = [r[...] for r in in_refs]

        def fd(*dv):
            full = list(vals)
            for i, v in zip(didx, dv):
                full[i] = v
            res = f(*full)
            return tuple(res) if isinstance(res, (tuple, list)) else (res,)

        out, vjp = jax.vjp(fd, *[vals[i] for i in didx])
        ctv, k = [], 0
        for o, c in zip(out, cts):
            acc = None
            for _ in c:
                piece = ct_refs[k][...].astype(o.dtype)
                acc = piece if acc is None else acc + piece
                k += 1
            ctv.append(jnp.zeros_like(o) if acc is None else acc.reshape(o.shape))
        gs = vjp(tuple(ctv))
        ids = [pl.program_id(a) for a in range(len(grid))]

        def emit(ref, val, first):
            if first is None:
                ref[...] = val.astype(ref.dtype)
            else:
                is_first = first(*ids)

                @pl.when(is_first)
                def _():
                    ref[...] = val.astype(ref.dtype)

                @pl.when(jnp.logical_not(is_first))
                def _():
                    ref[...] += val.astype(ref.dtype)

        for g, spec, ref in zip(gs, grads, g_refs):
            emit(ref, g, spec[5])
        for spec, ref in zip(prims, p_refs):
            emit(ref, out[spec[0]], spec[5])

    specs = list(grads) + list(prims)
    res = pl.pallas_call(
        body, grid=grid,
        in_specs=[pl.BlockSpec(b, m) for _, b, m in list(ins) + ct_flat],
        out_specs=[pl.BlockSpec(s[3], s[4]) for s in specs],
        out_shape=[jax.ShapeDtypeStruct(s[1], s[2]) for s in specs],
        name=name, compiler_params=_cparams(("arbitrary",) * len(grid)),
    )(*[a for a, _, _ in list(ins) + ct_flat])
    return res


def mm(name, pairs, ta=False, tb=False, out_dtype=F32):
    a0, b0, _ = pairs[0]
    m = a0.shape[1] if ta else a0.shape[0]
    n = b0.shape[0] if tb else b0.shape[1]
    cands = (1024, 1408, 768, 512, 256, 128)
    tks, nks = [], []
    for a, b, _ in pairs:
        k = a.shape[0] if ta else a.shape[1]
        tk = _pick(k, cands)
        tks.append(tk)
        nks.append(k // tk)

    def vmem_bytes(tm, tn):
        tiles = sum(2 * tk * (tm * a.dtype.itemsize + tn * b.dtype.itemsize) for (a, b, _), tk in zip(pairs, tks))
        return tiles + tm * tn * (4 + 2 * jnp.dtype(out_dtype).itemsize)

    tm_c = [c_ for c_ in cands if m % c_ == 0] or [m]
    tn_c = [c_ for c_ in cands if n % c_ == 0] or [n]
    tm, tn = tm_c[0], tn_c[0]
    while vmem_bytes(tm, tn) > MM_VMEM_BUDGET and (len(tm_c) > 1 or len(tn_c) > 1):
        if len(tm_c) > 1 and (tm >= tn or len(tn_c) == 1):
            tm_c = tm_c[1:]
        else:
            tn_c = tn_c[1:]
        tm, tn = tm_c[0], tn_c[0]
    starts = [sum(nks[:p]) for p in range(len(pairs))]
    nk = sum(nks)
    np_ = len(pairs)

    def body(*refs):
        o_ref, acc = refs[2 * np_], refs[2 * np_ + 1]
        kk = pl.program_id(2)

        @pl.when(kk == 0)
        def _():
            acc[...] = jnp.zeros_like(acc)

        for p in range(np_):
            def add(p=p):
                a = refs[2 * p][...].astype(BF16)
                b = refs[2 * p + 1][...].astype(BF16)
                dn = (((0 if ta else 1,), (1 if tb else 0,)), ((), ()))
                acc[...] += lax.dot_general(a, b, dn, preferred_element_type=F32)
            if np_ == 1:
                add()
            else:
                pl.when((kk >= starts[p]) & (kk < starts[p] + nks[p]))(add)

        @pl.when(kk == nk - 1)
        def _():
            o_ref[...] = acc[...].astype(o_ref.dtype)

    in_specs, args = [], []
    for p, (a, b, off) in enumerate(pairs):
        tk, s0, nkp = tks[p], starts[p], nks[p]
        assert off % tk == 0
        boff = off // tk

        def kloc(k, s0=s0, nkp=nkp):
            return jnp.clip(k - s0, 0, nkp - 1)
        if ta:
            in_specs.append(pl.BlockSpec((tk, tm), lambda i, j, k, kloc=kloc: (kloc(k), i)))
        else:
            in_specs.append(pl.BlockSpec((tm, tk), lambda i, j, k, kloc=kloc: (i, kloc(k))))
        if tb:
            in_specs.append(pl.BlockSpec((tn, tk), lambda i, j, k, kloc=kloc, boff=boff: (j, boff + kloc(k))))
        else:
            in_specs.append(pl.BlockSpec((tk, tn), lambda i, j, k, kloc=kloc, boff=boff: (boff + kloc(k), j)))
        args += [a, b]
    return pl.pallas_call(
        body, grid=(m // tm, n // tn, nk), in_specs=in_specs,
        out_specs=pl.BlockSpec((tm, tn), lambda i, j, k: (i, j)),
        out_shape=jax.ShapeDtypeStruct((m, n), out_dtype),
        scratch_shapes=[pltpu.VMEM((tm, tn), F32)],
        name=name, compiler_params=_cparams(("arbitrary", "arbitrary", "arbitrary")),
    )(*args)


def _rms(x, g):
    return x * lax.rsqrt(jnp.mean(x * x, axis=-1, keepdims=True) + RMS_EPS) * g


def f_nm0(x, g, sh, sc):
    return _rms(x, g) * (1.0 + sc) + sh


def f_nm(xp, o, gate, g, sh, sc):
    x = xp + gate * o
    return x, _rms(x, g) * (1.0 + sc) + sh


def f_final(xp, o, gate, g, tgt, valid):
    x = xp + gate * o
    e = (_rms(x, g) - tgt) * valid
    return jnp.sum(e * e, axis=0, keepdims=True) * (0.5 / D)


@functools.partial(jax.custom_vjp, nondiff_argnums=(1,))
def _sroll(x, s):
    return pltpu.roll(x, s, 0)


def _sroll_fwd(x, s):
    return pltpu.roll(x, s, 0), None


def _sroll_bwd(s, _, g):
    return (pltpu.roll(g, (g.shape[0] - s) % g.shape[0], 0),)


_sroll.defvjp(_sroll_fwd, _sroll_bwd)


def _shifted(x, o):
    n = x.shape[0]
    return x if o == 0 else _sroll(x, (n - o) % n)


def f_conv1d(x, w, b, *, lc, act):
    n = x.shape[0]
    pos = lax.broadcasted_iota(jnp.int32, (n, 1), 0)
    lo = jnp.where(pos < lc, 0, lc)
    hi = jnp.where(pos < lc, lc, n)
    y = x * w[1:2] + b
    for k, o in ((0, -1), (2, 1), (3, 2)):
        src = pos + o
        valid = (src >= lo) & (src < hi)
        y = y + jnp.where(valid, _shifted(x, o), 0.0) * w[k:k + 1]
    return jax.nn.silu(y) if act else y


def ffnconv_masks(n, lc):
    pos = lax.broadcasted_iota(jnp.int32, (n, 128), 0)
    is_ctx = pos < lc
    tl = pos - lc
    r = tl // GRID_W
    cc = tl - r * GRID_W
    rows = (n - lc) // GRID_W
    left = jnp.where(is_ctx, pos >= 1, cc >= 1)
    right = jnp.where(is_ctx, pos < lc - 1, cc < GRID_W - 1)
    above = jnp.logical_not(is_ctx) & (r >= 1)
    below = jnp.logical_not(is_ctx) & (r < rows - 1)
    return jnp.stack([left, right, above, below]).astype(F32)


def f_ffnconv(a, up, w, b, mk):
    cols = (mk[0] * _shifted(a, -1), a, mk[1] * _shifted(a, 1))
    y = b
    for dr in (-1, 0, 1):
        k = 3 * (dr + 1)
        inner = cols[0] * w[k:k + 1] + cols[1] * w[k + 1:k + 2] + cols[2] * w[k + 2:k + 3]
        y = y + (inner if dr == 0 else mk[2 + (dr > 0)] * _shifted(inner, GRID_W * dr))
    return jax.nn.silu(y) * up


def f_ssd(xs, bc, dtraw, bias, alog, st, *, d, reverse):
    L = xs.shape[0]
    dtv = jax.nn.softplus(dtraw + bias)
    la = dtv * (-jnp.exp(alog))
    ri = lax.broadcasted_iota(jnp.int32, (L, L), 0)
    ci = lax.broadcasted_iota(jnp.int32, (L, L), 1)
    mask = (ci >= ri) if reverse else (ci <= ri)
    cum = jnp.dot(mask.astype(F32), la, precision=HI, preferred_element_type=F32)
    cum_t = cum.T
    tot = cum[0:1] if reverse else cum[L - 1:L]
    lo = lax.broadcasted_iota(jnp.int32, (1, 128), 1) < 64
    rlo = lax.broadcasted_iota(jnp.int32, (128, 1), 0) < 64
    ys, new = [], []
    for g in range(2):
        bg = bc[:, g * 128:(g + 1) * 128].astype(BF16)
        cg = bc[:, 256 + g * 128:256 + (g + 1) * 128].astype(BF16)
        cb = lax.dot_general(cg, bg, (((1,), (1,)), ((), ())), preferred_element_type=F32)
        sg = st[4 * g:4 * g + 4].reshape(4 * 128, 128)
        ch_all = lax.dot_general(cg, sg.astype(BF16), (((1,), (1,)), ((), ())), preferred_element_type=F32)
        xes, dcols = [], []
        for jj in range(4):
            j = 4 * g + jj
            x = xs[:, j * 128:(j + 1) * 128]
            k1 = 16 * d + 2 * j
            k2 = k1 + 1
            c1, c2 = cum[:, k1:k1 + 1], cum[:, k2:k2 + 1]
            m1 = cb * jnp.exp(jnp.where(mask, c1 - cum_t[k1:k1 + 1, :], -1e30))
            m2 = cb * jnp.exp(jnp.where(mask, c2 - cum_t[k2:k2 + 1, :], -1e30))
            xdt = x * jnp.where(lo, dtv[:, k1:k1 + 1], dtv[:, k2:k2 + 1])
            mcat = jnp.concatenate([m1, m2], axis=1).astype(BF16)
            xcat = jnp.concatenate([jnp.where(lo, xdt, 0.0), jnp.where(lo, 0.0, xdt)], axis=0).astype(BF16)
            y = jnp.dot(mcat, xcat, preferred_element_type=F32)
            y = y + ch_all[:, jj * 128:(jj + 1) * 128] * jnp.where(lo, jnp.exp(c1), jnp.exp(c2))
            t1, t2 = tot[:, k1:k1 + 1], tot[:, k2:k2 + 1]
            xes.append((xdt * jnp.where(lo, jnp.exp(t1 - c1), jnp.exp(t2 - c2))).astype(BF16))
            dcols.append(jnp.where(rlo, jnp.exp(t1), jnp.exp(t2)))
            ys.append(y)
        upd = lax.dot_general(jnp.concatenate(xes, axis=1), bg, (((0,), (0,)), ((), ())), preferred_element_type=F32)
        new.append((sg * jnp.concatenate(dcols, axis=0) + upd).reshape(4, 128, 128))
    return jnp.concatenate(ys, axis=1), jnp.concatenate(new, axis=0)


def f_hgrn(q_raw, f_raw, v, lb, zt, *, reverse, chunk):
    n = q_raw.shape[0]
    c = chunk
    qa = jax.nn.silu(q_raw)
    logf = jnp.log(lb + (1.0 - lb) * jax.nn.sigmoid(f_raw))
    kk = (1.0 - lb) * jax.nn.sigmoid(-f_raw)
    ri = lax.broadcasted_iota(jnp.int32, (n, n), 0)
    ci = lax.broadcasted_iota(jnp.int32, (n, n), 1)
    tmat = ((ri // c == ci // c) & ((ci >= ri) if reverse else (ci <= ri))).astype(F32)
    cum_all = jnp.dot(tmat, logf, precision=HI, preferred_element_type=F32)
    r3 = lax.broadcasted_iota(jnp.int32, (c, c, 128), 0)
    c3 = lax.broadcasted_iota(jnp.int32, (c, c, 128), 1)
    mask3 = (c3 >= r3) if reverse else (c3 <= r3)
    nch = n // c
    outs = [None] * nch
    for chn in (reversed(range(nch)) if reverse else range(nch)):
        sl = slice(chn * c, (chn + 1) * c)
        q, k, vv, cum = qa[sl], kk[sl], v[sl], cum_all[sl]
        dec = jnp.exp(jnp.where(mask3, cum[:, None, :] - cum[None, :, :], -1e30))
        att = jnp.sum(q[:, None, :] * dec * k[None, :, :], axis=-1, keepdims=True)
        y = jnp.sum(att * vv[None, :, :], axis=1)
        y = y + lax.dot_general((q * jnp.exp(cum)).astype(BF16), zt.astype(BF16), (((1,), (1,)), ((), ())),
                                preferred_element_type=F32)
        tot = cum[0:1] if reverse else cum[c - 1:c]
        kd = (k * jnp.exp(tot - cum)).astype(BF16)
        zt = zt * jnp.exp(tot) + lax.dot_general(vv.astype(BF16), kd, (((0,), (0,)), ((), ())), preferred_element_type=F32)
        outs[chn] = y
    return jnp.concatenate(outs, axis=0), zt


def f_hgrn_group(q_raw, f_raw, v, lb, zt, *, reverse, chunk):
    ys, zs = [], []
    for h in range(q_raw.shape[1] // 128):
        sl = slice(h * 128, (h + 1) * 128)
        y, z = f_hgrn(q_raw[:, sl], f_raw[:, sl], v[:, sl], lb[:, sl], zt[sl], reverse=reverse, chunk=chunk)
        ys.append(y)
        zs.append(z)
    return jnp.concatenate(ys, axis=1), jnp.concatenate(zs, axis=0)


def _expm1(x):
    poly = x * (1.0 + x * (0.5 + x * (1.0 / 6 + x * (1.0 / 24 + x * (1.0 / 120 + x * (1.0 / 720))))))
    return jnp.where(jnp.abs(x) < 0.3, poly, jnp.exp(x) - 1.0)


def f_gates(u, wa, ba, wi, bi, lam):
    rs, is_ = [], []
    for nb in range(8):
        un = u[:, nb * 128:(nb + 1) * 128].astype(BF16)
        rs.append(jnp.dot(un, wa[nb].astype(BF16), preferred_element_type=F32))
        is_.append(jnp.dot(un, wi[nb].astype(BF16), preferred_element_type=F32))
    r = jax.nn.sigmoid(jnp.concatenate(rs, axis=1) + ba)
    i = jax.nn.sigmoid(jnp.concatenate(is_, axis=1) + bi)
    log_a = -LRU_C * jax.nn.softplus(-lam) * r
    return jnp.exp(log_a), jnp.sqrt(-_expm1(2.0 * log_a)) * (i * u)


def f_ssdfin(y0, y1, xs, z, gy, h0, h1, dpad, ng):
    kk = lax.broadcasted_iota(jnp.int32, (128, D), 0)
    ch = lax.broadcasted_iota(jnp.int32, (128, D), 1)
    expand = (ch // 64 == kk).astype(F32)
    dvec = jnp.dot(dpad, expand, precision=HI, preferred_element_type=F32)[0:1]
    y = y0 + y1 + dvec * xs
    yn = _rms(y * jax.nn.silu(z), ng)
    r = (h0 + h1) * jax.nn.gelu(gy)
    return jnp.concatenate([yn, r], axis=1)


def f_oddfin(o0, o1, g, y0, y1, u, hn, sd, gw, gb):
    parts = []
    for h in range(HG_HEADS):
        sl = slice(h * 128, (h + 1) * 128)
        parts.append(_rms(o0[:, sl] + o1[:, sl], hn[h:h + 1]) * jax.nn.silu(g[:, sl]))
    y = jax.nn.gelu(y0 + y1 + sd * u)
    y = y * jax.nn.sigmoid(jnp.dot(y.astype(BF16), gw.astype(BF16), preferred_element_type=F32) + gb)
    return jnp.concatenate(parts + [y], axis=1)


def f_s5p(lre, lim, lstep, btr, bti):
    step = jnp.exp(lstep)
    mag = jnp.exp(lre * step)
    ar, ai = mag * jnp.cos(lim * step), mag * jnp.sin(lim * step)
    den = lre * lre + lim * lim
    zr = ((ar - 1.0) * lre + ai * lim) / den
    zi = (ai * lre - (ar - 1.0) * lim) / den
    bbr = zr[:, None, :] * btr - zi[:, None, :] * bti
    bbi = zr[:, None, :] * bti + zi[:, None, :] * btr
    return ar, ai, bbr, bbi


def f_lb(logits):
    m = jnp.max(logits, axis=0, keepdims=True)
    e = jnp.exp(logits - m)
    p = e / jnp.sum(e, axis=0, keepdims=True)
    return p[1:2], p[1:2] + p[2:3] + p[3:4]


def f_adamw(w, m, v, *gs):
    g = gs[0]
    for t in gs[1:]:
        g = g + t
    m = ADAM_B1 * m + (1.0 - ADAM_B1) * g
    v = ADAM_B2 * v + (1.0 - ADAM_B2) * jnp.square(g)
    m_hat = m / (1.0 - ADAM_B1 ** ADAM_STEP)
    v_hat = v / (1.0 - ADAM_B2 ** ADAM_STEP)
    delta = -ADAM_LR * (m_hat / (jnp.sqrt(v_hat) + ADAM_EPS) + ADAM_WD * w)
    return g, delta, m, v


def scan_fwd(name, f, grid, ins, y_out, st_out, state_shape, is_first):
    n_in = len(ins)

    def body(*refs):
        y_ref, so_ref, st = refs[n_in], refs[n_in + 1], refs[n_in + 2]
        ids = [pl.program_id(a) for a in range(len(grid))]

        @pl.when(is_first(*ids))
        def _():
            st[...] = jnp.zeros_like(st)

        s = st[...]
        so_ref[...] = s
        y, new = f(*[r[...] for r in refs[:n_in]], s)
        y_ref[...] = y.astype(y_ref.dtype)
        st[...] = new

    return pl.pallas_call(
        body, grid=grid,
        in_specs=[pl.BlockSpec(b, m) for _, b, m in ins],
        out_specs=[pl.BlockSpec(y_out[2], y_out[3]), pl.BlockSpec(st_out[2], st_out[3])],
        out_shape=[jax.ShapeDtypeStruct(y_out[0], y_out[1]), jax.ShapeDtypeStruct(st_out[0], st_out[1])],
        scratch_shapes=[pltpu.VMEM(state_shape, F32)],
        name=name, compiler_params=_cparams(("arbitrary",) * len(grid)),
    )(*[a for a, _, _ in ins])


def scan_bwd(name, f, grid, ins, st_in, dy, grads, state_shape, is_first):
    n_in = len(ins)
    didx = [g[0] for g in grads]

    def body(*refs):
        s_ref, dy_ref = refs[n_in], refs[n_in + 1]
        g_refs = refs[n_in + 2:n_in + 2 + len(grads)]
        dst = refs[n_in + 2 + len(grads)]
        ids = [pl.program_id(a) for a in range(len(grid))]

        @pl.when(is_first(*ids))
        def _():
            dst[...] = jnp.zeros_like(dst)

        vals = [r[...] for r in refs[:n_in]]

        def fd(s, *dv):
            full = list(vals)
            for i, v in zip(didx, dv):
                full[i] = v
            return f(*full, s)

        (y, _), vjp = jax.vjp(fd, s_ref[...], *[vals[i] for i in didx])
        gs = vjp((dy_ref[...].astype(y.dtype), dst[...]))
        dst[...] = gs[0]
        for g, spec, ref in zip(gs[1:], grads, g_refs):
            first = spec[5]
            if first is None:
                ref[...] = g.astype(ref.dtype)
            else:
                fst = first(*ids)

                @pl.when(fst)
                def _(ref=ref, g=g):
                    ref[...] = g.astype(ref.dtype)

                @pl.when(jnp.logical_not(fst))
                def _(ref=ref, g=g):
                    ref[...] += g.astype(ref.dtype)

    allin = list(ins) + [st_in, dy]
    return pl.pallas_call(
        body, grid=grid,
        in_specs=[pl.BlockSpec(b, m) for _, b, m in allin],
        out_specs=[pl.BlockSpec(s[3], s[4]) for s in grads],
        out_shape=[jax.ShapeDtypeStruct(s[1], s[2]) for s in grads],
        scratch_shapes=[pltpu.VMEM(state_shape, F32)],
        name=name, compiler_params=_cparams(("arbitrary",) * len(grid)),
    )(*[a for a, _, _ in allin])


def _tile_order(order, nt, nctx=1):
    rev = lambda j: jnp.where(j < nctx, nctx - 1 - j, nt - 1 - (j - nctx))
    if order == 'F':
        return (lambda j: j), True
    if order == 'Fb':
        return (lambda j: nt - 1 - j), False
    if order == 'R':
        return rev, False
    return (lambda j: rev(nt - 1 - j)), True


def _scan8(coef, val, sub, asc):
    for step in (1, 2, 4):
        shift = step if asc else 8 - step
        keep = (sub >= step) if asc else (sub < 8 - step)
        val = jnp.where(keep, coef * pltpu.roll(val, shift, 0) + val, val)
        coef = jnp.where(keep, coef * pltpu.roll(coef, shift, 0), coef)
    return coef, val


def _prev_rows(tile, carry, sub, asc):
    return jnp.where(sub == 0, carry, pltpu.roll(tile, 1, 0)) if asc else jnp.where(sub == 7, carry, pltpu.roll(tile, 7, 0))


def _last_row(tile, asc):
    return jnp.broadcast_to(tile[7:8] if asc else tile[0:1], tile.shape)


def linrec(name, a, b, order):
    bsz, tt, cols = a.shape
    tq = _pick(tt, (256, 128))
    nt, ng, nj = tt // tq, tq // 8, cols // 128
    phys, asc = _tile_order(order, nt)

    def body(a_ref, b_ref, h_ref, hp_ref, hc):
        @pl.when(pl.program_id(0) == 0)
        def _():
            hc[...] = jnp.zeros_like(hc)

        sub = lax.broadcasted_iota(jnp.int32, (8, 128), 0)

        def group(i, carry):
            rows = pl.ds(pl.multiple_of((i if asc else ng - 1 - i) * 8, 8), 8)
            for bi in range(bsz):
                for j in range(nj):
                    cs = slice(j * 128, (j + 1) * 128)
                    h_in = hc[bi, j]
                    ca, cv = _scan8(a_ref[bi, rows, cs], b_ref[bi, rows, cs], sub, asc)
                    h = ca * h_in + cv
                    h_ref[bi, rows, cs] = h
                    hp_ref[bi, rows, cs] = _prev_rows(h, h_in, sub, asc)
                    hc[bi, j] = _last_row(h, asc)
            return carry

        lax.fori_loop(0, ng, group, 0)

    spec = pl.BlockSpec((bsz, tq, cols), lambda j: (0, phys(j), 0))
    return pl.pallas_call(
        body, grid=(nt,), in_specs=[spec, spec], out_specs=[spec, spec],
        out_shape=[jax.ShapeDtypeStruct(a.shape, F32)] * 2,
        scratch_shapes=[pltpu.VMEM((bsz, nj, 8, 128), F32)],
        name=name, compiler_params=_cparams(("arbitrary",)),
    )(a, b)


def linrec_bwd(name, a, dh, hprev, order):
    bsz, tt, cols = a.shape
    tq = _pick(tt, (256, 128))
    nt, ng, nj = tt // tq, tq // 8, cols // 128
    phys, asc = _tile_order(order, nt)

    def body(a_ref, dh_ref, hp_ref, g_ref, ga_ref, gc, ac):
        @pl.when(pl.program_id(0) == 0)
        def _():
            gc[...] = jnp.zeros_like(gc)
            ac[...] = jnp.zeros_like(ac)

        sub = lax.broadcasted_iota(jnp.int32, (8, 128), 0)

        def group(i, carry):
            rows = pl.ds(pl.multiple_of((i if asc else ng - 1 - i) * 8, 8), 8)
            for bi in range(bsz):
                for j in range(nj):
                    cs = slice(j * 128, (j + 1) * 128)
                    a_tile = a_ref[bi, rows, cs]
                    ca, cv = _scan8(_prev_rows(a_tile, ac[bi, j], sub, asc), dh_ref[bi, rows, cs], sub, asc)
                    g = ca * gc[bi, j] + cv
                    g_ref[bi, rows, cs] = g
                    ga_ref[bi, rows, cs] = g * hp_ref[bi, rows, cs]
                    gc[bi, j] = _last_row(g, asc)
                    ac[bi, j] = _last_row(a_tile, asc)
            return carry

        lax.fori_loop(0, ng, group, 0)

    spec = pl.BlockSpec((bsz, tq, cols), lambda j: (0, phys(j), 0))
    return pl.pallas_call(
        body, grid=(nt,), in_specs=[spec, spec, spec], out_specs=[spec, spec],
        out_shape=[jax.ShapeDtypeStruct(a.shape, F32)] * 2,
        scratch_shapes=[pltpu.VMEM((bsz, nj, 8, 128), F32), pltpu.VMEM((bsz, nj, 8, 128), F32)],
        name=name, compiler_params=_cparams(("arbitrary",)),
    )(a, dh, hprev)


def _cmul(a, b):
    return a[0] * b[0] - a[1] * b[1], a[0] * b[1] + a[1] * b[0]


def _cpow_tables(ar, ai, asc):
    pows = [(ar, ai)]
    for _ in range(7):
        pows.append(_cmul(pows[-1], (ar, ai)))
    tile = lambda p: jnp.broadcast_to(p[:, None, :], (8, 8, 128))
    steps = jnp.stack([jnp.stack([tile(pows[s - 1][0]), tile(pows[s - 1][1])]) for s in (1, 2, 4)])
    order = range(8) if asc else range(7, -1, -1)
    carry = jnp.stack([jnp.stack([pows[i][c] for i in order], axis=1) for c in (0, 1)])
    return steps, carry


def _cscan8(xr, xi, st_ref, j, sub, asc):
    for s, step in enumerate((1, 2, 4)):
        shift = step if asc else 8 - step
        keep = (sub >= step) if asc else (sub < 8 - step)
        pr, pi = st_ref[s, 0, j], st_ref[s, 1, j]
        rr, ri = pltpu.roll(xr, shift, 0), pltpu.roll(xi, shift, 0)
        xr, xi = jnp.where(keep, xr + pr * rr - pi * ri, xr), jnp.where(keep, xi + pr * ri + pi * rr, xi)
    return xr, xi


def clinrec(name, x, coef, order, lc, conj=False, hprev=None):
    btot, tt, cols2 = x.shape
    cols = cols2 // 2
    bsz, ngrp = btot, 1
    tq = 128
    nt, ng, nj = tt // tq, tq // 8, cols // 128
    phys, asc = _tile_order(order, nt, lc // tq)
    steps, carry = _cpow_tables(coef[0], -coef[1] if conj else coef[1], asc)
    adjoint = hprev is not None

    def body(*refs):
        if adjoint:
            x_ref, hp_ref, st_ref, cr_ref, h_ref, dc_ref, hc = refs
        else:
            x_ref, st_ref, cr_ref, h_ref, hp_ref, hc = refs

        @pl.when(pl.program_id(1) == 0)
        def _():
            hc[...] = jnp.zeros_like(hc)

        if adjoint:
            @pl.when((pl.program_id(0) == 0) & (pl.program_id(1) == 0))
            def _():
                dc_ref[...] = jnp.zeros_like(dc_ref)

        sub = lax.broadcasted_iota(jnp.int32, (8, 128), 0)

        def group(i, c_):
            rows = pl.ds(pl.multiple_of((i if asc else ng - 1 - i) * 8, 8), 8)
            for bi in range(bsz):
                for j in range(nj):
                    cr, ci = slice(j * 128, (j + 1) * 128), slice(cols + j * 128, cols + (j + 1) * 128)
                    sr, si = _cscan8(x_ref[bi, rows, cr], x_ref[bi, rows, ci], st_ref, j, sub, asc)
                    in_r, in_i = hc[bi, 0, j], hc[bi, 1, j]
                    pr, pi = cr_ref[0, j], cr_ref[1, j]
                    hr = sr + pr * in_r - pi * in_i
                    hi = si + pr * in_i + pi * in_r
                    h_ref[bi, rows, cr] = hr
                    h_ref[bi, rows, ci] = hi
                    if adjoint:
                        qr, qi = hp_ref[bi, rows, cr], hp_ref[bi, rows, ci]
                        dc_ref[0, j] += hr * qr + hi * qi
                        dc_ref[1, j] += hi * qr - hr * qi
                    else:
                        hp_ref[bi, rows, cr] = _prev_rows(hr, in_r, sub, asc)
                        hp_ref[bi, rows, ci] = _prev_rows(hi, in_i, sub, asc)
                    hc[bi, 0, j] = _last_row(hr, asc)
                    hc[bi, 1, j] = _last_row(hi, asc)
            return c_

        lax.fori_loop(0, ng, group, 0)

    spec = pl.BlockSpec((bsz, tq, cols2), lambda g, j: (g, phys(j), 0))
    full = lambda t: pl.BlockSpec(t.shape, lambda g, j, n=t.ndim: (0,) * n)
    dc_shape = (2, nj, 8, 128)
    if adjoint:
        ins, in_specs = (x, hprev, steps, carry), [spec, spec, full(steps), full(carry)]
        out_specs = [spec, pl.BlockSpec(dc_shape, lambda g, j: (0, 0, 0, 0))]
        out_shape = [jax.ShapeDtypeStruct(x.shape, F32), jax.ShapeDtypeStruct(dc_shape, F32)]
    else:
        ins, in_specs = (x, steps, carry), [spec, full(steps), full(carry)]
        out_specs = [spec, spec]
        out_shape = [jax.ShapeDtypeStruct(x.shape, F32)] * 2
    return pl.pallas_call(
        body, grid=(ngrp, nt), in_specs=in_specs, out_specs=out_specs, out_shape=out_shape,
        scratch_shapes=[pltpu.VMEM((bsz, 2, nj, 8, 128), F32)],
        name=name, compiler_params=_cparams(("arbitrary", "arbitrary")),
    )(*ins)


def _blockdiag(bb):
    eye = jnp.eye(16, dtype=bb.dtype)
    return (bb[:, :, None, :] * eye[:, None, :, None]).reshape(256, 1024)


def _blockdiag_t(c):
    eye = jnp.eye(16, dtype=c.dtype)
    return (jnp.swapaxes(c, 1, 2)[:, :, None, :] * eye[:, None, :, None]).reshape(1024, 256)


def _unblockdiag(m):
    eye = jnp.eye(16, dtype=m.dtype)
    return jnp.sum(m.reshape(16, 16, 16, 64) * eye[:, None, :, None], axis=2)


def _unblockdiag_t(m):
    eye = jnp.eye(16, dtype=m.dtype)
    return jnp.swapaxes(jnp.sum(m.reshape(16, 64, 16, 16) * eye[:, None, :, None], axis=2), 1, 2)


def _pad_rows(v, rows=8, cols=128):
    out = jnp.zeros((rows, cols), F32)
    return out.at[0, :v.shape[0]].set(v)


def cond_rows(c, c_ctx):
    return jnp.zeros((8, D), F32).at[:c.shape[0]].set(c).at[4].set(c_ctx)


def mod_table(name, cc_rows, w, b):
    nl, _, cols = w.shape
    nr = cc_rows.shape[0]

    def f(ccv, wv, bv):
        return jnp.dot(jax.nn.silu(ccv).astype(BF16), wv, preferred_element_type=F32) + bv

    return tile_fwd(name, f, (nl, cols // 1536),
                    [(cc_rows, (nr, D), lambda l, n: (0, 0)), (w, (None, D, 1536), lambda l, n: (l, 0, n)),
                     (b.reshape(nl, 1, cols), (None, 1, 1536), lambda l, n: (l, 0, n))],
                    [((nl, nr, cols), F32, (None, nr, 1536), lambda l, n: (l, 0, n))])[0]


def mod_cond_grad(name, cc_rows, dmod_rows, w):
    nl, nr, cols = dmod_rows.shape

    def body(cc_ref, d_ref, w_ref, o_ref):
        _, vjp = jax.vjp(jax.nn.silu, cc_ref[...])
        ds = lax.dot_general(d_ref[...].astype(BF16), w_ref[...], (((1,), (1,)), ((), ())), preferred_element_type=F32)
        (dcc,) = vjp(ds)
        first = (pl.program_id(0) == 0) & (pl.program_id(1) == 0)

        @pl.when(first)
        def _():
            o_ref[...] = dcc

        @pl.when(jnp.logical_not(first))
        def _():
            o_ref[...] += dcc

    return pl.pallas_call(
        body, grid=(nl, cols // 1536),
        in_specs=[pl.BlockSpec((nr, D), lambda l, n: (0, 0)), pl.BlockSpec((None, nr, 1536), lambda l, n: (l, 0, n)),
                  pl.BlockSpec((None, D, 1536), lambda l, n: (l, 0, n))],
        out_specs=pl.BlockSpec((nr, D), lambda l, n: (0, 0)), out_shape=jax.ShapeDtypeStruct((nr, D), F32),
        name=name, compiler_params=_cparams(("arbitrary", "arbitrary")),
    )(cc_rows, dmod_rows, w)


def local_step(x, ctx, target, mod, W):
    B, Tx, _ = x.shape
    Lc = ctx.shape[1]
    Tt = Lc + Tx
    tb = Lc
    nt = Tt // tb
    M = B * Tt
    nc = Tt // SSD_CHUNK
    ncc = Lc // SSD_CHUNK
    fgr, fgw = HG_HEADS // HG_FWD[1], 128 * HG_FWD[1]
    bgr, bgw = HG_HEADS // HG_BWD[1], 128 * HG_BWD[1]
    G = {}

    def add_grad(name, idx, val):
        G.setdefault(name, {})[idx] = val

    def tok(a, cb=None, off=0):
        cb = a.shape[-1] if cb is None else cb
        return (a, (None, tb, cb), lambda b, j, off=off: (b, j, off))

    def tok_out(cols, dtype=F32):
        return ((B, Tt, cols), dtype, (None, tb, cols), lambda b, j: (b, j, 0))

    def vec(a):
        return (a, a.shape, lambda *ids, n=a.ndim: (0,) * n)

    def vec_acc(shape):
        return (shape, F32, shape, lambda *ids, n=len(shape): (0,) * n, lambda *ids: functools.reduce(jnp.logical_and, [i == 0 for i in ids]))

    def modv(l, which):
        return (modr, (None, None, None, 1, D), lambda b, j, l=l, which=which: (l, jnp.where(j == 0, 4, b), which, 0, 0))

    dmod_spec = ((B, 2, 1, D), F32, (None, None, 1, D), lambda b, j: (b, jnp.where(j == 0, 0, 1), 0, 0), lambda b, j: j <= 1)

    def phys_chunk(n_all, n_ctx, reverse):
        if not reverse:
            return lambda s: s
        return lambda s: jnp.where(s < n_ctx, n_ctx - 1 - s, n_all - 1 - (s - n_ctx))

    nmc = N_MOD * D // 1536
    modr = mod.reshape(DEPTH, 8, N_MOD, 1, D)
    dmods = {}

    lb1, lb3 = tile_fwd('lb_fwd', f_lb, (1,), [vec(W['hg_lb_logits'])],
                        [((1, HG_W), F32, (1, HG_W), lambda i: (0, 0))] * 2)
    dlb = {1: [], 3: []}

    x0 = jnp.concatenate([ctx, x], axis=1)
    conv_mk = ffnconv_masks(Tt, Lc)
    R = [dict() for _ in range(DEPTH)]

    xprev, oprev = x0, None
    for l in range(DEPTH):
        r = R[l]
        j = l // 2
        ng = W['norm_mix_g'][l][None]
        if l == 0:
            h1 = tile_fwd(f'nm0_fwd', f_nm0, (B, nt), [tok(xprev), vec(ng), modv(l, 0), modv(l, 1)], [tok_out(D, BF16)])[0]
            xa = xprev
        else:
            xa, h1 = tile_fwd(f'nm_mix_fwd{l}', f_nm, (B, nt),
                              [tok(xprev), tok(oprev), modv(l - 1, 5), vec(ng), modv(l, 0), modv(l, 1)],
                              [tok_out(D), tok_out(D, BF16)])
        r['xin'], r['oin'], r['xa'], r['h1'] = xprev, oprev, xa, h1
        h1m = h1.reshape(M, D)
        if l % 2 == 0:
            win = W['ev_w_in_p'][j]
            proj = mm(f'ev_proj{l}', [(h1m, win, 0)]).reshape(B, Tt, EV_PAD)
            r['proj'] = proj
            scw, scb = W['ssd_conv_w'][j], W['ssd_conv_b'][j][None]
            lcw, lcb = W['lru_conv_w'][j], W['lru_conv_b'][j][None]

            def conv_call(name, colblk0, w, b, wblk0, ncols, act):
                return tile_fwd(name, functools.partial(f_conv1d, lc=Lc, act=act), (ncols // 256, B),
                                [(proj, (None, Tt, 256), lambda cb, bi: (bi, 0, colblk0 + cb)),
                                 (w, (4, 256), lambda cb, bi: (0, wblk0 + cb)), (b, (1, 256), lambda cb, bi: (0, wblk0 + cb))],
                                [((B, Tt, ncols), F32, (None, Tt, 256), lambda cb, bi: (bi, 0, cb))])[0]
            xs_c = conv_call(f'conv_xs{l}', 12, scw, scb, 0, 1024, True)
            bc_c = conv_call(f'conv_bc{l}', 16, scw, scb, 4, 512, True)
            u_c = conv_call(f'conv_u{l}', 8, lcw, lcb, 0, 1024, False)
            r['xs'], r['bc'], r['u'] = xs_c, bc_c, u_c
            bias = _pad_rows(W['ssd_dt_bias'][j].reshape(-1), 1)
            alog = _pad_rows(W['ssd_a_log'][j].reshape(-1), 1)
            r['bias'], r['alog'] = bias, alog
            r['y'], r['st'], r['a4'], r['hp4'], r['h'] = [], [], [], [], []
            for d in range(2):
                ph = phys_chunk(nc, ncc, d == 1)
                y, st = scan_fwd(
                    f'ssd_fwd{l}_{d}', functools.partial(f_ssd, d=d, reverse=(d == 1)), (B, nc),
                    [(xs_c, (None, SSD_CHUNK, D), lambda b, s, ph=ph: (b, ph(s), 0)),
                     (bc_c, (None, SSD_CHUNK, 512), lambda b, s, ph=ph: (b, ph(s), 0)),
                     (proj, (None, SSD_CHUNK, 128), lambda b, s, ph=ph: (b, ph(s), 36)),
                     vec(bias), vec(alog)],
                    ((B, Tt, D), F32, (None, SSD_CHUNK, D), lambda b, s, ph=ph: (b, ph(s), 0)),
                    ((B, nc, 8, 128, 128), F32, (None, None, 8, 128, 128), lambda b, s: (b, s, 0, 0, 0)),
                    (8, 128, 128), lambda b, s: s == 0)
                r['y'].append(y)
                r['st'].append(st)
                a_d, bx_d = tile_fwd(
                    f'gates_fwd{l}_{d}', f_gates, (B, nt),
                    [tok(u_c), vec(W['lru_w_a'][j, d]), vec(W['lru_b_a'][j, d][None]), vec(W['lru_w_i'][j, d]),
                     vec(W['lru_b_i'][j, d][None]), vec(W['lru_lam'][j, d][None])],
                    [tok_out(D), tok_out(D)])
                h_d, hp_d = linrec(f'lru_fwd{l}_{d}', a_d, bx_d, 'F' if d == 0 else 'R')
                r['a4'].append(a_d)
                r['hp4'].append(hp_d)
                r['h'].append(h_d)
            dpad = _pad_rows(W['ssd_d'][j])
            sng = W['ssd_norm_g'][j][None]
            r['dpad'], r['sng'] = dpad, sng
            mix = tile_fwd(f'ssdfin_fwd{l}', f_ssdfin, (B, nt),
                           [tok(r['y'][0]), tok(r['y'][1]), tok(xs_c), tok(proj, D, 0), tok(proj, D, 1), tok(r['h'][0]),
                            tok(r['h'][1]), vec(dpad), vec(sng)], [tok_out(2 * D, BF16)])[0]
            wout = W['ev_w_out'][j]
        else:
            win = W['od_w_in'][j]
            proj = mm(f'od_proj{l}', [(h1m, win, 0)]).reshape(B, Tt, 4096)
            r['proj'] = proj
            lbv = lb1 if l == 1 else lb3
            ns = nc
            r['o'], r['zst'], r['coef'], r['bcat'], r['ccat'], r['hp5'], r['hcat'], r['yd'], r['s5in'] = [], [], [], [], [], [], [], [], []
            u2 = proj[:, :, 3840:].reshape(M, S5_W)
            r['u2'] = u2
            for d in range(2):
                ph = phys_chunk(ns, ncc, d == 1)
                o_d, zst = scan_fwd(
                    f'hgrn_fwd{l}_{d}', functools.partial(f_hgrn_group, reverse=(d == 1), chunk=HG_FWD[0]), (fgr, B, ns),
                    [(proj, (None, 128, fgw), lambda h, b, s, ph=ph: (b, ph(s), h)),
                     (proj, (None, 128, fgw), lambda h, b, s, ph=ph, d=d: (b, ph(s), (1 + d) * fgr + h)),
                     (proj, (None, 128, fgw), lambda h, b, s, ph=ph: (b, ph(s), 3 * fgr + h)),
                     (lbv, (1, fgw), lambda h, b, s: (0, h))],
                    ((B, Tt, HG_W), F32, (None, 128, fgw), lambda h, b, s, ph=ph: (b, ph(s), h)),
                    ((B, ns, fgr, fgw, 128), F32, (None, None, None, fgw, 128), lambda h, b, s: (b, s, h, 0, 0)),
                    (fgw, 128), lambda h, b, s: s == 0)
                r['o'].append(o_d)
                r['zst'].append(zst)
                s5in = [W['s5_lam_re'][j, d], W['s5_lam_im'][j, d], W['s5_log_step'][j, d].reshape(16, 1),
                        jnp.swapaxes(W['s5_b_re'][j], 1, 2), jnp.swapaxes(W['s5_b_im'][j], 1, 2)]
                r['s5in'].append(s5in)
                ar, ai, bbr, bbi = tile_fwd(f's5p_fwd{l}_{d}', f_s5p, (1,), [vec(t) for t in s5in],
                                            [((16, 64), F32, (16, 64), lambda i: (0, 0))] * 2
                                            + [((16, 16, 64), F32, (16, 16, 64), lambda i: (0, 0, 0))] * 2)
                coef = jnp.stack([ar.reshape(8, 128), ai.reshape(8, 128)])
                bcat = jnp.concatenate([_blockdiag(bbr), _blockdiag(bbi)], axis=1).astype(BF16)
                ccat = jnp.concatenate([_blockdiag_t(W['s5_c_re'][j, d]), -_blockdiag_t(W['s5_c_im'][j, d])], axis=0).astype(BF16)
                xcat = mm(f's5_in{l}_{d}', [(u2, bcat, 0)])
                h5, hp5 = clinrec(f's5_fwd{l}_{d}', xcat.reshape(B, Tt, 2 * D), coef, 'F' if d == 0 else 'R', Lc)
                hcat = h5.reshape(M, 2 * D)
                yd = mm(f's5_out{l}_{d}', [(hcat, ccat, 0)]).reshape(B, Tt, S5_W)
                r['coef'].append(coef)
                r['bcat'].append(bcat)
                r['ccat'].append(ccat)
                r['hp5'].append(hp5)
                r['hcat'].append(hcat)
                r['yd'].append(yd)
            hn = jnp.zeros((8, 128), F32).at[:HG_HEADS].set(W['hg_norm_g'][j])
            sd, gw, gb = W['s5_d'][j][None], W['s5_glu_w'][j], W['s5_glu_b'][j][None]
            r['fin_par'] = (hn, sd, gw, gb)
            mix = tile_fwd(f'oddfin_fwd{l}', f_oddfin, (B, nt),
                           [tok(r['o'][0]), tok(r['o'][1]), tok(proj, HG_W, 4), tok(r['yd'][0]), tok(r['yd'][1]),
                            tok(proj, S5_W, 15), vec(hn), vec(sd), vec(gw), vec(gb)], [tok_out(D, BF16)])[0]
            wout = W['od_w_out'][j]
        r['mix'] = mix
        o1 = mm(f'mix_out{l}', [(mix.reshape(M, -1), wout, 0)]).reshape(B, Tt, D)
        r['o1'] = o1
        fg = W['norm_ffn_g'][l][None]
        xb, h2 = tile_fwd(f'nm_ffn_fwd{l}', f_nm, (B, nt), [tok(xa), tok(o1), modv(l, 2), vec(fg), modv(l, 3), modv(l, 4)],
                          [tok_out(D), tok_out(D, BF16)])
        r['h2'] = h2
        h2m = h2.reshape(M, D)
        a = mm(f'ffn_gate{l}', [(h2m, W['ffn_w_gate'][l], 0)]).reshape(B, Tt, D_FF)
        up = mm(f'ffn_up{l}', [(h2m, W['ffn_w_up'][l], 0)]).reshape(B, Tt, D_FF)
        w9 = W['ffn_conv_w'][l].reshape(9, D_FF)
        cbias = W['ffn_conv_b'][l][None]
        r['a'], r['up'], r['w9'], r['cbias'] = a, up, w9, cbias
        act = tile_fwd(f'ffnconv_fwd{l}', f_ffnconv, (D_FF // 128, B),
                       [(a, (None, Tt, 128), lambda cb, bi: (bi, 0, cb)), (up, (None, Tt, 128), lambda cb, bi: (bi, 0, cb)),
                        (w9, (9, 128), lambda cb, bi: (0, cb)), (cbias, (1, 128), lambda cb, bi: (0, cb)), vec(conv_mk)],
                       [((B, Tt, D_FF), BF16, (None, Tt, 128), lambda cb, bi: (bi, 0, cb))])[0]
        r['act'] = act
        o2 = mm(f'ffn_down{l}', [(act.reshape(M, D_FF), W['ffn_w_down'][l], 0)]).reshape(B, Tt, D)
        xprev, oprev = xb, o2

    vmask = jnp.ones((nt, 1, D), F32).at[0].set(0.0)
    ones = jnp.ones((1, D), F32)
    fng = W['final_norm_g'][None]
    d_xp, d_o2, dg5, dfng, loss_vec = tile_bwd(
        'loss_head', f_final, (B, nt),
        [tok(xprev), tok(oprev), modv(DEPTH - 1, 5), vec(fng),
         (target, (None, tb, D), lambda b, j: (b, jnp.maximum(j - 1, 0), 0)), (vmask, (None, 1, D), lambda b, j: (j, 0, 0))],
        [[vec(ones)]],
        [(0,) + tok_out(D) + (None,), (1,) + tok_out(D, BF16) + (None,), (2,) + dmod_spec, (3,) + vec_acc((1, D))],
        prims=[(0,) + vec_acc((1, D))])
    loss = jnp.sum(loss_vec)
    add_grad('final_norm_g', None, dfng[0])
    dmods[(DEPTH - 1, 5)] = dg5

    for l in reversed(range(DEPTH)):
        r = R[l]
        j = l // 2
        d_o2m = d_o2.reshape(M, D)
        d_act = mm(f'ffn_down_dx{l}', [(d_o2m, W['ffn_w_down'][l], 0)], tb=True).reshape(B, Tt, D_FF)
        add_grad('ffn_w_down', l, mm(f'ffn_down_dw{l}', [(r['act'].reshape(M, D_FF), d_o2m, 0)], ta=True))
        d_a, d_up, dw9, dcb = tile_bwd(
            f'ffnconv_bwd{l}', f_ffnconv, (D_FF // 128, B),
            [(r['a'], (None, Tt, 128), lambda cb, bi: (bi, 0, cb)), (r['up'], (None, Tt, 128), lambda cb, bi: (bi, 0, cb)),
             (r['w9'], (9, 128), lambda cb, bi: (0, cb)), (r['cbias'], (1, 128), lambda cb, bi: (0, cb)), vec(conv_mk)],
            [[(d_act, (None, Tt, 128), lambda cb, bi: (bi, 0, cb))]],
            [(0, (B, Tt, D_FF), BF16, (None, Tt, 128), lambda cb, bi: (bi, 0, cb), None),
             (1, (B, Tt, D_FF), BF16, (None, Tt, 128), lambda cb, bi: (bi, 0, cb), None),
             (2, (9, D_FF), F32, (9, 128), lambda cb, bi: (0, cb), lambda cb, bi: bi == 0),
             (3, (1, D_FF), F32, (1, 128), lambda cb, bi: (0, cb), lambda cb, bi: bi == 0)])
        add_grad('ffn_conv_w', l, dw9.reshape(3, 3, D_FF))
        add_grad('ffn_conv_b', l, dcb[0])
        d_am, d_upm = d_a.reshape(M, D_FF), d_up.reshape(M, D_FF)
        h2m = r['h2'].reshape(M, D)
        d_h2 = mm(f'ffn_in_dx{l}', [(d_am, W['ffn_w_gate'][l], 0), (d_upm, W['ffn_w_up'][l], 0)], tb=True).reshape(B, Tt, D)
        add_grad('ffn_w_gate', l, mm(f'ffn_gate_dw{l}', [(h2m, d_am, 0)], ta=True))
        add_grad('ffn_w_up', l, mm(f'ffn_up_dw{l}', [(h2m, d_upm, 0)], ta=True))
        fg = W['norm_ffn_g'][l][None]
        d_xa, d_o1, dgate, dfg, dsh, dsc = tile_bwd(
            f'nm_ffn_bwd{l}', f_nm, (B, nt), [tok(r['xa']), tok(r['o1']), modv(l, 2), vec(fg), modv(l, 3), modv(l, 4)],
            [[tok(d_xp)], [tok(d_h2)]],
            [(0,) + tok_out(D) + (None,), (1,) + tok_out(D, BF16) + (None,), (2,) + dmod_spec, (3,) + vec_acc((1, D)),
             (4,) + dmod_spec, (5,) + dmod_spec])
        add_grad('norm_ffn_g', l, dfg[0])
        dmods[(l, 2)], dmods[(l, 3)], dmods[(l, 4)] = dgate, dsh, dsc
        d_o1m = d_o1.reshape(M, D)
        h1m = r['h1'].reshape(M, D)
        proj = r['proj']
        if l % 2 == 0:
            wout, win = W['ev_w_out'][j], W['ev_w_in_p'][j]
            d_mix = mm(f'mix_out_dx{l}', [(d_o1m, wout, 0)], tb=True).reshape(B, Tt, 2 * D)
            add_grad('ev_w_out', j, mm(f'mix_out_dw{l}', [(r['mix'].reshape(M, 2 * D), d_o1m, 0)], ta=True))
            d_y, d_xs_fin, d_z, d_gy, d_h, ddpad, dsng = tile_bwd(
                f'ssdfin_bwd{l}', f_ssdfin, (B, nt),
                [tok(r['y'][0]), tok(r['y'][1]), tok(r['xs']), tok(proj, D, 0), tok(proj, D, 1), tok(r['h'][0]), tok(r['h'][1]),
                 vec(r['dpad']), vec(r['sng'])],
                [[tok(d_mix)]],
                [(0,) + tok_out(D) + (None,), (2,) + tok_out(D) + (None,), (3,) + tok_out(D) + (None,), (4,) + tok_out(D) + (None,),
                 (5,) + tok_out(D) + (None,), (7,) + vec_acc((8, 128)), (8,) + vec_acc((1, D))])
            add_grad('ssd_d', j, ddpad[0, :SSD_HEADS])
            add_grad('ssd_norm_g', j, dsng[0])
            d_xs_parts, d_bc_parts, d_dt_parts, d_u_parts = [d_xs_fin], [], [], []
            dbias_t, dalog_t = [], []
            dh4 = d_h
            for d in range(2):
                ph0 = phys_chunk(nc, ncc, d == 1)

                def ph(s, ph0=ph0):
                    return ph0(nc - 1 - s)
                dxs_d, dbc_d, ddt_d, dbias, dalog = scan_bwd(
                    f'ssd_bwd{l}_{d}', functools.partial(f_ssd, d=d, reverse=(d == 1)), (B, nc),
                    [(r['xs'], (None, SSD_CHUNK, D), lambda b, s, ph=ph: (b, ph(s), 0)),
                     (r['bc'], (None, SSD_CHUNK, 512), lambda b, s, ph=ph: (b, ph(s), 0)),
                     (proj, (None, SSD_CHUNK, 128), lambda b, s, ph=ph: (b, ph(s), 36)),
                     vec(r['bias']), vec(r['alog'])],
                    (r['st'][d], (None, None, 8, 128, 128), lambda b, s: (b, nc - 1 - s, 0, 0, 0)),
                    (d_y, (None, SSD_CHUNK, D), lambda b, s, ph=ph: (b, ph(s), 0)),
                    [(0, (B, Tt, D), F32, (None, SSD_CHUNK, D), lambda b, s, ph=ph: (b, ph(s), 0), None),
                     (1, (B, Tt, 512), F32, (None, SSD_CHUNK, 512), lambda b, s, ph=ph: (b, ph(s), 0), None),
                     (2, (B, Tt, 128), F32, (None, SSD_CHUNK, 128), lambda b, s, ph=ph: (b, ph(s), 0), None),
                     (3,) + vec_acc((1, 128)), (4,) + vec_acc((1, 128))],
                    (8, 128, 128), lambda b, s: s == 0)
                d_xs_parts.append(dxs_d)
                d_bc_parts.append(dbc_d)
                d_dt_parts.append(ddt_d)
                dbias_t.append(dbias)
                dalog_t.append(dalog)
                g4, ga4 = linrec_bwd(f'lru_bwd{l}_{d}', r['a4'][d], dh4, r['hp4'][d], 'Fb' if d == 0 else 'Rb')
                du_g, dwa, dba, dwi, dbi, dlam = tile_bwd(
                    f'gates_bwd{l}_{d}', f_gates, (B, nt),
                    [tok(r['u']), vec(W['lru_w_a'][j, d]), vec(W['lru_b_a'][j, d][None]), vec(W['lru_w_i'][j, d]),
                     vec(W['lru_b_i'][j, d][None]), vec(W['lru_lam'][j, d][None])],
                    [[tok(ga4)], [tok(g4)]],
                    [(0,) + tok_out(D) + (None,), (1,) + vec_acc((8, 128, 128)), (2,) + vec_acc((1, D)), (3,) + vec_acc((8, 128, 128)),
                     (4,) + vec_acc((1, D)), (5,) + vec_acc((1, D))])
                d_u_parts.append(du_g)
                add_grad('lru_w_a', (j, d), dwa)
                add_grad('lru_b_a', (j, d), dba[0])
                add_grad('lru_w_i', (j, d), dwi)
                add_grad('lru_b_i', (j, d), dbi[0])
                add_grad('lru_lam', (j, d), dlam[0])
            add_grad('ssd_dt_bias', j, (dbias_t[0] + dbias_t[1])[0, :32].reshape(2, SSD_HEADS))
            add_grad('ssd_a_log', j, (dalog_t[0] + dalog_t[1])[0, :32].reshape(2, SSD_HEADS))
            scw, scb = W['ssd_conv_w'][j], W['ssd_conv_b'][j][None]
            lcw, lcb = W['lru_conv_w'][j], W['lru_conv_b'][j][None]

            def conv_bwd(name, colblk0, w, b, wblk0, ncols, act, parts):
                return tile_bwd(
                    name, functools.partial(f_conv1d, lc=Lc, act=act), (ncols // 256, B),
                    [(proj, (None, Tt, 256), lambda cb, bi: (bi, 0, colblk0 + cb)),
                     (w, (4, 256), lambda cb, bi: (0, wblk0 + cb)), (b, (1, 256), lambda cb, bi: (0, wblk0 + cb))],
                    [[(p, (None, Tt, 256), lambda cb, bi: (bi, 0, cb)) for p in parts]],
                    [(0, (B, Tt, ncols), F32, (None, Tt, 256), lambda cb, bi: (bi, 0, cb), None),
                     (1, (4, ncols), F32, (4, 256), lambda cb, bi: (0, cb), lambda cb, bi: bi == 0),
                     (2, (1, ncols), F32, (1, 256), lambda cb, bi: (0, cb), lambda cb, bi: bi == 0)])
            d_xs_raw, dw_xs, db_xs = conv_bwd(f'conv_xs_bwd{l}', 12, scw, scb, 0, 1024, True, d_xs_parts)
            d_bc_raw, dw_bc, db_bc = conv_bwd(f'conv_bc_bwd{l}', 16, scw, scb, 4, 512, True, d_bc_parts)
            d_u_raw, dw_u, db_u = conv_bwd(f'conv_u_bwd{l}', 8, lcw, lcb, 0, 1024, False, d_u_parts)
            add_grad('ssd_conv_w', j, jnp.concatenate([dw_xs, dw_bc], axis=1))
            add_grad('ssd_conv_b', j, jnp.concatenate([db_xs, db_bc], axis=1)[0])
            add_grad('lru_conv_w', j, dw_u)
            add_grad('lru_conv_b', j, db_u[0])
            def f_ev_dproj(z_, gy_, u_, xs_, bc_, t0, t1):
                pad = jnp.zeros((z_.shape[0], EV_PAD - 4736), F32)
                return jnp.concatenate([z_, gy_, u_, xs_, bc_, t0 + t1, pad], axis=1)
            dproj = tile_fwd(f'ev_dproj{l}', f_ev_dproj, (B, nt),
                             [tok(d_z), tok(d_gy), tok(d_u_raw), tok(d_xs_raw), tok(d_bc_raw), tok(d_dt_parts[0]), tok(d_dt_parts[1])],
                             [tok_out(EV_PAD, BF16)])[0].reshape(M, EV_PAD)
            d_h1 = mm(f'ev_proj_dx{l}', [(dproj, win, 0)], tb=True).reshape(B, Tt, D)
            dwp = mm(f'ev_proj_dw{l}', [(h1m, dproj, 0)], ta=True)
            add_grad('ev_w_in', j, jnp.concatenate([dwp[:, 0:1024], dwp[:, 3072:4640], dwp[:, 1024:3072]], axis=1))
        else:
            wout, win = W['od_w_out'][j], W['od_w_in'][j]
            d_mix = mm(f'mix_out_dx{l}', [(d_o1m, wout, 0)], tb=True).reshape(B, Tt, D)
            add_grad('od_w_out', j, mm(f'mix_out_dw{l}', [(r['mix'].reshape(M, D), d_o1m, 0)], ta=True))
            hn, sd, gw, gb = r['fin_par']
            d_o, d_g, d_yv, d_u_fin, dhn, dsd, dgw, dgb = tile_bwd(
                f'oddfin_bwd{l}', f_oddfin, (B, nt),
                [tok(r['o'][0]), tok(r['o'][1]), tok(proj, HG_W, 4), tok(r['yd'][0]), tok(r['yd'][1]), tok(proj, S5_W, 15),
                 vec(hn), vec(sd), vec(gw), vec(gb)],
                [[tok(d_mix)]],
                [(0,) + tok_out(HG_W) + (None,), (2,) + tok_out(HG_W) + (None,), (3,) + tok_out(S5_W) + (None,),
                 (5,) + tok_out(S5_W) + (None,), (6,) + vec_acc((8, 128)), (7,) + vec_acc((1, S5_W)), (8,) + vec_acc((S5_W, S5_W)),
                 (9,) + vec_acc((1, S5_W))])
            add_grad('hg_norm_g', j, dhn[:HG_HEADS])
            add_grad('s5_d', j, dsd[0])
            add_grad('s5_glu_w', j, dgw)
            add_grad('s5_glu_b', j, dgb[0])
            lbv = lb1 if l == 1 else lb3
            ns = nc
            dq, df, dv, du_s5 = [], [], [], []
            d_ym = d_yv.reshape(M, S5_W)
            dbt_re, dbt_im = [], []
            for d in range(2):
                ph0 = phys_chunk(ns, ncc, d == 1)

                def ph(s, ph0=ph0):
                    return ph0(ns - 1 - s)
                dq_d, df_d, dv_d, dlb_d = scan_bwd(
                    f'hgrn_bwd{l}_{d}', functools.partial(f_hgrn_group, reverse=(d == 1), chunk=HG_BWD[0]), (bgr, B, ns),
                    [(proj, (None, 128, bgw), lambda h, b, s, ph=ph: (b, ph(s), h)),
                     (proj, (None, 128, bgw), lambda h, b, s, ph=ph, d=d: (b, ph(s), (1 + d) * bgr + h)),
                     (proj, (None, 128, bgw), lambda h, b, s, ph=ph: (b, ph(s), 3 * bgr + h)),
                     (lbv, (1, bgw), lambda h, b, s: (0, h))],
                    (r['zst'][d].reshape(B, ns, bgr, bgw, 128), (None, None, None, bgw, 128), lambda h, b, s: (b, ns - 1 - s, h, 0, 0)),
                    (d_o, (None, 128, bgw), lambda h, b, s, ph=ph: (b, ph(s), h)),
                    [(0, (B, Tt, HG_W), F32, (None, 128, bgw), lambda h, b, s, ph=ph: (b, ph(s), h), None),
                     (1, (B, Tt, HG_W), F32, (None, 128, bgw), lambda h, b, s, ph=ph: (b, ph(s), h), None),
                     (2, (B, Tt, HG_W), F32, (None, 128, bgw), lambda h, b, s, ph=ph: (b, ph(s), h), None),
                     (3, (1, HG_W), F32, (1, bgw), lambda h, b, s: (0, h), lambda h, b, s: (b == 0) & (s == 0))],
                    (bgw, 128), lambda h, b, s: s == 0)
                dq.append(dq_d)
                df.append(df_d)
                dv.append(dv_d)
                dlb[l].append(dlb_d)
                d_hcat = mm(f's5_out_dx{l}_{d}', [(d_ym, r['ccat'][d], 0)], tb=True)
                dccat = mm(f's5_out_dw{l}_{d}', [(r['hcat'][d], d_ym, 0)], ta=True)
                add_grad('s5_c_re', (j, d), _unblockdiag_t(dccat[:D]))
                add_grad('s5_c_im', (j, d), -_unblockdiag_t(dccat[D:]))
                g5, dcoef8 = clinrec(f's5_bwd{l}_{d}', d_hcat.reshape(B, Tt, 2 * D), r['coef'][d], 'Fb' if d == 0 else 'Rb',
                                     Lc, conj=True, hprev=r['hp5'][d])
                dcoef = jnp.sum(dcoef8, axis=2)
                gcat = g5.reshape(M, 2 * D)
                dbcat = mm(f's5_in_dw{l}_{d}', [(r['u2'], gcat, 0)], ta=True)
                du_s5.append(mm(f's5_in_dx{l}_{d}', [(gcat, r['bcat'][d], 0)], tb=True))
                cts5 = [dcoef[0].reshape(16, 64), dcoef[1].reshape(16, 64), _unblockdiag(dbcat[:, :D]), _unblockdiag(dbcat[:, D:])]
                dlre, dlim, dlst, dbtr, dbti = tile_bwd(
                    f's5p_bwd{l}_{d}', f_s5p, (1,), [vec(t) for t in r['s5in'][d]], [[vec(t)] for t in cts5],
                    [(i, t.shape, F32, t.shape, (lambda *ids, n=t.ndim: (0,) * n), None) for i, t in enumerate(r['s5in'][d])])
                add_grad('s5_lam_re', (j, d), dlre)
                add_grad('s5_lam_im', (j, d), dlim)
                add_grad('s5_log_step', (j, d), dlst[:, 0])
                dbt_re.append(dbtr)
                dbt_im.append(dbti)
            add_grad('s5_b_re', j, jnp.swapaxes(dbt_re[0] + dbt_re[1], 1, 2))
            add_grad('s5_b_im', j, jnp.swapaxes(dbt_im[0] + dbt_im[1], 1, 2))
            def f_od_dproj(q0, q1, f0, f1, v0, v1, g_, u0, u1, u2):
                return jnp.concatenate([q0 + q1, f0, f1, v0 + v1, g_, u0 + u1 + u2], axis=1)
            parts = [dq[0], dq[1], df[0], df[1], dv[0], dv[1], d_g, d_u_fin, du_s5[0].reshape(B, Tt, S5_W),
                     du_s5[1].reshape(B, Tt, S5_W)]
            dproj = tile_fwd(f'od_dproj{l}', f_od_dproj, (B, nt), [tok(t) for t in parts],
                             [tok_out(4096, BF16)])[0].reshape(M, 4096)
            d_h1 = mm(f'od_proj_dx{l}', [(dproj, win, 0)], tb=True).reshape(B, Tt, D)
            add_grad('od_w_in', j, mm(f'od_proj_dw{l}', [(h1m, dproj, 0)], ta=True))
        ng = W['norm_mix_g'][l][None]
        if l == 0:
            d_x0, dng, dsh, dsc = tile_bwd(
                'nm0_bwd', lambda xv, g, sh, sc: (xv, f_nm0(xv, g, sh, sc)), (B, nt),
                [tok(r['xin']), vec(ng), modv(l, 0), modv(l, 1)], [[tok(d_xa)], [tok(d_h1)]],
                [(0,) + tok_out(D) + (None,), (1,) + vec_acc((1, D)), (2,) + dmod_spec, (3,) + dmod_spec])
        else:
            d_xp, d_o2, dgate, dng, dsh, dsc = tile_bwd(
                f'nm_mix_bwd{l}', f_nm, (B, nt),
                [tok(r['xin']), tok(r['oin']), modv(l - 1, 5), vec(ng), modv(l, 0), modv(l, 1)],
                [[tok(d_xa)], [tok(d_h1)]],
                [(0,) + tok_out(D) + (None,), (1,) + tok_out(D, BF16) + (None,), (2,) + dmod_spec, (3,) + vec_acc((1, D)),
                 (4,) + dmod_spec, (5,) + dmod_spec])
            dmods[(l - 1, 5)] = dgate
        add_grad('norm_mix_g', l, dng[0])
        dmods[(l, 0)], dmods[(l, 1)] = dsh, dsc

    grad_x = d_x0[:, Lc:, :]

    (dlogits,) = tile_bwd('lb_bwd', f_lb, (1,), [vec(W['hg_lb_logits'])],
                          [[vec(t) for t in dlb[1]], [vec(t) for t in dlb[3]]],
                          [(0, (DEPTH, HG_W), F32, (DEPTH, HG_W), lambda i: (0, 0), None)])
    add_grad('hg_lb_logits', None, dlogits)

    dm = jnp.stack([jnp.stack([dmods[(l, w)] for w in range(N_MOD)]) for l in range(DEPTH)])
    dlat = jnp.transpose(dm[:, :, :, 1, 0, :], (0, 2, 1, 3)).reshape(DEPTH, B, N_MOD * D)
    dctx = jnp.transpose(dm[:, :, :, 0, 0, :], (0, 2, 1, 3)).reshape(DEPTH, B, N_MOD * D)
    dlat = jnp.zeros((DEPTH, 8, N_MOD * D), F32).at[:, :B].set(dlat)
    dctx = jnp.zeros((DEPTH, 8, N_MOD * D), F32).at[:, :B].set(dctx)

    def f_dmod(dl, dc):
        row = lax.broadcasted_iota(jnp.int32, (8, 1), 0)
        dall = dl + jnp.where(row == 4, jnp.sum(dc, axis=0, keepdims=True), 0.0)
        return dall, jnp.sum(dall, axis=0, keepdims=True)

    dmod, dbmod = tile_fwd('mod_bwd', f_dmod, (DEPTH, nmc),
                           [(dlat, (None, 8, 1536), lambda l, n: (l, 0, n)), (dctx, (None, 8, 1536), lambda l, n: (l, 0, n))],
                           [((DEPTH, 8, N_MOD * D), F32, (None, 8, 1536), lambda l, n: (l, 0, n)),
                            ((DEPTH, 1, N_MOD * D), F32, (None, 1, 1536), lambda l, n: (l, 0, n))])
    add_grad('b_mod', None, dbmod[:, 0])
    return loss, grad_x, G, dmod


def mod_weight_grad(name, cc_rows, dmod_rows):
    nl, nr, cols = dmod_rows.shape

    def f(ccv, dv):
        return lax.dot_general(jax.nn.silu(ccv).astype(BF16), dv.astype(BF16), (((0,), (0,)), ((), ())),
                               preferred_element_type=F32)

    return tile_fwd(name, f, (nl,), [(cc_rows, (nr, D), lambda l: (0, 0)), (dmod_rows, (None, nr, cols), lambda l: (l, 0, 0))],
                    [((nl, D, cols), F32, (None, D, cols), lambda l: (l, 0, 0))])[0]


def assemble_grads(G, like):
    out = {}
    for name, parts in G.items():
        shape = like[name].shape
        if None in parts:
            g = parts[None]
        elif isinstance(next(iter(parts)), tuple):
            g = jnp.stack([jnp.stack([parts[(j, d)] for d in range(2)]) for j in range(shape[0])])
        else:
            g = jnp.stack([parts[i] for i in range(shape[0])])
        out[name] = g.reshape(shape)
    return out


XY_RELS = ((1, 0, 0), (0, 1, 0), (1, 1, 0))
ALL_RELS = tuple((dx, dy, dc) for dx in (0, 1) for dy in (0, 1) for dc in (0, 1))[1:]


def exchange(name, src, out_shape, sends):
    return exchange_many(name, [(src, out_shape, sends)])[0]


def exchange_many(name, items):
    na = len(items)
    n = sum(len(sends) for _, _, sends in items)

    def body(*refs):
        src_refs, out_refs, send_sems, recv_sems = refs[:na], refs[na:2 * na], refs[2 * na], refs[2 * na + 1]
        me = (lax.axis_index("x"), lax.axis_index("y"), lax.axis_index("c"))
        copies, k = [], 0
        for (_, _, sends), src_ref, out_ref in zip(items, src_refs, out_refs):
            for rel, ssel, dsel in sends:
                tgt = tuple(1 - m if f else m for m, f in zip(me, rel))
                cp = pltpu.make_async_remote_copy(
                    src_ref=src_ref if ssel is None else src_ref.at[ssel(me, tgt)],
                    dst_ref=out_ref if dsel is None else out_ref.at[dsel(me, tgt)],
                    send_sem=send_sems.at[k], recv_sem=recv_sems.at[k], device_id=tgt, device_id_type=MESH)
                cp.start()
                copies.append(cp)
                k += 1
        for cp in copies:
            cp.wait()

    return pl.pallas_call(
        body, out_shape=[jax.ShapeDtypeStruct(shape, src.dtype) for src, shape, _ in items],
        in_specs=[pl.BlockSpec(memory_space=pl.ANY)] * na, out_specs=[pl.BlockSpec(memory_space=pl.ANY)] * na,
        scratch_shapes=[pltpu.SemaphoreType.DMA((n,)), pltpu.SemaphoreType.DMA((n,))],
        name=name,
    )(*[src for src, _, _ in items])


def _xy_index(dev):
    return 2 * dev[0] + dev[1]


def _my_xy():
    return 2 * lax.axis_index("x") + lax.axis_index("y")


def all_gather_xy(name, shard):
    got = exchange(name, shard, (4,) + shard.shape, [(rel, None, lambda me, tgt: _xy_index(me)) for rel in XY_RELS])
    return lax.dynamic_update_index_in_dim(got, shard, _my_xy(), 0)


def reduce_scatter_xy(name, g4):
    got = exchange(name, g4, (3,) + g4.shape[1:],
                   [(rel, (lambda me, tgt: _xy_index(tgt)), (lambda me, tgt, k=k: k)) for k, rel in enumerate(XY_RELS)])
    return got, lax.dynamic_index_in_dim(g4, _my_xy(), 0, keepdims=False)


def sibling_swap(name, v):
    return exchange(name, v, v.shape, [((0, 0, 1), None, None)])


def all_gather_all(name, v):
    got = exchange(name, v, (8,) + v.shape, [(rel, None, lambda me, tgt: 4 * me[0] + 2 * me[1] + me[2]) for rel in ALL_RELS])
    return lax.dynamic_update_index_in_dim(got, v, 2 * _my_xy() + lax.axis_index("c"), 0)


def all_gather_xy_halves(name, shards):
    na = len(shards)

    def body(*refs):
        src_refs, out_refs, send_sems, recv_sems = refs[:na], refs[na:2 * na], refs[2 * na], refs[2 * na + 1]
        x, y, c = lax.axis_index("x"), lax.axis_index("y"), lax.axis_index("c")
        peers = [(1 - x, y), (x, 1 - y), (1 - x, 1 - y)]

        def copy(k, src, dst, to):
            return pltpu.make_async_remote_copy(src_ref=src, dst_ref=dst, send_sem=send_sems.at[k], recv_sem=recv_sems.at[k],
                                                device_id=to, device_id_type=MESH)

        halves = [pl.ds(c * (s.shape[0] // 2), s.shape[0] // 2) for s in shards]
        first = [[copy(6 * i + k, src_refs[i].at[halves[i]], out_refs[i].at[2 * x + y, halves[i]], (px, py, c))
                  for k, (px, py) in enumerate(peers)] for i in range(na)]
        for row in first:
            for cp in row:
                cp.start()
        passed = []
        for i in range(na):
            for k, (px, py) in enumerate(peers):
                first[i][k].wait_recv()
                landed = out_refs[i].at[2 * px + py, halves[i]]
                fw = copy(6 * i + 3 + k, landed, landed, (x, y, 1 - c))
                fw.start()
                passed.append(fw)
        for fw in passed:
            fw.wait_recv()
        for cp in [cp for row in first for cp in row] + passed:
            cp.wait_send()

    got = pl.pallas_call(
        body, out_shape=[jax.ShapeDtypeStruct((4,) + s.shape, s.dtype) for s in shards],
        in_specs=[pl.BlockSpec(memory_space=pl.ANY)] * na, out_specs=[pl.BlockSpec(memory_space=pl.ANY)] * na,
        scratch_shapes=[pltpu.SemaphoreType.DMA((6 * na,)), pltpu.SemaphoreType.DMA((6 * na,))],
        name=name,
    )(*shards)
    return [lax.dynamic_update_index_in_dim(g, s, _my_xy(), 0) for g, s in zip(got, shards)]


def reduce_scatter_xy_many(name, g4s):
    got = exchange_many(name, [(g4, (3,) + g4.shape[1:],
                                [(rel, (lambda me, tgt: _xy_index(tgt)), (lambda me, tgt, k=k: k)) for k, rel in enumerate(XY_RELS)])
                               for g4 in g4s])
    return [(g, lax.dynamic_index_in_dim(g4, _my_xy(), 0, keepdims=False)) for g, g4 in zip(got, g4s)]


def sibling_split(name, g4s):
    halves = [g4.shape[1] // 2 for g4 in g4s]
    got = exchange_many(name, [(g4, (4, h) + g4.shape[2:], [((0, 0, 1), (lambda me, tgt, h=h: (slice(None), pl.ds(tgt[2] * h, h))), None)])
                               for g4, h in zip(g4s, halves)])
    return [(g, lax.dynamic_slice_in_dim(g4, lax.axis_index("c") * h, h, axis=1)) for g, g4, h in zip(got, g4s, halves)]


def sibling_join(name, qs):
    got = exchange_many(name, [(q, (2 * q.shape[0],) + q.shape[1:], [((0, 0, 1), None, lambda me, tgt, h=q.shape[0]: pl.ds(me[2] * h, h))])
                               for q in qs])
    return [lax.dynamic_update_slice_in_dim(g, q, lax.axis_index("c") * q.shape[0], axis=0) for g, q in zip(got, qs)]


def _rows_view(shape):
    cols = shape[-1] if len(shape) else 1
    rows = 1
    for s in shape[:-1]:
        rows *= s
    return rows, cols


def _row_block(rows, cols, n_arrays):
    budget = (24 * 1024 * 1024) // (8 * n_arrays * cols)
    if rows <= max(budget, 16):
        return rows
    br = (min(budget, rows) // 16) * 16
    while br > 16 and rows % br:
        br -= 16
    return br if rows % br == 0 else rows


def sum_slots(name, stacked, extra=(), out_dtype=F32):
    k = stacked.shape[0]
    rows, cols = _rows_view(stacked.shape[1:])
    br = _row_block(rows, cols, k + len(extra) + 1)

    def f(s, *more):
        parts = [s[i].astype(F32) for i in range(k)] + [m.astype(F32) for m in more]
        while len(parts) > 1:
            parts = [parts[i] + parts[i + 1] for i in range(0, len(parts) - 1, 2)] + ([parts[-1]] if len(parts) % 2 else [])
        return parts[0]

    out = tile_fwd(name, f, (rows // br,),
                   [(stacked.reshape(k, rows, cols), (k, br, cols), lambda i: (0, i, 0))]
                   + [(e.reshape(rows, cols), (br, cols), lambda i: (i, 0)) for e in extra],
                   [((rows, cols), out_dtype, (br, cols), lambda i: (i, 0))])[0]
    return out.reshape(stacked.shape[1:])


def adamw(name, w, m, v, gs):
    rows, cols = _rows_view(w.shape)
    br = _row_block(rows, cols, 7 + len(gs))
    spec = lambda a: (a.reshape(rows, cols), (br, cols), lambda i: (i, 0))
    outs = tile_fwd(name, f_adamw, (rows // br,), [spec(t) for t in (w, m, v) + tuple(gs)],
                    [((rows, cols), F32, (br, cols), lambda i: (i, 0))] * 4)
    return [o.reshape(w.shape) for o in outs]


IN_NAMES = ['x', 'c', 'ctx'] + W_NAMES + ['loss_target'] + ['m_' + n for n in W_NAMES] + ['v_' + n for n in W_NAMES]
SMALL_PAD = 128 * 1024


def kernel(x, c, ctx, c_ctx, w_mod, b_mod, norm_mix_g, norm_ffn_g, final_norm_g, ev_w_in, ev_w_out, ssd_conv_w, ssd_conv_b, ssd_dt_bias, ssd_a_log, ssd_d, ssd_norm_g, lru_conv_w, lru_conv_b, lru_w_a, lru_b_a, lru_w_i, lru_b_i, lru_lam, od_w_in, od_w_out, hg_lb_logits, hg_norm_g, s5_lam_re, s5_lam_im, s5_log_step, s5_b_re, s5_b_im, s5_c_re, s5_c_im, s5_d, s5_glu_w, s5_glu_b, ffn_w_gate, ffn_w_up, ffn_conv_w, ffn_conv_b, ffn_w_down, loss_target, m_c_ctx, m_w_mod, m_b_mod, m_norm_mix_g, m_norm_ffn_g, m_final_norm_g, m_ev_w_in, m_ev_w_out, m_ssd_conv_w, m_ssd_conv_b, m_ssd_dt_bias, m_ssd_a_log, m_ssd_d, m_ssd_norm_g, m_lru_conv_w, m_lru_conv_b, m_lru_w_a, m_lru_b_a, m_lru_w_i, m_lru_b_i, m_lru_lam, m_od_w_in, m_od_w_out, m_hg_lb_logits, m_hg_norm_g, m_s5_lam_re, m_s5_lam_im, m_s5_log_step, m_s5_b_re, m_s5_b_im, m_s5_c_re, m_s5_c_im, m_s5_d, m_s5_glu_w, m_s5_glu_b, m_ffn_w_gate, m_ffn_w_up, m_ffn_conv_w, m_ffn_conv_b, m_ffn_w_down, v_c_ctx, v_w_mod, v_b_mod, v_norm_mix_g, v_norm_ffn_g, v_final_norm_g, v_ev_w_in, v_ev_w_out, v_ssd_conv_w, v_ssd_conv_b, v_ssd_dt_bias, v_ssd_a_log, v_ssd_d, v_ssd_norm_g, v_lru_conv_w, v_lru_conv_b, v_lru_w_a, v_lru_b_a, v_lru_w_i, v_lru_b_i, v_lru_lam, v_od_w_in, v_od_w_out, v_hg_lb_logits, v_hg_norm_g, v_s5_lam_re, v_s5_lam_im, v_s5_log_step, v_s5_b_re, v_s5_b_im, v_s5_c_re, v_s5_c_im, v_s5_d, v_s5_glu_w, v_s5_glu_b, v_ffn_w_gate, v_ffn_w_up, v_ffn_conv_w, v_ffn_conv_b, v_ffn_w_down):
    a = dict(locals())
    big = [n for n in W_NAMES if n in MATMUL_WEIGHTS and n != 'w_mod']
    minor = [n for n in W_NAMES if n in SHARD_AXIS and n not in MATMUL_WEIGHTS]

    def pack(arrays, lead=()):
        flat = jnp.concatenate([t.reshape(lead + (-1,)) for t in arrays], axis=len(lead))
        pad = -flat.shape[-1] % 1024
        flat = jnp.concatenate([flat, jnp.zeros(lead + (pad,), flat.dtype)], axis=len(lead))
        return flat.reshape(lead + (-1, 128))

    def unpack(packed, names, lead=()):
        flat, out, off = packed.reshape(lead + (-1,)), {}, 0
        for n in names:
            size = math.prod(a[n].shape)
            out[n] = flat[..., off:off + size].reshape(lead + a[n].shape)
            off += size
        return out

    gathered = dict(zip(big, all_gather_xy_halves('ag_big', [a[n].astype(BF16) for n in big])))
    gathered.update(unpack(all_gather_xy('ag_minor', pack([a[n] for n in minor])), minor, (4,)))
    W = {}
    for n in W_NAMES:
        w = a[n]
        if n in gathered:
            ax = SHARD_AXIS[n]
            shape = list(w.shape)
            shape[ax] *= 4
            W[n] = jnp.moveaxis(gathered[n], 0, ax).reshape(shape)
        elif n not in SHARD_AXIS:
            W[n] = w
    e = W['ev_w_in']
    W['ev_w_in_p'] = jnp.concatenate(
        [e[:, :, 0:1024], e[:, :, 2592:3616], e[:, :, 3616:4640], e[:, :, 1024:2560], e[:, :, 2560:2592],
         jnp.zeros((e.shape[0], D, EV_PAD - 4640), e.dtype)], axis=2)

    my_dev = 2 * _my_xy() + lax.axis_index("c")
    mcols = a['w_mod'].shape[2]
    w_mod_mine = a['w_mod'].astype(BF16)
    cc_all = all_gather_all('ag_c', cond_rows(a['c'], a['c_ctx'])).reshape(8 * 8, D)
    mod_cols = mod_table('mod_fwd', cc_all, w_mod_mine, lax.dynamic_slice_in_dim(a['b_mod'], _my_xy() * mcols, mcols, axis=1))
    mod_all = jnp.moveaxis(all_gather_xy('ag_mod', mod_cols), 0, 2).reshape(DEPTH, 8 * 8, N_MOD * D)
    mod = lax.dynamic_slice_in_dim(mod_all, my_dev * 8, 8, axis=1)

    loss_local, grad_x, G, dmod = local_step(a['x'], a['ctx'], a['loss_target'], mod, W)
    grads = assemble_grads(G, W)
    loss = lax.psum(loss_local, ("x", "y", "c"))

    res = {}
    dm_all = jnp.moveaxis(all_gather_all('ag_dmod', dmod), 0, 1).reshape(DEPTH, 8 * 8, N_MOD * D)
    dm_mine = lax.dynamic_slice_in_dim(dm_all, _my_xy() * mcols, mcols, axis=2)
    res['w_mod'] = adamw('adamw_w_mod', a['w_mod'], a['m_w_mod'], a['v_w_mod'], (mod_weight_grad('mod_dw', cc_all, dm_mine),))
    dcc = mod_cond_grad('mod_dcc', cc_all, dm_mine, w_mod_mine).reshape(8, 8, D)
    grads['c_ctx'] = 0.5 * jnp.sum(dcc[:, 4], axis=0)
    g4 = {}
    for n in big + minor:
        ax = SHARD_AXIS[n]
        gf = grads[n]
        g4[n] = jnp.moveaxis(gf.reshape(gf.shape[:ax] + (4, a[n].shape[ax]) + gf.shape[ax + 1:]), ax, 0)
    parts = [sum_slots('csum_' + n, theirs[None], (ours,), BF16)
             for n, (theirs, ours) in zip(big, sibling_split('rsc_big', [g4[n] for n in big]))]
    halves = [sum_slots('gsum_' + n, got, (own,)) for n, (got, own) in zip(big, reduce_scatter_xy_many('rs_big', parts))]
    for n, g in zip(big, sibling_join('agc_big', halves)):
        res[n] = adamw('adamw_' + n, a[n], a['m_' + n], a['v_' + n], (g,))
    got, own = reduce_scatter_xy('rs_minor', pack([g4[n] for n in minor], (4,)))
    mine = sum_slots('gsum_minor', got, (own,))
    mine_n, other_n = unpack(mine, minor), unpack(sibling_swap('sw_minor', mine), minor)
    for n in minor:
        res[n] = adamw('adamw_' + n, a[n], a['m_' + n], a['v_' + n], (mine_n[n], other_n[n]))
    small = [n for n in W_NAMES if n not in SHARD_AXIS]
    flat = jnp.concatenate([grads[n].reshape(-1) for n in small])
    total = flat.shape[0]
    padded = -(-total // SMALL_PAD) * SMALL_PAD
    flat = jnp.concatenate([flat, jnp.zeros((padded - total,), F32)]).reshape(padded // 128, 128)
    pair = sum_slots('csum_small', flat[None], (sibling_swap('sw_small', flat),))
    summed = sum_slots('gsum_small', all_gather_xy('ag_small', pair)).reshape(-1)
    off = 0
    for n in small:
        size = math.prod(a[n].shape)
        g = summed[off:off + size].reshape(a[n].shape)
        off += size
        res[n] = adamw('adamw_' + n, a[n], a['m_' + n], a['v_' + n], (g,))
    outs = [loss, grad_x]
    for k in range(4):
        outs += [res[n][k] for n in W_NAMES]
    return tuple(outs)
```

```python
import functools
import math

import jax
import jax.numpy as jnp
from jax import lax
from jax.experimental import pallas as pl
from jax.experimental.pallas import tpu as pltpu

F32 = jnp.float32
BF16 = jnp.bfloat16
HI = lax.Precision.HIGHEST
MESH = pl.DeviceIdType.MESH

D = 1024
DEPTH = 4
N_MOD = 6
RMS_EPS = 1e-6
GRID_W = 64
SSD_HEADS = 16
SSD_CHUNK = 128
HG_W = 768
HG_HEADS = 6
HG_FWD = (32, 6)
HG_BWD = (16, 3)
S5_W = 256
D_FF = 2816
EV_PAD = 5120
LRU_C = 8.0
V7X_VMEM_LIMIT = 56 * 1024 * 1024
MM_VMEM_BUDGET = 36 * 1024 * 1024

ADAM_LR, ADAM_B1, ADAM_B2, ADAM_EPS, ADAM_WD, ADAM_STEP = 0.001, 0.9, 0.999, 1e-08, 0.01, 10

W_NAMES = ['c_ctx', 'w_mod', 'b_mod', 'norm_mix_g', 'norm_ffn_g', 'final_norm_g', 'ev_w_in', 'ev_w_out', 'ssd_conv_w',
           'ssd_conv_b', 'ssd_dt_bias', 'ssd_a_log', 'ssd_d', 'ssd_norm_g', 'lru_conv_w', 'lru_conv_b', 'lru_w_a', 'lru_b_a',
           'lru_w_i', 'lru_b_i', 'lru_lam', 'od_w_in', 'od_w_out', 'hg_lb_logits', 'hg_norm_g', 's5_lam_re', 's5_lam_im',
           's5_log_step', 's5_b_re', 's5_b_im', 's5_c_re', 's5_c_im', 's5_d', 's5_glu_w', 's5_glu_b', 'ffn_w_gate', 'ffn_w_up',
           'ffn_conv_w', 'ffn_conv_b', 'ffn_w_down']
SHARD_AXIS = {'w_mod': 2, 'ev_w_in': 2, 'ev_w_out': 1, 'ssd_conv_w': 2, 'lru_conv_w': 2, 'lru_b_a': 2, 'lru_b_i': 2,
              'lru_lam': 2, 'od_w_in': 2, 'od_w_out': 1, 's5_d': 1, 's5_glu_w': 1, 's5_glu_b': 1, 'ffn_w_gate': 2,
              'ffn_w_up': 2, 'ffn_conv_w': 3, 'ffn_w_down': 1}
MATMUL_WEIGHTS = ('w_mod', 'ev_w_in', 'ev_w_out', 'od_w_in', 'od_w_out', 'ffn_w_gate', 'ffn_w_up', 'ffn_w_down')


def _cparams(sem=None):
    return pltpu.CompilerParams(vmem_limit_bytes=V7X_VMEM_LIMIT, dimension_semantics=sem)


def _pick(n, cands):
    for c in cands:
        if n % c == 0:
            return c
    return n


def tile_fwd(name, f, grid, ins, outs):
    n_in = len(ins)

    def body(*refs):
        res = f(*[r[...] for r in refs[:n_in]])
        if not isinstance(res, (tuple, list)):
            res = (res,)
        for r, o in zip(res, refs[n_in:]):
            o[...] = r.astype(o.dtype)

    res = pl.pallas_call(
        body, grid=grid,
        in_specs=[pl.BlockSpec(b, m) for _, b, m in ins],
        out_specs=[pl.BlockSpec(b, m) for _, _, b, m in outs],
        out_shape=[jax.ShapeDtypeStruct(s, d) for s, d, _, _ in outs],
        name=name, compiler_params=_cparams(("arbitrary",) * len(grid)),
    )(*[a for a, _, _ in ins])
    return res


def tile_bwd(name, f, grid, ins, cts, grads, prims=()):
    n_in = len(ins)
    ct_flat = [p for c in cts for p in c]
    n_ct = len(ct_flat)
    didx = [g[0] for g in grads]

    def body(*refs):
        in_refs, ct_refs = refs[:n_in], refs[n_in:n_in + n_ct]
        g_refs = refs[n_in + n_ct:n_in + n_ct + len(grads)]
        p_refs = refs[n_in + n_ct + len(grads):]
        vals = [r[...] for r in in_refs]

        def fd(*dv):
            full = list(vals)
            for i, v in zip(didx, dv):
                full[i] = v
            res = f(*full)
            return tuple(res) if isinstance(res, (tuple, list)) else (res,)

        out, vjp = jax.vjp(fd, *[vals[i] for i in didx])
        ctv, k = [], 0
        for o, c in zip(out, cts):
            acc = None
            for _ in c:
                piece = ct_refs[k][...].astype(o.dtype)
                acc = piece if acc is None else acc + piece
                k += 1
            ctv.append(jnp.zeros_like(o) if acc is None else acc.reshape(o.shape))
        gs = vjp(tuple(ctv))
        ids = [pl.program_id(a) for a in range(len(grid))]

        def emit(ref, val, first):
            if first is None:
                ref[...] = val.astype(ref.dtype)
            else:
                is_first = first(*ids)

                @pl.when(is_first)
                def _():
                    ref[...] = val.astype(ref.dtype)

                @pl.when(jnp.logical_not(is_first))
                def _():
                    ref[...] += val.astype(ref.dtype)

        for g, spec, ref in zip(gs, grads, g_refs):
            emit(ref, g, spec[5])
        for spec, ref in zip(prims, p_refs):
            emit(ref, out[spec[0]], spec[5])

    specs = list(grads) + list(prims)
    res = pl.pallas_call(
        body, grid=grid,
        in_specs=[pl.BlockSpec(b, m) for _, b, m in list(ins) + ct_flat],
        out_specs=[pl.BlockSpec(s[3], s[4]) for s in specs],
        out_shape=[jax.ShapeDtypeStruct(s[1], s[2]) for s in specs],
        name=name, compiler_params=_cparams(("arbitrary",) * len(grid)),
    )(*[a for a, _, _ in list(ins) + ct_flat])
    return res


def mm(name, pairs, ta=False, tb=False, out_dtype=F32):
    a0, b0, _ = pairs[0]
    m = a0.shape[1] if ta else a0.shape[0]
    n = b0.shape[0] if tb else b0.shape[1]
    cands = (1024, 1408, 768, 512, 256, 128)
    tks, nks = [], []
    for a, b, _ in pairs:
        k = a.shape[0] if ta else a.shape[1]
        tk = _pick(k, cands)
        tks.append(tk)
        nks.append(k // tk)

    def vmem_bytes(tm, tn):
        tiles = sum(2 * tk * (tm * a.dtype.itemsize + tn * b.dtype.itemsize) for (a, b, _), tk in zip(pairs, tks))
        return tiles + tm * tn * (4 + 2 * jnp.dtype(out_dtype).itemsize)

    tm_c = [c_ for c_ in cands if m % c_ == 0] or [m]
    tn_c = [c_ for c_ in cands if n % c_ == 0] or [n]
    tm, tn = tm_c[0], tn_c[0]
    while vmem_bytes(tm, tn) > MM_VMEM_BUDGET and (len(tm_c) > 1 or len(tn_c) > 1):
        if len(tm_c) > 1 and (tm >= tn or len(tn_c) == 1):
            tm_c = tm_c[1:]
        else:
            tn_c = tn_c[1:]
        tm, tn = tm_c[0], tn_c[0]
    starts = [sum(nks[:p]) for p in range(len(pairs))]
    nk = sum(nks)
    np_ = len(pairs)

    def body(*refs):
        o_ref, acc = refs[2 * np_], refs[2 * np_ + 1]
        kk = pl.program_id(2)

        @pl.when(kk == 0)
        def _():
            acc[...] = jnp.zeros_like(acc)

        for p in range(np_):
            def add(p=p):
                a = refs[2 * p][...].astype(BF16)
                b = refs[2 * p + 1][...].astype(BF16)
                dn = (((0 if ta else 1,), (1 if tb else 0,)), ((), ()))
                acc[...] += lax.dot_general(a, b, dn, preferred_element_type=F32)
            if np_ == 1:
                add()
            else:
                pl.when((kk >= starts[p]) & (kk < starts[p] + nks[p]))(add)

        @pl.when(kk == nk - 1)
        def _():
            o_ref[...] = acc[...].astype(o_ref.dtype)

    in_specs, args = [], []
    for p, (a, b, off) in enumerate(pairs):
        tk, s0, nkp = tks[p], starts[p], nks[p]
        assert off % tk == 0
        boff = off // tk

        def kloc(k, s0=s0, nkp=nkp):
            return jnp.clip(k - s0, 0, nkp - 1)
        if ta:
            in_specs.append(pl.BlockSpec((tk, tm), lambda i, j, k, kloc=kloc: (kloc(k), i)))
        else:
            in_specs.append(pl.BlockSpec((tm, tk), lambda i, j, k, kloc=kloc: (i, kloc(k))))
        if tb:
            in_specs.append(pl.BlockSpec((tn, tk), lambda i, j, k, kloc=kloc, boff=boff: (j, boff + kloc(k))))
        else:
            in_specs.append(pl.BlockSpec((tk, tn), lambda i, j, k, kloc=kloc, boff=boff: (boff + kloc(k), j)))
        args += [a, b]
    return pl.pallas_call(
        body, grid=(m // tm, n // tn, nk), in_specs=in_specs,
        out_specs=pl.BlockSpec((tm, tn), lambda i, j, k: (i, j)),
        out_shape=jax.ShapeDtypeStruct((m, n), out_dtype),
        scratch_shapes=[pltpu.VMEM((tm, tn), F32)],
        name=name, compiler_params=_cparams(("arbitrary", "arbitrary", "arbitrary")),
    )(*args)


def _rms(x, g):
    return x * lax.rsqrt(jnp.mean(x * x, axis=-1, keepdims=True) + RMS_EPS) * g


def f_nm0(x, g, sh, sc):
    return _rms(x, g) * (1.0 + sc) + sh


def f_nm(xp, o, gate, g, sh, sc):
    x = xp + gate * o
    return x, _rms(x, g) * (1.0 + sc) + sh


def f_final(xp, o, gate, g, tgt, valid):
    x = xp + gate * o
    e = (_rms(x, g) - tgt) * valid
    return jnp.sum(e * e, axis=0, keepdims=True) * (0.5 / D)


@functools.partial(jax.custom_vjp, nondiff_argnums=(1,))
def _sroll(x, s):
    return pltpu.roll(x, s, 0)


def _sroll_fwd(x, s):
    return pltpu.roll(x, s, 0), None


def _sroll_bwd(s, _, g):
    return (pltpu.roll(g, (g.shape[0] - s) % g.shape[0], 0),)


_sroll.defvjp(_sroll_fwd, _sroll_bwd)


def _shifted(x, o):
    n = x.shape[0]
    return x if o == 0 else _sroll(x, (n - o) % n)


def f_conv1d(x, w, b, *, lc, act):
    n = x.shape[0]
    pos = lax.broadcasted_iota(jnp.int32, (n, 1), 0)
    lo = jnp.where(pos < lc, 0, lc)
    hi = jnp.where(pos < lc, lc, n)
    y = x * w[1:2] + b
    for k, o in ((0, -1), (2, 1), (3, 2)):
        src = pos + o
        valid = (src >= lo) & (src < hi)
        y = y + jnp.where(valid, _shifted(x, o), 0.0) * w[k:k + 1]
    return jax.nn.silu(y) if act else y


def ffnconv_masks(n, lc):
    pos = lax.broadcasted_iota(jnp.int32, (n, 128), 0)
    is_ctx = pos < lc
    tl = pos - lc
    r = tl // GRID_W
    cc = tl - r * GRID_W
    rows = (n - lc) // GRID_W
    left = jnp.where(is_ctx, pos >= 1, cc >= 1)
    right = jnp.where(is_ctx, pos < lc - 1, cc < GRID_W - 1)
    above = jnp.logical_not(is_ctx) & (r >= 1)
    below = jnp.logical_not(is_ctx) & (r < rows - 1)
    return jnp.stack([left, right, above, below]).astype(F32)


def f_ffnconv(a, up, w, b, mk):
    cols = (mk[0] * _shifted(a, -1), a, mk[1] * _shifted(a, 1))
    y = b
    for dr in (-1, 0, 1):
        k = 3 * (dr + 1)
        inner = cols[0] * w[k:k + 1] + cols[1] * w[k + 1:k + 2] + cols[2] * w[k + 2:k + 3]
        y = y + (inner if dr == 0 else mk[2 + (dr > 0)] * _shifted(inner, GRID_W * dr))
    return jax.nn.silu(y) * up


def f_ssd(xs, bc, dtraw, bias, alog, st, *, d, reverse):
    L = xs.shape[0]
    dtv = jax.nn.softplus(dtraw + bias)
    la = dtv * (-jnp.exp(alog))
    ri = lax.broadcasted_iota(jnp.int32, (L, L), 0)
    ci = lax.broadcasted_iota(jnp.int32, (L, L), 1)
    mask = (ci >= ri) if reverse else (ci <= ri)
    cum = jnp.dot(mask.astype(F32), la, precision=HI, preferred_element_type=F32)
    cum_t = cum.T
    tot = cum[0:1] if reverse else cum[L - 1:L]
    lo = lax.broadcasted_iota(jnp.int32, (1, 128), 1) < 64
    rlo = lax.broadcasted_iota(jnp.int32, (128, 1), 0) < 64
    ys, new = [], []
    for g in range(2):
        bg = bc[:, g * 128:(g + 1) * 128].astype(BF16)
        cg = bc[:, 256 + g * 128:256 + (g + 1) * 128].astype(BF16)
        cb = lax.dot_general(cg, bg, (((1,), (1,)), ((), ())), preferred_element_type=F32)
        sg = st[4 * g:4 * g + 4].reshape(4 * 128, 128)
        ch_all = lax.dot_general(cg, sg.astype(BF16), (((1,), (1,)), ((), ())), preferred_element_type=F32)
        xes, dcols = [], []
        for jj in range(4):
            j = 4 * g + jj
            x = xs[:, j * 128:(j + 1) * 128]
            k1 = 16 * d + 2 * j
            k2 = k1 + 1
            c1, c2 = cum[:, k1:k1 + 1], cum[:, k2:k2 + 1]
            m1 = cb * jnp.exp(jnp.where(mask, c1 - cum_t[k1:k1 + 1, :], -1e30))
            m2 = cb * jnp.exp(jnp.where(mask, c2 - cum_t[k2:k2 + 1, :], -1e30))
            xdt = x * jnp.where(lo, dtv[:, k1:k1 + 1], dtv[:, k2:k2 + 1])
            mcat = jnp.concatenate([m1, m2], axis=1).astype(BF16)
            xcat = jnp.concatenate([jnp.where(lo, xdt, 0.0), jnp.where(lo, 0.0, xdt)], axis=0).astype(BF16)
            y = jnp.dot(mcat, xcat, preferred_element_type=F32)
            y = y + ch_all[:, jj * 128:(jj + 1) * 128] * jnp.where(lo, jnp.exp(c1), jnp.exp(c2))
            t1, t2 = tot[:, k1:k1 + 1], tot[:, k2:k2 + 1]
            xes.append((xdt * jnp.where(lo, jnp.exp(t1 - c1), jnp.exp(t2 - c2))).astype(BF16))
            dcols.append(jnp.where(rlo, jnp.exp(t1), jnp.exp(t2)))
            ys.append(y)
        upd = lax.dot_general(jnp.concatenate(xes, axis=1), bg, (((0,), (0,)), ((), ())), preferred_element_type=F32)
        new.append((sg * jnp.concatenate(dcols, axis=0) + upd).reshape(4, 128, 128))
    return jnp.concatenate(ys, axis=1), jnp.concatenate(new, axis=0)


def f_hgrn(q_raw, f_raw, v, lb, zt, *, reverse, chunk):
    n = q_raw.shape[0]
    c = chunk
    qa = jax.nn.silu(q_raw)
    logf = jnp.log(lb + (1.0 - lb) * jax.nn.sigmoid(f_raw))
    kk = (1.0 - lb) * jax.nn.sigmoid(-f_raw)
    ri = lax.broadcasted_iota(jnp.int32, (n, n), 0)
    ci = lax.broadcasted_iota(jnp.int32, (n, n), 1)
    tmat = ((ri // c == ci // c) & ((ci >= ri) if reverse else (ci <= ri))).astype(F32)
    cum_all = jnp.dot(tmat, logf, precision=HI, preferred_element_type=F32)
    r3 = lax.broadcasted_iota(jnp.int32, (c, c, 128), 0)
    c3 = lax.broadcasted_iota(jnp.int32, (c, c, 128), 1)
    mask3 = (c3 >= r3) if reverse else (c3 <= r3)
    nch = n // c
    outs = [None] * nch
    for chn in (reversed(range(nch)) if reverse else range(nch)):
        sl = slice(chn * c, (chn + 1) * c)
        q, k, vv, cum = qa[sl], kk[sl], v[sl], cum_all[sl]
        dec = jnp.exp(jnp.where(mask3, cum[:, None, :] - cum[None, :, :], -1e30))
        att = jnp.sum(q[:, None, :] * dec * k[None, :, :], axis=-1, keepdims=True)
        y = jnp.sum(att * vv[None, :, :], axis=1)
        y = y + lax.dot_general((q * jnp.exp(cum)).astype(BF16), zt.astype(BF16), (((1,), (1,)), ((), ())),
                                preferred_element_type=F32)
        tot = cum[0:1] if reverse else cum[c - 1:c]
        kd = (k * jnp.exp(tot - cum)).astype(BF16)
        zt = zt * jnp.exp(tot) + lax.dot_general(vv.astype(BF16), kd, (((0,), (0,)), ((), ())), preferred_element_type=F32)
        outs[chn] = y
    return jnp.concatenate(outs, axis=0), zt


def f_hgrn_group(q_raw, f_raw, v, lb, zt, *, reverse, chunk):
    ys, zs = [], []
    for h in range(q_raw.shape[1] // 128):
        sl = slice(h * 128, (h + 1) * 128)
        y, z = f_hgrn(q_raw[:, sl], f_raw[:, sl], v[:, sl], lb[:, sl], zt[sl], reverse=reverse, chunk=chunk)
        ys.append(y)
        zs.append(z)
    return jnp.concatenate(ys, axis=1), jnp.concatenate(zs, axis=0)


def _expm1(x):
    poly = x * (1.0 + x * (0.5 + x * (1.0 / 6 + x * (1.0 / 24 + x * (1.0 / 120 + x * (1.0 / 720))))))
    return jnp.where(jnp.abs(x) < 0.3, poly, jnp.exp(x) - 1.0)


def f_gates(u, wa, ba, wi, bi, lam):
    rs, is_ = [], []
    for nb in range(8):
        un = u[:, nb * 128:(nb + 1) * 128].astype(BF16)
        rs.append(jnp.dot(un, wa[nb].astype(BF16), preferred_element_type=F32))
        is_.append(jnp.dot(un, wi[nb].astype(BF16), preferred_element_type=F32))
    r = jax.nn.sigmoid(jnp.concatenate(rs, axis=1) + ba)
    i = jax.nn.sigmoid(jnp.concatenate(is_, axis=1) + bi)
    log_a = -LRU_C * jax.nn.softplus(-lam) * r
    return jnp.exp(log_a), jnp.sqrt(-_expm1(2.0 * log_a)) * (i * u)


def f_ssdfin(y0, y1, xs, z, gy, h0, h1, dpad, ng):
    kk = lax.broadcasted_iota(jnp.int32, (128, D), 0)
    ch = lax.broadcasted_iota(jnp.int32, (128, D), 1)
    expand = (ch // 64 == kk).astype(F32)
    dvec = jnp.dot(dpad, expand, precision=HI, preferred_element_type=F32)[0:1]
    y = y0 + y1 + dvec * xs
    yn = _rms(y * jax.nn.silu(z), ng)
    r = (h0 + h1) * jax.nn.gelu(gy)
    return jnp.concatenate([yn, r], axis=1)


def f_oddfin(o0, o1, g, y0, y1, u, hn, sd, gw, gb):
    parts = []
    for h in range(HG_HEADS):
        sl = slice(h * 128, (h + 1) * 128)
        parts.append(_rms(o0[:, sl] + o1[:, sl], hn[h:h + 1]) * jax.nn.silu(g[:, sl]))
    y = jax.nn.gelu(y0 + y1 + sd * u)
    y = y * jax.nn.sigmoid(jnp.dot(y.astype(BF16), gw.astype(BF16), preferred_element_type=F32) + gb)
    return jnp.concatenate(parts + [y], axis=1)


def f_s5p(lre, lim, lstep, btr, bti):
    step = jnp.exp(lstep)
    mag = jnp.exp(lre * step)
    ar, ai = mag * jnp.cos(lim * step), mag * jnp.sin(lim * step)
    den = lre * lre + lim * lim
    zr = ((ar - 1.0) * lre + ai * lim) / den
    zi = (ai * lre - (ar - 1.0) * lim) / den
    bbr = zr[:, None, :] * btr - zi[:, None, :] * bti
    bbi = zr[:, None, :] * bti + zi[:, None, :] * btr
    return ar, ai, bbr, bbi


def f_lb(logits):
    m = jnp.max(logits, axis=0, keepdims=True)
    e = jnp.exp(logits - m)
    p = e / jnp.sum(e, axis=0, keepdims=True)
    return p[1:2], p[1:2] + p[2:3] + p[3:4]


def f_adamw(w, m, v, *gs):
    g = gs[0]
    for t in gs[1:]:
        g = g + t
    m = ADAM_B1 * m + (1.0 - ADAM_B1) * g
    v = ADAM_B2 * v + (1.0 - ADAM_B2) * jnp.square(g)
    m_hat = m / (1.0 - ADAM_B1 ** ADAM_STEP)
    v_hat = v / (1.0 - ADAM_B2 ** ADAM_STEP)
    delta = -ADAM_LR * (m_hat / (jnp.sqrt(v_hat) + ADAM_EPS) + ADAM_WD * w)
    return g, delta, m, v


def scan_fwd(name, f, grid, ins, y_out, st_out, state_shape, is_first):
    n_in = len(ins)

    def body(*refs):
        y_ref, so_ref, st = refs[n_in], refs[n_in + 1], refs[n_in + 2]
        ids = [pl.program_id(a) for a in range(len(grid))]

        @pl.when(is_first(*ids))
        def _():
            st[...] = jnp.zeros_like(st)

        s = st[...]
        so_ref[...] = s
        y, new = f(*[r[...] for r in refs[:n_in]], s)
        y_ref[...] = y.astype(y_ref.dtype)
        st[...] = new

    return pl.pallas_call(
        body, grid=grid,
        in_specs=[pl.BlockSpec(b, m) for _, b, m in ins],
        out_specs=[pl.BlockSpec(y_out[2], y_out[3]), pl.BlockSpec(st_out[2], st_out[3])],
        out_shape=[jax.ShapeDtypeStruct(y_out[0], y_out[1]), jax.ShapeDtypeStruct(st_out[0], st_out[1])],
        scratch_shapes=[pltpu.VMEM(state_shape, F32)],
        name=name, compiler_params=_cparams(("arbitrary",) * len(grid)),
    )(*[a for a, _, _ in ins])


def scan_bwd(name, f, grid, ins, st_in, dy, grads, state_shape, is_first):
    n_in = len(ins)
    didx = [g[0] for g in grads]

    def body(*refs):
        s_ref, dy_ref = refs[n_in], refs[n_in + 1]
        g_refs = refs[n_in + 2:n_in + 2 + len(grads)]
        dst = refs[n_in + 2 + len(grads)]
        ids = [pl.program_id(a) for a in range(len(grid))]

        @pl.when(is_first(*ids))
        def _():
            dst[...] = jnp.zeros_like(dst)

        vals = [r[...] for r in refs[:n_in]]

        def fd(s, *dv):
            full = list(vals)
            for i, v in zip(didx, dv):
                full[i] = v
            return f(*full, s)

        (y, _), vjp = jax.vjp(fd, s_ref[...], *[vals[i] for i in didx])
        gs = vjp((dy_ref[...].astype(y.dtype), dst[...]))
        dst[...] = gs[0]
        for g, spec, ref in zip(gs[1:], grads, g_refs):
            first = spec[5]
            if first is None:
                ref[...] = g.astype(ref.dtype)
            else:
                fst = first(*ids)

                @pl.when(fst)
                def _(ref=ref, g=g):
                    ref[...] = g.astype(ref.dtype)

                @pl.when(jnp.logical_not(fst))
                def _(ref=ref, g=g):
                    ref[...] += g.astype(ref.dtype)

    allin = list(ins) + [st_in, dy]
    return pl.pallas_call(
        body, grid=grid,
        in_specs=[pl.BlockSpec(b, m) for _, b, m in allin],
        out_specs=[pl.BlockSpec(s[3], s[4]) for s in grads],
        out_shape=[jax.ShapeDtypeStruct(s[1], s[2]) for s in grads],
        scratch_shapes=[pltpu.VMEM(state_shape, F32)],
        name=name, compiler_params=_cparams(("arbitrary",) * len(grid)),
    )(*[a for a, _, _ in allin])


def _tile_order(order, nt, nctx=1):
    rev = lambda j: jnp.where(j < nctx, nctx - 1 - j, nt - 1 - (j - nctx))
    if order == 'F':
        return (lambda j: j), True
    if order == 'Fb':
        return (lambda j: nt - 1 - j), False
    if order == 'R':
        return rev, False
    return (lambda j: rev(nt - 1 - j)), True


def _scan8(coef, val, sub, asc):
    for step in (1, 2, 4):
        shift = step if asc else 8 - step
        keep = (sub >= step) if asc else (sub < 8 - step)
        val = jnp.where(keep, coef * pltpu.roll(val, shift, 0) + val, val)
        coef = jnp.where(keep, coef * pltpu.roll(coef, shift, 0), coef)
    return coef, val


def _prev_rows(tile, carry, sub, asc):
    return jnp.where(sub == 0, carry, pltpu.roll(tile, 1, 0)) if asc else jnp.where(sub == 7, carry, pltpu.roll(tile, 7, 0))


def _last_row(tile, asc):
    return jnp.broadcast_to(tile[7:8] if asc else tile[0:1], tile.shape)


def linrec(name, a, b, order):
    bsz, tt, cols = a.shape
    tq = _pick(tt, (256, 128))
    nt, ng, nj = tt // tq, tq // 8, cols // 128
    phys, asc = _tile_order(order, nt)

    def body(a_ref, b_ref, h_ref, hp_ref, hc):
        @pl.when(pl.program_id(0) == 0)
        def _():
            hc[...] = jnp.zeros_like(hc)

        sub = lax.broadcasted_iota(jnp.int32, (8, 128), 0)

        def group(i, carry):
            rows = pl.ds(pl.multiple_of((i if asc else ng - 1 - i) * 8, 8), 8)
            for bi in range(bsz):
                for j in range(nj):
                    cs = slice(j * 128, (j + 1) * 128)
                    h_in = hc[bi, j]
                    ca, cv = _scan8(a_ref[bi, rows, cs], b_ref[bi, rows, cs], sub, asc)
                    h = ca * h_in + cv
                    h_ref[bi, rows, cs] = h
                    hp_ref[bi, rows, cs] = _prev_rows(h, h_in, sub, asc)
                    hc[bi, j] = _last_row(h, asc)
            return carry

        lax.fori_loop(0, ng, group, 0)

    spec = pl.BlockSpec((bsz, tq, cols), lambda j: (0, phys(j), 0))
    return pl.pallas_call(
        body, grid=(nt,), in_specs=[spec, spec], out_specs=[spec, spec],
        out_shape=[jax.ShapeDtypeStruct(a.shape, F32)] * 2,
        scratch_shapes=[pltpu.VMEM((bsz, nj, 8, 128), F32)],
        name=name, compiler_params=_cparams(("arbitrary",)),
    )(a, b)


def linrec_bwd(name, a, dh, hprev, order):
    bsz, tt, cols = a.shape
    tq = _pick(tt, (256, 128))
    nt, ng, nj = tt // tq, tq // 8, cols // 128
    phys, asc = _tile_order(order, nt)

    def body(a_ref, dh_ref, hp_ref, g_ref, ga_ref, gc, ac):
        @pl.when(pl.program_id(0) == 0)
        def _():
            gc[...] = jnp.zeros_like(gc)
            ac[...] = jnp.zeros_like(ac)

        sub = lax.broadcasted_iota(jnp.int32, (8, 128), 0)

        def group(i, carry):
            rows = pl.ds(pl.multiple_of((i if asc else ng - 1 - i) * 8, 8), 8)
            for bi in range(bsz):
                for j in range(nj):
                    cs = slice(j * 128, (j + 1) * 128)
                    a_tile = a_ref[bi, rows, cs]
                    ca, cv = _scan8(_prev_rows(a_tile, ac[bi, j], sub, asc), dh_ref[bi, rows, cs], sub, asc)
                    g = ca * gc[bi, j] + cv
                    g_ref[bi, rows, cs] = g
                    ga_ref[bi, rows, cs] = g * hp_ref[bi, rows, cs]
                    gc[bi, j] = _last_row(g, asc)
                    ac[bi, j] = _last_row(a_tile, asc)
            return carry

        lax.fori_loop(0, ng, group, 0)

    spec = pl.BlockSpec((bsz, tq, cols), lambda j: (0, phys(j), 0))
    return pl.pallas_call(
        body, grid=(nt,), in_specs=[spec, spec, spec], out_specs=[spec, spec],
        out_shape=[jax.ShapeDtypeStruct(a.shape, F32)] * 2,
        scratch_shapes=[pltpu.VMEM((bsz, nj, 8, 128), F32), pltpu.VMEM((bsz, nj, 8, 128), F32)],
        name=name, compiler_params=_cparams(("arbitrary",)),
    )(a, dh, hprev)


def _cmul(a, b):
    return a[0] * b[0] - a[1] * b[1], a[0] * b[1] + a[1] * b[0]


def _cpow_tables(ar, ai, asc):
    pows = [(ar, ai)]
    for _ in range(7):
        pows.append(_cmul(pows[-1], (ar, ai)))
    tile = lambda p: jnp.broadcast_to(p[:, None, :], (8, 8, 128))
    steps = jnp.stack([jnp.stack([tile(pows[s - 1][0]), tile(pows[s - 1][1])]) for s in (1, 2, 4)])
    order = range(8) if asc else range(7, -1, -1)
    carry = jnp.stack([jnp.stack([pows[i][c] for i in order], axis=1) for c in (0, 1)])
    return steps, carry


def _cscan8(xr, xi, st_ref, j, sub, asc):
    for s, step in enumerate((1, 2, 4)):
        shift = step if asc else 8 - step
        keep = (sub >= step) if asc else (sub < 8 - step)
        pr, pi = st_ref[s, 0, j], st_ref[s, 1, j]
        rr, ri = pltpu.roll(xr, shift, 0), pltpu.roll(xi, shift, 0)
        xr, xi = jnp.where(keep, xr + pr * rr - pi * ri, xr), jnp.where(keep, xi + pr * ri + pi * rr, xi)
    return xr, xi


def clinrec(name, x, coef, order, lc, conj=False, hprev=None):
    btot, tt, cols2 = x.shape
    cols = cols2 // 2
    bsz, ngrp = btot, 1
    tq = 128
    nt, ng, nj = tt // tq, tq // 8, cols // 128
    phys, asc = _tile_order(order, nt, lc // tq)
    steps, carry = _cpow_tables(coef[0], -coef[1] if conj else coef[1], asc)
    adjoint = hprev is not None

    def body(*refs):
        if adjoint:
            x_ref, hp_ref, st_ref, cr_ref, h_ref, dc_ref, hc = refs
        else:
            x_ref, st_ref, cr_ref, h_ref, hp_ref, hc = refs

        @pl.when(pl.program_id(1) == 0)
        def _():
            hc[...] = jnp.zeros_like(hc)

        if adjoint:
            @pl.when((pl.program_id(0) == 0) & (pl.program_id(1) == 0))
            def _():
                dc_ref[...] = jnp.zeros_like(dc_ref)

        sub = lax.broadcasted_iota(jnp.int32, (8, 128), 0)

        def group(i, c_):
            rows = pl.ds(pl.multiple_of((i if asc else ng - 1 - i) * 8, 8), 8)
            for bi in range(bsz):
                for j in range(nj):
                    cr, ci = slice(j * 128, (j + 1) * 128), slice(cols + j * 128, cols + (j + 1) * 128)
                    sr, si = _cscan8(x_ref[bi, rows, cr], x_ref[bi, rows, ci], st_ref, j, sub, asc)
                    in_r, in_i = hc[bi, 0, j], hc[bi, 1, j]
                    pr, pi = cr_ref[0, j], cr_ref[1, j]
                    hr = sr + pr * in_r - pi * in_i
                    hi = si + pr * in_i + pi * in_r
                    h_ref[bi, rows, cr] = hr
                    h_ref[bi, rows, ci] = hi
                    if adjoint:
                        qr, qi = hp_ref[bi, rows, cr], hp_ref[bi, rows, ci]
                        dc_ref[0, j] += hr * qr + hi * qi
                        dc_ref[1, j] += hi * qr - hr * qi
                    else:
                        hp_ref[bi, rows, cr] = _prev_rows(hr, in_r, sub, asc)
                        hp_ref[bi, rows, ci] = _prev_rows(hi, in_i, sub, asc)
                    hc[bi, 0, j] = _last_row(hr, asc)
                    hc[bi, 1, j] = _last_row(hi, asc)
            return c_

        lax.fori_loop(0, ng, group, 0)

    spec = pl.BlockSpec((bsz, tq, cols2), lambda g, j: (g, phys(j), 0))
    full = lambda t: pl.BlockSpec(t.shape, lambda g, j, n=t.ndim: (0,) * n)
    dc_shape = (2, nj, 8, 128)
    if adjoint:
        ins, in_specs = (x, hprev, steps, carry), [spec, spec, full(steps), full(carry)]
        out_specs = [spec, pl.BlockSpec(dc_shape, lambda g, j: (0, 0, 0, 0))]
        out_shape = [jax.ShapeDtypeStruct(x.shape, F32), jax.ShapeDtypeStruct(dc_shape, F32)]
    else:
        ins, in_specs = (x, steps, carry), [spec, full(steps), full(carry)]
        out_specs = [spec, spec]
        out_shape = [jax.ShapeDtypeStruct(x.shape, F32)] * 2
    return pl.pallas_call(
        body, grid=(ngrp, nt), in_specs=in_specs, out_specs=out_specs, out_shape=out_shape,
        scratch_shapes=[pltpu.VMEM((bsz, 2, nj, 8, 128), F32)],
        name=name, compiler_params=_cparams(("arbitrary", "arbitrary")),
    )(*ins)


def _blockdiag(bb):
    eye = jnp.eye(16, dtype=bb.dtype)
    return (bb[:, :, None, :] * eye[:, None, :, None]).reshape(256, 1024)


def _blockdiag_t(c):
    eye = jnp.eye(16, dtype=c.dtype)
    return (jnp.swapaxes(c, 1, 2)[:, :, None, :] * eye[:, None, :, None]).reshape(1024, 256)


def _unblockdiag(m):
    eye = jnp.eye(16, dtype=m.dtype)
    return jnp.sum(m.reshape(16, 16, 16, 64) * eye[:, None, :, None], axis=2)


def _unblockdiag_t(m):
    eye = jnp.eye(16, dtype=m.dtype)
    return jnp.swapaxes(jnp.sum(m.reshape(16, 64, 16, 16) * eye[:, None, :, None], axis=2), 1, 2)


def _pad_rows(v, rows=8, cols=128):
    out = jnp.zeros((rows, cols), F32)
    return out.at[0, :v.shape[0]].set(v)


def cond_rows(c, c_ctx):
    return jnp.zeros((8, D), F32).at[:c.shape[0]].set(c).at[4].set(c_ctx)


def mod_table(name, cc_rows, w, b):
    nl, _, cols = w.shape
    nr = cc_rows.shape[0]

    def f(ccv, wv, bv):
        return jnp.dot(jax.nn.silu(ccv).astype(BF16), wv, preferred_element_type=F32) + bv

    return tile_fwd(name, f, (nl, cols // 1536),
                    [(cc_rows, (nr, D), lambda l, n: (0, 0)), (w, (None, D, 1536), lambda l, n: (l, 0, n)),
                     (b.reshape(nl, 1, cols), (None, 1, 1536), lambda l, n: (l, 0, n))],
                    [((nl, nr, cols), F32, (None, nr, 1536), lambda l, n: (l, 0, n))])[0]


def mod_cond_grad(name, cc_rows, dmod_rows, w):
    nl, nr, cols = dmod_rows.shape

    def body(cc_ref, d_ref, w_ref, o_ref):
        _, vjp = jax.vjp(jax.nn.silu, cc_ref[...])
        ds = lax.dot_general(d_ref[...].astype(BF16), w_ref[...], (((1,), (1,)), ((), ())), preferred_element_type=F32)
        (dcc,) = vjp(ds)
        first = (pl.program_id(0) == 0) & (pl.program_id(1) == 0)

        @pl.when(first)
        def _():
            o_ref[...] = dcc

        @pl.when(jnp.logical_not(first))
        def _():
            o_ref[...] += dcc

    return pl.pallas_call(
        body, grid=(nl, cols // 1536),
        in_specs=[pl.BlockSpec((nr, D), lambda l, n: (0, 0)), pl.BlockSpec((None, nr, 1536), lambda l, n: (l, 0, n)),
                  pl.BlockSpec((None, D, 1536), lambda l, n: (l, 0, n))],
        out_specs=pl.BlockSpec((nr, D), lambda l, n: (0, 0)), out_shape=jax.ShapeDtypeStruct((nr, D), F32),
        name=name, compiler_params=_cparams(("arbitrary", "arbitrary")),
    )(cc_rows, dmod_rows, w)


def local_step(x, ctx, target, mod, W):
    B, Tx, _ = x.shape
    Lc = ctx.shape[1]
    Tt = Lc + Tx
    tb = Lc
    nt = Tt // tb
    M = B * Tt
    nc = Tt // SSD_CHUNK
    ncc = Lc // SSD_CHUNK
    fgr, fgw = HG_HEADS // HG_FWD[1], 128 * HG_FWD[1]
    bgr, bgw = HG_HEADS // HG_BWD[1], 128 * HG_BWD[1]
    G = {}

    def add_grad(name, idx, val):
        G.setdefault(name, {})[idx] = val

    def tok(a, cb=None, off=0):
        cb = a.shape[-1] if cb is None else cb
        return (a, (None, tb, cb), lambda b, j, off=off: (b, j, off))

    def tok_out(cols, dtype=F32):
        return ((B, Tt, cols), dtype, (None, tb, cols), lambda b, j: (b, j, 0))

    def vec(a):
        return (a, a.shape, lambda *ids, n=a.ndim: (0,) * n)

    def vec_acc(shape):
        return (shape, F32, shape, lambda *ids, n=len(shape): (0,) * n, lambda *ids: functools.reduce(jnp.logical_and, [i == 0 for i in ids]))

    def modv(l, which):
        return (modr, (None, None, None, 1, D), lambda b, j, l=l, which=which: (l, jnp.where(j == 0, 4, b), which, 0, 0))

    dmod_spec = ((B, 2, 1, D), F32, (None, None, 1, D), lambda b, j: (b, jnp.where(j == 0, 0, 1), 0, 0), lambda b, j: j <= 1)

    def phys_chunk(n_all, n_ctx, reverse):
        if not reverse:
            return lambda s: s
        return lambda s: jnp.where(s < n_ctx, n_ctx - 1 - s, n_all - 1 - (s - n_ctx))

    nmc = N_MOD * D // 1536
    modr = mod.reshape(DEPTH, 8, N_MOD, 1, D)
    dmods = {}

    lb1, lb3 = tile_fwd('lb_fwd', f_lb, (1,), [vec(W['hg_lb_logits'])],
                        [((1, HG_W), F32, (1, HG_W), lambda i: (0, 0))] * 2)
    dlb = {1: [], 3: []}

    x0 = jnp.concatenate([ctx, x], axis=1)
    conv_mk = ffnconv_masks(Tt, Lc)
    R = [dict() for _ in range(DEPTH)]

    xprev, oprev = x0, None
    for l in range(DEPTH):
        r = R[l]
        j = l // 2
        ng = W['norm_mix_g'][l][None]
        if l == 0:
            h1 = tile_fwd(f'nm0_fwd', f_nm0, (B, nt), [tok(xprev), vec(ng), modv(l, 0), modv(l, 1)], [tok_out(D, BF16)])[0]
            xa = xprev
        else:
            xa, h1 = tile_fwd(f'nm_mix_fwd{l}', f_nm, (B, nt),
                              [tok(xprev), tok(oprev), modv(l - 1, 5), vec(ng), modv(l, 0), modv(l, 1)],
                              [tok_out(D), tok_out(D, BF16)])
        r['xin'], r['oin'], r['xa'], r['h1'] = xprev, oprev, xa, h1
        h1m = h1.reshape(M, D)
        if l % 2 == 0:
            win = W['ev_w_in_p'][j]
            proj = mm(f'ev_proj{l}', [(h1m, win, 0)]).reshape(B, Tt, EV_PAD)
            r['proj'] = proj
            scw, scb = W['ssd_conv_w'][j], W['ssd_conv_b'][j][None]
            lcw, lcb = W['lru_conv_w'][j], W['lru_conv_b'][j][None]

            def conv_call(name, colblk0, w, b, wblk0, ncols, act):
                return tile_fwd(name, functools.partial(f_conv1d, lc=Lc, act=act), (ncols // 256, B),
                                [(proj, (None, Tt, 256), lambda cb, bi: (bi, 0, colblk0 + cb)),
                                 (w, (4, 256), lambda cb, bi: (0, wblk0 + cb)), (b, (1, 256), lambda cb, bi: (0, wblk0 + cb))],
                                [((B, Tt, ncols), F32, (None, Tt, 256), lambda cb, bi: (bi, 0, cb))])[0]
            xs_c = conv_call(f'conv_xs{l}', 12, scw, scb, 0, 1024, True)
            bc_c = conv_call(f'conv_bc{l}', 16, scw, scb, 4, 512, True)
            u_c = conv_call(f'conv_u{l}', 8, lcw, lcb, 0, 1024, False)
            r['xs'], r['bc'], r['u'] = xs_c, bc_c, u_c
            bias = _pad_rows(W['ssd_dt_bias'][j].reshape(-1), 1)
            alog = _pad_rows(W['ssd_a_log'][j].reshape(-1), 1)
            r['bias'], r['alog'] = bias, alog
            r['y'], r['st'], r['a4'], r['hp4'], r['h'] = [], [], [], [], []
            for d in range(2):
                ph = phys_chunk(nc, ncc, d == 1)
                y, st = scan_fwd(
                    f'ssd_fwd{l}_{d}', functools.partial(f_ssd, d=d, reverse=(d == 1)), (B, nc),
                    [(xs_c, (None, SSD_CHUNK, D), lambda b, s, ph=ph: (b, ph(s), 0)),
                     (bc_c, (None, SSD_CHUNK, 512), lambda b, s, ph=ph: (b, ph(s), 0)),
                     (proj, (None, SSD_CHUNK, 128), lambda b, s, ph=ph: (b, ph(s), 36)),
                     vec(bias), vec(alog)],
                    ((B, Tt, D), F32, (None, SSD_CHUNK, D), lambda b, s, ph=ph: (b, ph(s), 0)),
                    ((B, nc, 8, 128, 128), F32, (None, None, 8, 128, 128), lambda b, s: (b, s, 0, 0, 0)),
                    (8, 128, 128), lambda b, s: s == 0)
                r['y'].append(y)
                r['st'].append(st)
                a_d, bx_d = tile_fwd(
                    f'gates_fwd{l}_{d}', f_gates, (B, nt),
                    [tok(u_c), vec(W['lru_w_a'][j, d]), vec(W['lru_b_a'][j, d][None]), vec(W['lru_w_i'][j, d]),
                     vec(W['lru_b_i'][j, d][None]), vec(W['lru_lam'][j, d][None])],
                    [tok_out(D), tok_out(D)])
                h_d, hp_d = linrec(f'lru_fwd{l}_{d}', a_d, bx_d, 'F' if d == 0 else 'R')
                r['a4'].append(a_d)
                r['hp4'].append(hp_d)
                r['h'].append(h_d)
            dpad = _pad_rows(W['ssd_d'][j])
            sng = W['ssd_norm_g'][j][None]
            r['dpad'], r['sng'] = dpad, sng
            mix = tile_fwd(f'ssdfin_fwd{l}', f_ssdfin, (B, nt),
                           [tok(r['y'][0]), tok(r['y'][1]), tok(xs_c), tok(proj, D, 0), tok(proj, D, 1), tok(r['h'][0]),
                            tok(r['h'][1]), vec(dpad), vec(sng)], [tok_out(2 * D, BF16)])[0]
            wout = W['ev_w_out'][j]
        else:
            win = W['od_w_in'][j]
            proj = mm(f'od_proj{l}', [(h1m, win, 0)]).reshape(B, Tt, 4096)
            r['proj'] = proj
            lbv = lb1 if l == 1 else lb3
            ns = nc
            r['o'], r['zst'], r['coef'], r['bcat'], r['ccat'], r['hp5'], r['hcat'], r['yd'], r['s5in'] = [], [], [], [], [], [], [], [], []
            u2 = proj[:, :, 3840:].reshape(M, S5_W)
            r['u2'] = u2
            for d in range(2):
                ph = phys_chunk(ns, ncc, d == 1)
                o_d, zst = scan_fwd(
                    f'hgrn_fwd{l}_{d}', functools.partial(f_hgrn_group, reverse=(d == 1), chunk=HG_FWD[0]), (fgr, B, ns),
                    [(proj, (None, 128, fgw), lambda h, b, s, ph=ph: (b, ph(s), h)),
                     (proj, (None, 128, fgw), lambda h, b, s, ph=ph, d=d: (b, ph(s), (1 + d) * fgr + h)),
                     (proj, (None, 128, fgw), lambda h, b, s, ph=ph: (b, ph(s), 3 * fgr + h)),
                     (lbv, (1, fgw), lambda h, b, s: (0, h))],
                    ((B, Tt, HG_W), F32, (None, 128, fgw), lambda h, b, s, ph=ph: (b, ph(s), h)),
                    ((B, ns, fgr, fgw, 128), F32, (None, None, None, fgw, 128), lambda h, b, s: (b, s, h, 0, 0)),
                    (fgw, 128), lambda h, b, s: s == 0)
                r['o'].append(o_d)
                r['zst'].append(zst)
                s5in = [W['s5_lam_re'][j, d], W['s5_lam_im'][j, d], W['s5_log_step'][j, d].reshape(16, 1),
                        jnp.swapaxes(W['s5_b_re'][j], 1, 2), jnp.swapaxes(W['s5_b_im'][j], 1, 2)]
                r['s5in'].append(s5in)
                ar, ai, bbr, bbi = tile_fwd(f's5p_fwd{l}_{d}', f_s5p, (1,), [vec(t) for t in s5in],
                                            [((16, 64), F32, (16, 64), lambda i: (0, 0))] * 2
                                            + [((16, 16, 64), F32, (16, 16, 64), lambda i: (0, 0, 0))] * 2)
                coef = jnp.stack([ar.reshape(8, 128), ai.reshape(8, 128)])
                bcat = jnp.concatenate([_blockdiag(bbr), _blockdiag(bbi)], axis=1).astype(BF16)
                ccat = jnp.concatenate([_blockdiag_t(W['s5_c_re'][j, d]), -_blockdiag_t(W['s5_c_im'][j, d])], axis=0).astype(BF16)
                xcat = mm(f's5_in{l}_{d}', [(u2, bcat, 0)])
                h5, hp5 = clinrec(f's5_fwd{l}_{d}', xcat.reshape(B, Tt, 2 * D), coef, 'F' if d == 0 else 'R', Lc)
                hcat = h5.reshape(M, 2 * D)
                yd = mm(f's5_out{l}_{d}', [(hcat, ccat, 0)]).reshape(B, Tt, S5_W)
                r['coef'].append(coef)
                r['bcat'].append(bcat)
                r['ccat'].append(ccat)
                r['hp5'].append(hp5)
                r['hcat'].append(hcat)
                r['yd'].append(yd)
            hn = jnp.zeros((8, 128), F32).at[:HG_HEADS].set(W['hg_norm_g'][j])
            sd, gw, gb = W['s5_d'][j][None], W['s5_glu_w'][j], W['s5_glu_b'][j][None]
            r['fin_par'] = (hn, sd, gw, gb)
            mix = tile_fwd(f'oddfin_fwd{l}', f_oddfin, (B, nt),
                           [tok(r['o'][0]), tok(r['o'][1]), tok(proj, HG_W, 4), tok(r['yd'][0]), tok(r['yd'][1]),
                            tok(proj, S5_W, 15), vec(hn), vec(sd), vec(gw), vec(gb)], [tok_out(D, BF16)])[0]
            wout = W['od_w_out'][j]
        r['mix'] = mix
        o1 = mm(f'mix_out{l}', [(mix.reshape(M, -1), wout, 0)]).reshape(B, Tt, D)
        r['o1'] = o1
        fg = W['norm_ffn_g'][l][None]
        xb, h2 = tile_fwd(f'nm_ffn_fwd{l}', f_nm, (B, nt), [tok(xa), tok(o1), modv(l, 2), vec(fg), modv(l, 3), modv(l, 4)],
                          [tok_out(D), tok_out(D, BF16)])
        r['h2'] = h2
        h2m = h2.reshape(M, D)
        a = mm(f'ffn_gate{l}', [(h2m, W['ffn_w_gate'][l], 0)]).reshape(B, Tt, D_FF)
        up = mm(f'ffn_up{l}', [(h2m, W['ffn_w_up'][l], 0)]).reshape(B, Tt, D_FF)
        w9 = W['ffn_conv_w'][l].reshape(9, D_FF)
        cbias = W['ffn_conv_b'][l][None]
        r['a'], r['up'], r['w9'], r['cbias'] = a, up, w9, cbias
        act = tile_fwd(f'ffnconv_fwd{l}', f_ffnconv, (D_FF // 128, B),
                       [(a, (None, Tt, 128), lambda cb, bi: (bi, 0, cb)), (up, (None, Tt, 128), lambda cb, bi: (bi, 0, cb)),
                        (w9, (9, 128), lambda cb, bi: (0, cb)), (cbias, (1, 128), lambda cb, bi: (0, cb)), vec(conv_mk)],
                       [((B, Tt, D_FF), BF16, (None, Tt, 128), lambda cb, bi: (bi, 0, cb))])[0]
        r['act'] = act
        o2 = mm(f'ffn_down{l}', [(act.reshape(M, D_FF), W['ffn_w_down'][l], 0)]).reshape(B, Tt, D)
        xprev, oprev = xb, o2

    vmask = jnp.ones((nt, 1, D), F32).at[0].set(0.0)
    ones = jnp.ones((1, D), F32)
    fng = W['final_norm_g'][None]
    d_xp, d_o2, dg5, dfng, loss_vec = tile_bwd(
        'loss_head', f_final, (B, nt),
        [tok(xprev), tok(oprev), modv(DEPTH - 1, 5), vec(fng),
         (target, (None, tb, D), lambda b, j: (b, jnp.maximum(j - 1, 0), 0)), (vmask, (None, 1, D), lambda b, j: (j, 0, 0))],
        [[vec(ones)]],
        [(0,) + tok_out(D) + (None,), (1,) + tok_out(D, BF16) + (None,), (2,) + dmod_spec, (3,) + vec_acc((1, D))],
        prims=[(0,) + vec_acc((1, D))])
    loss = jnp.sum(loss_vec)
    add_grad('final_norm_g', None, dfng[0])
    dmods[(DEPTH - 1, 5)] = dg5

    for l in reversed(range(DEPTH)):
        r = R[l]
        j = l // 2
        d_o2m = d_o2.reshape(M, D)
        d_act = mm(f'ffn_down_dx{l}', [(d_o2m, W['ffn_w_down'][l], 0)], tb=True).reshape(B, Tt, D_FF)
        add_grad('ffn_w_down', l, mm(f'ffn_down_dw{l}', [(r['act'].reshape(M, D_FF), d_o2m, 0)], ta=True))
        d_a, d_up, dw9, dcb = tile_bwd(
            f'ffnconv_bwd{l}', f_ffnconv, (D_FF // 128, B),
            [(r['a'], (None, Tt, 128), lambda cb, bi: (bi, 0, cb)), (r['up'], (None, Tt, 128), lambda cb, bi: (bi, 0, cb)),
             (r['w9'], (9, 128), lambda cb, bi: (0, cb)), (r['cbias'], (1, 128), lambda cb, bi: (0, cb)), vec(conv_mk)],
            [[(d_act, (None, Tt, 128), lambda cb, bi: (bi, 0, cb))]],
            [(0, (B, Tt, D_FF), BF16, (None, Tt, 128), lambda cb, bi: (bi, 0, cb), None),
             (1, (B, Tt, D_FF), BF16, (None, Tt, 128), lambda cb, bi: (bi, 0, cb), None),
             (2, (9, D_FF), F32, (9, 128), lambda cb, bi: (0, cb), lambda cb, bi: bi == 0),
             (3, (1, D_FF), F32, (1, 128), lambda cb, bi: (0, cb), lambda cb, bi: bi == 0)])
        add_grad('ffn_conv_w', l, dw9.reshape(3, 3, D_FF))
        add_grad('ffn_conv_b', l, dcb[0])
        d_am, d_upm = d_a.reshape(M, D_FF), d_up.reshape(M, D_FF)
        h2m = r['h2'].reshape(M, D)
        d_h2 = mm(f'ffn_in_dx{l}', [(d_am, W['ffn_w_gate'][l], 0), (d_upm, W['ffn_w_up'][l], 0)], tb=True).reshape(B, Tt, D)
        add_grad('ffn_w_gate', l, mm(f'ffn_gate_dw{l}', [(h2m, d_am, 0)], ta=True))
        add_grad('ffn_w_up', l, mm(f'ffn_up_dw{l}', [(h2m, d_upm, 0)], ta=True))
        fg = W['norm_ffn_g'][l][None]
        d_xa, d_o1, dgate, dfg, dsh, dsc = tile_bwd(
            f'nm_ffn_bwd{l}', f_nm, (B, nt), [tok(r['xa']), tok(r['o1']), modv(l, 2), vec(fg), modv(l, 3), modv(l, 4)],
            [[tok(d_xp)], [tok(d_h2)]],
            [(0,) + tok_out(D) + (None,), (1,) + tok_out(D, BF16) + (None,), (2,) + dmod_spec, (3,) + vec_acc((1, D)),
             (4,) + dmod_spec, (5,) + dmod_spec])
        add_grad('norm_ffn_g', l, dfg[0])
        dmods[(l, 2)], dmods[(l, 3)], dmods[(l, 4)] = dgate, dsh, dsc
        d_o1m = d_o1.reshape(M, D)
        h1m = r['h1'].reshape(M, D)
        proj = r['proj']
        if l % 2 == 0:
            wout, win = W['ev_w_out'][j], W['ev_w_in_p'][j]
            d_mix = mm(f'mix_out_dx{l}', [(d_o1m, wout, 0)], tb=True).reshape(B, Tt, 2 * D)
            add_grad('ev_w_out', j, mm(f'mix_out_dw{l}', [(r['mix'].reshape(M, 2 * D), d_o1m, 0)], ta=True))
            d_y, d_xs_fin, d_z, d_gy, d_h, ddpad, dsng = tile_bwd(
                f'ssdfin_bwd{l}', f_ssdfin, (B, nt),
                [tok(r['y'][0]), tok(r['y'][1]), tok(r['xs']), tok(proj, D, 0), tok(proj, D, 1), tok(r['h'][0]), tok(r['h'][1]),
                 vec(r['dpad']), vec(r['sng'])],
                [[tok(d_mix)]],
                [(0,) + tok_out(D) + (None,), (2,) + tok_out(D) + (None,), (3,) + tok_out(D) + (None,), (4,) + tok_out(D) + (None,),
                 (5,) + tok_out(D) + (None,), (7,) + vec_acc((8, 128)), (8,) + vec_acc((1, D))])
            add_grad('ssd_d', j, ddpad[0, :SSD_HEADS])
            add_grad('ssd_norm_g', j, dsng[0])
            d_xs_parts, d_bc_parts, d_dt_parts, d_u_parts = [d_xs_fin], [], [], []
            dbias_t, dalog_t = [], []
            dh4 = d_h
            for d in range(2):
                ph0 = phys_chunk(nc, ncc, d == 1)

                def ph(s, ph0=ph0):
                    return ph0(nc - 1 - s)
                dxs_d, dbc_d, ddt_d, dbias, dalog = scan_bwd(
                    f'ssd_bwd{l}_{d}', functools.partial(f_ssd, d=d, reverse=(d == 1)), (B, nc),
                    [(r['xs'], (None, SSD_CHUNK, D), lambda b, s, ph=ph: (b, ph(s), 0)),
                     (r['bc'], (None, SSD_CHUNK, 512), lambda b, s, ph=ph: (b, ph(s), 0)),
                     (proj, (None, SSD_CHUNK, 128), lambda b, s, ph=ph: (b, ph(s), 36)),
                     vec(r['bias']), vec(r['alog'])],
                    (r['st'][d], (None, None, 8, 128, 128), lambda b, s: (b, nc - 1 - s, 0, 0, 0)),
                    (d_y, (None, SSD_CHUNK, D), lambda b, s, ph=ph: (b, ph(s), 0)),
                    [(0, (B, Tt, D), F32, (None, SSD_CHUNK, D), lambda b, s, ph=ph: (b, ph(s), 0), None),
                     (1, (B, Tt, 512), F32, (None, SSD_CHUNK, 512), lambda b, s, ph=ph: (b, ph(s), 0), None),
                     (2, (B, Tt, 128), F32, (None, SSD_CHUNK, 128), lambda b, s, ph=ph: (b, ph(s), 0), None),
                     (3,) + vec_acc((1, 128)), (4,) + vec_acc((1, 128))],
                    (8, 128, 128), lambda b, s: s == 0)
                d_xs_parts.append(dxs_d)
                d_bc_parts.append(dbc_d)
                d_dt_parts.append(ddt_d)
                dbias_t.append(dbias)
                dalog_t.append(dalog)
                g4, ga4 = linrec_bwd(f'lru_bwd{l}_{d}', r['a4'][d], dh4, r['hp4'][d], 'Fb' if d == 0 else 'Rb')
                du_g, dwa, dba, dwi, dbi, dlam = tile_bwd(
                    f'gates_bwd{l}_{d}', f_gates, (B, nt),
                    [tok(r['u']), vec(W['lru_w_a'][j, d]), vec(W['lru_b_a'][j, d][None]), vec(W['lru_w_i'][j, d]),
                     vec(W['lru_b_i'][j, d][None]), vec(W['lru_lam'][j, d][None])],
                    [[tok(ga4)], [tok(g4)]],
                    [(0,) + tok_out(D) + (None,), (1,) + vec_acc((8, 128, 128)), (2,) + vec_acc((1, D)), (3,) + vec_acc((8, 128, 128)),
                     (4,) + vec_acc((1, D)), (5,) + vec_acc((1, D))])
                d_u_parts.append(du_g)
                add_grad('lru_w_a', (j, d), dwa)
                add_grad('lru_b_a', (j, d), dba[0])
                add_grad('lru_w_i', (j, d), dwi)
                add_grad('lru_b_i', (j, d), dbi[0])
                add_grad('lru_lam', (j, d), dlam[0])
            add_grad('ssd_dt_bias', j, (dbias_t[0] + dbias_t[1])[0, :32].reshape(2, SSD_HEADS))
            add_grad('ssd_a_log', j, (dalog_t[0] + dalog_t[1])[0, :32].reshape(2, SSD_HEADS))
            scw, scb = W['ssd_conv_w'][j], W['ssd_conv_b'][j][None]
            lcw, lcb = W['lru_conv_w'][j], W['lru_conv_b'][j][None]

            def conv_bwd(name, colblk0, w, b, wblk0, ncols, act, parts):
                return tile_bwd(
                    name, functools.partial(f_conv1d, lc=Lc, act=act), (ncols // 256, B),
                    [(proj, (None, Tt, 256), lambda cb, bi: (bi, 0, colblk0 + cb)),
                     (w, (4, 256), lambda cb, bi: (0, wblk0 + cb)), (b, (1, 256), lambda cb, bi: (0, wblk0 + cb))],
                    [[(p, (None, Tt, 256), lambda cb, bi: (bi, 0, cb)) for p in parts]],
                    [(0, (B, Tt, ncols), F32, (None, Tt, 256), lambda cb, bi: (bi, 0, cb), None),
                     (1, (4, ncols), F32, (4, 256), lambda cb, bi: (0, cb), lambda cb, bi: bi == 0),
                     (2, (1, ncols), F32, (1, 256), lambda cb, bi: (0, cb), lambda cb, bi: bi == 0)])
            d_xs_raw, dw_xs, db_xs = conv_bwd(f'conv_xs_bwd{l}', 12, scw, scb, 0, 1024, True, d_xs_parts)
            d_bc_raw, dw_bc, db_bc = conv_bwd(f'conv_bc_bwd{l}', 16, scw, scb, 4, 512, True, d_bc_parts)
            d_u_raw, dw_u, db_u = conv_bwd(f'conv_u_bwd{l}', 8, lcw, lcb, 0, 1024, False, d_u_parts)
            add_grad('ssd_conv_w', j, jnp.concatenate([dw_xs, dw_bc], axis=1))
            add_grad('ssd_conv_b', j, jnp.concatenate([db_xs, db_bc], axis=1)[0])
            add_grad('lru_conv_w', j, dw_u)
            add_grad('lru_conv_b', j, db_u[0])
            def f_ev_dproj(z_, gy_, u_, xs_, bc_, t0, t1):
                pad = jnp.zeros((z_.shape[0], EV_PAD - 4736), F32)
                return jnp.concatenate([z_, gy_, u_, xs_, bc_, t0 + t1, pad], axis=1)
            dproj = tile_fwd(f'ev_dproj{l}', f_ev_dproj, (B, nt),
                             [tok(d_z), tok(d_gy), tok(d_u_raw), tok(d_xs_raw), tok(d_bc_raw), tok(d_dt_parts[0]), tok(d_dt_parts[1])],
                             [tok_out(EV_PAD, BF16)])[0].reshape(M, EV_PAD)
            d_h1 = mm(f'ev_proj_dx{l}', [(dproj, win, 0)], tb=True).reshape(B, Tt, D)
            dwp = mm(f'ev_proj_dw{l}', [(h1m, dproj, 0)], ta=True)
            add_grad('ev_w_in', j, jnp.concatenate([dwp[:, 0:1024], dwp[:, 3072:4640], dwp[:, 1024:3072]], axis=1))
        else:
            wout, win = W['od_w_out'][j], W['od_w_in'][j]
            d_mix = mm(f'mix_out_dx{l}', [(d_o1m, wout, 0)], tb=True).reshape(B, Tt, D)
            add_grad('od_w_out', j, mm(f'mix_out_dw{l}', [(r['mix'].reshape(M, D), d_o1m, 0)], ta=True))
            hn, sd, gw, gb = r['fin_par']
            d_o, d_g, d_yv, d_u_fin, dhn, dsd, dgw, dgb = tile_bwd(
                f'oddfin_bwd{l}', f_oddfin, (B, nt),
                [tok(r['o'][0]), tok(r['o'][1]), tok(proj, HG_W, 4), tok(r['yd'][0]), tok(r['yd'][1]), tok(proj, S5_W, 15),
                 vec(hn), vec(sd), vec(gw), vec(gb)],
                [[tok(d_mix)]],
                [(0,) + tok_out(HG_W) + (None,), (2,) + tok_out(HG_W) + (None,), (3,) + tok_out(S5_W) + (None,),
                 (5,) + tok_out(S5_W) + (None,), (6,) + vec_acc((8, 128)), (7,) + vec_acc((1, S5_W)), (8,) + vec_acc((S5_W, S5_W)),
                 (9,) + vec_acc((1, S5_W))])
            add_grad('hg_norm_g', j, dhn[:HG_HEADS])
            add_grad('s5_d', j, dsd[0])
            add_grad('s5_glu_w', j, dgw)
            add_grad('s5_glu_b', j, dgb[0])
            lbv = lb1 if l == 1 else lb3
            ns = nc
            dq, df, dv, du_s5 = [], [], [], []
            d_ym = d_yv.reshape(M, S5_W)
            dbt_re, dbt_im = [], []
            for d in range(2):
                ph0 = phys_chunk(ns, ncc, d == 1)

                def ph(s, ph0=ph0):
                    return ph0(ns - 1 - s)
                dq_d, df_d, dv_d, dlb_d = scan_bwd(
                    f'hgrn_bwd{l}_{d}', functools.partial(f_hgrn_group, reverse=(d == 1), chunk=HG_BWD[0]), (bgr, B, ns),
                    [(proj, (None, 128, bgw), lambda h, b, s, ph=ph: (b, ph(s), h)),
                     (proj, (None, 128, bgw), lambda h, b, s, ph=ph, d=d: (b, ph(s), (1 + d) * bgr + h)),
                     (proj, (None, 128, bgw), lambda h, b, s, ph=ph: (b, ph(s), 3 * bgr + h)),
                     (lbv, (1, bgw), lambda h, b, s: (0, h))],
                    (r['zst'][d].reshape(B, ns, bgr, bgw, 128), (None, None, None, bgw, 128), lambda h, b, s: (b, ns - 1 - s, h, 0, 0)),
                    (d_o, (None, 128, bgw), lambda h, b, s, ph=ph: (b, ph(s), h)),
                    [(0, (B, Tt, HG_W), F32, (None, 128, bgw), lambda h, b, s, ph=ph: (b, ph(s), h), None),
                     (1, (B, Tt, HG_W), F32, (None, 128, bgw), lambda h, b, s, ph=ph: (b, ph(s), h), None),
                     (2, (B, Tt, HG_W), F32, (None, 128, bgw), lambda h, b, s, ph=ph: (b, ph(s), h), None),
                     (3, (1, HG_W), F32, (1, bgw), lambda h, b, s: (0, h), lambda h, b, s: (b == 0) & (s == 0))],
                    (bgw, 128), lambda h, b, s: s == 0)
                dq.append(dq_d)
                df.append(df_d)
                dv.append(dv_d)
                dlb[l].append(dlb_d)
                d_hcat = mm(f's5_out_dx{l}_{d}', [(d_ym, r['ccat'][d], 0)], tb=True)
                dccat = mm(f's5_out_dw{l}_{d}', [(r['hcat'][d], d_ym, 0)], ta=True)
                add_grad('s5_c_re', (j, d), _unblockdiag_t(dccat[:D]))
                add_grad('s5_c_im', (j, d), -_unblockdiag_t(dccat[D:]))
                g5, dcoef8 = clinrec(f's5_bwd{l}_{d}', d_hcat.reshape(B, Tt, 2 * D), r['coef'][d], 'Fb' if d == 0 else 'Rb',
                                     Lc, conj=True, hprev=r['hp5'][d])
                dcoef = jnp.sum(dcoef8, axis=2)
                gcat = g5.reshape(M, 2 * D)
                dbcat = mm(f's5_in_dw{l}_{d}', [(r['u2'], gcat, 0)], ta=True)
                du_s5.append(mm(f's5_in_dx{l}_{d}', [(gcat, r['bcat'][d], 0)], tb=True))
                cts5 = [dcoef[0].reshape(16, 64), dcoef[1].reshape(16, 64), _unblockdiag(dbcat[:, :D]), _unblockdiag(dbcat[:, D:])]
                dlre, dlim, dlst, dbtr, dbti = tile_bwd(
                    f's5p_bwd{l}_{d}', f_s5p, (1,), [vec(t) for t in r['s5in'][d]], [[vec(t)] for t in cts5],
                    [(i, t.shape, F32, t.shape, (lambda *ids, n=t.ndim: (0,) * n), None) for i, t in enumerate(r['s5in'][d])])
                add_grad('s5_lam_re', (j, d), dlre)
                add_grad('s5_lam_im', (j, d), dlim)
                add_grad('s5_log_step', (j, d), dlst[:, 0])
                dbt_re.append(dbtr)
                dbt_im.append(dbti)
            add_grad('s5_b_re', j, jnp.swapaxes(dbt_re[0] + dbt_re[1], 1, 2))
            add_grad('s5_b_im', j, jnp.swapaxes(dbt_im[0] + dbt_im[1], 1, 2))
            def f_od_dproj(q0, q1, f0, f1, v0, v1, g_, u0, u1, u2):
                return jnp.concatenate([q0 + q1, f0, f1, v0 + v1, g_, u0 + u1 + u2], axis=1)
            parts = [dq[0], dq[1], df[0], df[1], dv[0], dv[1], d_g, d_u_fin, du_s5[0].reshape(B, Tt, S5_W),
                     du_s5[1].reshape(B, Tt, S5_W)]
            dproj = tile_fwd(f'od_dproj{l}', f_od_dproj, (B, nt), [tok(t) for t in parts],
                             [tok_out(4096, BF16)])[0].reshape(M, 4096)
            d_h1 = mm(f'od_proj_dx{l}', [(dproj, win, 0)], tb=True).reshape(B, Tt, D)
            add_grad('od_w_in', j, mm(f'od_proj_dw{l}', [(h1m, dproj, 0)], ta=True))
        ng = W['norm_mix_g'][l][None]
        if l == 0:
            d_x0, dng, dsh, dsc = tile_bwd(
                'nm0_bwd', lambda xv, g, sh, sc: (xv, f_nm0(xv, g, sh, sc)), (B, nt),
                [tok(r['xin']), vec(ng), modv(l, 0), modv(l, 1)], [[tok(d_xa)], [tok(d_h1)]],
                [(0,) + tok_out(D) + (None,), (1,) + vec_acc((1, D)), (2,) + dmod_spec, (3,) + dmod_spec])
        else:
            d_xp, d_o2, dgate, dng, dsh, dsc = tile_bwd(
                f'nm_mix_bwd{l}', f_nm, (B, nt),
                [tok(r['xin']), tok(r['oin']), modv(l - 1, 5), vec(ng), modv(l, 0), modv(l, 1)],
                [[tok(d_xa)], [tok(d_h1)]],
                [(0,) + tok_out(D) + (None,), (1,) + tok_out(D, BF16) + (None,), (2,) + dmod_spec, (3,) + vec_acc((1, D)),
                 (4,) + dmod_spec, (5,) + dmod_spec])
            dmods[(l - 1, 5)] = dgate
        add_grad('norm_mix_g', l, dng[0])
        dmods[(l, 0)], dmods[(l, 1)] = dsh, dsc

    grad_x = d_x0[:, Lc:, :]

    (dlogits,) = tile_bwd('lb_bwd', f_lb, (1,), [vec(W['hg_lb_logits'])],
                          [[vec(t) for t in dlb[1]], [vec(t) for t in dlb[3]]],
                          [(0, (DEPTH, HG_W), F32, (DEPTH, HG_W), lambda i: (0, 0), None)])
    add_grad('hg_lb_logits', None, dlogits)

    dm = jnp.stack([jnp.stack([dmods[(l, w)] for w in range(N_MOD)]) for l in range(DEPTH)])
    dlat = jnp.transpose(dm[:, :, :, 1, 0, :], (0, 2, 1, 3)).reshape(DEPTH, B, N_MOD * D)
    dctx = jnp.transpose(dm[:, :, :, 0, 0, :], (0, 2, 1, 3)).reshape(DEPTH, B, N_MOD * D)
    dlat = jnp.zeros((DEPTH, 8, N_MOD * D), F32).at[:, :B].set(dlat)
    dctx = jnp.zeros((DEPTH, 8, N_MOD * D), F32).at[:, :B].set(dctx)

    def f_dmod(dl, dc):
        row = lax.broadcasted_iota(jnp.int32, (8, 1), 0)
        dall = dl + jnp.where(row == 4, jnp.sum(dc, axis=0, keepdims=True), 0.0)
        return dall, jnp.sum(dall, axis=0, keepdims=True)

    dmod, dbmod = tile_fwd('mod_bwd', f_dmod, (DEPTH, nmc),
                           [(dlat, (None, 8, 1536), lambda l, n: (l, 0, n)), (dctx, (None, 8, 1536), lambda l, n: (l, 0, n))],
                           [((DEPTH, 8, N_MOD * D), F32, (None, 8, 1536), lambda l, n: (l, 0, n)),
                            ((DEPTH, 1, N_MOD * D), F32, (None, 1, 1536), lambda l, n: (l, 0, n))])
    add_grad('b_mod', None, dbmod[:, 0])
    return loss, grad_x, G, dmod


def mod_weight_grad(name, cc_rows, dmod_rows):
    nl, nr, cols = dmod_rows.shape

    def f(ccv, dv):
        return lax.dot_general(jax.nn.silu(ccv).astype(BF16), dv.astype(BF16), (((0,), (0,)), ((), ())),
                               preferred_element_type=F32)

    return tile_fwd(name, f, (nl,), [(cc_rows, (nr, D), lambda l: (0, 0)), (dmod_rows, (None, nr, cols), lambda l: (l, 0, 0))],
                    [((nl, D, cols), F32, (None, D, cols), lambda l: (l, 0, 0))])[0]


def assemble_grads(G, like):
    out = {}
    for name, parts in G.items():
        shape = like[name].shape
        if None in parts:
            g = parts[None]
        elif isinstance(next(iter(parts)), tuple):
            g = jnp.stack([jnp.stack([parts[(j, d)] for d in range(2)]) for j in range(shape[0])])
        else:
            g = jnp.stack([parts[i] for i in range(shape[0])])
        out[name] = g.reshape(shape)
    return out


XY_RELS = ((1, 0, 0), (0, 1, 0), (1, 1, 0))
ALL_RELS = tuple((dx, dy, dc) for dx in (0, 1) for dy in (0, 1) for dc in (0, 1))[1:]


def exchange(name, src, out_shape, sends):
    return exchange_many(name, [(src, out_shape, sends)])[0]


def exchange_many(name, items):
    na = len(items)
    n = sum(len(sends) for _, _, sends in items)

    def body(*refs):
        src_refs, out_refs, send_sems, recv_sems = refs[:na], refs[na:2 * na], refs[2 * na], refs[2 * na + 1]
        me = (lax.axis_index("x"), lax.axis_index("y"), lax.axis_index("c"))
        copies, k = [], 0
        for (_, _, sends), src_ref, out_ref in zip(items, src_refs, out_refs):
            for rel, ssel, dsel in sends:
                tgt = tuple(1 - m if f else m for m, f in zip(me, rel))
                cp = pltpu.make_async_remote_copy(
                    src_ref=src_ref if ssel is None else src_ref.at[ssel(me, tgt)],
                    dst_ref=out_ref if dsel is None else out_ref.at[dsel(me, tgt)],
                    send_sem=send_sems.at[k], recv_sem=recv_sems.at[k], device_id=tgt, device_id_type=MESH)
                cp.start()
                copies.append(cp)
                k += 1
        for cp in copies:
            cp.wait()

    return pl.pallas_call(
        body, out_shape=[jax.ShapeDtypeStruct(shape, src.dtype) for src, shape, _ in items],
        in_specs=[pl.BlockSpec(memory_space=pl.ANY)] * na, out_specs=[pl.BlockSpec(memory_space=pl.ANY)] * na,
        scratch_shapes=[pltpu.SemaphoreType.DMA((n,)), pltpu.SemaphoreType.DMA((n,))],
        name=name,
    )(*[src for src, _, _ in items])


def _xy_index(dev):
    return 2 * dev[0] + dev[1]


def _my_xy():
    return 2 * lax.axis_index("x") + lax.axis_index("y")


def all_gather_xy(name, shard):
    got = exchange(name, shard, (4,) + shard.shape, [(rel, None, lambda me, tgt: _xy_index(me)) for rel in XY_RELS])
    return lax.dynamic_update_index_in_dim(got, shard, _my_xy(), 0)


def reduce_scatter_xy(name, g4):
    got = exchange(name, g4, (3,) + g4.shape[1:],
                   [(rel, (lambda me, tgt: _xy_index(tgt)), (lambda me, tgt, k=k: k)) for k, rel in enumerate(XY_RELS)])
    return got, lax.dynamic_index_in_dim(g4, _my_xy(), 0, keepdims=False)


def sibling_swap(name, v):
    return exchange(name, v, v.shape, [((0, 0, 1), None, None)])


def all_gather_all(name, v):
    got = exchange(name, v, (8,) + v.shape, [(rel, None, lambda me, tgt: 4 * me[0] + 2 * me[1] + me[2]) for rel in ALL_RELS])
    return lax.dynamic_update_index_in_dim(got, v, 2 * _my_xy() + lax.axis_index("c"), 0)


def all_gather_xy_halves(name, shards):
    na = len(shards)

    def body(*refs):
        src_refs, out_refs, send_sems, recv_sems = refs[:na], refs[na:2 * na], refs[2 * na], refs[2 * na + 1]
        x, y, c = lax.axis_index("x"), lax.axis_index("y"), lax.axis_index("c")
        peers = [(1 - x, y), (x, 1 - y), (1 - x, 1 - y)]

        def copy(k, src, dst, to):
            return pltpu.make_async_remote_copy(src_ref=src, dst_ref=dst, send_sem=send_sems.at[k], recv_sem=recv_sems.at[k],
                                                device_id=to, device_id_type=MESH)

        halves = [pl.ds(c * (s.shape[0] // 2), s.shape[0] // 2) for s in shards]
        first = [[copy(6 * i + k, src_refs[i].at[halves[i]], out_refs[i].at[2 * x + y, halves[i]], (px, py, c))
                  for k, (px, py) in enumerate(peers)] for i in range(na)]
        for row in first:
            for cp in row:
                cp.start()
        passed = []
        for i in range(na):
            for k, (px, py) in enumerate(peers):
                first[i][k].wait_recv()
                landed = out_refs[i].at[2 * px + py, halves[i]]
                fw = copy(6 * i + 3 + k, landed, landed, (x, y, 1 - c))
                fw.start()
                passed.append(fw)
        for fw in passed:
            fw.wait_recv()
        for cp in [cp for row in first for cp in row] + passed:
            cp.wait_send()

    got = pl.pallas_call(
        body, out_shape=[jax.ShapeDtypeStruct((4,) + s.shape, s.dtype) for s in shards],
        in_specs=[pl.BlockSpec(memory_space=pl.ANY)] * na, out_specs=[pl.BlockSpec(memory_space=pl.ANY)] * na,
        scratch_shapes=[pltpu.SemaphoreType.DMA((6 * na,)), pltpu.SemaphoreType.DMA((6 * na,))],
        name=name,
    )(*shards)
    return [lax.dynamic_update_index_in_dim(g, s, _my_xy(), 0) for g, s in zip(got, shards)]


def reduce_scatter_xy_many(name, g4s):
    got = exchange_many(name, [(g4, (3,) + g4.shape[1:],
                                [(rel, (lambda me, tgt: _xy_index(tgt)), (lambda me, tgt, k=k: k)) for k, rel in enumerate(XY_RELS)])
                               for g4 in g4s])
    return [(g, lax.dynamic_index_in_dim(g4, _my_xy(), 0, keepdims=False)) for g, g4 in zip(got, g4s)]


def sibling_split(name, g4s):
    halves = [g4.shape[1] // 2 for g4 in g4s]
    got = exchange_many(name, [(g4, (4, h) + g4.shape[2:], [((0, 0, 1), (lambda me, tgt, h=h: (slice(None), pl.ds(tgt[2] * h, h))), None)])
                               for g4, h in zip(g4s, halves)])
    return [(g, lax.dynamic_slice_in_dim(g4, lax.axis_index("c") * h, h, axis=1)) for g, g4, h in zip(got, g4s, halves)]


def sibling_join(name, qs):
    got = exchange_many(name, [(q, (2 * q.shape[0],) + q.shape[1:], [((0, 0, 1), None, lambda me, tgt, h=q.shape[0]: pl.ds(me[2] * h, h))])
                               for q in qs])
    return [lax.dynamic_update_slice_in_dim(g, q, lax.axis_index("c") * q.shape[0], axis=0) for g, q in zip(got, qs)]


def _rows_view(shape):
    cols = shape[-1] if len(shape) else 1
    rows = 1
    for s in shape[:-1]:
        rows *= s
    return rows, cols


def _row_block(rows, cols, n_arrays):
    budget = (24 * 1024 * 1024) // (8 * n_arrays * cols)
    if rows <= max(budget, 16):
        return rows
    br = (min(budget, rows) // 16) * 16
    while br > 16 and rows % br:
        br -= 16
    return br if rows % br == 0 else rows


def sum_slots(name, stacked, extra=(), out_dtype=F32):
    k = stacked.shape[0]
    rows, cols = _rows_view(stacked.shape[1:])
    br = _row_block(rows, cols, k + len(extra) + 1)

    def f(s, *more):
        parts = [s[i].astype(F32) for i in range(k)] + [m.astype(F32) for m in more]
        while len(parts) > 1:
            parts = [parts[i] + parts[i + 1] for i in range(0, len(parts) - 1, 2)] + ([parts[-1]] if len(parts) % 2 else [])
        return parts[0]

    out = tile_fwd(name, f, (rows // br,),
                   [(stacked.reshape(k, rows, cols), (k, br, cols), lambda i: (0, i, 0))]
                   + [(e.reshape(rows, cols), (br, cols), lambda i: (i, 0)) for e in extra],
                   [((rows, cols), out_dtype, (br, cols), lambda i: (i, 0))])[0]
    return out.reshape(stacked.shape[1:])


def adamw(name, w, m, v, gs):
    rows, cols = _rows_view(w.shape)
    br = _row_block(rows, cols, 7 + len(gs))
    spec = lambda a: (a.reshape(rows, cols), (br, cols), lambda i: (i, 0))
    outs = tile_fwd(name, f_adamw, (rows // br,), [spec(t) for t in (w, m, v) + tuple(gs)],
                    [((rows, cols), F32, (br, cols), lambda i: (i, 0))] * 4)
    return [o.reshape(w.shape) for o in outs]


IN_NAMES = ['x', 'c', 'ctx'] + W_NAMES + ['loss_target'] + ['m_' + n for n in W_NAMES] + ['v_' + n for n in W_NAMES]
SMALL_PAD = 128 * 1024


def kernel(x, c, ctx, c_ctx, w_mod, b_mod, norm_mix_g, norm_ffn_g, final_norm_g, ev_w_in, ev_w_out, ssd_conv_w, ssd_conv_b, ssd_dt_bias, ssd_a_log, ssd_d, ssd_norm_g, lru_conv_w, lru_conv_b, lru_w_a, lru_b_a, lru_w_i, lru_b_i, lru_lam, od_w_in, od_w_out, hg_lb_logits, hg_norm_g, s5_lam_re, s5_lam_im, s5_log_step, s5_b_re, s5_b_im, s5_c_re, s5_c_im, s5_d, s5_glu_w, s5_glu_b, ffn_w_gate, ffn_w_up, ffn_conv_w, ffn_conv_b, ffn_w_down, loss_target, m_c_ctx, m_w_mod, m_b_mod, m_norm_mix_g, m_norm_ffn_g, m_final_norm_g, m_ev_w_in, m_ev_w_out, m_ssd_conv_w, m_ssd_conv_b, m_ssd_dt_bias, m_ssd_a_log, m_ssd_d, m_ssd_norm_g, m_lru_conv_w, m_lru_conv_b, m_lru_w_a, m_lru_b_a, m_lru_w_i, m_lru_b_i, m_lru_lam, m_od_w_in, m_od_w_out, m_hg_lb_logits, m_hg_norm_g, m_s5_lam_re, m_s5_lam_im, m_s5_log_step, m_s5_b_re, m_s5_b_im, m_s5_c_re, m_s5_c_im, m_s5_d, m_s5_glu_w, m_s5_glu_b, m_ffn_w_gate, m_ffn_w_up, m_ffn_conv_w, m_ffn_conv_b, m_ffn_w_down, v_c_ctx, v_w_mod, v_b_mod, v_norm_mix_g, v_norm_ffn_g, v_final_norm_g, v_ev_w_in, v_ev_w_out, v_ssd_conv_w, v_ssd_conv_b, v_ssd_dt_bias, v_ssd_a_log, v_ssd_d, v_ssd_norm_g, v_lru_conv_w, v_lru_conv_b, v_lru_w_a, v_lru_b_a, v_lru_w_i, v_lru_b_i, v_lru_lam, v_od_w_in, v_od_w_out, v_hg_lb_logits, v_hg_norm_g, v_s5_lam_re, v_s5_lam_im, v_s5_log_step, v_s5_b_re, v_s5_b_im, v_s5_c_re, v_s5_c_im, v_s5_d, v_s5_glu_w, v_s5_glu_b, v_ffn_w_gate, v_ffn_w_up, v_ffn_conv_w, v_ffn_conv_b, v_ffn_w_down):
    a = dict(locals())
    big = [n for n in W_NAMES if n in MATMUL_WEIGHTS and n != 'w_mod']
    minor = [n for n in W_NAMES if n in SHARD_AXIS and n not in MATMUL_WEIGHTS]

    def pack(arrays, lead=()):
        flat = jnp.concatenate([t.reshape(lead + (-1,)) for t in arrays], axis=len(lead))
        pad = -flat.shape[-1] % 1024
        flat = jnp.concatenate([flat, jnp.zeros(lead + (pad,), flat.dtype)], axis=len(lead))
        return flat.reshape(lead + (-1, 128))

    def unpack(packed, names, lead=()):
        flat, out, off = packed.reshape(lead + (-1,)), {}, 0
        for n in names:
            size = math.prod(a[n].shape)
            out[n] = flat[..., off:off + size].reshape(lead + a[n].shape)
            off += size
        return out

    gathered = dict(zip(big, all_gather_xy_halves('ag_big', [a[n].astype(BF16) for n in big])))
    gathered.update(unpack(all_gather_xy('ag_minor', pack([a[n] for n in minor])), minor, (4,)))
    W = {}
    for n in W_NAMES:
        w = a[n]
        if n in gathered:
            ax = SHARD_AXIS[n]
            shape = list(w.shape)
            shape[ax] *= 4
            W[n] = jnp.moveaxis(gathered[n], 0, ax).reshape(shape)
        elif n not in SHARD_AXIS:
            W[n] = w
    e = W['ev_w_in']
    W['ev_w_in_p'] = jnp.concatenate(
        [e[:, :, 0:1024], e[:, :, 2592:3616], e[:, :, 3616:4640], e[:, :, 1024:2560], e[:, :, 2560:2592],
         jnp.zeros((e.shape[0], D, EV_PAD - 4640), e.dtype)], axis=2)

    my_dev = 2 * _my_xy() + lax.axis_index("c")
    mcols = a['w_mod'].shape[2]
    w_mod_mine = a['w_mod'].astype(BF16)
    cc_all = all_gather_all('ag_c', cond_rows(a['c'], a['c_ctx'])).reshape(8 * 8, D)
    mod_cols = mod_table('mod_fwd', cc_all, w_mod_mine, lax.dynamic_slice_in_dim(a['b_mod'], _my_xy() * mcols, mcols, axis=1))
    rows_of = lambda dev: pl.ds((4 * dev[0] + 2 * dev[1] + dev[2]) * 8, 8)
    got = exchange('ag_mod', mod_cols, (4, DEPTH, 8, mcols),
                   [(rel, (lambda me, tgt: (slice(None), rows_of(tgt))), (lambda me, tgt: _xy_index(me))) for rel in XY_RELS])
    got = lax.dynamic_update_index_in_dim(got, lax.dynamic_slice_in_dim(mod_cols, my_dev * 8, 8, axis=1), _my_xy(), 0)
    mod = jnp.moveaxis(got, 0, 2).reshape(DEPTH, 8, N_MOD * D)

    loss_local, grad_x, G, dmod = local_step(a['x'], a['ctx'], a['loss_target'], mod, W)
    grads = assemble_grads(G, W)
    loss = lax.psum(loss_local, ("x", "y", "c"))

    res = {}
    cols_of = lambda dev: pl.ds(_xy_index(dev) * mcols, mcols)
    got = exchange('ag_dmod', dmod, (8, DEPTH, 8, mcols),
                   [(rel, (lambda me, tgt: (slice(None), slice(None), cols_of(tgt))), (lambda me, tgt: 4 * me[0] + 2 * me[1] + me[2]))
                    for rel in ALL_RELS])
    got = lax.dynamic_update_index_in_dim(got, lax.dynamic_slice_in_dim(dmod, _my_xy() * mcols, mcols, axis=2), my_dev, 0)
    dm_mine = jnp.moveaxis(got, 0, 1).reshape(DEPTH, 8 * 8, mcols)
    res['w_mod'] = adamw('adamw_w_mod', a['w_mod'], a['m_w_mod'], a['v_w_mod'], (mod_weight_grad('mod_dw', cc_all, dm_mine),))
    dcc = mod_cond_grad('mod_dcc', cc_all, dm_mine, w_mod_mine).reshape(8, 8, D)
    grads['c_ctx'] = 0.5 * jnp.sum(dcc[:, 4], axis=0)
    g4 = {}
    for n in big + minor:
        ax = SHARD_AXIS[n]
        gf = grads[n]
        g4[n] = jnp.moveaxis(gf.reshape(gf.shape[:ax] + (4, a[n].shape[ax]) + gf.shape[ax + 1:]), ax, 0)
    parts = [sum_slots('csum_' + n, theirs[None], (ours,), BF16)
             for n, (theirs, ours) in zip(big, sibling_split('rsc_big', [g4[n] for n in big]))]
    halves = [sum_slots('gsum_' + n, got, (own,)) for n, (got, own) in zip(big, reduce_scatter_xy_many('rs_big', parts))]
    for n, g in zip(big, sibling_join('agc_big', halves)):
        res[n] = adamw('adamw_' + n, a[n], a['m_' + n], a['v_' + n], (g,))
    got, own = reduce_scatter_xy('rs_minor', pack([g4[n] for n in minor], (4,)))
    mine = sum_slots('gsum_minor', got, (own,))
    mine_n, other_n = unpack(mine, minor), unpack(sibling_swap('sw_minor', mine), minor)
    for n in minor:
        res[n] = adamw('adamw_' + n, a[n], a['m_' + n], a['v_' + n], (mine_n[n], other_n[n]))
    small = [n for n in W_NAMES if n not in SHARD_AXIS]
    flat = jnp.concatenate([grads[n].reshape(-1) for n in small])
    total = flat.shape[0]
    padded = -(-total // SMALL_PAD) * SMALL_PAD
    flat = jnp.concatenate([flat, jnp.zeros((padded - total,), F32)]).reshape(padded // 128, 128)
    pair = sum_slots('csum_small', flat[None], (sibling_swap('sw_small', flat),))
    summed = sum_slots('gsum_small', all_gather_xy_halves('ag_small', [pair])[0]).reshape(-1)
    off = 0
    for n in small:
        size = math.prod(a[n].shape)
        g = summed[off:off + size].reshape(a[n].shape)
        off += size
        res[n] = adamw('adamw_' + n, a[n], a['m_' + n], a['v_' + n], (g,))
    outs = [loss, grad_x]
    for k in range(4):
        outs += [res[n][k] for n in W_NAMES]
    return tuple(outs)
```

```python
import functools
import math

import jax
import jax.numpy as jnp
from jax import lax
from jax.experimental import pallas as pl
from jax.experimental.pallas import tpu as pltpu

F32 = jnp.float32
BF16 = jnp.bfloat16
HI = lax.Precision.HIGHEST
MESH = pl.DeviceIdType.MESH

D = 1024
DEPTH = 4
N_MOD = 6
RMS_EPS = 1e-6
GRID_W = 64
SSD_HEADS = 16
SSD_CHUNK = 128
HG_W = 768
HG_HEADS = 6
HG_FWD = (32, 6)
HG_BWD = (16, 3)
S5_W = 256
D_FF = 2816
EV_PAD = 5120
LRU_C = 8.0
V7X_VMEM_LIMIT = 56 * 1024 * 1024
MM_VMEM_BUDGET = 36 * 1024 * 1024

ADAM_LR, ADAM_B1, ADAM_B2, ADAM_EPS, ADAM_WD, ADAM_STEP = 0.001, 0.9, 0.999, 1e-08, 0.01, 10

W_NAMES = ['c_ctx', 'w_mod', 'b_mod', 'norm_mix_g', 'norm_ffn_g', 'final_norm_g', 'ev_w_in', 'ev_w_out', 'ssd_conv_w',
           'ssd_conv_b', 'ssd_dt_bias', 'ssd_a_log', 'ssd_d', 'ssd_norm_g', 'lru_conv_w', 'lru_conv_b', 'lru_w_a', 'lru_b_a',
           'lru_w_i', 'lru_b_i', 'lru_lam', 'od_w_in', 'od_w_out', 'hg_lb_logits', 'hg_norm_g', 's5_lam_re', 's5_lam_im',
           's5_log_step', 's5_b_re', 's5_b_im', 's5_c_re', 's5_c_im', 's5_d', 's5_glu_w', 's5_glu_b', 'ffn_w_gate', 'ffn_w_up',
           'ffn_conv_w', 'ffn_conv_b', 'ffn_w_down']
SHARD_AXIS = {'w_mod': 2, 'ev_w_in': 2, 'ev_w_out': 1, 'ssd_conv_w': 2, 'lru_conv_w': 2, 'lru_b_a': 2, 'lru_b_i': 2,
              'lru_lam': 2, 'od_w_in': 2, 'od_w_out': 1, 's5_d': 1, 's5_glu_w': 1, 's5_glu_b': 1, 'ffn_w_gate': 2,
              'ffn_w_up': 2, 'ffn_conv_w': 3, 'ffn_w_down': 1}
MATMUL_WEIGHTS = ('w_mod', 'ev_w_in', 'ev_w_out', 'od_w_in', 'od_w_out', 'ffn_w_gate', 'ffn_w_up', 'ffn_w_down')


def _cparams(sem=None):
    return pltpu.CompilerParams(vmem_limit_bytes=V7X_VMEM_LIMIT, dimension_semantics=sem)


def _pick(n, cands):
    for c in cands:
        if n % c == 0:
            return c
    return n


def tile_fwd(name, f, grid, ins, outs):
    n_in = len(ins)

    def body(*refs):
        res = f(*[r[...] for r in refs[:n_in]])
        if not isinstance(res, (tuple, list)):
            res = (res,)
        for r, o in zip(res, refs[n_in:]):
            o[...] = r.astype(o.dtype)

    res = pl.pallas_call(
        body, grid=grid,
        in_specs=[pl.BlockSpec(b, m) for _, b, m in ins],
        out_specs=[pl.BlockSpec(b, m) for _, _, b, m in outs],
        out_shape=[jax.ShapeDtypeStruct(s, d) for s, d, _, _ in outs],
        name=name, compiler_params=_cparams(("arbitrary",) * len(grid)),
    )(*[a for a, _, _ in ins])
    return res


def tile_bwd(name, f, grid, ins, cts, grads, prims=()):
    n_in = len(ins)
    ct_flat = [p for c in cts for p in c]
    n_ct = len(ct_flat)
    didx = [g[0] for g in grads]

    def body(*refs):
        in_refs, ct_refs = refs[:n_in], refs[n_in:n_in + n_ct]
        g_refs = refs[n_in + n_ct:n_in + n_ct + len(grads)]
        p_refs = refs[n_in + n_ct + len(grads):]
        vals = [r[...] for r in in_refs]

        def fd(*dv):
            full = list(vals)
            for i, v in zip(didx, dv):
                full[i] = v
            res = f(*full)
            return tuple(res) if isinstance(res, (tuple, list)) else (res,)

        out, vjp = jax.vjp(fd, *[vals[i] for i in didx])
        ctv, k = [], 0
        for o, c in zip(out, cts):
            acc = None
            for _ in c:
                piece = ct_refs[k][...].astype(o.dtype)
                acc = piece if acc is None else acc + piece
                k += 1
            ctv.append(jnp.zeros_like(o) if acc is None else acc.reshape(o.shape))
        gs = vjp(tuple(ctv))
        ids = [pl.program_id(a) for a in range(len(grid))]

        def emit(ref, val, first):
            if first is None:
                ref[...] = val.astype(ref.dtype)
            else:
                is_first = first(*ids)

                @pl.when(is_first)
                def _():
                    ref[...] = val.astype(ref.dtype)

                @pl.when(jnp.logical_not(is_first))
                def _():
                    ref[...] += val.astype(ref.dtype)

        for g, spec, ref in zip(gs, grads, g_refs):
            emit(ref, g, spec[5])
        for spec, ref in zip(prims, p_refs):
            emit(ref, out[spec[0]], spec[5])

    specs = list(grads) + list(prims)
    res = pl.pallas_call(
        body, grid=grid,
        in_specs=[pl.BlockSpec(b, m) for _, b, m in list(ins) + ct_flat],
        out_specs=[pl.BlockSpec(s[3], s[4]) for s in specs],
        out_shape=[jax.ShapeDtypeStruct(s[1], s[2]) for s in specs],
        name=name, compiler_params=_cparams(("arbitrary",) * len(grid)),
    )(*[a for a, _, _ in list(ins) + ct_flat])
    return res


def mm(name, pairs, ta=False, tb=False, out_dtype=F32):
    a0, b0, _ = pairs[0]
    m = a0.shape[1] if ta else a0.shape[0]
    n = b0.shape[0] if tb else b0.shape[1]
    cands = (1024, 1408, 768, 512, 256, 128)
    tks, nks = [], []
    for a, b, _ in pairs:
        k = a.shape[0] if ta else a.shape[1]
        tk = _pick(k, cands)
        tks.append(tk)
        nks.append(k // tk)

    def vmem_bytes(tm, tn):
        tiles = sum(2 * tk * (tm * a.dtype.itemsize + tn * b.dtype.itemsize) for (a, b, _), tk in zip(pairs, tks))
        return tiles + tm * tn * (4 + 2 * jnp.dtype(out_dtype).itemsize)

    tm_c = [c_ for c_ in cands if m % c_ == 0] or [m]
    tn_c = [c_ for c_ in cands if n % c_ == 0] or [n]
    tm, tn = tm_c[0], tn_c[0]
    while vmem_bytes(tm, tn) > MM_VMEM_BUDGET and (len(tm_c) > 1 or len(tn_c) > 1):
        if len(tm_c) > 1 and (tm >= tn or len(tn_c) == 1):
            tm_c = tm_c[1:]
        else:
            tn_c = tn_c[1:]
        tm, tn = tm_c[0], tn_c[0]
    starts = [sum(nks[:p]) for p in range(len(pairs))]
    nk = sum(nks)
    np_ = len(pairs)

    def body(*refs):
        o_ref, acc = refs[2 * np_], refs[2 * np_ + 1]
        kk = pl.program_id(2)

        @pl.when(kk == 0)
        def _():
            acc[...] = jnp.zeros_like(acc)

        for p in range(np_):
            def add(p=p):
                a = refs[2 * p][...].astype(BF16)
                b = refs[2 * p + 1][...].astype(BF16)
                dn = (((0 if ta else 1,), (1 if tb else 0,)), ((), ()))
                acc[...] += lax.dot_general(a, b, dn, preferred_element_type=F32)
            if np_ == 1:
                add()
            else:
                pl.when((kk >= starts[p]) & (kk < starts[p] + nks[p]))(add)

        @pl.when(kk == nk - 1)
        def _():
            o_ref[...] = acc[...].astype(o_ref.dtype)

    in_specs, args = [], []
    for p, (a, b, off) in enumerate(pairs):
        tk, s0, nkp = tks[p], starts[p], nks[p]
        assert off % tk == 0
        boff = off // tk

        def kloc(k, s0=s0, nkp=nkp):
            return jnp.clip(k - s0, 0, nkp - 1)
        if ta:
            in_specs.append(pl.BlockSpec((tk, tm), lambda i, j, k, kloc=kloc: (kloc(k), i)))
        else:
            in_specs.append(pl.BlockSpec((tm, tk), lambda i, j, k, kloc=kloc: (i, kloc(k))))
        if tb:
            in_specs.append(pl.BlockSpec((tn, tk), lambda i, j, k, kloc=kloc, boff=boff: (j, boff + kloc(k))))
        else:
            in_specs.append(pl.BlockSpec((tk, tn), lambda i, j, k, kloc=kloc, boff=boff: (boff + kloc(k), j)))
        args += [a, b]
    return pl.pallas_call(
        body, grid=(m // tm, n // tn, nk), in_specs=in_specs,
        out_specs=pl.BlockSpec((tm, tn), lambda i, j, k: (i, j)),
        out_shape=jax.ShapeDtypeStruct((m, n), out_dtype),
        scratch_shapes=[pltpu.VMEM((tm, tn), F32)],
        name=name, compiler_params=_cparams(("arbitrary", "arbitrary", "arbitrary")),
    )(*args)


def _rms(x, g):
    return x * lax.rsqrt(jnp.mean(x * x, axis=-1, keepdims=True) + RMS_EPS) * g


def f_nm0(x, g, sh, sc):
    return _rms(x, g) * (1.0 + sc) + sh


def f_nm(xp, o, gate, g, sh, sc):
    x = xp + gate * o
    return x, _rms(x, g) * (1.0 + sc) + sh


def f_final(xp, o, gate, g, tgt, valid):
    x = xp + gate * o
    e = (_rms(x, g) - tgt) * valid
    return jnp.sum(e * e, axis=0, keepdims=True) * (0.5 / D)


@functools.partial(jax.custom_vjp, nondiff_argnums=(1,))
def _sroll(x, s):
    return pltpu.roll(x, s, 0)


def _sroll_fwd(x, s):
    return pltpu.roll(x, s, 0), None


def _sroll_bwd(s, _, g):
    return (pltpu.roll(g, (g.shape[0] - s) % g.shape[0], 0),)


_sroll.defvjp(_sroll_fwd, _sroll_bwd)


def _shifted(x, o):
    n = x.shape[0]
    return x if o == 0 else _sroll(x, (n - o) % n)


def f_conv1d(x, w, b, *, lc, act):
    n = x.shape[0]
    pos = lax.broadcasted_iota(jnp.int32, (n, 1), 0)
    lo = jnp.where(pos < lc, 0, lc)
    hi = jnp.where(pos < lc, lc, n)
    y = x * w[1:2] + b
    for k, o in ((0, -1), (2, 1), (3, 2)):
        src = pos + o
        valid = (src >= lo) & (src < hi)
        y = y + jnp.where(valid, _shifted(x, o), 0.0) * w[k:k + 1]
    return jax.nn.silu(y) if act else y


def ffnconv_masks(n, lc):
    pos = lax.broadcasted_iota(jnp.int32, (n, 128), 0)
    is_ctx = pos < lc
    tl = pos - lc
    r = tl // GRID_W
    cc = tl - r * GRID_W
    rows = (n - lc) // GRID_W
    left = jnp.where(is_ctx, pos >= 1, cc >= 1)
    right = jnp.where(is_ctx, pos < lc - 1, cc < GRID_W - 1)
    above = jnp.logical_not(is_ctx) & (r >= 1)
    below = jnp.logical_not(is_ctx) & (r < rows - 1)
    return jnp.stack([left, right, above, below]).astype(F32)


def f_ffnconv(a, up, w, b, mk):
    cols = (mk[0] * _shifted(a, -1), a, mk[1] * _shifted(a, 1))
    y = b
    for dr in (-1, 0, 1):
        k = 3 * (dr + 1)
        inner = cols[0] * w[k:k + 1] + cols[1] * w[k + 1:k + 2] + cols[2] * w[k + 2:k + 3]
        y = y + (inner if dr == 0 else mk[2 + (dr > 0)] * _shifted(inner, GRID_W * dr))
    return jax.nn.silu(y) * up


def f_ssd(xs, bc, dtraw, bias, alog, st, *, d, reverse):
    L = xs.shape[0]
    dtv = jax.nn.softplus(dtraw + bias)
    la = dtv * (-jnp.exp(alog))
    ri = lax.broadcasted_iota(jnp.int32, (L, L), 0)
    ci = lax.broadcasted_iota(jnp.int32, (L, L), 1)
    mask = (ci >= ri) if reverse else (ci <= ri)
    cum = jnp.dot(mask.astype(F32), la, precision=HI, preferred_element_type=F32)
    cum_t = cum.T
    tot = cum[0:1] if reverse else cum[L - 1:L]
    lo = lax.broadcasted_iota(jnp.int32, (1, 128), 1) < 64
    rlo = lax.broadcasted_iota(jnp.int32, (128, 1), 0) < 64
    ys, new = [], []
    for g in range(2):
        bg = bc[:, g * 128:(g + 1) * 128].astype(BF16)
        cg = bc[:, 256 + g * 128:256 + (g + 1) * 128].astype(BF16)
        cb = lax.dot_general(cg, bg, (((1,), (1,)), ((), ())), preferred_element_type=F32)
        sg = st[4 * g:4 * g + 4].reshape(4 * 128, 128)
        ch_all = lax.dot_general(cg, sg.astype(BF16), (((1,), (1,)), ((), ())), preferred_element_type=F32)
        xes, dcols = [], []
        for jj in range(4):
            j = 4 * g + jj
            x = xs[:, j * 128:(j + 1) * 128]
            k1 = 16 * d + 2 * j
            k2 = k1 + 1
            c1, c2 = cum[:, k1:k1 + 1], cum[:, k2:k2 + 1]
            m1 = cb * jnp.exp(jnp.where(mask, c1 - cum_t[k1:k1 + 1, :], -1e30))
            m2 = cb * jnp.exp(jnp.where(mask, c2 - cum_t[k2:k2 + 1, :], -1e30))
            xdt = x * jnp.where(lo, dtv[:, k1:k1 + 1], dtv[:, k2:k2 + 1])
            mcat = jnp.concatenate([m1, m2], axis=1).astype(BF16)
            xcat = jnp.concatenate([jnp.where(lo, xdt, 0.0), jnp.where(lo, 0.0, xdt)], axis=0).astype(BF16)
            y = jnp.dot(mcat, xcat, preferred_element_type=F32)
            y = y + ch_all[:, jj * 128:(jj + 1) * 128] * jnp.where(lo, jnp.exp(c1), jnp.exp(c2))
            t1, t2 = tot[:, k1:k1 + 1], tot[:, k2:k2 + 1]
            xes.append((xdt * jnp.where(lo, jnp.exp(t1 - c1), jnp.exp(t2 - c2))).astype(BF16))
            dcols.append(jnp.where(rlo, jnp.exp(t1), jnp.exp(t2)))
            ys.append(y)
        upd = lax.dot_general(jnp.concatenate(xes, axis=1), bg, (((0,), (0,)), ((), ())), preferred_element_type=F32)
        new.append((sg * jnp.concatenate(dcols, axis=0) + upd).reshape(4, 128, 128))
    return jnp.concatenate(ys, axis=1), jnp.concatenate(new, axis=0)


def f_hgrn(q_raw, f_raw, v, lb, zt, *, reverse, chunk):
    n = q_raw.shape[0]
    c = chunk
    qa = jax.nn.silu(q_raw)
    logf = jnp.log(lb + (1.0 - lb) * jax.nn.sigmoid(f_raw))
    kk = (1.0 - lb) * jax.nn.sigmoid(-f_raw)
    ri = lax.broadcasted_iota(jnp.int32, (n, n), 0)
    ci = lax.broadcasted_iota(jnp.int32, (n, n), 1)
    tmat = ((ri // c == ci // c) & ((ci >= ri) if reverse else (ci <= ri))).astype(F32)
    cum_all = jnp.dot(tmat, logf, precision=HI, preferred_element_type=F32)
    r3 = lax.broadcasted_iota(jnp.int32, (c, c, 128), 0)
    c3 = lax.broadcasted_iota(jnp.int32, (c, c, 128), 1)
    mask3 = (c3 >= r3) if reverse else (c3 <= r3)
    nch = n // c
    outs = [None] * nch
    for chn in (reversed(range(nch)) if reverse else range(nch)):
        sl = slice(chn * c, (chn + 1) * c)
        q, k, vv, cum = qa[sl], kk[sl], v[sl], cum_all[sl]
        dec = jnp.exp(jnp.where(mask3, cum[:, None, :] - cum[None, :, :], -1e30))
        att = jnp.sum(q[:, None, :] * dec * k[None, :, :], axis=-1, keepdims=True)
        y = jnp.sum(att * vv[None, :, :], axis=1)
        y = y + lax.dot_general((q * jnp.exp(cum)).astype(BF16), zt.astype(BF16), (((1,), (1,)), ((), ())),
                                preferred_element_type=F32)
        tot = cum[0:1] if reverse else cum[c - 1:c]
        kd = (k * jnp.exp(tot - cum)).astype(BF16)
        zt = zt * jnp.exp(tot) + lax.dot_general(vv.astype(BF16), kd, (((0,), (0,)), ((), ())), preferred_element_type=F32)
        outs[chn] = y
    return jnp.concatenate(outs, axis=0), zt


def f_hgrn_group(q_raw, f_raw, v, lb, zt, *, reverse, chunk):
    ys, zs = [], []
    for h in range(q_raw.shape[1] // 128):
        sl = slice(h * 128, (h + 1) * 128)
        y, z = f_hgrn(q_raw[:, sl], f_raw[:, sl], v[:, sl], lb[:, sl], zt[sl], reverse=reverse, chunk=chunk)
        ys.append(y)
        zs.append(z)
    return jnp.concatenate(ys, axis=1), jnp.concatenate(zs, axis=0)


def _expm1(x):
    poly = x * (1.0 + x * (0.5 + x * (1.0 / 6 + x * (1.0 / 24))))
    return jnp.where(jnp.abs(x) < 0.1, poly, jnp.exp(x) - 1.0)


def f_gates(u, wa, ba, wi, bi, lam):
    rs, is_ = [], []
    for nb in range(8):
        un = u[:, nb * 128:(nb + 1) * 128].astype(BF16)
        rs.append(jnp.dot(un, wa[nb].astype(BF16), preferred_element_type=F32))
        is_.append(jnp.dot(un, wi[nb].astype(BF16), preferred_element_type=F32))
    r = jax.nn.sigmoid(jnp.concatenate(rs, axis=1) + ba)
    i = jax.nn.sigmoid(jnp.concatenate(is_, axis=1) + bi)
    log_a = -LRU_C * jax.nn.softplus(-lam) * r
    return jnp.exp(log_a), jnp.sqrt(-_expm1(2.0 * log_a)) * (i * u)


def f_ssdfin(y0, y1, xs, z, gy, h0, h1, dpad, ng):
    kk = lax.broadcasted_iota(jnp.int32, (128, D), 0)
    ch = lax.broadcasted_iota(jnp.int32, (128, D), 1)
    expand = (ch // 64 == kk).astype(F32)
    dvec = jnp.dot(dpad, expand, precision=HI, preferred_element_type=F32)[0:1]
    y = y0 + y1 + dvec * xs
    yn = _rms(y * jax.nn.silu(z), ng)
    r = (h0 + h1) * jax.nn.gelu(gy)
    return jnp.concatenate([yn, r], axis=1)


def f_oddfin(o0, o1, g, y0, y1, u, hn, sd, gw, gb):
    parts = []
    for h in range(HG_HEADS):
        sl = slice(h * 128, (h + 1) * 128)
        parts.append(_rms(o0[:, sl] + o1[:, sl], hn[h:h + 1]) * jax.nn.silu(g[:, sl]))
    y = jax.nn.gelu(y0 + y1 + sd * u)
    y = y * jax.nn.sigmoid(jnp.dot(y.astype(BF16), gw.astype(BF16), preferred_element_type=F32) + gb)
    return jnp.concatenate(parts + [y], axis=1)


def f_s5p(lre, lim, lstep, btr, bti):
    step = jnp.exp(lstep)
    mag = jnp.exp(lre * step)
    ar, ai = mag * jnp.cos(lim * step), mag * jnp.sin(lim * step)
    den = lre * lre + lim * lim
    zr = ((ar - 1.0) * lre + ai * lim) / den
    zi = (ai * lre - (ar - 1.0) * lim) / den
    bbr = zr[:, None, :] * btr - zi[:, None, :] * bti
    bbi = zr[:, None, :] * bti + zi[:, None, :] * btr
    return ar, ai, bbr, bbi


def f_lb(logits):
    m = jnp.max(logits, axis=0, keepdims=True)
    e = jnp.exp(logits - m)
    p = e / jnp.sum(e, axis=0, keepdims=True)
    return p[1:2], p[1:2] + p[2:3] + p[3:4]


def f_adamw(w, m, v, *gs):
    g = gs[0]
    for t in gs[1:]:
        g = g + t
    m = ADAM_B1 * m + (1.0 - ADAM_B1) * g
    v = ADAM_B2 * v + (1.0 - ADAM_B2) * jnp.square(g)
    m_hat = m / (1.0 - ADAM_B1 ** ADAM_STEP)
    v_hat = v / (1.0 - ADAM_B2 ** ADAM_STEP)
    delta = -ADAM_LR * (m_hat / (jnp.sqrt(v_hat) + ADAM_EPS) + ADAM_WD * w)
    return g, delta, m, v


def scan_fwd(name, f, grid, ins, y_out, st_out, state_shape, is_first):
    n_in = len(ins)

    def body(*refs):
        y_ref, so_ref, st = refs[n_in], refs[n_in + 1], refs[n_in + 2]
        ids = [pl.program_id(a) for a in range(len(grid))]

        @pl.when(is_first(*ids))
        def _():
            st[...] = jnp.zeros_like(st)

        s = st[...]
        so_ref[...] = s
        y, new = f(*[r[...] for r in refs[:n_in]], s)
        y_ref[...] = y.astype(y_ref.dtype)
        st[...] = new

    return pl.pallas_call(
        body, grid=grid,
        in_specs=[pl.BlockSpec(b, m) for _, b, m in ins],
        out_specs=[pl.BlockSpec(y_out[2], y_out[3]), pl.BlockSpec(st_out[2], st_out[3])],
        out_shape=[jax.ShapeDtypeStruct(y_out[0], y_out[1]), jax.ShapeDtypeStruct(st_out[0], st_out[1])],
        scratch_shapes=[pltpu.VMEM(state_shape, F32)],
        name=name, compiler_params=_cparams(("arbitrary",) * len(grid)),
    )(*[a for a, _, _ in ins])


def scan_bwd(name, f, grid, ins, st_in, dy, grads, state_shape, is_first):
    n_in = len(ins)
    didx = [g[0] for g in grads]

    def body(*refs):
        s_ref, dy_ref = refs[n_in], refs[n_in + 1]
        g_refs = refs[n_in + 2:n_in + 2 + len(grads)]
        dst = refs[n_in + 2 + len(grads)]
        ids = [pl.program_id(a) for a in range(len(grid))]

        @pl.when(is_first(*ids))
        def _():
            dst[...] = jnp.zeros_like(dst)

        vals = [r[...] for r in refs[:n_in]]

        def fd(s, *dv):
            full = list(vals)
            for i, v in zip(didx, dv):
                full[i] = v
            return f(*full, s)

        (y, _), vjp = jax.vjp(fd, s_ref[...], *[vals[i] for i in didx])
        gs = vjp((dy_ref[...].astype(y.dtype), dst[...]))
        dst[...] = gs[0]
        for g, spec, ref in zip(gs[1:], grads, g_refs):
            first = spec[5]
            if first is None:
                ref[...] = g.astype(ref.dtype)
            else:
                fst = first(*ids)

                @pl.when(fst)
                def _(ref=ref, g=g):
                    ref[...] = g.astype(ref.dtype)

                @pl.when(jnp.logical_not(fst))
                def _(ref=ref, g=g):
                    ref[...] += g.astype(ref.dtype)

    allin = list(ins) + [st_in, dy]
    return pl.pallas_call(
        body, grid=grid,
        in_specs=[pl.BlockSpec(b, m) for _, b, m in allin],
        out_specs=[pl.BlockSpec(s[3], s[4]) for s in grads],
        out_shape=[jax.ShapeDtypeStruct(s[1], s[2]) for s in grads],
        scratch_shapes=[pltpu.VMEM(state_shape, F32)],
        name=name, compiler_params=_cparams(("arbitrary",) * len(grid)),
    )(*[a for a, _, _ in allin])


def _tile_order(order, nt, nctx=1):
    rev = lambda j: jnp.where(j < nctx, nctx - 1 - j, nt - 1 - (j - nctx))
    if order == 'F':
        return (lambda j: j), True
    if order == 'Fb':
        return (lambda j: nt - 1 - j), False
    if order == 'R':
        return rev, False
    return (lambda j: rev(nt - 1 - j)), True


def _scan8(coef, val, sub, asc):
    for step in (1, 2, 4):
        shift = step if asc else 8 - step
        keep = (sub >= step) if asc else (sub < 8 - step)
        val = jnp.where(keep, coef * pltpu.roll(val, shift, 0) + val, val)
        coef = jnp.where(keep, coef * pltpu.roll(coef, shift, 0), coef)
    return coef, val


def _prev_rows(tile, carry, sub, asc):
    return jnp.where(sub == 0, carry, pltpu.roll(tile, 1, 0)) if asc else jnp.where(sub == 7, carry, pltpu.roll(tile, 7, 0))


def _last_row(tile, asc):
    return jnp.broadcast_to(tile[7:8] if asc else tile[0:1], tile.shape)


def linrec(name, a, b, order):
    bsz, tt, cols = a.shape
    tq = _pick(tt, (256, 128))
    nt, ng, nj = tt // tq, tq // 8, cols // 128
    phys, asc = _tile_order(order, nt)

    def body(a_ref, b_ref, h_ref, hp_ref, hc):
        @pl.when(pl.program_id(0) == 0)
        def _():
            hc[...] = jnp.zeros_like(hc)

        sub = lax.broadcasted_iota(jnp.int32, (8, 128), 0)

        def group(i, carry):
            rows = pl.ds(pl.multiple_of((i if asc else ng - 1 - i) * 8, 8), 8)
            for bi in range(bsz):
                for j in range(nj):
                    cs = slice(j * 128, (j + 1) * 128)
                    h_in = hc[bi, j]
                    ca, cv = _scan8(a_ref[bi, rows, cs], b_ref[bi, rows, cs], sub, asc)
                    h = ca * h_in + cv
                    h_ref[bi, rows, cs] = h
                    hp_ref[bi, rows, cs] = _prev_rows(h, h_in, sub, asc)
                    hc[bi, j] = _last_row(h, asc)
            return carry

        lax.fori_loop(0, ng, group, 0)

    spec = pl.BlockSpec((bsz, tq, cols), lambda j: (0, phys(j), 0))
    return pl.pallas_call(
        body, grid=(nt,), in_specs=[spec, spec], out_specs=[spec, spec],
        out_shape=[jax.ShapeDtypeStruct(a.shape, F32)] * 2,
        scratch_shapes=[pltpu.VMEM((bsz, nj, 8, 128), F32)],
        name=name, compiler_params=_cparams(("arbitrary",)),
    )(a, b)


def linrec_bwd(name, a, dh, hprev, order):
    bsz, tt, cols = a.shape
    tq = _pick(tt, (256, 128))
    nt, ng, nj = tt // tq, tq // 8, cols // 128
    phys, asc = _tile_order(order, nt)

    def body(a_ref, dh_ref, hp_ref, g_ref, ga_ref, gc, ac):
        @pl.when(pl.program_id(0) == 0)
        def _():
            gc[...] = jnp.zeros_like(gc)
            ac[...] = jnp.zeros_like(ac)

        sub = lax.broadcasted_iota(jnp.int32, (8, 128), 0)

        def group(i, carry):
            rows = pl.ds(pl.multiple_of((i if asc else ng - 1 - i) * 8, 8), 8)
            for bi in range(bsz):
                for j in range(nj):
                    cs = slice(j * 128, (j + 1) * 128)
                    a_tile = a_ref[bi, rows, cs]
                    ca, cv = _scan8(_prev_rows(a_tile, ac[bi, j], sub, asc), dh_ref[bi, rows, cs], sub, asc)
                    g = ca * gc[bi, j] + cv
                    g_ref[bi, rows, cs] = g
                    ga_ref[bi, rows, cs] = g * hp_ref[bi, rows, cs]
                    gc[bi, j] = _last_row(g, asc)
                    ac[bi, j] = _last_row(a_tile, asc)
            return carry

        lax.fori_loop(0, ng, group, 0)

    spec = pl.BlockSpec((bsz, tq, cols), lambda j: (0, phys(j), 0))
    return pl.pallas_call(
        body, grid=(nt,), in_specs=[spec, spec, spec], out_specs=[spec, spec],
        out_shape=[jax.ShapeDtypeStruct(a.shape, F32)] * 2,
        scratch_shapes=[pltpu.VMEM((bsz, nj, 8, 128), F32), pltpu.VMEM((bsz, nj, 8, 128), F32)],
        name=name, compiler_params=_cparams(("arbitrary",)),
    )(a, dh, hprev)


def _cmul(a, b):
    return a[0] * b[0] - a[1] * b[1], a[0] * b[1] + a[1] * b[0]


def _cpow_tables(ar, ai, asc):
    pows = [(ar, ai)]
    for _ in range(7):
        pows.append(_cmul(pows[-1], (ar, ai)))
    tile = lambda p: jnp.broadcast_to(p[:, None, :], (8, 8, 128))
    steps = jnp.stack([jnp.stack([tile(pows[s - 1][0]), tile(pows[s - 1][1])]) for s in (1, 2, 4)])
    order = range(8) if asc else range(7, -1, -1)
    carry = jnp.stack([jnp.stack([pows[i][c] for i in order], axis=1) for c in (0, 1)])
    return steps, carry


def _cscan8(xr, xi, st_ref, j, sub, asc):
    for s, step in enumerate((1, 2, 4)):
        shift = step if asc else 8 - step
        keep = (sub >= step) if asc else (sub < 8 - step)
        pr, pi = st_ref[s, 0, j], st_ref[s, 1, j]
        rr, ri = pltpu.roll(xr, shift, 0), pltpu.roll(xi, shift, 0)
        xr, xi = jnp.where(keep, xr + pr * rr - pi * ri, xr), jnp.where(keep, xi + pr * ri + pi * rr, xi)
    return xr, xi


def clinrec(name, x, coef, order, lc, conj=False, hprev=None):
    btot, tt, cols2 = x.shape
    cols = cols2 // 2
    bsz, ngrp = btot, 1
    tq = 128
    nt, ng, nj = tt // tq, tq // 8, cols // 128
    phys, asc = _tile_order(order, nt, lc // tq)
    steps, carry = _cpow_tables(coef[0], -coef[1] if conj else coef[1], asc)
    adjoint = hprev is not None

    def body(*refs):
        if adjoint:
            x_ref, hp_ref, st_ref, cr_ref, h_ref, dc_ref, hc = refs
        else:
            x_ref, st_ref, cr_ref, h_ref, hp_ref, hc = refs

        @pl.when(pl.program_id(1) == 0)
        def _():
            hc[...] = jnp.zeros_like(hc)

        if adjoint:
            @pl.when((pl.program_id(0) == 0) & (pl.program_id(1) == 0))
            def _():
                dc_ref[...] = jnp.zeros_like(dc_ref)

        sub = lax.broadcasted_iota(jnp.int32, (8, 128), 0)

        def group(i, c_):
            rows = pl.ds(pl.multiple_of((i if asc else ng - 1 - i) * 8, 8), 8)
            for bi in range(bsz):
                for j in range(nj):
                    cr, ci = slice(j * 128, (j + 1) * 128), slice(cols + j * 128, cols + (j + 1) * 128)
                    sr, si = _cscan8(x_ref[bi, rows, cr], x_ref[bi, rows, ci], st_ref, j, sub, asc)
                    in_r, in_i = hc[bi, 0, j], hc[bi, 1, j]
                    pr, pi = cr_ref[0, j], cr_ref[1, j]
                    hr = sr + pr * in_r - pi * in_i
                    hi = si + pr * in_i + pi * in_r
                    h_ref[bi, rows, cr] = hr
                    h_ref[bi, rows, ci] = hi
                    if adjoint:
                        qr, qi = hp_ref[bi, rows, cr], hp_ref[bi, rows, ci]
                        dc_ref[0, j] += hr * qr + hi * qi
                        dc_ref[1, j] += hi * qr - hr * qi
                    else:
                        hp_ref[bi, rows, cr] = _prev_rows(hr, in_r, sub, asc)
                        hp_ref[bi, rows, ci] = _prev_rows(hi, in_i, sub, asc)
                    hc[bi, 0, j] = _last_row(hr, asc)
                    hc[bi, 1, j] = _last_row(hi, asc)
            return c_

        lax.fori_loop(0, ng, group, 0)

    spec = pl.BlockSpec((bsz, tq, cols2), lambda g, j: (g, phys(j), 0))
    full = lambda t: pl.BlockSpec(t.shape, lambda g, j, n=t.ndim: (0,) * n)
    dc_shape = (2, nj, 8, 128)
    if adjoint:
        ins, in_specs = (x, hprev, steps, carry), [spec, spec, full(steps), full(carry)]
        out_specs = [spec, pl.BlockSpec(dc_shape, lambda g, j: (0, 0, 0, 0))]
        out_shape = [jax.ShapeDtypeStruct(x.shape, F32), jax.ShapeDtypeStruct(dc_shape, F32)]
    else:
        ins, in_specs = (x, steps, carry), [spec, full(steps), full(carry)]
        out_specs = [spec, spec]
        out_shape = [jax.ShapeDtypeStruct(x.shape, F32)] * 2
    return pl.pallas_call(
        body, grid=(ngrp, nt), in_specs=in_specs, out_specs=out_specs, out_shape=out_shape,
        scratch_shapes=[pltpu.VMEM((bsz, 2, nj, 8, 128), F32)],
        name=name, compiler_params=_cparams(("arbitrary", "arbitrary")),
    )(*ins)


def _blockdiag(bb):
    eye = jnp.eye(16, dtype=bb.dtype)
    return (bb[:, :, None, :] * eye[:, None, :, None]).reshape(256, 1024)


def _blockdiag_t(c):
    eye = jnp.eye(16, dtype=c.dtype)
    return (jnp.swapaxes(c, 1, 2)[:, :, None, :] * eye[:, None, :, None]).reshape(1024, 256)


def _unblockdiag(m):
    eye = jnp.eye(16, dtype=m.dtype)
    return jnp.sum(m.reshape(16, 16, 16, 64) * eye[:, None, :, None], axis=2)


def _unblockdiag_t(m):
    eye = jnp.eye(16, dtype=m.dtype)
    return jnp.swapaxes(jnp.sum(m.reshape(16, 64, 16, 16) * eye[:, None, :, None], axis=2), 1, 2)


def _pad_rows(v, rows=8, cols=128):
    out = jnp.zeros((rows, cols), F32)
    return out.at[0, :v.shape[0]].set(v)


def cond_rows(c, c_ctx):
    return jnp.zeros((8, D), F32).at[:c.shape[0]].set(c).at[4].set(c_ctx)


def mod_table(name, cc_rows, w, b):
    nl, _, cols = w.shape
    nr = cc_rows.shape[0]

    def f(ccv, wv, bv):
        return jnp.dot(jax.nn.silu(ccv).astype(BF16), wv, preferred_element_type=F32) + bv

    return tile_fwd(name, f, (nl, cols // 1536),
                    [(cc_rows, (nr, D), lambda l, n: (0, 0)), (w, (None, D, 1536), lambda l, n: (l, 0, n)),
                     (b.reshape(nl, 1, cols), (None, 1, 1536), lambda l, n: (l, 0, n))],
                    [((nl, nr, cols), F32, (None, nr, 1536), lambda l, n: (l, 0, n))])[0]


def mod_cond_grad(name, cc_rows, dmod_rows, w):
    nl, nr, cols = dmod_rows.shape

    def body(cc_ref, d_ref, w_ref, o_ref):
        _, vjp = jax.vjp(jax.nn.silu, cc_ref[...])
        ds = lax.dot_general(d_ref[...].astype(BF16), w_ref[...], (((1,), (1,)), ((), ())), preferred_element_type=F32)
        (dcc,) = vjp(ds)
        first = (pl.program_id(0) == 0) & (pl.program_id(1) == 0)

        @pl.when(first)
        def _():
            o_ref[...] = dcc

        @pl.when(jnp.logical_not(first))
        def _():
            o_ref[...] += dcc

    return pl.pallas_call(
        body, grid=(nl, cols // 1536),
        in_specs=[pl.BlockSpec((nr, D), lambda l, n: (0, 0)), pl.BlockSpec((None, nr, 1536), lambda l, n: (l, 0, n)),
                  pl.BlockSpec((None, D, 1536), lambda l, n: (l, 0, n))],
        out_specs=pl.BlockSpec((nr, D), lambda l, n: (0, 0)), out_shape=jax.ShapeDtypeStruct((nr, D), F32),
        name=name, compiler_params=_cparams(("arbitrary", "arbitrary")),
    )(cc_rows, dmod_rows, w)


def local_step(x, ctx, target, mod, W):
    B, Tx, _ = x.shape
    Lc = ctx.shape[1]
    Tt = Lc + Tx
    tb = Lc
    nt = Tt // tb
    M = B * Tt
    nc = Tt // SSD_CHUNK
    ncc = Lc // SSD_CHUNK
    fgr, fgw = HG_HEADS // HG_FWD[1], 128 * HG_FWD[1]
    bgr, bgw = HG_HEADS // HG_BWD[1], 128 * HG_BWD[1]
    G = {}

    def add_grad(name, idx, val):
        G.setdefault(name, {})[idx] = val

    def tok(a, cb=None, off=0):
        cb = a.shape[-1] if cb is None else cb
        return (a, (None, tb, cb), lambda b, j, off=off: (b, j, off))

    def tok_out(cols, dtype=F32):
        return ((B, Tt, cols), dtype, (None, tb, cols), lambda b, j: (b, j, 0))

    def vec(a):
        return (a, a.shape, lambda *ids, n=a.ndim: (0,) * n)

    def vec_acc(shape):
        return (shape, F32, shape, lambda *ids, n=len(shape): (0,) * n, lambda *ids: functools.reduce(jnp.logical_and, [i == 0 for i in ids]))

    def modv(l, which):
        return (modr, (None, None, None, 1, D), lambda b, j, l=l, which=which: (l, jnp.where(j == 0, 4, b), which, 0, 0))

    dmod_spec = ((B, 2, 1, D), F32, (None, None, 1, D), lambda b, j: (b, jnp.where(j == 0, 0, 1), 0, 0), lambda b, j: j <= 1)

    def phys_chunk(n_all, n_ctx, reverse):
        if not reverse:
            return lambda s: s
        return lambda s: jnp.where(s < n_ctx, n_ctx - 1 - s, n_all - 1 - (s - n_ctx))

    nmc = N_MOD * D // 1536
    modr = mod.reshape(DEPTH, 8, N_MOD, 1, D)
    dmods = {}

    lb1, lb3 = tile_fwd('lb_fwd', f_lb, (1,), [vec(W['hg_lb_logits'])],
                        [((1, HG_W), F32, (1, HG_W), lambda i: (0, 0))] * 2)
    dlb = {1: [], 3: []}

    x0 = jnp.concatenate([ctx, x], axis=1)
    conv_mk = ffnconv_masks(Tt, Lc)
    R = [dict() for _ in range(DEPTH)]

    xprev, oprev = x0, None
    for l in range(DEPTH):
        r = R[l]
        j = l // 2
        ng = W['norm_mix_g'][l][None]
        if l == 0:
            h1 = tile_fwd(f'nm0_fwd', f_nm0, (B, nt), [tok(xprev), vec(ng), modv(l, 0), modv(l, 1)], [tok_out(D, BF16)])[0]
            xa = xprev
        else:
            xa, h1 = tile_fwd(f'nm_mix_fwd{l}', f_nm, (B, nt),
                              [tok(xprev), tok(oprev), modv(l - 1, 5), vec(ng), modv(l, 0), modv(l, 1)],
                              [tok_out(D), tok_out(D, BF16)])
        r['xin'], r['oin'], r['xa'], r['h1'] = xprev, oprev, xa, h1
        h1m = h1.reshape(M, D)
        if l % 2 == 0:
            win = W['ev_w_in_p'][j]
            proj = mm(f'ev_proj{l}', [(h1m, win, 0)]).reshape(B, Tt, EV_PAD)
            r['proj'] = proj
            scw, scb = W['ssd_conv_w'][j], W['ssd_conv_b'][j][None]
            lcw, lcb = W['lru_conv_w'][j], W['lru_conv_b'][j][None]

            def conv_call(name, colblk0, w, b, wblk0, ncols, act):
                return tile_fwd(name, functools.partial(f_conv1d, lc=Lc, act=act), (ncols // 256, B),
                                [(proj, (None, Tt, 256), lambda cb, bi: (bi, 0, colblk0 + cb)),
                                 (w, (4, 256), lambda cb, bi: (0, wblk0 + cb)), (b, (1, 256), lambda cb, bi: (0, wblk0 + cb))],
                                [((B, Tt, ncols), F32, (None, Tt, 256), lambda cb, bi: (bi, 0, cb))])[0]
            xs_c = conv_call(f'conv_xs{l}', 12, scw, scb, 0, 1024, True)
            bc_c = conv_call(f'conv_bc{l}', 16, scw, scb, 4, 512, True)
            u_c = conv_call(f'conv_u{l}', 8, lcw, lcb, 0, 1024, False)
            r['xs'], r['bc'], r['u'] = xs_c, bc_c, u_c
            bias = _pad_rows(W['ssd_dt_bias'][j].reshape(-1), 1)
            alog = _pad_rows(W['ssd_a_log'][j].reshape(-1), 1)
            r['bias'], r['alog'] = bias, alog
            r['y'], r['st'], r['a4'], r['hp4'], r['h'] = [], [], [], [], []
            for d in range(2):
                ph = phys_chunk(nc, ncc, d == 1)
                y, st = scan_fwd(
                    f'ssd_fwd{l}_{d}', functools.partial(f_ssd, d=d, reverse=(d == 1)), (B, nc),
                    [(xs_c, (None, SSD_CHUNK, D), lambda b, s, ph=ph: (b, ph(s), 0)),
                     (bc_c, (None, SSD_CHUNK, 512), lambda b, s, ph=ph: (b, ph(s), 0)),
                     (proj, (None, SSD_CHUNK, 128), lambda b, s, ph=ph: (b, ph(s), 36)),
                     vec(bias), vec(alog)],
                    ((B, Tt, D), F32, (None, SSD_CHUNK, D), lambda b, s, ph=ph: (b, ph(s), 0)),
                    ((B, nc, 8, 128, 128), F32, (None, None, 8, 128, 128), lambda b, s: (b, s, 0, 0, 0)),
                    (8, 128, 128), lambda b, s: s == 0)
                r['y'].append(y)
                r['st'].append(st)
                a_d, bx_d = tile_fwd(
                    f'gates_fwd{l}_{d}', f_gates, (B, nt),
                    [tok(u_c), vec(W['lru_w_a'][j, d]), vec(W['lru_b_a'][j, d][None]), vec(W['lru_w_i'][j, d]),
                     vec(W['lru_b_i'][j, d][None]), vec(W['lru_lam'][j, d][None])],
                    [tok_out(D), tok_out(D)])
                h_d, hp_d = linrec(f'lru_fwd{l}_{d}', a_d, bx_d, 'F' if d == 0 else 'R')
                r['a4'].append(a_d)
                r['hp4'].append(hp_d)
                r['h'].append(h_d)
            dpad = _pad_rows(W['ssd_d'][j])
            sng = W['ssd_norm_g'][j][None]
            r['dpad'], r['sng'] = dpad, sng
            mix = tile_fwd(f'ssdfin_fwd{l}', f_ssdfin, (B, nt),
                           [tok(r['y'][0]), tok(r['y'][1]), tok(xs_c), tok(proj, D, 0), tok(proj, D, 1), tok(r['h'][0]),
                            tok(r['h'][1]), vec(dpad), vec(sng)], [tok_out(2 * D, BF16)])[0]
            wout = W['ev_w_out'][j]
        else:
            win = W['od_w_in'][j]
            proj = mm(f'od_proj{l}', [(h1m, win, 0)]).reshape(B, Tt, 4096)
            r['proj'] = proj
            lbv = lb1 if l == 1 else lb3
            ns = nc
            r['o'], r['zst'], r['coef'], r['bcat'], r['ccat'], r['hp5'], r['hcat'], r['yd'], r['s5in'] = [], [], [], [], [], [], [], [], []
            u2 = proj[:, :, 3840:].reshape(M, S5_W)
            r['u2'] = u2
            for d in range(2):
                ph = phys_chunk(ns, ncc, d == 1)
                o_d, zst = scan_fwd(
                    f'hgrn_fwd{l}_{d}', functools.partial(f_hgrn_group, reverse=(d == 1), chunk=HG_FWD[0]), (fgr, B, ns),
                    [(proj, (None, 128, fgw), lambda h, b, s, ph=ph: (b, ph(s), h)),
                     (proj, (None, 128, fgw), lambda h, b, s, ph=ph, d=d: (b, ph(s), (1 + d) * fgr + h)),
                     (proj, (None, 128, fgw), lambda h, b, s, ph=ph: (b, ph(s), 3 * fgr + h)),
                     (lbv, (1, fgw), lambda h, b, s: (0, h))],
                    ((B, Tt, HG_W), F32, (None, 128, fgw), lambda h, b, s, ph=ph: (b, ph(s), h)),
                    ((B, ns, fgr, fgw, 128), F32, (None, None, None, fgw, 128), lambda h, b, s: (b, s, h, 0, 0)),
                    (fgw, 128), lambda h, b, s: s == 0)
                r['o'].append(o_d)
                r['zst'].append(zst)
                s5in = [W['s5_lam_re'][j, d], W['s5_lam_im'][j, d], W['s5_log_step'][j, d].reshape(16, 1),
                        jnp.swapaxes(W['s5_b_re'][j], 1, 2), jnp.swapaxes(W['s5_b_im'][j], 1, 2)]
                r['s5in'].append(s5in)
                ar, ai, bbr, bbi = tile_fwd(f's5p_fwd{l}_{d}', f_s5p, (1,), [vec(t) for t in s5in],
                                            [((16, 64), F32, (16, 64), lambda i: (0, 0))] * 2
                                            + [((16, 16, 64), F32, (16, 16, 64), lambda i: (0, 0, 0))] * 2)
                coef = jnp.stack([ar.reshape(8, 128), ai.reshape(8, 128)])
                bcat = jnp.concatenate([_blockdiag(bbr), _blockdiag(bbi)], axis=1).astype(BF16)
                ccat = jnp.concatenate([_blockdiag_t(W['s5_c_re'][j, d]), -_blockdiag_t(W['s5_c_im'][j, d])], axis=0).astype(BF16)
                xcat = mm(f's5_in{l}_{d}', [(u2, bcat, 0)])
                h5, hp5 = clinrec(f's5_fwd{l}_{d}', xcat.reshape(B, Tt, 2 * D), coef, 'F' if d == 0 else 'R', Lc)
                hcat = h5.reshape(M, 2 * D)
                yd = mm(f's5_out{l}_{d}', [(hcat, ccat, 0)]).reshape(B, Tt, S5_W)
                r['coef'].append(coef)
                r['bcat'].append(bcat)
                r['ccat'].append(ccat)
                r['hp5'].append(hp5)
                r['hcat'].append(hcat)
                r['yd'].append(yd)
            hn = jnp.zeros((8, 128), F32).at[:HG_HEADS].set(W['hg_norm_g'][j])
            sd, gw, gb = W['s5_d'][j][None], W['s5_glu_w'][j], W['s5_glu_b'][j][None]
            r['fin_par'] = (hn, sd, gw, gb)
            mix = tile_fwd(f'oddfin_fwd{l}', f_oddfin, (B, nt),
                           [tok(r['o'][0]), tok(r['o'][1]), tok(proj, HG_W, 4), tok(r['yd'][0]), tok(r['yd'][1]),
                            tok(proj, S5_W, 15), vec(hn), vec(sd), vec(gw), vec(gb)], [tok_out(D, BF16)])[0]
            wout = W['od_w_out'][j]
        r['mix'] = mix
        o1 = mm(f'mix_out{l}', [(mix.reshape(M, -1), wout, 0)]).reshape(B, Tt, D)
        r['o1'] = o1
        fg = W['norm_ffn_g'][l][None]
        xb, h2 = tile_fwd(f'nm_ffn_fwd{l}', f_nm, (B, nt), [tok(xa), tok(o1), modv(l, 2), vec(fg), modv(l, 3), modv(l, 4)],
                          [tok_out(D), tok_out(D, BF16)])
        r['h2'] = h2
        h2m = h2.reshape(M, D)
        a = mm(f'ffn_gate{l}', [(h2m, W['ffn_w_gate'][l], 0)]).reshape(B, Tt, D_FF)
        up = mm(f'ffn_up{l}', [(h2m, W['ffn_w_up'][l], 0)]).reshape(B, Tt, D_FF)
        w9 = W['ffn_conv_w'][l].reshape(9, D_FF)
        cbias = W['ffn_conv_b'][l][None]
        r['a'], r['up'], r['w9'], r['cbias'] = a, up, w9, cbias
        act = tile_fwd(f'ffnconv_fwd{l}', f_ffnconv, (D_FF // 128, B),
                       [(a, (None, Tt, 128), lambda cb, bi: (bi, 0, cb)), (up, (None, Tt, 128), lambda cb, bi: (bi, 0, cb)),
                        (w9, (9, 128), lambda cb, bi: (0, cb)), (cbias, (1, 128), lambda cb, bi: (0, cb)), vec(conv_mk)],
                       [((B, Tt, D_FF), BF16, (None, Tt, 128), lambda cb, bi: (bi, 0, cb))])[0]
        r['act'] = act
        o2 = mm(f'ffn_down{l}', [(act.reshape(M, D_FF), W['ffn_w_down'][l], 0)]).reshape(B, Tt, D)
        xprev, oprev = xb, o2

    vmask = jnp.ones((nt, 1, D), F32).at[0].set(0.0)
    ones = jnp.ones((1, D), F32)
    fng = W['final_norm_g'][None]
    d_xp, d_o2, dg5, dfng, loss_vec = tile_bwd(
        'loss_head', f_final, (B, nt),
        [tok(xprev), tok(oprev), modv(DEPTH - 1, 5), vec(fng),
         (target, (None, tb, D), lambda b, j: (b, jnp.maximum(j - 1, 0), 0)), (vmask, (None, 1, D), lambda b, j: (j, 0, 0))],
        [[vec(ones)]],
        [(0,) + tok_out(D) + (None,), (1,) + tok_out(D, BF16) + (None,), (2,) + dmod_spec, (3,) + vec_acc((1, D))],
        prims=[(0,) + vec_acc((1, D))])
    loss = jnp.sum(loss_vec)
    add_grad('final_norm_g', None, dfng[0])
    dmods[(DEPTH - 1, 5)] = dg5

    for l in reversed(range(DEPTH)):
        r = R[l]
        j = l // 2
        d_o2m = d_o2.reshape(M, D)
        d_act = mm(f'ffn_down_dx{l}', [(d_o2m, W['ffn_w_down'][l], 0)], tb=True).reshape(B, Tt, D_FF)
        add_grad('ffn_w_down', l, mm(f'ffn_down_dw{l}', [(r['act'].reshape(M, D_FF), d_o2m, 0)], ta=True))
        d_a, d_up, dw9, dcb = tile_bwd(
            f'ffnconv_bwd{l}', f_ffnconv, (D_FF // 128, B),
            [(r['a'], (None, Tt, 128), lambda cb, bi: (bi, 0, cb)), (r['up'], (None, Tt, 128), lambda cb, bi: (bi, 0, cb)),
             (r['w9'], (9, 128), lambda cb, bi: (0, cb)), (r['cbias'], (1, 128), lambda cb, bi: (0, cb)), vec(conv_mk)],
            [[(d_act, (None, Tt, 128), lambda cb, bi: (bi, 0, cb))]],
            [(0, (B, Tt, D_FF), BF16, (None, Tt, 128), lambda cb, bi: (bi, 0, cb), None),
             (1, (B, Tt, D_FF), BF16, (None, Tt, 128), lambda cb, bi: (bi, 0, cb), None),
             (2, (9, D_FF), F32, (9, 128), lambda cb, bi: (0, cb), lambda cb, bi: bi == 0),
             (3, (1, D_FF), F32, (1, 128), lambda cb, bi: (0, cb), lambda cb, bi: bi == 0)])
        add_grad('ffn_conv_w', l, dw9.reshape(3, 3, D_FF))
        add_grad('ffn_conv_b', l, dcb[0])
        d_am, d_upm = d_a.reshape(M, D_FF), d_up.reshape(M, D_FF)
        h2m = r['h2'].reshape(M, D)
        d_h2 = mm(f'ffn_in_dx{l}', [(d_am, W['ffn_w_gate'][l], 0), (d_upm, W['ffn_w_up'][l], 0)], tb=True).reshape(B, Tt, D)
        add_grad('ffn_w_gate', l, mm(f'ffn_gate_dw{l}', [(h2m, d_am, 0)], ta=True))
        add_grad('ffn_w_up', l, mm(f'ffn_up_dw{l}', [(h2m, d_upm, 0)], ta=True))
        fg = W['norm_ffn_g'][l][None]
        d_xa, d_o1, dgate, dfg, dsh, dsc = tile_bwd(
            f'nm_ffn_bwd{l}', f_nm, (B, nt), [tok(r['xa']), tok(r['o1']), modv(l, 2), vec(fg), modv(l, 3), modv(l, 4)],
            [[tok(d_xp)], [tok(d_h2)]],
            [(0,) + tok_out(D) + (None,), (1,) + tok_out(D, BF16) + (None,), (2,) + dmod_spec, (3,) + vec_acc((1, D)),
             (4,) + dmod_spec, (5,) + dmod_spec])
        add_grad('norm_ffn_g', l, dfg[0])
        dmods[(l, 2)], dmods[(l, 3)], dmods[(l, 4)] = dgate, dsh, dsc
        d_o1m = d_o1.reshape(M, D)
        h1m = r['h1'].reshape(M, D)
        proj = r['proj']
        if l % 2 == 0:
            wout, win = W['ev_w_out'][j], W['ev_w_in_p'][j]
            d_mix = mm(f'mix_out_dx{l}', [(d_o1m, wout, 0)], tb=True).reshape(B, Tt, 2 * D)
            add_grad('ev_w_out', j, mm(f'mix_out_dw{l}', [(r['mix'].reshape(M, 2 * D), d_o1m, 0)], ta=True))
            d_y, d_xs_fin, d_z, d_gy, d_h, ddpad, dsng = tile_bwd(
                f'ssdfin_bwd{l}', f_ssdfin, (B, nt),
                [tok(r['y'][0]), tok(r['y'][1]), tok(r['xs']), tok(proj, D, 0), tok(proj, D, 1), tok(r['h'][0]), tok(r['h'][1]),
                 vec(r['dpad']), vec(r['sng'])],
                [[tok(d_mix)]],
                [(0,) + tok_out(D) + (None,), (2,) + tok_out(D) + (None,), (3,) + tok_out(D) + (None,), (4,) + tok_out(D) + (None,),
                 (5,) + tok_out(D) + (None,), (7,) + vec_acc((8, 128)), (8,) + vec_acc((1, D))])
            add_grad('ssd_d', j, ddpad[0, :SSD_HEADS])
            add_grad('ssd_norm_g', j, dsng[0])
            d_xs_parts, d_bc_parts, d_dt_parts, d_u_parts = [d_xs_fin], [], [], []
            dbias_t, dalog_t = [], []
            dh4 = d_h
            for d in range(2):
                ph0 = phys_chunk(nc, ncc, d == 1)

                def ph(s, ph0=ph0):
                    return ph0(nc - 1 - s)
                dxs_d, dbc_d, ddt_d, dbias, dalog = scan_bwd(
                    f'ssd_bwd{l}_{d}', functools.partial(f_ssd, d=d, reverse=(d == 1)), (B, nc),
                    [(r['xs'], (None, SSD_CHUNK, D), lambda b, s, ph=ph: (b, ph(s), 0)),
                     (r['bc'], (None, SSD_CHUNK, 512), lambda b, s, ph=ph: (b, ph(s), 0)),
                     (proj, (None, SSD_CHUNK, 128), lambda b, s, ph=ph: (b, ph(s), 36)),
                     vec(r['bias']), vec(r['alog'])],
                    (r['st'][d], (None, None, 8, 128, 128), lambda b, s: (b, nc - 1 - s, 0, 0, 0)),
                    (d_y, (None, SSD_CHUNK, D), lambda b, s, ph=ph: (b, ph(s), 0)),
                    [(0, (B, Tt, D), F32, (None, SSD_CHUNK, D), lambda b, s, ph=ph: (b, ph(s), 0), None),
                     (1, (B, Tt, 512), F32, (None, SSD_CHUNK, 512), lambda b, s, ph=ph: (b, ph(s), 0), None),
                     (2, (B, Tt, 128), F32, (None, SSD_CHUNK, 128), lambda b, s, ph=ph: (b, ph(s), 0), None),
                     (3,) + vec_acc((1, 128)), (4,) + vec_acc((1, 128))],
                    (8, 128, 128), lambda b, s: s == 0)
                d_xs_parts.append(dxs_d)
                d_bc_parts.append(dbc_d)
                d_dt_parts.append(ddt_d)
                dbias_t.append(dbias)
                dalog_t.append(dalog)
                g4, ga4 = linrec_bwd(f'lru_bwd{l}_{d}', r['a4'][d], dh4, r['hp4'][d], 'Fb' if d == 0 else 'Rb')
                du_g, dwa, dba, dwi, dbi, dlam = tile_bwd(
                    f'gates_bwd{l}_{d}', f_gates, (B, nt),
                    [tok(r['u']), vec(W['lru_w_a'][j, d]), vec(W['lru_b_a'][j, d][None]), vec(W['lru_w_i'][j, d]),
                     vec(W['lru_b_i'][j, d][None]), vec(W['lru_lam'][j, d][None])],
                    [[tok(ga4)], [tok(g4)]],
                    [(0,) + tok_out(D) + (None,), (1,) + vec_acc((8, 128, 128)), (2,) + vec_acc((1, D)), (3,) + vec_acc((8, 128, 128)),
                     (4,) + vec_acc((1, D)), (5,) + vec_acc((1, D))])
                d_u_parts.append(du_g)
                add_grad('lru_w_a', (j, d), dwa)
                add_grad('lru_b_a', (j, d), dba[0])
                add_grad('lru_w_i', (j, d), dwi)
                add_grad('lru_b_i', (j, d), dbi[0])
                add_grad('lru_lam', (j, d), dlam[0])
            add_grad('ssd_dt_bias', j, (dbias_t[0] + dbias_t[1])[0, :32].reshape(2, SSD_HEADS))
            add_grad('ssd_a_log', j, (dalog_t[0] + dalog_t[1])[0, :32].reshape(2, SSD_HEADS))
            scw, scb = W['ssd_conv_w'][j], W['ssd_conv_b'][j][None]
            lcw, lcb = W['lru_conv_w'][j], W['lru_conv_b'][j][None]

            def conv_bwd(name, colblk0, w, b, wblk0, ncols, act, parts):
                return tile_bwd(
                    name, functools.partial(f_conv1d, lc=Lc, act=act), (ncols // 256, B),
                    [(proj, (None, Tt, 256), lambda cb, bi: (bi, 0, colblk0 + cb)),
                     (w, (4, 256), lambda cb, bi: (0, wblk0 + cb)), (b, (1, 256), lambda cb, bi: (0, wblk0 + cb))],
                    [[(p, (None, Tt, 256), lambda cb, bi: (bi, 0, cb)) for p in parts]],
                    [(0, (B, Tt, ncols), F32, (None, Tt, 256), lambda cb, bi: (bi, 0, cb), None),
                     (1, (4, ncols), F32, (4, 256), lambda cb, bi: (0, cb), lambda cb, bi: bi == 0),
                     (2, (1, ncols), F32, (1, 256), lambda cb, bi: (0, cb), lambda cb, bi: bi == 0)])
            d_xs_raw, dw_xs, db_xs = conv_bwd(f'conv_xs_bwd{l}', 12, scw, scb, 0, 1024, True, d_xs_parts)
            d_bc_raw, dw_bc, db_bc = conv_bwd(f'conv_bc_bwd{l}', 16, scw, scb, 4, 512, True, d_bc_parts)
            d_u_raw, dw_u, db_u = conv_bwd(f'conv_u_bwd{l}', 8, lcw, lcb, 0, 1024, False, d_u_parts)
            add_grad('ssd_conv_w', j, jnp.concatenate([dw_xs, dw_bc], axis=1))
            add_grad('ssd_conv_b', j, jnp.concatenate([db_xs, db_bc], axis=1)[0])
            add_grad('lru_conv_w', j, dw_u)
            add_grad('lru_conv_b', j, db_u[0])
            def f_ev_dproj(z_, gy_, u_, xs_, bc_, t0, t1):
                pad = jnp.zeros((z_.shape[0], EV_PAD - 4736), F32)
                return jnp.concatenate([z_, gy_, u_, xs_, bc_, t0 + t1, pad], axis=1)
            dproj = tile_fwd(f'ev_dproj{l}', f_ev_dproj, (B, nt),
                             [tok(d_z), tok(d_gy), tok(d_u_raw), tok(d_xs_raw), tok(d_bc_raw), tok(d_dt_parts[0]), tok(d_dt_parts[1])],
                             [tok_out(EV_PAD, BF16)])[0].reshape(M, EV_PAD)
            d_h1 = mm(f'ev_proj_dx{l}', [(dproj, win, 0)], tb=True).reshape(B, Tt, D)
            dwp = mm(f'ev_proj_dw{l}', [(h1m, dproj, 0)], ta=True)
            add_grad('ev_w_in', j, jnp.concatenate([dwp[:, 0:1024], dwp[:, 3072:4640], dwp[:, 1024:3072]], axis=1))
        else:
            wout, win = W['od_w_out'][j], W['od_w_in'][j]
            d_mix = mm(f'mix_out_dx{l}', [(d_o1m, wout, 0)], tb=True).reshape(B, Tt, D)
            add_grad('od_w_out', j, mm(f'mix_out_dw{l}', [(r['mix'].reshape(M, D), d_o1m, 0)], ta=True))
            hn, sd, gw, gb = r['fin_par']
            d_o, d_g, d_yv, d_u_fin, dhn, dsd, dgw, dgb = tile_bwd(
                f'oddfin_bwd{l}', f_oddfin, (B, nt),
                [tok(r['o'][0]), tok(r['o'][1]), tok(proj, HG_W, 4), tok(r['yd'][0]), tok(r['yd'][1]), tok(proj, S5_W, 15),
                 vec(hn), vec(sd), vec(gw), vec(gb)],
                [[tok(d_mix)]],
                [(0,) + tok_out(HG_W) + (None,), (2,) + tok_out(HG_W) + (None,), (3,) + tok_out(S5_W) + (None,),
                 (5,) + tok_out(S5_W) + (None,), (6,) + vec_acc((8, 128)), (7,) + vec_acc((1, S5_W)), (8,) + vec_acc((S5_W, S5_W)),
                 (9,) + vec_acc((1, S5_W))])
            add_grad('hg_norm_g', j, dhn[:HG_HEADS])
            add_grad('s5_d', j, dsd[0])
            add_grad('s5_glu_w', j, dgw)
            add_grad('s5_glu_b', j, dgb[0])
            lbv = lb1 if l == 1 else lb3
            ns = nc
            dq, df, dv, du_s5 = [], [], [], []
            d_ym = d_yv.reshape(M, S5_W)
            dbt_re, dbt_im = [], []
            for d in range(2):
                ph0 = phys_chunk(ns, ncc, d == 1)

                def ph(s, ph0=ph0):
                    return ph0(ns - 1 - s)
                dq_d, df_d, dv_d, dlb_d = scan_bwd(
                    f'hgrn_bwd{l}_{d}', functools.partial(f_hgrn_group, reverse=(d == 1), chunk=HG_BWD[0]), (bgr, B, ns),
                    [(proj, (None, 128, bgw), lambda h, b, s, ph=ph: (b, ph(s), h)),
                     (proj, (None, 128, bgw), lambda h, b, s, ph=ph, d=d: (b, ph(s), (1 + d) * bgr + h)),
                     (proj, (None, 128, bgw), lambda h, b, s, ph=ph: (b, ph(s), 3 * bgr + h)),
                     (lbv, (1, bgw), lambda h, b, s: (0, h))],
                    (r['zst'][d].reshape(B, ns, bgr, bgw, 128), (None, None, None, bgw, 128), lambda h, b, s: (b, ns - 1 - s, h, 0, 0)),
                    (d_o, (None, 128, bgw), lambda h, b, s, ph=ph: (b, ph(s), h)),
                    [(0, (B, Tt, HG_W), F32, (None, 128, bgw), lambda h, b, s, ph=ph: (b, ph(s), h), None),
                     (1, (B, Tt, HG_W), F32, (None, 128, bgw), lambda h, b, s, ph=ph: (b, ph(s), h), None),
                     (2, (B, Tt, HG_W), F32, (None, 128, bgw), lambda h, b, s, ph=ph: (b, ph(s), h), None),
                     (3, (1, HG_W), F32, (1, bgw), lambda h, b, s: (0, h), lambda h, b, s: (b == 0) & (s == 0))],
                    (bgw, 128), lambda h, b, s: s == 0)
                dq.append(dq_d)
                df.append(df_d)
                dv.append(dv_d)
                dlb[l].append(dlb_d)
                d_hcat = mm(f's5_out_dx{l}_{d}', [(d_ym, r['ccat'][d], 0)], tb=True)
                dccat = mm(f's5_out_dw{l}_{d}', [(r['hcat'][d], d_ym, 0)], ta=True)
                add_grad('s5_c_re', (j, d), _unblockdiag_t(dccat[:D]))
                add_grad('s5_c_im', (j, d), -_unblockdiag_t(dccat[D:]))
                g5, dcoef8 = clinrec(f's5_bwd{l}_{d}', d_hcat.reshape(B, Tt, 2 * D), r['coef'][d], 'Fb' if d == 0 else 'Rb',
                                     Lc, conj=True, hprev=r['hp5'][d])
                dcoef = jnp.sum(dcoef8, axis=2)
                gcat = g5.reshape(M, 2 * D)
                dbcat = mm(f's5_in_dw{l}_{d}', [(r['u2'], gcat, 0)], ta=True)
                du_s5.append(mm(f's5_in_dx{l}_{d}', [(gcat, r['bcat'][d], 0)], tb=True))
                cts5 = [dcoef[0].reshape(16, 64), dcoef[1].reshape(16, 64), _unblockdiag(dbcat[:, :D]), _unblockdiag(dbcat[:, D:])]
                dlre, dlim, dlst, dbtr, dbti = tile_bwd(
                    f's5p_bwd{l}_{d}', f_s5p, (1,), [vec(t) for t in r['s5in'][d]], [[vec(t)] for t in cts5],
                    [(i, t.shape, F32, t.shape, (lambda *ids, n=t.ndim: (0,) * n), None) for i, t in enumerate(r['s5in'][d])])
                add_grad('s5_lam_re', (j, d), dlre)
                add_grad('s5_lam_im', (j, d), dlim)
                add_grad('s5_log_step', (j, d), dlst[:, 0])
                dbt_re.append(dbtr)
                dbt_im.append(dbti)
            add_grad('s5_b_re', j, jnp.swapaxes(dbt_re[0] + dbt_re[1], 1, 2))
            add_grad('s5_b_im', j, jnp.swapaxes(dbt_im[0] + dbt_im[1], 1, 2))
            def f_od_dproj(q0, q1, f0, f1, v0, v1, g_, u0, u1, u2):
                return jnp.concatenate([q0 + q1, f0, f1, v0 + v1, g_, u0 + u1 + u2], axis=1)
            parts = [dq[0], dq[1], df[0], df[1], dv[0], dv[1], d_g, d_u_fin, du_s5[0].reshape(B, Tt, S5_W),
                     du_s5[1].reshape(B, Tt, S5_W)]
            dproj = tile_fwd(f'od_dproj{l}', f_od_dproj, (B, nt), [tok(t) for t in parts],
                             [tok_out(4096, BF16)])[0].reshape(M, 4096)
            d_h1 = mm(f'od_proj_dx{l}', [(dproj, win, 0)], tb=True).reshape(B, Tt, D)
            add_grad('od_w_in', j, mm(f'od_proj_dw{l}', [(h1m, dproj, 0)], ta=True))
        ng = W['norm_mix_g'][l][None]
        if l == 0:
            d_x0, dng, dsh, dsc = tile_bwd(
                'nm0_bwd', lambda xv, g, sh, sc: (xv, f_nm0(xv, g, sh, sc)), (B, nt),
                [tok(r['xin']), vec(ng), modv(l, 0), modv(l, 1)], [[tok(d_xa)], [tok(d_h1)]],
                [(0,) + tok_out(D) + (None,), (1,) + vec_acc((1, D)), (2,) + dmod_spec, (3,) + dmod_spec])
        else:
            d_xp, d_o2, dgate, dng, dsh, dsc = tile_bwd(
                f'nm_mix_bwd{l}', f_nm, (B, nt),
                [tok(r['xin']), tok(r['oin']), modv(l - 1, 5), vec(ng), modv(l, 0), modv(l, 1)],
                [[tok(d_xa)], [tok(d_h1)]],
                [(0,) + tok_out(D) + (None,), (1,) + tok_out(D, BF16) + (None,), (2,) + dmod_spec, (3,) + vec_acc((1, D)),
                 (4,) + dmod_spec, (5,) + dmod_spec])
            dmods[(l - 1, 5)] = dgate
        add_grad('norm_mix_g', l, dng[0])
        dmods[(l, 0)], dmods[(l, 1)] = dsh, dsc

    grad_x = d_x0[:, Lc:, :]

    (dlogits,) = tile_bwd('lb_bwd', f_lb, (1,), [vec(W['hg_lb_logits'])],
                          [[vec(t) for t in dlb[1]], [vec(t) for t in dlb[3]]],
                          [(0, (DEPTH, HG_W), F32, (DEPTH, HG_W), lambda i: (0, 0), None)])
    add_grad('hg_lb_logits', None, dlogits)

    dm = jnp.stack([jnp.stack([dmods[(l, w)] for w in range(N_MOD)]) for l in range(DEPTH)])
    dlat = jnp.transpose(dm[:, :, :, 1, 0, :], (0, 2, 1, 3)).reshape(DEPTH, B, N_MOD * D)
    dctx = jnp.transpose(dm[:, :, :, 0, 0, :], (0, 2, 1, 3)).reshape(DEPTH, B, N_MOD * D)
    dlat = jnp.zeros((DEPTH, 8, N_MOD * D), F32).at[:, :B].set(dlat)
    dctx = jnp.zeros((DEPTH, 8, N_MOD * D), F32).at[:, :B].set(dctx)

    def f_dmod(dl, dc):
        row = lax.broadcasted_iota(jnp.int32, (8, 1), 0)
        dall = dl + jnp.where(row == 4, jnp.sum(dc, axis=0, keepdims=True), 0.0)
        return dall, jnp.sum(dall, axis=0, keepdims=True)

    dmod, dbmod = tile_fwd('mod_bwd', f_dmod, (DEPTH, nmc),
                           [(dlat, (None, 8, 1536), lambda l, n: (l, 0, n)), (dctx, (None, 8, 1536), lambda l, n: (l, 0, n))],
                           [((DEPTH, 8, N_MOD * D), F32, (None, 8, 1536), lambda l, n: (l, 0, n)),
                            ((DEPTH, 1, N_MOD * D), F32, (None, 1, 1536), lambda l, n: (l, 0, n))])
    add_grad('b_mod', None, dbmod[:, 0])
    return loss, grad_x, G, dmod


def mod_weight_grad(name, cc_rows, dmod_rows):
    nl, nr, cols = dmod_rows.shape

    def f(ccv, dv):
        return lax.dot_general(jax.nn.silu(ccv).astype(BF16), dv.astype(BF16), (((0,), (0,)), ((), ())),
                               preferred_element_type=F32)

    return tile_fwd(name, f, (nl,), [(cc_rows, (nr, D), lambda l: (0, 0)), (dmod_rows, (None, nr, cols), lambda l: (l, 0, 0))],
                    [((nl, D, cols), F32, (None, D, cols), lambda l: (l, 0, 0))])[0]


def assemble_grads(G, like):
    out = {}
    for name, parts in G.items():
        shape = like[name].shape
        if None in parts:
            g = parts[None]
        elif isinstance(next(iter(parts)), tuple):
            g = jnp.stack([jnp.stack([parts[(j, d)] for d in range(2)]) for j in range(shape[0])])
        else:
            g = jnp.stack([parts[i] for i in range(shape[0])])
        out[name] = g.reshape(shape)
    return out


XY_RELS = ((1, 0, 0), (0, 1, 0), (1, 1, 0))
ALL_RELS = tuple((dx, dy, dc) for dx in (0, 1) for dy in (0, 1) for dc in (0, 1))[1:]


def exchange(name, src, out_shape, sends):
    return exchange_many(name, [(src, out_shape, sends)])[0]


def exchange_many(name, items):
    na = len(items)
    n = sum(len(sends) for _, _, sends in items)

    def body(*refs):
        src_refs, out_refs, send_sems, recv_sems = refs[:na], refs[na:2 * na], refs[2 * na], refs[2 * na + 1]
        me = (lax.axis_index("x"), lax.axis_index("y"), lax.axis_index("c"))
        copies, k = [], 0
        for (_, _, sends), src_ref, out_ref in zip(items, src_refs, out_refs):
            for rel, ssel, dsel in sends:
                tgt = tuple(1 - m if f else m for m, f in zip(me, rel))
                cp = pltpu.make_async_remote_copy(
                    src_ref=src_ref if ssel is None else src_ref.at[ssel(me, tgt)],
                    dst_ref=out_ref if dsel is None else out_ref.at[dsel(me, tgt)],
                    send_sem=send_sems.at[k], recv_sem=recv_sems.at[k], device_id=tgt, device_id_type=MESH)
                cp.start()
                copies.append(cp)
                k += 1
        for cp in copies:
            cp.wait()

    return pl.pallas_call(
        body, out_shape=[jax.ShapeDtypeStruct(shape, src.dtype) for src, shape, _ in items],
        in_specs=[pl.BlockSpec(memory_space=pl.ANY)] * na, out_specs=[pl.BlockSpec(memory_space=pl.ANY)] * na,
        scratch_shapes=[pltpu.SemaphoreType.DMA((n,)), pltpu.SemaphoreType.DMA((n,))],
        name=name,
    )(*[src for src, _, _ in items])


def _xy_index(dev):
    return 2 * dev[0] + dev[1]


def _my_xy():
    return 2 * lax.axis_index("x") + lax.axis_index("y")


def all_gather_xy(name, shard):
    got = exchange(name, shard, (4,) + shard.shape, [(rel, None, lambda me, tgt: _xy_index(me)) for rel in XY_RELS])
    return lax.dynamic_update_index_in_dim(got, shard, _my_xy(), 0)


def reduce_scatter_xy(name, g4):
    got = exchange(name, g4, (3,) + g4.shape[1:],
                   [(rel, (lambda me, tgt: _xy_index(tgt)), (lambda me, tgt, k=k: k)) for k, rel in enumerate(XY_RELS)])
    return got, lax.dynamic_index_in_dim(g4, _my_xy(), 0, keepdims=False)


def sibling_swap(name, v):
    return exchange(name, v, v.shape, [((0, 0, 1), None, None)])


def all_gather_all(name, v):
    got = exchange(name, v, (8,) + v.shape, [(rel, None, lambda me, tgt: 4 * me[0] + 2 * me[1] + me[2]) for rel in ALL_RELS])
    return lax.dynamic_update_index_in_dim(got, v, 2 * _my_xy() + lax.axis_index("c"), 0)


def all_gather_xy_halves(name, shards):
    na = len(shards)

    def body(*refs):
        src_refs, out_refs, send_sems, recv_sems = refs[:na], refs[na:2 * na], refs[2 * na], refs[2 * na + 1]
        x, y, c = lax.axis_index("x"), lax.axis_index("y"), lax.axis_index("c")
        peers = [(1 - x, y), (x, 1 - y), (1 - x, 1 - y)]

        def copy(k, src, dst, to):
            return pltpu.make_async_remote_copy(src_ref=src, dst_ref=dst, send_sem=send_sems.at[k], recv_sem=recv_sems.at[k],
                                                device_id=to, device_id_type=MESH)

        halves = [pl.ds(c * (s.shape[0] // 2), s.shape[0] // 2) for s in shards]
        first = [[copy(6 * i + k, src_refs[i].at[halves[i]], out_refs[i].at[2 * x + y, halves[i]], (px, py, c))
                  for k, (px, py) in enumerate(peers)] for i in range(na)]
        for row in first:
            for cp in row:
                cp.start()
        passed = []
        for i in range(na):
            for k, (px, py) in enumerate(peers):
                first[i][k].wait_recv()
                landed = out_refs[i].at[2 * px + py, halves[i]]
                fw = copy(6 * i + 3 + k, landed, landed, (x, y, 1 - c))
                fw.start()
                passed.append(fw)
        for fw in passed:
            fw.wait_recv()
        for cp in [cp for row in first for cp in row] + passed:
            cp.wait_send()

    got = pl.pallas_call(
        body, out_shape=[jax.ShapeDtypeStruct((4,) + s.shape, s.dtype) for s in shards],
        in_specs=[pl.BlockSpec(memory_space=pl.ANY)] * na, out_specs=[pl.BlockSpec(memory_space=pl.ANY)] * na,
        scratch_shapes=[pltpu.SemaphoreType.DMA((6 * na,)), pltpu.SemaphoreType.DMA((6 * na,))],
        name=name,
    )(*shards)
    return [lax.dynamic_update_index_in_dim(g, s, _my_xy(), 0) for g, s in zip(got, shards)]


def reduce_scatter_xy_many(name, g4s):
    got = exchange_many(name, [(g4, (3,) + g4.shape[1:],
                                [(rel, (lambda me, tgt: _xy_index(tgt)), (lambda me, tgt, k=k: k)) for k, rel in enumerate(XY_RELS)])
                               for g4 in g4s])
    return [(g, lax.dynamic_index_in_dim(g4, _my_xy(), 0, keepdims=False)) for g, g4 in zip(got, g4s)]


def sibling_split(name, g4s):
    halves = [g4.shape[1] // 2 for g4 in g4s]
    got = exchange_many(name, [(g4, (4, h) + g4.shape[2:], [((0, 0, 1), (lambda me, tgt, h=h: (slice(None), pl.ds(tgt[2] * h, h))), None)])
                               for g4, h in zip(g4s, halves)])
    return [(g, lax.dynamic_slice_in_dim(g4, lax.axis_index("c") * h, h, axis=1)) for g, g4, h in zip(got, g4s, halves)]


def sibling_join(name, qs):
    got = exchange_many(name, [(q, (2 * q.shape[0],) + q.shape[1:], [((0, 0, 1), None, lambda me, tgt, h=q.shape[0]: pl.ds(me[2] * h, h))])
                               for q in qs])
    return [lax.dynamic_update_slice_in_dim(g, q, lax.axis_index("c") * q.shape[0], axis=0) for g, q in zip(got, qs)]


def _rows_view(shape):
    cols = shape[-1] if len(shape) else 1
    rows = 1
    for s in shape[:-1]:
        rows *= s
    return rows, cols


def _row_block(rows, cols, n_arrays):
    budget = (24 * 1024 * 1024) // (8 * n_arrays * cols)
    if rows <= max(budget, 16):
        return rows
    br = (min(budget, rows) // 16) * 16
    while br > 16 and rows % br:
        br -= 16
    return br if rows % br == 0 else rows


def sum_slots(name, stacked, extra=(), out_dtype=F32):
    k = stacked.shape[0]
    rows, cols = _rows_view(stacked.shape[1:])
    br = _row_block(rows, cols, k + len(extra) + 1)

    def f(s, *more):
        parts = [s[i].astype(F32) for i in range(k)] + [m.astype(F32) for m in more]
        while len(parts) > 1:
            parts = [parts[i] + parts[i + 1] for i in range(0, len(parts) - 1, 2)] + ([parts[-1]] if len(parts) % 2 else [])
        return parts[0]

    out = tile_fwd(name, f, (rows // br,),
                   [(stacked.reshape(k, rows, cols), (k, br, cols), lambda i: (0, i, 0))]
                   + [(e.reshape(rows, cols), (br, cols), lambda i: (i, 0)) for e in extra],
                   [((rows, cols), out_dtype, (br, cols), lambda i: (i, 0))])[0]
    return out.reshape(stacked.shape[1:])


def adamw(name, w, m, v, gs):
    rows, cols = _rows_view(w.shape)
    br = _row_block(rows, cols, 7 + len(gs))
    spec = lambda a: (a.reshape(rows, cols), (br, cols), lambda i: (i, 0))
    outs = tile_fwd(name, f_adamw, (rows // br,), [spec(t) for t in (w, m, v) + tuple(gs)],
                    [((rows, cols), F32, (br, cols), lambda i: (i, 0))] * 4)
    return [o.reshape(w.shape) for o in outs]


IN_NAMES = ['x', 'c', 'ctx'] + W_NAMES + ['loss_target'] + ['m_' + n for n in W_NAMES] + ['v_' + n for n in W_NAMES]
SMALL_PAD = 128 * 1024


def kernel(x, c, ctx, c_ctx, w_mod, b_mod, norm_mix_g, norm_ffn_g, final_norm_g, ev_w_in, ev_w_out, ssd_conv_w, ssd_conv_b, ssd_dt_bias, ssd_a_log, ssd_d, ssd_norm_g, lru_conv_w, lru_conv_b, lru_w_a, lru_b_a, lru_w_i, lru_b_i, lru_lam, od_w_in, od_w_out, hg_lb_logits, hg_norm_g, s5_lam_re, s5_lam_im, s5_log_step, s5_b_re, s5_b_im, s5_c_re, s5_c_im, s5_d, s5_glu_w, s5_glu_b, ffn_w_gate, ffn_w_up, ffn_conv_w, ffn_conv_b, ffn_w_down, loss_target, m_c_ctx, m_w_mod, m_b_mod, m_norm_mix_g, m_norm_ffn_g, m_final_norm_g, m_ev_w_in, m_ev_w_out, m_ssd_conv_w, m_ssd_conv_b, m_ssd_dt_bias, m_ssd_a_log, m_ssd_d, m_ssd_norm_g, m_lru_conv_w, m_lru_conv_b, m_lru_w_a, m_lru_b_a, m_lru_w_i, m_lru_b_i, m_lru_lam, m_od_w_in, m_od_w_out, m_hg_lb_logits, m_hg_norm_g, m_s5_lam_re, m_s5_lam_im, m_s5_log_step, m_s5_b_re, m_s5_b_im, m_s5_c_re, m_s5_c_im, m_s5_d, m_s5_glu_w, m_s5_glu_b, m_ffn_w_gate, m_ffn_w_up, m_ffn_conv_w, m_ffn_conv_b, m_ffn_w_down, v_c_ctx, v_w_mod, v_b_mod, v_norm_mix_g, v_norm_ffn_g, v_final_norm_g, v_ev_w_in, v_ev_w_out, v_ssd_conv_w, v_ssd_conv_b, v_ssd_dt_bias, v_ssd_a_log, v_ssd_d, v_ssd_norm_g, v_lru_conv_w, v_lru_conv_b, v_lru_w_a, v_lru_b_a, v_lru_w_i, v_lru_b_i, v_lru_lam, v_od_w_in, v_od_w_out, v_hg_lb_logits, v_hg_norm_g, v_s5_lam_re, v_s5_lam_im, v_s5_log_step, v_s5_b_re, v_s5_b_im, v_s5_c_re, v_s5_c_im, v_s5_d, v_s5_glu_w, v_s5_glu_b, v_ffn_w_gate, v_ffn_w_up, v_ffn_conv_w, v_ffn_conv_b, v_ffn_w_down):
    a = dict(locals())
    big = [n for n in W_NAMES if n in MATMUL_WEIGHTS and n != 'w_mod']
    minor = [n for n in W_NAMES if n in SHARD_AXIS and n not in MATMUL_WEIGHTS]

    def pack(arrays, lead=()):
        flat = jnp.concatenate([t.reshape(lead + (-1,)) for t in arrays], axis=len(lead))
        pad = -flat.shape[-1] % 1024
        flat = jnp.concatenate([flat, jnp.zeros(lead + (pad,), flat.dtype)], axis=len(lead))
        return flat.reshape(lead + (-1, 128))

    def unpack(packed, names, lead=()):
        flat, out, off = packed.reshape(lead + (-1,)), {}, 0
        for n in names:
            size = math.prod(a[n].shape)
            out[n] = flat[..., off:off + size].reshape(lead + a[n].shape)
            off += size
        return out

    gathered = dict(zip(big, all_gather_xy_halves('ag_big', [a[n].astype(BF16) for n in big])))
    gathered.update(unpack(all_gather_xy('ag_minor', pack([a[n] for n in minor])), minor, (4,)))
    W = {}
    for n in W_NAMES:
        w = a[n]
        if n in gathered:
            ax = SHARD_AXIS[n]
            shape = list(w.shape)
            shape[ax] *= 4
            W[n] = jnp.moveaxis(gathered[n], 0, ax).reshape(shape)
        elif n not in SHARD_AXIS:
            W[n] = w
    e = W['ev_w_in']
    W['ev_w_in_p'] = jnp.concatenate(
        [e[:, :, 0:1024], e[:, :, 2592:3616], e[:, :, 3616:4640], e[:, :, 1024:2560], e[:, :, 2560:2592],
         jnp.zeros((e.shape[0], D, EV_PAD - 4640), e.dtype)], axis=2)

    my_dev = 2 * _my_xy() + lax.axis_index("c")
    mcols = a['w_mod'].shape[2]
    w_mod_mine = a['w_mod'].astype(BF16)
    cc_all = all_gather_all('ag_c', cond_rows(a['c'], a['c_ctx'])).reshape(8 * 8, D)
    mod_cols = mod_table('mod_fwd', cc_all, w_mod_mine, lax.dynamic_slice_in_dim(a['b_mod'], _my_xy() * mcols, mcols, axis=1))
    rows_of = lambda dev: pl.ds((4 * dev[0] + 2 * dev[1] + dev[2]) * 8, 8)
    got = exchange('ag_mod', mod_cols, (4, DEPTH, 8, mcols),
                   [(rel, (lambda me, tgt: (slice(None), rows_of(tgt))), (lambda me, tgt: _xy_index(me))) for rel in XY_RELS])
    got = lax.dynamic_update_index_in_dim(got, lax.dynamic_slice_in_dim(mod_cols, my_dev * 8, 8, axis=1), _my_xy(), 0)
    mod = jnp.moveaxis(got, 0, 2).reshape(DEPTH, 8, N_MOD * D)

    loss_local, grad_x, G, dmod = local_step(a['x'], a['ctx'], a['loss_target'], mod, W)
    grads = assemble_grads(G, W)
    loss = lax.psum(loss_local, ("x", "y", "c"))

    res = {}
    cols_of = lambda dev: pl.ds(_xy_index(dev) * mcols, mcols)
    got = exchange('ag_dmod', dmod, (8, DEPTH, 8, mcols),
                   [(rel, (lambda me, tgt: (slice(None), slice(None), cols_of(tgt))), (lambda me, tgt: 4 * me[0] + 2 * me[1] + me[2]))
                    for rel in ALL_RELS])
    got = lax.dynamic_update_index_in_dim(got, lax.dynamic_slice_in_dim(dmod, _my_xy() * mcols, mcols, axis=2), my_dev, 0)
    dm_mine = jnp.moveaxis(got, 0, 1).reshape(DEPTH, 8 * 8, mcols)
    res['w_mod'] = adamw('adamw_w_mod', a['w_mod'], a['m_w_mod'], a['v_w_mod'], (mod_weight_grad('mod_dw', cc_all, dm_mine),))
    dcc = mod_cond_grad('mod_dcc', cc_all, dm_mine, w_mod_mine).reshape(8, 8, D)
    grads['c_ctx'] = 0.5 * jnp.sum(dcc[:, 4], axis=0)
    g4 = {}
    for n in big + minor:
        ax = SHARD_AXIS[n]
        gf = grads[n]
        g4[n] = jnp.moveaxis(gf.reshape(gf.shape[:ax] + (4, a[n].shape[ax]) + gf.shape[ax + 1:]), ax, 0)
    parts = [sum_slots('csum_' + n, theirs[None], (ours,), BF16)
             for n, (theirs, ours) in zip(big, sibling_split('rsc_big', [g4[n] for n in big]))]
    halves = [sum_slots('gsum_' + n, got, (own,)) for n, (got, own) in zip(big, reduce_scatter_xy_many('rs_big', parts))]
    for n, g in zip(big, sibling_join('agc_big', halves)):
        res[n] = adamw('adamw_' + n, a[n], a['m_' + n], a['v_' + n], (g,))
    got, own = reduce_scatter_xy('rs_minor', pack([g4[n] for n in minor], (4,)))
    mine = sum_slots('gsum_minor', got, (own,))
    mine_n, other_n = unpack(mine, minor), unpack(sibling_swap('sw_minor', mine), minor)
    for n in minor:
        res[n] = adamw('adamw_' + n, a[n], a['m_' + n], a['v_' + n], (mine_n[n], other_n[n]))
    small = [n for n in W_NAMES if n not in SHARD_AXIS]
    flat = jnp.concatenate([grads[n].reshape(-1) for n in small])
    total = flat.shape[0]
    padded = -(-total // SMALL_PAD) * SMALL_PAD
    flat = jnp.concatenate([flat, jnp.zeros((padded - total,), F32)]).reshape(padded // 128, 128)
    pair = sum_slots('csum_small', flat[None], (sibling_swap('sw_small', flat),))
    summed = sum_slots('gsum_small', all_gather_xy_halves('ag_small', [pair])[0]).reshape(-1)
    off = 0
    for n in small:
        size = math.prod(a[n].shape)
        g = summed[off:off + size].reshape(a[n].shape)
        off += size
        res[n] = adamw('adamw_' + n, a[n], a['m_' + n], a['v_' + n], (g,))
    outs = [loss, grad_x]
    for k in range(4):
        outs += [res[n][k] for n in W_NAMES]
    return tuple(outs)
```

```python
import functools
import math

import jax
import jax.numpy as jnp
from jax import lax
from jax.experimental import pallas as pl
from jax.experimental.pallas import tpu as pltpu

F32 = jnp.float32
BF16 = jnp.bfloat16
HI = lax.Precision.HIGHEST
MESH = pl.DeviceIdType.MESH

D = 1024
DEPTH = 4
N_MOD = 6
RMS_EPS = 1e-6
GRID_W = 64
SSD_HEADS = 16
SSD_CHUNK = 128
HG_W = 768
HG_HEADS = 6
HG_FWD = (32, 6)
HG_BWD = (16, 3)
S5_W = 256
D_FF = 2816
EV_PAD = 5120
LRU_C = 8.0
V7X_VMEM_LIMIT = 56 * 1024 * 1024
MM_VMEM_BUDGET = 36 * 1024 * 1024

ADAM_LR, ADAM_B1, ADAM_B2, ADAM_EPS, ADAM_WD, ADAM_STEP = 0.001, 0.9, 0.999, 1e-08, 0.01, 10

W_NAMES = ['c_ctx', 'w_mod', 'b_mod', 'norm_mix_g', 'norm_ffn_g', 'final_norm_g', 'ev_w_in', 'ev_w_out', 'ssd_conv_w',
           'ssd_conv_b', 'ssd_dt_bias', 'ssd_a_log', 'ssd_d', 'ssd_norm_g', 'lru_conv_w', 'lru_conv_b', 'lru_w_a', 'lru_b_a',
           'lru_w_i', 'lru_b_i', 'lru_lam', 'od_w_in', 'od_w_out', 'hg_lb_logits', 'hg_norm_g', 's5_lam_re', 's5_lam_im',
           's5_log_step', 's5_b_re', 's5_b_im', 's5_c_re', 's5_c_im', 's5_d', 's5_glu_w', 's5_glu_b', 'ffn_w_gate', 'ffn_w_up',
           'ffn_conv_w', 'ffn_conv_b', 'ffn_w_down']
SHARD_AXIS = {'w_mod': 2, 'ev_w_in': 2, 'ev_w_out': 1, 'ssd_conv_w': 2, 'lru_conv_w': 2, 'lru_b_a': 2, 'lru_b_i': 2,
              'lru_lam': 2, 'od_w_in': 2, 'od_w_out': 1, 's5_d': 1, 's5_glu_w': 1, 's5_glu_b': 1, 'ffn_w_gate': 2,
              'ffn_w_up': 2, 'ffn_conv_w': 3, 'ffn_w_down': 1}
MATMUL_WEIGHTS = ('w_mod', 'ev_w_in', 'ev_w_out', 'od_w_in', 'od_w_out', 'ffn_w_gate', 'ffn_w_up', 'ffn_w_down')


def _cparams(sem=None):
    return pltpu.CompilerParams(vmem_limit_bytes=V7X_VMEM_LIMIT, dimension_semantics=sem)


def _pick(n, cands):
    for c in cands:
        if n % c == 0:
            return c
    return n


def tile_fwd(name, f, grid, ins, outs):
    n_in = len(ins)

    def body(*refs):
        res = f(*[r[...] for r in refs[:n_in]])
        if not isinstance(res, (tuple, list)):
            res = (res,)
        for r, o in zip(res, refs[n_in:]):
            o[...] = r.astype(o.dtype)

    res = pl.pallas_call(
        body, grid=grid,
        in_specs=[pl.BlockSpec(b, m) for _, b, m in ins],
        out_specs=[pl.BlockSpec(b, m) for _, _, b, m in outs],
        out_shape=[jax.ShapeDtypeStruct(s, d) for s, d, _, _ in outs],
        name=name, compiler_params=_cparams(("arbitrary",) * len(grid)),
    )(*[a for a, _, _ in ins])
    return res


def tile_bwd(name, f, grid, ins, cts, grads, prims=()):
    n_in = len(ins)
    ct_flat = [p for c in cts for p in c]
    n_ct = len(ct_flat)
    didx = [g[0] for g in grads]

    def body(*refs):
        in_refs, ct_refs = refs[:n_in], refs[n_in:n_in + n_ct]
        g_refs = refs[n_in + n_ct:n_in + n_ct + len(grads)]
        p_refs = refs[n_in + n_ct + len(grads):]
        vals = [r[...] for r in in_refs]

        def fd(*dv):
            full = list(vals)
            for i, v in zip(didx, dv):
                full[i] = v
            res = f(*full)
            return tuple(res) if isinstance(res, (tuple, list)) else (res,)

        out, vjp = jax.vjp(fd, *[vals[i] for i in didx])
        ctv, k = [], 0
        for o, c in zip(out, cts):
            acc = None
            for _ in c:
                piece = ct_refs[k][...].astype(o.dtype)
                acc = piece if acc is None else acc + piece
                k += 1
            ctv.append(jnp.zeros_like(o) if acc is None else acc.reshape(o.shape))
        gs = vjp(tuple(ctv))
        ids = [pl.program_id(a) for a in range(len(grid))]

        def emit(ref, val, first):
            if first is None:
                ref[...] = val.astype(ref.dtype)
            else:
                is_first = first(*ids)

                @pl.when(is_first)
                def _():
                    ref[...] = val.astype(ref.dtype)

                @pl.when(jnp.logical_not(is_first))
                def _():
                    ref[...] += val.astype(ref.dtype)

        for g, spec, ref in zip(gs, grads, g_refs):
            emit(ref, g, spec[5])
        for spec, ref in zip(prims, p_refs):
            emit(ref, out[spec[0]], spec[5])

    specs = list(grads) + list(prims)
    res = pl.pallas_call(
        body, grid=grid,
        in_specs=[pl.BlockSpec(b, m) for _, b, m in list(ins) + ct_flat],
        out_specs=[pl.BlockSpec(s[3], s[4]) for s in specs],
        out_shape=[jax.ShapeDtypeStruct(s[1], s[2]) for s in specs],
        name=name, compiler_params=_cparams(("arbitrary",) * len(grid)),
    )(*[a for a, _, _ in list(ins) + ct_flat])
    return res


def mm(name, pairs, ta=False, tb=False, out_dtype=F32):
    a0, b0, _ = pairs[0]
    m = a0.shape[1] if ta else a0.shape[0]
    n = b0.shape[0] if tb else b0.shape[1]
    cands = (1024, 1408, 768, 512, 256, 128)
    tks, nks = [], []
    for a, b, _ in pairs:
        k = a.shape[0] if ta else a.shape[1]
        tk = _pick(k, cands)
        tks.append(tk)
        nks.append(k // tk)

    def vmem_bytes(tm, tn):
        tiles = sum(2 * tk * (tm * a.dtype.itemsize + tn * b.dtype.itemsize) for (a, b, _), tk in zip(pairs, tks))
        return tiles + tm * tn * (4 + 2 * jnp.dtype(out_dtype).itemsize)

    tm_c = [c_ for c_ in cands if m % c_ == 0] or [m]
    tn_c = [c_ for c_ in cands if n % c_ == 0] or [n]
    tm, tn = tm_c[0], tn_c[0]
    while vmem_bytes(tm, tn) > MM_VMEM_BUDGET and (len(tm_c) > 1 or len(tn_c) > 1):
        if len(tm_c) > 1 and (tm >= tn or len(tn_c) == 1):
            tm_c = tm_c[1:]
        else:
            tn_c = tn_c[1:]
        tm, tn = tm_c[0], tn_c[0]
    starts = [sum(nks[:p]) for p in range(len(pairs))]
    nk = sum(nks)
    np_ = len(pairs)

    def body(*refs):
        o_ref, acc = refs[2 * np_], refs[2 * np_ + 1]
        kk = pl.program_id(2)

        @pl.when(kk == 0)
        def _():
            acc[...] = jnp.zeros_like(acc)

        for p in range(np_):
            def add(p=p):
                a = refs[2 * p][...].astype(BF16)
                b = refs[2 * p + 1][...].astype(BF16)
                dn = (((0 if ta else 1,), (1 if tb else 0,)), ((), ()))
                acc[...] += lax.dot_general(a, b, dn, preferred_element_type=F32)
            if np_ == 1:
                add()
            else:
                pl.when((kk >= starts[p]) & (kk < starts[p] + nks[p]))(add)

        @pl.when(kk == nk - 1)
        def _():
            o_ref[...] = acc[...].astype(o_ref.dtype)

    in_specs, args = [], []
    for p, (a, b, off) in enumerate(pairs):
        tk, s0, nkp = tks[p], starts[p], nks[p]
        assert off % tk == 0
        boff = off // tk

        def kloc(k, s0=s0, nkp=nkp):
            return jnp.clip(k - s0, 0, nkp - 1)
        if ta:
            in_specs.append(pl.BlockSpec((tk, tm), lambda i, j, k, kloc=kloc: (kloc(k), i)))
        else:
            in_specs.append(pl.BlockSpec((tm, tk), lambda i, j, k, kloc=kloc: (i, kloc(k))))
        if tb:
            in_specs.append(pl.BlockSpec((tn, tk), lambda i, j, k, kloc=kloc, boff=boff: (j, boff + kloc(k))))
        else:
            in_specs.append(pl.BlockSpec((tk, tn), lambda i, j, k, kloc=kloc, boff=boff: (boff + kloc(k), j)))
        args += [a, b]
    return pl.pallas_call(
        body, grid=(m // tm, n // tn, nk), in_specs=in_specs,
        out_specs=pl.BlockSpec((tm, tn), lambda i, j, k: (i, j)),
        out_shape=jax.ShapeDtypeStruct((m, n), out_dtype),
        scratch_shapes=[pltpu.VMEM((tm, tn), F32)],
        name=name, compiler_params=_cparams(("arbitrary", "arbitrary", "arbitrary")),
    )(*args)


def _rms(x, g):
    return x * lax.rsqrt(jnp.mean(x * x, axis=-1, keepdims=True) + RMS_EPS) * g


def f_nm0(x, g, sh, sc):
    return _rms(x, g) * (1.0 + sc) + sh


def f_nm(xp, o, gate, g, sh, sc):
    x = xp + gate * o
    return x, _rms(x, g) * (1.0 + sc) + sh


def f_final(xp, o, gate, g, tgt, valid):
    x = xp + gate * o
    e = (_rms(x, g) - tgt) * valid
    return jnp.sum(e * e, axis=0, keepdims=True) * (0.5 / D)


@functools.partial(jax.custom_vjp, nondiff_argnums=(1,))
def _sroll(x, s):
    return pltpu.roll(x, s, 0)


def _sroll_fwd(x, s):
    return pltpu.roll(x, s, 0), None


def _sroll_bwd(s, _, g):
    return (pltpu.roll(g, (g.shape[0] - s) % g.shape[0], 0),)


_sroll.defvjp(_sroll_fwd, _sroll_bwd)


def _shifted(x, o):
    n = x.shape[0]
    return x if o == 0 else _sroll(x, (n - o) % n)


def f_conv1d(x, w, b, *, lc, act):
    n = x.shape[0]
    pos = lax.broadcasted_iota(jnp.int32, (n, 1), 0)
    lo = jnp.where(pos < lc, 0, lc)
    hi = jnp.where(pos < lc, lc, n)
    y = x * w[1:2] + b
    for k, o in ((0, -1), (2, 1), (3, 2)):
        src = pos + o
        valid = (src >= lo) & (src < hi)
        y = y + jnp.where(valid, _shifted(x, o), 0.0) * w[k:k + 1]
    return jax.nn.silu(y) if act else y


def ffnconv_masks(n, lc):
    pos = lax.broadcasted_iota(jnp.int32, (n, 128), 0)
    is_ctx = pos < lc
    tl = pos - lc
    r = tl // GRID_W
    cc = tl - r * GRID_W
    rows = (n - lc) // GRID_W
    left = jnp.where(is_ctx, pos >= 1, cc >= 1)
    right = jnp.where(is_ctx, pos < lc - 1, cc < GRID_W - 1)
    above = jnp.logical_not(is_ctx) & (r >= 1)
    below = jnp.logical_not(is_ctx) & (r < rows - 1)
    return jnp.stack([left, right, above, below]).astype(F32)


def f_ffnconv(a, up, w, b, mk):
    cols = (mk[0] * _shifted(a, -1), a, mk[1] * _shifted(a, 1))
    y = b
    for dr in (-1, 0, 1):
        k = 3 * (dr + 1)
        inner = cols[0] * w[k:k + 1] + cols[1] * w[k + 1:k + 2] + cols[2] * w[k + 2:k + 3]
        y = y + (inner if dr == 0 else mk[2 + (dr > 0)] * _shifted(inner, GRID_W * dr))
    return jax.nn.silu(y) * up


def f_ssd(xs, bc, dtraw, bias, alog, st, *, d, reverse):
    L = xs.shape[0]
    dtv = jax.nn.softplus(dtraw + bias)
    la = dtv * (-jnp.exp(alog))
    ri = lax.broadcasted_iota(jnp.int32, (L, L), 0)
    ci = lax.broadcasted_iota(jnp.int32, (L, L), 1)
    mask = (ci >= ri) if reverse else (ci <= ri)
    cum = jnp.dot(mask.astype(F32), la, precision=HI, preferred_element_type=F32)
    cum_t = cum.T
    tot = cum[0:1] if reverse else cum[L - 1:L]
    lo = lax.broadcasted_iota(jnp.int32, (1, 128), 1) < 64
    rlo = lax.broadcasted_iota(jnp.int32, (128, 1), 0) < 64
    ys, new = [], []
    for g in range(2):
        bg = bc[:, g * 128:(g + 1) * 128].astype(BF16)
        cg = bc[:, 256 + g * 128:256 + (g + 1) * 128].astype(BF16)
        cb = lax.dot_general(cg, bg, (((1,), (1,)), ((), ())), preferred_element_type=F32)
        sg = st[4 * g:4 * g + 4].reshape(4 * 128, 128)
        ch_all = lax.dot_general(cg, sg.astype(BF16), (((1,), (1,)), ((), ())), preferred_element_type=F32)
        xes, dcols = [], []
        for jj in range(4):
            j = 4 * g + jj
            x = xs[:, j * 128:(j + 1) * 128]
            k1 = 16 * d + 2 * j
            k2 = k1 + 1
            c1, c2 = cum[:, k1:k1 + 1], cum[:, k2:k2 + 1]
            m1 = cb * jnp.exp(jnp.where(mask, c1 - cum_t[k1:k1 + 1, :], -1e30))
            m2 = cb * jnp.exp(jnp.where(mask, c2 - cum_t[k2:k2 + 1, :], -1e30))
            xdt = x * jnp.where(lo, dtv[:, k1:k1 + 1], dtv[:, k2:k2 + 1])
            mcat = jnp.concatenate([m1, m2], axis=1).astype(BF16)
            xcat = jnp.concatenate([jnp.where(lo, xdt, 0.0), jnp.where(lo, 0.0, xdt)], axis=0).astype(BF16)
            y = jnp.dot(mcat, xcat, preferred_element_type=F32)
            y = y + ch_all[:, jj * 128:(jj + 1) * 128] * jnp.where(lo, jnp.exp(c1), jnp.exp(c2))
            t1, t2 = tot[:, k1:k1 + 1], tot[:, k2:k2 + 1]
            xes.append((xdt * jnp.where(lo, jnp.exp(t1 - c1), jnp.exp(t2 - c2))).astype(BF16))
            dcols.append(jnp.where(rlo, jnp.exp(t1), jnp.exp(t2)))
            ys.append(y)
        upd = lax.dot_general(jnp.concatenate(xes, axis=1), bg, (((0,), (0,)), ((), ())), preferred_element_type=F32)
        new.append((sg * jnp.concatenate(dcols, axis=0) + upd).reshape(4, 128, 128))
    return jnp.concatenate(ys, axis=1), jnp.concatenate(new, axis=0)


def f_hgrn(q_raw, f_raw, v, lb, zt, *, reverse, chunk):
    n = q_raw.shape[0]
    c = chunk
    qa = jax.nn.silu(q_raw)
    logf = jnp.log(lb + (1.0 - lb) * jax.nn.sigmoid(f_raw))
    kk = (1.0 - lb) * jax.nn.sigmoid(-f_raw)
    ri = lax.broadcasted_iota(jnp.int32, (n, n), 0)
    ci = lax.broadcasted_iota(jnp.int32, (n, n), 1)
    tmat = ((ri // c == ci // c) & ((ci >= ri) if reverse else (ci <= ri))).astype(F32)
    cum_all = jnp.dot(tmat, logf, precision=HI, preferred_element_type=F32)
    r3 = lax.broadcasted_iota(jnp.int32, (c, c, 128), 0)
    c3 = lax.broadcasted_iota(jnp.int32, (c, c, 128), 1)
    mask3 = (c3 >= r3) if reverse else (c3 <= r3)
    nch = n // c
    outs = [None] * nch
    for chn in (reversed(range(nch)) if reverse else range(nch)):
        sl = slice(chn * c, (chn + 1) * c)
        q, k, vv, cum = qa[sl], kk[sl], v[sl], cum_all[sl]
        dec = jnp.exp(jnp.where(mask3, cum[:, None, :] - cum[None, :, :], -1e30))
        att = jnp.sum(q[:, None, :] * dec * k[None, :, :], axis=-1, keepdims=True)
        y = jnp.sum(att * vv[None, :, :], axis=1)
        y = y + lax.dot_general((q * jnp.exp(cum)).astype(BF16), zt.astype(BF16), (((1,), (1,)), ((), ())),
                                preferred_element_type=F32)
        tot = cum[0:1] if reverse else cum[c - 1:c]
        kd = (k * jnp.exp(tot - cum)).astype(BF16)
        zt = zt * jnp.exp(tot) + lax.dot_general(vv.astype(BF16), kd, (((0,), (0,)), ((), ())), preferred_element_type=F32)
        outs[chn] = y
    return jnp.concatenate(outs, axis=0), zt


def f_hgrn_group(q_raw, f_raw, v, lb, zt, *, reverse, chunk):
    ys, zs = [], []
    for h in range(q_raw.shape[1] // 128):
        sl = slice(h * 128, (h + 1) * 128)
        y, z = f_hgrn(q_raw[:, sl], f_raw[:, sl], v[:, sl], lb[:, sl], zt[sl], reverse=reverse, chunk=chunk)
        ys.append(y)
        zs.append(z)
    return jnp.concatenate(ys, axis=1), jnp.concatenate(zs, axis=0)


def _expm1(x):
    poly = x * (1.0 + x * (0.5 + x * (1.0 / 6 + x * (1.0 / 24))))
    return jnp.where(jnp.abs(x) < 0.1, poly, jnp.exp(x) - 1.0)


def f_gates(u, wa, ba, wi, bi, lam):
    rs, is_ = [], []
    for nb in range(8):
        un = u[:, nb * 128:(nb + 1) * 128].astype(BF16)
        rs.append(jnp.dot(un, wa[nb].astype(BF16), preferred_element_type=F32))
        is_.append(jnp.dot(un, wi[nb].astype(BF16), preferred_element_type=F32))
    r = jax.nn.sigmoid(jnp.concatenate(rs, axis=1) + ba)
    i = jax.nn.sigmoid(jnp.concatenate(is_, axis=1) + bi)
    log_a = -LRU_C * jax.nn.softplus(-lam) * r
    return jnp.exp(log_a), jnp.sqrt(-_expm1(2.0 * log_a)) * (i * u)


def f_ssdfin(y0, y1, xs, z, gy, h0, h1, dpad, ng):
    kk = lax.broadcasted_iota(jnp.int32, (128, D), 0)
    ch = lax.broadcasted_iota(jnp.int32, (128, D), 1)
    expand = (ch // 64 == kk).astype(F32)
    dvec = jnp.dot(dpad, expand, precision=HI, preferred_element_type=F32)[0:1]
    y = y0 + y1 + dvec * xs
    yn = _rms(y * jax.nn.silu(z), ng)
    r = (h0 + h1) * jax.nn.gelu(gy)
    return jnp.concatenate([yn, r], axis=1)


def f_oddfin(o0, o1, g, y0, y1, u, hn, sd, gw, gb):
    parts = []
    for h in range(HG_HEADS):
        sl = slice(h * 128, (h + 1) * 128)
        parts.append(_rms(o0[:, sl] + o1[:, sl], hn[h:h + 1]) * jax.nn.silu(g[:, sl]))
    y = jax.nn.gelu(y0 + y1 + sd * u)
    y = y * jax.nn.sigmoid(jnp.dot(y.astype(BF16), gw.astype(BF16), preferred_element_type=F32) + gb)
    return jnp.concatenate(parts + [y], axis=1)


def f_s5p(lre, lim, lstep, btr, bti):
    step = jnp.exp(lstep)
    mag = jnp.exp(lre * step)
    ar, ai = mag * jnp.cos(lim * step), mag * jnp.sin(lim * step)
    den = lre * lre + lim * lim
    zr = ((ar - 1.0) * lre + ai * lim) / den
    zi = (ai * lre - (ar - 1.0) * lim) / den
    bbr = zr[:, None, :] * btr - zi[:, None, :] * bti
    bbi = zr[:, None, :] * bti + zi[:, None, :] * btr
    return ar, ai, bbr, bbi


def f_lb(logits):
    m = jnp.max(logits, axis=0, keepdims=True)
    e = jnp.exp(logits - m)
    p = e / jnp.sum(e, axis=0, keepdims=True)
    return p[1:2], p[1:2] + p[2:3] + p[3:4]


def f_adamw(w, m, v, *gs):
    g = gs[0]
    for t in gs[1:]:
        g = g + t
    m = ADAM_B1 * m + (1.0 - ADAM_B1) * g
    v = ADAM_B2 * v + (1.0 - ADAM_B2) * jnp.square(g)
    m_hat = m / (1.0 - ADAM_B1 ** ADAM_STEP)
    v_hat = v / (1.0 - ADAM_B2 ** ADAM_STEP)
    delta = -ADAM_LR * (m_hat / (jnp.sqrt(v_hat) + ADAM_EPS) + ADAM_WD * w)
    return g, delta, m, v


def scan_fwd(name, f, grid, ins, y_out, st_out, state_shape, is_first):
    n_in = len(ins)

    def body(*refs):
        y_ref, so_ref, st = refs[n_in], refs[n_in + 1], refs[n_in + 2]
        ids = [pl.program_id(a) for a in range(len(grid))]

        @pl.when(is_first(*ids))
        def _():
            st[...] = jnp.zeros_like(st)

        s = st[...]
        so_ref[...] = s
        y, new = f(*[r[...] for r in refs[:n_in]], s)
        y_ref[...] = y.astype(y_ref.dtype)
        st[...] = new

    return pl.pallas_call(
        body, grid=grid,
        in_specs=[pl.BlockSpec(b, m) for _, b, m in ins],
        out_specs=[pl.BlockSpec(y_out[2], y_out[3]), pl.BlockSpec(st_out[2], st_out[3])],
        out_shape=[jax.ShapeDtypeStruct(y_out[0], y_out[1]), jax.ShapeDtypeStruct(st_out[0], st_out[1])],
        scratch_shapes=[pltpu.VMEM(state_shape, F32)],
        name=name, compiler_params=_cparams(("arbitrary",) * len(grid)),
    )(*[a for a, _, _ in ins])


def scan_bwd(name, f, grid, ins, st_in, dy, grads, state_shape, is_first):
    n_in = len(ins)
    didx = [g[0] for g in grads]

    def body(*refs):
        s_ref, dy_ref = refs[n_in], refs[n_in + 1]
        g_refs = refs[n_in + 2:n_in + 2 + len(grads)]
        dst = refs[n_in + 2 + len(grads)]
        ids = [pl.program_id(a) for a in range(len(grid))]

        @pl.when(is_first(*ids))
        def _():
            dst[...] = jnp.zeros_like(dst)

        vals = [r[...] for r in refs[:n_in]]

        def fd(s, *dv):
            full = list(vals)
            for i, v in zip(didx, dv):
                full[i] = v
            return f(*full, s)

        (y, _), vjp = jax.vjp(fd, s_ref[...], *[vals[i] for i in didx])
        gs = vjp((dy_ref[...].astype(y.dtype), dst[...]))
        dst[...] = gs[0]
        for g, spec, ref in zip(gs[1:], grads, g_refs):
            first = spec[5]
            if first is None:
                ref[...] = g.astype(ref.dtype)
            else:
                fst = first(*ids)

                @pl.when(fst)
                def _(ref=ref, g=g):
                    ref[...] = g.astype(ref.dtype)

                @pl.when(jnp.logical_not(fst))
                def _(ref=ref, g=g):
                    ref[...] += g.astype(ref.dtype)

    allin = list(ins) + [st_in, dy]
    return pl.pallas_call(
        body, grid=grid,
        in_specs=[pl.BlockSpec(b, m) for _, b, m in allin],
        out_specs=[pl.BlockSpec(s[3], s[4]) for s in grads],
        out_shape=[jax.ShapeDtypeStruct(s[1], s[2]) for s in grads],
        scratch_shapes=[pltpu.VMEM(state_shape, F32)],
        name=name, compiler_params=_cparams(("arbitrary",) * len(grid)),
    )(*[a for a, _, _ in allin])


def _tile_order(order, nt, nctx=1):
    rev = lambda j: jnp.where(j < nctx, nctx - 1 - j, nt - 1 - (j - nctx))
    if order == 'F':
        return (lambda j: j), True
    if order == 'Fb':
        return (lambda j: nt - 1 - j), False
    if order == 'R':
        return rev, False
    return (lambda j: rev(nt - 1 - j)), True


def _scan8(coef, val, sub, asc):
    for step in (1, 2, 4):
        shift = step if asc else 8 - step
        keep = (sub >= step) if asc else (sub < 8 - step)
        val = jnp.where(keep, coef * pltpu.roll(val, shift, 0) + val, val)
        coef = jnp.where(keep, coef * pltpu.roll(coef, shift, 0), coef)
    return coef, val


def _prev_rows(tile, carry, sub, asc):
    return jnp.where(sub == 0, carry, pltpu.roll(tile, 1, 0)) if asc else jnp.where(sub == 7, carry, pltpu.roll(tile, 7, 0))


def _last_row(tile, asc):
    return jnp.broadcast_to(tile[7:8] if asc else tile[0:1], tile.shape)


def linrec(name, a, b, order):
    bsz, tt, cols = a.shape
    tq = _pick(tt, (256, 128))
    nt, ng, nj = tt // tq, tq // 8, cols // 128
    phys, asc = _tile_order(order, nt)

    def body(a_ref, b_ref, h_ref, hp_ref, hc):
        @pl.when(pl.program_id(0) == 0)
        def _():
            hc[...] = jnp.zeros_like(hc)

        sub = lax.broadcasted_iota(jnp.int32, (8, 128), 0)

        def group(i, carry):
            rows = pl.ds(pl.multiple_of((i if asc else ng - 1 - i) * 8, 8), 8)
            for bi in range(bsz):
                for j in range(nj):
                    cs = slice(j * 128, (j + 1) * 128)
                    h_in = hc[bi, j]
                    ca, cv = _scan8(a_ref[bi, rows, cs], b_ref[bi, rows, cs], sub, asc)
                    h = ca * h_in + cv
                    h_ref[bi, rows, cs] = h
                    hp_ref[bi, rows, cs] = _prev_rows(h, h_in, sub, asc)
                    hc[bi, j] = _last_row(h, asc)
            return carry

        lax.fori_loop(0, ng, group, 0)

    spec = pl.BlockSpec((bsz, tq, cols), lambda j: (0, phys(j), 0))
    return pl.pallas_call(
        body, grid=(nt,), in_specs=[spec, spec], out_specs=[spec, spec],
        out_shape=[jax.ShapeDtypeStruct(a.shape, F32)] * 2,
        scratch_shapes=[pltpu.VMEM((bsz, nj, 8, 128), F32)],
        name=name, compiler_params=_cparams(("arbitrary",)),
    )(a, b)


def linrec_bwd(name, a, dh, hprev, order):
    bsz, tt, cols = a.shape
    tq = _pick(tt, (256, 128))
    nt, ng, nj = tt // tq, tq // 8, cols // 128
    phys, asc = _tile_order(order, nt)

    def body(a_ref, dh_ref, hp_ref, g_ref, ga_ref, gc, ac):
        @pl.when(pl.program_id(0) == 0)
        def _():
            gc[...] = jnp.zeros_like(gc)
            ac[...] = jnp.zeros_like(ac)

        sub = lax.broadcasted_iota(jnp.int32, (8, 128), 0)

        def group(i, carry):
            rows = pl.ds(pl.multiple_of((i if asc else ng - 1 - i) * 8, 8), 8)
            for bi in range(bsz):
                for j in range(nj):
                    cs = slice(j * 128, (j + 1) * 128)
                    a_tile = a_ref[bi, rows, cs]
                    ca, cv = _scan8(_prev_rows(a_tile, ac[bi, j], sub, asc), dh_ref[bi, rows, cs], sub, asc)
                    g = ca * gc[bi, j] + cv
                    g_ref[bi, rows, cs] = g
                    ga_ref[bi, rows, cs] = g * hp_ref[bi, rows, cs]
                    gc[bi, j] = _last_row(g, asc)
                    ac[bi, j] = _last_row(a_tile, asc)
            return carry

        lax.fori_loop(0, ng, group, 0)

    spec = pl.BlockSpec((bsz, tq, cols), lambda j: (0, phys(j), 0))
    return pl.pallas_call(
        body, grid=(nt,), in_specs=[spec, spec, spec], out_specs=[spec, spec],
        out_shape=[jax.ShapeDtypeStruct(a.shape, F32)] * 2,
        scratch_shapes=[pltpu.VMEM((bsz, nj, 8, 128), F32), pltpu.VMEM((bsz, nj, 8, 128), F32)],
        name=name, compiler_params=_cparams(("arbitrary",)),
    )(a, dh, hprev)


def _cmul(a, b):
    return a[0] * b[0] - a[1] * b[1], a[0] * b[1] + a[1] * b[0]


def _cpow_tables(ar, ai, asc):
    pows = [(ar, ai)]
    for _ in range(7):
        pows.append(_cmul(pows[-1], (ar, ai)))
    tile = lambda p: jnp.broadcast_to(p[:, None, :], (8, 8, 128))
    steps = jnp.stack([jnp.stack([tile(pows[s - 1][0]), tile(pows[s - 1][1])]) for s in (1, 2, 4)])
    order = range(8) if asc else range(7, -1, -1)
    carry = jnp.stack([jnp.stack([pows[i][c] for i in order], axis=1) for c in (0, 1)])
    return steps, carry


def _cscan8(xr, xi, st_ref, j, sub, asc):
    for s, step in enumerate((1, 2, 4)):
        shift = step if asc else 8 - step
        keep = (sub >= step) if asc else (sub < 8 - step)
        pr, pi = st_ref[s, 0, j], st_ref[s, 1, j]
        rr, ri = pltpu.roll(xr, shift, 0), pltpu.roll(xi, shift, 0)
        xr, xi = jnp.where(keep, xr + pr * rr - pi * ri, xr), jnp.where(keep, xi + pr * ri + pi * rr, xi)
    return xr, xi


def clinrec(name, x, coef, order, lc, conj=False, hprev=None):
    btot, tt, cols2 = x.shape
    cols = cols2 // 2
    bsz, ngrp = btot, 1
    tq = 128
    nt, ng, nj = tt // tq, tq // 8, cols // 128
    phys, asc = _tile_order(order, nt, lc // tq)
    steps, carry = _cpow_tables(coef[0], -coef[1] if conj else coef[1], asc)
    adjoint = hprev is not None

    def body(*refs):
        if adjoint:
            x_ref, hp_ref, st_ref, cr_ref, h_ref, dc_ref, hc = refs
        else:
            x_ref, st_ref, cr_ref, h_ref, hp_ref, hc = refs

        @pl.when(pl.program_id(1) == 0)
        def _():
            hc[...] = jnp.zeros_like(hc)

        if adjoint:
            @pl.when((pl.program_id(0) == 0) & (pl.program_id(1) == 0))
            def _():
                dc_ref[...] = jnp.zeros_like(dc_ref)

        sub = lax.broadcasted_iota(jnp.int32, (8, 128), 0)

        def group(i, c_):
            rows = pl.ds(pl.multiple_of((i if asc else ng - 1 - i) * 8, 8), 8)
            for bi in range(bsz):
                for j in range(nj):
                    cr, ci = slice(j * 128, (j + 1) * 128), slice(cols + j * 128, cols + (j + 1) * 128)
                    sr, si = _cscan8(x_ref[bi, rows, cr], x_ref[bi, rows, ci], st_ref, j, sub, asc)
                    in_r, in_i = hc[bi, 0, j], hc[bi, 1, j]
                    pr, pi = cr_ref[0, j], cr_ref[1, j]
                    hr = sr + pr * in_r - pi * in_i
                    hi = si + pr * in_i + pi * in_r
                    h_ref[bi, rows, cr] = hr
                    h_ref[bi, rows, ci] = hi
                    if adjoint:
                        qr, qi = hp_ref[bi, rows, cr], hp_ref[bi, rows, ci]
                        dc_ref[0, j] += hr * qr + hi * qi
                        dc_ref[1, j] += hi * qr - hr * qi
                    else:
                        hp_ref[bi, rows, cr] = _prev_rows(hr, in_r, sub, asc)
                        hp_ref[bi, rows, ci] = _prev_rows(hi, in_i, sub, asc)
                    hc[bi, 0, j] = _last_row(hr, asc)
                    hc[bi, 1, j] = _last_row(hi, asc)
            return c_

        lax.fori_loop(0, ng, group, 0)

    spec = pl.BlockSpec((bsz, tq, cols2), lambda g, j: (g, phys(j), 0))
    full = lambda t: pl.BlockSpec(t.shape, lambda g, j, n=t.ndim: (0,) * n)
    dc_shape = (2, nj, 8, 128)
    if adjoint:
        ins, in_specs = (x, hprev, steps, carry), [spec, spec, full(steps), full(carry)]
        out_specs = [spec, pl.BlockSpec(dc_shape, lambda g, j: (0, 0, 0, 0))]
        out_shape = [jax.ShapeDtypeStruct(x.shape, F32), jax.ShapeDtypeStruct(dc_shape, F32)]
    else:
        ins, in_specs = (x, steps, carry), [spec, full(steps), full(carry)]
        out_specs = [spec, spec]
        out_shape = [jax.ShapeDtypeStruct(x.shape, F32)] * 2
    return pl.pallas_call(
        body, grid=(ngrp, nt), in_specs=in_specs, out_specs=out_specs, out_shape=out_shape,
        scratch_shapes=[pltpu.VMEM((bsz, 2, nj, 8, 128), F32)],
        name=name, compiler_params=_cparams(("arbitrary", "arbitrary")),
    )(*ins)


def _blockdiag(bb):
    eye = jnp.eye(16, dtype=bb.dtype)
    return (bb[:, :, None, :] * eye[:, None, :, None]).reshape(256, 1024)


def _blockdiag_t(c):
    eye = jnp.eye(16, dtype=c.dtype)
    return (jnp.swapaxes(c, 1, 2)[:, :, None, :] * eye[:, None, :, None]).reshape(1024, 256)


def _unblockdiag(m):
    eye = jnp.eye(16, dtype=m.dtype)
    return jnp.sum(m.reshape(16, 16, 16, 64) * eye[:, None, :, None], axis=2)


def _unblockdiag_t(m):
    eye = jnp.eye(16, dtype=m.dtype)
    return jnp.swapaxes(jnp.sum(m.reshape(16, 64, 16, 16) * eye[:, None, :, None], axis=2), 1, 2)


def _pad_rows(v, rows=8, cols=128):
    out = jnp.zeros((rows, cols), F32)
    return out.at[0, :v.shape[0]].set(v)


def cond_rows(c, c_ctx):
    return jnp.zeros((8, D), F32).at[:c.shape[0]].set(c).at[4].set(c_ctx)


def mod_table(name, cc_rows, w, b):
    nl, _, cols = w.shape
    nr = cc_rows.shape[0]

    def f(ccv, wv, bv):
        return jnp.dot(jax.nn.silu(ccv).astype(BF16), wv, preferred_element_type=F32) + bv

    return tile_fwd(name, f, (nl, cols // 1536),
                    [(cc_rows, (nr, D), lambda l, n: (0, 0)), (w, (None, D, 1536), lambda l, n: (l, 0, n)),
                     (b.reshape(nl, 1, cols), (None, 1, 1536), lambda l, n: (l, 0, n))],
                    [((nl, nr, cols), F32, (None, nr, 1536), lambda l, n: (l, 0, n))])[0]


def mod_cond_grad(name, cc_rows, dmod_rows, w):
    nl, nr, cols = dmod_rows.shape

    def body(cc_ref, d_ref, w_ref, o_ref):
        _, vjp = jax.vjp(jax.nn.silu, cc_ref[...])
        ds = lax.dot_general(d_ref[...].astype(BF16), w_ref[...], (((1,), (1,)), ((), ())), preferred_element_type=F32)
        (dcc,) = vjp(ds)
        first = (pl.program_id(0) == 0) & (pl.program_id(1) == 0)

        @pl.when(first)
        def _():
            o_ref[...] = dcc

        @pl.when(jnp.logical_not(first))
        def _():
            o_ref[...] += dcc

    return pl.pallas_call(
        body, grid=(nl, cols // 1536),
        in_specs=[pl.BlockSpec((nr, D), lambda l, n: (0, 0)), pl.BlockSpec((None, nr, 1536), lambda l, n: (l, 0, n)),
                  pl.BlockSpec((None, D, 1536), lambda l, n: (l, 0, n))],
        out_specs=pl.BlockSpec((nr, D), lambda l, n: (0, 0)), out_shape=jax.ShapeDtypeStruct((nr, D), F32),
        name=name, compiler_params=_cparams(("arbitrary", "arbitrary")),
    )(cc_rows, dmod_rows, w)


def local_step(x, ctx, target, mod, W):
    B, Tx, _ = x.shape
    Lc = ctx.shape[1]
    Tt = Lc + Tx
    tb = Lc
    nt = Tt // tb
    M = B * Tt
    nc = Tt // SSD_CHUNK
    ncc = Lc // SSD_CHUNK
    fgr, fgw = HG_HEADS // HG_FWD[1], 128 * HG_FWD[1]
    bgr, bgw = HG_HEADS // HG_BWD[1], 128 * HG_BWD[1]
    G = {}

    def add_grad(name, idx, val):
        G.setdefault(name, {})[idx] = val

    def tok(a, cb=None, off=0):
        cb = a.shape[-1] if cb is None else cb
        return (a, (None, tb, cb), lambda b, j, off=off: (b, j, off))

    def tok_out(cols, dtype=F32):
        return ((B, Tt, cols), dtype, (None, tb, cols), lambda b, j: (b, j, 0))

    def vec(a):
        return (a, a.shape, lambda *ids, n=a.ndim: (0,) * n)

    def vec_acc(shape):
        return (shape, F32, shape, lambda *ids, n=len(shape): (0,) * n, lambda *ids: functools.reduce(jnp.logical_and, [i == 0 for i in ids]))

    def modv(l, which):
        return (modr, (None, None, None, 1, D), lambda b, j, l=l, which=which: (l, jnp.where(j == 0, 4, b), which, 0, 0))

    dmod_spec = ((B, 2, 1, D), F32, (None, None, 1, D), lambda b, j: (b, jnp.where(j == 0, 0, 1), 0, 0), lambda b, j: j <= 1)

    def phys_chunk(n_all, n_ctx, reverse):
        if not reverse:
            return lambda s: s
        return lambda s: jnp.where(s < n_ctx, n_ctx - 1 - s, n_all - 1 - (s - n_ctx))

    nmc = N_MOD * D // 1536
    modr = mod.reshape(DEPTH, 8, N_MOD, 1, D)
    dmods = {}

    lb1, lb3 = tile_fwd('lb_fwd', f_lb, (1,), [vec(W['hg_lb_logits'])],
                        [((1, HG_W), F32, (1, HG_W), lambda i: (0, 0))] * 2)
    dlb = {1: [], 3: []}

    x0 = jnp.concatenate([ctx, x], axis=1)
    conv_mk = ffnconv_masks(Tt, Lc)
    R = [dict() for _ in range(DEPTH)]

    xprev, oprev = x0, None
    for l in range(DEPTH):
        r = R[l]
        j = l // 2
        ng = W['norm_mix_g'][l][None]
        if l == 0:
            h1 = tile_fwd(f'nm0_fwd', f_nm0, (B, nt), [tok(xprev), vec(ng), modv(l, 0), modv(l, 1)], [tok_out(D, BF16)])[0]
            xa = xprev
        else:
            xa, h1 = tile_fwd(f'nm_mix_fwd{l}', f_nm, (B, nt),
                              [tok(xprev), tok(oprev), modv(l - 1, 5), vec(ng), modv(l, 0), modv(l, 1)],
                              [tok_out(D), tok_out(D, BF16)])
        r['xin'], r['oin'], r['xa'], r['h1'] = xprev, oprev, xa, h1
        h1m = h1.reshape(M, D)
        if l % 2 == 0:
            win = W['ev_w_in_p'][j]
            proj = mm(f'ev_proj{l}', [(h1m, win, 0)]).reshape(B, Tt, EV_PAD)
            r['proj'] = proj
            scw, scb = W['ssd_conv_w'][j], W['ssd_conv_b'][j][None]
            lcw, lcb = W['lru_conv_w'][j], W['lru_conv_b'][j][None]

            def conv_call(name, colblk0, w, b, wblk0, ncols, act):
                return tile_fwd(name, functools.partial(f_conv1d, lc=Lc, act=act), (ncols // 256, B),
                                [(proj, (None, Tt, 256), lambda cb, bi: (bi, 0, colblk0 + cb)),
                                 (w, (4, 256), lambda cb, bi: (0, wblk0 + cb)), (b, (1, 256), lambda cb, bi: (0, wblk0 + cb))],
                                [((B, Tt, ncols), F32, (None, Tt, 256), lambda cb, bi: (bi, 0, cb))])[0]
            xs_c = conv_call(f'conv_xs{l}', 12, scw, scb, 0, 1024, True)
            bc_c = conv_call(f'conv_bc{l}', 16, scw, scb, 4, 512, True)
            u_c = conv_call(f'conv_u{l}', 8, lcw, lcb, 0, 1024, False)
            r['xs'], r['bc'], r['u'] = xs_c, bc_c, u_c
            bias = _pad_rows(W['ssd_dt_bias'][j].reshape(-1), 1)
            alog = _pad_rows(W['ssd_a_log'][j].reshape(-1), 1)
            r['bias'], r['alog'] = bias, alog
            r['y'], r['st'], r['a4'], r['hp4'], r['h'] = [], [], [], [], []
            for d in range(2):
                ph = phys_chunk(nc, ncc, d == 1)
                y, st = scan_fwd(
                    f'ssd_fwd{l}_{d}', functools.partial(f_ssd, d=d, reverse=(d == 1)), (B, nc),
                    [(xs_c, (None, SSD_CHUNK, D), lambda b, s, ph=ph: (b, ph(s), 0)),
                     (bc_c, (None, SSD_CHUNK, 512), lambda b, s, ph=ph: (b, ph(s), 0)),
                     (proj, (None, SSD_CHUNK, 128), lambda b, s, ph=ph: (b, ph(s), 36)),
                     vec(bias), vec(alog)],
                    ((B, Tt, D), F32, (None, SSD_CHUNK, D), lambda b, s, ph=ph: (b, ph(s), 0)),
                    ((B, nc, 8, 128, 128), F32, (None, None, 8, 128, 128), lambda b, s: (b, s, 0, 0, 0)),
                    (8, 128, 128), lambda b, s: s == 0)
                r['y'].append(y)
                r['st'].append(st)
                a_d, bx_d = tile_fwd(
                    f'gates_fwd{l}_{d}', f_gates, (B, nt),
                    [tok(u_c), vec(W['lru_w_a'][j, d]), vec(W['lru_b_a'][j, d][None]), vec(W['lru_w_i'][j, d]),
                     vec(W['lru_b_i'][j, d][None]), vec(W['lru_lam'][j, d][None])],
                    [tok_out(D), tok_out(D)])
                h_d, hp_d = linrec(f'lru_fwd{l}_{d}', a_d, bx_d, 'F' if d == 0 else 'R')
                r['a4'].append(a_d)
                r['hp4'].append(hp_d)
                r['h'].append(h_d)
            dpad = _pad_rows(W['ssd_d'][j])
            sng = W['ssd_norm_g'][j][None]
            r['dpad'], r['sng'] = dpad, sng
            mix = tile_fwd(f'ssdfin_fwd{l}', f_ssdfin, (B, nt),
                           [tok(r['y'][0]), tok(r['y'][1]), tok(xs_c), tok(proj, D, 0), tok(proj, D, 1), tok(r['h'][0]),
                            tok(r['h'][1]), vec(dpad), vec(sng)], [tok_out(2 * D, BF16)])[0]
            wout = W['ev_w_out'][j]
        else:
            win = W['od_w_in'][j]
            proj = mm(f'od_proj{l}', [(h1m, win, 0)]).reshape(B, Tt, 4096)
            r['proj'] = proj
            lbv = lb1 if l == 1 else lb3
            ns = nc
            r['o'], r['zst'], r['coef'], r['bcat'], r['ccat'], r['hp5'], r['hcat'], r['yd'], r['s5in'] = [], [], [], [], [], [], [], [], []
            u2 = proj[:, :, 3840:].reshape(M, S5_W)
            r['u2'] = u2
            for d in range(2):
                ph = phys_chunk(ns, ncc, d == 1)
                o_d, zst = scan_fwd(
                    f'hgrn_fwd{l}_{d}', functools.partial(f_hgrn_group, reverse=(d == 1), chunk=HG_FWD[0]), (fgr, B, ns),
                    [(proj, (None, 128, fgw), lambda h, b, s, ph=ph: (b, ph(s), h)),
                     (proj, (None, 128, fgw), lambda h, b, s, ph=ph, d=d: (b, ph(s), (1 + d) * fgr + h)),
                     (proj, (None, 128, fgw), lambda h, b, s, ph=ph: (b, ph(s), 3 * fgr + h)),
                     (lbv, (1, fgw), lambda h, b, s: (0, h))],
                    ((B, Tt, HG_W), F32, (None, 128, fgw), lambda h, b, s, ph=ph: (b, ph(s), h)),
                    ((B, ns, fgr, fgw, 128), F32, (None, None, None, fgw, 128), lambda h, b, s: (b, s, h, 0, 0)),
                    (fgw, 128), lambda h, b, s: s == 0)
                r['o'].append(o_d)
                r['zst'].append(zst)
                s5in = [W['s5_lam_re'][j, d], W['s5_lam_im'][j, d], W['s5_log_step'][j, d].reshape(16, 1),
                        jnp.swapaxes(W['s5_b_re'][j], 1, 2), jnp.swapaxes(W['s5_b_im'][j], 1, 2)]
                r['s5in'].append(s5in)
                ar, ai, bbr, bbi = tile_fwd(f's5p_fwd{l}_{d}', f_s5p, (1,), [vec(t) for t in s5in],
                                            [((16, 64), F32, (16, 64), lambda i: (0, 0))] * 2
                                            + [((16, 16, 64), F32, (16, 16, 64), lambda i: (0, 0, 0))] * 2)
                coef = jnp.stack([ar.reshape(8, 128), ai.reshape(8, 128)])
                bcat = jnp.concatenate([_blockdiag(bbr), _blockdiag(bbi)], axis=1).astype(BF16)
                ccat = jnp.concatenate([_blockdiag_t(W['s5_c_re'][j, d]), -_blockdiag_t(W['s5_c_im'][j, d])], axis=0).astype(BF16)
                xcat = mm(f's5_in{l}_{d}', [(u2, bcat, 0)])
                h5, hp5 = clinrec(f's5_fwd{l}_{d}', xcat.reshape(B, Tt, 2 * D), coef, 'F' if d == 0 else 'R', Lc)
                hcat = h5.reshape(M, 2 * D)
                yd = mm(f's5_out{l}_{d}', [(hcat, ccat, 0)]).reshape(B, Tt, S5_W)
                r['coef'].append(coef)
                r['bcat'].append(bcat)
                r['ccat'].append(ccat)
                r['hp5'].append(hp5)
                r['hcat'].append(hcat)
                r['yd'].append(yd)
            hn = jnp.zeros((8, 128), F32).at[:HG_HEADS].set(W['hg_norm_g'][j])
            sd, gw, gb = W['s5_d'][j][None], W['s5_glu_w'][j], W['s5_glu_b'][j][None]
            r['fin_par'] = (hn, sd, gw, gb)
            mix = tile_fwd(f'oddfin_fwd{l}', f_oddfin, (B, nt),
                           [tok(r['o'][0]), tok(r['o'][1]), tok(proj, HG_W, 4), tok(r['yd'][0]), tok(r['yd'][1]),
                            tok(proj, S5_W, 15), vec(hn), vec(sd), vec(gw), vec(gb)], [tok_out(D, BF16)])[0]
            wout = W['od_w_out'][j]
        r['mix'] = mix
        o1 = mm(f'mix_out{l}', [(mix.reshape(M, -1), wout, 0)]).reshape(B, Tt, D)
        r['o1'] = o1
        fg = W['norm_ffn_g'][l][None]
        xb, h2 = tile_fwd(f'nm_ffn_fwd{l}', f_nm, (B, nt), [tok(xa), tok(o1), modv(l, 2), vec(fg), modv(l, 3), modv(l, 4)],
                          [tok_out(D), tok_out(D, BF16)])
        r['h2'] = h2
        h2m = h2.reshape(M, D)
        a = mm(f'ffn_gate{l}', [(h2m, W['ffn_w_gate'][l], 0)]).reshape(B, Tt, D_FF)
        up = mm(f'ffn_up{l}', [(h2m, W['ffn_w_up'][l], 0)]).reshape(B, Tt, D_FF)
        w9 = W['ffn_conv_w'][l].reshape(9, D_FF)
        cbias = W['ffn_conv_b'][l][None]
        r['a'], r['up'], r['w9'], r['cbias'] = a, up, w9, cbias
        act = tile_fwd(f'ffnconv_fwd{l}', f_ffnconv, (D_FF // 128, B),
                       [(a, (None, Tt, 128), lambda cb, bi: (bi, 0, cb)), (up, (None, Tt, 128), lambda cb, bi: (bi, 0, cb)),
                        (w9, (9, 128), lambda cb, bi: (0, cb)), (cbias, (1, 128), lambda cb, bi: (0, cb)), vec(conv_mk)],
                       [((B, Tt, D_FF), BF16, (None, Tt, 128), lambda cb, bi: (bi, 0, cb))])[0]
        r['act'] = act
        o2 = mm(f'ffn_down{l}', [(act.reshape(M, D_FF), W['ffn_w_down'][l], 0)]).reshape(B, Tt, D)
        xprev, oprev = xb, o2

    vmask = jnp.ones((nt, 1, D), F32).at[0].set(0.0)
    ones = jnp.ones((1, D), F32)
    fng = W['final_norm_g'][None]
    d_xp, d_o2, dg5, dfng, loss_vec = tile_bwd(
        'loss_head', f_final, (B, nt),
        [tok(xprev), tok(oprev), modv(DEPTH - 1, 5), vec(fng),
         (target, (None, tb, D), lambda b, j: (b, jnp.maximum(j - 1, 0), 0)), (vmask, (None, 1, D), lambda b, j: (j, 0, 0))],
        [[vec(ones)]],
        [(0,) + tok_out(D) + (None,), (1,) + tok_out(D, BF16) + (None,), (2,) + dmod_spec, (3,) + vec_acc((1, D))],
        prims=[(0,) + vec_acc((1, D))])
    loss = jnp.sum(loss_vec)
    add_grad('final_norm_g', None, dfng[0])
    dmods[(DEPTH - 1, 5)] = dg5

    for l in reversed(range(DEPTH)):
        r = R[l]
        j = l // 2
        d_o2m = d_o2.reshape(M, D)
        d_act = mm(f'ffn_down_dx{l}', [(d_o2m, W['ffn_w_down'][l], 0)], tb=True).reshape(B, Tt, D_FF)
        add_grad('ffn_w_down', l, mm(f'ffn_down_dw{l}', [(r['act'].reshape(M, D_FF), d_o2m, 0)], ta=True))
        d_a, d_up, dw9, dcb = tile_bwd(
            f'ffnconv_bwd{l}', f_ffnconv, (D_FF // 128, B),
            [(r['a'], (None, Tt, 128), lambda cb, bi: (bi, 0, cb)), (r['up'], (None, Tt, 128), lambda cb, bi: (bi, 0, cb)),
             (r['w9'], (9, 128), lambda cb, bi: (0, cb)), (r['cbias'], (1, 128), lambda cb, bi: (0, cb)), vec(conv_mk)],
            [[(d_act, (None, Tt, 128), lambda cb, bi: (bi, 0, cb))]],
            [(0, (B, Tt, D_FF), BF16, (None, Tt, 128), lambda cb, bi: (bi, 0, cb), None),
             (1, (B, Tt, D_FF), BF16, (None, Tt, 128), lambda cb, bi: (bi, 0, cb), None),
             (2, (9, D_FF), F32, (9, 128), lambda cb, bi: (0, cb), lambda cb, bi: bi == 0),
             (3, (1, D_FF), F32, (1, 128), lambda cb, bi: (0, cb), lambda cb, bi: bi == 0)])
        add_grad('ffn_conv_w', l, dw9.reshape(3, 3, D_FF))
        add_grad('ffn_conv_b', l, dcb[0])
        d_am, d_upm = d_a.reshape(M, D_FF), d_up.reshape(M, D_FF)
        h2m = r['h2'].reshape(M, D)
        d_h2 = mm(f'ffn_in_dx{l}', [(d_am, W['ffn_w_gate'][l], 0), (d_upm, W['ffn_w_up'][l], 0)], tb=True).reshape(B, Tt, D)
        add_grad('ffn_w_gate', l, mm(f'ffn_gate_dw{l}', [(h2m, d_am, 0)], ta=True))
        add_grad('ffn_w_up', l, mm(f'ffn_up_dw{l}', [(h2m, d_upm, 0)], ta=True))
        fg = W['norm_ffn_g'][l][None]
        d_xa, d_o1, dgate, dfg, dsh, dsc = tile_bwd(
            f'nm_ffn_bwd{l}', f_nm, (B, nt), [tok(r['xa']), tok(r['o1']), modv(l, 2), vec(fg), modv(l, 3), modv(l, 4)],
            [[tok(d_xp)], [tok(d_h2)]],
            [(0,) + tok_out(D) + (None,), (1,) + tok_out(D, BF16) + (None,), (2,) + dmod_spec, (3,) + vec_acc((1, D)),
             (4,) + dmod_spec, (5,) + dmod_spec])
        add_grad('norm_ffn_g', l, dfg[0])
        dmods[(l, 2)], dmods[(l, 3)], dmods[(l, 4)] = dgate, dsh, dsc
        d_o1m = d_o1.reshape(M, D)
        h1m = r['h1'].reshape(M, D)
        proj = r['proj']
        if l % 2 == 0:
            wout, win = W['ev_w_out'][j], W['ev_w_in_p'][j]
            d_mix = mm(f'mix_out_dx{l}', [(d_o1m, wout, 0)], tb=True).reshape(B, Tt, 2 * D)
            add_grad('ev_w_out', j, mm(f'mix_out_dw{l}', [(r['mix'].reshape(M, 2 * D), d_o1m, 0)], ta=True))
            d_y, d_xs_fin, d_z, d_gy, d_h, ddpad, dsng = tile_bwd(
                f'ssdfin_bwd{l}', f_ssdfin, (B, nt),
                [tok(r['y'][0]), tok(r['y'][1]), tok(r['xs']), tok(proj, D, 0), tok(proj, D, 1), tok(r['h'][0]), tok(r['h'][1]),
                 vec(r['dpad']), vec(r['sng'])],
                [[tok(d_mix)]],
                [(0,) + tok_out(D) + (None,), (2,) + tok_out(D) + (None,), (3,) + tok_out(D) + (None,), (4,) + tok_out(D) + (None,),
                 (5,) + tok_out(D) + (None,), (7,) + vec_acc((8, 128)), (8,) + vec_acc((1, D))])
            add_grad('ssd_d', j, ddpad[0, :SSD_HEADS])
            add_grad('ssd_norm_g', j, dsng[0])
            d_xs_parts, d_bc_parts, d_dt_parts, d_u_parts = [d_xs_fin], [], [], []
            dbias_t, dalog_t = [], []
            dh4 = d_h
            for d in range(2):
                ph0 = phys_chunk(nc, ncc, d == 1)

                def ph(s, ph0=ph0):
                    return ph0(nc - 1 - s)
                dxs_d, dbc_d, ddt_d, dbias, dalog = scan_bwd(
                    f'ssd_bwd{l}_{d}', functools.partial(f_ssd, d=d, reverse=(d == 1)), (B, nc),
                    [(r['xs'], (None, SSD_CHUNK, D), lambda b, s, ph=ph: (b, ph(s), 0)),
                     (r['bc'], (None, SSD_CHUNK, 512), lambda b, s, ph=ph: (b, ph(s), 0)),
                     (proj, (None, SSD_CHUNK, 128), lambda b, s, ph=ph: (b, ph(s), 36)),
                     vec(r['bias']), vec(r['alog'])],
                    (r['st'][d], (None, None, 8, 128, 128), lambda b, s: (b, nc - 1 - s, 0, 0, 0)),
                    (d_y, (None, SSD_CHUNK, D), lambda b, s, ph=ph: (b, ph(s), 0)),
                    [(0, (B, Tt, D), F32, (None, SSD_CHUNK, D), lambda b, s, ph=ph: (b, ph(s), 0), None),
                     (1, (B, Tt, 512), F32, (None, SSD_CHUNK, 512), lambda b, s, ph=ph: (b, ph(s), 0), None),
                     (2, (B, Tt, 128), F32, (None, SSD_CHUNK, 128), lambda b, s, ph=ph: (b, ph(s), 0), None),
                     (3,) + vec_acc((1, 128)), (4,) + vec_acc((1, 128))],
                    (8, 128, 128), lambda b, s: s == 0)
                d_xs_parts.append(dxs_d)
                d_bc_parts.append(dbc_d)
                d_dt_parts.append(ddt_d)
                dbias_t.append(dbias)
                dalog_t.append(dalog)
                g4, ga4 = linrec_bwd(f'lru_bwd{l}_{d}', r['a4'][d], dh4, r['hp4'][d], 'Fb' if d == 0 else 'Rb')
                du_g, dwa, dba, dwi, dbi, dlam = tile_bwd(
                    f'gates_bwd{l}_{d}', f_gates, (B, nt),
                    [tok(r['u']), vec(W['lru_w_a'][j, d]), vec(W['lru_b_a'][j, d][None]), vec(W['lru_w_i'][j, d]),
                     vec(W['lru_b_i'][j, d][None]), vec(W['lru_lam'][j, d][None])],
                    [[tok(ga4)], [tok(g4)]],
                    [(0,) + tok_out(D) + (None,), (1,) + vec_acc((8, 128, 128)), (2,) + vec_acc((1, D)), (3,) + vec_acc((8, 128, 128)),
                     (4,) + vec_acc((1, D)), (5,) + vec_acc((1, D))])
                d_u_parts.append(du_g)
                add_grad('lru_w_a', (j, d), dwa)
                add_grad('lru_b_a', (j, d), dba[0])
                add_grad('lru_w_i', (j, d), dwi)
                add_grad('lru_b_i', (j, d), dbi[0])
                add_grad('lru_lam', (j, d), dlam[0])
            add_grad('ssd_dt_bias', j, (dbias_t[0] + dbias_t[1])[0, :32].reshape(2, SSD_HEADS))
            add_grad('ssd_a_log', j, (dalog_t[0] + dalog_t[1])[0, :32].reshape(2, SSD_HEADS))
            scw, scb = W['ssd_conv_w'][j], W['ssd_conv_b'][j][None]
            lcw, lcb = W['lru_conv_w'][j], W['lru_conv_b'][j][None]

            def conv_bwd(name, colblk0, w, b, wblk0, ncols, act, parts):
                return tile_bwd(
                    name, functools.partial(f_conv1d, lc=Lc, act=act), (ncols // 256, B),
                    [(proj, (None, Tt, 256), lambda cb, bi: (bi, 0, colblk0 + cb)),
                     (w, (4, 256), lambda cb, bi: (0, wblk0 + cb)), (b, (1, 256), lambda cb, bi: (0, wblk0 + cb))],
                    [[(p, (None, Tt, 256), lambda cb, bi: (bi, 0, cb)) for p in parts]],
                    [(0, (B, Tt, ncols), F32, (None, Tt, 256), lambda cb, bi: (bi, 0, cb), None),
                     (1, (4, ncols), F32, (4, 256), lambda cb, bi: (0, cb), lambda cb, bi: bi == 0),
                     (2, (1, ncols), F32, (1, 256), lambda cb, bi: (0, cb), lambda cb, bi: bi == 0)])
            d_xs_raw, dw_xs, db_xs = conv_bwd(f'conv_xs_bwd{l}', 12, scw, scb, 0, 1024, True, d_xs_parts)
            d_bc_raw, dw_bc, db_bc = conv_bwd(f'conv_bc_bwd{l}', 16, scw, scb, 4, 512, True, d_bc_parts)
            d_u_raw, dw_u, db_u = conv_bwd(f'conv_u_bwd{l}', 8, lcw, lcb, 0, 1024, False, d_u_parts)
            add_grad('ssd_conv_w', j, jnp.concatenate([dw_xs, dw_bc], axis=1))
            add_grad('ssd_conv_b', j, jnp.concatenate([db_xs, db_bc], axis=1)[0])
            add_grad('lru_conv_w', j, dw_u)
            add_grad('lru_conv_b', j, db_u[0])
            def f_ev_dproj(z_, gy_, u_, xs_, bc_, t0, t1):
                pad = jnp.zeros((z_.shape[0], EV_PAD - 4736), F32)
                return jnp.concatenate([z_, gy_, u_, xs_, bc_, t0 + t1, pad], axis=1)
            dproj = tile_fwd(f'ev_dproj{l}', f_ev_dproj, (B, nt),
                             [tok(d_z), tok(d_gy), tok(d_u_raw), tok(d_xs_raw), tok(d_bc_raw), tok(d_dt_parts[0]), tok(d_dt_parts[1])],
                             [tok_out(EV_PAD, BF16)])[0].reshape(M, EV_PAD)
            d_h1 = mm(f'ev_proj_dx{l}', [(dproj, win, 0)], tb=True).reshape(B, Tt, D)
            dwp = mm(f'ev_proj_dw{l}', [(h1m, dproj, 0)], ta=True)
            add_grad('ev_w_in', j, jnp.concatenate([dwp[:, 0:1024], dwp[:, 3072:4640], dwp[:, 1024:3072]], axis=1))
        else:
            wout, win = W['od_w_out'][j], W['od_w_in'][j]
            d_mix = mm(f'mix_out_dx{l}', [(d_o1m, wout, 0)], tb=True).reshape(B, Tt, D)
            add_grad('od_w_out', j, mm(f'mix_out_dw{l}', [(r['mix'].reshape(M, D), d_o1m, 0)], ta=True))
            hn, sd, gw, gb = r['fin_par']
            d_o, d_g, d_yv, d_u_fin, dhn, dsd, dgw, dgb = tile_bwd(
                f'oddfin_bwd{l}', f_oddfin, (B, nt),
                [tok(r['o'][0]), tok(r['o'][1]), tok(proj, HG_W, 4), tok(r['yd'][0]), tok(r['yd'][1]), tok(proj, S5_W, 15),
                 vec(hn), vec(sd), vec(gw), vec(gb)],
                [[tok(d_mix)]],
                [(0,) + tok_out(HG_W) + (None,), (2,) + tok_out(HG_W) + (None,), (3,) + tok_out(S5_W) + (None,),
                 (5,) + tok_out(S5_W) + (None,), (6,) + vec_acc((8, 128)), (7,) + vec_acc((1, S5_W)), (8,) + vec_acc((S5_W, S5_W)),
                 (9,) + vec_acc((1, S5_W))])
            add_grad('hg_norm_g', j, dhn[:HG_HEADS])
            add_grad('s5_d', j, dsd[0])
            add_grad('s5_glu_w', j, dgw)
            add_grad('s5_glu_b', j, dgb[0])
            lbv = lb1 if l == 1 else lb3
            ns = nc
            dq, df, dv, du_s5 = [], [], [], []
            d_ym = d_yv.reshape(M, S5_W)
            dbt_re, dbt_im = [], []
            for d in range(2):
                ph0 = phys_chunk(ns, ncc, d == 1)

                def ph(s, ph0=ph0):
                    return ph0(ns - 1 - s)
                dq_d, df_d, dv_d, dlb_d = scan_bwd(
                    f'hgrn_bwd{l}_{d}', functools.partial(f_hgrn_group, reverse=(d == 1), chunk=HG_BWD[0]), (bgr, B, ns),
                    [(proj, (None, 128, bgw), lambda h, b, s, ph=ph: (b, ph(s), h)),
                     (proj, (None, 128, bgw), lambda h, b, s, ph=ph, d=d: (b, ph(s), (1 + d) * bgr + h)),
                     (proj, (None, 128, bgw), lambda h, b, s, ph=ph: (b, ph(s), 3 * bgr + h)),
                     (lbv, (1, bgw), lambda h, b, s: (0, h))],
                    (r['zst'][d].reshape(B, ns, bgr, bgw, 128), (None, None, None, bgw, 128), lambda h, b, s: (b, ns - 1 - s, h, 0, 0)),
                    (d_o, (None, 128, bgw), lambda h, b, s, ph=ph: (b, ph(s), h)),
                    [(0, (B, Tt, HG_W), F32, (None, 128, bgw), lambda h, b, s, ph=ph: (b, ph(s), h), None),
                     (1, (B, Tt, HG_W), F32, (None, 128, bgw), lambda h, b, s, ph=ph: (b, ph(s), h), None),
                     (2, (B, Tt, HG_W), F32, (None, 128, bgw), lambda h, b, s, ph=ph: (b, ph(s), h), None),
                     (3, (1, HG_W), F32, (1, bgw), lambda h, b, s: (0, h), lambda h, b, s: (b == 0) & (s == 0))],
                    (bgw, 128), lambda h, b, s: s == 0)
                dq.append(dq_d)
                df.append(df_d)
                dv.append(dv_d)
                dlb[l].append(dlb_d)
                d_hcat = mm(f's5_out_dx{l}_{d}', [(d_ym, r['ccat'][d], 0)], tb=True)
                dccat = mm(f's5_out_dw{l}_{d}', [(r['hcat'][d], d_ym, 0)], ta=True)
                add_grad('s5_c_re', (j, d), _unblockdiag_t(dccat[:D]))
                add_grad('s5_c_im', (j, d), -_unblockdiag_t(dccat[D:]))
                g5, dcoef8 = clinrec(f's5_bwd{l}_{d}', d_hcat.reshape(B, Tt, 2 * D), r['coef'][d], 'Fb' if d == 0 else 'Rb',
                                     Lc, conj=True, hprev=r['hp5'][d])
                dcoef = jnp.sum(dcoef8, axis=2)
                gcat = g5.reshape(M, 2 * D)
                dbcat = mm(f's5_in_dw{l}_{d}', [(r['u2'], gcat, 0)], ta=True)
                du_s5.append(mm(f's5_in_dx{l}_{d}', [(gcat, r['bcat'][d], 0)], tb=True))
                cts5 = [dcoef[0].reshape(16, 64), dcoef[1].reshape(16, 64), _unblockdiag(dbcat[:, :D]), _unblockdiag(dbcat[:, D:])]
                dlre, dlim, dlst, dbtr, dbti = tile_bwd(
                    f's5p_bwd{l}_{d}', f_s5p, (1,), [vec(t) for t in r['s5in'][d]], [[vec(t)] for t in cts5],
                    [(i, t.shape, F32, t.shape, (lambda *ids, n=t.ndim: (0,) * n), None) for i, t in enumerate(r['s5in'][d])])
                add_grad('s5_lam_re', (j, d), dlre)
                add_grad('s5_lam_im', (j, d), dlim)
                add_grad('s5_log_step', (j, d), dlst[:, 0])
                dbt_re.append(dbtr)
                dbt_im.append(dbti)
            add_grad('s5_b_re', j, jnp.swapaxes(dbt_re[0] + dbt_re[1], 1, 2))
            add_grad('s5_b_im', j, jnp.swapaxes(dbt_im[0] + dbt_im[1], 1, 2))
            def f_od_dproj(q0, q1, f0, f1, v0, v1, g_, u0, u1, u2):
                return jnp.concatenate([q0 + q1, f0, f1, v0 + v1, g_, u0 + u1 + u2], axis=1)
            parts = [dq[0], dq[1], df[0], df[1], dv[0], dv[1], d_g, d_u_fin, du_s5[0].reshape(B, Tt, S5_W),
                     du_s5[1].reshape(B, Tt, S5_W)]
            dproj = tile_fwd(f'od_dproj{l}', f_od_dproj, (B, nt), [tok(t) for t in parts],
                             [tok_out(4096, BF16)])[0].reshape(M, 4096)
            d_h1 = mm(f'od_proj_dx{l}', [(dproj, win, 0)], tb=True).reshape(B, Tt, D)
            add_grad('od_w_in', j, mm(f'od_proj_dw{l}', [(h1m, dproj, 0)], ta=True))
        ng = W['norm_mix_g'][l][None]
        if l == 0:
            d_x0, dng, dsh, dsc = tile_bwd(
                'nm0_bwd', lambda xv, g, sh, sc: (xv, f_nm0(xv, g, sh, sc)), (B, nt),
                [tok(r['xin']), vec(ng), modv(l, 0), modv(l, 1)], [[tok(d_xa)], [tok(d_h1)]],
                [(0,) + tok_out(D) + (None,), (1,) + vec_acc((1, D)), (2,) + dmod_spec, (3,) + dmod_spec])
        else:
            d_xp, d_o2, dgate, dng, dsh, dsc = tile_bwd(
                f'nm_mix_bwd{l}', f_nm, (B, nt),
                [tok(r['xin']), tok(r['oin']), modv(l - 1, 5), vec(ng), modv(l, 0), modv(l, 1)],
                [[tok(d_xa)], [tok(d_h1)]],
                [(0,) + tok_out(D) + (None,), (1,) + tok_out(D, BF16) + (None,), (2,) + dmod_spec, (3,) + vec_acc((1, D)),
                 (4,) + dmod_spec, (5,) + dmod_spec])
            dmods[(l - 1, 5)] = dgate
        add_grad('norm_mix_g', l, dng[0])
        dmods[(l, 0)], dmods[(l, 1)] = dsh, dsc

    grad_x = d_x0[:, Lc:, :]

    (dlogits,) = tile_bwd('lb_bwd', f_lb, (1,), [vec(W['hg_lb_logits'])],
                          [[vec(t) for t in dlb[1]], [vec(t) for t in dlb[3]]],
                          [(0, (DEPTH, HG_W), F32, (DEPTH, HG_W), lambda i: (0, 0), None)])
    add_grad('hg_lb_logits', None, dlogits)

    dm = jnp.stack([jnp.stack([dmods[(l, w)] for w in range(N_MOD)]) for l in range(DEPTH)])
    dlat = jnp.transpose(dm[:, :, :, 1, 0, :], (0, 2, 1, 3)).reshape(DEPTH, B, N_MOD * D)
    dctx = jnp.transpose(dm[:, :, :, 0, 0, :], (0, 2, 1, 3)).reshape(DEPTH, B, N_MOD * D)
    dlat = jnp.zeros((DEPTH, 8, N_MOD * D), F32).at[:, :B].set(dlat)
    dctx = jnp.zeros((DEPTH, 8, N_MOD * D), F32).at[:, :B].set(dctx)

    def f_dmod(dl, dc):
        row = lax.broadcasted_iota(jnp.int32, (8, 1), 0)
        dall = dl + jnp.where(row == 4, jnp.sum(dc, axis=0, keepdims=True), 0.0)
        return dall, jnp.sum(dall, axis=0, keepdims=True)

    dmod, dbmod = tile_fwd('mod_bwd', f_dmod, (DEPTH, nmc),
                           [(dlat, (None, 8, 1536), lambda l, n: (l, 0, n)), (dctx, (None, 8, 1536), lambda l, n: (l, 0, n))],
                           [((DEPTH, 8, N_MOD * D), F32, (None, 8, 1536), lambda l, n: (l, 0, n)),
                            ((DEPTH, 1, N_MOD * D), F32, (None, 1, 1536), lambda l, n: (l, 0, n))])
    add_grad('b_mod', None, dbmod[:, 0])
    return loss, grad_x, G, dmod


def mod_weight_grad(name, cc_rows, dmod_rows):
    nl, nr, cols = dmod_rows.shape

    def f(ccv, dv):
        return lax.dot_general(jax.nn.silu(ccv).astype(BF16), dv.astype(BF16), (((0,), (0,)), ((), ())),
                               preferred_element_type=F32)

    return tile_fwd(name, f, (nl,), [(cc_rows, (nr, D), lambda l: (0, 0)), (dmod_rows, (None, nr, cols), lambda l: (l, 0, 0))],
                    [((nl, D, cols), F32, (None, D, cols), lambda l: (l, 0, 0))])[0]


def assemble_grads(G, like):
    out = {}
    for name, parts in G.items():
        shape = like[name].shape
        if None in parts:
            g = parts[None]
        elif isinstance(next(iter(parts)), tuple):
            g = jnp.stack([jnp.stack([parts[(j, d)] for d in range(2)]) for j in range(shape[0])])
        else:
            g = jnp.stack([parts[i] for i in range(shape[0])])
        out[name] = g.reshape(shape)
    return out


XY_RELS = ((1, 0, 0), (0, 1, 0), (1, 1, 0))
ALL_RELS = tuple((dx, dy, dc) for dx in (0, 1) for dy in (0, 1) for dc in (0, 1))[1:]


def exchange(name, src, out_shape, sends):
    return exchange_many(name, [(src, out_shape, sends)])[0]


def exchange_many(name, items):
    na = len(items)
    n = sum(len(sends) for _, _, sends in items)

    def body(*refs):
        src_refs, out_refs, send_sems, recv_sems = refs[:na], refs[na:2 * na], refs[2 * na], refs[2 * na + 1]
        me = (lax.axis_index("x"), lax.axis_index("y"), lax.axis_index("c"))
        copies, k = [], 0
        for (_, _, sends), src_ref, out_ref in zip(items, src_refs, out_refs):
            for rel, ssel, dsel in sends:
                tgt = tuple(1 - m if f else m for m, f in zip(me, rel))
                cp = pltpu.make_async_remote_copy(
                    src_ref=src_ref if ssel is None else src_ref.at[ssel(me, tgt)],
                    dst_ref=out_ref if dsel is None else out_ref.at[dsel(me, tgt)],
                    send_sem=send_sems.at[k], recv_sem=recv_sems.at[k], device_id=tgt, device_id_type=MESH)
                cp.start()
                copies.append(cp)
                k += 1
        for cp in copies:
            cp.wait()

    return pl.pallas_call(
        body, out_shape=[jax.ShapeDtypeStruct(shape, src.dtype) for src, shape, _ in items],
        in_specs=[pl.BlockSpec(memory_space=pl.ANY)] * na, out_specs=[pl.BlockSpec(memory_space=pl.ANY)] * na,
        scratch_shapes=[pltpu.SemaphoreType.DMA((n,)), pltpu.SemaphoreType.DMA((n,))],
        name=name,
    )(*[src for src, _, _ in items])


def _xy_index(dev):
    return 2 * dev[0] + dev[1]


def _my_xy():
    return 2 * lax.axis_index("x") + lax.axis_index("y")


def all_gather_xy(name, shard):
    got = exchange(name, shard, (4,) + shard.shape, [(rel, None, lambda me, tgt: _xy_index(me)) for rel in XY_RELS])
    return lax.dynamic_update_index_in_dim(got, shard, _my_xy(), 0)


def reduce_scatter_xy(name, g4):
    got = exchange(name, g4, (3,) + g4.shape[1:],
                   [(rel, (lambda me, tgt: _xy_index(tgt)), (lambda me, tgt, k=k: k)) for k, rel in enumerate(XY_RELS)])
    return got, lax.dynamic_index_in_dim(g4, _my_xy(), 0, keepdims=False)


def sibling_swap(name, v):
    return exchange(name, v, v.shape, [((0, 0, 1), None, None)])


def all_gather_all(name, v):
    got = exchange(name, v, (8,) + v.shape, [(rel, None, lambda me, tgt: 4 * me[0] + 2 * me[1] + me[2]) for rel in ALL_RELS])
    return lax.dynamic_update_index_in_dim(got, v, 2 * _my_xy() + lax.axis_index("c"), 0)


def all_gather_xy_halves(name, shards):
    na = len(shards)

    def body(*refs):
        src_refs, out_refs, send_sems, recv_sems = refs[:na], refs[na:2 * na], refs[2 * na], refs[2 * na + 1]
        x, y, c = lax.axis_index("x"), lax.axis_index("y"), lax.axis_index("c")
        peers = [(1 - x, y), (x, 1 - y), (1 - x, 1 - y)]

        def copy(k, src, dst, to):
            return pltpu.make_async_remote_copy(src_ref=src, dst_ref=dst, send_sem=send_sems.at[k], recv_sem=recv_sems.at[k],
                                                device_id=to, device_id_type=MESH)

        halves = [pl.ds(c * (s.shape[0] // 2), s.shape[0] // 2) for s in shards]
        first = [[copy(6 * i + k, src_refs[i].at[halves[i]], out_refs[i].at[2 * x + y, halves[i]], (px, py, c))
                  for k, (px, py) in enumerate(peers)] for i in range(na)]
        for row in first:
            for cp in row:
                cp.start()
        passed = []
        for i in range(na):
            for k, (px, py) in enumerate(peers):
                first[i][k].wait_recv()
                landed = out_refs[i].at[2 * px + py, halves[i]]
                fw = copy(6 * i + 3 + k, landed, landed, (x, y, 1 - c))
                fw.start()
                passed.append(fw)
        for fw in passed:
            fw.wait_recv()
        for cp in [cp for row in first for cp in row] + passed:
            cp.wait_send()

    got = pl.pallas_call(
        body, out_shape=[jax.ShapeDtypeStruct((4,) + s.shape, s.dtype) for s in shards],
        in_specs=[pl.BlockSpec(memory_space=pl.ANY)] * na, out_specs=[pl.BlockSpec(memory_space=pl.ANY)] * na,
        scratch_shapes=[pltpu.SemaphoreType.DMA((6 * na,)), pltpu.SemaphoreType.DMA((6 * na,))],
        name=name,
    )(*shards)
    return [lax.dynamic_update_index_in_dim(g, s, _my_xy(), 0) for g, s in zip(got, shards)]


def reduce_scatter_xy_many(name, g4s):
    got = exchange_many(name, [(g4, (3,) + g4.shape[1:],
                                [(rel, (lambda me, tgt: _xy_index(tgt)), (lambda me, tgt, k=k: k)) for k, rel in enumerate(XY_RELS)])
                               for g4 in g4s])
    return [(g, lax.dynamic_index_in_dim(g4, _my_xy(), 0, keepdims=False)) for g, g4 in zip(got, g4s)]


def sibling_split(name, g4s):
    halves = [g4.shape[1] // 2 for g4 in g4s]
    got = exchange_many(name, [(g4, (4, h) + g4.shape[2:], [((0, 0, 1), (lambda me, tgt, h=h: (slice(None), pl.ds(tgt[2] * h, h))), None)])
                               for g4, h in zip(g4s, halves)])
    return [(g, lax.dynamic_slice_in_dim(g4, lax.axis_index("c") * h, h, axis=1)) for g, g4, h in zip(got, g4s, halves)]


def sibling_join(name, qs):
    got = exchange_many(name, [(q, (2 * q.shape[0],) + q.shape[1:], [((0, 0, 1), None, lambda me, tgt, h=q.shape[0]: pl.ds(me[2] * h, h))])
                               for q in qs])
    return [lax.dynamic_update_slice_in_dim(g, q, lax.axis_index("c") * q.shape[0], axis=0) for g, q in zip(got, qs)]


def _rows_view(shape):
    cols = shape[-1] if len(shape) else 1
    rows = 1
    for s in shape[:-1]:
        rows *= s
    return rows, cols


def _row_block(rows, cols, n_arrays):
    budget = (24 * 1024 * 1024) // (8 * n_arrays * cols)
    if rows <= max(budget, 16):
        return rows
    br = (min(budget, rows) // 16) * 16
    while br > 16 and rows % br:
        br -= 16
    return br if rows % br == 0 else rows


def sum_slots(name, stacked, extra=(), out_dtype=F32):
    k = stacked.shape[0]
    rows, cols = _rows_view(stacked.shape[1:])
    br = _row_block(rows, cols, k + len(extra) + 1)

    def f(s, *more):
        parts = [s[i].astype(F32) for i in range(k)] + [m.astype(F32) for m in more]
        while len(parts) > 1:
            parts = [parts[i] + parts[i + 1] for i in range(0, len(parts) - 1, 2)] + ([parts[-1]] if len(parts) % 2 else [])
        return parts[0]

    out = tile_fwd(name, f, (rows // br,),
                   [(stacked.reshape(k, rows, cols), (k, br, cols), lambda i: (0, i, 0))]
                   + [(e.reshape(rows, cols), (br, cols), lambda i: (i, 0)) for e in extra],
                   [((rows, cols), out_dtype, (br, cols), lambda i: (i, 0))])[0]
    return out.reshape(stacked.shape[1:])


def adamw(name, w, m, v, gs):
    rows, cols = _rows_view(w.shape)
    br = _row_block(rows, cols, 7 + len(gs))
    spec = lambda a: (a.reshape(rows, cols), (br, cols), lambda i: (i, 0))
    outs = tile_fwd(name, f_adamw, (rows // br,), [spec(t) for t in (w, m, v) + tuple(gs)],
                    [((rows, cols), F32, (br, cols), lambda i: (i, 0))] * 4)
    return [o.reshape(w.shape) for o in outs]


IN_NAMES = ['x', 'c', 'ctx'] + W_NAMES + ['loss_target'] + ['m_' + n for n in W_NAMES] + ['v_' + n for n in W_NAMES]
SMALL_PAD = 128 * 1024


def kernel(x, c, ctx, c_ctx, w_mod, b_mod, norm_mix_g, norm_ffn_g, final_norm_g, ev_w_in, ev_w_out, ssd_conv_w, ssd_conv_b, ssd_dt_bias, ssd_a_log, ssd_d, ssd_norm_g, lru_conv_w, lru_conv_b, lru_w_a, lru_b_a, lru_w_i, lru_b_i, lru_lam, od_w_in, od_w_out, hg_lb_logits, hg_norm_g, s5_lam_re, s5_lam_im, s5_log_step, s5_b_re, s5_b_im, s5_c_re, s5_c_im, s5_d, s5_glu_w, s5_glu_b, ffn_w_gate, ffn_w_up, ffn_conv_w, ffn_conv_b, ffn_w_down, loss_target, m_c_ctx, m_w_mod, m_b_mod, m_norm_mix_g, m_norm_ffn_g, m_final_norm_g, m_ev_w_in, m_ev_w_out, m_ssd_conv_w, m_ssd_conv_b, m_ssd_dt_bias, m_ssd_a_log, m_ssd_d, m_ssd_norm_g, m_lru_conv_w, m_lru_conv_b, m_lru_w_a, m_lru_b_a, m_lru_w_i, m_lru_b_i, m_lru_lam, m_od_w_in, m_od_w_out, m_hg_lb_logits, m_hg_norm_g, m_s5_lam_re, m_s5_lam_im, m_s5_log_step, m_s5_b_re, m_s5_b_im, m_s5_c_re, m_s5_c_im, m_s5_d, m_s5_glu_w, m_s5_glu_b, m_ffn_w_gate, m_ffn_w_up, m_ffn_conv_w, m_ffn_conv_b, m_ffn_w_down, v_c_ctx, v_w_mod, v_b_mod, v_norm_mix_g, v_norm_ffn_g, v_final_norm_g, v_ev_w_in, v_ev_w_out, v_ssd_conv_w, v_ssd_conv_b, v_ssd_dt_bias, v_ssd_a_log, v_ssd_d, v_ssd_norm_g, v_lru_conv_w, v_lru_conv_b, v_lru_w_a, v_lru_b_a, v_lru_w_i, v_lru_b_i, v_lru_lam, v_od_w_in, v_od_w_out, v_hg_lb_logits, v_hg_norm_g, v_s5_lam_re, v_s5_lam_im, v_s5_log_step, v_s5_b_re, v_s5_b_im, v_s5_c_re, v_s5_c_im, v_s5_d, v_s5_glu_w, v_s5_glu_b, v_ffn_w_gate, v_ffn_w_up, v_ffn_conv_w, v_ffn_conv_b, v_ffn_w_down):
    a = dict(locals())
    big = [n for n in W_NAMES if n in MATMUL_WEIGHTS and n != 'w_mod']
    minor = [n for n in W_NAMES if n in SHARD_AXIS and n not in MATMUL_WEIGHTS]

    def pack(arrays, lead=()):
        flat = jnp.concatenate([t.reshape(lead + (-1,)) for t in arrays], axis=len(lead))
        pad = -flat.shape[-1] % 1024
        flat = jnp.concatenate([flat, jnp.zeros(lead + (pad,), flat.dtype)], axis=len(lead))
        return flat.reshape(lead + (-1, 128))

    def unpack(packed, names, lead=()):
        flat, out, off = packed.reshape(lead + (-1,)), {}, 0
        for n in names:
            size = math.prod(a[n].shape)
            out[n] = flat[..., off:off + size].reshape(lead + a[n].shape)
            off += size
        return out

    gathered = dict(zip(big, all_gather_xy_halves('ag_big', [a[n].astype(BF16) for n in big])))
    gathered.update(unpack(all_gather_xy('ag_minor', pack([a[n] for n in minor])), minor, (4,)))
    W = {}
    for n in W_NAMES:
        w = a[n]
        if n in gathered:
            ax = SHARD_AXIS[n]
            shape = list(w.shape)
            shape[ax] *= 4
            W[n] = jnp.moveaxis(gathered[n], 0, ax).reshape(shape)
        elif n not in SHARD_AXIS:
            W[n] = w
    e = W['ev_w_in']
    W['ev_w_in_p'] = jnp.concatenate(
        [e[:, :, 0:1024], e[:, :, 2592:3616], e[:, :, 3616:4640], e[:, :, 1024:2560], e[:, :, 2560:2592],
         jnp.zeros((e.shape[0], D, EV_PAD - 4640), e.dtype)], axis=2)

    my_dev = 2 * _my_xy() + lax.axis_index("c")
    mcols = a['w_mod'].shape[2]
    w_mod_mine = a['w_mod'].astype(BF16)
    cc_all = all_gather_all('ag_c', cond_rows(a['c'], a['c_ctx'])).reshape(8 * 8, D)
    mod_cols = mod_table('mod_fwd', cc_all, w_mod_mine, lax.dynamic_slice_in_dim(a['b_mod'], _my_xy() * mcols, mcols, axis=1))
    rows_of = lambda dev: pl.ds((4 * dev[0] + 2 * dev[1] + dev[2]) * 8, 8)
    got = exchange('ag_mod', mod_cols, (4, DEPTH, 8, mcols),
                   [(rel, (lambda me, tgt: (slice(None), rows_of(tgt))), (lambda me, tgt: _xy_index(me))) for rel in XY_RELS])
    got = lax.dynamic_update_index_in_dim(got, lax.dynamic_slice_in_dim(mod_cols, my_dev * 8, 8, axis=1), _my_xy(), 0)
    mod = jnp.moveaxis(got, 0, 2).reshape(DEPTH, 8, N_MOD * D)

    loss_local, grad_x, G, dmod = local_step(a['x'], a['ctx'], a['loss_target'], mod, W)
    grads = assemble_grads(G, W)

    res = {}
    cols_of = lambda dev: pl.ds(_xy_index(dev) * mcols, mcols)
    got = exchange('ag_dmod', dmod, (8, DEPTH, 8, mcols),
                   [(rel, (lambda me, tgt: (slice(None), slice(None), cols_of(tgt))), (lambda me, tgt: 4 * me[0] + 2 * me[1] + me[2]))
                    for rel in ALL_RELS])
    got = lax.dynamic_update_index_in_dim(got, lax.dynamic_slice_in_dim(dmod, _my_xy() * mcols, mcols, axis=2), my_dev, 0)
    dm_mine = jnp.moveaxis(got, 0, 1).reshape(DEPTH, 8 * 8, mcols)
    res['w_mod'] = adamw('adamw_w_mod', a['w_mod'], a['m_w_mod'], a['v_w_mod'], (mod_weight_grad('mod_dw', cc_all, dm_mine),))
    dcc = mod_cond_grad('mod_dcc', cc_all, dm_mine, w_mod_mine).reshape(8, 8, D)
    grads['c_ctx'] = 0.5 * jnp.sum(dcc[:, 4], axis=0)
    g4 = {}
    for n in big + minor:
        ax = SHARD_AXIS[n]
        gf = grads[n]
        g4[n] = jnp.moveaxis(gf.reshape(gf.shape[:ax] + (4, a[n].shape[ax]) + gf.shape[ax + 1:]), ax, 0)
    parts = [sum_slots('csum_' + n, theirs[None], (ours,), BF16)
             for n, (theirs, ours) in zip(big, sibling_split('rsc_big', [g4[n] for n in big]))]
    halves = [sum_slots('gsum_' + n, got, (own,)) for n, (got, own) in zip(big, reduce_scatter_xy_many('rs_big', parts))]
    for n, g in zip(big, sibling_join('agc_big', halves)):
        res[n] = adamw('adamw_' + n, a[n], a['m_' + n], a['v_' + n], (g,))
    got, own = reduce_scatter_xy('rs_minor', pack([g4[n] for n in minor], (4,)))
    mine = sum_slots('gsum_minor', got, (own,))
    mine_n, other_n = unpack(mine, minor), unpack(sibling_swap('sw_minor', mine), minor)
    for n in minor:
        res[n] = adamw('adamw_' + n, a[n], a['m_' + n], a['v_' + n], (mine_n[n], other_n[n]))
    small = [n for n in W_NAMES if n not in SHARD_AXIS]
    flat = jnp.concatenate([loss_local.reshape(1)] + [grads[n].reshape(-1) for n in small])
    total = flat.shape[0]
    padded = -(-total // SMALL_PAD) * SMALL_PAD
    flat = jnp.concatenate([flat, jnp.zeros((padded - total,), F32)]).reshape(padded // 128, 128)
    pair = sum_slots('csum_small', flat[None], (sibling_swap('sw_small', flat),))
    summed = sum_slots('gsum_small', all_gather_xy_halves('ag_small', [pair])[0]).reshape(-1)
    loss = summed[0]
    off = 1
    for n in small:
        size = math.prod(a[n].shape)
        g = summed[off:off + size].reshape(a[n].shape)
        off += size
        res[n] = adamw('adamw_' + n, a[n], a['m_' + n], a['v_' + n], (g,))
    outs = [loss, grad_x]
    for k in range(4):
        outs += [res[n][k] for n in W_NAMES]
    return tuple(outs)
```
